```python
import jax, jax.numpy as jnp
from jax import lax
import numpy as np

D_MODEL = 1024
BATCH = 32
SEQ = 256
DEPTH = 1
DEC_BATCH = 2
DEC_SEQ = 1024
PAST_LEN = 256

GRID_W = 64
N_HEADS_ATT = 8
N_KV_HEADS = 2
GQA_GROUP = N_HEADS_ATT // N_KV_HEADS
DH_ATT = 64
D_ATT = N_HEADS_ATT * DH_ATT
ROPE_THETA = 10000.0
Q_BLOCK = 128
N_HEADS_M = 4
DH_M = 128
D_M = N_HEADS_M * DH_M
CHUNK = 128
D_MIX = D_ATT + D_M
N_GATE_COLS = 4 * N_HEADS_M
P_IN = D_ATT + 2 * N_KV_HEADS * DH_ATT + 4 * D_M + N_GATE_COLS
N_EXPERTS = 64
TOP_K = 8
N_GROUPS = 8
TOPK_GROUPS = 4
D_EXPERT = 256
ROUTED_SCALE = 2.5
TOKEN_BLOCK = 128
EPS = 1e-6
NEG_INIT = -1e30

kernel_name = "hybrid_mlstm_gqa_moe_diffusion_step"


def rms_norm(x, w):
    x32 = x.astype(jnp.float32)
    y = x32 * lax.rsqrt(jnp.mean(x32 * x32, axis=-1, keepdims=True) + EPS)
    return (y * w.astype(jnp.float32)).astype(x.dtype)


def axial_rope_tables(n_tokens):
    rows = n_tokens // GRID_W
    row = jnp.repeat(jnp.arange(rows), GRID_W).astype(jnp.float32)
    col = jnp.tile(jnp.arange(GRID_W), rows).astype(jnp.float32)
    n_freq = DH_ATT // 4
    inv_freq = ROPE_THETA ** (-jnp.arange(n_freq, dtype=jnp.float32) / n_freq)
    ang = jnp.stack([row[:, None] * inv_freq, col[:, None] * inv_freq], axis=1)
    return jnp.cos(ang), jnp.sin(ang)


def apply_axial_rope(x, cos, sin):
    shp = x.shape
    xr = x.astype(jnp.float32).reshape(shp[:-1] + (2, 2, DH_ATT // 4))
    x1, x2 = xr[..., 0, :], xr[..., 1, :]
    out = jnp.stack([x1 * cos - x2 * sin, x2 * cos + x1 * sin], axis=-2)
    return out.reshape(shp).astype(x.dtype)


def blocked_attention(q, k, v):
    b, kv, g, lq, d = q.shape
    nb = lq // Q_BLOCK
    qb = jnp.moveaxis(q.reshape(b, kv, g, nb, Q_BLOCK, d), 3, 0)
    scale = d ** -0.5

    def one_block(qblk):
        s = jnp.einsum('bkgqd,bkld->bkgql', qblk, k).astype(jnp.float32) * scale
        p = jax.nn.softmax(s, axis=-1).astype(v.dtype)
        return jnp.einsum('bkgql,bkld->bkgqd', p, v)

    out = lax.map(one_block, qb)
    return jnp.moveaxis(out, 0, 3).reshape(b, kv, g, lq, d)


def mlstm_scan(q, k, v, log_i, log_f, c0, n0, m0):
    b, h, L, d = q.shape
    nc = L // CHUNK

    def chunked(a):
        return jnp.moveaxis(a.reshape((b, h, nc, CHUNK) + a.shape[3:]), 2, 0)

    causal = jnp.tril(jnp.ones((CHUNK, CHUNK), dtype=bool))

    def step(carry, xs):
        c, n, m = carry
        qc, kc, vc, li, lf = xs
        bcum = jnp.cumsum(lf, axis=-1)
        dlog = jnp.where(causal, bcum[..., :, None] - bcum[..., None, :] + li[..., None, :], -jnp.inf)
        g = bcum + m[..., None]
        m_row = jnp.maximum(g, jnp.max(dlog, axis=-1))
        s = jnp.einsum('bhtd,bhsd->bhts', qc, kc) * jnp.exp(dlog - m_row[..., None])
        g_w = jnp.exp(g - m_row)
        num = jnp.einsum('bhts,bhse->bhte', s, vc) + g_w[..., None] * jnp.einsum('bhtd,bhde->bhte', qc, c)
        den = jnp.sum(s, axis=-1) + g_w * jnp.einsum('bhtd,bhd->bht', qc, n)
        h_out = num / jnp.maximum(jnp.abs(den), jnp.exp(-m_row))[..., None]
        b_last = bcum[..., -1:]
        w_log = b_last - bcum + li
        g_end = b_last[..., 0] + m
        m_new = jnp.maximum(g_end, jnp.max(w_log, axis=-1))
        w = jnp.exp(w_log - m_new[..., None])
        decay = jnp.exp(g_end - m_new)
        c_new = decay[..., None, None] * c + jnp.einsum('bhs,bhsd,bhse->bhde', w, kc, vc)
        n_new = decay[..., None] * n + jnp.einsum('bhs,bhsd->bhd', w, kc)
        return (c_new, n_new, m_new), h_out

    (c_f, n_f, m_f), hs = lax.scan(
        step, (c0, n0, m0),
        (chunked(q), chunked(k), chunked(v), chunked(log_i), chunked(log_f)))
    h_seq = jnp.moveaxis(hs, 0, 2).reshape(b, h, L, d)
    return h_seq, c_f, n_f, m_f


def mlstm_bidirectional(q, k, v, i_pre, f_pre, c0, n0, m0):
    log_f = jax.nn.log_sigmoid(f_pre)
    h_fw, c_fw, n_fw, m_fw = mlstm_scan(q, k, v, i_pre[:, 0], log_f[:, 0], c0[:, 0], n0[:, 0], m0[:, 0])
    rev = lambda a: jnp.flip(a, axis=2)
    h_bw, c_bw, n_bw, m_bw = mlstm_scan(rev(q), rev(k), rev(v), rev(i_pre[:, 1]), rev(log_f[:, 1]),
                                        c0[:, 1], n0[:, 1], m0[:, 1])
    h = h_fw + rev(h_bw)
    return (h, jnp.stack([c_fw, c_bw], axis=1), jnp.stack([n_fw, n_bw], axis=1),
            jnp.stack([m_fw, m_bw], axis=1))


def token_mixers(h, rope, ctx, w_in, q_norm_w, k_norm_w, b_gates, m_norm_w, w_out):
    b, L, _ = h.shape
    proj = h @ w_in
    sizes = [D_ATT, N_KV_HEADS * DH_ATT, N_KV_HEADS * DH_ATT, D_M, D_M, D_M, D_M]
    points, acc = [], 0
    for s_ in sizes:
        acc += s_
        points.append(acc)
    q_a, k_a, v_a, q_m, k_m, v_m, o_m, gates = jnp.split(proj, points, axis=-1)
    heads = lambda a, nh: a.reshape(b, L, nh, -1).transpose(0, 2, 1, 3)
    q_a = rms_norm(heads(q_a, N_HEADS_ATT), q_norm_w)
    k_a = rms_norm(heads(k_a, N_KV_HEADS), k_norm_w)
    v_a = heads(v_a, N_KV_HEADS)
    if ctx is None:
        keys, vals = k_a, v_a
        c0 = jnp.zeros((b, 2, N_HEADS_M, DH_M, DH_M), jnp.float32)
        n0 = jnp.zeros((b, 2, N_HEADS_M, DH_M), jnp.float32)
        m0 = jnp.full((b, 2, N_HEADS_M), NEG_INIT, jnp.float32)
    else:
        k_c, v_c, c0, n0, m0 = ctx
        cos, sin = rope
        q_a = apply_axial_rope(q_a, cos, sin)
        k_a = apply_axial_rope(k_a, cos, sin)
        keys = jnp.concatenate([k_a, k_c.astype(k_a.dtype)], axis=2)
        vals = jnp.concatenate([v_a, v_c.astype(v_a.dtype)], axis=2)
    att = blocked_attention(q_a.reshape(b, N_KV_HEADS, GQA_GROUP, L, DH_ATT), keys, vals)
    att = att.reshape(b, N_HEADS_ATT, L, DH_ATT).transpose(0, 2, 1, 3).reshape(b, L, D_ATT)

    g = (gates.astype(jnp.float32).reshape(b, L, 2, 2, N_HEADS_M) + b_gates.astype(jnp.float32))
    g = g.transpose(2, 0, 3, 4, 1)
    qm = heads(q_m, N_HEADS_M).astype(jnp.float32)
    km = heads(k_m, N_HEADS_M).astype(jnp.float32) * (DH_M ** -0.5)
    vm = heads(v_m, N_HEADS_M).astype(jnp.float32)
    hm, c_f, n_f, m_f = mlstm_bidirectional(qm, km, vm, g[0], g[1], c0.astype(jnp.float32),
                                            n0.astype(jnp.float32), m0.astype(jnp.float32))
    hm = rms_norm(hm, m_norm_w.reshape(N_HEADS_M, 1, DH_M))
    hm = jax.nn.sigmoid(heads(o_m, N_HEADS_M).astype(jnp.float32)) * hm
    hm = hm.transpose(0, 2, 1, 3).reshape(b, L, D_M).astype(h.dtype)
    out = jnp.concatenate([att, hm], axis=-1) @ w_out
    return out, (k_a, v_a, c_f, n_f, m_f)


def moe_ffn(h, router_w, router_bias, w_gate, w_up, w_down, ws_gate, ws_up, ws_down):
    b, L, d = h.shape
    t = h.reshape(-1, d)
    n_tok = t.shape[0]
    scores = jax.nn.sigmoid((t @ router_w).astype(jnp.float32))
    biased = scores + router_bias.astype(jnp.float32)
    grp = biased.reshape(n_tok, N_GROUPS, N_EXPERTS // N_GROUPS)
    grp_score = jnp.sum(lax.top_k(grp, 2)[0], axis=-1)
    _, grp_idx = lax.top_k(grp_score, TOPK_GROUPS)
    grp_mask = jnp.sum(jax.nn.one_hot(grp_idx, N_GROUPS, dtype=jnp.float32), axis=1)
    exp_mask = jnp.repeat(grp_mask, N_EXPERTS // N_GROUPS, axis=-1) > 0
    _, top_idx = lax.top_k(jnp.where(exp_mask, biased, -jnp.inf), TOP_K)
    top_w = jnp.take_along_axis(scores, top_idx, axis=-1)
    top_w = top_w / jnp.sum(top_w, axis=-1, keepdims=True) * ROUTED_SCALE
    gate = jnp.sum(jax.nn.one_hot(top_idx, N_EXPERTS, dtype=jnp.float32) * top_w[..., None], axis=1)
    gate = gate.astype(h.dtype)
    nb = n_tok // TOKEN_BLOCK

    def expert_block(args):
        tb, gb = args
        a = jax.nn.silu(jnp.einsum('td,edf->tef', tb, w_gate)) * jnp.einsum('td,edf->tef', tb, w_up)
        return jnp.einsum('tef,efd->td', a * gb[..., None], w_down)

    routed = lax.map(expert_block, (t.reshape(nb, TOKEN_BLOCK, d), gate.reshape(nb, TOKEN_BLOCK, N_EXPERTS)))
    shared = (jax.nn.silu(t @ ws_gate) * (t @ ws_up)) @ ws_down
    return (routed.reshape(n_tok, d) + shared).reshape(b, L, d)


def trunk_layer(x, mod, rope, ctx, params):
    (norm1_w, norm2_w, w_in, q_norm_w, k_norm_w, b_gates, m_norm_w, w_out,
     router_w, router_bias, w_gate, w_up, w_down, ws_gate, ws_up, ws_down) = params
    sh1, sc1, g1, sh2, sc2, g2 = jnp.split(mod, 6, axis=-1)
    hn = rms_norm(x, norm1_w) * (1 + sc1) + sh1
    y, ctx_out = token_mixers(hn, rope, ctx, w_in, q_norm_w, k_norm_w, b_gates, m_norm_w, w_out)
    x = x + g1 * y
    hn = rms_norm(x, norm2_w) * (1 + sc2) + sh2
    x = x + g2 * moe_ffn(hn, router_w, router_bias, w_gate, w_up, w_down, ws_gate, ws_up, ws_down)
    return x, ctx_out


def setup_inputs(seed: int = 0) -> dict:
    key = jax.random.key(seed)
    ks = jax.random.split(key, 32)
    nrm = lambda i, shape, s=1.0: jax.random.normal(ks[i], shape, jnp.float32) * s
    d = D_MODEL
    return {
        "x_prompt": nrm(0, (BATCH, SEQ, d)),
        "x_sample": nrm(1, (DEC_BATCH, DEC_SEQ, d)),
        "cache_attn_k": nrm(2, (DEC_BATCH, DEPTH, N_KV_HEADS, PAST_LEN, DH_ATT)),
        "cache_attn_v": nrm(3, (DEC_BATCH, DEPTH, N_KV_HEADS, PAST_LEN, DH_ATT)),
        "state_mlstm_c": nrm(4, (DEC_BATCH, DEPTH, 2, N_HEADS_M, DH_M, DH_M), 0.1),
        "state_mlstm_n": nrm(5, (DEC_BATCH, DEPTH, 2, N_HEADS_M, DH_M), 0.1),
        "state_mlstm_m": nrm(6, (DEC_BATCH, DEPTH, 2, N_HEADS_M), 0.5),
        "c": nrm(7, (DEC_BATCH, d)),
        "c_ctx": nrm(8, (d,)),
        "w_mod": nrm(9, (DEPTH, d, 6 * d), 0.3 * d ** -0.5),
        "b_mod": nrm(10, (DEPTH, 6 * d), 0.02),
        "norm1_w": 1.0 + nrm(11, (DEPTH, d), 0.05),
        "norm2_w": 1.0 + nrm(12, (DEPTH, d), 0.05),
        "w_in": nrm(13, (DEPTH, d, P_IN), d ** -0.5),
        "q_norm_w": 1.0 + nrm(14, (DEPTH, DH_ATT), 0.05),
        "k_norm_w": 1.0 + nrm(15, (DEPTH, DH_ATT), 0.05),
        "b_gates": nrm(16, (DEPTH, 2, 2, N_HEADS_M), 0.1) + jnp.array([0.0, 3.0], jnp.float32)[None, :, None, None],
        "m_norm_w": 1.0 + nrm(17, (DEPTH, D_M), 0.05),
        "w_out": nrm(18, (DEPTH, D_MIX, d), D_MIX ** -0.5),
        "router_w": nrm(19, (DEPTH, d, N_EXPERTS), d ** -0.5),
        "router_bias": nrm(20, (DEPTH, N_EXPERTS), 0.01),
        "w_gate": nrm(21, (DEPTH, N_EXPERTS, d, D_EXPERT), d ** -0.5),
        "w_up": nrm(22, (DEPTH, N_EXPERTS, d, D_EXPERT), d ** -0.5),
        "w_down": nrm(23, (DEPTH, N_EXPERTS, D_EXPERT, d), D_EXPERT ** -0.5),
        "ws_gate": nrm(24, (DEPTH, d, D_EXPERT), d ** -0.5),
        "ws_up": nrm(25, (DEPTH, d, D_EXPERT), d ** -0.5),
        "ws_down": nrm(26, (DEPTH, D_EXPERT, d), D_EXPERT ** -0.5),
    }


def reference(x_prompt, x_sample, cache_attn_k, cache_attn_v, state_mlstm_c, state_mlstm_n,
              state_mlstm_m, c, c_ctx, w_mod, b_mod, norm1_w, norm2_w, w_in, q_norm_w, k_norm_w,
              b_gates, m_norm_w, w_out, router_w, router_bias, w_gate, w_up, w_down,
              ws_gate, ws_up, ws_down):
    rope = axial_rope_tables(x_sample.shape[1])
    xp, xs = x_prompt, x_sample
    ks_, vs_, cs_, ns_, ms_ = [], [], [], [], []
    for l in range(DEPTH):
        params = (norm1_w[l], norm2_w[l], w_in[l], q_norm_w[l], k_norm_w[l], b_gates[l], m_norm_w[l],
                  w_out[l], router_w[l], router_bias[l], w_gate[l], w_up[l], w_down[l],
                  ws_gate[l], ws_up[l], ws_down[l])
        mod_ctx = (jax.nn.silu(c_ctx) @ w_mod[l] + b_mod[l])[None, None, :]
        mod_lat = (jax.nn.silu(c) @ w_mod[l] + b_mod[l])[:, None, :]
        xp, (k_c, v_c, c_c, n_c, m_c) = trunk_layer(xp, mod_ctx, None, None, params)
        ks_.append(k_c)
        vs_.append(v_c)
        cs_.append(c_c)
        ns_.append(n_c)
        ms_.append(m_c)
        ctx = (cache_attn_k[:, l], cache_attn_v[:, l], state_mlstm_c[:, l], state_mlstm_n[:, l],
               state_mlstm_m[:, l])
        xs, _ = trunk_layer(xs, mod_lat, rope, ctx, params)
    new_attn_k = jnp.stack(ks_, axis=1)
    new_attn_v = jnp.stack(vs_, axis=1)
    new_mlstm_c = jnp.stack(cs_, axis=1)
    new_mlstm_n = jnp.stack(ns_, axis=1)
    new_mlstm_m = jnp.stack(ms_, axis=1)
    return (xp, xs, new_attn_k, new_attn_v, new_mlstm_c, new_mlstm_n, new_mlstm_m)
```

```python
import functools

import jax
import jax.numpy as jnp
from jax import lax
from jax.experimental import pallas as pl
from jax.experimental.pallas import tpu as pltpu

F32 = jnp.float32
BF16 = jnp.bfloat16

D_MODEL = 1024
BATCH = 32
SEQ = 256
DEC_BATCH = 2
DEC_SEQ = 1024
PAST_LEN = 256
GRID_W = 64
N_HEADS_ATT = 8
N_KV_HEADS = 2
DH_ATT = 64
D_ATT = 512
ROPE_THETA = 10000.0
N_HEADS_M = 4
DH_M = 128
D_M = 512
CHUNK = 128
N_EXPERTS = 64
TOP_K = 8
N_GROUPS = 8
TOPK_GROUPS = 4
D_EXPERT = 256
ROUTED_SCALE = 2.5
EPS = 1e-6
NEG_INIT = -1e30

T_CTX = BATCH * SEQ
T_LAT = DEC_BATCH * DEC_SEQ
T_ALL = T_CTX + T_LAT
TB = 256
N_CTX_BLOCKS = T_CTX // TB
LAT_BLOCKS_PER_BATCH = DEC_SEQ // TB
TB_MOE = 1024
P_MAIN = 2816
VMEM_LIMIT = 56 * 1024 * 1024

_NT = (((1,), (1,)), ((), ()))
_TN = (((0,), (0,)), ((), ()))


def _cparams(sem):
    return pltpu.CompilerParams(dimension_semantics=sem, vmem_limit_bytes=VMEM_LIMIT)


def _split3(x):
    hi = x.astype(BF16)
    r1 = x - hi.astype(F32)
    mid = r1.astype(BF16)
    lo = (r1 - mid.astype(F32)).astype(BF16)
    return hi, mid, lo


def _dot(a, b):
    return jnp.dot(a, b, preferred_element_type=F32)


def _dot3(x, m_bf16):
    hi, mid, lo = _split3(x)
    return _dot(hi, m_bf16) + _dot(mid, m_bf16) + _dot(lo, m_bf16)


def _silu(x):
    return x * jax.nn.sigmoid(x)


def _mod_row(i):
    return jnp.where(i < N_CTX_BLOCKS, 0, 1 + (i - N_CTX_BLOCKS) // LAT_BLOCKS_PER_BATCH)


def _mod_kernel(ct_ref, w_ref, b_ref, o_ref):
    s = _silu(ct_ref[...])
    w = w_ref[...]
    rows = [jnp.sum(w * s[:, r:r + 1], axis=0, keepdims=True) for r in range(3)]
    rows.append(jnp.zeros((5, w.shape[1]), F32))
    o_ref[...] = jnp.concatenate(rows, axis=0) + b_ref[...]


def _modulation(c, c_ctx, w_mod, b_mod):
    cvec = jnp.concatenate([c_ctx[None, :], c, jnp.zeros((5, D_MODEL), F32)], axis=0)
    nb = 1024
    out = pl.pallas_call(
        _mod_kernel,
        grid=(6 * D_MODEL // nb,),
        in_specs=[pl.BlockSpec((D_MODEL, 8), lambda j: (0, 0)),
                  pl.BlockSpec((D_MODEL, nb), lambda j: (0, j)),
                  pl.BlockSpec((1, nb), lambda j: (0, j))],
        out_specs=pl.BlockSpec((8, nb), lambda j: (0, j)),
        out_shape=jax.ShapeDtypeStruct((8, 6 * D_MODEL), F32),
        compiler_params=_cparams(("arbitrary",)),
        name="modulation",
    )(cvec.T, w_mod, b_mod[None, :])
    return out.reshape(8, 6, D_MODEL)


def _inproj_kernel(x_ref, mod_ref, nw_ref, w_ref, wgt_ref,
                   qa_ref, ka_ref, va_ref, qm_ref, km_ref, vm_ref, om_ref, gt_ref):
    x = x_ref[...]
    y = x * lax.rsqrt(jnp.mean(x * x, axis=-1, keepdims=True) + EPS) * nw_ref[...]
    hn = y * (1.0 + mod_ref[1:2, :]) + mod_ref[0:1, :]
    hb = hn.astype(BF16)
    qa_ref[...] = _dot(hb, w_ref[:, 0:512])
    ka_ref[...] = _dot(hb, w_ref[:, 512:640])
    va_ref[...] = _dot(hb, w_ref[:, 640:768])
    qm_ref[...] = _dot(hb, w_ref[:, 768:1280]).astype(BF16)
    km_ref[...] = (_dot(hb, w_ref[:, 1280:1792]) * (DH_M ** -0.5)).astype(BF16)
    vm_ref[...] = _dot(hb, w_ref[:, 1792:2304]).astype(BF16)
    om_ref[...] = _dot(hb, w_ref[:, 2304:2816])
    gt_ref[...] = lax.dot_general(wgt_ref[...], hb, _NT, preferred_element_type=F32)


def _inproj(x_all, mod3, norm1_w, w_main, w_gates_t):
    tok = lambda w: pl.BlockSpec((TB, w), lambda i: (i, 0))
    full = lambda a: pl.BlockSpec(a.shape, lambda i: (0,) * a.ndim)
    sd = lambda w, dt: jax.ShapeDtypeStruct((T_ALL, w), dt)
    return pl.pallas_call(
        _inproj_kernel,
        grid=(T_ALL // TB,),
        in_specs=[tok(D_MODEL),
                  pl.BlockSpec((None, 6, D_MODEL), lambda i: (_mod_row(i), 0, 0)),
                  full(norm1_w), full(w_main), full(w_gates_t)],
        out_specs=[tok(512), tok(128), tok(128), tok(512), tok(512), tok(512), tok(512),
                   pl.BlockSpec((16, TB), lambda i: (0, i))],
        out_shape=[sd(512, F32), sd(128, F32), sd(128, F32), sd(512, BF16), sd(512, BF16),
                   sd(512, BF16), sd(512, F32), jax.ShapeDtypeStruct((16, T_ALL), F32)],
        compiler_params=_cparams(("arbitrary",)),
        name="inproj",
    )(x_all, mod3, norm1_w, w_main, w_gates_t)


def _head_norm(x, ind, ind_t, w_row):
    ss = _dot3(x * x, ind)
    inv = lax.rsqrt(ss * (1.0 / DH_ATT) + EPS)
    return x * _dot3(inv, ind_t) * w_row


def _rope(x, cos, sin_signed):
    lane = lax.broadcasted_iota(jnp.int32, x.shape, 1)
    partner = jnp.where((lane % 32) < 16, pltpu.roll(x, 128 - 16, 1), pltpu.roll(x, 16, 1))
    return x * cos + partner * sin_signed


def _qkprep_kernel(*refs, rope):
    if rope:
        qa_ref, ka_ref, qw_ref, kw_ref, iq_ref, iqt_ref, ik_ref, ikt_ref, cos_ref, sin_ref, qn_ref, kn_ref = refs
    else:
        qa_ref, ka_ref, qw_ref, kw_ref, iq_ref, iqt_ref, ik_ref, ikt_ref, qn_ref, kn_ref = refs
    qn = _head_norm(qa_ref[...], iq_ref[...], iqt_ref[...], qw_ref[...])
    kn = _head_norm(ka_ref[...], ik_ref[...], ikt_ref[...], kw_ref[...])
    if rope:
        cos, sin = cos_ref[...], sin_ref[...]
        qn = jnp.concatenate([_rope(qn[:, 128 * j:128 * (j + 1)], cos, sin) for j in range(4)], axis=1)
        kn = _rope(kn, cos, sin)
    qn_ref[...] = qn.astype(BF16)
    kn_ref[...] = kn


def _qkprep(qa, ka, qw, kw, inds, rope_tabs, *, first_block, n_blocks):
    full = lambda a: pl.BlockSpec(a.shape, lambda i: (0,) * a.ndim)
    tok = lambda w: pl.BlockSpec((TB, w), lambda i: (i + first_block, 0))
    in_specs = [tok(512), tok(128), full(qw), full(kw)] + [full(a) for a in inds]
    args = [qa, ka, qw, kw] + list(inds)
    if rope_tabs is not None:
        in_specs += [pl.BlockSpec((TB, 128), lambda i: (i % LAT_BLOCKS_PER_BATCH, 0))] * 2
        args += list(rope_tabs)
    n_tok = n_blocks * TB
    return pl.pallas_call(
        functools.partial(_qkprep_kernel, rope=rope_tabs is not None),
        grid=(n_blocks,),
        in_specs=in_specs,
        out_specs=[pl.BlockSpec((TB, 512), lambda i: (i, 0)), pl.BlockSpec((TB, 128), lambda i: (i, 0))],
        out_shape=[jax.ShapeDtypeStruct((n_tok, 512), BF16), jax.ShapeDtypeStruct((n_tok, 128), F32)],
        compiler_params=_cparams(("arbitrary",)),
        name="qkprep_lat" if rope_tabs is not None else "qkprep_ctx",
    )(*args)


def _rope_tables():
    t = jnp.arange(DEC_SEQ)
    pos = jnp.stack([t // GRID_W, t % GRID_W], axis=1).astype(F32)
    n_freq = DH_ATT // 4
    inv_freq = ROPE_THETA ** (-jnp.arange(n_freq, dtype=F32) / n_freq)
    ang = pos[:, :, None] * inv_freq
    cos, sin = jnp.cos(ang), jnp.sin(ang)
    cos_h = jnp.stack([cos, cos], axis=2).reshape(DEC_SEQ, DH_ATT)
    sin_h = jnp.stack([-sin, sin], axis=2).reshape(DEC_SEQ, DH_ATT)
    return jnp.tile(cos_h, (1, 2)), jnp.tile(sin_h, (1, 2))


def _attn_kernel(*refs, has_cache):
    if has_cache:
        q_ref, k_ref, v_ref, kc_ref, vc_ref, o_ref = refs
    else:
        q_ref, k_ref, v_ref, o_ref = refs
    q = q_ref[...]
    k = k_ref[...].astype(BF16)
    v = v_ref[...].astype(BF16)
    qb = q.shape[0]
    scale = DH_ATT ** -0.5
    outs = []
    for g in range(N_KV_HEADS):
        kg = k[:, DH_ATT * g:DH_ATT * (g + 1)]
        vg = v[:, DH_ATT * g:DH_ATT * (g + 1)]
        qg = jnp.concatenate([q[:, 256 * g + DH_ATT * j:256 * g + DH_ATT * (j + 1)] for j in range(4)], axis=0)
        s = lax.dot_general(qg, kg, _NT, preferred_element_type=F32) * scale
        m = jnp.max(s, axis=-1, keepdims=True)
        if has_cache:
            kc = kc_ref[g].astype(BF16)
            vc = vc_ref[g].astype(BF16)
            sc = lax.dot_general(qg, kc, _NT, preferred_element_type=F32) * scale
            m = jnp.maximum(m, jnp.max(sc, axis=-1, keepdims=True))
        p = jnp.exp(s - m)
        den = jnp.sum(p, axis=-1, keepdims=True)
        o = _dot(p.astype(BF16), vg)
        if has_cache:
            pc = jnp.exp(sc - m)
            den = den + jnp.sum(pc, axis=-1, keepdims=True)
            o = o + _dot(pc.astype(BF16), vc)
        o = o / den
        outs += [o[qb * j:qb * (j + 1), :] for j in range(4)]
    o_ref[...] = jnp.concatenate(outs, axis=1).astype(BF16)


def _attention(qn, kn, v_all, cache, *, n_batch, seq, v_first_block):
    qblocks = seq // TB
    kblk = lambda off: pl.BlockSpec((seq, 128), lambda b, i: (b + off, 0))
    in_specs = [pl.BlockSpec((TB, 512), lambda b, i: (b * qblocks + i, 0)), kblk(0), kblk(v_first_block)]
    args = [qn, kn, v_all]
    if cache is not None:
        cspec = pl.BlockSpec((None, None, N_KV_HEADS, PAST_LEN, DH_ATT), lambda b, i: (b, 0, 0, 0, 0))
        in_specs += [cspec, cspec]
        args += list(cache)
    return pl.pallas_call(
        functools.partial(_attn_kernel, has_cache=cache is not None),
        grid=(n_batch, qblocks),
        in_specs=in_specs,
        out_specs=pl.BlockSpec((TB, 512), lambda b, i: (b * qblocks + i, 0)),
        out_shape=jax.ShapeDtypeStruct((n_batch * seq, 512), BF16),
        compiler_params=_cparams(("arbitrary", "arbitrary")),
        name="attention_lat" if cache is not None else "attention_ctx",
    )(*args)


def _log_sigmoid(x):
    return jnp.minimum(x, 0.0) - jnp.log1p(jnp.exp(-jnp.abs(x)))


def _lane_scan(x, op, fill, reverse):
    lane = lax.broadcasted_iota(jnp.int32, x.shape, 1)
    for sh in (1, 2, 4, 8, 16, 32, 64):
        if reverse:
            shifted = jnp.where(lane < CHUNK - sh, pltpu.roll(x, CHUNK - sh, 1), fill)
        else:
            shifted = jnp.where(lane >= sh, pltpu.roll(x, sh, 1), fill)
        x = op(x, shifted)
    return x


def _mlstm_kernel(*refs, has_state, n_chunks):
    if has_state:
        (q_ref, k_ref, v_ref, g_ref, bg_ref, om_ref, nw_ref, c0_ref, n0_ref, m0_ref,
         hm_ref, cs, ns, ms, hacc) = refs
    else:
        (q_ref, k_ref, v_ref, g_ref, bg_ref, om_ref, nw_ref,
         hm_ref, c_ref, n_ref, m_ref, cs, ns, ms, hacc) = refs

    if has_state:
        for d in range(2):
            for h in range(N_HEADS_M):
                cs[4 * d + h] = c0_ref[d, h]
            ns[4 * d:4 * d + 4, :] = n0_ref[d]
        ms[...] = m0_ref[...]
    else:
        cs[...] = jnp.zeros(cs.shape, F32)
        ns[...] = jnp.zeros(ns.shape, F32)
        ms[...] = jnp.full(ms.shape, NEG_INIT, F32)
    hacc[...] = jnp.zeros(hacc.shape, F32)

    sub = lax.broadcasted_iota(jnp.int32, (CHUNK, CHUNK), 0)
    lan = lax.broadcasted_iota(jnp.int32, (CHUNK, CHUNK), 1)
    neg_inf = F32(-jnp.inf)

    def chunk_step(i, carry):
        for d in range(2):
            c = i if d == 0 else n_chunks - 1 - i
            r0 = pl.multiple_of(c * CHUNK, CHUNK)
            pre = g_ref[c] + bg_ref[...]
            bcum = _lane_scan(_log_sigmoid(pre), jnp.add, 0.0, d == 1)
            li4 = pre[4 * d:4 * d + 4, :]
            b4 = bcum[8 + 4 * d:12 + 4 * d, :]
            r4 = li4 - b4
            cm4 = _lane_scan(r4, jnp.maximum, neg_inf, d == 1)
            mp4 = ms[4 * d:4 * d + 4, :]
            g4 = b4 + mp4
            mrow4 = jnp.maximum(g4, b4 + cm4)
            blast4 = b4[:, CHUNK - 1:CHUNK] if d == 0 else b4[:, 0:1]
            wlog4 = blast4 + r4
            gend4 = blast4 + mp4
            mnew4 = jnp.maximum(gend4, jnp.max(wlog4, axis=-1, keepdims=True))
            decay4 = jnp.exp(gend4 - mnew4)
            rows = jnp.concatenate([b4 - mrow4, jnp.exp(g4 - mrow4), jnp.exp(-mrow4), jnp.exp(wlog4 - mnew4),
                                    jnp.zeros((CHUNK - 16, CHUNK), F32)], axis=0)
            cols = rows.T
            mask = (lan <= sub) if d == 0 else (lan >= sub)
            for h in range(N_HEADS_M):
                idx = 4 * d + h
                hs = slice(DH_M * h, DH_M * (h + 1))
                qc = q_ref[pl.ds(r0, CHUNK), hs]
                kc = k_ref[pl.ds(r0, CHUNK), hs]
                vc = v_ref[pl.ds(r0, CHUNK), hs]
                a_col = cols[:, h:h + 1]
                gw_col = cols[:, 4 + h:5 + h]
                en_col = cols[:, 8 + h:9 + h]
                w_col = cols[:, 12 + h:13 + h]
                decay = jnp.exp(jnp.where(mask, a_col + r4[h:h + 1, :], neg_inf))
                s = lax.dot_general(qc, kc, _NT, preferred_element_type=F32) * decay
                c_old = cs[idx]
                n_old = ns[idx:idx + 1, :]
                num = _dot(s.astype(BF16), vc) + gw_col * _dot(qc, c_old.astype(BF16))
                den = (jnp.sum(s, axis=-1, keepdims=True)
                       + gw_col * jnp.sum(qc.astype(F32) * n_old, axis=-1, keepdims=True))
                h_out = num / jnp.maximum(jnp.abs(den), en_col)
                hacc[pl.ds(r0, CHUNK), hs] += h_out
                kw = kc.astype(F32) * w_col
                dec = decay4[h:h + 1, :]
                cs[idx] = dec * c_old + lax.dot_general(kw.astype(BF16), vc, _TN, preferred_element_type=F32)
                ns[idx:idx + 1, :] = dec * n_old + jnp.sum(kw, axis=0, keepdims=True)
            ms[4 * d:4 * d + 4, :] = jnp.broadcast_to(mnew4, (4, CHUNK))
        return carry

    lax.fori_loop(0, n_chunks, chunk_step, 0)

    for h in range(N_HEADS_M):
        hs = slice(DH_M * h, DH_M * (h + 1))
        hh = hacc[:, hs]
        y = hh * lax.rsqrt(jnp.mean(hh * hh, axis=-1, keepdims=True) + EPS) * nw_ref[:, hs]
        hm_ref[:, hs] = (jax.nn.sigmoid(om_ref[:, hs]) * y).astype(BF16)
    if not has_state:
        for d in range(2):
            for h in range(N_HEADS_M):
                c_ref[d, h] = cs[4 * d + h]
            n_ref[d] = ns[4 * d:4 * d + 4, :]
        m_ref[...] = ms[...]


def _mlstm(qm, km, vm, g3, bg, om, nw, state, *, n_batch, seq, first_block):
    n_chunks = seq // CHUNK
    rb = seq // TB
    tok = lambda: pl.BlockSpec((seq, 512), lambda b: (b + first_block // rb, 0))
    full = lambda a: pl.BlockSpec(a.shape, lambda b: (0,) * a.ndim)
    in_specs = [tok(), tok(), tok(),
                pl.BlockSpec((n_chunks, 16, CHUNK), lambda b: (b + first_block // rb, 0, 0)),
                full(bg), tok(), full(nw)]
    args = [qm, km, vm, g3, bg, om, nw]
    hm_spec = pl.BlockSpec((seq, 512), lambda b: (b, 0))
    hm_shape = jax.ShapeDtypeStruct((n_batch * seq, 512), BF16)
    scratch = [pltpu.VMEM((8, DH_M, DH_M), F32), pltpu.VMEM((8, DH_M), F32), pltpu.VMEM((8, CHUNK), F32),
               pltpu.VMEM((seq, 512), F32)]
    if state is not None:
        c0, n0, m0 = state
        in_specs += [pl.BlockSpec((None, None, 2, N_HEADS_M, DH_M, DH_M), lambda b: (b, 0, 0, 0, 0, 0)),
                     pl.BlockSpec((None, None, 2, N_HEADS_M, DH_M), lambda b: (b, 0, 0, 0, 0)),
                     pl.BlockSpec((None, 8, CHUNK), lambda b: (b, 0, 0))]
        args += [c0, n0, m0]
        out_specs, out_shape = hm_spec, hm_shape
    else:
        out_specs = [hm_spec,
                     pl.BlockSpec((None, 2, N_HEADS_M, DH_M, DH_M), lambda b: (b, 0, 0, 0, 0)),
                     pl.BlockSpec((None, 2, N_HEADS_M, DH_M), lambda b: (b, 0, 0, 0)),
                     pl.BlockSpec((None, 8, CHUNK), lambda b: (b, 0, 0))]
        out_shape = [hm_shape,
                     jax.ShapeDtypeStruct((n_batch, 2, N_HEADS_M, DH_M, DH_M), F32),
                     jax.ShapeDtypeStruct((n_batch, 2, N_HEADS_M, DH_M), F32),
                     jax.ShapeDtypeStruct((n_batch, 8, CHUNK), F32)]
    return pl.pallas_call(
        functools.partial(_mlstm_kernel, has_state=state is not None, n_chunks=n_chunks),
        grid=(n_batch,),
        in_specs=in_specs,
        out_specs=out_specs,
        out_shape=out_shape,
        scratch_shapes=scratch,
        compiler_params=_cparams(("arbitrary",)),
        name="mlstm_lat" if state is not None else "mlstm_ctx",
    )(*args)


def _outproj_kernel(att_ref, hm_ref, x_ref, mod_ref, nw_ref, wo_ref, rwh_ref, rwl_ref,
                    wsg_ref, wsu_ref, wsd_ref, base_ref, hn_ref, lt_ref):
    y = _dot(att_ref[...], wo_ref[0:D_ATT, :]) + _dot(hm_ref[...], wo_ref[D_ATT:, :])
    x1 = x_ref[...] + mod_ref[2:3, :] * y
    z = x1 * lax.rsqrt(jnp.mean(x1 * x1, axis=-1, keepdims=True) + EPS) * nw_ref[...]
    hn = z * (1.0 + mod_ref[4:5, :]) + mod_ref[3:4, :]
    hb = hn.astype(BF16)
    hl = (hn - hb.astype(F32)).astype(BF16)
    nt = lambda w, t: lax.dot_general(w, t, _NT, preferred_element_type=F32)
    lt_ref[...] = nt(rwh_ref[...], hb) + nt(rwl_ref[...], hb) + nt(rwh_ref[...], hl)
    a = _silu(_dot(hb, wsg_ref[...])) * _dot(hb, wsu_ref[...])
    shared = _dot(a.astype(BF16), wsd_ref[...])
    base_ref[...] = x1 + mod_ref[5:6, :] * shared
    hn_ref[...] = hb


def _outproj(att, hm, x_all, mod3, norm2_w, w_out, rw_hi, rw_lo, wsg, wsu, wsd):
    tok = lambda w: pl.BlockSpec((TB, w), lambda i: (i, 0))
    full = lambda a: pl.BlockSpec(a.shape, lambda i: (0,) * a.ndim)
    return pl.pallas_call(
        _outproj_kernel,
        grid=(T_ALL // TB,),
        in_specs=[tok(512), tok(512), tok(D_MODEL),
                  pl.BlockSpec((None, 6, D_MODEL), lambda i: (_mod_row(i), 0, 0)),
                  full(norm2_w), full(w_out), full(rw_hi), full(rw_lo), full(wsg), full(wsu), full(wsd)],
        out_specs=[tok(D_MODEL), tok(D_MODEL), pl.BlockSpec((N_EXPERTS, TB), lambda i: (0, i))],
        out_shape=[jax.ShapeDtypeStruct((T_ALL, D_MODEL), F32),
                   jax.ShapeDtypeStruct((T_ALL, D_MODEL), BF16),
                   jax.ShapeDtypeStruct((N_EXPERTS, T_ALL), F32)],
        compiler_params=_cparams(("arbitrary",)),
        name="outproj",
    )(att, hm, x_all, mod3, norm2_w, w_out, rw_hi, rw_lo, wsg, wsu, wsd)


def _first_max(vals, ids, limit):
    m = functools.reduce(jnp.maximum, [jnp.max(v, axis=0, keepdims=True) for v in vals])
    cand = [jnp.min(jnp.where(v == m, i, limit), axis=0, keepdims=True) for v, i in zip(vals, ids)]
    return m, functools.reduce(jnp.minimum, cand)


def _router_kernel(lt_ref, bias_ref, gate_ref):
    n = lt_ref.shape[1]
    score = jax.nn.sigmoid(lt_ref[...])
    biased = score + bias_ref[...]
    sub = lax.broadcasted_iota(jnp.int32, (8, n), 0).astype(F32)
    neg_inf = F32(-jnp.inf)
    slabs = [biased[8 * g:8 * (g + 1), :] for g in range(N_GROUPS)]
    gs = []
    for sl in slabs:
        m1, i1 = _first_max([sl], [sub], 8.0)
        m2 = jnp.max(jnp.where(sub == i1, neg_inf, sl), axis=0, keepdims=True)
        gs.append(m1 + m2)
    cur = jnp.concatenate(gs, axis=0)
    gsel = jnp.zeros((8, n), F32)
    for _ in range(TOPK_GROUPS):
        _, i = _first_max([cur], [sub], 8.0)
        hit = sub == i
        gsel = jnp.where(hit, 1.0, gsel)
        cur = jnp.where(hit, neg_inf, cur)
    vals = [jnp.where(gsel[g:g + 1, :] > 0.0, slabs[g], neg_inf) for g in range(N_GROUPS)]
    ids = [sub + 8.0 * g for g in range(N_GROUPS)]
    picked = [jnp.zeros((8, n), F32) for _ in range(N_GROUPS)]
    for _ in range(TOP_K):
        _, i = _first_max(vals, ids, float(N_EXPERTS))
        hits = [idg == i for idg in ids]
        picked = [jnp.where(hh, score[8 * g:8 * (g + 1), :], p) for g, (p, hh) in enumerate(zip(picked, hits))]
        vals = [jnp.where(hh, neg_inf, v) for v, hh in zip(vals, hits)]
    total = functools.reduce(jnp.add, [jnp.sum(p, axis=0, keepdims=True) for p in picked])
    gate_t = jnp.concatenate([p / total * ROUTED_SCALE for p in picked]
                             + [jnp.zeros((128 - N_EXPERTS, n), F32)], axis=0)
    gate_ref[...] = gate_t.T


def _router(logits_t, bias_col):
    return pl.pallas_call(
        _router_kernel,
        grid=(T_ALL // TB_MOE,),
        in_specs=[pl.BlockSpec((N_EXPERTS, TB_MOE), lambda i: (0, i)),
                  pl.BlockSpec((N_EXPERTS, 1), lambda i: (0, 0))],
        out_specs=pl.BlockSpec((TB_MOE, 128), lambda i: (i, 0)),
        out_shape=jax.ShapeDtypeStruct((T_ALL, 128), F32),
        compiler_params=_cparams(("arbitrary",)),
        name="router",
    )(logits_t, bias_col)


def _moe_kernel(h_ref, gate_ref, wg_ref, wu_ref, wd_ref, base_ref, mod_ref, o_ref, acc_ref):
    e = pl.program_id(1)

    @pl.when(e == 0)
    def _():
        acc_ref[...] = jnp.zeros(acc_ref.shape, F32)

    h = h_ref[...]
    a = _silu(_dot(h, wg_ref[...])) * _dot(h, wu_ref[...])
    gate = gate_ref[...]
    pick = (lax.broadcasted_iota(jnp.int32, (128, D_EXPERT), 0) == e).astype(BF16)
    g_hi = gate.astype(BF16)
    g_lo = (gate - g_hi.astype(F32)).astype(BF16)
    g_col = _dot(g_hi, pick) + _dot(g_lo, pick)
    acc_ref[...] += _dot((a * g_col).astype(BF16), wd_ref[...])

    @pl.when(e == N_EXPERTS - 1)
    def _():
        o_ref[...] = base_ref[...] + mod_ref[5:6, :] * acc_ref[...]


def _moe(hn, gate, wg, wu, wd, base, mod3):
    tok = lambda w: pl.BlockSpec((TB_MOE, w), lambda i, e: (i, 0))
    ctx_blocks = T_CTX // TB_MOE
    return pl.pallas_call(
        _moe_kernel,
        grid=(T_ALL // TB_MOE, N_EXPERTS),
        in_specs=[tok(D_MODEL), tok(128),
                  pl.BlockSpec((None, D_MODEL, D_EXPERT), lambda i, e: (e, 0, 0)),
                  pl.BlockSpec((None, D_MODEL, D_EXPERT), lambda i, e: (e, 0, 0)),
                  pl.BlockSpec((None, D_EXPERT, D_MODEL), lambda i, e: (e, 0, 0)),
                  tok(D_MODEL),
                  pl.BlockSpec((None, 6, D_MODEL),
                               lambda i, e: (jnp.where(i < ctx_blocks, 0, i - ctx_blocks + 1), 0, 0))],
        out_specs=tok(D_MODEL),
        out_shape=jax.ShapeDtypeStruct((T_ALL, D_MODEL), F32),
        scratch_shapes=[pltpu.VMEM((TB_MOE, D_MODEL), F32)],
        compiler_params=_cparams(("arbitrary", "arbitrary")),
        name="moe",
    )(hn, gate, wg, wu, wd, base, mod3)


def _head_indicators(width):
    head = jnp.arange(width) // DH_ATT
    ind = (head[:, None] == jnp.arange(128)[None, :]).astype(BF16)
    return ind, ind.T


def kernel(x_prompt, x_sample, cache_attn_k, cache_attn_v, state_mlstm_c, state_mlstm_n, state_mlstm_m, c, c_ctx, w_mod, b_mod, norm1_w, norm2_w, w_in, q_norm_w, k_norm_w, b_gates, m_norm_w, w_out, router_w, router_bias, w_gate, w_up, w_down, ws_gate, ws_up, ws_down):
    x_all = jnp.concatenate([x_prompt.reshape(T_CTX, D_MODEL), x_sample.reshape(T_LAT, D_MODEL)], axis=0)
    mod3 = _modulation(c, c_ctx, w_mod[0], b_mod[0])

    w_main = w_in[0, :, :P_MAIN].astype(BF16)
    w_gates_t = w_in[0, :, P_MAIN:].T.astype(BF16)
    qa, ka, va, qm, km, vm, om, gt = _inproj(x_all, mod3, norm1_w, w_main, w_gates_t)

    qw = jnp.tile(q_norm_w, (1, N_HEADS_ATT))
    kw = jnp.tile(k_norm_w, (1, N_KV_HEADS))
    inds = _head_indicators(512) + _head_indicators(128)
    qn_c, kn_c = _qkprep(qa, ka, qw, kw, inds, None, first_block=0, n_blocks=N_CTX_BLOCKS)
    qn_l, kn_l = _qkprep(qa, ka, qw, kw, inds, _rope_tables(), first_block=N_CTX_BLOCKS,
                         n_blocks=T_LAT // TB)
    att_c = _attention(qn_c, kn_c, va, None, n_batch=BATCH, seq=SEQ, v_first_block=0)
    att_l = _attention(qn_l, kn_l, va, (cache_attn_k, cache_attn_v), n_batch=DEC_BATCH, seq=DEC_SEQ,
                       v_first_block=T_CTX // DEC_SEQ)

    g3 = gt.reshape(16, T_ALL // CHUNK, CHUNK).transpose(1, 0, 2)
    bg = b_gates.reshape(16, 1)
    hm_c, c_new, n_new, m_new = _mlstm(qm, km, vm, g3, bg, om, m_norm_w, None,
                                       n_batch=BATCH, seq=SEQ, first_block=0)
    m0 = jnp.broadcast_to(state_mlstm_m.reshape(DEC_BATCH, 8, 1), (DEC_BATCH, 8, CHUNK))
    hm_l = _mlstm(qm, km, vm, g3, bg, om, m_norm_w, (state_mlstm_c, state_mlstm_n, m0),
                  n_batch=DEC_BATCH, seq=DEC_SEQ, first_block=N_CTX_BLOCKS)

    att = jnp.concatenate([att_c, att_l], axis=0)
    hm = jnp.concatenate([hm_c, hm_l], axis=0)
    rw_t = router_w[0].T
    rw_hi = rw_t.astype(BF16)
    rw_lo = (rw_t - rw_hi.astype(F32)).astype(BF16)
    base, hn2, logits_t = _outproj(att, hm, x_all, mod3, norm2_w, w_out[0].astype(BF16), rw_hi, rw_lo,
                                   ws_gate[0].astype(BF16), ws_up[0].astype(BF16), ws_down[0].astype(BF16))
    gate = _router(logits_t, router_bias.reshape(N_EXPERTS, 1))
    out = _moe(hn2, gate, w_gate[0].astype(BF16), w_up[0].astype(BF16), w_down[0].astype(BF16), base, mod3)

    y_prompt = out[:T_CTX].reshape(BATCH, SEQ, D_MODEL)
    y_sample = out[T_CTX:].reshape(DEC_BATCH, DEC_SEQ, D_MODEL)
    to_cache = lambda a: a[:T_CTX].reshape(BATCH, SEQ, N_KV_HEADS, DH_ATT).transpose(0, 2, 1, 3)[:, None]
    new_k = to_cache(kn_c)
    new_v = to_cache(va)
    new_m = m_new[:, :, 0].reshape(BATCH, 1, 2, N_HEADS_M)
    return (y_prompt, y_sample, new_k, new_v, c_new[:, None], n_new[:, None], new_m)
```

```python
import functools

import jax
import jax.numpy as jnp
from jax import lax
from jax.experimental import pallas as pl
from jax.experimental.pallas import tpu as pltpu

F32 = jnp.float32
BF16 = jnp.bfloat16

D_MODEL = 1024
BATCH = 32
SEQ = 256
DEC_BATCH = 2
DEC_SEQ = 1024
PAST_LEN = 256
GRID_W = 64
N_HEADS_ATT = 8
N_KV_HEADS = 2
DH_ATT = 64
D_ATT = 512
ROPE_THETA = 10000.0
N_HEADS_M = 4
DH_M = 128
D_M = 512
CHUNK = 128
N_EXPERTS = 64
TOP_K = 8
N_GROUPS = 8
TOPK_GROUPS = 4
D_EXPERT = 256
ROUTED_SCALE = 2.5
EPS = 1e-6
NEG_INIT = -1e30

T_CTX = BATCH * SEQ
T_LAT = DEC_BATCH * DEC_SEQ
T_ALL = T_CTX + T_LAT
TB = 256
N_CTX_BLOCKS = T_CTX // TB
LAT_BLOCKS_PER_BATCH = DEC_SEQ // TB
TB_MOE = 1024
T_HALF = T_ALL // 2
SLOT_CAP = TOP_K * T_HALF + 8 * N_EXPERTS
MT = 256
MT_STRIDE = MT + 8
P_MAIN = 2816
VMEM_LIMIT = 56 * 1024 * 1024

_NT = (((1,), (1,)), ((), ()))
_TN = (((0,), (0,)), ((), ()))


def _cparams(sem):
    return pltpu.CompilerParams(dimension_semantics=sem, vmem_limit_bytes=VMEM_LIMIT)


def _split3(x):
    hi = x.astype(BF16)
    r1 = x - hi.astype(F32)
    mid = r1.astype(BF16)
    lo = (r1 - mid.astype(F32)).astype(BF16)
    return hi, mid, lo


def _dot(a, b):
    return jnp.dot(a, b, preferred_element_type=F32)


def _dot3(x, m_bf16):
    hi, mid, lo = _split3(x)
    return _dot(hi, m_bf16) + _dot(mid, m_bf16) + _dot(lo, m_bf16)


def _silu(x):
    return x * jax.nn.sigmoid(x)


def _mod_row(i):
    return jnp.where(i < N_CTX_BLOCKS, 0, 1 + (i - N_CTX_BLOCKS) // LAT_BLOCKS_PER_BATCH)


def _mod_kernel(ct_ref, w_ref, b_ref, o_ref):
    s = _silu(ct_ref[...])
    w = w_ref[...]
    rows = [jnp.sum(w * s[:, r:r + 1], axis=0, keepdims=True) for r in range(3)]
    rows.append(jnp.zeros((5, w.shape[1]), F32))
    o_ref[...] = jnp.concatenate(rows, axis=0) + b_ref[...]


def _modulation(c, c_ctx, w_mod, b_mod):
    cvec = jnp.concatenate([c_ctx[None, :], c, jnp.zeros((5, D_MODEL), F32)], axis=0)
    nb = 1024
    out = pl.pallas_call(
        _mod_kernel,
        grid=(6 * D_MODEL // nb,),
        in_specs=[pl.BlockSpec((D_MODEL, 8), lambda j: (0, 0)),
                  pl.BlockSpec((D_MODEL, nb), lambda j: (0, j)),
                  pl.BlockSpec((1, nb), lambda j: (0, j))],
        out_specs=pl.BlockSpec((8, nb), lambda j: (0, j)),
        out_shape=jax.ShapeDtypeStruct((8, 6 * D_MODEL), F32),
        compiler_params=_cparams(("arbitrary",)),
        name="modulation",
    )(cvec.T, w_mod, b_mod[None, :])
    return out.reshape(8, 6, D_MODEL)


def _inproj_kernel(x_ref, mod_ref, nw_ref, w_ref, wgt_ref,
                   qa_ref, ka_ref, va_ref, qm_ref, km_ref, vm_ref, om_ref, gt_ref):
    x = x_ref[...]
    y = x * lax.rsqrt(jnp.mean(x * x, axis=-1, keepdims=True) + EPS) * nw_ref[...]
    hn = y * (1.0 + mod_ref[1:2, :]) + mod_ref[0:1, :]
    hb = hn.astype(BF16)
    qa_ref[...] = _dot(hb, w_ref[:, 0:512])
    ka_ref[...] = _dot(hb, w_ref[:, 512:640])
    va_ref[...] = _dot(hb, w_ref[:, 640:768])
    qm_ref[...] = _dot(hb, w_ref[:, 768:1280]).astype(BF16)
    km_ref[...] = (_dot(hb, w_ref[:, 1280:1792]) * (DH_M ** -0.5)).astype(BF16)
    vm_ref[...] = _dot(hb, w_ref[:, 1792:2304]).astype(BF16)
    om_ref[...] = _dot(hb, w_ref[:, 2304:2816])
    gt_ref[...] = lax.dot_general(wgt_ref[...], hb, _NT, preferred_element_type=F32)


def _inproj(x_all, mod3, norm1_w, w_main, w_gates_t):
    tok = lambda w: pl.BlockSpec((TB, w), lambda i: (i, 0))
    full = lambda a: pl.BlockSpec(a.shape, lambda i: (0,) * a.ndim)
    sd = lambda w, dt: jax.ShapeDtypeStruct((T_ALL, w), dt)
    return pl.pallas_call(
        _inproj_kernel,
        grid=(T_ALL // TB,),
        in_specs=[tok(D_MODEL),
                  pl.BlockSpec((None, 6, D_MODEL), lambda i: (_mod_row(i), 0, 0)),
                  full(norm1_w), full(w_main), full(w_gates_t)],
        out_specs=[tok(512), tok(128), tok(128), tok(512), tok(512), tok(512), tok(512),
                   pl.BlockSpec((16, TB), lambda i: (0, i))],
        out_shape=[sd(512, F32), sd(128, F32), sd(128, F32), sd(512, BF16), sd(512, BF16),
                   sd(512, BF16), sd(512, F32), jax.ShapeDtypeStruct((16, T_ALL), F32)],
        compiler_params=_cparams(("arbitrary",)),
        name="inproj",
    )(x_all, mod3, norm1_w, w_main, w_gates_t)


def _head_norm(x, ind, ind_t, w_row):
    ss = _dot3(x * x, ind)
    inv = lax.rsqrt(ss * (1.0 / DH_ATT) + EPS)
    return x * _dot3(inv, ind_t) * w_row


def _rope(x, cos, sin_signed):
    lane = lax.broadcasted_iota(jnp.int32, x.shape, 1)
    partner = jnp.where((lane % 32) < 16, pltpu.roll(x, 128 - 16, 1), pltpu.roll(x, 16, 1))
    return x * cos + partner * sin_signed


def _qkprep_kernel(*refs, rope):
    if rope:
        qa_ref, ka_ref, qw_ref, kw_ref, iq_ref, iqt_ref, ik_ref, ikt_ref, cos_ref, sin_ref, qn_ref, kn_ref = refs
    else:
        qa_ref, ka_ref, qw_ref, kw_ref, iq_ref, iqt_ref, ik_ref, ikt_ref, qn_ref, kn_ref = refs
    qn = _head_norm(qa_ref[...], iq_ref[...], iqt_ref[...], qw_ref[...])
    kn = _head_norm(ka_ref[...], ik_ref[...], ikt_ref[...], kw_ref[...])
    if rope:
        cos, sin = cos_ref[...], sin_ref[...]
        qn = jnp.concatenate([_rope(qn[:, 128 * j:128 * (j + 1)], cos, sin) for j in range(4)], axis=1)
        kn = _rope(kn, cos, sin)
    qn_ref[...] = qn.astype(BF16)
    kn_ref[...] = kn


def _qkprep(qa, ka, qw, kw, inds, rope_tabs, *, first_block, n_blocks):
    full = lambda a: pl.BlockSpec(a.shape, lambda i: (0,) * a.ndim)
    tok = lambda w: pl.BlockSpec((TB, w), lambda i: (i + first_block, 0))
    in_specs = [tok(512), tok(128), full(qw), full(kw)] + [full(a) for a in inds]
    args = [qa, ka, qw, kw] + list(inds)
    if rope_tabs is not None:
        in_specs += [pl.BlockSpec((TB, 128), lambda i: (i % LAT_BLOCKS_PER_BATCH, 0))] * 2
        args += list(rope_tabs)
    n_tok = n_blocks * TB
    return pl.pallas_call(
        functools.partial(_qkprep_kernel, rope=rope_tabs is not None),
        grid=(n_blocks,),
        in_specs=in_specs,
        out_specs=[pl.BlockSpec((TB, 512), lambda i: (i, 0)), pl.BlockSpec((TB, 128), lambda i: (i, 0))],
        out_shape=[jax.ShapeDtypeStruct((n_tok, 512), BF16), jax.ShapeDtypeStruct((n_tok, 128), F32)],
        compiler_params=_cparams(("arbitrary",)),
        name="qkprep_lat" if rope_tabs is not None else "qkprep_ctx",
    )(*args)


def _rope_tables():
    t = jnp.arange(DEC_SEQ)
    pos = jnp.stack([t // GRID_W, t % GRID_W], axis=1).astype(F32)
    n_freq = DH_ATT // 4
    inv_freq = ROPE_THETA ** (-jnp.arange(n_freq, dtype=F32) / n_freq)
    ang = pos[:, :, None] * inv_freq
    cos, sin = jnp.cos(ang), jnp.sin(ang)
    cos_h = jnp.stack([cos, cos], axis=2).reshape(DEC_SEQ, DH_ATT)
    sin_h = jnp.stack([-sin, sin], axis=2).reshape(DEC_SEQ, DH_ATT)
    return jnp.tile(cos_h, (1, 2)), jnp.tile(sin_h, (1, 2))


def _attn_kernel(*refs, has_cache):
    if has_cache:
        q_ref, k_ref, v_ref, kc_ref, vc_ref, o_ref = refs
    else:
        q_ref, k_ref, v_ref, o_ref = refs
    q = q_ref[...]
    k = k_ref[...].astype(BF16)
    v = v_ref[...].astype(BF16)
    qb = q.shape[0]
    scale = DH_ATT ** -0.5
    outs = []
    for g in range(N_KV_HEADS):
        kg = k[:, DH_ATT * g:DH_ATT * (g + 1)]
        vg = v[:, DH_ATT * g:DH_ATT * (g + 1)]
        qg = jnp.concatenate([q[:, 256 * g + DH_ATT * j:256 * g + DH_ATT * (j + 1)] for j in range(4)], axis=0)
        s = lax.dot_general(qg, kg, _NT, preferred_element_type=F32) * scale
        m = jnp.max(s, axis=-1, keepdims=True)
        if has_cache:
            kc = kc_ref[g].astype(BF16)
            vc = vc_ref[g].astype(BF16)
            sc = lax.dot_general(qg, kc, _NT, preferred_element_type=F32) * scale
            m = jnp.maximum(m, jnp.max(sc, axis=-1, keepdims=True))
        p = jnp.exp(s - m)
        den = jnp.sum(p, axis=-1, keepdims=True)
        o = _dot(p.astype(BF16), vg)
        if has_cache:
            pc = jnp.exp(sc - m)
            den = den + jnp.sum(pc, axis=-1, keepdims=True)
            o = o + _dot(pc.astype(BF16), vc)
        o = o / den
        outs += [o[qb * j:qb * (j + 1), :] for j in range(4)]
    o_ref[...] = jnp.concatenate(outs, axis=1).astype(BF16)


def _attention(qn, kn, v_all, cache, *, n_batch, seq, v_first_block):
    qblocks = seq // TB
    kblk = lambda off: pl.BlockSpec((seq, 128), lambda b, i: (b + off, 0))
    in_specs = [pl.BlockSpec((TB, 512), lambda b, i: (b * qblocks + i, 0)), kblk(0), kblk(v_first_block)]
    args = [qn, kn, v_all]
    if cache is not None:
        cspec = pl.BlockSpec((None, None, N_KV_HEADS, PAST_LEN, DH_ATT), lambda b, i: (b, 0, 0, 0, 0))
        in_specs += [cspec, cspec]
        args += list(cache)
    return pl.pallas_call(
        functools.partial(_attn_kernel, has_cache=cache is not None),
        grid=(n_batch, qblocks),
        in_specs=in_specs,
        out_specs=pl.BlockSpec((TB, 512), lambda b, i: (b * qblocks + i, 0)),
        out_shape=jax.ShapeDtypeStruct((n_batch * seq, 512), BF16),
        compiler_params=_cparams(("arbitrary", "arbitrary")),
        name="attention_lat" if cache is not None else "attention_ctx",
    )(*args)


def _log_sigmoid(x):
    return jnp.minimum(x, 0.0) - jnp.log1p(jnp.exp(-jnp.abs(x)))


def _lane_scan(x, op, fill, reverse):
    lane = lax.broadcasted_iota(jnp.int32, x.shape, 1)
    for sh in (1, 2, 4, 8, 16, 32, 64):
        if reverse:
            shifted = jnp.where(lane < CHUNK - sh, pltpu.roll(x, CHUNK - sh, 1), fill)
        else:
            shifted = jnp.where(lane >= sh, pltpu.roll(x, sh, 1), fill)
        x = op(x, shifted)
    return x


def _mlstm_kernel(*refs, has_state, n_chunks):
    if has_state:
        (q_ref, k_ref, v_ref, g_ref, bg_ref, om_ref, nw_ref, c0_ref, n0_ref, m0_ref,
         hm_ref, cs, ns, ms, hacc) = refs
    else:
        (q_ref, k_ref, v_ref, g_ref, bg_ref, om_ref, nw_ref,
         hm_ref, c_ref, n_ref, m_ref, cs, ns, ms, hacc) = refs

    if has_state:
        for d in range(2):
            for h in range(N_HEADS_M):
                cs[4 * d + h] = c0_ref[d, h]
            ns[4 * d:4 * d + 4, :] = n0_ref[d]
        ms[...] = m0_ref[...]
    else:
        cs[...] = jnp.zeros(cs.shape, F32)
        ns[...] = jnp.zeros(ns.shape, F32)
        ms[...] = jnp.full(ms.shape, NEG_INIT, F32)
    hacc[...] = jnp.zeros(hacc.shape, F32)

    sub = lax.broadcasted_iota(jnp.int32, (CHUNK, CHUNK), 0)
    lan = lax.broadcasted_iota(jnp.int32, (CHUNK, CHUNK), 1)
    neg_inf = F32(-jnp.inf)

    def chunk_step(i, carry):
        for d in range(2):
            c = i if d == 0 else n_chunks - 1 - i
            r0 = pl.multiple_of(c * CHUNK, CHUNK)
            pre = g_ref[c] + bg_ref[...]
            bcum = _lane_scan(_log_sigmoid(pre), jnp.add, 0.0, d == 1)
            li4 = pre[4 * d:4 * d + 4, :]
            b4 = bcum[8 + 4 * d:12 + 4 * d, :]
            r4 = li4 - b4
            cm4 = _lane_scan(r4, jnp.maximum, neg_inf, d == 1)
            mp4 = ms[4 * d:4 * d + 4, :]
            g4 = b4 + mp4
            mrow4 = jnp.maximum(g4, b4 + cm4)
            blast4 = b4[:, CHUNK - 1:CHUNK] if d == 0 else b4[:, 0:1]
            wlog4 = blast4 + r4
            gend4 = blast4 + mp4
            mnew4 = jnp.maximum(gend4, jnp.max(wlog4, axis=-1, keepdims=True))
            decay4 = jnp.exp(gend4 - mnew4)
            rows = jnp.concatenate([b4 - mrow4, jnp.exp(g4 - mrow4), jnp.exp(-mrow4), jnp.exp(wlog4 - mnew4),
                                    jnp.zeros((CHUNK - 16, CHUNK), F32)], axis=0)
            cols = rows.T
            mask = (lan <= sub) if d == 0 else (lan >= sub)
            for h in range(N_HEADS_M):
                idx = 4 * d + h
                hs = slice(DH_M * h, DH_M * (h + 1))
                qc = q_ref[pl.ds(r0, CHUNK), hs]
                kc = k_ref[pl.ds(r0, CHUNK), hs]
                vc = v_ref[pl.ds(r0, CHUNK), hs]
                a_col = cols[:, h:h + 1]
                gw_col = cols[:, 4 + h:5 + h]
                en_col = cols[:, 8 + h:9 + h]
                w_col = cols[:, 12 + h:13 + h]
                decay = jnp.exp(jnp.where(mask, a_col + r4[h:h + 1, :], neg_inf))
                s = lax.dot_general(qc, kc, _NT, preferred_element_type=F32) * decay
                c_old = cs[idx]
                n_old = ns[idx:idx + 1, :]
                num = _dot(s.astype(BF16), vc) + gw_col * _dot(qc, c_old.astype(BF16))
                den = (jnp.sum(s, axis=-1, keepdims=True)
                       + gw_col * jnp.sum(qc.astype(F32) * n_old, axis=-1, keepdims=True))
                h_out = num / jnp.maximum(jnp.abs(den), en_col)
                hacc[pl.ds(r0, CHUNK), hs] += h_out
                kw = kc.astype(F32) * w_col
                dec = decay4[h:h + 1, :]
                cs[idx] = dec * c_old + lax.dot_general(kw.astype(BF16), vc, _TN, preferred_element_type=F32)
                ns[idx:idx + 1, :] = dec * n_old + jnp.sum(kw, axis=0, keepdims=True)
            ms[4 * d:4 * d + 4, :] = jnp.broadcast_to(mnew4, (4, CHUNK))
        return carry

    lax.fori_loop(0, n_chunks, chunk_step, 0)

    for h in range(N_HEADS_M):
        hs = slice(DH_M * h, DH_M * (h + 1))
        hh = hacc[:, hs]
        y = hh * lax.rsqrt(jnp.mean(hh * hh, axis=-1, keepdims=True) + EPS) * nw_ref[:, hs]
        hm_ref[:, hs] = (jax.nn.sigmoid(om_ref[:, hs]) * y).astype(BF16)
    if not has_state:
        for d in range(2):
            for h in range(N_HEADS_M):
                c_ref[d, h] = cs[4 * d + h]
            n_ref[d] = ns[4 * d:4 * d + 4, :]
        m_ref[...] = ms[...]


def _mlstm(qm, km, vm, g3, bg, om, nw, state, *, n_batch, seq, first_block):
    n_chunks = seq // CHUNK
    rb = seq // TB
    tok = lambda: pl.BlockSpec((seq, 512), lambda b: (b + first_block // rb, 0))
    full = lambda a: pl.BlockSpec(a.shape, lambda b: (0,) * a.ndim)
    in_specs = [tok(), tok(), tok(),
                pl.BlockSpec((n_chunks, 16, CHUNK), lambda b: (b + first_block // rb, 0, 0)),
                full(bg), tok(), full(nw)]
    args = [qm, km, vm, g3, bg, om, nw]
    hm_spec = pl.BlockSpec((seq, 512), lambda b: (b, 0))
    hm_shape = jax.ShapeDtypeStruct((n_batch * seq, 512), BF16)
    scratch = [pltpu.VMEM((8, DH_M, DH_M), F32), pltpu.VMEM((8, DH_M), F32), pltpu.VMEM((8, CHUNK), F32),
               pltpu.VMEM((seq, 512), F32)]
    if state is not None:
        c0, n0, m0 = state
        in_specs += [pl.BlockSpec((None, None, 2, N_HEADS_M, DH_M, DH_M), lambda b: (b, 0, 0, 0, 0, 0)),
                     pl.BlockSpec((None, None, 2, N_HEADS_M, DH_M), lambda b: (b, 0, 0, 0, 0)),
                     pl.BlockSpec((None, 8, CHUNK), lambda b: (b, 0, 0))]
        args += [c0, n0, m0]
        out_specs, out_shape = hm_spec, hm_shape
    else:
        out_specs = [hm_spec,
                     pl.BlockSpec((None, 2, N_HEADS_M, DH_M, DH_M), lambda b: (b, 0, 0, 0, 0)),
                     pl.BlockSpec((None, 2, N_HEADS_M, DH_M), lambda b: (b, 0, 0, 0)),
                     pl.BlockSpec((None, 8, CHUNK), lambda b: (b, 0, 0))]
        out_shape = [hm_shape,
                     jax.ShapeDtypeStruct((n_batch, 2, N_HEADS_M, DH_M, DH_M), F32),
                     jax.ShapeDtypeStruct((n_batch, 2, N_HEADS_M, DH_M), F32),
                     jax.ShapeDtypeStruct((n_batch, 8, CHUNK), F32)]
    return pl.pallas_call(
        functools.partial(_mlstm_kernel, has_state=state is not None, n_chunks=n_chunks),
        grid=(n_batch,),
        in_specs=in_specs,
        out_specs=out_specs,
        out_shape=out_shape,
        scratch_shapes=scratch,
        compiler_params=_cparams(("arbitrary",)),
        name="mlstm_lat" if state is not None else "mlstm_ctx",
    )(*args)


def _outproj_kernel(att_ref, hm_ref, x_ref, mod_ref, nw_ref, wo_ref, rwh_ref, rwl_ref,
                    wsg_ref, wsu_ref, wsd_ref, base_ref, hn_ref, lt_ref):
    y = _dot(att_ref[...], wo_ref[0:D_ATT, :]) + _dot(hm_ref[...], wo_ref[D_ATT:, :])
    x1 = x_ref[...] + mod_ref[2:3, :] * y
    z = x1 * lax.rsqrt(jnp.mean(x1 * x1, axis=-1, keepdims=True) + EPS) * nw_ref[...]
    hn = z * (1.0 + mod_ref[4:5, :]) + mod_ref[3:4, :]
    hb = hn.astype(BF16)
    hl = (hn - hb.astype(F32)).astype(BF16)
    nt = lambda w, t: lax.dot_general(w, t, _NT, preferred_element_type=F32)
    lt_ref[...] = nt(rwh_ref[...], hb) + nt(rwl_ref[...], hb) + nt(rwh_ref[...], hl)
    a = _silu(_dot(hb, wsg_ref[...])) * _dot(hb, wsu_ref[...])
    shared = _dot(a.astype(BF16), wsd_ref[...])
    base_ref[...] = x1 + mod_ref[5:6, :] * shared
    packed = pltpu.pack_elementwise([hn[:, :512], hn[:, 512:]], packed_dtype=BF16)
    for c in range(4):
        hn_ref[pl.ds(c, TB, stride=4), :] = packed[:, 128 * c:128 * (c + 1)]


def _outproj(att, hm, x_all, mod3, norm2_w, w_out, rw_hi, rw_lo, wsg, wsu, wsd):
    tok = lambda w: pl.BlockSpec((TB, w), lambda i: (i, 0))
    full = lambda a: pl.BlockSpec(a.shape, lambda i: (0,) * a.ndim)
    return pl.pallas_call(
        _outproj_kernel,
        grid=(T_ALL // TB,),
        in_specs=[tok(512), tok(512), tok(D_MODEL),
                  pl.BlockSpec((None, 6, D_MODEL), lambda i: (_mod_row(i), 0, 0)),
                  full(norm2_w), full(w_out), full(rw_hi), full(rw_lo), full(wsg), full(wsu), full(wsd)],
        out_specs=[tok(D_MODEL), pl.BlockSpec((4 * TB, 128), lambda i: (i, 0)),
                   pl.BlockSpec((N_EXPERTS, TB), lambda i: (0, i))],
        out_shape=[jax.ShapeDtypeStruct((T_ALL, D_MODEL), F32),
                   jax.ShapeDtypeStruct((4 * T_ALL, 128), jnp.uint32),
                   jax.ShapeDtypeStruct((N_EXPERTS, T_ALL), F32)],
        compiler_params=_cparams(("arbitrary",)),
        name="outproj",
    )(att, hm, x_all, mod3, norm2_w, w_out, rw_hi, rw_lo, wsg, wsu, wsd)


def _first_max(vals, ids, limit):
    m = functools.reduce(jnp.maximum, [jnp.max(v, axis=0, keepdims=True) for v in vals])
    cand = [jnp.min(jnp.where(v == m, i, limit), axis=0, keepdims=True) for v, i in zip(vals, ids)]
    return m, functools.reduce(jnp.minimum, cand)


def _router_kernel(lt_ref, bias_ref, gate_ref, idx_ref):
    n = lt_ref.shape[1]
    score = jax.nn.sigmoid(lt_ref[...])
    biased = score + bias_ref[...]
    sub = lax.broadcasted_iota(jnp.int32, (8, n), 0).astype(F32)
    neg_inf = F32(-jnp.inf)
    slabs = [biased[8 * g:8 * (g + 1), :] for g in range(N_GROUPS)]
    gs = []
    for sl in slabs:
        m1, i1 = _first_max([sl], [sub], 8.0)
        m2 = jnp.max(jnp.where(sub == i1, neg_inf, sl), axis=0, keepdims=True)
        gs.append(m1 + m2)
    cur = jnp.concatenate(gs, axis=0)
    gsel = jnp.zeros((8, n), F32)
    for _ in range(TOPK_GROUPS):
        _, i = _first_max([cur], [sub], 8.0)
        hit = sub == i
        gsel = jnp.where(hit, 1.0, gsel)
        cur = jnp.where(hit, neg_inf, cur)
    vals = [jnp.where(gsel[g:g + 1, :] > 0.0, slabs[g], neg_inf) for g in range(N_GROUPS)]
    ids = [sub + 8.0 * g for g in range(N_GROUPS)]
    picked = [jnp.zeros((8, n), F32) for _ in range(N_GROUPS)]
    order = []
    for _ in range(TOP_K):
        _, i = _first_max(vals, ids, float(N_EXPERTS))
        order.append(i)
        hits = [idg == i for idg in ids]
        picked = [jnp.where(hh, score[8 * g:8 * (g + 1), :], p) for g, (p, hh) in enumerate(zip(picked, hits))]
        vals = [jnp.where(hh, neg_inf, v) for v, hh in zip(vals, hits)]
    total = functools.reduce(jnp.add, [jnp.sum(p, axis=0, keepdims=True) for p in picked])
    gate_t = jnp.concatenate([p / total * ROUTED_SCALE for p in picked]
                             + [jnp.zeros((128 - N_EXPERTS, n), F32)], axis=0)
    gate_ref[...] = gate_t.T
    idx_ref[...] = jnp.concatenate(order, axis=0).astype(jnp.int32)


def _router(logits_t, bias_col):
    return pl.pallas_call(
        _router_kernel,
        grid=(T_ALL // TB_MOE,),
        in_specs=[pl.BlockSpec((N_EXPERTS, TB_MOE), lambda i: (0, i)),
                  pl.BlockSpec((N_EXPERTS, 1), lambda i: (0, 0))],
        out_specs=[pl.BlockSpec((TB_MOE, 128), lambda i: (i, 0)),
                   pl.BlockSpec((TOP_K, TB_MOE), lambda i: (0, i))],
        out_shape=[jax.ShapeDtypeStruct((T_ALL, 128), F32),
                   jax.ShapeDtypeStruct((TOP_K, T_ALL), jnp.int32)],
        compiler_params=_cparams(("arbitrary",)),
        name="router",
    )(logits_t, bias_col)


def _plan_kernel(idx_ref, slot_ref, cnt_ref, off_ref, pos_sc):
    n_tiles = T_HALF // 128
    eid = lax.broadcasted_iota(jnp.int32, (N_EXPERTS, 128), 0)
    tri = (lax.broadcasted_iota(jnp.int32, (128, 128), 0)
           <= lax.broadcasted_iota(jnp.int32, (128, 128), 1)).astype(BF16)
    carry = jnp.zeros((N_EXPERTS, 1), F32)
    for j in range(n_tiles):
        it = idx_ref[:, 128 * j:128 * (j + 1)]
        sel = jnp.zeros((N_EXPERTS, 128), F32)
        for k in range(TOP_K):
            sel = jnp.where(it[k:k + 1, :] == eid, 1.0, sel)
        inc = _dot(sel.astype(BF16), tri) + carry
        carry = inc[:, 127:128]
        pos_sc[:, 128 * j:128 * (j + 1)] = inc - 1.0
    cnt8 = jnp.broadcast_to(jnp.floor((carry + 7.0) * 0.125) * 8.0, (N_EXPERTS, 128))
    before = (lax.broadcasted_iota(jnp.int32, (N_EXPERTS, N_EXPERTS), 1)
              < lax.broadcasted_iota(jnp.int32, (N_EXPERTS, N_EXPERTS), 0)).astype(BF16)
    hi, mid, lo = _split3(cnt8)
    off = _dot(before, hi) + _dot(before, mid) + _dot(before, lo)
    cnt_ref[...] = cnt8.astype(jnp.int32)
    off_ref[...] = off.astype(jnp.int32)
    for j in range(n_tiles):
        it = idx_ref[:, 128 * j:128 * (j + 1)]
        val = off + pos_sc[:, 128 * j:128 * (j + 1)]
        rows = [jnp.sum(jnp.where(it[k:k + 1, :] == eid, val, 0.0), axis=0, keepdims=True)
                for k in range(TOP_K)]
        slot_ref[:, 128 * j:128 * (j + 1)] = jnp.concatenate(rows, axis=0).astype(jnp.int32)


def _plan(idx8):
    return pl.pallas_call(
        _plan_kernel,
        grid=(2,),
        in_specs=[pl.BlockSpec((TOP_K, T_HALF), lambda h: (0, h))],
        out_specs=[pl.BlockSpec((None, TOP_K, T_HALF), lambda h: (h, 0, 0)),
                   pl.BlockSpec((None, N_EXPERTS, 128), lambda h: (h, 0, 0)),
                   pl.BlockSpec((None, N_EXPERTS, 128), lambda h: (h, 0, 0))],
        out_shape=[jax.ShapeDtypeStruct((2, TOP_K, T_HALF), jnp.int32),
                   jax.ShapeDtypeStruct((2, N_EXPERTS, 128), jnp.int32),
                   jax.ShapeDtypeStruct((2, N_EXPERTS, 128), jnp.int32)],
        scratch_shapes=[pltpu.VMEM((N_EXPERTS, T_HALF), F32)],
        compiler_params=_cparams(("arbitrary",)),
        name="plan",
    )(idx8)


def _invert_kernel(slot_ref, tok_ref):
    def fill(i, carry):
        for u in range(8):
            tok_ref[i * 8 + u] = T_HALF
        return carry

    lax.fori_loop(0, SLOT_CAP // 8, fill, 0)
    for k in range(TOP_K):
        def body(i, carry, k=k):
            for u in range(8):
                t = i * 8 + u
                tok_ref[slot_ref[k * T_HALF + t]] = t
            return carry

        lax.fori_loop(0, T_HALF // 8, body, 0)


def _invert(slot_flat):
    return pl.pallas_call(
        _invert_kernel,
        in_specs=[pl.BlockSpec(memory_space=pltpu.SMEM)],
        out_specs=pl.BlockSpec(memory_space=pltpu.SMEM),
        out_shape=jax.ShapeDtypeStruct((SLOT_CAP,), jnp.int32),
        name="invert",
    )(slot_flat)


def _moe_kernel(tok_ref, cnt_ref, off_ref, src_hbm, gate_hbm, wg_ref, wu_ref, wd_ref, out_hbm,
                src_v, gate_v, acc_v, wgu_b, wd_b, xbuf, gbuf, ybuf, sem):
    h = pl.program_id(0)
    e = pl.program_id(1)

    @pl.when(e == 0)
    def _():
        src_cp = pltpu.make_async_copy(src_hbm.at[pl.ds(pl.multiple_of(h * (4 * T_HALF), 8), 4 * T_HALF)],
                                       src_v.at[pl.ds(0, 4 * T_HALF)], sem.at[0])
        gate_cp = pltpu.make_async_copy(gate_hbm.at[pl.ds(pl.multiple_of(h * T_HALF, 8), T_HALF)],
                                        gate_v.at[pl.ds(0, T_HALF)], sem.at[1])
        src_cp.start()
        gate_cp.start()
        acc_v[...] = jnp.zeros(acc_v.shape, F32)
        zero = jnp.zeros((8, 128), F32)
        zero_words = pltpu.pack_elementwise([zero, zero], packed_dtype=BF16)
        src_v[pl.ds(4 * T_HALF, 8), :] = zero_words
        gate_v[pl.ds(T_HALF, 8), :] = zero
        zero_tile = jnp.zeros(xbuf.shape, F32)
        xbuf[...] = pltpu.pack_elementwise([zero_tile, zero_tile], packed_dtype=BF16)
        gbuf[...] = jnp.zeros(gbuf.shape, F32)
        src_cp.wait()
        gate_cp.wait()

    wgu_b[:, 0:D_EXPERT] = wg_ref[...].astype(BF16)
    wgu_b[:, D_EXPERT:] = wu_ref[...].astype(BF16)
    wd_b[...] = wd_ref[...].astype(BF16)
    n_rows = cnt_ref[h * N_EXPERTS + e]
    first = h * SLOT_CAP + off_ref[h * N_EXPERTS + e]
    pick = (lax.broadcasted_iota(jnp.int32, (128, D_EXPERT), 0) == e).astype(BF16)

    def tile(j, carry):
        base = first + j * MT
        groups = jnp.minimum(n_rows - j * MT, MT) // 8

        def gather(i, c):
            for u in range(8):
                m = i * 8 + u
                t = tok_ref[base + m]
                xbuf[pl.ds(m, 4, stride=MT_STRIDE), :] = src_v[pl.ds(pl.multiple_of(t * 4, 4), 4), :]
                gbuf[pl.ds(m, 1), :] = gate_v[pl.ds(t, 1), :]
            return c

        lax.fori_loop(0, groups, gather, 0)

        lo, hi = [], []
        for c in range(4):
            words = xbuf[MT_STRIDE * c:MT_STRIDE * c + MT, :]
            unpack = functools.partial(pltpu.unpack_elementwise, words, packed_dtype=BF16, unpacked_dtype=F32)
            lo.append(unpack(index=0).astype(BF16))
            hi.append(unpack(index=1).astype(BF16))
        x = jnp.concatenate(lo + hi, axis=1)
        hgu = _dot(x, wgu_b[...])
        g = gbuf[...]
        g_hi = g.astype(BF16)
        g_lo = (g - g_hi.astype(F32)).astype(BF16)
        g_col = _dot(g_hi, pick) + _dot(g_lo, pick)
        a = _silu(hgu[:, :D_EXPERT]) * hgu[:, D_EXPERT:] * g_col
        y = _dot(a.astype(BF16), wd_b[...])
        for c in range(8):
            ybuf[MT_STRIDE * c:MT_STRIDE * c + MT, :] = y[:, 128 * c:128 * (c + 1)]

        def scatter(i, c):
            rows = [pl.ds(pl.multiple_of(tok_ref[base + i * 8 + u] * 8, 8), 8) for u in range(8)]
            vals = [acc_v[rows[u], :] + ybuf[pl.ds(i * 8 + u, 8, stride=MT_STRIDE), :] for u in range(8)]
            for u in range(8):
                acc_v[rows[u], :] = vals[u]
            return c

        lax.fori_loop(0, groups, scatter, 0)
        return carry

    lax.fori_loop(0, (n_rows + MT - 1) // MT, tile, 0)

    @pl.when(e == N_EXPERTS - 1)
    def _():
        out_cp = pltpu.make_async_copy(acc_v.at[pl.ds(0, 8 * T_HALF)],
                                       out_hbm.at[pl.ds(pl.multiple_of(h * (8 * T_HALF), 8), 8 * T_HALF)],
                                       sem.at[2])
        out_cp.start()
        out_cp.wait()


def _moe(tok, cnt8, off, src, gate, wg, wu, wd):
    wspec = lambda shape: pl.BlockSpec((None,) + shape, lambda h, e, *_: (e, 0, 0))
    any_spec = pl.BlockSpec(memory_space=pl.ANY)
    return pl.pallas_call(
        _moe_kernel,
        grid_spec=pltpu.PrefetchScalarGridSpec(
            num_scalar_prefetch=3,
            grid=(2, N_EXPERTS),
            in_specs=[any_spec, any_spec, wspec((D_MODEL, D_EXPERT)), wspec((D_MODEL, D_EXPERT)),
                      wspec((D_EXPERT, D_MODEL))],
            out_specs=any_spec,
            scratch_shapes=[pltpu.VMEM((4 * T_HALF + 8, 128), jnp.uint32),
                            pltpu.VMEM((T_HALF + 8, 128), F32),
                            pltpu.VMEM((8 * T_HALF + 8, 128), F32),
                            pltpu.VMEM((D_MODEL, 2 * D_EXPERT), BF16),
                            pltpu.VMEM((D_EXPERT, D_MODEL), BF16),
                            pltpu.VMEM((4 * MT_STRIDE, 128), jnp.uint32),
                            pltpu.VMEM((MT, 128), F32),
                            pltpu.VMEM((8 * MT_STRIDE, 128), F32),
                            pltpu.SemaphoreType.DMA((3,))]),
        out_shape=jax.ShapeDtypeStruct((8 * T_ALL, 128), F32),
        compiler_params=_cparams(("arbitrary", "arbitrary")),
        name="moe",
    )(tok, cnt8, off, src, gate, wg, wu, wd)


def _finalize_kernel(acc_ref, base_ref, mod_ref, o_ref):
    for c in range(8):
        cs = slice(128 * c, 128 * (c + 1))
        o_ref[:, cs] = base_ref[:, cs] + mod_ref[5:6, cs] * acc_ref[pl.ds(c, TB, stride=8), :]


def _finalize(acc, base, mod3):
    return pl.pallas_call(
        _finalize_kernel,
        grid=(T_ALL // TB,),
        in_specs=[pl.BlockSpec((8 * TB, 128), lambda i: (i, 0)),
                  pl.BlockSpec((TB, D_MODEL), lambda i: (i, 0)),
                  pl.BlockSpec((None, 6, D_MODEL), lambda i: (_mod_row(i), 0, 0))],
        out_specs=pl.BlockSpec((TB, D_MODEL), lambda i: (i, 0)),
        out_shape=jax.ShapeDtypeStruct((T_ALL, D_MODEL), F32),
        compiler_params=_cparams(("arbitrary",)),
        name="finalize",
    )(acc, base, mod3)


def _head_indicators(width):
    head = jnp.arange(width) // DH_ATT
    ind = (head[:, None] == jnp.arange(128)[None, :]).astype(BF16)
    return ind, ind.T


def kernel(x_prompt, x_sample, cache_attn_k, cache_attn_v, state_mlstm_c, state_mlstm_n, state_mlstm_m, c, c_ctx, w_mod, b_mod, norm1_w, norm2_w, w_in, q_norm_w, k_norm_w, b_gates, m_norm_w, w_out, router_w, router_bias, w_gate, w_up, w_down, ws_gate, ws_up, ws_down):
    x_all = jnp.concatenate([x_prompt.reshape(T_CTX, D_MODEL), x_sample.reshape(T_LAT, D_MODEL)], axis=0)
    mod3 = _modulation(c, c_ctx, w_mod[0], b_mod[0])

    w_main = w_in[0, :, :P_MAIN].astype(BF16)
    w_gates_t = w_in[0, :, P_MAIN:].T.astype(BF16)
    qa, ka, va, qm, km, vm, om, gt = _inproj(x_all, mod3, norm1_w, w_main, w_gates_t)

    qw = jnp.tile(q_norm_w, (1, N_HEADS_ATT))
    kw = jnp.tile(k_norm_w, (1, N_KV_HEADS))
    inds = _head_indicators(512) + _head_indicators(128)
    qn_c, kn_c = _qkprep(qa, ka, qw, kw, inds, None, first_block=0, n_blocks=N_CTX_BLOCKS)
    qn_l, kn_l = _qkprep(qa, ka, qw, kw, inds, _rope_tables(), first_block=N_CTX_BLOCKS,
                         n_blocks=T_LAT // TB)
    att_c = _attention(qn_c, kn_c, va, None, n_batch=BATCH, seq=SEQ, v_first_block=0)
    att_l = _attention(qn_l, kn_l, va, (cache_attn_k, cache_attn_v), n_batch=DEC_BATCH, seq=DEC_SEQ,
                       v_first_block=T_CTX // DEC_SEQ)

    g3 = gt.reshape(16, T_ALL // CHUNK, CHUNK).transpose(1, 0, 2)
    bg = b_gates.reshape(16, 1)
    hm_c, c_new, n_new, m_new = _mlstm(qm, km, vm, g3, bg, om, m_norm_w, None,
                                       n_batch=BATCH, seq=SEQ, first_block=0)
    m0 = jnp.broadcast_to(state_mlstm_m.reshape(DEC_BATCH, 8, 1), (DEC_BATCH, 8, CHUNK))
    hm_l = _mlstm(qm, km, vm, g3, bg, om, m_norm_w, (state_mlstm_c, state_mlstm_n, m0),
                  n_batch=DEC_BATCH, seq=DEC_SEQ, first_block=N_CTX_BLOCKS)

    att = jnp.concatenate([att_c, att_l], axis=0)
    hm = jnp.concatenate([hm_c, hm_l], axis=0)
    rw_t = router_w[0].T
    rw_hi = rw_t.astype(BF16)
    rw_lo = (rw_t - rw_hi.astype(F32)).astype(BF16)
    base, hn2, logits_t = _outproj(att, hm, x_all, mod3, norm2_w, w_out[0].astype(BF16), rw_hi, rw_lo,
                                   ws_gate[0].astype(BF16), ws_up[0].astype(BF16), ws_down[0].astype(BF16))
    gate, idx8 = _router(logits_t, router_bias.reshape(N_EXPERTS, 1))
    slot, cnt8, off = _plan(idx8)
    tok = jnp.concatenate([_invert(slot[hf].reshape(TOP_K * T_HALF)) for hf in range(2)])
    acc = _moe(tok, cnt8[:, :, 0].reshape(2 * N_EXPERTS), off[:, :, 0].reshape(2 * N_EXPERTS),
               hn2, gate, w_gate[0], w_up[0], w_down[0])
    out = _finalize(acc, base, mod3)

    y_prompt = out[:T_CTX].reshape(BATCH, SEQ, D_MODEL)
    y_sample = out[T_CTX:].reshape(DEC_BATCH, DEC_SEQ, D_MODEL)
    to_cache = lambda a: a[:T_CTX].reshape(BATCH, SEQ, N_KV_HEADS, DH_ATT).transpose(0, 2, 1, 3)[:, None]
    new_k = to_cache(kn_c)
    new_v = to_cache(va)
    new_m = m_new[:, :, 0].reshape(BATCH, 1, 2, N_HEADS_M)
    return (y_prompt, y_sample, new_k, new_v, c_new[:, None], n_new[:, None], new_m)
```

```python
import functools

import jax
import jax.numpy as jnp
from jax import lax
from jax.experimental import pallas as pl
from jax.experimental.pallas import tpu as pltpu

F32 = jnp.float32
BF16 = jnp.bfloat16

D_MODEL = 1024
BATCH = 32
SEQ = 256
DEC_BATCH = 2
DEC_SEQ = 1024
PAST_LEN = 256
GRID_W = 64
N_HEADS_ATT = 8
N_KV_HEADS = 2
DH_ATT = 64
D_ATT = 512
ROPE_THETA = 10000.0
N_HEADS_M = 4
DH_M = 128
D_M = 512
CHUNK = 128
N_EXPERTS = 64
TOP_K = 8
N_GROUPS = 8
TOPK_GROUPS = 4
D_EXPERT = 256
ROUTED_SCALE = 2.5
EPS = 1e-6
NEG_INIT = -1e30

T_CTX = BATCH * SEQ
T_LAT = DEC_BATCH * DEC_SEQ
T_ALL = T_CTX + T_LAT
TB = 256
N_CTX_BLOCKS = T_CTX // TB
LAT_BLOCKS_PER_BATCH = DEC_SEQ // TB
TB_MOE = 1024
T_HALF = T_ALL // 2
ROW_UNROLL = 16
INV_UNROLL = 16
SLOT_CAP = TOP_K * T_HALF + ROW_UNROLL * N_EXPERTS
assert SLOT_CAP < 2 ** 16 and (T_HALF // 2) % INV_UNROLL == 0
MT = 256
MT_STRIDE = MT + 8
P_MAIN = 2816
VMEM_LIMIT = 56 * 1024 * 1024

_NT = (((1,), (1,)), ((), ()))
_TN = (((0,), (0,)), ((), ()))


def _cparams(sem):
    return pltpu.CompilerParams(dimension_semantics=sem, vmem_limit_bytes=VMEM_LIMIT)


def _split3(x):
    hi = x.astype(BF16)
    r1 = x - hi.astype(F32)
    mid = r1.astype(BF16)
    lo = (r1 - mid.astype(F32)).astype(BF16)
    return hi, mid, lo


def _dot(a, b):
    return jnp.dot(a, b, preferred_element_type=F32)


def _dot3(x, m_bf16):
    hi, mid, lo = _split3(x)
    return _dot(hi, m_bf16) + _dot(mid, m_bf16) + _dot(lo, m_bf16)


def _silu(x):
    return x * jax.nn.sigmoid(x)


def _mod_row(i):
    return jnp.where(i < N_CTX_BLOCKS, 0, 1 + (i - N_CTX_BLOCKS) // LAT_BLOCKS_PER_BATCH)


def _mod_kernel(ct_ref, w_ref, b_ref, o_ref):
    s = _silu(ct_ref[...])
    w = w_ref[...]
    rows = [jnp.sum(w * s[:, r:r + 1], axis=0, keepdims=True) for r in range(3)]
    rows.append(jnp.zeros((5, w.shape[1]), F32))
    o_ref[...] = jnp.concatenate(rows, axis=0) + b_ref[...]


def _modulation(c, c_ctx, w_mod, b_mod):
    cvec = jnp.concatenate([c_ctx[None, :], c, jnp.zeros((5, D_MODEL), F32)], axis=0)
    nb = 1024
    out = pl.pallas_call(
        _mod_kernel,
        grid=(6 * D_MODEL // nb,),
        in_specs=[pl.BlockSpec((D_MODEL, 8), lambda j: (0, 0)),
                  pl.BlockSpec((D_MODEL, nb), lambda j: (0, j)),
                  pl.BlockSpec((1, nb), lambda j: (0, j))],
        out_specs=pl.BlockSpec((8, nb), lambda j: (0, j)),
        out_shape=jax.ShapeDtypeStruct((8, 6 * D_MODEL), F32),
        compiler_params=_cparams(("arbitrary",)),
        name="modulation",
    )(cvec.T, w_mod, b_mod[None, :])
    return out.reshape(8, 6, D_MODEL)


def _inproj_kernel(x_ref, mod_ref, nw_ref, w_ref, wgt_ref,
                   qa_ref, ka_ref, va_ref, qm_ref, km_ref, vm_ref, om_ref, gt_ref):
    x = x_ref[...]
    y = x * lax.rsqrt(jnp.mean(x * x, axis=-1, keepdims=True) + EPS) * nw_ref[...]
    hn = y * (1.0 + mod_ref[1:2, :]) + mod_ref[0:1, :]
    hb = hn.astype(BF16)
    qa_ref[...] = _dot(hb, w_ref[:, 0:512])
    ka_ref[...] = _dot(hb, w_ref[:, 512:640])
    va_ref[...] = _dot(hb, w_ref[:, 640:768])
    qm_ref[...] = _dot(hb, w_ref[:, 768:1280]).astype(BF16)
    km_ref[...] = (_dot(hb, w_ref[:, 1280:1792]) * (DH_M ** -0.5)).astype(BF16)
    vm_ref[...] = _dot(hb, w_ref[:, 1792:2304]).astype(BF16)
    om_ref[...] = _dot(hb, w_ref[:, 2304:2816])
    gt_ref[...] = lax.dot_general(wgt_ref[...], hb, _NT, preferred_element_type=F32)


def _inproj(x_all, mod3, norm1_w, w_main, w_gates_t):
    tok = lambda w: pl.BlockSpec((TB, w), lambda i: (i, 0))
    full = lambda a: pl.BlockSpec(a.shape, lambda i: (0,) * a.ndim)
    sd = lambda w, dt: jax.ShapeDtypeStruct((T_ALL, w), dt)
    return pl.pallas_call(
        _inproj_kernel,
        grid=(T_ALL // TB,),
        in_specs=[tok(D_MODEL),
                  pl.BlockSpec((None, 6, D_MODEL), lambda i: (_mod_row(i), 0, 0)),
                  full(norm1_w), full(w_main), full(w_gates_t)],
        out_specs=[tok(512), tok(128), tok(128), tok(512), tok(512), tok(512), tok(512),
                   pl.BlockSpec((16, TB), lambda i: (0, i))],
        out_shape=[sd(512, F32), sd(128, F32), sd(128, F32), sd(512, BF16), sd(512, BF16),
                   sd(512, BF16), sd(512, F32), jax.ShapeDtypeStruct((16, T_ALL), F32)],
        compiler_params=_cparams(("arbitrary",)),
        name="inproj",
    )(x_all, mod3, norm1_w, w_main, w_gates_t)


def _head_norm(x, ind, ind_t, w_row):
    ss = _dot3(x * x, ind)
    inv = lax.rsqrt(ss * (1.0 / DH_ATT) + EPS)
    return x * _dot3(inv, ind_t) * w_row


def _rope(x, cos, sin_signed):
    lane = lax.broadcasted_iota(jnp.int32, x.shape, 1)
    partner = jnp.where((lane % 32) < 16, pltpu.roll(x, 128 - 16, 1), pltpu.roll(x, 16, 1))
    return x * cos + partner * sin_signed


def _qkprep_kernel(*refs, rope):
    if rope:
        qa_ref, ka_ref, qw_ref, kw_ref, iq_ref, iqt_ref, ik_ref, ikt_ref, cos_ref, sin_ref, qn_ref, kn_ref = refs
    else:
        qa_ref, ka_ref, qw_ref, kw_ref, iq_ref, iqt_ref, ik_ref, ikt_ref, qn_ref, kn_ref = refs
    qn = _head_norm(qa_ref[...], iq_ref[...], iqt_ref[...], qw_ref[...])
    kn = _head_norm(ka_ref[...], ik_ref[...], ikt_ref[...], kw_ref[...])
    if rope:
        cos, sin = cos_ref[...], sin_ref[...]
        qn = jnp.concatenate([_rope(qn[:, 128 * j:128 * (j + 1)], cos, sin) for j in range(4)], axis=1)
        kn = _rope(kn, cos, sin)
    qn_ref[...] = qn.astype(BF16)
    kn_ref[...] = kn


def _qkprep(qa, ka, qw, kw, inds, rope_tabs, *, first_block, n_blocks):
    full = lambda a: pl.BlockSpec(a.shape, lambda i: (0,) * a.ndim)
    tok = lambda w: pl.BlockSpec((TB, w), lambda i: (i + first_block, 0))
    in_specs = [tok(512), tok(128), full(qw), full(kw)] + [full(a) for a in inds]
    args = [qa, ka, qw, kw] + list(inds)
    if rope_tabs is not None:
        in_specs += [pl.BlockSpec((TB, 128), lambda i: (i % LAT_BLOCKS_PER_BATCH, 0))] * 2
        args += list(rope_tabs)
    n_tok = n_blocks * TB
    return pl.pallas_call(
        functools.partial(_qkprep_kernel, rope=rope_tabs is not None),
        grid=(n_blocks,),
        in_specs=in_specs,
        out_specs=[pl.BlockSpec((TB, 512), lambda i: (i, 0)), pl.BlockSpec((TB, 128), lambda i: (i, 0))],
        out_shape=[jax.ShapeDtypeStruct((n_tok, 512), BF16), jax.ShapeDtypeStruct((n_tok, 128), F32)],
        compiler_params=_cparams(("arbitrary",)),
        name="qkprep_lat" if rope_tabs is not None else "qkprep_ctx",
    )(*args)


def _rope_tables():
    t = jnp.arange(DEC_SEQ)
    pos = jnp.stack([t // GRID_W, t % GRID_W], axis=1).astype(F32)
    n_freq = DH_ATT // 4
    inv_freq = ROPE_THETA ** (-jnp.arange(n_freq, dtype=F32) / n_freq)
    ang = pos[:, :, None] * inv_freq
    cos, sin = jnp.cos(ang), jnp.sin(ang)
    cos_h = jnp.stack([cos, cos], axis=2).reshape(DEC_SEQ, DH_ATT)
    sin_h = jnp.stack([-sin, sin], axis=2).reshape(DEC_SEQ, DH_ATT)
    return jnp.tile(cos_h, (1, 2)), jnp.tile(sin_h, (1, 2))


def _attn_kernel(*refs, has_cache):
    if has_cache:
        q_ref, k_ref, v_ref, kc_ref, vc_ref, o_ref = refs
    else:
        q_ref, k_ref, v_ref, o_ref = refs
    q = q_ref[...]
    k = k_ref[...].astype(BF16)
    v = v_ref[...].astype(BF16)
    qb = q.shape[0]
    scale = DH_ATT ** -0.5
    outs = []
    for g in range(N_KV_HEADS):
        kg = k[:, DH_ATT * g:DH_ATT * (g + 1)]
        vg = v[:, DH_ATT * g:DH_ATT * (g + 1)]
        qg = jnp.concatenate([q[:, 256 * g + DH_ATT * j:256 * g + DH_ATT * (j + 1)] for j in range(4)], axis=0)
        s = lax.dot_general(qg, kg, _NT, preferred_element_type=F32) * scale
        m = jnp.max(s, axis=-1, keepdims=True)
        if has_cache:
            kc = kc_ref[g].astype(BF16)
            vc = vc_ref[g].astype(BF16)
            sc = lax.dot_general(qg, kc, _NT, preferred_element_type=F32) * scale
            m = jnp.maximum(m, jnp.max(sc, axis=-1, keepdims=True))
        p = jnp.exp(s - m)
        den = jnp.sum(p, axis=-1, keepdims=True)
        o = _dot(p.astype(BF16), vg)
        if has_cache:
            pc = jnp.exp(sc - m)
            den = den + jnp.sum(pc, axis=-1, keepdims=True)
            o = o + _dot(pc.astype(BF16), vc)
        o = o / den
        outs += [o[qb * j:qb * (j + 1), :] for j in range(4)]
    o_ref[...] = jnp.concatenate(outs, axis=1).astype(BF16)


def _attention(qn, kn, v_all, cache, *, n_batch, seq, v_first_block):
    qblocks = seq // TB
    kblk = lambda off: pl.BlockSpec((seq, 128), lambda b, i: (b + off, 0))
    in_specs = [pl.BlockSpec((TB, 512), lambda b, i: (b * qblocks + i, 0)), kblk(0), kblk(v_first_block)]
    args = [qn, kn, v_all]
    if cache is not None:
        cspec = pl.BlockSpec((None, None, N_KV_HEADS, PAST_LEN, DH_ATT), lambda b, i: (b, 0, 0, 0, 0))
        in_specs += [cspec, cspec]
        args += list(cache)
    return pl.pallas_call(
        functools.partial(_attn_kernel, has_cache=cache is not None),
        grid=(n_batch, qblocks),
        in_specs=in_specs,
        out_specs=pl.BlockSpec((TB, 512), lambda b, i: (b * qblocks + i, 0)),
        out_shape=jax.ShapeDtypeStruct((n_batch * seq, 512), BF16),
        compiler_params=_cparams(("arbitrary", "arbitrary")),
        name="attention_lat" if cache is not None else "attention_ctx",
    )(*args)


def _log_sigmoid(x):
    return jnp.minimum(x, 0.0) - jnp.log1p(jnp.exp(-jnp.abs(x)))


def _lane_scan(x, op, fill, reverse):
    lane = lax.broadcasted_iota(jnp.int32, x.shape, 1)
    for sh in (1, 2, 4, 8, 16, 32, 64):
        if reverse:
            shifted = jnp.where(lane < CHUNK - sh, pltpu.roll(x, CHUNK - sh, 1), fill)
        else:
            shifted = jnp.where(lane >= sh, pltpu.roll(x, sh, 1), fill)
        x = op(x, shifted)
    return x


def _mlstm_kernel(*refs, has_state, n_chunks):
    n_in = 10 if has_state else 7
    n_out = 1 if has_state else 4
    q_ref, k_ref, v_ref, g_ref, bg_ref, om_ref, nw_ref = refs[:7]
    hm_ref = refs[n_in]
    scratch = refs[n_in + n_out:]
    cs = scratch[0:8]
    nm = scratch[8:16]
    hdir = scratch[16:18]

    for d in range(2):
        for h in range(N_HEADS_M):
            idx = 4 * d + h
            if has_state:
                c0_ref, n0_ref, m0_ref = refs[7:10]
                cs[idx][...] = c0_ref[d, h]
                nm[idx][0:1, :] = n0_ref[d, h:h + 1, :]
                nm[idx][1:2, :] = m0_ref[idx:idx + 1, :]
            else:
                cs[idx][...] = jnp.zeros((DH_M, DH_M), F32)
                nm[idx][0:1, :] = jnp.zeros((1, DH_M), F32)
                nm[idx][1:2, :] = jnp.full((1, CHUNK), NEG_INIT, F32)

    sub = lax.broadcasted_iota(jnp.int32, (CHUNK, CHUNK), 0)
    lan = lax.broadcasted_iota(jnp.int32, (CHUNK, CHUNK), 1)
    neg_inf = F32(-jnp.inf)

    def chunk_step(i, carry):
        for d in range(2):
            c = i if d == 0 else n_chunks - 1 - i
            r0 = pl.multiple_of(c * CHUNK, CHUNK)
            pre = g_ref[c] + bg_ref[...]
            bcum = _lane_scan(_log_sigmoid(pre), jnp.add, 0.0, d == 1)
            li4 = pre[4 * d:4 * d + 4, :]
            b4 = bcum[8 + 4 * d:12 + 4 * d, :]
            r4 = li4 - b4
            cm4 = _lane_scan(r4, jnp.maximum, neg_inf, d == 1)
            mp4 = jnp.concatenate([nm[4 * d + h][1:2, :] for h in range(N_HEADS_M)], axis=0)
            g4 = b4 + mp4
            mrow4 = jnp.maximum(g4, b4 + cm4)
            blast4 = b4[:, CHUNK - 1:CHUNK] if d == 0 else b4[:, 0:1]
            wlog4 = blast4 + r4
            gend4 = blast4 + mp4
            mnew4 = jnp.maximum(gend4, jnp.max(wlog4, axis=-1, keepdims=True))
            decay4 = jnp.exp(gend4 - mnew4)
            rows = jnp.concatenate([b4 - mrow4, jnp.exp(g4 - mrow4), jnp.exp(-mrow4), jnp.exp(wlog4 - mnew4),
                                    jnp.zeros((CHUNK - 16, CHUNK), F32)], axis=0)
            cols = rows.T
            mask = (lan <= sub) if d == 0 else (lan >= sub)
            for h in range(N_HEADS_M):
                idx = 4 * d + h
                hs = slice(DH_M * h, DH_M * (h + 1))
                qc = q_ref[pl.ds(r0, CHUNK), hs]
                kc = k_ref[pl.ds(r0, CHUNK), hs]
                vc = v_ref[pl.ds(r0, CHUNK), hs]
                a_col = cols[:, h:h + 1]
                gw_col = cols[:, 4 + h:5 + h]
                en_col = cols[:, 8 + h:9 + h]
                w_col = cols[:, 12 + h:13 + h]
                decay = jnp.exp(jnp.where(mask, a_col + r4[h:h + 1, :], neg_inf))
                s = lax.dot_general(qc, kc, _NT, preferred_element_type=F32) * decay
                c_old = cs[idx][...]
                n_old = nm[idx][0:1, :]
                num = _dot(s.astype(BF16), vc) + gw_col * _dot(qc, c_old.astype(BF16))
                den = (jnp.sum(s, axis=-1, keepdims=True)
                       + gw_col * jnp.sum(qc.astype(F32) * n_old, axis=-1, keepdims=True))
                hdir[d][pl.ds(r0, CHUNK), hs] = num / jnp.maximum(jnp.abs(den), en_col)
                kw = kc.astype(F32) * w_col
                dec = decay4[h:h + 1, :]
                cs[idx][...] = dec * c_old + lax.dot_general(kw.astype(BF16), vc, _TN,
                                                             preferred_element_type=F32)
                nm[idx][0:1, :] = dec * n_old + jnp.sum(kw, axis=0, keepdims=True)
                nm[idx][1:2, :] = mnew4[h:h + 1, :]
        return carry

    lax.fori_loop(0, n_chunks, chunk_step, 0)

    for h in range(N_HEADS_M):
        hs = slice(DH_M * h, DH_M * (h + 1))
        hh = hdir[0][:, hs] + hdir[1][:, hs]
        y = hh * lax.rsqrt(jnp.mean(hh * hh, axis=-1, keepdims=True) + EPS) * nw_ref[:, hs]
        hm_ref[:, hs] = (jax.nn.sigmoid(om_ref[:, hs]) * y).astype(BF16)
    if not has_state:
        c_ref, n_ref, m_ref = refs[n_in + 1:n_in + 4]
        for d in range(2):
            for h in range(N_HEADS_M):
                idx = 4 * d + h
                c_ref[d, h] = cs[idx][...]
                n_ref[d, h:h + 1, :] = nm[idx][0:1, :]
                m_ref[idx:idx + 1, :] = nm[idx][1:2, :]


def _mlstm(qm, km, vm, g3, bg, om, nw, state, *, n_batch, seq, first_block):
    n_chunks = seq // CHUNK
    rb = seq // TB
    tok = lambda: pl.BlockSpec((seq, 512), lambda b: (b + first_block // rb, 0))
    full = lambda a: pl.BlockSpec(a.shape, lambda b: (0,) * a.ndim)
    in_specs = [tok(), tok(), tok(),
                pl.BlockSpec((n_chunks, 16, CHUNK), lambda b: (b + first_block // rb, 0, 0)),
                full(bg), tok(), full(nw)]
    args = [qm, km, vm, g3, bg, om, nw]
    hm_spec = pl.BlockSpec((seq, 512), lambda b: (b, 0))
    hm_shape = jax.ShapeDtypeStruct((n_batch * seq, 512), BF16)
    scratch = ([pltpu.VMEM((DH_M, DH_M), F32)] * 8 + [pltpu.VMEM((8, DH_M), F32)] * 8
               + [pltpu.VMEM((seq, 512), F32)] * 2)
    if state is not None:
        c0, n0, m0 = state
        in_specs += [pl.BlockSpec((None, None, 2, N_HEADS_M, DH_M, DH_M), lambda b: (b, 0, 0, 0, 0, 0)),
                     pl.BlockSpec((None, None, 2, N_HEADS_M, DH_M), lambda b: (b, 0, 0, 0, 0)),
                     pl.BlockSpec((None, 8, CHUNK), lambda b: (b, 0, 0))]
        args += [c0, n0, m0]
        out_specs, out_shape = hm_spec, hm_shape
    else:
        out_specs = [hm_spec,
                     pl.BlockSpec((None, 2, N_HEADS_M, DH_M, DH_M), lambda b: (b, 0, 0, 0, 0)),
                     pl.BlockSpec((None, 2, N_HEADS_M, DH_M), lambda b: (b, 0, 0, 0)),
                     pl.BlockSpec((None, 8, CHUNK), lambda b: (b, 0, 0))]
        out_shape = [hm_shape,
                     jax.ShapeDtypeStruct((n_batch, 2, N_HEADS_M, DH_M, DH_M), F32),
                     jax.ShapeDtypeStruct((n_batch, 2, N_HEADS_M, DH_M), F32),
                     jax.ShapeDtypeStruct((n_batch, 8, CHUNK), F32)]
    return pl.pallas_call(
        functools.partial(_mlstm_kernel, has_state=state is not None, n_chunks=n_chunks),
        grid=(n_batch,),
        in_specs=in_specs,
        out_specs=out_specs,
        out_shape=out_shape,
        scratch_shapes=scratch,
        compiler_params=_cparams(("arbitrary",)),
        name="mlstm_lat" if state is not None else "mlstm_ctx",
    )(*args)


def _outproj_kernel(att_ref, hm_ref, x_ref, mod_ref, nw_ref, wo_ref, rwh_ref, rwl_ref,
                    wsg_ref, wsu_ref, wsd_ref, base_ref, hn_ref, lt_ref):
    y = _dot(att_ref[...], wo_ref[0:D_ATT, :]) + _dot(hm_ref[...], wo_ref[D_ATT:, :])
    x1 = x_ref[...] + mod_ref[2:3, :] * y
    z = x1 * lax.rsqrt(jnp.mean(x1 * x1, axis=-1, keepdims=True) + EPS) * nw_ref[...]
    hn = z * (1.0 + mod_ref[4:5, :]) + mod_ref[3:4, :]
    hb = hn.astype(BF16)
    hl = (hn - hb.astype(F32)).astype(BF16)
    nt = lambda w, t: lax.dot_general(w, t, _NT, preferred_element_type=F32)
    lt_ref[...] = nt(rwh_ref[...], hb) + nt(rwl_ref[...], hb) + nt(rwh_ref[...], hl)
    a = _silu(_dot(hb, wsg_ref[...])) * _dot(hb, wsu_ref[...])
    shared = _dot(a.astype(BF16), wsd_ref[...])
    base_ref[...] = x1 + mod_ref[5:6, :] * shared
    packed = pltpu.pack_elementwise([hn[:, :512], hn[:, 512:]], packed_dtype=BF16)
    for c in range(4):
        hn_ref[pl.ds(c, TB, stride=4), :] = packed[:, 128 * c:128 * (c + 1)]


def _outproj(att, hm, x_all, mod3, norm2_w, w_out, rw_hi, rw_lo, wsg, wsu, wsd):
    tok = lambda w: pl.BlockSpec((TB, w), lambda i: (i, 0))
    full = lambda a: pl.BlockSpec(a.shape, lambda i: (0,) * a.ndim)
    return pl.pallas_call(
        _outproj_kernel,
        grid=(T_ALL // TB,),
        in_specs=[tok(512), tok(512), tok(D_MODEL),
                  pl.BlockSpec((None, 6, D_MODEL), lambda i: (_mod_row(i), 0, 0)),
                  full(norm2_w), full(w_out), full(rw_hi), full(rw_lo), full(wsg), full(wsu), full(wsd)],
        out_specs=[tok(D_MODEL), pl.BlockSpec((4 * TB, 128), lambda i: (i, 0)),
                   pl.BlockSpec((N_EXPERTS, TB), lambda i: (0, i))],
        out_shape=[jax.ShapeDtypeStruct((T_ALL, D_MODEL), F32),
                   jax.ShapeDtypeStruct((4 * T_ALL, 128), jnp.uint32),
                   jax.ShapeDtypeStruct((N_EXPERTS, T_ALL), F32)],
        compiler_params=_cparams(("arbitrary",)),
        name="outproj",
    )(att, hm, x_all, mod3, norm2_w, w_out, rw_hi, rw_lo, wsg, wsu, wsd)


def _first_max(vals, ids, limit):
    m = functools.reduce(jnp.maximum, [jnp.max(v, axis=0, keepdims=True) for v in vals])
    cand = [jnp.min(jnp.where(v == m, i, limit), axis=0, keepdims=True) for v, i in zip(vals, ids)]
    return m, functools.reduce(jnp.minimum, cand)


def _router_kernel(lt_ref, bias_ref, gate_ref, idx_ref):
    n = lt_ref.shape[1]
    score = jax.nn.sigmoid(lt_ref[...])
    biased = score + bias_ref[...]
    sub = lax.broadcasted_iota(jnp.int32, (8, n), 0).astype(F32)
    neg_inf = F32(-jnp.inf)
    slabs = [biased[8 * g:8 * (g + 1), :] for g in range(N_GROUPS)]
    gs = []
    for sl in slabs:
        m1, i1 = _first_max([sl], [sub], 8.0)
        m2 = jnp.max(jnp.where(sub == i1, neg_inf, sl), axis=0, keepdims=True)
        gs.append(m1 + m2)
    cur = jnp.concatenate(gs, axis=0)
    gsel = jnp.zeros((8, n), F32)
    for _ in range(TOPK_GROUPS):
        _, i = _first_max([cur], [sub], 8.0)
        hit = sub == i
        gsel = jnp.where(hit, 1.0, gsel)
        cur = jnp.where(hit, neg_inf, cur)
    vals = [jnp.where(gsel[g:g + 1, :] > 0.0, slabs[g], neg_inf) for g in range(N_GROUPS)]
    ids = [sub + 8.0 * g for g in range(N_GROUPS)]
    picked = [jnp.zeros((8, n), F32) for _ in range(N_GROUPS)]
    order = []
    for _ in range(TOP_K):
        _, i = _first_max(vals, ids, float(N_EXPERTS))
        order.append(i)
        hits = [idg == i for idg in ids]
        picked = [jnp.where(hh, score[8 * g:8 * (g + 1), :], p) for g, (p, hh) in enumerate(zip(picked, hits))]
        vals = [jnp.where(hh, neg_inf, v) for v, hh in zip(vals, hits)]
    total = functools.reduce(jnp.add, [jnp.sum(p, axis=0, keepdims=True) for p in picked])
    gate_t = jnp.concatenate([p / total * ROUTED_SCALE for p in picked]
                             + [jnp.zeros((128 - N_EXPERTS, n), F32)], axis=0)
    gate_ref[...] = gate_t.T
    idx_ref[...] = jnp.concatenate(order, axis=0).astype(jnp.int32)


def _router(logits_t, bias_col):
    return pl.pallas_call(
        _router_kernel,
        grid=(T_ALL // TB_MOE,),
        in_specs=[pl.BlockSpec((N_EXPERTS, TB_MOE), lambda i: (0, i)),
                  pl.BlockSpec((N_EXPERTS, 1), lambda i: (0, 0))],
        out_specs=[pl.BlockSpec((TB_MOE, 128), lambda i: (i, 0)),
                   pl.BlockSpec((TOP_K, TB_MOE), lambda i: (0, i))],
        out_shape=[jax.ShapeDtypeStruct((T_ALL, 128), F32),
                   jax.ShapeDtypeStruct((TOP_K, T_ALL), jnp.int32)],
        compiler_params=_cparams(("arbitrary",)),
        name="router",
    )(logits_t, bias_col)


def _plan_kernel(idx_ref, slot_ref, cnt_ref, off_ref, pos_sc):
    n_tiles = T_HALF // 128
    eid = lax.broadcasted_iota(jnp.int32, (N_EXPERTS, 128), 0)
    tri = (lax.broadcasted_iota(jnp.int32, (128, 128), 0)
           <= lax.broadcasted_iota(jnp.int32, (128, 128), 1)).astype(BF16)
    carry = jnp.zeros((N_EXPERTS, 1), F32)
    for j in range(n_tiles):
        it = idx_ref[:, 128 * j:128 * (j + 1)]
        sel = jnp.zeros((N_EXPERTS, 128), F32)
        for k in range(TOP_K):
            sel = jnp.where(it[k:k + 1, :] == eid, 1.0, sel)
        inc = _dot(sel.astype(BF16), tri) + carry
        carry = inc[:, 127:128]
        pos_sc[:, 128 * j:128 * (j + 1)] = inc - 1.0
    cnt8 = jnp.broadcast_to(jnp.floor((carry + (ROW_UNROLL - 1.0)) * (1.0 / ROW_UNROLL)) * ROW_UNROLL,
                            (N_EXPERTS, 128))
    before = (lax.broadcasted_iota(jnp.int32, (N_EXPERTS, N_EXPERTS), 1)
              < lax.broadcasted_iota(jnp.int32, (N_EXPERTS, N_EXPERTS), 0)).astype(BF16)
    hi, mid, lo = _split3(cnt8)
    off = _dot(before, hi) + _dot(before, mid) + _dot(before, lo)
    cnt_ref[...] = cnt8.astype(jnp.int32)
    off_ref[...] = off.astype(jnp.int32)

    def slots(j):
        it = idx_ref[:, 128 * j:128 * (j + 1)]
        val = off + pos_sc[:, 128 * j:128 * (j + 1)]
        rows = [jnp.sum(jnp.where(it[k:k + 1, :] == eid, val, 0.0), axis=0, keepdims=True)
                for k in range(TOP_K)]
        return jnp.concatenate(rows, axis=0).astype(jnp.int32)

    for j in range(n_tiles // 2):
        slot_ref[:, 128 * j:128 * (j + 1)] = slots(j) | (slots(j + n_tiles // 2) << 16)


def _plan(idx8):
    return pl.pallas_call(
        _plan_kernel,
        grid=(2,),
        in_specs=[pl.BlockSpec((TOP_K, T_HALF), lambda h: (0, h))],
        out_specs=[pl.BlockSpec((None, TOP_K, T_HALF // 2), lambda h: (h, 0, 0)),
                   pl.BlockSpec((None, N_EXPERTS, 128), lambda h: (h, 0, 0)),
                   pl.BlockSpec((None, N_EXPERTS, 128), lambda h: (h, 0, 0))],
        out_shape=[jax.ShapeDtypeStruct((2, TOP_K, T_HALF // 2), jnp.int32),
                   jax.ShapeDtypeStruct((2, N_EXPERTS, 128), jnp.int32),
                   jax.ShapeDtypeStruct((2, N_EXPERTS, 128), jnp.int32)],
        scratch_shapes=[pltpu.VMEM((N_EXPERTS, T_HALF), F32)],
        compiler_params=_cparams(("arbitrary",)),
        name="plan",
    )(idx8)


def _invert_slots(slot_ref, cnt_ref, off_ref, tok_ref, h):
    def pad(x, carry):
        n = cnt_ref[h * N_EXPERTS + x]
        last = off_ref[h * N_EXPERTS + x] + n - ROW_UNROLL

        @pl.when(n > 0)
        def _():
            for u in range(ROW_UNROLL):
                tok_ref[last + u] = T_HALF
        return carry

    lax.fori_loop(0, N_EXPERTS, pad, 0)
    half = T_HALF // 2
    for k in range(TOP_K):
        def body(i, carry, k=k):
            words = [slot_ref[(h * TOP_K + k) * half + i * INV_UNROLL + u] for u in range(INV_UNROLL)]
            for u in range(INV_UNROLL):
                t = i * INV_UNROLL + u
                tok_ref[words[u] & 0xFFFF] = t
                tok_ref[lax.shift_right_logical(words[u], 16)] = t + half
            return carry

        lax.fori_loop(0, half // INV_UNROLL, body, 0)


def _moe_kernel(slot_ref, cnt_ref, off_ref, src_hbm, gate_hbm, wg_ref, wu_ref, wd_ref, out_hbm,
                src_v, gate_v, acc_v, wgu_b, wd_b, xbuf, gbuf, ybuf, tok_ref, sem):
    h = pl.program_id(0)
    e = pl.program_id(1)

    @pl.when(e == 0)
    def _():
        src_cp = pltpu.make_async_copy(src_hbm.at[pl.ds(pl.multiple_of(h * (4 * T_HALF), 8), 4 * T_HALF)],
                                       src_v.at[pl.ds(0, 4 * T_HALF)], sem.at[0])
        gate_cp = pltpu.make_async_copy(gate_hbm.at[pl.ds(pl.multiple_of(h * T_HALF, 8), T_HALF)],
                                        gate_v.at[pl.ds(0, T_HALF)], sem.at[1])
        src_cp.start()
        gate_cp.start()
        acc_v[...] = jnp.zeros(acc_v.shape, F32)
        zero = jnp.zeros((8, 128), F32)
        zero_words = pltpu.pack_elementwise([zero, zero], packed_dtype=BF16)
        src_v[pl.ds(4 * T_HALF, 8), :] = zero_words
        gate_v[pl.ds(T_HALF, 8), :] = zero
        zero_tile = jnp.zeros(xbuf.shape, F32)
        xbuf[...] = pltpu.pack_elementwise([zero_tile, zero_tile], packed_dtype=BF16)
        gbuf[...] = jnp.zeros(gbuf.shape, F32)
        _invert_slots(slot_ref, cnt_ref, off_ref, tok_ref, h)
        src_cp.wait()
        gate_cp.wait()

    wgu_b[:, 0:D_EXPERT] = wg_ref[...].astype(BF16)
    wgu_b[:, D_EXPERT:] = wu_ref[...].astype(BF16)
    wd_b[...] = wd_ref[...].astype(BF16)
    n_rows = cnt_ref[h * N_EXPERTS + e]
    first = off_ref[h * N_EXPERTS + e]
    pick = (lax.broadcasted_iota(jnp.int32, (128, D_EXPERT), 0) == e).astype(BF16)

    def tile(j, carry):
        base = first + j * MT
        groups = (jnp.minimum(n_rows - j * MT, MT) + ROW_UNROLL - 1) // ROW_UNROLL

        def gather(i, c):
            toks = [tok_ref[base + i * ROW_UNROLL + u] for u in range(ROW_UNROLL)]
            for u in range(ROW_UNROLL):
                m = i * ROW_UNROLL + u
                t = toks[u]
                xbuf[pl.ds(m, 4, stride=MT_STRIDE), :] = src_v[pl.ds(pl.multiple_of(t * 4, 4), 4), :]
                gbuf[pl.ds(m, 1), :] = gate_v[pl.ds(t, 1), :]
            return c

        lax.fori_loop(0, groups, gather, 0)

        lo, hi = [], []
        for c in range(4):
            words = xbuf[MT_STRIDE * c:MT_STRIDE * c + MT, :]
            unpack = functools.partial(pltpu.unpack_elementwise, words, packed_dtype=BF16, unpacked_dtype=F32)
            lo.append(unpack(index=0).astype(BF16))
            hi.append(unpack(index=1).astype(BF16))
        x = jnp.concatenate(lo + hi, axis=1)
        hgu = _dot(x, wgu_b[...])
        g = gbuf[...]
        g_hi = g.astype(BF16)
        g_lo = (g - g_hi.astype(F32)).astype(BF16)
        g_col = _dot(g_hi, pick) + _dot(g_lo, pick)
        a = _silu(hgu[:, :D_EXPERT]) * hgu[:, D_EXPERT:] * g_col
        y = _dot(a.astype(BF16), wd_b[...])
        for c in range(8):
            ybuf[MT_STRIDE * c:MT_STRIDE * c + MT, :] = y[:, 128 * c:128 * (c + 1)]

        def scatter(i, c):
            rows = [pl.ds(pl.multiple_of(tok_ref[base + i * ROW_UNROLL + u] * 8, 8), 8)
                    for u in range(ROW_UNROLL)]
            vals = [acc_v[rows[u], :] + ybuf[pl.ds(i * ROW_UNROLL + u, 8, stride=MT_STRIDE), :]
                    for u in range(ROW_UNROLL)]
            for u in range(ROW_UNROLL):
                acc_v[rows[u], :] = vals[u]
            return c

        lax.fori_loop(0, groups, scatter, 0)
        return carry

    lax.fori_loop(0, (n_rows + MT - 1) // MT, tile, 0)

    @pl.when(e == N_EXPERTS - 1)
    def _():
        out_cp = pltpu.make_async_copy(acc_v.at[pl.ds(0, 8 * T_HALF)],
                                       out_hbm.at[pl.ds(pl.multiple_of(h * (8 * T_HALF), 8), 8 * T_HALF)],
                                       sem.at[2])
        out_cp.start()
        out_cp.wait()


def _moe(slot_words, cnt, off, src, gate, wg, wu, wd):
    wspec = lambda shape: pl.BlockSpec((None,) + shape, lambda h, e, *_: (e, 0, 0))
    any_spec = pl.BlockSpec(memory_space=pl.ANY)
    return pl.pallas_call(
        _moe_kernel,
        grid_spec=pltpu.PrefetchScalarGridSpec(
            num_scalar_prefetch=3,
            grid=(2, N_EXPERTS),
            in_specs=[any_spec, any_spec, wspec((D_MODEL, D_EXPERT)), wspec((D_MODEL, D_EXPERT)),
                      wspec((D_EXPERT, D_MODEL))],
            out_specs=any_spec,
            scratch_shapes=[pltpu.VMEM((4 * T_HALF + 8, 128), jnp.uint32),
                            pltpu.VMEM((T_HALF + 8, 128), F32),
                            pltpu.VMEM((8 * T_HALF + 8, 128), F32),
                            pltpu.VMEM((D_MODEL, 2 * D_EXPERT), BF16),
                            pltpu.VMEM((D_EXPERT, D_MODEL), BF16),
                            pltpu.VMEM((4 * MT_STRIDE, 128), jnp.uint32),
                            pltpu.VMEM((MT, 128), F32),
                            pltpu.VMEM((8 * MT_STRIDE, 128), F32),
                            pltpu.SMEM((SLOT_CAP,), jnp.int32),
                            pltpu.SemaphoreType.DMA((3,))]),
        out_shape=jax.ShapeDtypeStruct((8 * T_ALL, 128), F32),
        compiler_params=_cparams(("arbitrary", "arbitrary")),
        name="moe",
    )(slot_words, cnt, off, src, gate, wg, wu, wd)


def _finalize_kernel(acc_ref, base_ref, mod_ref, o_ref):
    for c in range(8):
        cs = slice(128 * c, 128 * (c + 1))
        o_ref[:, cs] = base_ref[:, cs] + mod_ref[5:6, cs] * acc_ref[pl.ds(c, TB, stride=8), :]


def _finalize(acc, base, mod3):
    return pl.pallas_call(
        _finalize_kernel,
        grid=(T_ALL // TB,),
        in_specs=[pl.BlockSpec((8 * TB, 128), lambda i: (i, 0)),
                  pl.BlockSpec((TB, D_MODEL), lambda i: (i, 0)),
                  pl.BlockSpec((None, 6, D_MODEL), lambda i: (_mod_row(i), 0, 0))],
        out_specs=pl.BlockSpec((TB, D_MODEL), lambda i: (i, 0)),
        out_shape=jax.ShapeDtypeStruct((T_ALL, D_MODEL), F32),
        compiler_params=_cparams(("arbitrary",)),
        name="finalize",
    )(acc, base, mod3)


def _head_indicators(width):
    head = jnp.arange(width) // DH_ATT
    ind = (head[:, None] == jnp.arange(128)[None, :]).astype(BF16)
    return ind, ind.T


def kernel(x_prompt, x_sample, cache_attn_k, cache_attn_v, state_mlstm_c, state_mlstm_n, state_mlstm_m, c, c_ctx, w_mod, b_mod, norm1_w, norm2_w, w_in, q_norm_w, k_norm_w, b_gates, m_norm_w, w_out, router_w, router_bias, w_gate, w_up, w_down, ws_gate, ws_up, ws_down):
    x_all = jnp.concatenate([x_prompt.reshape(T_CTX, D_MODEL), x_sample.reshape(T_LAT, D_MODEL)], axis=0)
    mod3 = _modulation(c, c_ctx, w_mod[0], b_mod[0])

    w_main = w_in[0, :, :P_MAIN].astype(BF16)
    w_gates_t = w_in[0, :, P_MAIN:].T.astype(BF16)
    qa, ka, va, qm, km, vm, om, gt = _inproj(x_all, mod3, norm1_w, w_main, w_gates_t)

    qw = jnp.tile(q_norm_w, (1, N_HEADS_ATT))
    kw = jnp.tile(k_norm_w, (1, N_KV_HEADS))
    inds = _head_indicators(512) + _head_indicators(128)
    qn_c, kn_c = _qkprep(qa, ka, qw, kw, inds, None, first_block=0, n_blocks=N_CTX_BLOCKS)
    qn_l, kn_l = _qkprep(qa, ka, qw, kw, inds, _rope_tables(), first_block=N_CTX_BLOCKS,
                         n_blocks=T_LAT // TB)
    att_c = _attention(qn_c, kn_c, va, None, n_batch=BATCH, seq=SEQ, v_first_block=0)
    att_l = _attention(qn_l, kn_l, va, (cache_attn_k, cache_attn_v), n_batch=DEC_BATCH, seq=DEC_SEQ,
                       v_first_block=T_CTX // DEC_SEQ)

    g3 = gt.reshape(16, T_ALL // CHUNK, CHUNK).transpose(1, 0, 2)
    bg = b_gates.reshape(16, 1)
    hm_c, c_new, n_new, m_new = _mlstm(qm, km, vm, g3, bg, om, m_norm_w, None,
                                       n_batch=BATCH, seq=SEQ, first_block=0)
    m0 = jnp.broadcast_to(state_mlstm_m.reshape(DEC_BATCH, 8, 1), (DEC_BATCH, 8, CHUNK))
    hm_l = _mlstm(qm, km, vm, g3, bg, om, m_norm_w, (state_mlstm_c, state_mlstm_n, m0),
                  n_batch=DEC_BATCH, seq=DEC_SEQ, first_block=N_CTX_BLOCKS)

    att = jnp.concatenate([att_c, att_l], axis=0)
    hm = jnp.concatenate([hm_c, hm_l], axis=0)
    rw_t = router_w[0].T
    rw_hi = rw_t.astype(BF16)
    rw_lo = (rw_t - rw_hi.astype(F32)).astype(BF16)
    base, hn2, logits_t = _outproj(att, hm, x_all, mod3, norm2_w, w_out[0].astype(BF16), rw_hi, rw_lo,
                                   ws_gate[0].astype(BF16), ws_up[0].astype(BF16), ws_down[0].astype(BF16))
    gate, idx8 = _router(logits_t, router_bias.reshape(N_EXPERTS, 1))
    slot_words, cnt, off = _plan(idx8)
    acc = _moe(slot_words.reshape(TOP_K * T_HALF), cnt[:, :, 0].reshape(2 * N_EXPERTS),
               off[:, :, 0].reshape(2 * N_EXPERTS), hn2, gate, w_gate[0], w_up[0], w_down[0])
    out = _finalize(acc, base, mod3)

    y_prompt = out[:T_CTX].reshape(BATCH, SEQ, D_MODEL)
    y_sample = out[T_CTX:].reshape(DEC_BATCH, DEC_SEQ, D_MODEL)
    to_cache = lambda a: a[:T_CTX].reshape(BATCH, SEQ, N_KV_HEADS, DH_ATT).transpose(0, 2, 1, 3)[:, None]
    new_k = to_cache(kn_c)
    new_v = to_cache(va)
    new_m = m_new[:, :, 0].reshape(BATCH, 1, 2, N_HEADS_M)
    return (y_prompt, y_sample, new_k, new_v, c_new[:, None], n_new[:, None], new_m)
```

```python
import functools

import jax
import jax.numpy as jnp
from jax import lax
from jax.experimental import pallas as pl
from jax.experimental.pallas import tpu as pltpu

F32 = jnp.float32
BF16 = jnp.bfloat16

D_MODEL = 1024
BATCH = 32
SEQ = 256
DEC_BATCH = 2
DEC_SEQ = 1024
PAST_LEN = 256
GRID_W = 64
N_HEADS_ATT = 8
N_KV_HEADS = 2
DH_ATT = 64
D_ATT = 512
ROPE_THETA = 10000.0
N_HEADS_M = 4
DH_M = 128
D_M = 512
CHUNK = 128
N_EXPERTS = 64
TOP_K = 8
N_GROUPS = 8
TOPK_GROUPS = 4
D_EXPERT = 256
ROUTED_SCALE = 2.5
EPS = 1e-6
NEG_INIT = -1e30

T_CTX = BATCH * SEQ
T_LAT = DEC_BATCH * DEC_SEQ
T_ALL = T_CTX + T_LAT
TB = 256
N_CTX_BLOCKS = T_CTX // TB
LAT_BLOCKS_PER_BATCH = DEC_SEQ // TB
TB_MOE = 1024
T_HALF = T_ALL // 2
ROW_UNROLL = 16
INV_UNROLL = 16
SLOT_CAP = TOP_K * T_HALF + ROW_UNROLL * N_EXPERTS
assert SLOT_CAP < 2 ** 16 and (T_HALF // 2) % INV_UNROLL == 0
MT = 256
MT_STRIDE = MT + 8
P_MAIN = 2816
VMEM_LIMIT = 56 * 1024 * 1024

_NT = (((1,), (1,)), ((), ()))
_TN = (((0,), (0,)), ((), ()))


def _cparams(sem):
    return pltpu.CompilerParams(dimension_semantics=sem, vmem_limit_bytes=VMEM_LIMIT)


def _split3(x):
    hi = x.astype(BF16)
    r1 = x - hi.astype(F32)
    mid = r1.astype(BF16)
    lo = (r1 - mid.astype(F32)).astype(BF16)
    return hi, mid, lo


def _dot(a, b):
    return jnp.dot(a, b, preferred_element_type=F32)


def _dot3(x, m_bf16):
    hi, mid, lo = _split3(x)
    return _dot(hi, m_bf16) + _dot(mid, m_bf16) + _dot(lo, m_bf16)


def _silu(x):
    return x * jax.nn.sigmoid(x)


def _mod_row(i):
    return jnp.where(i < N_CTX_BLOCKS, 0, 1 + (i - N_CTX_BLOCKS) // LAT_BLOCKS_PER_BATCH)


def _mod_kernel(ct_ref, w_ref, b_ref, o_ref):
    s = _silu(ct_ref[...])
    w = w_ref[...]
    rows = [jnp.sum(w * s[:, r:r + 1], axis=0, keepdims=True) for r in range(3)]
    rows.append(jnp.zeros((5, w.shape[1]), F32))
    o_ref[...] = jnp.concatenate(rows, axis=0) + b_ref[...]


def _modulation(c, c_ctx, w_mod, b_mod):
    cvec = jnp.concatenate([c_ctx[None, :], c, jnp.zeros((5, D_MODEL), F32)], axis=0)
    nb = 1024
    out = pl.pallas_call(
        _mod_kernel,
        grid=(6 * D_MODEL // nb,),
        in_specs=[pl.BlockSpec((D_MODEL, 8), lambda j: (0, 0)),
                  pl.BlockSpec((D_MODEL, nb), lambda j: (0, j)),
                  pl.BlockSpec((1, nb), lambda j: (0, j))],
        out_specs=pl.BlockSpec((8, nb), lambda j: (0, j)),
        out_shape=jax.ShapeDtypeStruct((8, 6 * D_MODEL), F32),
        compiler_params=_cparams(("arbitrary",)),
        name="modulation",
    )(cvec.T, w_mod, b_mod[None, :])
    return out.reshape(8, 6, D_MODEL)


def _ctx_or_lat(ctx_ref, lat_ref):
    return jnp.where(pl.program_id(0) < N_CTX_BLOCKS, ctx_ref[...], lat_ref[...])


def _split_specs(width):
    return [pl.BlockSpec((TB, width), lambda i: (jnp.minimum(i, N_CTX_BLOCKS - 1), 0)),
            pl.BlockSpec((TB, width), lambda i: (jnp.maximum(i - N_CTX_BLOCKS, 0), 0))]


def _inproj_kernel(xc_ref, xl_ref, mod_ref, nw_ref, w_ref, wgt_ref,
                   qa_ref, ka_ref, va_ref, qm_ref, km_ref, vm_ref, om_ref, gt_ref):
    x = _ctx_or_lat(xc_ref, xl_ref)
    y = x * lax.rsqrt(jnp.mean(x * x, axis=-1, keepdims=True) + EPS) * nw_ref[...]
    hn = y * (1.0 + mod_ref[1:2, :]) + mod_ref[0:1, :]
    hb = hn.astype(BF16)
    qa_ref[...] = _dot(hb, w_ref[:, 0:512])
    ka_ref[...] = _dot(hb, w_ref[:, 512:640])
    va_ref[...] = _dot(hb, w_ref[:, 640:768])
    qm_ref[...] = _dot(hb, w_ref[:, 768:1280]).astype(BF16)
    km_ref[...] = (_dot(hb, w_ref[:, 1280:1792]) * (DH_M ** -0.5)).astype(BF16)
    vm_ref[...] = _dot(hb, w_ref[:, 1792:2304]).astype(BF16)
    om_ref[...] = _dot(hb, w_ref[:, 2304:2816])
    gt_ref[...] = lax.dot_general(wgt_ref[...], hb, _NT, preferred_element_type=F32)


def _inproj(x_ctx, x_lat, mod3, norm1_w, w_main, w_gates_t):
    tok = lambda w: pl.BlockSpec((TB, w), lambda i: (i, 0))
    full = lambda a: pl.BlockSpec(a.shape, lambda i: (0,) * a.ndim)
    sd = lambda w, dt: jax.ShapeDtypeStruct((T_ALL, w), dt)
    return pl.pallas_call(
        _inproj_kernel,
        grid=(T_ALL // TB,),
        in_specs=_split_specs(D_MODEL) + [
                  pl.BlockSpec((None, 6, D_MODEL), lambda i: (_mod_row(i), 0, 0)),
                  full(norm1_w), full(w_main), full(w_gates_t)],
        out_specs=[tok(512), tok(128), tok(128), tok(512), tok(512), tok(512), tok(512),
                   pl.BlockSpec((16, TB), lambda i: (0, i))],
        out_shape=[sd(512, F32), sd(128, F32), sd(128, F32), sd(512, BF16), sd(512, BF16),
                   sd(512, BF16), sd(512, F32), jax.ShapeDtypeStruct((16, T_ALL), F32)],
        compiler_params=_cparams(("arbitrary",)),
        name="inproj",
    )(x_ctx, x_lat, mod3, norm1_w, w_main, w_gates_t)


def _head_norm(x, ind, ind_t, w_row):
    ss = _dot3(x * x, ind)
    inv = lax.rsqrt(ss * (1.0 / DH_ATT) + EPS)
    return x * _dot3(inv, ind_t) * w_row


def _rope(x, cos, sin_signed):
    lane = lax.broadcasted_iota(jnp.int32, x.shape, 1)
    partner = jnp.where((lane % 32) < 16, pltpu.roll(x, 128 - 16, 1), pltpu.roll(x, 16, 1))
    return x * cos + partner * sin_signed


def _qkprep_kernel(*refs, rope):
    if rope:
        qa_ref, ka_ref, qw_ref, kw_ref, iq_ref, iqt_ref, ik_ref, ikt_ref, cos_ref, sin_ref, qn_ref, kn_ref = refs
    else:
        qa_ref, ka_ref, qw_ref, kw_ref, iq_ref, iqt_ref, ik_ref, ikt_ref, qn_ref, kn_ref = refs
    qn = _head_norm(qa_ref[...], iq_ref[...], iqt_ref[...], qw_ref[...])
    kn = _head_norm(ka_ref[...], ik_ref[...], ikt_ref[...], kw_ref[...])
    if rope:
        cos, sin = cos_ref[...], sin_ref[...]
        qn = jnp.concatenate([_rope(qn[:, 128 * j:128 * (j + 1)], cos, sin) for j in range(4)], axis=1)
        kn = _rope(kn, cos, sin)
    qn_ref[...] = qn.astype(BF16)
    kn_ref[...] = kn


def _qkprep(qa, ka, qw, kw, inds, rope_tabs, *, first_block, n_blocks):
    full = lambda a: pl.BlockSpec(a.shape, lambda i: (0,) * a.ndim)
    tok = lambda w: pl.BlockSpec((TB, w), lambda i: (i + first_block, 0))
    in_specs = [tok(512), tok(128), full(qw), full(kw)] + [full(a) for a in inds]
    args = [qa, ka, qw, kw] + list(inds)
    if rope_tabs is not None:
        in_specs += [pl.BlockSpec((TB, 128), lambda i: (i % LAT_BLOCKS_PER_BATCH, 0))] * 2
        args += list(rope_tabs)
    n_tok = n_blocks * TB
    return pl.pallas_call(
        functools.partial(_qkprep_kernel, rope=rope_tabs is not None),
        grid=(n_blocks,),
        in_specs=in_specs,
        out_specs=[pl.BlockSpec((TB, 512), lambda i: (i, 0)), pl.BlockSpec((TB, 128), lambda i: (i, 0))],
        out_shape=[jax.ShapeDtypeStruct((n_tok, 512), BF16), jax.ShapeDtypeStruct((n_tok, 128), F32)],
        compiler_params=_cparams(("arbitrary",)),
        name="qkprep_lat" if rope_tabs is not None else "qkprep_ctx",
    )(*args)


def _rope_tables():
    t = jnp.arange(DEC_SEQ)
    pos = jnp.stack([t // GRID_W, t % GRID_W], axis=1).astype(F32)
    n_freq = DH_ATT // 4
    inv_freq = ROPE_THETA ** (-jnp.arange(n_freq, dtype=F32) / n_freq)
    ang = pos[:, :, None] * inv_freq
    cos, sin = jnp.cos(ang), jnp.sin(ang)
    cos_h = jnp.stack([cos, cos], axis=2).reshape(DEC_SEQ, DH_ATT)
    sin_h = jnp.stack([-sin, sin], axis=2).reshape(DEC_SEQ, DH_ATT)
    return jnp.tile(cos_h, (1, 2)), jnp.tile(sin_h, (1, 2))


def _attn_kernel(*refs, has_cache):
    if has_cache:
        q_ref, k_ref, v_ref, kc_ref, vc_ref, o_ref = refs
    else:
        q_ref, k_ref, v_ref, o_ref = refs
    q = q_ref[...]
    k = k_ref[...].astype(BF16)
    v = v_ref[...].astype(BF16)
    qb = q.shape[0]
    scale = DH_ATT ** -0.5
    outs = []
    for g in range(N_KV_HEADS):
        kg = k[:, DH_ATT * g:DH_ATT * (g + 1)]
        vg = v[:, DH_ATT * g:DH_ATT * (g + 1)]
        qg = jnp.concatenate([q[:, 256 * g + DH_ATT * j:256 * g + DH_ATT * (j + 1)] for j in range(4)], axis=0)
        s = lax.dot_general(qg, kg, _NT, preferred_element_type=F32) * scale
        m = jnp.max(s, axis=-1, keepdims=True)
        if has_cache:
            kc = kc_ref[g].astype(BF16)
            vc = vc_ref[g].astype(BF16)
            sc = lax.dot_general(qg, kc, _NT, preferred_element_type=F32) * scale
            m = jnp.maximum(m, jnp.max(sc, axis=-1, keepdims=True))
        p = jnp.exp(s - m)
        den = jnp.sum(p, axis=-1, keepdims=True)
        o = _dot(p.astype(BF16), vg)
        if has_cache:
            pc = jnp.exp(sc - m)
            den = den + jnp.sum(pc, axis=-1, keepdims=True)
            o = o + _dot(pc.astype(BF16), vc)
        o = o / den
        outs += [o[qb * j:qb * (j + 1), :] for j in range(4)]
    o_ref[...] = jnp.concatenate(outs, axis=1).astype(BF16)


def _attention(qn, kn, v_all, cache, *, n_batch, seq, v_first_block):
    qblocks = seq // TB
    kblk = lambda off: pl.BlockSpec((seq, 128), lambda b, i: (b + off, 0))
    in_specs = [pl.BlockSpec((TB, 512), lambda b, i: (b * qblocks + i, 0)), kblk(0), kblk(v_first_block)]
    args = [qn, kn, v_all]
    if cache is not None:
        cspec = pl.BlockSpec((None, None, N_KV_HEADS, PAST_LEN, DH_ATT), lambda b, i: (b, 0, 0, 0, 0))
        in_specs += [cspec, cspec]
        args += list(cache)
    return pl.pallas_call(
        functools.partial(_attn_kernel, has_cache=cache is not None),
        grid=(n_batch, qblocks),
        in_specs=in_specs,
        out_specs=pl.BlockSpec((TB, 512), lambda b, i: (b * qblocks + i, 0)),
        out_shape=jax.ShapeDtypeStruct((n_batch * seq, 512), BF16),
        compiler_params=_cparams(("arbitrary", "arbitrary")),
        name="attention_lat" if cache is not None else "attention_ctx",
    )(*args)


def _log_sigmoid(x):
    return jnp.minimum(x, 0.0) - jnp.log1p(jnp.exp(-jnp.abs(x)))


def _lane_scan(x, op, fill, reverse):
    lane = lax.broadcasted_iota(jnp.int32, x.shape, 1)
    for sh in (1, 2, 4, 8, 16, 32, 64):
        if reverse:
            shifted = jnp.where(lane < CHUNK - sh, pltpu.roll(x, CHUNK - sh, 1), fill)
        else:
            shifted = jnp.where(lane >= sh, pltpu.roll(x, sh, 1), fill)
        x = op(x, shifted)
    return x


def _mlstm_kernel(*refs, has_state, n_chunks):
    n_in = 10 if has_state else 7
    n_out = 1 if has_state else 4
    q_ref, k_ref, v_ref, g_ref, bg_ref, om_ref, nw_ref = refs[:7]
    hm_ref = refs[n_in]
    scratch = refs[n_in + n_out:]
    cs = scratch[0:8]
    nm = scratch[8:16]
    hdir = scratch[16:18]

    for d in range(2):
        for h in range(N_HEADS_M):
            idx = 4 * d + h
            if has_state:
                c0_ref, n0_ref, m0_ref = refs[7:10]
                cs[idx][...] = c0_ref[d, h]
                nm[idx][0:1, :] = n0_ref[d, h:h + 1, :]
                nm[idx][1:2, :] = m0_ref[idx:idx + 1, :]
            else:
                cs[idx][...] = jnp.zeros((DH_M, DH_M), F32)
                nm[idx][0:1, :] = jnp.zeros((1, DH_M), F32)
                nm[idx][1:2, :] = jnp.full((1, CHUNK), NEG_INIT, F32)

    sub = lax.broadcasted_iota(jnp.int32, (CHUNK, CHUNK), 0)
    lan = lax.broadcasted_iota(jnp.int32, (CHUNK, CHUNK), 1)
    neg_inf = F32(-jnp.inf)

    def chunk_step(i, carry):
        for d in range(2):
            c = i if d == 0 else n_chunks - 1 - i
            r0 = pl.multiple_of(c * CHUNK, CHUNK)
            pre = g_ref[c] + bg_ref[...]
            bcum = _lane_scan(_log_sigmoid(pre), jnp.add, 0.0, d == 1)
            li4 = pre[4 * d:4 * d + 4, :]
            b4 = bcum[8 + 4 * d:12 + 4 * d, :]
            r4 = li4 - b4
            cm4 = _lane_scan(r4, jnp.maximum, neg_inf, d == 1)
            mp4 = jnp.concatenate([nm[4 * d + h][1:2, :] for h in range(N_HEADS_M)], axis=0)
            g4 = b4 + mp4
            mrow4 = jnp.maximum(g4, b4 + cm4)
            blast4 = b4[:, CHUNK - 1:CHUNK] if d == 0 else b4[:, 0:1]
            wlog4 = blast4 + r4
            gend4 = blast4 + mp4
            mnew4 = jnp.maximum(gend4, jnp.max(wlog4, axis=-1, keepdims=True))
            decay4 = jnp.exp(gend4 - mnew4)
            rows = jnp.concatenate([b4 - mrow4, jnp.exp(g4 - mrow4), jnp.exp(-mrow4), jnp.exp(wlog4 - mnew4),
                                    jnp.zeros((CHUNK - 16, CHUNK), F32)], axis=0)
            cols = rows.T
            mask = (lan <= sub) if d == 0 else (lan >= sub)
            for h in range(N_HEADS_M):
                idx = 4 * d + h
                hs = slice(DH_M * h, DH_M * (h + 1))
                qc = q_ref[pl.ds(r0, CHUNK), hs]
                kc = k_ref[pl.ds(r0, CHUNK), hs]
                vc = v_ref[pl.ds(r0, CHUNK), hs]
                a_col = cols[:, h:h + 1]
                gw_col = cols[:, 4 + h:5 + h]
                en_col = cols[:, 8 + h:9 + h]
                w_col = cols[:, 12 + h:13 + h]
                decay = jnp.exp(jnp.where(mask, a_col + r4[h:h + 1, :], neg_inf))
                s = lax.dot_general(qc, kc, _NT, preferred_element_type=F32) * decay
                c_old = cs[idx][...]
                n_old = nm[idx][0:1, :]
                num = _dot(s.astype(BF16), vc) + gw_col * _dot(qc, c_old.astype(BF16))
                den = (jnp.sum(s, axis=-1, keepdims=True)
                       + gw_col * jnp.sum(qc.astype(F32) * n_old, axis=-1, keepdims=True))
                hdir[d][pl.ds(r0, CHUNK), hs] = num / jnp.maximum(jnp.abs(den), en_col)
                kw = kc.astype(F32) * w_col
                dec = decay4[h:h + 1, :]
                cs[idx][...] = dec * c_old + lax.dot_general(kw.astype(BF16), vc, _TN,
                                                             preferred_element_type=F32)
                nm[idx][0:1, :] = dec * n_old + jnp.sum(kw, axis=0, keepdims=True)
                nm[idx][1:2, :] = mnew4[h:h + 1, :]
        return carry

    lax.fori_loop(0, n_chunks, chunk_step, 0)

    for h in range(N_HEADS_M):
        hs = slice(DH_M * h, DH_M * (h + 1))
        hh = hdir[0][:, hs] + hdir[1][:, hs]
        y = hh * lax.rsqrt(jnp.mean(hh * hh, axis=-1, keepdims=True) + EPS) * nw_ref[:, hs]
        hm_ref[:, hs] = (jax.nn.sigmoid(om_ref[:, hs]) * y).astype(BF16)
    if not has_state:
        c_ref, n_ref, m_ref = refs[n_in + 1:n_in + 4]
        for d in range(2):
            for h in range(N_HEADS_M):
                idx = 4 * d + h
                c_ref[d, h] = cs[idx][...]
                n_ref[d, h:h + 1, :] = nm[idx][0:1, :]
                m_ref[idx:idx + 1, :] = nm[idx][1:2, :]


def _mlstm(qm, km, vm, g3, bg, om, nw, state, *, n_batch, seq, first_block):
    n_chunks = seq // CHUNK
    rb = seq // TB
    tok = lambda: pl.BlockSpec((seq, 512), lambda b: (b + first_block // rb, 0))
    full = lambda a: pl.BlockSpec(a.shape, lambda b: (0,) * a.ndim)
    in_specs = [tok(), tok(), tok(),
                pl.BlockSpec((n_chunks, 16, CHUNK), lambda b: (b + first_block // rb, 0, 0)),
                full(bg), tok(), full(nw)]
    args = [qm, km, vm, g3, bg, om, nw]
    hm_spec = pl.BlockSpec((seq, 512), lambda b: (b, 0))
    hm_shape = jax.ShapeDtypeStruct((n_batch * seq, 512), BF16)
    scratch = ([pltpu.VMEM((DH_M, DH_M), F32)] * 8 + [pltpu.VMEM((8, DH_M), F32)] * 8
               + [pltpu.VMEM((seq, 512), F32)] * 2)
    if state is not None:
        c0, n0, m0 = state
        in_specs += [pl.BlockSpec((None, None, 2, N_HEADS_M, DH_M, DH_M), lambda b: (b, 0, 0, 0, 0, 0)),
                     pl.BlockSpec((None, None, 2, N_HEADS_M, DH_M), lambda b: (b, 0, 0, 0, 0)),
                     pl.BlockSpec((None, 8, CHUNK), lambda b: (b, 0, 0))]
        args += [c0, n0, m0]
        out_specs, out_shape = hm_spec, hm_shape
    else:
        out_specs = [hm_spec,
                     pl.BlockSpec((None, 2, N_HEADS_M, DH_M, DH_M), lambda b: (b, 0, 0, 0, 0)),
                     pl.BlockSpec((None, 2, N_HEADS_M, DH_M), lambda b: (b, 0, 0, 0)),
                     pl.BlockSpec((None, 8, CHUNK), lambda b: (b, 0, 0))]
        out_shape = [hm_shape,
                     jax.ShapeDtypeStruct((n_batch, 2, N_HEADS_M, DH_M, DH_M), F32),
                     jax.ShapeDtypeStruct((n_batch, 2, N_HEADS_M, DH_M), F32),
                     jax.ShapeDtypeStruct((n_batch, 8, CHUNK), F32)]
    return pl.pallas_call(
        functools.partial(_mlstm_kernel, has_state=state is not None, n_chunks=n_chunks),
        grid=(n_batch,),
        in_specs=in_specs,
        out_specs=out_specs,
        out_shape=out_shape,
        scratch_shapes=scratch,
        compiler_params=_cparams(("arbitrary",)),
        name="mlstm_lat" if state is not None else "mlstm_ctx",
    )(*args)


def _outproj_kernel(attc_ref, attl_ref, hmc_ref, hml_ref, xc_ref, xl_ref, mod_ref, nw_ref, wo_ref,
                    rwh_ref, rwl_ref, wsg_ref, wsu_ref, wsd_ref, base_ref, hn_ref, lt_ref):
    y = (_dot(_ctx_or_lat(attc_ref, attl_ref), wo_ref[0:D_ATT, :])
         + _dot(_ctx_or_lat(hmc_ref, hml_ref), wo_ref[D_ATT:, :]))
    x1 = _ctx_or_lat(xc_ref, xl_ref) + mod_ref[2:3, :] * y
    z = x1 * lax.rsqrt(jnp.mean(x1 * x1, axis=-1, keepdims=True) + EPS) * nw_ref[...]
    hn = z * (1.0 + mod_ref[4:5, :]) + mod_ref[3:4, :]
    hb = hn.astype(BF16)
    hl = (hn - hb.astype(F32)).astype(BF16)
    nt = lambda w, t: lax.dot_general(w, t, _NT, preferred_element_type=F32)
    lt_ref[...] = nt(rwh_ref[...], hb) + nt(rwl_ref[...], hb) + nt(rwh_ref[...], hl)
    a = _silu(_dot(hb, wsg_ref[...])) * _dot(hb, wsu_ref[...])
    shared = _dot(a.astype(BF16), wsd_ref[...])
    base_ref[...] = x1 + mod_ref[5:6, :] * shared
    packed = pltpu.pack_elementwise([hn[:, :512], hn[:, 512:]], packed_dtype=BF16)
    for c in range(4):
        hn_ref[pl.ds(c, TB, stride=4), :] = packed[:, 128 * c:128 * (c + 1)]


def _outproj(att, hm, x, mod3, norm2_w, w_out, rw_hi, rw_lo, wsg, wsu, wsd):
    tok = lambda w: pl.BlockSpec((TB, w), lambda i: (i, 0))
    full = lambda a: pl.BlockSpec(a.shape, lambda i: (0,) * a.ndim)
    return pl.pallas_call(
        _outproj_kernel,
        grid=(T_ALL // TB,),
        in_specs=_split_specs(512) + _split_specs(512) + _split_specs(D_MODEL) + [
                  pl.BlockSpec((None, 6, D_MODEL), lambda i: (_mod_row(i), 0, 0)),
                  full(norm2_w), full(w_out), full(rw_hi), full(rw_lo), full(wsg), full(wsu), full(wsd)],
        out_specs=[tok(D_MODEL), pl.BlockSpec((4 * TB, 128), lambda i: (i, 0)),
                   pl.BlockSpec((N_EXPERTS, TB), lambda i: (0, i))],
        out_shape=[jax.ShapeDtypeStruct((T_ALL, D_MODEL), F32),
                   jax.ShapeDtypeStruct((4 * T_ALL, 128), jnp.uint32),
                   jax.ShapeDtypeStruct((N_EXPERTS, T_ALL), F32)],
        compiler_params=_cparams(("arbitrary",)),
        name="outproj",
    )(*att, *hm, *x, mod3, norm2_w, w_out, rw_hi, rw_lo, wsg, wsu, wsd)


def _first_max(vals, ids, limit):
    m = functools.reduce(jnp.maximum, [jnp.max(v, axis=0, keepdims=True) for v in vals])
    cand = [jnp.min(jnp.where(v == m, i, limit), axis=0, keepdims=True) for v, i in zip(vals, ids)]
    return m, functools.reduce(jnp.minimum, cand)


def _router_kernel(lt_ref, bias_ref, gate_ref, idx_ref):
    n = lt_ref.shape[1]
    score = jax.nn.sigmoid(lt_ref[...])
    biased = score + bias_ref[...]
    sub = lax.broadcasted_iota(jnp.int32, (8, n), 0).astype(F32)
    neg_inf = F32(-jnp.inf)
    slabs = [biased[8 * g:8 * (g + 1), :] for g in range(N_GROUPS)]
    gs = []
    for sl in slabs:
        m1, i1 = _first_max([sl], [sub], 8.0)
        m2 = jnp.max(jnp.where(sub == i1, neg_inf, sl), axis=0, keepdims=True)
        gs.append(m1 + m2)
    cur = jnp.concatenate(gs, axis=0)
    gsel = jnp.zeros((8, n), F32)
    for _ in range(TOPK_GROUPS):
        _, i = _first_max([cur], [sub], 8.0)
        hit = sub == i
        gsel = jnp.where(hit, 1.0, gsel)
        cur = jnp.where(hit, neg_inf, cur)
    vals = [jnp.where(gsel[g:g + 1, :] > 0.0, slabs[g], neg_inf) for g in range(N_GROUPS)]
    ids = [sub + 8.0 * g for g in range(N_GROUPS)]
    picked = [jnp.zeros((8, n), F32) for _ in range(N_GROUPS)]
    order = []
    for _ in range(TOP_K):
        _, i = _first_max(vals, ids, float(N_EXPERTS))
        order.append(i)
        hits = [idg == i for idg in ids]
        picked = [jnp.where(hh, score[8 * g:8 * (g + 1), :], p) for g, (p, hh) in enumerate(zip(picked, hits))]
        vals = [jnp.where(hh, neg_inf, v) for v, hh in zip(vals, hits)]
    total = functools.reduce(jnp.add, [jnp.sum(p, axis=0, keepdims=True) for p in picked])
    gate_t = jnp.concatenate([p / total * ROUTED_SCALE for p in picked]
                             + [jnp.zeros((128 - N_EXPERTS, n), F32)], axis=0)
    gate_ref[...] = gate_t.T
    idx_ref[...] = jnp.concatenate(order, axis=0).astype(jnp.int32)


def _router(logits_t, bias_col):
    return pl.pallas_call(
        _router_kernel,
        grid=(T_ALL // TB_MOE,),
        in_specs=[pl.BlockSpec((N_EXPERTS, TB_MOE), lambda i: (0, i)),
                  pl.BlockSpec((N_EXPERTS, 1), lambda i: (0, 0))],
        out_specs=[pl.BlockSpec((TB_MOE, 128), lambda i: (i, 0)),
                   pl.BlockSpec((TOP_K, TB_MOE), lambda i: (0, i))],
        out_shape=[jax.ShapeDtypeStruct((T_ALL, 128), F32),
                   jax.ShapeDtypeStruct((TOP_K, T_ALL), jnp.int32)],
        compiler_params=_cparams(("arbitrary",)),
        name="router",
    )(logits_t, bias_col)


def _plan_kernel(idx_ref, slot_ref, cnt_ref, off_ref, pos_sc):
    n_tiles = T_HALF // 128
    eid = lax.broadcasted_iota(jnp.int32, (N_EXPERTS, 128), 0)
    tri = (lax.broadcasted_iota(jnp.int32, (128, 128), 0)
           <= lax.broadcasted_iota(jnp.int32, (128, 128), 1)).astype(BF16)
    carry = jnp.zeros((N_EXPERTS, 1), F32)
    for j in range(n_tiles):
        it = idx_ref[:, 128 * j:128 * (j + 1)]
        sel = jnp.zeros((N_EXPERTS, 128), F32)
        for k in range(TOP_K):
            sel = jnp.where(it[k:k + 1, :] == eid, 1.0, sel)
        inc = _dot(sel.astype(BF16), tri) + carry
        carry = inc[:, 127:128]
        pos_sc[:, 128 * j:128 * (j + 1)] = inc - 1.0
    cnt8 = jnp.broadcast_to(jnp.floor((carry + (ROW_UNROLL - 1.0)) * (1.0 / ROW_UNROLL)) * ROW_UNROLL,
                            (N_EXPERTS, 128))
    before = (lax.broadcasted_iota(jnp.int32, (N_EXPERTS, N_EXPERTS), 1)
              < lax.broadcasted_iota(jnp.int32, (N_EXPERTS, N_EXPERTS), 0)).astype(BF16)
    hi, mid, lo = _split3(cnt8)
    off = _dot(before, hi) + _dot(before, mid) + _dot(before, lo)
    cnt_ref[...] = cnt8.astype(jnp.int32)
    off_ref[...] = off.astype(jnp.int32)

    def slots(j):
        it = idx_ref[:, 128 * j:128 * (j + 1)]
        val = off + pos_sc[:, 128 * j:128 * (j + 1)]
        rows = [jnp.sum(jnp.where(it[k:k + 1, :] == eid, val, 0.0), axis=0, keepdims=True)
                for k in range(TOP_K)]
        return jnp.concatenate(rows, axis=0).astype(jnp.int32)

    for j in range(n_tiles // 2):
        slot_ref[:, 128 * j:128 * (j + 1)] = slots(j) | (slots(j + n_tiles // 2) << 16)


def _plan(idx8):
    return pl.pallas_call(
        _plan_kernel,
        grid=(2,),
        in_specs=[pl.BlockSpec((TOP_K, T_HALF), lambda h: (0, h))],
        out_specs=[pl.BlockSpec((None, TOP_K, T_HALF // 2), lambda h: (h, 0, 0)),
                   pl.BlockSpec((None, N_EXPERTS, 128), lambda h: (h, 0, 0)),
                   pl.BlockSpec((None, N_EXPERTS, 128), lambda h: (h, 0, 0))],
        out_shape=[jax.ShapeDtypeStruct((2, TOP_K, T_HALF // 2), jnp.int32),
                   jax.ShapeDtypeStruct((2, N_EXPERTS, 128), jnp.int32),
                   jax.ShapeDtypeStruct((2, N_EXPERTS, 128), jnp.int32)],
        scratch_shapes=[pltpu.VMEM((N_EXPERTS, T_HALF), F32)],
        compiler_params=_cparams(("arbitrary",)),
        name="plan",
    )(idx8)


def _invert_slots(slot_ref, cnt_ref, off_ref, tok_ref, h):
    def pad(x, carry):
        n = cnt_ref[h * N_EXPERTS + x]
        last = off_ref[h * N_EXPERTS + x] + n - ROW_UNROLL

        @pl.when(n > 0)
        def _():
            for u in range(ROW_UNROLL):
                tok_ref[last + u] = T_HALF
        return carry

    lax.fori_loop(0, N_EXPERTS, pad, 0)
    half = T_HALF // 2
    for k in range(TOP_K):
        def body(i, carry, k=k):
            words = [slot_ref[(h * TOP_K + k) * half + i * INV_UNROLL + u] for u in range(INV_UNROLL)]
            for u in range(INV_UNROLL):
                t = i * INV_UNROLL + u
                tok_ref[words[u] & 0xFFFF] = t
                tok_ref[lax.shift_right_logical(words[u], 16)] = t + half
            return carry

        lax.fori_loop(0, half // INV_UNROLL, body, 0)


def _moe_kernel(slot_ref, cnt_ref, off_ref, src_hbm, gate_hbm, wg_ref, wu_ref, wd_ref, out_hbm,
                src_v, gate_v, acc_v, wgu_b, wd_b, xbuf, gbuf, ybuf, tok_ref, sem):
    h = pl.program_id(0)
    e = pl.program_id(1)

    @pl.when(e == 0)
    def _():
        src_cp = pltpu.make_async_copy(src_hbm.at[pl.ds(pl.multiple_of(h * (4 * T_HALF), 8), 4 * T_HALF)],
                                       src_v.at[pl.ds(0, 4 * T_HALF)], sem.at[0])
        gate_cp = pltpu.make_async_copy(gate_hbm.at[pl.ds(pl.multiple_of(h * T_HALF, 8), T_HALF)],
                                        gate_v.at[pl.ds(0, T_HALF)], sem.at[1])
        src_cp.start()
        gate_cp.start()
        acc_v[...] = jnp.zeros(acc_v.shape, F32)
        zero = jnp.zeros((8, 128), F32)
        zero_words = pltpu.pack_elementwise([zero, zero], packed_dtype=BF16)
        src_v[pl.ds(4 * T_HALF, 8), :] = zero_words
        gate_v[pl.ds(T_HALF, 8), :] = zero
        zero_tile = jnp.zeros(xbuf.shape, F32)
        xbuf[...] = pltpu.pack_elementwise([zero_tile, zero_tile], packed_dtype=BF16)
        gbuf[...] = jnp.zeros(gbuf.shape, F32)
        _invert_slots(slot_ref, cnt_ref, off_ref, tok_ref, h)
        src_cp.wait()
        gate_cp.wait()

    wgu_b[:, 0:D_EXPERT] = wg_ref[...].astype(BF16)
    wgu_b[:, D_EXPERT:] = wu_ref[...].astype(BF16)
    wd_b[...] = wd_ref[...].astype(BF16)
    n_rows = cnt_ref[h * N_EXPERTS + e]
    first = off_ref[h * N_EXPERTS + e]
    pick = (lax.broadcasted_iota(jnp.int32, (128, D_EXPERT), 0) == e).astype(BF16)

    def tile(j, carry):
        base = first + j * MT
        groups = (jnp.minimum(n_rows - j * MT, MT) + ROW_UNROLL - 1) // ROW_UNROLL

        def gather(i, c):
            toks = [tok_ref[base + i * ROW_UNROLL + u] for u in range(ROW_UNROLL)]
            for u in range(ROW_UNROLL):
                m = i * ROW_UNROLL + u
                t = toks[u]
                xbuf[pl.ds(m, 4, stride=MT_STRIDE), :] = src_v[pl.ds(pl.multiple_of(t * 4, 4), 4), :]
                gbuf[pl.ds(m, 1), :] = gate_v[pl.ds(t, 1), :]
            return c

        lax.fori_loop(0, groups, gather, 0)

        lo, hi = [], []
        for c in range(4):
            words = xbuf[MT_STRIDE * c:MT_STRIDE * c + MT, :]
            unpack = functools.partial(pltpu.unpack_elementwise, words, packed_dtype=BF16, unpacked_dtype=F32)
            lo.append(unpack(index=0).astype(BF16))
            hi.append(unpack(index=1).astype(BF16))
        x = jnp.concatenate(lo + hi, axis=1)
        hgu = _dot(x, wgu_b[...])
        g = gbuf[...]
        g_hi = g.astype(BF16)
        g_lo = (g - g_hi.astype(F32)).astype(BF16)
        g_col = _dot(g_hi, pick) + _dot(g_lo, pick)
        a = _silu(hgu[:, :D_EXPERT]) * hgu[:, D_EXPERT:] * g_col
        y = _dot(a.astype(BF16), wd_b[...])
        for c in range(8):
            ybuf[MT_STRIDE * c:MT_STRIDE * c + MT, :] = y[:, 128 * c:128 * (c + 1)]

        def scatter(i, c):
            rows = [pl.ds(pl.multiple_of(tok_ref[base + i * ROW_UNROLL + u] * 8, 8), 8)
                    for u in range(ROW_UNROLL)]
            vals = [acc_v[rows[u], :] + ybuf[pl.ds(i * ROW_UNROLL + u, 8, stride=MT_STRIDE), :]
                    for u in range(ROW_UNROLL)]
            for u in range(ROW_UNROLL):
                acc_v[rows[u], :] = vals[u]
            return c

        lax.fori_loop(0, groups, scatter, 0)
        return carry

    lax.fori_loop(0, (n_rows + MT - 1) // MT, tile, 0)

    @pl.when(e == N_EXPERTS - 1)
    def _():
        out_cp = pltpu.make_async_copy(acc_v.at[pl.ds(0, 8 * T_HALF)],
                                       out_hbm.at[pl.ds(pl.multiple_of(h * (8 * T_HALF), 8), 8 * T_HALF)],
                                       sem.at[2])
        out_cp.start()
        out_cp.wait()


def _moe(slot_words, cnt, off, src, gate, wg, wu, wd):
    wspec = lambda shape: pl.BlockSpec((None,) + shape, lambda h, e, *_: (e, 0, 0))
    any_spec = pl.BlockSpec(memory_space=pl.ANY)
    return pl.pallas_call(
        _moe_kernel,
        grid_spec=pltpu.PrefetchScalarGridSpec(
            num_scalar_prefetch=3,
            grid=(2, N_EXPERTS),
            in_specs=[any_spec, any_spec, wspec((D_MODEL, D_EXPERT)), wspec((D_MODEL, D_EXPERT)),
                      wspec((D_EXPERT, D_MODEL))],
            out_specs=any_spec,
            scratch_shapes=[pltpu.VMEM((4 * T_HALF + 8, 128), jnp.uint32),
                            pltpu.VMEM((T_HALF + 8, 128), F32),
                            pltpu.VMEM((8 * T_HALF + 8, 128), F32),
                            pltpu.VMEM((D_MODEL, 2 * D_EXPERT), BF16),
                            pltpu.VMEM((D_EXPERT, D_MODEL), BF16),
                            pltpu.VMEM((4 * MT_STRIDE, 128), jnp.uint32),
                            pltpu.VMEM((MT, 128), F32),
                            pltpu.VMEM((8 * MT_STRIDE, 128), F32),
                            pltpu.SMEM((SLOT_CAP,), jnp.int32),
                            pltpu.SemaphoreType.DMA((3,))]),
        out_shape=jax.ShapeDtypeStruct((8 * T_ALL, 128), F32),
        compiler_params=_cparams(("arbitrary", "arbitrary")),
        name="moe",
    )(slot_words, cnt, off, src, gate, wg, wu, wd)


def _finalize_kernel(acc_ref, base_ref, mod_ref, oc_ref, ol_ref):
    def write(o_ref):
        for c in range(8):
            cs = slice(128 * c, 128 * (c + 1))
            o_ref[:, cs] = base_ref[:, cs] + mod_ref[5:6, cs] * acc_ref[pl.ds(c, TB, stride=8), :]

    is_ctx = pl.program_id(0) < N_CTX_BLOCKS
    pl.when(is_ctx)(lambda: write(oc_ref))
    pl.when(jnp.logical_not(is_ctx))(lambda: write(ol_ref))


def _finalize(acc, base, mod3):
    return pl.pallas_call(
        _finalize_kernel,
        grid=(T_ALL // TB,),
        in_specs=[pl.BlockSpec((8 * TB, 128), lambda i: (i, 0)),
                  pl.BlockSpec((TB, D_MODEL), lambda i: (i, 0)),
                  pl.BlockSpec((None, 6, D_MODEL), lambda i: (_mod_row(i), 0, 0))],
        out_specs=_split_specs(D_MODEL),
        out_shape=[jax.ShapeDtypeStruct((T_CTX, D_MODEL), F32), jax.ShapeDtypeStruct((T_LAT, D_MODEL), F32)],
        compiler_params=_cparams(("arbitrary",)),
        name="finalize",
    )(acc, base, mod3)


def _head_indicators(width):
    head = jnp.arange(width) // DH_ATT
    ind = (head[:, None] == jnp.arange(128)[None, :]).astype(BF16)
    return ind, ind.T


def kernel(x_prompt, x_sample, cache_attn_k, cache_attn_v, state_mlstm_c, state_mlstm_n, state_mlstm_m, c, c_ctx, w_mod, b_mod, norm1_w, norm2_w, w_in, q_norm_w, k_norm_w, b_gates, m_norm_w, w_out, router_w, router_bias, w_gate, w_up, w_down, ws_gate, ws_up, ws_down):
    x = (x_prompt.reshape(T_CTX, D_MODEL), x_sample.reshape(T_LAT, D_MODEL))
    mod3 = _modulation(c, c_ctx, w_mod[0], b_mod[0])

    w_main = w_in[0, :, :P_MAIN].astype(BF16)
    w_gates_t = w_in[0, :, P_MAIN:].T.astype(BF16)
    qa, ka, va, qm, km, vm, om, gt = _inproj(*x, mod3, norm1_w, w_main, w_gates_t)

    qw = jnp.tile(q_norm_w, (1, N_HEADS_ATT))
    kw = jnp.tile(k_norm_w, (1, N_KV_HEADS))
    inds = _head_indicators(512) + _head_indicators(128)
    qn_c, kn_c = _qkprep(qa, ka, qw, kw, inds, None, first_block=0, n_blocks=N_CTX_BLOCKS)
    qn_l, kn_l = _qkprep(qa, ka, qw, kw, inds, _rope_tables(), first_block=N_CTX_BLOCKS,
                         n_blocks=T_LAT // TB)
    att_c = _attention(qn_c, kn_c, va, None, n_batch=BATCH, seq=SEQ, v_first_block=0)
    att_l = _attention(qn_l, kn_l, va, (cache_attn_k, cache_attn_v), n_batch=DEC_BATCH, seq=DEC_SEQ,
                       v_first_block=T_CTX // DEC_SEQ)

    g3 = gt.reshape(16, T_ALL // CHUNK, CHUNK).transpose(1, 0, 2)
    bg = b_gates.reshape(16, 1)
    hm_c, c_new, n_new, m_new = _mlstm(qm, km, vm, g3, bg, om, m_norm_w, None,
                                       n_batch=BATCH, seq=SEQ, first_block=0)
    m0 = jnp.broadcast_to(state_mlstm_m.reshape(DEC_BATCH, 8, 1), (DEC_BATCH, 8, CHUNK))
    hm_l = _mlstm(qm, km, vm, g3, bg, om, m_norm_w, (state_mlstm_c, state_mlstm_n, m0),
                  n_batch=DEC_BATCH, seq=DEC_SEQ, first_block=N_CTX_BLOCKS)

    rw_t = router_w[0].T
    rw_hi = rw_t.astype(BF16)
    rw_lo = (rw_t - rw_hi.astype(F32)).astype(BF16)
    base, hn2, logits_t = _outproj((att_c, att_l), (hm_c, hm_l), x, mod3, norm2_w, w_out[0].astype(BF16),
                                   rw_hi, rw_lo, ws_gate[0].astype(BF16), ws_up[0].astype(BF16),
                                   ws_down[0].astype(BF16))
    gate, idx8 = _router(logits_t, router_bias.reshape(N_EXPERTS, 1))
    slot_words, cnt, off = _plan(idx8)
    acc = _moe(slot_words.reshape(TOP_K * T_HALF), cnt[:, :, 0].reshape(2 * N_EXPERTS),
               off[:, :, 0].reshape(2 * N_EXPERTS), hn2, gate, w_gate[0], w_up[0], w_down[0])
    out_c, out_l = _finalize(acc, base, mod3)

    y_prompt = out_c.reshape(BATCH, SEQ, D_MODEL)
    y_sample = out_l.reshape(DEC_BATCH, DEC_SEQ, D_MODEL)
    to_cache = lambda a: a[:T_CTX].reshape(BATCH, SEQ, N_KV_HEADS, DH_ATT).transpose(0, 2, 1, 3)[:, None]
    new_k = to_cache(kn_c)
    new_v = to_cache(va)
    new_m = m_new[:, :, 0].reshape(BATCH, 1, 2, N_HEADS_M)
    return (y_prompt, y_sample, new_k, new_v, c_new[:, None], n_new[:, None], new_m)
```

```python
import functools

import jax
import jax.numpy as jnp
from jax import lax
from jax.experimental import pallas as pl
from jax.experimental.pallas import tpu as pltpu

F32 = jnp.float32
BF16 = jnp.bfloat16

D_MODEL = 1024
BATCH = 32
SEQ = 256
DEC_BATCH = 2
DEC_SEQ = 1024
PAST_LEN = 256
GRID_W = 64
N_HEADS_ATT = 8
N_KV_HEADS = 2
DH_ATT = 64
D_ATT = 512
ROPE_THETA = 10000.0
N_HEADS_M = 4
DH_M = 128
D_M = 512
CHUNK = 128
N_EXPERTS = 64
TOP_K = 8
N_GROUPS = 8
TOPK_GROUPS = 4
D_EXPERT = 256
ROUTED_SCALE = 2.5
EPS = 1e-6
NEG_INIT = -1e30

T_CTX = BATCH * SEQ
T_LAT = DEC_BATCH * DEC_SEQ
T_ALL = T_CTX + T_LAT
TB = 256
N_CTX_BLOCKS = T_CTX // TB
LAT_BLOCKS_PER_BATCH = DEC_SEQ // TB
TB_MOE = 1024
T_HALF = T_ALL // 2
MT = 256
MT_STRIDE = MT + 8
NT_MAX = TOP_K * T_HALF // MT + N_EXPERTS
SLOT_CAP = (NT_MAX + 2) * MT
RMW_BATCH = 16
INV_UNROLL = 16
assert SLOT_CAP < 2 ** 16 and (T_HALF // 2) % INV_UNROLL == 0 and MT % RMW_BATCH == 0
P_MAIN = 2816
VMEM_LIMIT = 56 * 1024 * 1024

_NT = (((1,), (1,)), ((), ()))
_TN = (((0,), (0,)), ((), ()))


def _cparams(sem):
    return pltpu.CompilerParams(dimension_semantics=sem, vmem_limit_bytes=VMEM_LIMIT)


def _split3(x):
    hi = x.astype(BF16)
    r1 = x - hi.astype(F32)
    mid = r1.astype(BF16)
    lo = (r1 - mid.astype(F32)).astype(BF16)
    return hi, mid, lo


def _dot(a, b):
    return jnp.dot(a, b, preferred_element_type=F32)


def _dot3(x, m_bf16):
    hi, mid, lo = _split3(x)
    return _dot(hi, m_bf16) + _dot(mid, m_bf16) + _dot(lo, m_bf16)


def _silu(x):
    return x * jax.nn.sigmoid(x)


def _mod_row(i):
    return jnp.where(i < N_CTX_BLOCKS, 0, 1 + (i - N_CTX_BLOCKS) // LAT_BLOCKS_PER_BATCH)


def _mod_kernel(ct_ref, w_ref, b_ref, o_ref):
    s = _silu(ct_ref[...])
    w = w_ref[...]
    rows = [jnp.sum(w * s[:, r:r + 1], axis=0, keepdims=True) for r in range(3)]
    rows.append(jnp.zeros((5, w.shape[1]), F32))
    o_ref[...] = jnp.concatenate(rows, axis=0) + b_ref[...]


def _modulation(c, c_ctx, w_mod, b_mod):
    cvec = jnp.concatenate([c_ctx[None, :], c, jnp.zeros((5, D_MODEL), F32)], axis=0)
    nb = 1024
    out = pl.pallas_call(
        _mod_kernel,
        grid=(6 * D_MODEL // nb,),
        in_specs=[pl.BlockSpec((D_MODEL, 8), lambda j: (0, 0)),
                  pl.BlockSpec((D_MODEL, nb), lambda j: (0, j)),
                  pl.BlockSpec((1, nb), lambda j: (0, j))],
        out_specs=pl.BlockSpec((8, nb), lambda j: (0, j)),
        out_shape=jax.ShapeDtypeStruct((8, 6 * D_MODEL), F32),
        compiler_params=_cparams(("arbitrary",)),
        name="modulation",
    )(cvec.T, w_mod, b_mod[None, :])
    return out.reshape(8, 6, D_MODEL)


def _ctx_or_lat(ctx_ref, lat_ref):
    return jnp.where(pl.program_id(0) < N_CTX_BLOCKS, ctx_ref[...], lat_ref[...])


def _split_specs(width):
    return [pl.BlockSpec((TB, width), lambda i: (jnp.minimum(i, N_CTX_BLOCKS - 1), 0)),
            pl.BlockSpec((TB, width), lambda i: (jnp.maximum(i - N_CTX_BLOCKS, 0), 0))]


def _inproj_kernel(xc_ref, xl_ref, mod_ref, nw_ref, w_ref, wgt_ref,
                   qa_ref, ka_ref, va_ref, qm_ref, km_ref, vm_ref, om_ref, gt_ref):
    x = _ctx_or_lat(xc_ref, xl_ref)
    y = x * lax.rsqrt(jnp.mean(x * x, axis=-1, keepdims=True) + EPS) * nw_ref[...]
    hn = y * (1.0 + mod_ref[1:2, :]) + mod_ref[0:1, :]
    hb = hn.astype(BF16)
    qa_ref[...] = _dot(hb, w_ref[:, 0:512])
    ka_ref[...] = _dot(hb, w_ref[:, 512:640])
    va_ref[...] = _dot(hb, w_ref[:, 640:768])
    qm_ref[...] = _dot(hb, w_ref[:, 768:1280]).astype(BF16)
    km_ref[...] = (_dot(hb, w_ref[:, 1280:1792]) * (DH_M ** -0.5)).astype(BF16)
    vm_ref[...] = _dot(hb, w_ref[:, 1792:2304]).astype(BF16)
    om_ref[...] = _dot(hb, w_ref[:, 2304:2816])
    gt_ref[...] = lax.dot_general(wgt_ref[...], hb, _NT, preferred_element_type=F32)


def _inproj(x_ctx, x_lat, mod3, norm1_w, w_main, w_gates_t):
    tok = lambda w: pl.BlockSpec((TB, w), lambda i: (i, 0))
    full = lambda a: pl.BlockSpec(a.shape, lambda i: (0,) * a.ndim)
    sd = lambda w, dt: jax.ShapeDtypeStruct((T_ALL, w), dt)
    return pl.pallas_call(
        _inproj_kernel,
        grid=(T_ALL // TB,),
        in_specs=_split_specs(D_MODEL) + [
                  pl.BlockSpec((None, 6, D_MODEL), lambda i: (_mod_row(i), 0, 0)),
                  full(norm1_w), full(w_main), full(w_gates_t)],
        out_specs=[tok(512), tok(128), tok(128), tok(512), tok(512), tok(512), tok(512),
                   pl.BlockSpec((16, TB), lambda i: (0, i))],
        out_shape=[sd(512, F32), sd(128, F32), sd(128, F32), sd(512, BF16), sd(512, BF16),
                   sd(512, BF16), sd(512, F32), jax.ShapeDtypeStruct((16, T_ALL), F32)],
        compiler_params=_cparams(("arbitrary",)),
        name="inproj",
    )(x_ctx, x_lat, mod3, norm1_w, w_main, w_gates_t)


def _head_norm(x, ind, ind_t, w_row):
    ss = _dot3(x * x, ind)
    inv = lax.rsqrt(ss * (1.0 / DH_ATT) + EPS)
    return x * _dot3(inv, ind_t) * w_row


def _rope(x, cos, sin_signed):
    lane = lax.broadcasted_iota(jnp.int32, x.shape, 1)
    partner = jnp.where((lane % 32) < 16, pltpu.roll(x, 128 - 16, 1), pltpu.roll(x, 16, 1))
    return x * cos + partner * sin_signed


def _qkprep_kernel(*refs, rope):
    if rope:
        qa_ref, ka_ref, qw_ref, kw_ref, iq_ref, iqt_ref, ik_ref, ikt_ref, cos_ref, sin_ref, qn_ref, kn_ref = refs
    else:
        qa_ref, ka_ref, qw_ref, kw_ref, iq_ref, iqt_ref, ik_ref, ikt_ref, qn_ref, kn_ref = refs
    qn = _head_norm(qa_ref[...], iq_ref[...], iqt_ref[...], qw_ref[...])
    kn = _head_norm(ka_ref[...], ik_ref[...], ikt_ref[...], kw_ref[...])
    if rope:
        cos, sin = cos_ref[...], sin_ref[...]
        qn = jnp.concatenate([_rope(qn[:, 128 * j:128 * (j + 1)], cos, sin) for j in range(4)], axis=1)
        kn = _rope(kn, cos, sin)
    qn_ref[...] = qn.astype(BF16)
    kn_ref[...] = kn


def _qkprep(qa, ka, qw, kw, inds, rope_tabs, *, first_block, n_blocks):
    full = lambda a: pl.BlockSpec(a.shape, lambda i: (0,) * a.ndim)
    tok = lambda w: pl.BlockSpec((TB, w), lambda i: (i + first_block, 0))
    in_specs = [tok(512), tok(128), full(qw), full(kw)] + [full(a) for a in inds]
    args = [qa, ka, qw, kw] + list(inds)
    if rope_tabs is not None:
        in_specs += [pl.BlockSpec((TB, 128), lambda i: (i % LAT_BLOCKS_PER_BATCH, 0))] * 2
        args += list(rope_tabs)
    n_tok = n_blocks * TB
    return pl.pallas_call(
        functools.partial(_qkprep_kernel, rope=rope_tabs is not None),
        grid=(n_blocks,),
        in_specs=in_specs,
        out_specs=[pl.BlockSpec((TB, 512), lambda i: (i, 0)), pl.BlockSpec((TB, 128), lambda i: (i, 0))],
        out_shape=[jax.ShapeDtypeStruct((n_tok, 512), BF16), jax.ShapeDtypeStruct((n_tok, 128), F32)],
        compiler_params=_cparams(("arbitrary",)),
        name="qkprep_lat" if rope_tabs is not None else "qkprep_ctx",
    )(*args)


def _rope_tables():
    t = jnp.arange(DEC_SEQ)
    pos = jnp.stack([t // GRID_W, t % GRID_W], axis=1).astype(F32)
    n_freq = DH_ATT // 4
    inv_freq = ROPE_THETA ** (-jnp.arange(n_freq, dtype=F32) / n_freq)
    ang = pos[:, :, None] * inv_freq
    cos, sin = jnp.cos(ang), jnp.sin(ang)
    cos_h = jnp.stack([cos, cos], axis=2).reshape(DEC_SEQ, DH_ATT)
    sin_h = jnp.stack([-sin, sin], axis=2).reshape(DEC_SEQ, DH_ATT)
    return jnp.tile(cos_h, (1, 2)), jnp.tile(sin_h, (1, 2))


def _attn_kernel(*refs, has_cache):
    if has_cache:
        q_ref, k_ref, v_ref, kc_ref, vc_ref, o_ref = refs
    else:
        q_ref, k_ref, v_ref, o_ref = refs
    q = q_ref[...]
    k = k_ref[...].astype(BF16)
    v = v_ref[...].astype(BF16)
    qb = q.shape[0]
    scale = DH_ATT ** -0.5
    outs = []
    for g in range(N_KV_HEADS):
        kg = k[:, DH_ATT * g:DH_ATT * (g + 1)]
        vg = v[:, DH_ATT * g:DH_ATT * (g + 1)]
        qg = jnp.concatenate([q[:, 256 * g + DH_ATT * j:256 * g + DH_ATT * (j + 1)] for j in range(4)], axis=0)
        s = lax.dot_general(qg, kg, _NT, preferred_element_type=F32) * scale
        m = jnp.max(s, axis=-1, keepdims=True)
        if has_cache:
            kc = kc_ref[g].astype(BF16)
            vc = vc_ref[g].astype(BF16)
            sc = lax.dot_general(qg, kc, _NT, preferred_element_type=F32) * scale
            m = jnp.maximum(m, jnp.max(sc, axis=-1, keepdims=True))
        p = jnp.exp(s - m)
        den = jnp.sum(p, axis=-1, keepdims=True)
        o = _dot(p.astype(BF16), vg)
        if has_cache:
            pc = jnp.exp(sc - m)
            den = den + jnp.sum(pc, axis=-1, keepdims=True)
            o = o + _dot(pc.astype(BF16), vc)
        o = o / den
        outs += [o[qb * j:qb * (j + 1), :] for j in range(4)]
    o_ref[...] = jnp.concatenate(outs, axis=1).astype(BF16)


def _attention(qn, kn, v_all, cache, *, n_batch, seq, v_first_block):
    qblocks = seq // TB
    kblk = lambda off: pl.BlockSpec((seq, 128), lambda b, i: (b + off, 0))
    in_specs = [pl.BlockSpec((TB, 512), lambda b, i: (b * qblocks + i, 0)), kblk(0), kblk(v_first_block)]
    args = [qn, kn, v_all]
    if cache is not None:
        cspec = pl.BlockSpec((None, None, N_KV_HEADS, PAST_LEN, DH_ATT), lambda b, i: (b, 0, 0, 0, 0))
        in_specs += [cspec, cspec]
        args += list(cache)
    return pl.pallas_call(
        functools.partial(_attn_kernel, has_cache=cache is not None),
        grid=(n_batch, qblocks),
        in_specs=in_specs,
        out_specs=pl.BlockSpec((TB, 512), lambda b, i: (b * qblocks + i, 0)),
        out_shape=jax.ShapeDtypeStruct((n_batch * seq, 512), BF16),
        compiler_params=_cparams(("arbitrary", "arbitrary")),
        name="attention_lat" if cache is not None else "attention_ctx",
    )(*args)


def _log_sigmoid(x):
    return jnp.minimum(x, 0.0) - jnp.log1p(jnp.exp(-jnp.abs(x)))


def _lane_scan(x, op, fill, reverse):
    lane = lax.broadcasted_iota(jnp.int32, x.shape, 1)
    for sh in (1, 2, 4, 8, 16, 32, 64):
        if reverse:
            shifted = jnp.where(lane < CHUNK - sh, pltpu.roll(x, CHUNK - sh, 1), fill)
        else:
            shifted = jnp.where(lane >= sh, pltpu.roll(x, sh, 1), fill)
        x = op(x, shifted)
    return x


def _mlstm_kernel(*refs, has_state, n_chunks):
    n_in = 10 if has_state else 7
    n_out = 1 if has_state else 4
    q_ref, k_ref, v_ref, g_ref, bg_ref, om_ref, nw_ref = refs[:7]
    hm_ref = refs[n_in]
    scratch = refs[n_in + n_out:]
    cs = scratch[0:8]
    nm = scratch[8:16]
    hdir = scratch[16:18]

    for d in range(2):
        for h in range(N_HEADS_M):
            idx = 4 * d + h
            if has_state:
                c0_ref, n0_ref, m0_ref = refs[7:10]
                cs[idx][...] = c0_ref[d, h]
                nm[idx][0:1, :] = n0_ref[d, h:h + 1, :]
                nm[idx][1:2, :] = m0_ref[idx:idx + 1, :]
            else:
                cs[idx][...] = jnp.zeros((DH_M, DH_M), F32)
                nm[idx][0:1, :] = jnp.zeros((1, DH_M), F32)
                nm[idx][1:2, :] = jnp.full((1, CHUNK), NEG_INIT, F32)

    sub = lax.broadcasted_iota(jnp.int32, (CHUNK, CHUNK), 0)
    lan = lax.broadcasted_iota(jnp.int32, (CHUNK, CHUNK), 1)
    neg_inf = F32(-jnp.inf)

    def chunk_step(i, carry):
        for d in range(2):
            c = i if d == 0 else n_chunks - 1 - i
            r0 = pl.multiple_of(c * CHUNK, CHUNK)
            pre = g_ref[c] + bg_ref[...]
            bcum = _lane_scan(_log_sigmoid(pre), jnp.add, 0.0, d == 1)
            li4 = pre[4 * d:4 * d + 4, :]
            b4 = bcum[8 + 4 * d:12 + 4 * d, :]
            r4 = li4 - b4
            cm4 = _lane_scan(r4, jnp.maximum, neg_inf, d == 1)
            mp4 = jnp.concatenate([nm[4 * d + h][1:2, :] for h in range(N_HEADS_M)], axis=0)
            g4 = b4 + mp4
            mrow4 = jnp.maximum(g4, b4 + cm4)
            blast4 = b4[:, CHUNK - 1:CHUNK] if d == 0 else b4[:, 0:1]
            wlog4 = blast4 + r4
            gend4 = blast4 + mp4
            mnew4 = jnp.maximum(gend4, jnp.max(wlog4, axis=-1, keepdims=True))
            decay4 = jnp.exp(gend4 - mnew4)
            rows = jnp.concatenate([b4 - mrow4, jnp.exp(g4 - mrow4), jnp.exp(-mrow4), jnp.exp(wlog4 - mnew4),
                                    jnp.zeros((CHUNK - 16, CHUNK), F32)], axis=0)
            cols = rows.T
            mask = (lan <= sub) if d == 0 else (lan >= sub)
            for h in range(N_HEADS_M):
                idx = 4 * d + h
                hs = slice(DH_M * h, DH_M * (h + 1))
                qc = q_ref[pl.ds(r0, CHUNK), hs]
                kc = k_ref[pl.ds(r0, CHUNK), hs]
                vc = v_ref[pl.ds(r0, CHUNK), hs]
                a_col = cols[:, h:h + 1]
                gw_col = cols[:, 4 + h:5 + h]
                en_col = cols[:, 8 + h:9 + h]
                w_col = cols[:, 12 + h:13 + h]
                decay = jnp.exp(jnp.where(mask, a_col + r4[h:h + 1, :], neg_inf))
                s = lax.dot_general(qc, kc, _NT, preferred_element_type=F32) * decay
                c_old = cs[idx][...]
                n_old = nm[idx][0:1, :]
                num = _dot(s.astype(BF16), vc) + gw_col * _dot(qc, c_old.astype(BF16))
                den = (jnp.sum(s, axis=-1, keepdims=True)
                       + gw_col * jnp.sum(qc.astype(F32) * n_old, axis=-1, keepdims=True))
                hdir[d][pl.ds(r0, CHUNK), hs] = num / jnp.maximum(jnp.abs(den), en_col)
                kw = kc.astype(F32) * w_col
                dec = decay4[h:h + 1, :]
                cs[idx][...] = dec * c_old + lax.dot_general(kw.astype(BF16), vc, _TN,
                                                             preferred_element_type=F32)
                nm[idx][0:1, :] = dec * n_old + jnp.sum(kw, axis=0, keepdims=True)
                nm[idx][1:2, :] = mnew4[h:h + 1, :]
        return carry

    lax.fori_loop(0, n_chunks, chunk_step, 0)

    for h in range(N_HEADS_M):
        hs = slice(DH_M * h, DH_M * (h + 1))
        hh = hdir[0][:, hs] + hdir[1][:, hs]
        y = hh * lax.rsqrt(jnp.mean(hh * hh, axis=-1, keepdims=True) + EPS) * nw_ref[:, hs]
        hm_ref[:, hs] = (jax.nn.sigmoid(om_ref[:, hs]) * y).astype(BF16)
    if not has_state:
        c_ref, n_ref, m_ref = refs[n_in + 1:n_in + 4]
        for d in range(2):
            for h in range(N_HEADS_M):
                idx = 4 * d + h
                c_ref[d, h] = cs[idx][...]
                n_ref[d, h:h + 1, :] = nm[idx][0:1, :]
                m_ref[idx:idx + 1, :] = nm[idx][1:2, :]


def _mlstm(qm, km, vm, g3, bg, om, nw, state, *, n_batch, seq, first_block):
    n_chunks = seq // CHUNK
    rb = seq // TB
    tok = lambda: pl.BlockSpec((seq, 512), lambda b: (b + first_block // rb, 0))
    full = lambda a: pl.BlockSpec(a.shape, lambda b: (0,) * a.ndim)
    in_specs = [tok(), tok(), tok(),
                pl.BlockSpec((n_chunks, 16, CHUNK), lambda b: (b + first_block // rb, 0, 0)),
                full(bg), tok(), full(nw)]
    args = [qm, km, vm, g3, bg, om, nw]
    hm_spec = pl.BlockSpec((seq, 512), lambda b: (b, 0))
    hm_shape = jax.ShapeDtypeStruct((n_batch * seq, 512), BF16)
    scratch = ([pltpu.VMEM((DH_M, DH_M), F32)] * 8 + [pltpu.VMEM((8, DH_M), F32)] * 8
               + [pltpu.VMEM((seq, 512), F32)] * 2)
    if state is not None:
        c0, n0, m0 = state
        in_specs += [pl.BlockSpec((None, None, 2, N_HEADS_M, DH_M, DH_M), lambda b: (b, 0, 0, 0, 0, 0)),
                     pl.BlockSpec((None, None, 2, N_HEADS_M, DH_M), lambda b: (b, 0, 0, 0, 0)),
                     pl.BlockSpec((None, 8, CHUNK), lambda b: (b, 0, 0))]
        args += [c0, n0, m0]
        out_specs, out_shape = hm_spec, hm_shape
    else:
        out_specs = [hm_spec,
                     pl.BlockSpec((None, 2, N_HEADS_M, DH_M, DH_M), lambda b: (b, 0, 0, 0, 0)),
                     pl.BlockSpec((None, 2, N_HEADS_M, DH_M), lambda b: (b, 0, 0, 0)),
                     pl.BlockSpec((None, 8, CHUNK), lambda b: (b, 0, 0))]
        out_shape = [hm_shape,
                     jax.ShapeDtypeStruct((n_batch, 2, N_HEADS_M, DH_M, DH_M), F32),
                     jax.ShapeDtypeStruct((n_batch, 2, N_HEADS_M, DH_M), F32),
                     jax.ShapeDtypeStruct((n_batch, 8, CHUNK), F32)]
    return pl.pallas_call(
        functools.partial(_mlstm_kernel, has_state=state is not None, n_chunks=n_chunks),
        grid=(n_batch,),
        in_specs=in_specs,
        out_specs=out_specs,
        out_shape=out_shape,
        scratch_shapes=scratch,
        compiler_params=_cparams(("arbitrary",)),
        name="mlstm_lat" if state is not None else "mlstm_ctx",
    )(*args)


def _outproj_kernel(attc_ref, attl_ref, hmc_ref, hml_ref, xc_ref, xl_ref, mod_ref, nw_ref, wo_ref,
                    rwh_ref, rwl_ref, wsg_ref, wsu_ref, wsd_ref, base_ref, hn_ref, lt_ref):
    y = (_dot(_ctx_or_lat(attc_ref, attl_ref), wo_ref[0:D_ATT, :])
         + _dot(_ctx_or_lat(hmc_ref, hml_ref), wo_ref[D_ATT:, :]))
    x1 = _ctx_or_lat(xc_ref, xl_ref) + mod_ref[2:3, :] * y
    z = x1 * lax.rsqrt(jnp.mean(x1 * x1, axis=-1, keepdims=True) + EPS) * nw_ref[...]
    hn = z * (1.0 + mod_ref[4:5, :]) + mod_ref[3:4, :]
    hb = hn.astype(BF16)
    hl = (hn - hb.astype(F32)).astype(BF16)
    nt = lambda w, t: lax.dot_general(w, t, _NT, preferred_element_type=F32)
    lt_ref[...] = nt(rwh_ref[...], hb) + nt(rwl_ref[...], hb) + nt(rwh_ref[...], hl)
    a = _silu(_dot(hb, wsg_ref[...])) * _dot(hb, wsu_ref[...])
    shared = _dot(a.astype(BF16), wsd_ref[...])
    base_ref[...] = x1 + mod_ref[5:6, :] * shared
    packed = pltpu.pack_elementwise([hn[:, :512], hn[:, 512:]], packed_dtype=BF16)
    for c in range(4):
        hn_ref[pl.ds(c, TB, stride=4), :] = packed[:, 128 * c:128 * (c + 1)]


def _outproj(att, hm, x, mod3, norm2_w, w_out, rw_hi, rw_lo, wsg, wsu, wsd):
    tok = lambda w: pl.BlockSpec((TB, w), lambda i: (i, 0))
    full = lambda a: pl.BlockSpec(a.shape, lambda i: (0,) * a.ndim)
    return pl.pallas_call(
        _outproj_kernel,
        grid=(T_ALL // TB,),
        in_specs=_split_specs(512) + _split_specs(512) + _split_specs(D_MODEL) + [
                  pl.BlockSpec((None, 6, D_MODEL), lambda i: (_mod_row(i), 0, 0)),
                  full(norm2_w), full(w_out), full(rw_hi), full(rw_lo), full(wsg), full(wsu), full(wsd)],
        out_specs=[tok(D_MODEL), pl.BlockSpec((4 * TB, 128), lambda i: (i, 0)),
                   pl.BlockSpec((N_EXPERTS, TB), lambda i: (0, i))],
        out_shape=[jax.ShapeDtypeStruct((T_ALL, D_MODEL), F32),
                   jax.ShapeDtypeStruct((4 * T_ALL, 128), jnp.uint32),
                   jax.ShapeDtypeStruct((N_EXPERTS, T_ALL), F32)],
        compiler_params=_cparams(("arbitrary",)),
        name="outproj",
    )(*att, *hm, *x, mod3, norm2_w, w_out, rw_hi, rw_lo, wsg, wsu, wsd)


def _first_max(vals, ids, limit):
    m = functools.reduce(jnp.maximum, [jnp.max(v, axis=0, keepdims=True) for v in vals])
    cand = [jnp.min(jnp.where(v == m, i, limit), axis=0, keepdims=True) for v, i in zip(vals, ids)]
    return m, functools.reduce(jnp.minimum, cand)


def _router_kernel(lt_ref, bias_ref, gate_ref, idx_ref):
    n = lt_ref.shape[1]
    score = jax.nn.sigmoid(lt_ref[...])
    biased = score + bias_ref[...]
    sub = lax.broadcasted_iota(jnp.int32, (8, n), 0).astype(F32)
    neg_inf = F32(-jnp.inf)
    slabs = [biased[8 * g:8 * (g + 1), :] for g in range(N_GROUPS)]
    gs = []
    for sl in slabs:
        m1, i1 = _first_max([sl], [sub], 8.0)
        m2 = jnp.max(jnp.where(sub == i1, neg_inf, sl), axis=0, keepdims=True)
        gs.append(m1 + m2)
    cur = jnp.concatenate(gs, axis=0)
    gsel = jnp.zeros((8, n), F32)
    for _ in range(TOPK_GROUPS):
        _, i = _first_max([cur], [sub], 8.0)
        hit = sub == i
        gsel = jnp.where(hit, 1.0, gsel)
        cur = jnp.where(hit, neg_inf, cur)
    vals = [jnp.where(gsel[g:g + 1, :] > 0.0, slabs[g], neg_inf) for g in range(N_GROUPS)]
    ids = [sub + 8.0 * g for g in range(N_GROUPS)]
    picked = [jnp.zeros((8, n), F32) for _ in range(N_GROUPS)]
    order = []
    for _ in range(TOP_K):
        _, i = _first_max(vals, ids, float(N_EXPERTS))
        order.append(i)
        hits = [idg == i for idg in ids]
        picked = [jnp.where(hh, score[8 * g:8 * (g + 1), :], p) for g, (p, hh) in enumerate(zip(picked, hits))]
        vals = [jnp.where(hh, neg_inf, v) for v, hh in zip(vals, hits)]
    total = functools.reduce(jnp.add, [jnp.sum(p, axis=0, keepdims=True) for p in picked])
    gate_t = jnp.concatenate([p / total * ROUTED_SCALE for p in picked]
                             + [jnp.zeros((128 - N_EXPERTS, n), F32)], axis=0)
    gate_ref[...] = gate_t.T
    idx_ref[...] = jnp.concatenate(order, axis=0).astype(jnp.int32)


def _router(logits_t, bias_col):
    return pl.pallas_call(
        _router_kernel,
        grid=(T_ALL // TB_MOE,),
        in_specs=[pl.BlockSpec((N_EXPERTS, TB_MOE), lambda i: (0, i)),
                  pl.BlockSpec((N_EXPERTS, 1), lambda i: (0, 0))],
        out_specs=[pl.BlockSpec((TB_MOE, 128), lambda i: (i, 0)),
                   pl.BlockSpec((TOP_K, TB_MOE), lambda i: (0, i))],
        out_shape=[jax.ShapeDtypeStruct((T_ALL, 128), F32),
                   jax.ShapeDtypeStruct((TOP_K, T_ALL), jnp.int32)],
        compiler_params=_cparams(("arbitrary",)),
        name="router",
    )(logits_t, bias_col)


def _plan_kernel(idx_ref, slot_ref, cnt_ref, off_ref, pos_sc):
    n_tiles = T_HALF // 128
    eid = lax.broadcasted_iota(jnp.int32, (N_EXPERTS, 128), 0)
    tri = (lax.broadcasted_iota(jnp.int32, (128, 128), 0)
           <= lax.broadcasted_iota(jnp.int32, (128, 128), 1)).astype(BF16)
    carry = jnp.zeros((N_EXPERTS, 1), F32)
    for j in range(n_tiles):
        it = idx_ref[:, 128 * j:128 * (j + 1)]
        sel = jnp.zeros((N_EXPERTS, 128), F32)
        for k in range(TOP_K):
            sel = jnp.where(it[k:k + 1, :] == eid, 1.0, sel)
        inc = _dot(sel.astype(BF16), tri) + carry
        carry = inc[:, 127:128]
        pos_sc[:, 128 * j:128 * (j + 1)] = inc - 1.0
    count = jnp.broadcast_to(carry, (N_EXPERTS, 128))
    padded = jnp.floor((count + (MT - 1.0)) * (1.0 / MT)) * MT
    before = (lax.broadcasted_iota(jnp.int32, (N_EXPERTS, N_EXPERTS), 1)
              < lax.broadcasted_iota(jnp.int32, (N_EXPERTS, N_EXPERTS), 0)).astype(BF16)
    hi, mid, lo = _split3(padded)
    off = _dot(before, hi) + _dot(before, mid) + _dot(before, lo)
    cnt_ref[...] = count.astype(jnp.int32)
    off_ref[...] = off.astype(jnp.int32)

    def slots(j):
        it = idx_ref[:, 128 * j:128 * (j + 1)]
        val = off + pos_sc[:, 128 * j:128 * (j + 1)]
        rows = [jnp.sum(jnp.where(it[k:k + 1, :] == eid, val, 0.0), axis=0, keepdims=True)
                for k in range(TOP_K)]
        return jnp.concatenate(rows, axis=0).astype(jnp.int32)

    for j in range(n_tiles // 2):
        slot_ref[:, 128 * j:128 * (j + 1)] = slots(j) | (slots(j + n_tiles // 2) << 16)


def _plan(idx8):
    return pl.pallas_call(
        _plan_kernel,
        grid=(2,),
        in_specs=[pl.BlockSpec((TOP_K, T_HALF), lambda h: (0, h))],
        out_specs=[pl.BlockSpec((None, TOP_K, T_HALF // 2), lambda h: (h, 0, 0)),
                   pl.BlockSpec((None, N_EXPERTS, 128), lambda h: (h, 0, 0)),
                   pl.BlockSpec((None, N_EXPERTS, 128), lambda h: (h, 0, 0))],
        out_shape=[jax.ShapeDtypeStruct((2, TOP_K, T_HALF // 2), jnp.int32),
                   jax.ShapeDtypeStruct((2, N_EXPERTS, 128), jnp.int32),
                   jax.ShapeDtypeStruct((2, N_EXPERTS, 128), jnp.int32)],
        scratch_shapes=[pltpu.VMEM((N_EXPERTS, T_HALF), F32)],
        compiler_params=_cparams(("arbitrary",)),
        name="plan",
    )(idx8)


def _prepare_half(slot_ref, cnt_ref, off_ref, tok_ref, tile_e, tile_first, elist, h):
    def per_expert(x, carry):
        j, q = carry
        n = cnt_ref[h * N_EXPERTS + x]
        first = off_ref[h * N_EXPERTS + x]
        tiles = (n + MT - 1) // MT
        elist[q] = x

        def mark(i, c):
            tile_e[j + i] = x
            tile_first[j + i] = jnp.where(i == 0, 1, 0)
            return c

        lax.fori_loop(0, tiles, mark, 0)

        def pad(p, c):
            tok_ref[first + p] = T_HALF
            return c

        lax.fori_loop(n, tiles * MT, pad, 0)
        return j + tiles, q + jnp.where(tiles > 0, 1, 0)

    n_tiles, n_live = lax.fori_loop(0, N_EXPERTS, per_expert, (0, 0))
    for extra in range(2):
        tile_e[n_tiles + extra] = 0
        tile_first[n_tiles + extra] = 0

    def pad_tail(p, c):
        tok_ref[n_tiles * MT + p] = T_HALF
        return c

    lax.fori_loop(0, 2 * MT, pad_tail, 0)
    return n_tiles, n_live


def _invert_slots(slot_ref, tok_ref, h):
    half = T_HALF // 2
    for k in range(TOP_K):
        def body(i, carry, k=k):
            words = [slot_ref[(h * TOP_K + k) * half + i * INV_UNROLL + u] for u in range(INV_UNROLL)]
            for u in range(INV_UNROLL):
                t = i * INV_UNROLL + u
                tok_ref[words[u] & 0xFFFF] = t
                tok_ref[lax.shift_right_logical(words[u], 16)] = t + half
            return carry

        lax.fori_loop(0, half // INV_UNROLL, body, 0)


def _moe_kernel(slot_ref, cnt_ref, off_ref, src_hbm, gate_hbm, wg_hbm, wu_hbm, wd_hbm, out_hbm,
                src_v, gate_v, acc_v, wg_l, wu_l, wd_l, wgu_b, wd_b, xbuf0, xbuf1, gbuf0, gbuf1,
                ybuf0, ybuf1, tok_ref, tile_e, tile_first, elist, live, sem, wsem):
    h = pl.program_id(0)
    xbuf, gbuf, ybuf = (xbuf0, xbuf1), (gbuf0, gbuf1), (ybuf0, ybuf1)

    src_cp = pltpu.make_async_copy(src_hbm.at[pl.ds(pl.multiple_of(h * (4 * T_HALF), 8), 4 * T_HALF)],
                                   src_v.at[pl.ds(0, 4 * T_HALF)], sem.at[0])
    gate_cp = pltpu.make_async_copy(gate_hbm.at[pl.ds(pl.multiple_of(h * T_HALF, 8), T_HALF)],
                                    gate_v.at[pl.ds(0, T_HALF)], sem.at[1])
    src_cp.start()
    gate_cp.start()

    def weight_copies(x, slot):
        return [pltpu.make_async_copy(w_hbm.at[x], w_l.at[slot], wsem.at[slot, i])
                for i, (w_hbm, w_l) in enumerate(((wg_hbm, wg_l), (wu_hbm, wu_l), (wd_hbm, wd_l)))]

    n_tiles, n_live = _prepare_half(slot_ref, cnt_ref, off_ref, tok_ref, tile_e, tile_first, elist, h)
    live[0] = 0
    live[1] = n_live

    @pl.when(n_live > 0)
    def _():
        for cp in weight_copies(elist[0], 0):
            cp.start()

    acc_v[...] = jnp.zeros(acc_v.shape, F32)
    zero = jnp.zeros((8, 128), F32)
    src_v[pl.ds(4 * T_HALF, 8), :] = pltpu.pack_elementwise([zero, zero], packed_dtype=BF16)
    gate_v[pl.ds(T_HALF, 8), :] = zero
    ybuf0[...] = jnp.zeros(ybuf0.shape, F32)
    ybuf1[...] = jnp.zeros(ybuf1.shape, F32)
    _invert_slots(slot_ref, tok_ref, h)
    src_cp.wait()
    gate_cp.wait()

    def switch_expert():
        q = live[0]
        slot = q % 2
        for cp in weight_copies(elist[q], slot):
            cp.wait()
        wgu_b[:, 0:D_EXPERT] = wg_l[slot].astype(BF16)
        wgu_b[:, D_EXPERT:] = wu_l[slot].astype(BF16)
        wd_b[...] = wd_l[slot].astype(BF16)
        live[0] = q + 1

        @pl.when(q + 1 < live[1])
        def _():
            for cp in weight_copies(elist[q + 1], 1 - slot):
                cp.start()

    def gather(j, xb, gb):
        base = j * MT
        for m in range(MT):
            t = tok_ref[base + m]
            xb[pl.ds(m, 4, stride=MT_STRIDE), :] = src_v[pl.ds(pl.multiple_of(t * 4, 4), 4), :]
            gb[m:m + 1, :] = gate_v[pl.ds(t, 1), :]

    def compute(j, xb, gb, yb):
        lo, hi = [], []
        for c in range(4):
            words = xb[MT_STRIDE * c:MT_STRIDE * c + MT, :]
            unpack = functools.partial(pltpu.unpack_elementwise, words, packed_dtype=BF16, unpacked_dtype=F32)
            lo.append(unpack(index=0).astype(BF16))
            hi.append(unpack(index=1).astype(BF16))
        x = jnp.concatenate(lo + hi, axis=1)
        hgu = _dot(x, wgu_b[...])
        g = gb[...]
        g_hi = g.astype(BF16)
        g_lo = (g - g_hi.astype(F32)).astype(BF16)
        pick = (lax.broadcasted_iota(jnp.int32, (128, D_EXPERT), 0) == tile_e[j]).astype(BF16)
        g_col = _dot(g_hi, pick) + _dot(g_lo, pick)
        a = _silu(hgu[:, :D_EXPERT]) * hgu[:, D_EXPERT:] * g_col
        y = _dot(a.astype(BF16), wd_b[...])
        for c in range(8):
            yb[MT_STRIDE * c:MT_STRIDE * c + MT, :] = y[:, 128 * c:128 * (c + 1)]

    def scatter(j, yb):
        base = j * MT
        for b in range(MT // RMW_BATCH):
            ms = [b * RMW_BATCH + u for u in range(RMW_BATCH)]
            rows = [pl.ds(pl.multiple_of(tok_ref[base + m] * 8, 8), 8) for m in ms]
            vals = [acc_v[r, :] + yb[pl.ds(m, 8, stride=MT_STRIDE), :] for r, m in zip(rows, ms)]
            for r, v in zip(rows, vals):
                acc_v[r, :] = v

    def step(j, p):
        pl.when(tile_first[j] == 1)(switch_expert)
        gather(j + 1, xbuf[1 - p], gbuf[1 - p])
        compute(j, xbuf[p], gbuf[p], ybuf[p])
        scatter(jnp.maximum(j - 1, 0), ybuf[1 - p])

    gather(0, xbuf[0], gbuf[0])
    n_pairs = (n_tiles + 1) // 2

    def pair(i, carry):
        step(2 * i, 0)
        step(2 * i + 1, 1)
        return carry

    lax.fori_loop(0, n_pairs, pair, 0)
    scatter(jnp.maximum(2 * n_pairs - 1, 0), ybuf[1])

    out_cp = pltpu.make_async_copy(acc_v.at[pl.ds(0, 8 * T_HALF)],
                                   out_hbm.at[pl.ds(pl.multiple_of(h * (8 * T_HALF), 8), 8 * T_HALF)],
                                   sem.at[2])
    out_cp.start()
    out_cp.wait()


def _moe(slot_words, cnt, off, src, gate, wg, wu, wd):
    any_spec = pl.BlockSpec(memory_space=pl.ANY)
    tile_buf = lambda rows, dt: pltpu.VMEM((rows * MT_STRIDE, 128), dt)
    return pl.pallas_call(
        _moe_kernel,
        grid_spec=pltpu.PrefetchScalarGridSpec(
            num_scalar_prefetch=3,
            grid=(2,),
            in_specs=[any_spec] * 5,
            out_specs=any_spec,
            scratch_shapes=[pltpu.VMEM((4 * T_HALF + 8, 128), jnp.uint32),
                            pltpu.VMEM((T_HALF + 8, 128), F32),
                            pltpu.VMEM((8 * T_HALF + 8, 128), F32),
                            pltpu.VMEM((2, D_MODEL, D_EXPERT), F32),
                            pltpu.VMEM((2, D_MODEL, D_EXPERT), F32),
                            pltpu.VMEM((2, D_EXPERT, D_MODEL), F32),
                            pltpu.VMEM((D_MODEL, 2 * D_EXPERT), BF16),
                            pltpu.VMEM((D_EXPERT, D_MODEL), BF16),
                            tile_buf(4, jnp.uint32), tile_buf(4, jnp.uint32),
                            pltpu.VMEM((MT, 128), F32), pltpu.VMEM((MT, 128), F32),
                            tile_buf(8, F32), tile_buf(8, F32),
                            pltpu.SMEM((SLOT_CAP,), jnp.int32),
                            pltpu.SMEM((NT_MAX + 2,), jnp.int32),
                            pltpu.SMEM((NT_MAX + 2,), jnp.int32),
                            pltpu.SMEM((N_EXPERTS,), jnp.int32),
                            pltpu.SMEM((2,), jnp.int32),
                            pltpu.SemaphoreType.DMA((3,)),
                            pltpu.SemaphoreType.DMA((2, 3))]),
        out_shape=jax.ShapeDtypeStruct((8 * T_ALL, 128), F32),
        compiler_params=_cparams(("arbitrary",)),
        name="moe",
    )(slot_words, cnt, off, src, gate, wg, wu, wd)


def _finalize_kernel(acc_ref, base_ref, mod_ref, oc_ref, ol_ref):
    def write(o_ref):
        for c in range(8):
            cs = slice(128 * c, 128 * (c + 1))
            o_ref[:, cs] = base_ref[:, cs] + mod_ref[5:6, cs] * acc_ref[pl.ds(c, TB, stride=8), :]

    is_ctx = pl.program_id(0) < N_CTX_BLOCKS
    pl.when(is_ctx)(lambda: write(oc_ref))
    pl.when(jnp.logical_not(is_ctx))(lambda: write(ol_ref))


def _finalize(acc, base, mod3):
    return pl.pallas_call(
        _finalize_kernel,
        grid=(T_ALL // TB,),
        in_specs=[pl.BlockSpec((8 * TB, 128), lambda i: (i, 0)),
                  pl.BlockSpec((TB, D_MODEL), lambda i: (i, 0)),
                  pl.BlockSpec((None, 6, D_MODEL), lambda i: (_mod_row(i), 0, 0))],
        out_specs=_split_specs(D_MODEL),
        out_shape=[jax.ShapeDtypeStruct((T_CTX, D_MODEL), F32), jax.ShapeDtypeStruct((T_LAT, D_MODEL), F32)],
        compiler_params=_cparams(("arbitrary",)),
        name="finalize",
    )(acc, base, mod3)


def _head_indicators(width):
    head = jnp.arange(width) // DH_ATT
    ind = (head[:, None] == jnp.arange(128)[None, :]).astype(BF16)
    return ind, ind.T


def kernel(x_prompt, x_sample, cache_attn_k, cache_attn_v, state_mlstm_c, state_mlstm_n, state_mlstm_m, c, c_ctx, w_mod, b_mod, norm1_w, norm2_w, w_in, q_norm_w, k_norm_w, b_gates, m_norm_w, w_out, router_w, router_bias, w_gate, w_up, w_down, ws_gate, ws_up, ws_down):
    x = (x_prompt.reshape(T_CTX, D_MODEL), x_sample.reshape(T_LAT, D_MODEL))
    mod3 = _modulation(c, c_ctx, w_mod[0], b_mod[0])

    w_main = w_in[0, :, :P_MAIN].astype(BF16)
    w_gates_t = w_in[0, :, P_MAIN:].T.astype(BF16)
    qa, ka, va, qm, km, vm, om, gt = _inproj(*x, mod3, norm1_w, w_main, w_gates_t)

    qw = jnp.tile(q_norm_w, (1, N_HEADS_ATT))
    kw = jnp.tile(k_norm_w, (1, N_KV_HEADS))
    inds = _head_indicators(512) + _head_indicators(128)
    qn_c, kn_c = _qkprep(qa, ka, qw, kw, inds, None, first_block=0, n_blocks=N_CTX_BLOCKS)
    qn_l, kn_l = _qkprep(qa, ka, qw, kw, inds, _rope_tables(), first_block=N_CTX_BLOCKS,
                         n_blocks=T_LAT // TB)
    att_c = _attention(qn_c, kn_c, va, None, n_batch=BATCH, seq=SEQ, v_first_block=0)
    att_l = _attention(qn_l, kn_l, va, (cache_attn_k, cache_attn_v), n_batch=DEC_BATCH, seq=DEC_SEQ,
                       v_first_block=T_CTX // DEC_SEQ)

    g3 = gt.reshape(16, T_ALL // CHUNK, CHUNK).transpose(1, 0, 2)
    bg = b_gates.reshape(16, 1)
    hm_c, c_new, n_new, m_new = _mlstm(qm, km, vm, g3, bg, om, m_norm_w, None,
                                       n_batch=BATCH, seq=SEQ, first_block=0)
    m0 = jnp.broadcast_to(state_mlstm_m.reshape(DEC_BATCH, 8, 1), (DEC_BATCH, 8, CHUNK))
    hm_l = _mlstm(qm, km, vm, g3, bg, om, m_norm_w, (state_mlstm_c, state_mlstm_n, m0),
                  n_batch=DEC_BATCH, seq=DEC_SEQ, first_block=N_CTX_BLOCKS)

    rw_t = router_w[0].T
    rw_hi = rw_t.astype(BF16)
    rw_lo = (rw_t - rw_hi.astype(F32)).astype(BF16)
    base, hn2, logits_t = _outproj((att_c, att_l), (hm_c, hm_l), x, mod3, norm2_w, w_out[0].astype(BF16),
                                   rw_hi, rw_lo, ws_gate[0].astype(BF16), ws_up[0].astype(BF16),
                                   ws_down[0].astype(BF16))
    gate, idx8 = _router(logits_t, router_bias.reshape(N_EXPERTS, 1))
    slot_words, cnt, off = _plan(idx8)
    acc = _moe(slot_words.reshape(TOP_K * T_HALF), cnt[:, :, 0].reshape(2 * N_EXPERTS),
               off[:, :, 0].reshape(2 * N_EXPERTS), hn2, gate, w_gate[0], w_up[0], w_down[0])
    out_c, out_l = _finalize(acc, base, mod3)

    y_prompt = out_c.reshape(BATCH, SEQ, D_MODEL)
    y_sample = out_l.reshape(DEC_BATCH, DEC_SEQ, D_MODEL)
    to_cache = lambda a: a[:T_CTX].reshape(BATCH, SEQ, N_KV_HEADS, DH_ATT).transpose(0, 2, 1, 3)[:, None]
    new_k = to_cache(kn_c)
    new_v = to_cache(va)
    new_m = m_new[:, :, 0].reshape(BATCH, 1, 2, N_HEADS_M)
    return (y_prompt, y_sample, new_k, new_v, c_new[:, None], n_new[:, None], new_m)
```

```python
import functools

import jax
import jax.numpy as jnp
from jax import lax
from jax.experimental import pallas as pl
from jax.experimental.pallas import tpu as pltpu

F32 = jnp.float32
BF16 = jnp.bfloat16

D_MODEL = 1024
BATCH = 32
SEQ = 256
DEC_BATCH = 2
DEC_SEQ = 1024
PAST_LEN = 256
GRID_W = 64
N_HEADS_ATT = 8
N_KV_HEADS = 2
DH_ATT = 64
D_ATT = 512
ROPE_THETA = 10000.0
N_HEADS_M = 4
DH_M = 128
D_M = 512
CHUNK = 128
N_EXPERTS = 64
TOP_K = 8
N_GROUPS = 8
TOPK_GROUPS = 4
D_EXPERT = 256
ROUTED_SCALE = 2.5
EPS = 1e-6
NEG_INIT = -1e30

T_CTX = BATCH * SEQ
T_LAT = DEC_BATCH * DEC_SEQ
T_ALL = T_CTX + T_LAT
TB = 256
N_CTX_BLOCKS = T_CTX // TB
LAT_BLOCKS_PER_BATCH = DEC_SEQ // TB
TB_MOE = 1024
T_HALF = T_ALL // 2
MT = 256
MT_STRIDE = MT + 8
NT_MAX = TOP_K * T_HALF // MT + N_EXPERTS
SLOT_CAP = (NT_MAX + 2) * MT
RMW_BATCH = 16
INV_UNROLL = 16
assert SLOT_CAP < 2 ** 16 and (T_HALF // 2) % INV_UNROLL == 0 and MT % RMW_BATCH == 0
P_MAIN = 2816
VMEM_LIMIT = 56 * 1024 * 1024

_NT = (((1,), (1,)), ((), ()))
_TN = (((0,), (0,)), ((), ()))


def _cparams(sem):
    return pltpu.CompilerParams(dimension_semantics=sem, vmem_limit_bytes=VMEM_LIMIT)


def _split3(x):
    hi = x.astype(BF16)
    r1 = x - hi.astype(F32)
    mid = r1.astype(BF16)
    lo = (r1 - mid.astype(F32)).astype(BF16)
    return hi, mid, lo


def _dot(a, b):
    return jnp.dot(a, b, preferred_element_type=F32)


def _dot3(x, m_bf16):
    hi, mid, lo = _split3(x)
    return _dot(hi, m_bf16) + _dot(mid, m_bf16) + _dot(lo, m_bf16)


def _silu(x):
    return x * jax.nn.sigmoid(x)


def _mod_row(i):
    return jnp.where(i < N_CTX_BLOCKS, 0, 1 + (i - N_CTX_BLOCKS) // LAT_BLOCKS_PER_BATCH)


def _mod_kernel(ct_ref, w_ref, b_ref, o_ref):
    s = _silu(ct_ref[...])
    w = w_ref[...]
    rows = [jnp.sum(w * s[:, r:r + 1], axis=0, keepdims=True) for r in range(3)]
    rows.append(jnp.zeros((5, w.shape[1]), F32))
    o_ref[...] = jnp.concatenate(rows, axis=0) + b_ref[...]


def _modulation(c, c_ctx, w_mod, b_mod):
    cvec = jnp.concatenate([c_ctx[None, :], c, jnp.zeros((5, D_MODEL), F32)], axis=0)
    nb = 1024
    out = pl.pallas_call(
        _mod_kernel,
        grid=(6 * D_MODEL // nb,),
        in_specs=[pl.BlockSpec((D_MODEL, 8), lambda j: (0, 0)),
                  pl.BlockSpec((D_MODEL, nb), lambda j: (0, j)),
                  pl.BlockSpec((1, nb), lambda j: (0, j))],
        out_specs=pl.BlockSpec((8, nb), lambda j: (0, j)),
        out_shape=jax.ShapeDtypeStruct((8, 6 * D_MODEL), F32),
        compiler_params=_cparams(("arbitrary",)),
        name="modulation",
    )(cvec.T, w_mod, b_mod[None, :])
    return out.reshape(8, 6, D_MODEL)


def _ctx_or_lat(ctx_ref, lat_ref):
    return jnp.where(pl.program_id(0) < N_CTX_BLOCKS, ctx_ref[...], lat_ref[...])


def _split_specs(width):
    return [pl.BlockSpec((TB, width), lambda i: (jnp.minimum(i, N_CTX_BLOCKS - 1), 0)),
            pl.BlockSpec((TB, width), lambda i: (jnp.maximum(i - N_CTX_BLOCKS, 0), 0))]


def _inproj_kernel(xc_ref, xl_ref, mod_ref, nw_ref, w_ref, wgt_ref,
                   qa_ref, ka_ref, va_ref, qm_ref, km_ref, vm_ref, om_ref, gt_ref):
    x = _ctx_or_lat(xc_ref, xl_ref)
    y = x * lax.rsqrt(jnp.mean(x * x, axis=-1, keepdims=True) + EPS) * nw_ref[...]
    hn = y * (1.0 + mod_ref[1:2, :]) + mod_ref[0:1, :]
    hb = hn.astype(BF16)
    qa_ref[...] = _dot(hb, w_ref[:, 0:512])
    ka_ref[...] = _dot(hb, w_ref[:, 512:640])
    va_ref[...] = _dot(hb, w_ref[:, 640:768])
    qm_ref[...] = _dot(hb, w_ref[:, 768:1280]).astype(BF16)
    km_ref[...] = (_dot(hb, w_ref[:, 1280:1792]) * (DH_M ** -0.5)).astype(BF16)
    vm_ref[...] = _dot(hb, w_ref[:, 1792:2304]).astype(BF16)
    om_ref[...] = _dot(hb, w_ref[:, 2304:2816])
    gt_ref[...] = lax.dot_general(wgt_ref[...], hb, _NT, preferred_element_type=F32)


def _inproj(x_ctx, x_lat, mod3, norm1_w, w_main, w_gates_t):
    tok = lambda w: pl.BlockSpec((TB, w), lambda i: (i, 0))
    full = lambda a: pl.BlockSpec(a.shape, lambda i: (0,) * a.ndim)
    sd = lambda w, dt: jax.ShapeDtypeStruct((T_ALL, w), dt)
    return pl.pallas_call(
        _inproj_kernel,
        grid=(T_ALL // TB,),
        in_specs=_split_specs(D_MODEL) + [
                  pl.BlockSpec((None, 6, D_MODEL), lambda i: (_mod_row(i), 0, 0)),
                  full(norm1_w), full(w_main), full(w_gates_t)],
        out_specs=[tok(512), tok(128), tok(128), tok(512), tok(512), tok(512), tok(512),
                   pl.BlockSpec((16, TB), lambda i: (0, i))],
        out_shape=[sd(512, F32), sd(128, F32), sd(128, F32), sd(512, BF16), sd(512, BF16),
                   sd(512, BF16), sd(512, F32), jax.ShapeDtypeStruct((16, T_ALL), F32)],
        compiler_params=_cparams(("arbitrary",)),
        name="inproj",
    )(x_ctx, x_lat, mod3, norm1_w, w_main, w_gates_t)


def _head_norm(x, ind, ind_t, w_row):
    ss = _dot3(x * x, ind)
    inv = lax.rsqrt(ss * (1.0 / DH_ATT) + EPS)
    return x * _dot3(inv, ind_t) * w_row


def _rope(x, cos, sin_signed):
    lane = lax.broadcasted_iota(jnp.int32, x.shape, 1)
    partner = jnp.where((lane % 32) < 16, pltpu.roll(x, 128 - 16, 1), pltpu.roll(x, 16, 1))
    return x * cos + partner * sin_signed


def _qkprep_kernel(*refs, rope):
    if rope:
        qa_ref, ka_ref, qw_ref, kw_ref, iq_ref, iqt_ref, ik_ref, ikt_ref, cos_ref, sin_ref, qn_ref, kn_ref = refs
    else:
        qa_ref, ka_ref, qw_ref, kw_ref, iq_ref, iqt_ref, ik_ref, ikt_ref, qn_ref, kn_ref = refs
    qn = _head_norm(qa_ref[...], iq_ref[...], iqt_ref[...], qw_ref[...])
    kn = _head_norm(ka_ref[...], ik_ref[...], ikt_ref[...], kw_ref[...])
    if rope:
        cos, sin = cos_ref[...], sin_ref[...]
        qn = jnp.concatenate([_rope(qn[:, 128 * j:128 * (j + 1)], cos, sin) for j in range(4)], axis=1)
        kn = _rope(kn, cos, sin)
    qn_ref[...] = qn.astype(BF16)
    kn_ref[...] = kn


def _qkprep(qa, ka, qw, kw, inds, rope_tabs, *, first_block, n_blocks):
    full = lambda a: pl.BlockSpec(a.shape, lambda i: (0,) * a.ndim)
    tok = lambda w: pl.BlockSpec((TB, w), lambda i: (i + first_block, 0))
    in_specs = [tok(512), tok(128), full(qw), full(kw)] + [full(a) for a in inds]
    args = [qa, ka, qw, kw] + list(inds)
    if rope_tabs is not None:
        in_specs += [pl.BlockSpec((TB, 128), lambda i: (i % LAT_BLOCKS_PER_BATCH, 0))] * 2
        args += list(rope_tabs)
    n_tok = n_blocks * TB
    return pl.pallas_call(
        functools.partial(_qkprep_kernel, rope=rope_tabs is not None),
        grid=(n_blocks,),
        in_specs=in_specs,
        out_specs=[pl.BlockSpec((TB, 512), lambda i: (i, 0)), pl.BlockSpec((TB, 128), lambda i: (i, 0))],
        out_shape=[jax.ShapeDtypeStruct((n_tok, 512), BF16), jax.ShapeDtypeStruct((n_tok, 128), F32)],
        compiler_params=_cparams(("arbitrary",)),
        name="qkprep_lat" if rope_tabs is not None else "qkprep_ctx",
    )(*args)


def _rope_tables():
    t = jnp.arange(DEC_SEQ)
    pos = jnp.stack([t // GRID_W, t % GRID_W], axis=1).astype(F32)
    n_freq = DH_ATT // 4
    inv_freq = ROPE_THETA ** (-jnp.arange(n_freq, dtype=F32) / n_freq)
    ang = pos[:, :, None] * inv_freq
    cos, sin = jnp.cos(ang), jnp.sin(ang)
    cos_h = jnp.stack([cos, cos], axis=2).reshape(DEC_SEQ, DH_ATT)
    sin_h = jnp.stack([-sin, sin], axis=2).reshape(DEC_SEQ, DH_ATT)
    return jnp.tile(cos_h, (1, 2)), jnp.tile(sin_h, (1, 2))


def _attn_kernel(*refs, has_cache):
    if has_cache:
        q_ref, k_ref, v_ref, kc_ref, vc_ref, o_ref = refs
    else:
        q_ref, k_ref, v_ref, o_ref = refs
    q = q_ref[...]
    k = k_ref[...].astype(BF16)
    v = v_ref[...].astype(BF16)
    qb = q.shape[0]
    scale = DH_ATT ** -0.5
    outs = []
    for g in range(N_KV_HEADS):
        kg = k[:, DH_ATT * g:DH_ATT * (g + 1)]
        vg = v[:, DH_ATT * g:DH_ATT * (g + 1)]
        qg = jnp.concatenate([q[:, 256 * g + DH_ATT * j:256 * g + DH_ATT * (j + 1)] for j in range(4)], axis=0)
        s = lax.dot_general(qg, kg, _NT, preferred_element_type=F32) * scale
        m = jnp.max(s, axis=-1, keepdims=True)
        if has_cache:
            kc = kc_ref[g].astype(BF16)
            vc = vc_ref[g].astype(BF16)
            sc = lax.dot_general(qg, kc, _NT, preferred_element_type=F32) * scale
            m = jnp.maximum(m, jnp.max(sc, axis=-1, keepdims=True))
        p = jnp.exp(s - m)
        den = jnp.sum(p, axis=-1, keepdims=True)
        o = _dot(p.astype(BF16), vg)
        if has_cache:
            pc = jnp.exp(sc - m)
            den = den + jnp.sum(pc, axis=-1, keepdims=True)
            o = o + _dot(pc.astype(BF16), vc)
        o = o / den
        outs += [o[qb * j:qb * (j + 1), :] for j in range(4)]
    o_ref[...] = jnp.concatenate(outs, axis=1).astype(BF16)


def _attention(qn, kn, v_all, cache, *, n_batch, seq, v_first_block):
    qblocks = seq // TB
    kblk = lambda off: pl.BlockSpec((seq, 128), lambda b, i: (b + off, 0))
    in_specs = [pl.BlockSpec((TB, 512), lambda b, i: (b * qblocks + i, 0)), kblk(0), kblk(v_first_block)]
    args = [qn, kn, v_all]
    if cache is not None:
        cspec = pl.BlockSpec((None, None, N_KV_HEADS, PAST_LEN, DH_ATT), lambda b, i: (b, 0, 0, 0, 0))
        in_specs += [cspec, cspec]
        args += list(cache)
    return pl.pallas_call(
        functools.partial(_attn_kernel, has_cache=cache is not None),
        grid=(n_batch, qblocks),
        in_specs=in_specs,
        out_specs=pl.BlockSpec((TB, 512), lambda b, i: (b * qblocks + i, 0)),
        out_shape=jax.ShapeDtypeStruct((n_batch * seq, 512), BF16),
        compiler_params=_cparams(("arbitrary", "arbitrary")),
        name="attention_lat" if cache is not None else "attention_ctx",
    )(*args)


def _log_sigmoid(x):
    return jnp.minimum(x, 0.0) - jnp.log1p(jnp.exp(-jnp.abs(x)))


def _col_bcast(cols, j):
    return jnp.broadcast_to(cols[:, j:j + 1], (CHUNK, CHUNK))


def _mlstm_kernel(*refs, has_state, n_chunks):
    n_in = 10 if has_state else 7
    n_out = 1 if has_state else 4
    q_ref, k_ref, v_ref, g_ref, bg_ref, om_ref, nw_ref = refs[:7]
    hm_ref = refs[n_in]
    scratch = refs[n_in + n_out:]
    st = scratch[0:8]
    ms = scratch[8:16]
    hdir = scratch[16:18]
    rows_sc, cmb_sc, bb_sc, kt_sc = scratch[18:22]
    neg_inf = F32(-jnp.inf)
    zeros112 = jnp.zeros((CHUNK - 16, CHUNK), F32)
    sub = lax.broadcasted_iota(jnp.int32, (CHUNK, CHUNK), 0)
    lan = lax.broadcasted_iota(jnp.int32, (CHUNK, CHUNK), 1)
    ones = jnp.ones((CHUNK, CHUNK), BF16)

    for c in range(n_chunks):
        for h in range(N_HEADS_M):
            kc = k_ref[CHUNK * c:CHUNK * (c + 1), DH_M * h:DH_M * (h + 1)]
            kt_sc[c * N_HEADS_M + h] = kc.astype(F32).T.astype(BF16)
        pre = g_ref[c] + bg_ref[...]
        logf = _log_sigmoid(pre)
        for d in range(2):
            within = (lan <= sub) if d == 0 else (lan >= sub)
            bcum = _dot3(logf, ((sub <= lan) if d == 0 else (sub >= lan)).astype(BF16))
            li4 = pre[4 * d:4 * d + 4, :]
            lf4 = logf[8 + 4 * d:12 + 4 * d, :]
            b4 = bcum[8 + 4 * d:12 + 4 * d, :]
            r4 = li4 - b4
            blast4 = b4[:, CHUNK - 1:CHUNK] if d == 0 else b4[:, 0:1]
            wlog4 = blast4 + r4
            wmax4 = jnp.max(wlog4, axis=-1, keepdims=True)
            full = lambda a: jnp.broadcast_to(a, (4, CHUNK))
            rows_sc[2 * c + d] = jnp.concatenate(
                [r4, wlog4, full(blast4), full(wmax4), jnp.zeros((16, CHUNK), F32)], axis=0)
            for h in range(N_HEADS_M):
                cummax = jnp.max(jnp.where(within, r4[h:h + 1, :], neg_inf), axis=-1, keepdims=True)
                cmb_sc[(2 * c + d) * N_HEADS_M + h] = jnp.broadcast_to(cummax, (CHUNK, CHUNK))
                bb_sc[(2 * c + d) * N_HEADS_M + h] = _dot3(jnp.where(within, lf4[h:h + 1, :], 0.0), ones)

    for d in range(2):
        if has_state:
            c0_ref, n0_ref, m0_ref = refs[7:10]
            ncols = jnp.concatenate([n0_ref[d], jnp.zeros((12, DH_M), F32), zeros112], axis=0).T
        for h in range(N_HEADS_M):
            idx = 4 * d + h
            if has_state:
                st[idx][:, 0:DH_M] = c0_ref[d, h]
                st[idx][:, DH_M:] = _col_bcast(ncols, h)
                ms[idx][0:1, :] = m0_ref[idx:idx + 1, :]
            else:
                st[idx][...] = jnp.zeros((DH_M, 2 * DH_M), F32)
                ms[idx][0:1, :] = jnp.full((1, CHUNK), NEG_INIT, F32)

    def chunk_step(i, carry):
        for d in range(2):
            c = i if d == 0 else n_chunks - 1 - i
            r0 = pl.multiple_of(c * CHUNK, CHUNK)
            rows = rows_sc[2 * c + d]
            mask = (lan <= sub) if d == 0 else (lan >= sub)
            for h in range(N_HEADS_M):
                idx = 4 * d + h
                hs = slice(DH_M * h, DH_M * (h + 1))
                qc = q_ref[pl.ds(r0, CHUNK), hs]
                vc = v_ref[pl.ds(r0, CHUNK), hs]
                v_ones = jnp.concatenate([vc, ones], axis=1)
                mp = ms[idx][0:1, :]
                r_row = rows[h:h + 1, :]
                wlog_row = rows[4 + h:5 + h, :]
                blast = rows[8 + h:9 + h, :]
                wmax = rows[12 + h:13 + h, :]
                m_b = jnp.maximum(mp, cmb_sc[(2 * c + d) * N_HEADS_M + h])
                decay = jnp.exp(jnp.where(mask, r_row - m_b, neg_inf))
                gw_b = jnp.exp(mp - m_b)
                en_b = jnp.exp(-(bb_sc[(2 * c + d) * N_HEADS_M + h] + m_b))
                kt = kt_sc[c * N_HEADS_M + h]
                s = _dot(qc, kt) * decay
                s_hi = s.astype(BF16)
                s_lo = (s - s_hi.astype(F32)).astype(BF16)
                sv = _dot(s_hi, v_ones)
                state = st[idx][...]
                qs = _dot(qc, state.astype(BF16))
                num = sv[:, 0:DH_M] + gw_b * qs[:, 0:DH_M]
                den = sv[:, DH_M:] + _dot(s_lo, ones) + gw_b * qs[:, DH_M:]
                hdir[d][pl.ds(r0, CHUNK), hs] = num / jnp.maximum(jnp.abs(den), en_b)
                gend = blast + mp
                mnew = jnp.maximum(gend, wmax)
                w_row = jnp.exp(wlog_row - mnew)
                kw_t = (kt.astype(F32) * w_row).astype(BF16)
                dec = jnp.exp(gend - mnew)
                st[idx][...] = jnp.concatenate([dec, dec], axis=1) * state + _dot(kw_t, v_ones)
                ms[idx][0:1, :] = mnew
        return carry

    lax.fori_loop(0, n_chunks, chunk_step, 0)

    for h in range(N_HEADS_M):
        hs = slice(DH_M * h, DH_M * (h + 1))
        hh = hdir[0][:, hs] + hdir[1][:, hs]
        y = hh * lax.rsqrt(jnp.mean(hh * hh, axis=-1, keepdims=True) + EPS) * nw_ref[:, hs]
        hm_ref[:, hs] = (jax.nn.sigmoid(om_ref[:, hs]) * y).astype(BF16)
    if not has_state:
        c_ref, n_ref, m_ref = refs[n_in + 1:n_in + 4]
        for d in range(2):
            for h in range(N_HEADS_M):
                idx = 4 * d + h
                c_ref[d, h] = st[idx][:, 0:DH_M]
                n_ref[d, h:h + 1, :] = st[idx][:, DH_M:].T[0:1, :]
                m_ref[idx:idx + 1, :] = ms[idx][0:1, :]


def _mlstm(qm, km, vm, g3, bg, om, nw, state, *, n_batch, seq, first_block):
    n_chunks = seq // CHUNK
    rb = seq // TB
    tok = lambda: pl.BlockSpec((seq, 512), lambda b: (b + first_block // rb, 0))
    full = lambda a: pl.BlockSpec(a.shape, lambda b: (0,) * a.ndim)
    in_specs = [tok(), tok(), tok(),
                pl.BlockSpec((n_chunks, 16, CHUNK), lambda b: (b + first_block // rb, 0, 0)),
                full(bg), tok(), full(nw)]
    args = [qm, km, vm, g3, bg, om, nw]
    hm_spec = pl.BlockSpec((seq, 512), lambda b: (b, 0))
    hm_shape = jax.ShapeDtypeStruct((n_batch * seq, 512), BF16)
    scratch = ([pltpu.VMEM((DH_M, 2 * DH_M), F32)] * 8 + [pltpu.VMEM((8, CHUNK), F32)] * 8
               + [pltpu.VMEM((seq, 512), F32)] * 2
               + [pltpu.VMEM((2 * n_chunks, 32, CHUNK), F32),
                  pltpu.VMEM((8 * n_chunks, CHUNK, CHUNK), F32),
                  pltpu.VMEM((8 * n_chunks, CHUNK, CHUNK), F32),
                  pltpu.VMEM((N_HEADS_M * n_chunks, DH_M, CHUNK), BF16)])
    if state is not None:
        c0, n0, m0 = state
        in_specs += [pl.BlockSpec((None, None, 2, N_HEADS_M, DH_M, DH_M), lambda b: (b, 0, 0, 0, 0, 0)),
                     pl.BlockSpec((None, None, 2, N_HEADS_M, DH_M), lambda b: (b, 0, 0, 0, 0)),
                     pl.BlockSpec((None, 8, CHUNK), lambda b: (b, 0, 0))]
        args += [c0, n0, m0]
        out_specs, out_shape = hm_spec, hm_shape
    else:
        out_specs = [hm_spec,
                     pl.BlockSpec((None, 2, N_HEADS_M, DH_M, DH_M), lambda b: (b, 0, 0, 0, 0)),
                     pl.BlockSpec((None, 2, N_HEADS_M, DH_M), lambda b: (b, 0, 0, 0)),
                     pl.BlockSpec((None, 8, CHUNK), lambda b: (b, 0, 0))]
        out_shape = [hm_shape,
                     jax.ShapeDtypeStruct((n_batch, 2, N_HEADS_M, DH_M, DH_M), F32),
                     jax.ShapeDtypeStruct((n_batch, 2, N_HEADS_M, DH_M), F32),
                     jax.ShapeDtypeStruct((n_batch, 8, CHUNK), F32)]
    return pl.pallas_call(
        functools.partial(_mlstm_kernel, has_state=state is not None, n_chunks=n_chunks),
        grid=(n_batch,),
        in_specs=in_specs,
        out_specs=out_specs,
        out_shape=out_shape,
        scratch_shapes=scratch,
        compiler_params=_cparams(("arbitrary",)),
        name="mlstm_lat" if state is not None else "mlstm_ctx",
    )(*args)


def _outproj_kernel(attc_ref, attl_ref, hmc_ref, hml_ref, xc_ref, xl_ref, mod_ref, nw_ref, wo_ref,
                    rwh_ref, rwl_ref, wsg_ref, wsu_ref, wsd_ref, base_ref, hn_ref, lt_ref):
    y = (_dot(_ctx_or_lat(attc_ref, attl_ref), wo_ref[0:D_ATT, :])
         + _dot(_ctx_or_lat(hmc_ref, hml_ref), wo_ref[D_ATT:, :]))
    x1 = _ctx_or_lat(xc_ref, xl_ref) + mod_ref[2:3, :] * y
    z = x1 * lax.rsqrt(jnp.mean(x1 * x1, axis=-1, keepdims=True) + EPS) * nw_ref[...]
    hn = z * (1.0 + mod_ref[4:5, :]) + mod_ref[3:4, :]
    hb = hn.astype(BF16)
    hl = (hn - hb.astype(F32)).astype(BF16)
    nt = lambda w, t: lax.dot_general(w, t, _NT, preferred_element_type=F32)
    lt_ref[...] = nt(rwh_ref[...], hb) + nt(rwl_ref[...], hb) + nt(rwh_ref[...], hl)
    a = _silu(_dot(hb, wsg_ref[...])) * _dot(hb, wsu_ref[...])
    shared = _dot(a.astype(BF16), wsd_ref[...])
    base_ref[...] = x1 + mod_ref[5:6, :] * shared
    packed = pltpu.pack_elementwise([hn[:, :512], hn[:, 512:]], packed_dtype=BF16)
    for c in range(4):
        hn_ref[pl.ds(c, TB, stride=4), :] = packed[:, 128 * c:128 * (c + 1)]


def _outproj(att, hm, x, mod3, norm2_w, w_out, rw_hi, rw_lo, wsg, wsu, wsd):
    tok = lambda w: pl.BlockSpec((TB, w), lambda i: (i, 0))
    full = lambda a: pl.BlockSpec(a.shape, lambda i: (0,) * a.ndim)
    return pl.pallas_call(
        _outproj_kernel,
        grid=(T_ALL // TB,),
        in_specs=_split_specs(512) + _split_specs(512) + _split_specs(D_MODEL) + [
                  pl.BlockSpec((None, 6, D_MODEL), lambda i: (_mod_row(i), 0, 0)),
                  full(norm2_w), full(w_out), full(rw_hi), full(rw_lo), full(wsg), full(wsu), full(wsd)],
        out_specs=[tok(D_MODEL), pl.BlockSpec((4 * TB, 128), lambda i: (i, 0)),
                   pl.BlockSpec((N_EXPERTS, TB), lambda i: (0, i))],
        out_shape=[jax.ShapeDtypeStruct((T_ALL, D_MODEL), F32),
                   jax.ShapeDtypeStruct((4 * T_ALL, 128), jnp.uint32),
                   jax.ShapeDtypeStruct((N_EXPERTS, T_ALL), F32)],
        compiler_params=_cparams(("arbitrary",)),
        name="outproj",
    )(*att, *hm, *x, mod3, norm2_w, w_out, rw_hi, rw_lo, wsg, wsu, wsd)


def _first_max(vals, ids, limit):
    m = functools.reduce(jnp.maximum, [jnp.max(v, axis=0, keepdims=True) for v in vals])
    cand = [jnp.min(jnp.where(v == m, i, limit), axis=0, keepdims=True) for v, i in zip(vals, ids)]
    return m, functools.reduce(jnp.minimum, cand)


def _router_kernel(lt_ref, bias_ref, gate_ref, idx_ref):
    n = lt_ref.shape[1]
    score = jax.nn.sigmoid(lt_ref[...])
    biased = score + bias_ref[...]
    sub = lax.broadcasted_iota(jnp.int32, (8, n), 0).astype(F32)
    neg_inf = F32(-jnp.inf)
    slabs = [biased[8 * g:8 * (g + 1), :] for g in range(N_GROUPS)]
    gs = []
    for sl in slabs:
        m1, i1 = _first_max([sl], [sub], 8.0)
        m2 = jnp.max(jnp.where(sub == i1, neg_inf, sl), axis=0, keepdims=True)
        gs.append(m1 + m2)
    cur = jnp.concatenate(gs, axis=0)
    gsel = jnp.zeros((8, n), F32)
    for _ in range(TOPK_GROUPS):
        _, i = _first_max([cur], [sub], 8.0)
        hit = sub == i
        gsel = jnp.where(hit, 1.0, gsel)
        cur = jnp.where(hit, neg_inf, cur)
    vals = [jnp.where(gsel[g:g + 1, :] > 0.0, slabs[g], neg_inf) for g in range(N_GROUPS)]
    ids = [sub + 8.0 * g for g in range(N_GROUPS)]
    picked = [jnp.zeros((8, n), F32) for _ in range(N_GROUPS)]
    order = []
    for _ in range(TOP_K):
        _, i = _first_max(vals, ids, float(N_EXPERTS))
        order.append(i)
        hits = [idg == i for idg in ids]
        picked = [jnp.where(hh, score[8 * g:8 * (g + 1), :], p) for g, (p, hh) in enumerate(zip(picked, hits))]
        vals = [jnp.where(hh, neg_inf, v) for v, hh in zip(vals, hits)]
    total = functools.reduce(jnp.add, [jnp.sum(p, axis=0, keepdims=True) for p in picked])
    gate_t = jnp.concatenate([p / total * ROUTED_SCALE for p in picked]
                             + [jnp.zeros((128 - N_EXPERTS, n), F32)], axis=0)
    gate_ref[...] = gate_t.T
    idx_ref[...] = jnp.concatenate(order, axis=0).astype(jnp.int32)


def _router(logits_t, bias_col):
    return pl.pallas_call(
        _router_kernel,
        grid=(T_ALL // TB_MOE,),
        in_specs=[pl.BlockSpec((N_EXPERTS, TB_MOE), lambda i: (0, i)),
                  pl.BlockSpec((N_EXPERTS, 1), lambda i: (0, 0))],
        out_specs=[pl.BlockSpec((TB_MOE, 128), lambda i: (i, 0)),
                   pl.BlockSpec((TOP_K, TB_MOE), lambda i: (0, i))],
        out_shape=[jax.ShapeDtypeStruct((T_ALL, 128), F32),
                   jax.ShapeDtypeStruct((TOP_K, T_ALL), jnp.int32)],
        compiler_params=_cparams(("arbitrary",)),
        name="router",
    )(logits_t, bias_col)


def _plan_kernel(idx_ref, slot_ref, cnt_ref, off_ref, pos_sc):
    n_tiles = T_HALF // 128
    eid = lax.broadcasted_iota(jnp.int32, (N_EXPERTS, 128), 0)
    tri = (lax.broadcasted_iota(jnp.int32, (128, 128), 0)
           <= lax.broadcasted_iota(jnp.int32, (128, 128), 1)).astype(BF16)
    carry = jnp.zeros((N_EXPERTS, 1), F32)
    for j in range(n_tiles):
        it = idx_ref[:, 128 * j:128 * (j + 1)]
        sel = jnp.zeros((N_EXPERTS, 128), F32)
        for k in range(TOP_K):
            sel = jnp.where(it[k:k + 1, :] == eid, 1.0, sel)
        inc = _dot(sel.astype(BF16), tri) + carry
        carry = inc[:, 127:128]
        pos_sc[:, 128 * j:128 * (j + 1)] = inc - 1.0
    count = jnp.broadcast_to(carry, (N_EXPERTS, 128))
    padded = jnp.floor((count + (MT - 1.0)) * (1.0 / MT)) * MT
    before = (lax.broadcasted_iota(jnp.int32, (N_EXPERTS, N_EXPERTS), 1)
              < lax.broadcasted_iota(jnp.int32, (N_EXPERTS, N_EXPERTS), 0)).astype(BF16)
    hi, mid, lo = _split3(padded)
    off = _dot(before, hi) + _dot(before, mid) + _dot(before, lo)
    cnt_ref[...] = count.astype(jnp.int32)
    off_ref[...] = off.astype(jnp.int32)

    def slots(j):
        it = idx_ref[:, 128 * j:128 * (j + 1)]
        val = off + pos_sc[:, 128 * j:128 * (j + 1)]
        rows = [jnp.sum(jnp.where(it[k:k + 1, :] == eid, val, 0.0), axis=0, keepdims=True)
                for k in range(TOP_K)]
        return jnp.concatenate(rows, axis=0).astype(jnp.int32)

    for j in range(n_tiles // 2):
        slot_ref[:, 128 * j:128 * (j + 1)] = slots(j) | (slots(j + n_tiles // 2) << 16)


def _plan(idx8):
    return pl.pallas_call(
        _plan_kernel,
        grid=(2,),
        in_specs=[pl.BlockSpec((TOP_K, T_HALF), lambda h: (0, h))],
        out_specs=[pl.BlockSpec((None, TOP_K, T_HALF // 2), lambda h: (h, 0, 0)),
                   pl.BlockSpec((None, N_EXPERTS, 128), lambda h: (h, 0, 0)),
                   pl.BlockSpec((None, N_EXPERTS, 128), lambda h: (h, 0, 0))],
        out_shape=[jax.ShapeDtypeStruct((2, TOP_K, T_HALF // 2), jnp.int32),
                   jax.ShapeDtypeStruct((2, N_EXPERTS, 128), jnp.int32),
                   jax.ShapeDtypeStruct((2, N_EXPERTS, 128), jnp.int32)],
        scratch_shapes=[pltpu.VMEM((N_EXPERTS, T_HALF), F32)],
        compiler_params=_cparams(("arbitrary",)),
        name="plan",
    )(idx8)


def _prepare_half(slot_ref, cnt_ref, off_ref, tok_ref, tile_e, tile_first, elist, h):
    def per_expert(x, carry):
        j, q = carry
        n = cnt_ref[h * N_EXPERTS + x]
        first = off_ref[h * N_EXPERTS + x]
        tiles = (n + MT - 1) // MT
        elist[q] = x

        def mark(i, c):
            tile_e[j + i] = x
            tile_first[j + i] = jnp.where(i == 0, 1, 0)
            return c

        lax.fori_loop(0, tiles, mark, 0)

        def pad(p, c):
            tok_ref[first + p] = T_HALF
            return c

        lax.fori_loop(n, tiles * MT, pad, 0)
        return j + tiles, q + jnp.where(tiles > 0, 1, 0)

    n_tiles, n_live = lax.fori_loop(0, N_EXPERTS, per_expert, (0, 0))
    for extra in range(2):
        tile_e[n_tiles + extra] = 0
        tile_first[n_tiles + extra] = 0

    def pad_tail(p, c):
        tok_ref[n_tiles * MT + p] = T_HALF
        return c

    lax.fori_loop(0, 2 * MT, pad_tail, 0)
    return n_tiles, n_live


def _invert_slots(slot_ref, tok_ref, h):
    half = T_HALF // 2
    for k in range(TOP_K):
        def body(i, carry, k=k):
            words = [slot_ref[(h * TOP_K + k) * half + i * INV_UNROLL + u] for u in range(INV_UNROLL)]
            for u in range(INV_UNROLL):
                t = i * INV_UNROLL + u
                tok_ref[words[u] & 0xFFFF] = t
                tok_ref[lax.shift_right_logical(words[u], 16)] = t + half
            return carry

        lax.fori_loop(0, half // INV_UNROLL, body, 0)


def _moe_kernel(slot_ref, cnt_ref, off_ref, src_hbm, gate_hbm, wg_hbm, wu_hbm, wd_hbm, out_hbm,
                src_v, gate_v, acc_v, wg_l, wu_l, wd_l, wgu_b, wd_b, xbuf0, xbuf1, gbuf0, gbuf1,
                ybuf0, ybuf1, tok_ref, tile_e, tile_first, elist, live, sem, wsem):
    h = pl.program_id(0)
    xbuf, gbuf, ybuf = (xbuf0, xbuf1), (gbuf0, gbuf1), (ybuf0, ybuf1)

    src_cp = pltpu.make_async_copy(src_hbm.at[pl.ds(pl.multiple_of(h * (4 * T_HALF), 8), 4 * T_HALF)],
                                   src_v.at[pl.ds(0, 4 * T_HALF)], sem.at[0])
    gate_cp = pltpu.make_async_copy(gate_hbm.at[pl.ds(pl.multiple_of(h * T_HALF, 8), T_HALF)],
                                    gate_v.at[pl.ds(0, T_HALF)], sem.at[1])
    src_cp.start()
    gate_cp.start()

    def weight_copies(x, slot):
        return [pltpu.make_async_copy(w_hbm.at[x], w_l.at[slot], wsem.at[slot, i])
                for i, (w_hbm, w_l) in enumerate(((wg_hbm, wg_l), (wu_hbm, wu_l), (wd_hbm, wd_l)))]

    n_tiles, n_live = _prepare_half(slot_ref, cnt_ref, off_ref, tok_ref, tile_e, tile_first, elist, h)
    live[0] = 0
    live[1] = n_live

    @pl.when(n_live > 0)
    def _():
        for cp in weight_copies(elist[0], 0):
            cp.start()

    acc_v[...] = jnp.zeros(acc_v.shape, F32)
    zero = jnp.zeros((8, 128), F32)
    src_v[pl.ds(4 * T_HALF, 8), :] = pltpu.pack_elementwise([zero, zero], packed_dtype=BF16)
    gate_v[pl.ds(T_HALF, 8), :] = zero
    ybuf0[...] = jnp.zeros(ybuf0.shape, F32)
    ybuf1[...] = jnp.zeros(ybuf1.shape, F32)
    _invert_slots(slot_ref, tok_ref, h)
    src_cp.wait()
    gate_cp.wait()

    def switch_expert():
        q = live[0]
        slot = q % 2
        for cp in weight_copies(elist[q], slot):
            cp.wait()
        wgu_b[:, 0:D_EXPERT] = wg_l[slot].astype(BF16)
        wgu_b[:, D_EXPERT:] = wu_l[slot].astype(BF16)
        wd_b[...] = wd_l[slot].astype(BF16)
        live[0] = q + 1

        @pl.when(q + 1 < live[1])
        def _():
            for cp in weight_copies(elist[q + 1], 1 - slot):
                cp.start()

    def gather(j, xb, gb):
        base = j * MT
        for m in range(MT):
            t = tok_ref[base + m]
            xb[pl.ds(m, 4, stride=MT_STRIDE), :] = src_v[pl.ds(pl.multiple_of(t * 4, 4), 4), :]
            gb[m:m + 1, :] = gate_v[pl.ds(t, 1), :]

    def compute(j, xb, gb, yb):
        lo, hi = [], []
        for c in range(4):
            words = xb[MT_STRIDE * c:MT_STRIDE * c + MT, :]
            unpack = functools.partial(pltpu.unpack_elementwise, words, packed_dtype=BF16, unpacked_dtype=F32)
            lo.append(unpack(index=0).astype(BF16))
            hi.append(unpack(index=1).astype(BF16))
        x = jnp.concatenate(lo + hi, axis=1)
        hgu = _dot(x, wgu_b[...])
        g = gb[...]
        g_hi = g.astype(BF16)
        g_lo = (g - g_hi.astype(F32)).astype(BF16)
        pick = (lax.broadcasted_iota(jnp.int32, (128, D_EXPERT), 0) == tile_e[j]).astype(BF16)
        g_col = _dot(g_hi, pick) + _dot(g_lo, pick)
        a = _silu(hgu[:, :D_EXPERT]) * hgu[:, D_EXPERT:] * g_col
        y = _dot(a.astype(BF16), wd_b[...])
        for c in range(8):
            yb[MT_STRIDE * c:MT_STRIDE * c + MT, :] = y[:, 128 * c:128 * (c + 1)]

    def scatter(j, yb):
        base = j * MT
        for b in range(MT // RMW_BATCH):
            ms = [b * RMW_BATCH + u for u in range(RMW_BATCH)]
            rows = [pl.ds(pl.multiple_of(tok_ref[base + m] * 8, 8), 8) for m in ms]
            vals = [acc_v[r, :] + yb[pl.ds(m, 8, stride=MT_STRIDE), :] for r, m in zip(rows, ms)]
            for r, v in zip(rows, vals):
                acc_v[r, :] = v

    def step(j, p):
        pl.when(tile_first[j] == 1)(switch_expert)
        gather(j + 1, xbuf[1 - p], gbuf[1 - p])
        compute(j, xbuf[p], gbuf[p], ybuf[p])
        scatter(jnp.maximum(j - 1, 0), ybuf[1 - p])

    gather(0, xbuf[0], gbuf[0])
    n_pairs = (n_tiles + 1) // 2

    def pair(i, carry):
        step(2 * i, 0)
        step(2 * i + 1, 1)
        return carry

    lax.fori_loop(0, n_pairs, pair, 0)
    scatter(jnp.maximum(2 * n_pairs - 1, 0), ybuf[1])

    out_cp = pltpu.make_async_copy(acc_v.at[pl.ds(0, 8 * T_HALF)],
                                   out_hbm.at[pl.ds(pl.multiple_of(h * (8 * T_HALF), 8), 8 * T_HALF)],
                                   sem.at[2])
    out_cp.start()
    out_cp.wait()


def _moe(slot_words, cnt, off, src, gate, wg, wu, wd):
    any_spec = pl.BlockSpec(memory_space=pl.ANY)
    tile_buf = lambda rows, dt: pltpu.VMEM((rows * MT_STRIDE, 128), dt)
    return pl.pallas_call(
        _moe_kernel,
        grid_spec=pltpu.PrefetchScalarGridSpec(
            num_scalar_prefetch=3,
            grid=(2,),
            in_specs=[any_spec] * 5,
            out_specs=any_spec,
            scratch_shapes=[pltpu.VMEM((4 * T_HALF + 8, 128), jnp.uint32),
                            pltpu.VMEM((T_HALF + 8, 128), F32),
                            pltpu.VMEM((8 * T_HALF + 8, 128), F32),
                            pltpu.VMEM((2, D_MODEL, D_EXPERT), F32),
                            pltpu.VMEM((2, D_MODEL, D_EXPERT), F32),
                            pltpu.VMEM((2, D_EXPERT, D_MODEL), F32),
                            pltpu.VMEM((D_MODEL, 2 * D_EXPERT), BF16),
                            pltpu.VMEM((D_EXPERT, D_MODEL), BF16),
                            tile_buf(4, jnp.uint32), tile_buf(4, jnp.uint32),
                            pltpu.VMEM((MT, 128), F32), pltpu.VMEM((MT, 128), F32),
                            tile_buf(8, F32), tile_buf(8, F32),
                            pltpu.SMEM((SLOT_CAP,), jnp.int32),
                            pltpu.SMEM((NT_MAX + 2,), jnp.int32),
                            pltpu.SMEM((NT_MAX + 2,), jnp.int32),
                            pltpu.SMEM((N_EXPERTS,), jnp.int32),
                            pltpu.SMEM((2,), jnp.int32),
                            pltpu.SemaphoreType.DMA((3,)),
                            pltpu.SemaphoreType.DMA((2, 3))]),
        out_shape=jax.ShapeDtypeStruct((8 * T_ALL, 128), F32),
        compiler_params=_cparams(("arbitrary",)),
        name="moe",
    )(slot_words, cnt, off, src, gate, wg, wu, wd)


def _finalize_kernel(acc_ref, base_ref, mod_ref, oc_ref, ol_ref):
    def write(o_ref):
        for c in range(8):
            cs = slice(128 * c, 128 * (c + 1))
            o_ref[:, cs] = base_ref[:, cs] + mod_ref[5:6, cs] * acc_ref[pl.ds(c, TB, stride=8), :]

    is_ctx = pl.program_id(0) < N_CTX_BLOCKS
    pl.when(is_ctx)(lambda: write(oc_ref))
    pl.when(jnp.logical_not(is_ctx))(lambda: write(ol_ref))


def _finalize(acc, base, mod3):
    return pl.pallas_call(
        _finalize_kernel,
        grid=(T_ALL // TB,),
        in_specs=[pl.BlockSpec((8 * TB, 128), lambda i: (i, 0)),
                  pl.BlockSpec((TB, D_MODEL), lambda i: (i, 0)),
                  pl.BlockSpec((None, 6, D_MODEL), lambda i: (_mod_row(i), 0, 0))],
        out_specs=_split_specs(D_MODEL),
        out_shape=[jax.ShapeDtypeStruct((T_CTX, D_MODEL), F32), jax.ShapeDtypeStruct((T_LAT, D_MODEL), F32)],
        compiler_params=_cparams(("arbitrary",)),
        name="finalize",
    )(acc, base, mod3)


def _head_indicators(width):
    head = jnp.arange(width) // DH_ATT
    ind = (head[:, None] == jnp.arange(128)[None, :]).astype(BF16)
    return ind, ind.T


def kernel(x_prompt, x_sample, cache_attn_k, cache_attn_v, state_mlstm_c, state_mlstm_n, state_mlstm_m, c, c_ctx, w_mod, b_mod, norm1_w, norm2_w, w_in, q_norm_w, k_norm_w, b_gates, m_norm_w, w_out, router_w, router_bias, w_gate, w_up, w_down, ws_gate, ws_up, ws_down):
    x = (x_prompt.reshape(T_CTX, D_MODEL), x_sample.reshape(T_LAT, D_MODEL))
    mod3 = _modulation(c, c_ctx, w_mod[0], b_mod[0])

    w_main = w_in[0, :, :P_MAIN].astype(BF16)
    w_gates_t = w_in[0, :, P_MAIN:].T.astype(BF16)
    qa, ka, va, qm, km, vm, om, gt = _inproj(*x, mod3, norm1_w, w_main, w_gates_t)

    qw = jnp.tile(q_norm_w, (1, N_HEADS_ATT))
    kw = jnp.tile(k_norm_w, (1, N_KV_HEADS))
    inds = _head_indicators(512) + _head_indicators(128)
    qn_c, kn_c = _qkprep(qa, ka, qw, kw, inds, None, first_block=0, n_blocks=N_CTX_BLOCKS)
    qn_l, kn_l = _qkprep(qa, ka, qw, kw, inds, _rope_tables(), first_block=N_CTX_BLOCKS,
                         n_blocks=T_LAT // TB)
    att_c = _attention(qn_c, kn_c, va, None, n_batch=BATCH, seq=SEQ, v_first_block=0)
    att_l = _attention(qn_l, kn_l, va, (cache_attn_k, cache_attn_v), n_batch=DEC_BATCH, seq=DEC_SEQ,
                       v_first_block=T_CTX // DEC_SEQ)

    g3 = gt.reshape(16, T_ALL // CHUNK, CHUNK).transpose(1, 0, 2)
    bg = b_gates.reshape(16, 1)
    hm_c, c_new, n_new, m_new = _mlstm(qm, km, vm, g3, bg, om, m_norm_w, None,
                                       n_batch=BATCH, seq=SEQ, first_block=0)
    m0 = jnp.broadcast_to(state_mlstm_m.reshape(DEC_BATCH, 8, 1), (DEC_BATCH, 8, CHUNK))
    hm_l = _mlstm(qm, km, vm, g3, bg, om, m_norm_w, (state_mlstm_c, state_mlstm_n, m0),
                  n_batch=DEC_BATCH, seq=DEC_SEQ, first_block=N_CTX_BLOCKS)

    rw_t = router_w[0].T
    rw_hi = rw_t.astype(BF16)
    rw_lo = (rw_t - rw_hi.astype(F32)).astype(BF16)
    base, hn2, logits_t = _outproj((att_c, att_l), (hm_c, hm_l), x, mod3, norm2_w, w_out[0].astype(BF16),
                                   rw_hi, rw_lo, ws_gate[0].astype(BF16), ws_up[0].astype(BF16),
                                   ws_down[0].astype(BF16))
    gate, idx8 = _router(logits_t, router_bias.reshape(N_EXPERTS, 1))
    slot_words, cnt, off = _plan(idx8)
    acc = _moe(slot_words.reshape(TOP_K * T_HALF), cnt[:, :, 0].reshape(2 * N_EXPERTS),
               off[:, :, 0].reshape(2 * N_EXPERTS), hn2, gate, w_gate[0], w_up[0], w_down[0])
    out_c, out_l = _finalize(acc, base, mod3)

    y_prompt = out_c.reshape(BATCH, SEQ, D_MODEL)
    y_sample = out_l.reshape(DEC_BATCH, DEC_SEQ, D_MODEL)
    to_cache = lambda a: a[:T_CTX].reshape(BATCH, SEQ, N_KV_HEADS, DH_ATT).transpose(0, 2, 1, 3)[:, None]
    new_k = to_cache(kn_c)
    new_v = to_cache(va)
    new_m = m_new[:, :, 0].reshape(BATCH, 1, 2, N_HEADS_M)
    return (y_prompt, y_sample, new_k, new_v, c_new[:, None], n_new[:, None], new_m)
```

```python
import functools

import jax
import jax.numpy as jnp
from jax import lax
from jax.experimental import pallas as pl
from jax.experimental.pallas import tpu as pltpu

F32 = jnp.float32
BF16 = jnp.bfloat16

D_MODEL = 1024
BATCH = 32
SEQ = 256
DEC_BATCH = 2
DEC_SEQ = 1024
PAST_LEN = 256
GRID_W = 64
N_HEADS_ATT = 8
N_KV_HEADS = 2
DH_ATT = 64
D_ATT = 512
ROPE_THETA = 10000.0
N_HEADS_M = 4
DH_M = 128
D_M = 512
CHUNK = 128
N_EXPERTS = 64
TOP_K = 8
N_GROUPS = 8
TOPK_GROUPS = 4
D_EXPERT = 256
ROUTED_SCALE = 2.5
EPS = 1e-6
NEG_INIT = -1e30

T_CTX = BATCH * SEQ
T_LAT = DEC_BATCH * DEC_SEQ
T_ALL = T_CTX + T_LAT
TB = 256
N_CTX_BLOCKS = T_CTX // TB
LAT_BLOCKS_PER_BATCH = DEC_SEQ // TB
TBP = 512
NP_CTX = T_CTX // TBP
NP_LAT_PER_BATCH = DEC_SEQ // TBP
TB_MOE = 1024
T_HALF = T_ALL // 2
MT = 256
MT_STRIDE = MT + 8
NT_MAX = TOP_K * T_HALF // MT + N_EXPERTS
SLOT_CAP = (NT_MAX + 2) * MT
RMW_BATCH = 16
INV_UNROLL = 16
assert SLOT_CAP < 2 ** 16 and (T_HALF // 2) % INV_UNROLL == 0 and MT % RMW_BATCH == 0
P_MAIN = 2816
VMEM_LIMIT = 56 * 1024 * 1024

_NT = (((1,), (1,)), ((), ()))
_TN = (((0,), (0,)), ((), ()))


def _cparams(sem):
    return pltpu.CompilerParams(dimension_semantics=sem, vmem_limit_bytes=VMEM_LIMIT)


def _split3(x):
    hi = x.astype(BF16)
    r1 = x - hi.astype(F32)
    mid = r1.astype(BF16)
    lo = (r1 - mid.astype(F32)).astype(BF16)
    return hi, mid, lo


def _dot(a, b):
    return jnp.dot(a, b, preferred_element_type=F32)


def _dot3(x, m_bf16):
    hi, mid, lo = _split3(x)
    return _dot(hi, m_bf16) + _dot(mid, m_bf16) + _dot(lo, m_bf16)


def _silu(x):
    return x * jax.nn.sigmoid(x)


def _mod_row(i):
    return jnp.where(i < NP_CTX, 0, 1 + (i - NP_CTX) // NP_LAT_PER_BATCH)


def _mod_kernel(ct_ref, w_ref, b_ref, o_ref):
    s = _silu(ct_ref[...])
    w = w_ref[...]
    rows = [jnp.sum(w * s[:, r:r + 1], axis=0, keepdims=True) for r in range(3)]
    rows.append(jnp.zeros((5, w.shape[1]), F32))
    o_ref[...] = jnp.concatenate(rows, axis=0) + b_ref[...]


def _modulation(c, c_ctx, w_mod, b_mod):
    cvec = jnp.concatenate([c_ctx[None, :], c, jnp.zeros((5, D_MODEL), F32)], axis=0)
    nb = 1024
    out = pl.pallas_call(
        _mod_kernel,
        grid=(6 * D_MODEL // nb,),
        in_specs=[pl.BlockSpec((D_MODEL, 8), lambda j: (0, 0)),
                  pl.BlockSpec((D_MODEL, nb), lambda j: (0, j)),
                  pl.BlockSpec((1, nb), lambda j: (0, j))],
        out_specs=pl.BlockSpec((8, nb), lambda j: (0, j)),
        out_shape=jax.ShapeDtypeStruct((8, 6 * D_MODEL), F32),
        compiler_params=_cparams(("arbitrary",)),
        name="modulation",
    )(cvec.T, w_mod, b_mod[None, :])
    return out.reshape(8, 6, D_MODEL)


def _ctx_or_lat(ctx_ref, lat_ref):
    return jnp.where(pl.program_id(0) < NP_CTX, ctx_ref[...], lat_ref[...])


def _split_specs(width):
    return [pl.BlockSpec((TBP, width), lambda i: (jnp.minimum(i, NP_CTX - 1), 0)),
            pl.BlockSpec((TBP, width), lambda i: (jnp.maximum(i - NP_CTX, 0), 0))]


def _inproj_kernel(xc_ref, xl_ref, mod_ref, nw_ref, w_ref, wgt_ref,
                   qa_ref, ka_ref, va_ref, qm_ref, km_ref, vm_ref, om_ref, gt_ref):
    x = _ctx_or_lat(xc_ref, xl_ref)
    y = x * lax.rsqrt(jnp.mean(x * x, axis=-1, keepdims=True) + EPS) * nw_ref[...]
    hn = y * (1.0 + mod_ref[1:2, :]) + mod_ref[0:1, :]
    hb = hn.astype(BF16)
    qa_ref[...] = _dot(hb, w_ref[:, 0:512])
    ka_ref[...] = _dot(hb, w_ref[:, 512:640])
    va_ref[...] = _dot(hb, w_ref[:, 640:768])
    qm_ref[...] = _dot(hb, w_ref[:, 768:1280]).astype(BF16)
    km_ref[...] = (_dot(hb, w_ref[:, 1280:1792]) * (DH_M ** -0.5)).astype(BF16)
    vm_ref[...] = _dot(hb, w_ref[:, 1792:2304]).astype(BF16)
    om_ref[...] = _dot(hb, w_ref[:, 2304:2816])
    gt_ref[...] = lax.dot_general(wgt_ref[...], hb, _NT, preferred_element_type=F32)


def _inproj(x_ctx, x_lat, mod3, norm1_w, w_main, w_gates_t):
    tok = lambda w: pl.BlockSpec((TBP, w), lambda i: (i, 0))
    full = lambda a: pl.BlockSpec(a.shape, lambda i: (0,) * a.ndim)
    sd = lambda w, dt: jax.ShapeDtypeStruct((T_ALL, w), dt)
    return pl.pallas_call(
        _inproj_kernel,
        grid=(T_ALL // TBP,),
        in_specs=_split_specs(D_MODEL) + [
                  pl.BlockSpec((None, 6, D_MODEL), lambda i: (_mod_row(i), 0, 0)),
                  full(norm1_w), full(w_main), full(w_gates_t)],
        out_specs=[tok(512), tok(128), tok(128), tok(512), tok(512), tok(512), tok(512),
                   pl.BlockSpec((16, TBP), lambda i: (0, i))],
        out_shape=[sd(512, F32), sd(128, F32), sd(128, F32), sd(512, BF16), sd(512, BF16),
                   sd(512, BF16), sd(512, F32), jax.ShapeDtypeStruct((16, T_ALL), F32)],
        compiler_params=_cparams(("arbitrary",)),
        name="inproj",
    )(x_ctx, x_lat, mod3, norm1_w, w_main, w_gates_t)


def _head_norm(x, ind, ind_t, w_row):
    ss = _dot3(x * x, ind)
    inv = lax.rsqrt(ss * (1.0 / DH_ATT) + EPS)
    return x * _dot3(inv, ind_t) * w_row


def _rope(x, cos, sin_signed):
    lane = lax.broadcasted_iota(jnp.int32, x.shape, 1)
    partner = jnp.where((lane % 32) < 16, pltpu.roll(x, 128 - 16, 1), pltpu.roll(x, 16, 1))
    return x * cos + partner * sin_signed


def _qkprep_kernel(*refs, rope):
    if rope:
        qa_ref, ka_ref, qw_ref, kw_ref, iq_ref, iqt_ref, ik_ref, ikt_ref, cos_ref, sin_ref, qn_ref, kn_ref = refs
    else:
        qa_ref, ka_ref, qw_ref, kw_ref, iq_ref, iqt_ref, ik_ref, ikt_ref, qn_ref, kn_ref = refs
    qn = _head_norm(qa_ref[...], iq_ref[...], iqt_ref[...], qw_ref[...])
    kn = _head_norm(ka_ref[...], ik_ref[...], ikt_ref[...], kw_ref[...])
    if rope:
        cos, sin = cos_ref[...], sin_ref[...]
        qn = jnp.concatenate([_rope(qn[:, 128 * j:128 * (j + 1)], cos, sin) for j in range(4)], axis=1)
        kn = _rope(kn, cos, sin)
    qn_ref[...] = qn.astype(BF16)
    kn_ref[...] = kn


def _qkprep(qa, ka, qw, kw, inds, rope_tabs, *, first_block, n_blocks):
    full = lambda a: pl.BlockSpec(a.shape, lambda i: (0,) * a.ndim)
    tok = lambda w: pl.BlockSpec((TB, w), lambda i: (i + first_block, 0))
    in_specs = [tok(512), tok(128), full(qw), full(kw)] + [full(a) for a in inds]
    args = [qa, ka, qw, kw] + list(inds)
    if rope_tabs is not None:
        in_specs += [pl.BlockSpec((TB, 128), lambda i: (i % LAT_BLOCKS_PER_BATCH, 0))] * 2
        args += list(rope_tabs)
    n_tok = n_blocks * TB
    return pl.pallas_call(
        functools.partial(_qkprep_kernel, rope=rope_tabs is not None),
        grid=(n_blocks,),
        in_specs=in_specs,
        out_specs=[pl.BlockSpec((TB, 512), lambda i: (i, 0)), pl.BlockSpec((TB, 128), lambda i: (i, 0))],
        out_shape=[jax.ShapeDtypeStruct((n_tok, 512), BF16), jax.ShapeDtypeStruct((n_tok, 128), F32)],
        compiler_params=_cparams(("arbitrary",)),
        name="qkprep_lat" if rope_tabs is not None else "qkprep_ctx",
    )(*args)


def _rope_tables():
    t = jnp.arange(DEC_SEQ)
    pos = jnp.stack([t // GRID_W, t % GRID_W], axis=1).astype(F32)
    n_freq = DH_ATT // 4
    inv_freq = ROPE_THETA ** (-jnp.arange(n_freq, dtype=F32) / n_freq)
    ang = pos[:, :, None] * inv_freq
    cos, sin = jnp.cos(ang), jnp.sin(ang)
    cos_h = jnp.stack([cos, cos], axis=2).reshape(DEC_SEQ, DH_ATT)
    sin_h = jnp.stack([-sin, sin], axis=2).reshape(DEC_SEQ, DH_ATT)
    return jnp.tile(cos_h, (1, 2)), jnp.tile(sin_h, (1, 2))


def _attn_kernel(*refs, has_cache):
    if has_cache:
        q_ref, k_ref, v_ref, kc_ref, vc_ref, o_ref = refs
    else:
        q_ref, k_ref, v_ref, o_ref = refs
    q = q_ref[...]
    k = k_ref[...].astype(BF16)
    v = v_ref[...].astype(BF16)
    qb = q.shape[0]
    scale = DH_ATT ** -0.5
    outs = []
    for g in range(N_KV_HEADS):
        kg = k[:, DH_ATT * g:DH_ATT * (g + 1)]
        vg = v[:, DH_ATT * g:DH_ATT * (g + 1)]
        qg = jnp.concatenate([q[:, 256 * g + DH_ATT * j:256 * g + DH_ATT * (j + 1)] for j in range(4)], axis=0)
        s = lax.dot_general(qg, kg, _NT, preferred_element_type=F32) * scale
        m = jnp.max(s, axis=-1, keepdims=True)
        if has_cache:
            kc = kc_ref[g].astype(BF16)
            vc = vc_ref[g].astype(BF16)
            sc = lax.dot_general(qg, kc, _NT, preferred_element_type=F32) * scale
            m = jnp.maximum(m, jnp.max(sc, axis=-1, keepdims=True))
        p = jnp.exp(s - m)
        den = jnp.sum(p, axis=-1, keepdims=True)
        o = _dot(p.astype(BF16), vg)
        if has_cache:
            pc = jnp.exp(sc - m)
            den = den + jnp.sum(pc, axis=-1, keepdims=True)
            o = o + _dot(pc.astype(BF16), vc)
        o = o / den
        outs += [o[qb * j:qb * (j + 1), :] for j in range(4)]
    o_ref[...] = jnp.concatenate(outs, axis=1).astype(BF16)


def _attention(qn, kn, v_all, cache, *, n_batch, seq, v_first_block):
    qblocks = seq // TB
    kblk = lambda off: pl.BlockSpec((seq, 128), lambda b, i: (b + off, 0))
    in_specs = [pl.BlockSpec((TB, 512), lambda b, i: (b * qblocks + i, 0)), kblk(0), kblk(v_first_block)]
    args = [qn, kn, v_all]
    if cache is not None:
        cspec = pl.BlockSpec((None, None, N_KV_HEADS, PAST_LEN, DH_ATT), lambda b, i: (b, 0, 0, 0, 0))
        in_specs += [cspec, cspec]
        args += list(cache)
    return pl.pallas_call(
        functools.partial(_attn_kernel, has_cache=cache is not None),
        grid=(n_batch, qblocks),
        in_specs=in_specs,
        out_specs=pl.BlockSpec((TB, 512), lambda b, i: (b * qblocks + i, 0)),
        out_shape=jax.ShapeDtypeStruct((n_batch * seq, 512), BF16),
        compiler_params=_cparams(("arbitrary", "arbitrary")),
        name="attention_lat" if cache is not None else "attention_ctx",
    )(*args)


def _log_sigmoid(x):
    return jnp.minimum(x, 0.0) - jnp.log1p(jnp.exp(-jnp.abs(x)))


def _col_bcast(cols, j):
    return jnp.broadcast_to(cols[:, j:j + 1], (CHUNK, CHUNK))


def _mlstm_kernel(*refs, has_state, n_chunks):
    n_in = 10 if has_state else 7
    n_out = 1 if has_state else 4
    q_ref, k_ref, v_ref, g_ref, bg_ref, om_ref, nw_ref = refs[:7]
    hm_ref = refs[n_in]
    scratch = refs[n_in + n_out:]
    st = scratch[0:8]
    ms = scratch[8:16]
    hdir = scratch[16:18]
    rows_sc, cmb_sc, bb_sc, kt_sc = scratch[18:22]
    neg_inf = F32(-jnp.inf)
    zeros112 = jnp.zeros((CHUNK - 16, CHUNK), F32)
    sub = lax.broadcasted_iota(jnp.int32, (CHUNK, CHUNK), 0)
    lan = lax.broadcasted_iota(jnp.int32, (CHUNK, CHUNK), 1)
    ones = jnp.ones((CHUNK, CHUNK), BF16)

    for c in range(n_chunks):
        for h in range(N_HEADS_M):
            kc = k_ref[CHUNK * c:CHUNK * (c + 1), DH_M * h:DH_M * (h + 1)]
            kt_sc[c * N_HEADS_M + h] = kc.astype(F32).T.astype(BF16)
        pre = g_ref[c] + bg_ref[...]
        logf = _log_sigmoid(pre)
        for d in range(2):
            within = (lan <= sub) if d == 0 else (lan >= sub)
            bcum = _dot3(logf, ((sub <= lan) if d == 0 else (sub >= lan)).astype(BF16))
            li4 = pre[4 * d:4 * d + 4, :]
            lf4 = logf[8 + 4 * d:12 + 4 * d, :]
            b4 = bcum[8 + 4 * d:12 + 4 * d, :]
            r4 = li4 - b4
            blast4 = b4[:, CHUNK - 1:CHUNK] if d == 0 else b4[:, 0:1]
            wlog4 = blast4 + r4
            wmax4 = jnp.max(wlog4, axis=-1, keepdims=True)
            full = lambda a: jnp.broadcast_to(a, (4, CHUNK))
            rows_sc[2 * c + d] = jnp.concatenate(
                [r4, wlog4, full(blast4), full(wmax4), jnp.zeros((16, CHUNK), F32)], axis=0)
            for h in range(N_HEADS_M):
                cummax = jnp.max(jnp.where(within, r4[h:h + 1, :], neg_inf), axis=-1, keepdims=True)
                cmb_sc[(2 * c + d) * N_HEADS_M + h] = jnp.broadcast_to(cummax, (CHUNK, CHUNK))
                bb_sc[(2 * c + d) * N_HEADS_M + h] = _dot3(jnp.where(within, lf4[h:h + 1, :], 0.0), ones)

    for d in range(2):
        if has_state:
            c0_ref, n0_ref, m0_ref = refs[7:10]
            ncols = jnp.concatenate([n0_ref[d], jnp.zeros((12, DH_M), F32), zeros112], axis=0).T
        for h in range(N_HEADS_M):
            idx = 4 * d + h
            if has_state:
                st[idx][:, 0:DH_M] = c0_ref[d, h]
                st[idx][:, DH_M:] = _col_bcast(ncols, h)
                ms[idx][0:1, :] = m0_ref[idx:idx + 1, :]
            else:
                st[idx][...] = jnp.zeros((DH_M, 2 * DH_M), F32)
                ms[idx][0:1, :] = jnp.full((1, CHUNK), NEG_INIT, F32)

    def chunk_step(i, carry):
        for d in range(2):
            c = i if d == 0 else n_chunks - 1 - i
            r0 = pl.multiple_of(c * CHUNK, CHUNK)
            rows = rows_sc[2 * c + d]
            mask = (lan <= sub) if d == 0 else (lan >= sub)
            for h in range(N_HEADS_M):
                idx = 4 * d + h
                hs = slice(DH_M * h, DH_M * (h + 1))
                qc = q_ref[pl.ds(r0, CHUNK), hs]
                vc = v_ref[pl.ds(r0, CHUNK), hs]
                v_ones = jnp.concatenate([vc, ones], axis=1)
                mp = ms[idx][0:1, :]
                r_row = rows[h:h + 1, :]
                wlog_row = rows[4 + h:5 + h, :]
                blast = rows[8 + h:9 + h, :]
                wmax = rows[12 + h:13 + h, :]
                m_b = jnp.maximum(mp, cmb_sc[(2 * c + d) * N_HEADS_M + h])
                decay = jnp.exp(jnp.where(mask, r_row - m_b, neg_inf))
                gw_b = jnp.exp(mp - m_b)
                en_b = jnp.exp(-(bb_sc[(2 * c + d) * N_HEADS_M + h] + m_b))
                kt = kt_sc[c * N_HEADS_M + h]
                s = _dot(qc, kt) * decay
                s_hi = s.astype(BF16)
                s_lo = (s - s_hi.astype(F32)).astype(BF16)
                sv = _dot(s_hi, v_ones)
                state = st[idx][...]
                qs = _dot(qc, state.astype(BF16))
                num = sv[:, 0:DH_M] + gw_b * qs[:, 0:DH_M]
                den = sv[:, DH_M:] + _dot(s_lo, ones) + gw_b * qs[:, DH_M:]
                hdir[d][pl.ds(r0, CHUNK), hs] = num / jnp.maximum(jnp.abs(den), en_b)
                gend = blast + mp
                mnew = jnp.maximum(gend, wmax)
                w_row = jnp.exp(wlog_row - mnew)
                kw_t = (kt.astype(F32) * w_row).astype(BF16)
                dec = jnp.exp(gend - mnew)
                st[idx][...] = jnp.concatenate([dec, dec], axis=1) * state + _dot(kw_t, v_ones)
                ms[idx][0:1, :] = mnew
        return carry

    lax.fori_loop(0, n_chunks, chunk_step, 0)

    for h in range(N_HEADS_M):
        hs = slice(DH_M * h, DH_M * (h + 1))
        hh = hdir[0][:, hs] + hdir[1][:, hs]
        y = hh * lax.rsqrt(jnp.mean(hh * hh, axis=-1, keepdims=True) + EPS) * nw_ref[:, hs]
        hm_ref[:, hs] = (jax.nn.sigmoid(om_ref[:, hs]) * y).astype(BF16)
    if not has_state:
        c_ref, n_ref, m_ref = refs[n_in + 1:n_in + 4]
        for d in range(2):
            for h in range(N_HEADS_M):
                idx = 4 * d + h
                c_ref[d, h] = st[idx][:, 0:DH_M]
                n_ref[d, h:h + 1, :] = st[idx][:, DH_M:].T[0:1, :]
                m_ref[idx:idx + 1, :] = ms[idx][0:1, :]


def _mlstm(qm, km, vm, g3, bg, om, nw, state, *, n_batch, seq, first_block):
    n_chunks = seq // CHUNK
    rb = seq // TB
    tok = lambda: pl.BlockSpec((seq, 512), lambda b: (b + first_block // rb, 0))
    full = lambda a: pl.BlockSpec(a.shape, lambda b: (0,) * a.ndim)
    in_specs = [tok(), tok(), tok(),
                pl.BlockSpec((n_chunks, 16, CHUNK), lambda b: (b + first_block // rb, 0, 0)),
                full(bg), tok(), full(nw)]
    args = [qm, km, vm, g3, bg, om, nw]
    hm_spec = pl.BlockSpec((seq, 512), lambda b: (b, 0))
    hm_shape = jax.ShapeDtypeStruct((n_batch * seq, 512), BF16)
    scratch = ([pltpu.VMEM((DH_M, 2 * DH_M), F32)] * 8 + [pltpu.VMEM((8, CHUNK), F32)] * 8
               + [pltpu.VMEM((seq, 512), F32)] * 2
               + [pltpu.VMEM((2 * n_chunks, 32, CHUNK), F32),
                  pltpu.VMEM((8 * n_chunks, CHUNK, CHUNK), F32),
                  pltpu.VMEM((8 * n_chunks, CHUNK, CHUNK), F32),
                  pltpu.VMEM((N_HEADS_M * n_chunks, DH_M, CHUNK), BF16)])
    if state is not None:
        c0, n0, m0 = state
        in_specs += [pl.BlockSpec((None, None, 2, N_HEADS_M, DH_M, DH_M), lambda b: (b, 0, 0, 0, 0, 0)),
                     pl.BlockSpec((None, None, 2, N_HEADS_M, DH_M), lambda b: (b, 0, 0, 0, 0)),
                     pl.BlockSpec((None, 8, CHUNK), lambda b: (b, 0, 0))]
        args += [c0, n0, m0]
        out_specs, out_shape = hm_spec, hm_shape
    else:
        out_specs = [hm_spec,
                     pl.BlockSpec((None, 2, N_HEADS_M, DH_M, DH_M), lambda b: (b, 0, 0, 0, 0)),
                     pl.BlockSpec((None, 2, N_HEADS_M, DH_M), lambda b: (b, 0, 0, 0)),
                     pl.BlockSpec((None, 8, CHUNK), lambda b: (b, 0, 0))]
        out_shape = [hm_shape,
                     jax.ShapeDtypeStruct((n_batch, 2, N_HEADS_M, DH_M, DH_M), F32),
                     jax.ShapeDtypeStruct((n_batch, 2, N_HEADS_M, DH_M), F32),
                     jax.ShapeDtypeStruct((n_batch, 8, CHUNK), F32)]
    return pl.pallas_call(
        functools.partial(_mlstm_kernel, has_state=state is not None, n_chunks=n_chunks),
        grid=(n_batch,),
        in_specs=in_specs,
        out_specs=out_specs,
        out_shape=out_shape,
        scratch_shapes=scratch,
        compiler_params=_cparams(("arbitrary",)),
        name="mlstm_lat" if state is not None else "mlstm_ctx",
    )(*args)


def _outproj_kernel(attc_ref, attl_ref, hmc_ref, hml_ref, xc_ref, xl_ref, mod_ref, nw_ref, wo_ref,
                    rwh_ref, rwl_ref, wsg_ref, wsu_ref, wsd_ref, base_ref, hn_ref, lt_ref):
    y = (_dot(_ctx_or_lat(attc_ref, attl_ref), wo_ref[0:D_ATT, :])
         + _dot(_ctx_or_lat(hmc_ref, hml_ref), wo_ref[D_ATT:, :]))
    x1 = _ctx_or_lat(xc_ref, xl_ref) + mod_ref[2:3, :] * y
    z = x1 * lax.rsqrt(jnp.mean(x1 * x1, axis=-1, keepdims=True) + EPS) * nw_ref[...]
    hn = z * (1.0 + mod_ref[4:5, :]) + mod_ref[3:4, :]
    hb = hn.astype(BF16)
    hl = (hn - hb.astype(F32)).astype(BF16)
    nt = lambda w, t: lax.dot_general(w, t, _NT, preferred_element_type=F32)
    lt_ref[...] = nt(rwh_ref[...], hb) + nt(rwl_ref[...], hb) + nt(rwh_ref[...], hl)
    a = _silu(_dot(hb, wsg_ref[...])) * _dot(hb, wsu_ref[...])
    shared = _dot(a.astype(BF16), wsd_ref[...])
    base_ref[...] = x1 + mod_ref[5:6, :] * shared
    packed = pltpu.pack_elementwise([hn[:, :512], hn[:, 512:]], packed_dtype=BF16)
    for c in range(4):
        hn_ref[pl.ds(c, TBP, stride=4), :] = packed[:, 128 * c:128 * (c + 1)]


def _outproj(att, hm, x, mod3, norm2_w, w_out, rw_hi, rw_lo, wsg, wsu, wsd):
    tok = lambda w: pl.BlockSpec((TBP, w), lambda i: (i, 0))
    full = lambda a: pl.BlockSpec(a.shape, lambda i: (0,) * a.ndim)
    return pl.pallas_call(
        _outproj_kernel,
        grid=(T_ALL // TBP,),
        in_specs=_split_specs(512) + _split_specs(512) + _split_specs(D_MODEL) + [
                  pl.BlockSpec((None, 6, D_MODEL), lambda i: (_mod_row(i), 0, 0)),
                  full(norm2_w), full(w_out), full(rw_hi), full(rw_lo), full(wsg), full(wsu), full(wsd)],
        out_specs=[tok(D_MODEL), pl.BlockSpec((4 * TBP, 128), lambda i: (i, 0)),
                   pl.BlockSpec((N_EXPERTS, TBP), lambda i: (0, i))],
        out_shape=[jax.ShapeDtypeStruct((T_ALL, D_MODEL), F32),
                   jax.ShapeDtypeStruct((4 * T_ALL, 128), jnp.uint32),
                   jax.ShapeDtypeStruct((N_EXPERTS, T_ALL), F32)],
        compiler_params=_cparams(("arbitrary",)),
        name="outproj",
    )(*att, *hm, *x, mod3, norm2_w, w_out, rw_hi, rw_lo, wsg, wsu, wsd)


def _first_max(vals, ids, limit):
    m = functools.reduce(jnp.maximum, [jnp.max(v, axis=0, keepdims=True) for v in vals])
    cand = [jnp.min(jnp.where(v == m, i, limit), axis=0, keepdims=True) for v, i in zip(vals, ids)]
    return m, functools.reduce(jnp.minimum, cand)


def _router_kernel(lt_ref, bias_ref, gate_ref, idx_ref):
    n = lt_ref.shape[1]
    score = jax.nn.sigmoid(lt_ref[...])
    biased = score + bias_ref[...]
    sub = lax.broadcasted_iota(jnp.int32, (8, n), 0).astype(F32)
    neg_inf = F32(-jnp.inf)
    slabs = [biased[8 * g:8 * (g + 1), :] for g in range(N_GROUPS)]
    gs = []
    for sl in slabs:
        m1, i1 = _first_max([sl], [sub], 8.0)
        m2 = jnp.max(jnp.where(sub == i1, neg_inf, sl), axis=0, keepdims=True)
        gs.append(m1 + m2)
    cur = jnp.concatenate(gs, axis=0)
    gsel = jnp.zeros((8, n), F32)
    for _ in range(TOPK_GROUPS):
        _, i = _first_max([cur], [sub], 8.0)
        hit = sub == i
        gsel = jnp.where(hit, 1.0, gsel)
        cur = jnp.where(hit, neg_inf, cur)
    vals = [jnp.where(gsel[g:g + 1, :] > 0.0, slabs[g], neg_inf) for g in range(N_GROUPS)]
    ids = [sub + 8.0 * g for g in range(N_GROUPS)]
    picked = [jnp.zeros((8, n), F32) for _ in range(N_GROUPS)]
    order = []
    for _ in range(TOP_K):
        _, i = _first_max(vals, ids, float(N_EXPERTS))
        order.append(i)
        hits = [idg == i for idg in ids]
        picked = [jnp.where(hh, score[8 * g:8 * (g + 1), :], p) for g, (p, hh) in enumerate(zip(picked, hits))]
        vals = [jnp.where(hh, neg_inf, v) for v, hh in zip(vals, hits)]
    total = functools.reduce(jnp.add, [jnp.sum(p, axis=0, keepdims=True) for p in picked])
    gate_t = jnp.concatenate([p / total * ROUTED_SCALE for p in picked]
                             + [jnp.zeros((128 - N_EXPERTS, n), F32)], axis=0)
    gate_ref[...] = gate_t.T
    idx_ref[...] = jnp.concatenate(order, axis=0).astype(jnp.int32)


def _router(logits_t, bias_col):
    return pl.pallas_call(
        _router_kernel,
        grid=(T_ALL // TB_MOE,),
        in_specs=[pl.BlockSpec((N_EXPERTS, TB_MOE), lambda i: (0, i)),
                  pl.BlockSpec((N_EXPERTS, 1), lambda i: (0, 0))],
        out_specs=[pl.BlockSpec((TB_MOE, 128), lambda i: (i, 0)),
                   pl.BlockSpec((TOP_K, TB_MOE), lambda i: (0, i))],
        out_shape=[jax.ShapeDtypeStruct((T_ALL, 128), F32),
                   jax.ShapeDtypeStruct((TOP_K, T_ALL), jnp.int32)],
        compiler_params=_cparams(("arbitrary",)),
        name="router",
    )(logits_t, bias_col)


def _plan_kernel(idx_ref, slot_ref, cnt_ref, off_ref, pos_sc):
    n_tiles = T_HALF // 128
    eid = lax.broadcasted_iota(jnp.int32, (N_EXPERTS, 128), 0)
    tri = (lax.broadcasted_iota(jnp.int32, (128, 128), 0)
           <= lax.broadcasted_iota(jnp.int32, (128, 128), 1)).astype(BF16)
    carry = jnp.zeros((N_EXPERTS, 1), F32)
    for j in range(n_tiles):
        it = idx_ref[:, 128 * j:128 * (j + 1)]
        sel = jnp.zeros((N_EXPERTS, 128), F32)
        for k in range(TOP_K):
            sel = jnp.where(it[k:k + 1, :] == eid, 1.0, sel)
        inc = _dot(sel.astype(BF16), tri) + carry
        carry = inc[:, 127:128]
        pos_sc[:, 128 * j:128 * (j + 1)] = inc - 1.0
    count = jnp.broadcast_to(carry, (N_EXPERTS, 128))
    padded = jnp.floor((count + (MT - 1.0)) * (1.0 / MT)) * MT
    before = (lax.broadcasted_iota(jnp.int32, (N_EXPERTS, N_EXPERTS), 1)
              < lax.broadcasted_iota(jnp.int32, (N_EXPERTS, N_EXPERTS), 0)).astype(BF16)
    hi, mid, lo = _split3(padded)
    off = _dot(before, hi) + _dot(before, mid) + _dot(before, lo)
    cnt_ref[...] = count.astype(jnp.int32)
    off_ref[...] = off.astype(jnp.int32)

    def slots(j):
        it = idx_ref[:, 128 * j:128 * (j + 1)]
        val = off + pos_sc[:, 128 * j:128 * (j + 1)]
        rows = [jnp.sum(jnp.where(it[k:k + 1, :] == eid, val, 0.0), axis=0, keepdims=True)
                for k in range(TOP_K)]
        return jnp.concatenate(rows, axis=0).astype(jnp.int32)

    for j in range(n_tiles // 2):
        slot_ref[:, 128 * j:128 * (j + 1)] = slots(j) | (slots(j + n_tiles // 2) << 16)


def _plan(idx8):
    return pl.pallas_call(
        _plan_kernel,
        grid=(2,),
        in_specs=[pl.BlockSpec((TOP_K, T_HALF), lambda h: (0, h))],
        out_specs=[pl.BlockSpec((None, TOP_K, T_HALF // 2), lambda h: (h, 0, 0)),
                   pl.BlockSpec((None, N_EXPERTS, 128), lambda h: (h, 0, 0)),
                   pl.BlockSpec((None, N_EXPERTS, 128), lambda h: (h, 0, 0))],
        out_shape=[jax.ShapeDtypeStruct((2, TOP_K, T_HALF // 2), jnp.int32),
                   jax.ShapeDtypeStruct((2, N_EXPERTS, 128), jnp.int32),
                   jax.ShapeDtypeStruct((2, N_EXPERTS, 128), jnp.int32)],
        scratch_shapes=[pltpu.VMEM((N_EXPERTS, T_HALF), F32)],
        compiler_params=_cparams(("arbitrary",)),
        name="plan",
    )(idx8)


def _prepare_half(slot_ref, cnt_ref, off_ref, tok_ref, tile_e, tile_first, elist, h):
    def per_expert(x, carry):
        j, q = carry
        n = cnt_ref[h * N_EXPERTS + x]
        first = off_ref[h * N_EXPERTS + x]
        tiles = (n + MT - 1) // MT
        elist[q] = x

        def mark(i, c):
            tile_e[j + i] = x
            tile_first[j + i] = jnp.where(i == 0, 1, 0)
            return c

        lax.fori_loop(0, tiles, mark, 0)

        def pad(p, c):
            tok_ref[first + p] = T_HALF
            return c

        lax.fori_loop(n, tiles * MT, pad, 0)
        return j + tiles, q + jnp.where(tiles > 0, 1, 0)

    n_tiles, n_live = lax.fori_loop(0, N_EXPERTS, per_expert, (0, 0))
    for extra in range(2):
        tile_e[n_tiles + extra] = 0
        tile_first[n_tiles + extra] = 0

    def pad_tail(p, c):
        tok_ref[n_tiles * MT + p] = T_HALF
        return c

    lax.fori_loop(0, 2 * MT, pad_tail, 0)
    return n_tiles, n_live


def _invert_slots(slot_ref, tok_ref, h):
    half = T_HALF // 2
    for k in range(TOP_K):
        def body(i, carry, k=k):
            words = [slot_ref[(h * TOP_K + k) * half + i * INV_UNROLL + u] for u in range(INV_UNROLL)]
            for u in range(INV_UNROLL):
                t = i * INV_UNROLL + u
                tok_ref[words[u] & 0xFFFF] = t
                tok_ref[lax.shift_right_logical(words[u], 16)] = t + half
            return carry

        lax.fori_loop(0, half // INV_UNROLL, body, 0)


def _moe_kernel(slot_ref, cnt_ref, off_ref, src_hbm, gate_hbm, wg_hbm, wu_hbm, wd_hbm, out_hbm,
                src_v, gate_v, acc_v, wg_l, wu_l, wd_l, wgu_b, wd_b, xbuf0, xbuf1, gbuf0, gbuf1,
                ybuf0, ybuf1, tok_ref, tile_e, tile_first, elist, live, sem, wsem):
    h = pl.program_id(0)
    xbuf, gbuf, ybuf = (xbuf0, xbuf1), (gbuf0, gbuf1), (ybuf0, ybuf1)

    src_cp = pltpu.make_async_copy(src_hbm.at[pl.ds(pl.multiple_of(h * (4 * T_HALF), 8), 4 * T_HALF)],
                                   src_v.at[pl.ds(0, 4 * T_HALF)], sem.at[0])
    gate_cp = pltpu.make_async_copy(gate_hbm.at[pl.ds(pl.multiple_of(h * T_HALF, 8), T_HALF)],
                                    gate_v.at[pl.ds(0, T_HALF)], sem.at[1])
    src_cp.start()
    gate_cp.start()

    def weight_copies(x, slot):
        return [pltpu.make_async_copy(w_hbm.at[x], w_l.at[slot], wsem.at[slot, i])
                for i, (w_hbm, w_l) in enumerate(((wg_hbm, wg_l), (wu_hbm, wu_l), (wd_hbm, wd_l)))]

    n_tiles, n_live = _prepare_half(slot_ref, cnt_ref, off_ref, tok_ref, tile_e, tile_first, elist, h)
    live[0] = 0
    live[1] = n_live

    @pl.when(n_live > 0)
    def _():
        for cp in weight_copies(elist[0], 0):
            cp.start()

    acc_v[...] = jnp.zeros(acc_v.shape, F32)
    zero = jnp.zeros((8, 128), F32)
    src_v[pl.ds(4 * T_HALF, 8), :] = pltpu.pack_elementwise([zero, zero], packed_dtype=BF16)
    gate_v[pl.ds(T_HALF, 8), :] = zero
    ybuf0[...] = jnp.zeros(ybuf0.shape, F32)
    ybuf1[...] = jnp.zeros(ybuf1.shape, F32)
    _invert_slots(slot_ref, tok_ref, h)
    src_cp.wait()
    gate_cp.wait()

    def switch_expert():
        q = live[0]
        slot = q % 2
        for cp in weight_copies(elist[q], slot):
            cp.wait()
        wgu_b[:, 0:D_EXPERT] = wg_l[slot].astype(BF16)
        wgu_b[:, D_EXPERT:] = wu_l[slot].astype(BF16)
        wd_b[...] = wd_l[slot].astype(BF16)
        live[0] = q + 1

        @pl.when(q + 1 < live[1])
        def _():
            for cp in weight_copies(elist[q + 1], 1 - slot):
                cp.start()

    quarter = MT // 4

    def gather(j, xb, gb, part=None):
        base = j * MT
        for m in range(MT) if part is None else range(part * quarter, (part + 1) * quarter):
            t = tok_ref[base + m]
            xb[pl.ds(m, 4, stride=MT_STRIDE), :] = src_v[pl.ds(pl.multiple_of(t * 4, 4), 4), :]
            gb[m:m + 1, :] = gate_v[pl.ds(t, 1), :]

    def scatter(j, yb, part=None):
        base = j * MT
        batches = MT // RMW_BATCH
        per_part = batches // 4
        for b in range(batches) if part is None else range(part * per_part, (part + 1) * per_part):
            ms = [b * RMW_BATCH + u for u in range(RMW_BATCH)]
            rows = [pl.ds(pl.multiple_of(tok_ref[base + m] * 8, 8), 8) for m in ms]
            vals = [acc_v[r, :] + yb[pl.ds(m, 8, stride=MT_STRIDE), :] for r, m in zip(rows, ms)]
            for r, v in zip(rows, vals):
                acc_v[r, :] = v

    def step(j, p):
        pl.when(tile_first[j] == 1)(switch_expert)
        xb, gb, yb = xbuf[p], gbuf[p], ybuf[p]
        nxt = (j + 1, xbuf[1 - p], gbuf[1 - p])
        prv = (jnp.maximum(j - 1, 0), ybuf[1 - p])
        lo, hi = [], []
        for c in range(4):
            words = xb[MT_STRIDE * c:MT_STRIDE * c + MT, :]
            unpack = functools.partial(pltpu.unpack_elementwise, words, packed_dtype=BF16, unpacked_dtype=F32)
            lo.append(unpack(index=0).astype(BF16))
            hi.append(unpack(index=1).astype(BF16))
        x = jnp.concatenate(lo + hi, axis=1)
        gather(*nxt, part=0)
        h_gate = _dot(x, wgu_b[:, 0:D_EXPERT])
        gather(*nxt, part=1)
        h_up = _dot(x, wgu_b[:, D_EXPERT:])
        gather(*nxt, part=2)
        g = gb[...]
        g_hi = g.astype(BF16)
        g_lo = (g - g_hi.astype(F32)).astype(BF16)
        pick = (lax.broadcasted_iota(jnp.int32, (128, D_EXPERT), 0) == tile_e[j]).astype(BF16)
        g_col = _dot(g_hi, pick) + _dot(g_lo, pick)
        a = (_silu(h_gate) * h_up * g_col).astype(BF16)
        gather(*nxt, part=3)
        scatter(*prv, part=0)
        y_lo = _dot(a, wd_b[:, 0:D_MODEL // 2])
        scatter(*prv, part=1)
        y_hi = _dot(a, wd_b[:, D_MODEL // 2:])
        scatter(*prv, part=2)
        for c in range(4):
            yb[MT_STRIDE * c:MT_STRIDE * c + MT, :] = y_lo[:, 128 * c:128 * (c + 1)]
            yb[MT_STRIDE * (c + 4):MT_STRIDE * (c + 4) + MT, :] = y_hi[:, 128 * c:128 * (c + 1)]
        scatter(*prv, part=3)

    gather(0, xbuf[0], gbuf[0])
    n_pairs = (n_tiles + 1) // 2

    def pair(i, carry):
        step(2 * i, 0)
        step(2 * i + 1, 1)
        return carry

    lax.fori_loop(0, n_pairs, pair, 0)
    scatter(jnp.maximum(2 * n_pairs - 1, 0), ybuf[1])

    out_cp = pltpu.make_async_copy(acc_v.at[pl.ds(0, 8 * T_HALF)],
                                   out_hbm.at[pl.ds(pl.multiple_of(h * (8 * T_HALF), 8), 8 * T_HALF)],
                                   sem.at[2])
    out_cp.start()
    out_cp.wait()


def _moe(slot_words, cnt, off, src, gate, wg, wu, wd):
    any_spec = pl.BlockSpec(memory_space=pl.ANY)
    tile_buf = lambda rows, dt: pltpu.VMEM((rows * MT_STRIDE, 128), dt)
    return pl.pallas_call(
        _moe_kernel,
        grid_spec=pltpu.PrefetchScalarGridSpec(
            num_scalar_prefetch=3,
            grid=(2,),
            in_specs=[any_spec] * 5,
            out_specs=any_spec,
            scratch_shapes=[pltpu.VMEM((4 * T_HALF + 8, 128), jnp.uint32),
                            pltpu.VMEM((T_HALF + 8, 128), F32),
                            pltpu.VMEM((8 * T_HALF + 8, 128), F32),
                            pltpu.VMEM((2, D_MODEL, D_EXPERT), F32),
                            pltpu.VMEM((2, D_MODEL, D_EXPERT), F32),
                            pltpu.VMEM((2, D_EXPERT, D_MODEL), F32),
                            pltpu.VMEM((D_MODEL, 2 * D_EXPERT), BF16),
                            pltpu.VMEM((D_EXPERT, D_MODEL), BF16),
                            tile_buf(4, jnp.uint32), tile_buf(4, jnp.uint32),
                            pltpu.VMEM((MT, 128), F32), pltpu.VMEM((MT, 128), F32),
                            tile_buf(8, F32), tile_buf(8, F32),
                            pltpu.SMEM((SLOT_CAP,), jnp.int32),
                            pltpu.SMEM((NT_MAX + 2,), jnp.int32),
                            pltpu.SMEM((NT_MAX + 2,), jnp.int32),
                            pltpu.SMEM((N_EXPERTS,), jnp.int32),
                            pltpu.SMEM((2,), jnp.int32),
                            pltpu.SemaphoreType.DMA((3,)),
                            pltpu.SemaphoreType.DMA((2, 3))]),
        out_shape=jax.ShapeDtypeStruct((8 * T_ALL, 128), F32),
        compiler_params=_cparams(("arbitrary",)),
        name="moe",
    )(slot_words, cnt, off, src, gate, wg, wu, wd)


def _finalize_kernel(acc_ref, base_ref, mod_ref, oc_ref, ol_ref):
    def write(o_ref):
        for c in range(8):
            cs = slice(128 * c, 128 * (c + 1))
            o_ref[:, cs] = base_ref[:, cs] + mod_ref[5:6, cs] * acc_ref[pl.ds(c, TBP, stride=8), :]

    is_ctx = pl.program_id(0) < NP_CTX
    pl.when(is_ctx)(lambda: write(oc_ref))
    pl.when(jnp.logical_not(is_ctx))(lambda: write(ol_ref))


def _finalize(acc, base, mod3):
    return pl.pallas_call(
        _finalize_kernel,
        grid=(T_ALL // TBP,),
        in_specs=[pl.BlockSpec((8 * TBP, 128), lambda i: (i, 0)),
                  pl.BlockSpec((TBP, D_MODEL), lambda i: (i, 0)),
                  pl.BlockSpec((None, 6, D_MODEL), lambda i: (_mod_row(i), 0, 0))],
        out_specs=_split_specs(D_MODEL),
        out_shape=[jax.ShapeDtypeStruct((T_CTX, D_MODEL), F32), jax.ShapeDtypeStruct((T_LAT, D_MODEL), F32)],
        compiler_params=_cparams(("arbitrary",)),
        name="finalize",
    )(acc, base, mod3)


def _head_indicators(width):
    head = jnp.arange(width) // DH_ATT
    ind = (head[:, None] == jnp.arange(128)[None, :]).astype(BF16)
    return ind, ind.T


def kernel(x_prompt, x_sample, cache_attn_k, cache_attn_v, state_mlstm_c, state_mlstm_n, state_mlstm_m, c, c_ctx, w_mod, b_mod, norm1_w, norm2_w, w_in, q_norm_w, k_norm_w, b_gates, m_norm_w, w_out, router_w, router_bias, w_gate, w_up, w_down, ws_gate, ws_up, ws_down):
    x = (x_prompt.reshape(T_CTX, D_MODEL), x_sample.reshape(T_LAT, D_MODEL))
    mod3 = _modulation(c, c_ctx, w_mod[0], b_mod[0])

    w_main = w_in[0, :, :P_MAIN].astype(BF16)
    w_gates_t = w_in[0, :, P_MAIN:].T.astype(BF16)
    qa, ka, va, qm, km, vm, om, gt = _inproj(*x, mod3, norm1_w, w_main, w_gates_t)

    qw = jnp.tile(q_norm_w, (1, N_HEADS_ATT))
    kw = jnp.tile(k_norm_w, (1, N_KV_HEADS))
    inds = _head_indicators(512) + _head_indicators(128)
    qn_c, kn_c = _qkprep(qa, ka, qw, kw, inds, None, first_block=0, n_blocks=N_CTX_BLOCKS)
    qn_l, kn_l = _qkprep(qa, ka, qw, kw, inds, _rope_tables(), first_block=N_CTX_BLOCKS,
                         n_blocks=T_LAT // TB)
    att_c = _attention(qn_c, kn_c, va, None, n_batch=BATCH, seq=SEQ, v_first_block=0)
    att_l = _attention(qn_l, kn_l, va, (cache_attn_k, cache_attn_v), n_batch=DEC_BATCH, seq=DEC_SEQ,
                       v_first_block=T_CTX // DEC_SEQ)

    g3 = gt.reshape(16, T_ALL // CHUNK, CHUNK).transpose(1, 0, 2)
    bg = b_gates.reshape(16, 1)
    hm_c, c_new, n_new, m_new = _mlstm(qm, km, vm, g3, bg, om, m_norm_w, None,
                                       n_batch=BATCH, seq=SEQ, first_block=0)
    m0 = jnp.broadcast_to(state_mlstm_m.reshape(DEC_BATCH, 8, 1), (DEC_BATCH, 8, CHUNK))
    hm_l = _mlstm(qm, km, vm, g3, bg, om, m_norm_w, (state_mlstm_c, state_mlstm_n, m0),
                  n_batch=DEC_BATCH, seq=DEC_SEQ, first_block=N_CTX_BLOCKS)

    rw_t = router_w[0].T
    rw_hi = rw_t.astype(BF16)
    rw_lo = (rw_t - rw_hi.astype(F32)).astype(BF16)
    base, hn2, logits_t = _outproj((att_c, att_l), (hm_c, hm_l), x, mod3, norm2_w, w_out[0].astype(BF16),
                                   rw_hi, rw_lo, ws_gate[0].astype(BF16), ws_up[0].astype(BF16),
                                   ws_down[0].astype(BF16))
    gate, idx8 = _router(logits_t, router_bias.reshape(N_EXPERTS, 1))
    slot_words, cnt, off = _plan(idx8)
    acc = _moe(slot_words.reshape(TOP_K * T_HALF), cnt[:, :, 0].reshape(2 * N_EXPERTS),
               off[:, :, 0].reshape(2 * N_EXPERTS), hn2, gate, w_gate[0], w_up[0], w_down[0])
    out_c, out_l = _finalize(acc, base, mod3)

    y_prompt = out_c.reshape(BATCH, SEQ, D_MODEL)
    y_sample = out_l.reshape(DEC_BATCH, DEC_SEQ, D_MODEL)
    to_cache = lambda a: a[:T_CTX].reshape(BATCH, SEQ, N_KV_HEADS, DH_ATT).transpose(0, 2, 1, 3)[:, None]
    new_k = to_cache(kn_c)
    new_v = to_cache(va)
    new_m = m_new[:, :, 0].reshape(BATCH, 1, 2, N_HEADS_M)
    return (y_prompt, y_sample, new_k, new_v, c_new[:, None], n_new[:, None], new_m)
```

```python
import functools

import jax
import jax.numpy as jnp
from jax import lax
from jax.experimental import pallas as pl
from jax.experimental.pallas import tpu as pltpu

F32 = jnp.float32
BF16 = jnp.bfloat16

D_MODEL = 1024
BATCH = 32
SEQ = 256
DEC_BATCH = 2
DEC_SEQ = 1024
PAST_LEN = 256
GRID_W = 64
N_HEADS_ATT = 8
N_KV_HEADS = 2
DH_ATT = 64
D_ATT = 512
ROPE_THETA = 10000.0
N_HEADS_M = 4
DH_M = 128
D_M = 512
CHUNK = 128
N_EXPERTS = 64
TOP_K = 8
N_GROUPS = 8
TOPK_GROUPS = 4
D_EXPERT = 256
ROUTED_SCALE = 2.5
EPS = 1e-6
NEG_INIT = -1e30

T_CTX = BATCH * SEQ
T_LAT = DEC_BATCH * DEC_SEQ
T_ALL = T_CTX + T_LAT
TB = 256
N_CTX_BLOCKS = T_CTX // TB
LAT_BLOCKS_PER_BATCH = DEC_SEQ // TB
TBP = 512
NP_CTX = T_CTX // TBP
NP_LAT_PER_BATCH = DEC_SEQ // TBP
TB_MOE = 1024
T_HALF = T_ALL // 2
MT = 256
MT_STRIDE = MT + 8
NT_MAX = TOP_K * T_HALF // MT + N_EXPERTS
SLOT_CAP = (NT_MAX + 2) * MT
RMW_BATCH = 16
INV_UNROLL = 16
assert SLOT_CAP < 2 ** 16 and (T_HALF // 2) % INV_UNROLL == 0 and MT % RMW_BATCH == 0
P_MAIN = 2816
VMEM_LIMIT = 56 * 1024 * 1024

_NT = (((1,), (1,)), ((), ()))
_TN = (((0,), (0,)), ((), ()))


def _cparams(sem):
    return pltpu.CompilerParams(dimension_semantics=sem, vmem_limit_bytes=VMEM_LIMIT)


def _split3(x):
    hi = x.astype(BF16)
    r1 = x - hi.astype(F32)
    mid = r1.astype(BF16)
    lo = (r1 - mid.astype(F32)).astype(BF16)
    return hi, mid, lo


def _dot(a, b):
    return jnp.dot(a, b, preferred_element_type=F32)


def _dot3(x, m_bf16):
    hi, mid, lo = _split3(x)
    return _dot(hi, m_bf16) + _dot(mid, m_bf16) + _dot(lo, m_bf16)


def _silu(x):
    return x * jax.nn.sigmoid(x)


def _mod_row(i):
    return jnp.where(i < NP_CTX, 0, 1 + (i - NP_CTX) // NP_LAT_PER_BATCH)


def _mod_kernel(ct_ref, w_ref, b_ref, o_ref):
    s = _silu(ct_ref[...])
    w = w_ref[...]
    rows = [jnp.sum(w * s[:, r:r + 1], axis=0, keepdims=True) for r in range(3)]
    rows.append(jnp.zeros((5, w.shape[1]), F32))
    o_ref[...] = jnp.concatenate(rows, axis=0) + b_ref[...]


def _modulation(c, c_ctx, w_mod, b_mod):
    cvec = jnp.concatenate([c_ctx[None, :], c, jnp.zeros((5, D_MODEL), F32)], axis=0)
    nb = 1024
    out = pl.pallas_call(
        _mod_kernel,
        grid=(6 * D_MODEL // nb,),
        in_specs=[pl.BlockSpec((D_MODEL, 8), lambda j: (0, 0)),
                  pl.BlockSpec((D_MODEL, nb), lambda j: (0, j)),
                  pl.BlockSpec((1, nb), lambda j: (0, j))],
        out_specs=pl.BlockSpec((8, nb), lambda j: (0, j)),
        out_shape=jax.ShapeDtypeStruct((8, 6 * D_MODEL), F32),
        compiler_params=_cparams(("arbitrary",)),
        name="modulation",
    )(cvec.T, w_mod, b_mod[None, :])
    return out.reshape(8, 6, D_MODEL)


def _ctx_or_lat(ctx_ref, lat_ref):
    return jnp.where(pl.program_id(0) < NP_CTX, ctx_ref[...], lat_ref[...])


def _split_specs(width):
    return [pl.BlockSpec((TBP, width), lambda i: (jnp.minimum(i, NP_CTX - 1), 0)),
            pl.BlockSpec((TBP, width), lambda i: (jnp.maximum(i - NP_CTX, 0), 0))]


def _inproj_kernel(xc_ref, xl_ref, mod_ref, nw_ref, w_ref, wgt_ref,
                   qa_ref, ka_ref, va_ref, qm_ref, km_ref, vm_ref, om_ref, gt_ref):
    x = _ctx_or_lat(xc_ref, xl_ref)
    y = x * lax.rsqrt(jnp.mean(x * x, axis=-1, keepdims=True) + EPS) * nw_ref[...]
    hn = y * (1.0 + mod_ref[1:2, :]) + mod_ref[0:1, :]
    hb = hn.astype(BF16)
    qa_ref[...] = _dot(hb, w_ref[:, 0:512])
    ka_ref[...] = _dot(hb, w_ref[:, 512:640])
    va_ref[...] = _dot(hb, w_ref[:, 640:768])
    qm_ref[...] = _dot(hb, w_ref[:, 768:1280]).astype(BF16)
    km_ref[...] = (_dot(hb, w_ref[:, 1280:1792]) * (DH_M ** -0.5)).astype(BF16)
    vm_ref[...] = _dot(hb, w_ref[:, 1792:2304]).astype(BF16)
    om_ref[...] = _dot(hb, w_ref[:, 2304:2816])
    gt_ref[...] = lax.dot_general(wgt_ref[...], hb, _NT, preferred_element_type=F32)


def _inproj(x_ctx, x_lat, mod3, norm1_w, w_main, w_gates_t):
    tok = lambda w: pl.BlockSpec((TBP, w), lambda i: (i, 0))
    full = lambda a: pl.BlockSpec(a.shape, lambda i: (0,) * a.ndim)
    sd = lambda w, dt: jax.ShapeDtypeStruct((T_ALL, w), dt)
    return pl.pallas_call(
        _inproj_kernel,
        grid=(T_ALL // TBP,),
        in_specs=_split_specs(D_MODEL) + [
                  pl.BlockSpec((None, 6, D_MODEL), lambda i: (_mod_row(i), 0, 0)),
                  full(norm1_w), full(w_main), full(w_gates_t)],
        out_specs=[tok(512), tok(128), tok(128), tok(512), tok(512), tok(512), tok(512),
                   pl.BlockSpec((16, TBP), lambda i: (0, i))],
        out_shape=[sd(512, F32), sd(128, F32), sd(128, F32), sd(512, BF16), sd(512, BF16),
                   sd(512, BF16), sd(512, F32), jax.ShapeDtypeStruct((16, T_ALL), F32)],
        compiler_params=_cparams(("arbitrary",)),
        name="inproj",
    )(x_ctx, x_lat, mod3, norm1_w, w_main, w_gates_t)


def _head_norm(x, ind, ind_t, w_row):
    ss = _dot3(x * x, ind)
    inv = lax.rsqrt(ss * (1.0 / DH_ATT) + EPS)
    return x * _dot3(inv, ind_t) * w_row


def _rope(x, cos, sin_signed):
    lane = lax.broadcasted_iota(jnp.int32, x.shape, 1)
    partner = jnp.where((lane % 32) < 16, pltpu.roll(x, 128 - 16, 1), pltpu.roll(x, 16, 1))
    return x * cos + partner * sin_signed


def _qkprep_kernel(*refs, rope):
    if rope:
        qa_ref, ka_ref, qw_ref, kw_ref, iq_ref, iqt_ref, ik_ref, ikt_ref, cos_ref, sin_ref, qn_ref, kn_ref = refs
    else:
        qa_ref, ka_ref, qw_ref, kw_ref, iq_ref, iqt_ref, ik_ref, ikt_ref, qn_ref, kn_ref = refs
    qn = _head_norm(qa_ref[...], iq_ref[...], iqt_ref[...], qw_ref[...])
    kn = _head_norm(ka_ref[...], ik_ref[...], ikt_ref[...], kw_ref[...])
    if rope:
        cos, sin = cos_ref[...], sin_ref[...]
        qn = jnp.concatenate([_rope(qn[:, 128 * j:128 * (j + 1)], cos, sin) for j in range(4)], axis=1)
        kn = _rope(kn, cos, sin)
    qn_ref[...] = qn.astype(BF16)
    kn_ref[...] = kn


def _qkprep(qa, ka, qw, kw, inds, rope_tabs, *, first_block, n_blocks):
    full = lambda a: pl.BlockSpec(a.shape, lambda i: (0,) * a.ndim)
    tok = lambda w: pl.BlockSpec((TB, w), lambda i: (i + first_block, 0))
    in_specs = [tok(512), tok(128), full(qw), full(kw)] + [full(a) for a in inds]
    args = [qa, ka, qw, kw] + list(inds)
    if rope_tabs is not None:
        in_specs += [pl.BlockSpec((TB, 128), lambda i: (i % LAT_BLOCKS_PER_BATCH, 0))] * 2
        args += list(rope_tabs)
    n_tok = n_blocks * TB
    return pl.pallas_call(
        functools.partial(_qkprep_kernel, rope=rope_tabs is not None),
        grid=(n_blocks,),
        in_specs=in_specs,
        out_specs=[pl.BlockSpec((TB, 512), lambda i: (i, 0)), pl.BlockSpec((TB, 128), lambda i: (i, 0))],
        out_shape=[jax.ShapeDtypeStruct((n_tok, 512), BF16), jax.ShapeDtypeStruct((n_tok, 128), F32)],
        compiler_params=_cparams(("arbitrary",)),
        name="qkprep_lat" if rope_tabs is not None else "qkprep_ctx",
    )(*args)


def _rope_tables():
    t = jnp.arange(DEC_SEQ)
    pos = jnp.stack([t // GRID_W, t % GRID_W], axis=1).astype(F32)
    n_freq = DH_ATT // 4
    inv_freq = ROPE_THETA ** (-jnp.arange(n_freq, dtype=F32) / n_freq)
    ang = pos[:, :, None] * inv_freq
    cos, sin = jnp.cos(ang), jnp.sin(ang)
    cos_h = jnp.stack([cos, cos], axis=2).reshape(DEC_SEQ, DH_ATT)
    sin_h = jnp.stack([-sin, sin], axis=2).reshape(DEC_SEQ, DH_ATT)
    return jnp.tile(cos_h, (1, 2)), jnp.tile(sin_h, (1, 2))


def _attn_kernel(*refs, has_cache):
    if has_cache:
        q_ref, k_ref, v_ref, kc_ref, vc_ref, o_ref = refs
    else:
        q_ref, k_ref, v_ref, o_ref = refs
    q = q_ref[...]
    k = k_ref[...].astype(BF16)
    v = v_ref[...].astype(BF16)
    qb = q.shape[0]
    scale = DH_ATT ** -0.5
    outs = []
    for g in range(N_KV_HEADS):
        kg = k[:, DH_ATT * g:DH_ATT * (g + 1)]
        vg = v[:, DH_ATT * g:DH_ATT * (g + 1)]
        qg = jnp.concatenate([q[:, 256 * g + DH_ATT * j:256 * g + DH_ATT * (j + 1)] for j in range(4)], axis=0)
        s = lax.dot_general(qg, kg, _NT, preferred_element_type=F32) * scale
        m = jnp.max(s, axis=-1, keepdims=True)
        if has_cache:
            kc = kc_ref[g].astype(BF16)
            vc = vc_ref[g].astype(BF16)
            sc = lax.dot_general(qg, kc, _NT, preferred_element_type=F32) * scale
            m = jnp.maximum(m, jnp.max(sc, axis=-1, keepdims=True))
        p = jnp.exp(s - m)
        den = jnp.sum(p, axis=-1, keepdims=True)
        o = _dot(p.astype(BF16), vg)
        if has_cache:
            pc = jnp.exp(sc - m)
            den = den + jnp.sum(pc, axis=-1, keepdims=True)
            o = o + _dot(pc.astype(BF16), vc)
        o = o / den
        outs += [o[qb * j:qb * (j + 1), :] for j in range(4)]
    o_ref[...] = jnp.concatenate(outs, axis=1).astype(BF16)


def _attention(qn, kn, v_all, cache, *, n_batch, seq, v_first_block):
    qblocks = seq // TB
    kblk = lambda off: pl.BlockSpec((seq, 128), lambda b, i: (b + off, 0))
    in_specs = [pl.BlockSpec((TB, 512), lambda b, i: (b * qblocks + i, 0)), kblk(0), kblk(v_first_block)]
    args = [qn, kn, v_all]
    if cache is not None:
        cspec = pl.BlockSpec((None, None, N_KV_HEADS, PAST_LEN, DH_ATT), lambda b, i: (b, 0, 0, 0, 0))
        in_specs += [cspec, cspec]
        args += list(cache)
    return pl.pallas_call(
        functools.partial(_attn_kernel, has_cache=cache is not None),
        grid=(n_batch, qblocks),
        in_specs=in_specs,
        out_specs=pl.BlockSpec((TB, 512), lambda b, i: (b * qblocks + i, 0)),
        out_shape=jax.ShapeDtypeStruct((n_batch * seq, 512), BF16),
        compiler_params=_cparams(("arbitrary", "arbitrary")),
        name="attention_lat" if cache is not None else "attention_ctx",
    )(*args)


def _log_sigmoid(x):
    return jnp.minimum(x, 0.0) - jnp.log1p(jnp.exp(-jnp.abs(x)))


def _col_bcast(cols, j):
    return jnp.broadcast_to(cols[:, j:j + 1], (CHUNK, CHUNK))


def _mlstm_kernel(*refs, has_state, n_chunks):
    n_in = 10 if has_state else 7
    n_out = 1 if has_state else 4
    q_ref, k_ref, v_ref, g_ref, bg_ref, om_ref, nw_ref = refs[:7]
    hm_ref = refs[n_in]
    scratch = refs[n_in + n_out:]
    st = scratch[0:8]
    ms = scratch[8:16]
    hdir = scratch[16:18]
    rows_sc, cmb_sc, bb_sc, kt_sc = scratch[18:22]
    neg_inf = F32(-jnp.inf)
    zeros112 = jnp.zeros((CHUNK - 16, CHUNK), F32)
    sub = lax.broadcasted_iota(jnp.int32, (CHUNK, CHUNK), 0)
    lan = lax.broadcasted_iota(jnp.int32, (CHUNK, CHUNK), 1)
    ones = jnp.ones((CHUNK, CHUNK), BF16)

    for c in range(n_chunks):
        for h in range(N_HEADS_M):
            kc = k_ref[CHUNK * c:CHUNK * (c + 1), DH_M * h:DH_M * (h + 1)]
            kt_sc[c * N_HEADS_M + h] = kc.astype(F32).T.astype(BF16)
        pre = g_ref[c] + bg_ref[...]
        logf = _log_sigmoid(pre)
        for d in range(2):
            within = (lan <= sub) if d == 0 else (lan >= sub)
            bcum = _dot3(logf, ((sub <= lan) if d == 0 else (sub >= lan)).astype(BF16))
            li4 = pre[4 * d:4 * d + 4, :]
            lf4 = logf[8 + 4 * d:12 + 4 * d, :]
            b4 = bcum[8 + 4 * d:12 + 4 * d, :]
            r4 = li4 - b4
            blast4 = b4[:, CHUNK - 1:CHUNK] if d == 0 else b4[:, 0:1]
            wlog4 = blast4 + r4
            wmax4 = jnp.max(wlog4, axis=-1, keepdims=True)
            full = lambda a: jnp.broadcast_to(a, (4, CHUNK))
            rows_sc[2 * c + d] = jnp.concatenate(
                [r4, wlog4, full(blast4), full(wmax4), jnp.zeros((16, CHUNK), F32)], axis=0)
            for h in range(N_HEADS_M):
                cummax = jnp.max(jnp.where(within, r4[h:h + 1, :], neg_inf), axis=-1, keepdims=True)
                cmb_sc[(2 * c + d) * N_HEADS_M + h] = jnp.broadcast_to(cummax, (CHUNK, CHUNK))
                bb_sc[(2 * c + d) * N_HEADS_M + h] = _dot3(jnp.where(within, lf4[h:h + 1, :], 0.0), ones)

    for d in range(2):
        if has_state:
            c0_ref, n0_ref, m0_ref = refs[7:10]
            ncols = jnp.concatenate([n0_ref[d], jnp.zeros((12, DH_M), F32), zeros112], axis=0).T
        for h in range(N_HEADS_M):
            idx = 4 * d + h
            if has_state:
                st[idx][:, 0:DH_M] = c0_ref[d, h]
                st[idx][:, DH_M:] = _col_bcast(ncols, h)
                ms[idx][0:1, :] = m0_ref[idx:idx + 1, :]
            else:
                st[idx][...] = jnp.zeros((DH_M, 2 * DH_M), F32)
                ms[idx][0:1, :] = jnp.full((1, CHUNK), NEG_INIT, F32)

    def chunk_step(i, carry):
        for d in range(2):
            c = i if d == 0 else n_chunks - 1 - i
            r0 = pl.multiple_of(c * CHUNK, CHUNK)
            rows = rows_sc[2 * c + d]
            mask = (lan <= sub) if d == 0 else (lan >= sub)
            heads = []
            for h in range(N_HEADS_M):
                idx = 4 * d + h
                hs = slice(DH_M * h, DH_M * (h + 1))
                qc = q_ref[pl.ds(r0, CHUNK), hs]
                kt = kt_sc[c * N_HEADS_M + h]
                state = st[idx][...]
                heads.append(dict(idx=idx, hs=hs, qc=qc, kt=kt, state=state,
                                  s_raw=_dot(qc, kt), qs=_dot(qc, state.astype(BF16))))
            for h, hd in enumerate(heads):
                mp = ms[hd["idx"]][0:1, :]
                m_b = jnp.maximum(mp, cmb_sc[(2 * c + d) * N_HEADS_M + h])
                s = hd["s_raw"] * jnp.exp(jnp.where(mask, rows[h:h + 1, :] - m_b, neg_inf))
                s_hi = s.astype(BF16)
                vc = v_ref[pl.ds(r0, CHUNK), hd["hs"]]
                v_ones = jnp.concatenate([vc, ones], axis=1)
                hd.update(mp=mp, m_b=m_b, v_ones=v_ones, sv=_dot(s_hi, v_ones),
                          s_lo_sum=_dot((s - s_hi.astype(F32)).astype(BF16), ones))
            for h, hd in enumerate(heads):
                mp, m_b = hd["mp"], hd["m_b"]
                gw_b = jnp.exp(mp - m_b)
                en_b = jnp.exp(-(bb_sc[(2 * c + d) * N_HEADS_M + h] + m_b))
                num = hd["sv"][:, 0:DH_M] + gw_b * hd["qs"][:, 0:DH_M]
                den = hd["sv"][:, DH_M:] + hd["s_lo_sum"] + gw_b * hd["qs"][:, DH_M:]
                hdir[d][pl.ds(r0, CHUNK), hd["hs"]] = num / jnp.maximum(jnp.abs(den), en_b)
            for h, hd in enumerate(heads):
                mp = hd["mp"]
                gend = rows[8 + h:9 + h, :] + mp
                mnew = jnp.maximum(gend, rows[12 + h:13 + h, :])
                w_row = jnp.exp(rows[4 + h:5 + h, :] - mnew)
                kw_t = (hd["kt"].astype(F32) * w_row).astype(BF16)
                dec = jnp.exp(gend - mnew)
                st[hd["idx"]][...] = (jnp.concatenate([dec, dec], axis=1) * hd["state"]
                                      + _dot(kw_t, hd["v_ones"]))
                ms[hd["idx"]][0:1, :] = mnew
        return carry

    lax.fori_loop(0, n_chunks, chunk_step, 0)

    for h in range(N_HEADS_M):
        hs = slice(DH_M * h, DH_M * (h + 1))
        hh = hdir[0][:, hs] + hdir[1][:, hs]
        y = hh * lax.rsqrt(jnp.mean(hh * hh, axis=-1, keepdims=True) + EPS) * nw_ref[:, hs]
        hm_ref[:, hs] = (jax.nn.sigmoid(om_ref[:, hs]) * y).astype(BF16)
    if not has_state:
        c_ref, n_ref, m_ref = refs[n_in + 1:n_in + 4]
        for d in range(2):
            for h in range(N_HEADS_M):
                idx = 4 * d + h
                c_ref[d, h] = st[idx][:, 0:DH_M]
                n_ref[d, h:h + 1, :] = st[idx][:, DH_M:].T[0:1, :]
                m_ref[idx:idx + 1, :] = ms[idx][0:1, :]


def _mlstm(qm, km, vm, g3, bg, om, nw, state, *, n_batch, seq, first_block):
    n_chunks = seq // CHUNK
    rb = seq // TB
    tok = lambda: pl.BlockSpec((seq, 512), lambda b: (b + first_block // rb, 0))
    full = lambda a: pl.BlockSpec(a.shape, lambda b: (0,) * a.ndim)
    in_specs = [tok(), tok(), tok(),
                pl.BlockSpec((n_chunks, 16, CHUNK), lambda b: (b + first_block // rb, 0, 0)),
                full(bg), tok(), full(nw)]
    args = [qm, km, vm, g3, bg, om, nw]
    hm_spec = pl.BlockSpec((seq, 512), lambda b: (b, 0))
    hm_shape = jax.ShapeDtypeStruct((n_batch * seq, 512), BF16)
    scratch = ([pltpu.VMEM((DH_M, 2 * DH_M), F32)] * 8 + [pltpu.VMEM((8, CHUNK), F32)] * 8
               + [pltpu.VMEM((seq, 512), F32)] * 2
               + [pltpu.VMEM((2 * n_chunks, 32, CHUNK), F32),
                  pltpu.VMEM((8 * n_chunks, CHUNK, CHUNK), F32),
                  pltpu.VMEM((8 * n_chunks, CHUNK, CHUNK), F32),
                  pltpu.VMEM((N_HEADS_M * n_chunks, DH_M, CHUNK), BF16)])
    if state is not None:
        c0, n0, m0 = state
        in_specs += [pl.BlockSpec((None, None, 2, N_HEADS_M, DH_M, DH_M), lambda b: (b, 0, 0, 0, 0, 0)),
                     pl.BlockSpec((None, None, 2, N_HEADS_M, DH_M), lambda b: (b, 0, 0, 0, 0)),
                     pl.BlockSpec((None, 8, CHUNK), lambda b: (b, 0, 0))]
        args += [c0, n0, m0]
        out_specs, out_shape = hm_spec, hm_shape
    else:
        out_specs = [hm_spec,
                     pl.BlockSpec((None, 2, N_HEADS_M, DH_M, DH_M), lambda b: (b, 0, 0, 0, 0)),
                     pl.BlockSpec((None, 2, N_HEADS_M, DH_M), lambda b: (b, 0, 0, 0)),
                     pl.BlockSpec((None, 8, CHUNK), lambda b: (b, 0, 0))]
        out_shape = [hm_shape,
                     jax.ShapeDtypeStruct((n_batch, 2, N_HEADS_M, DH_M, DH_M), F32),
                     jax.ShapeDtypeStruct((n_batch, 2, N_HEADS_M, DH_M), F32),
                     jax.ShapeDtypeStruct((n_batch, 8, CHUNK), F32)]
    return pl.pallas_call(
        functools.partial(_mlstm_kernel, has_state=state is not None, n_chunks=n_chunks),
        grid=(n_batch,),
        in_specs=in_specs,
        out_specs=out_specs,
        out_shape=out_shape,
        scratch_shapes=scratch,
        compiler_params=_cparams(("arbitrary",)),
        name="mlstm_lat" if state is not None else "mlstm_ctx",
    )(*args)


def _outproj_kernel(attc_ref, attl_ref, hmc_ref, hml_ref, xc_ref, xl_ref, mod_ref, nw_ref, wo_ref,
                    rwh_ref, rwl_ref, wsg_ref, wsu_ref, wsd_ref, base_ref, hn_ref, lt_ref):
    y = (_dot(_ctx_or_lat(attc_ref, attl_ref), wo_ref[0:D_ATT, :])
         + _dot(_ctx_or_lat(hmc_ref, hml_ref), wo_ref[D_ATT:, :]))
    x1 = _ctx_or_lat(xc_ref, xl_ref) + mod_ref[2:3, :] * y
    z = x1 * lax.rsqrt(jnp.mean(x1 * x1, axis=-1, keepdims=True) + EPS) * nw_ref[...]
    hn = z * (1.0 + mod_ref[4:5, :]) + mod_ref[3:4, :]
    hb = hn.astype(BF16)
    hl = (hn - hb.astype(F32)).astype(BF16)
    nt = lambda w, t: lax.dot_general(w, t, _NT, preferred_element_type=F32)
    lt_ref[...] = nt(rwh_ref[...], hb) + nt(rwl_ref[...], hb) + nt(rwh_ref[...], hl)
    a = _silu(_dot(hb, wsg_ref[...])) * _dot(hb, wsu_ref[...])
    shared = _dot(a.astype(BF16), wsd_ref[...])
    base_ref[...] = x1 + mod_ref[5:6, :] * shared
    packed = pltpu.pack_elementwise([hn[:, :512], hn[:, 512:]], packed_dtype=BF16)
    for c in range(4):
        hn_ref[pl.ds(c, TBP, stride=4), :] = packed[:, 128 * c:128 * (c + 1)]


def _outproj(att, hm, x, mod3, norm2_w, w_out, rw_hi, rw_lo, wsg, wsu, wsd):
    tok = lambda w: pl.BlockSpec((TBP, w), lambda i: (i, 0))
    full = lambda a: pl.BlockSpec(a.shape, lambda i: (0,) * a.ndim)
    return pl.pallas_call(
        _outproj_kernel,
        grid=(T_ALL // TBP,),
        in_specs=_split_specs(512) + _split_specs(512) + _split_specs(D_MODEL) + [
                  pl.BlockSpec((None, 6, D_MODEL), lambda i: (_mod_row(i), 0, 0)),
                  full(norm2_w), full(w_out), full(rw_hi), full(rw_lo), full(wsg), full(wsu), full(wsd)],
        out_specs=[tok(D_MODEL), pl.BlockSpec((4 * TBP, 128), lambda i: (i, 0)),
                   pl.BlockSpec((N_EXPERTS, TBP), lambda i: (0, i))],
        out_shape=[jax.ShapeDtypeStruct((T_ALL, D_MODEL), F32),
                   jax.ShapeDtypeStruct((4 * T_ALL, 128), jnp.uint32),
                   jax.ShapeDtypeStruct((N_EXPERTS, T_ALL), F32)],
        compiler_params=_cparams(("arbitrary",)),
        name="outproj",
    )(*att, *hm, *x, mod3, norm2_w, w_out, rw_hi, rw_lo, wsg, wsu, wsd)


def _first_max(vals, ids, limit):
    m = functools.reduce(jnp.maximum, [jnp.max(v, axis=0, keepdims=True) for v in vals])
    cand = [jnp.min(jnp.where(v == m, i, limit), axis=0, keepdims=True) for v, i in zip(vals, ids)]
    return m, functools.reduce(jnp.minimum, cand)


def _router_kernel(lt_ref, bias_ref, gate_ref, idx_ref):
    n = lt_ref.shape[1]
    score = jax.nn.sigmoid(lt_ref[...])
    biased = score + bias_ref[...]
    sub = lax.broadcasted_iota(jnp.int32, (8, n), 0).astype(F32)
    neg_inf = F32(-jnp.inf)
    slabs = [biased[8 * g:8 * (g + 1), :] for g in range(N_GROUPS)]
    gs = []
    for sl in slabs:
        m1, i1 = _first_max([sl], [sub], 8.0)
        m2 = jnp.max(jnp.where(sub == i1, neg_inf, sl), axis=0, keepdims=True)
        gs.append(m1 + m2)
    cur = jnp.concatenate(gs, axis=0)
    gsel = jnp.zeros((8, n), F32)
    for _ in range(TOPK_GROUPS):
        _, i = _first_max([cur], [sub], 8.0)
        hit = sub == i
        gsel = jnp.where(hit, 1.0, gsel)
        cur = jnp.where(hit, neg_inf, cur)
    vals = [jnp.where(gsel[g:g + 1, :] > 0.0, slabs[g], neg_inf) for g in range(N_GROUPS)]
    ids = [sub + 8.0 * g for g in range(N_GROUPS)]
    picked = [jnp.zeros((8, n), F32) for _ in range(N_GROUPS)]
    order = []
    for _ in range(TOP_K):
        _, i = _first_max(vals, ids, float(N_EXPERTS))
        order.append(i)
        hits = [idg == i for idg in ids]
        picked = [jnp.where(hh, score[8 * g:8 * (g + 1), :], p) for g, (p, hh) in enumerate(zip(picked, hits))]
        vals = [jnp.where(hh, neg_inf, v) for v, hh in zip(vals, hits)]
    total = functools.reduce(jnp.add, [jnp.sum(p, axis=0, keepdims=True) for p in picked])
    gate_t = jnp.concatenate([p / total * ROUTED_SCALE for p in picked]
                             + [jnp.zeros((128 - N_EXPERTS, n), F32)], axis=0)
    gate_ref[...] = gate_t.T
    idx_ref[...] = jnp.concatenate(order, axis=0).astype(jnp.int32)


def _router(logits_t, bias_col):
    return pl.pallas_call(
        _router_kernel,
        grid=(T_ALL // TB_MOE,),
        in_specs=[pl.BlockSpec((N_EXPERTS, TB_MOE), lambda i: (0, i)),
                  pl.BlockSpec((N_EXPERTS, 1), lambda i: (0, 0))],
        out_specs=[pl.BlockSpec((TB_MOE, 128), lambda i: (i, 0)),
                   pl.BlockSpec((TOP_K, TB_MOE), lambda i: (0, i))],
        out_shape=[jax.ShapeDtypeStruct((T_ALL, 128), F32),
                   jax.ShapeDtypeStruct((TOP_K, T_ALL), jnp.int32)],
        compiler_params=_cparams(("arbitrary",)),
        name="router",
    )(logits_t, bias_col)


def _plan_kernel(idx_ref, slot_ref, cnt_ref, off_ref, pos_sc):
    n_tiles = T_HALF // 128
    eid = lax.broadcasted_iota(jnp.int32, (N_EXPERTS, 128), 0)
    tri = (lax.broadcasted_iota(jnp.int32, (128, 128), 0)
           <= lax.broadcasted_iota(jnp.int32, (128, 128), 1)).astype(BF16)
    carry = jnp.zeros((N_EXPERTS, 1), F32)
    for j in range(n_tiles):
        it = idx_ref[:, 128 * j:128 * (j + 1)]
        sel = jnp.zeros((N_EXPERTS, 128), F32)
        for k in range(TOP_K):
            sel = jnp.where(it[k:k + 1, :] == eid, 1.0, sel)
        inc = _dot(sel.astype(BF16), tri) + carry
        carry = inc[:, 127:128]
        pos_sc[:, 128 * j:128 * (j + 1)] = inc - 1.0
    count = jnp.broadcast_to(carry, (N_EXPERTS, 128))
    padded = jnp.floor((count + (MT - 1.0)) * (1.0 / MT)) * MT
    before = (lax.broadcasted_iota(jnp.int32, (N_EXPERTS, N_EXPERTS), 1)
              < lax.broadcasted_iota(jnp.int32, (N_EXPERTS, N_EXPERTS), 0)).astype(BF16)
    hi, mid, lo = _split3(padded)
    off = _dot(before, hi) + _dot(before, mid) + _dot(before, lo)
    cnt_ref[...] = count.astype(jnp.int32)
    off_ref[...] = off.astype(jnp.int32)

    def slots(j):
        it = idx_ref[:, 128 * j:128 * (j + 1)]
        val = off + pos_sc[:, 128 * j:128 * (j + 1)]
        rows = [jnp.sum(jnp.where(it[k:k + 1, :] == eid, val, 0.0), axis=0, keepdims=True)
                for k in range(TOP_K)]
        return jnp.concatenate(rows, axis=0).astype(jnp.int32)

    for j in range(n_tiles // 2):
        slot_ref[:, 128 * j:128 * (j + 1)] = slots(j) | (slots(j + n_tiles // 2) << 16)


def _plan(idx8):
    return pl.pallas_call(
        _plan_kernel,
        grid=(2,),
        in_specs=[pl.BlockSpec((TOP_K, T_HALF), lambda h: (0, h))],
        out_specs=[pl.BlockSpec((None, TOP_K, T_HALF // 2), lambda h: (h, 0, 0)),
                   pl.BlockSpec((None, N_EXPERTS, 128), lambda h: (h, 0, 0)),
                   pl.BlockSpec((None, N_EXPERTS, 128), lambda h: (h, 0, 0))],
        out_shape=[jax.ShapeDtypeStruct((2, TOP_K, T_HALF // 2), jnp.int32),
                   jax.ShapeDtypeStruct((2, N_EXPERTS, 128), jnp.int32),
                   jax.ShapeDtypeStruct((2, N_EXPERTS, 128), jnp.int32)],
        scratch_shapes=[pltpu.VMEM((N_EXPERTS, T_HALF), F32)],
        compiler_params=_cparams(("arbitrary",)),
        name="plan",
    )(idx8)


def _prepare_half(slot_ref, cnt_ref, off_ref, tok_ref, tile_e, tile_first, elist, h):
    def per_expert(x, carry):
        j, q = carry
        n = cnt_ref[h * N_EXPERTS + x]
        first = off_ref[h * N_EXPERTS + x]
        tiles = (n + MT - 1) // MT
        elist[q] = x

        def mark(i, c):
            tile_e[j + i] = x
            tile_first[j + i] = jnp.where(i == 0, 1, 0)
            return c

        lax.fori_loop(0, tiles, mark, 0)

        def pad(p, c):
            tok_ref[first + p] = T_HALF
            return c

        lax.fori_loop(n, tiles * MT, pad, 0)
        return j + tiles, q + jnp.where(tiles > 0, 1, 0)

    n_tiles, n_live = lax.fori_loop(0, N_EXPERTS, per_expert, (0, 0))
    for extra in range(2):
        tile_e[n_tiles + extra] = 0
        tile_first[n_tiles + extra] = 0

    def pad_tail(p, c):
        tok_ref[n_tiles * MT + p] = T_HALF
        return c

    lax.fori_loop(0, 2 * MT, pad_tail, 0)
    return n_tiles, n_live


def _invert_slots(slot_ref, tok_ref, h):
    half = T_HALF // 2
    for k in range(TOP_K):
        def body(i, carry, k=k):
            words = [slot_ref[(h * TOP_K + k) * half + i * INV_UNROLL + u] for u in range(INV_UNROLL)]
            for u in range(INV_UNROLL):
                t = i * INV_UNROLL + u
                tok_ref[words[u] & 0xFFFF] = t
                tok_ref[lax.shift_right_logical(words[u], 16)] = t + half
            return carry

        lax.fori_loop(0, half // INV_UNROLL, body, 0)


def _moe_kernel(slot_ref, cnt_ref, off_ref, src_hbm, gate_hbm, wg_hbm, wu_hbm, wd_hbm, out_hbm,
                src_v, gate_v, acc_v, wg_l, wu_l, wd_l, wgu_b, wd_b, xbuf0, xbuf1, gbuf0, gbuf1,
                ybuf0, ybuf1, tok_ref, tile_e, tile_first, elist, live, sem, wsem):
    h = pl.program_id(0)
    xbuf, gbuf, ybuf = (xbuf0, xbuf1), (gbuf0, gbuf1), (ybuf0, ybuf1)

    src_cp = pltpu.make_async_copy(src_hbm.at[pl.ds(pl.multiple_of(h * (4 * T_HALF), 8), 4 * T_HALF)],
                                   src_v.at[pl.ds(0, 4 * T_HALF)], sem.at[0])
    gate_cp = pltpu.make_async_copy(gate_hbm.at[pl.ds(pl.multiple_of(h * T_HALF, 8), T_HALF)],
                                    gate_v.at[pl.ds(0, T_HALF)], sem.at[1])
    src_cp.start()
    gate_cp.start()

    def weight_copies(x, slot):
        return [pltpu.make_async_copy(w_hbm.at[x], w_l.at[slot], wsem.at[slot, i])
                for i, (w_hbm, w_l) in enumerate(((wg_hbm, wg_l), (wu_hbm, wu_l), (wd_hbm, wd_l)))]

    n_tiles, n_live = _prepare_half(slot_ref, cnt_ref, off_ref, tok_ref, tile_e, tile_first, elist, h)
    live[0] = 0
    live[1] = n_live

    @pl.when(n_live > 0)
    def _():
        for cp in weight_copies(elist[0], 0):
            cp.start()

    acc_v[...] = jnp.zeros(acc_v.shape, F32)
    zero = jnp.zeros((8, 128), F32)
    src_v[pl.ds(4 * T_HALF, 8), :] = pltpu.pack_elementwise([zero, zero], packed_dtype=BF16)
    gate_v[pl.ds(T_HALF, 8), :] = zero
    ybuf0[...] = jnp.zeros(ybuf0.shape, F32)
    ybuf1[...] = jnp.zeros(ybuf1.shape, F32)
    _invert_slots(slot_ref, tok_ref, h)
    src_cp.wait()
    gate_cp.wait()

    def switch_expert():
        q = live[0]
        slot = q % 2
        for cp in weight_copies(elist[q], slot):
            cp.wait()
        wgu_b[:, 0:D_EXPERT] = wg_l[slot].astype(BF16)
        wgu_b[:, D_EXPERT:] = wu_l[slot].astype(BF16)
        wd_b[...] = wd_l[slot].astype(BF16)
        live[0] = q + 1

        @pl.when(q + 1 < live[1])
        def _():
            for cp in weight_copies(elist[q + 1], 1 - slot):
                cp.start()

    quarter = MT // 4

    def gather(j, xb, gb, part=None):
        base = j * MT
        for m in range(MT) if part is None else range(part * quarter, (part + 1) * quarter):
            t = tok_ref[base + m]
            xb[pl.ds(m, 4, stride=MT_STRIDE), :] = src_v[pl.ds(pl.multiple_of(t * 4, 4), 4), :]
            gb[m:m + 1, :] = gate_v[pl.ds(t, 1), :]

    def scatter(j, yb, part=None):
        base = j * MT
        batches = MT // RMW_BATCH
        per_part = batches // 4
        for b in range(batches) if part is None else range(part * per_part, (part + 1) * per_part):
            ms = [b * RMW_BATCH + u for u in range(RMW_BATCH)]
            rows = [pl.ds(pl.multiple_of(tok_ref[base + m] * 8, 8), 8) for m in ms]
            vals = [acc_v[r, :] + yb[pl.ds(m, 8, stride=MT_STRIDE), :] for r, m in zip(rows, ms)]
            for r, v in zip(rows, vals):
                acc_v[r, :] = v

    def step(j, p):
        pl.when(tile_first[j] == 1)(switch_expert)
        xb, gb, yb = xbuf[p], gbuf[p], ybuf[p]
        nxt = (j + 1, xbuf[1 - p], gbuf[1 - p])
        prv = (jnp.maximum(j - 1, 0), ybuf[1 - p])
        lo, hi = [], []
        for c in range(4):
            words = xb[MT_STRIDE * c:MT_STRIDE * c + MT, :]
            unpack = functools.partial(pltpu.unpack_elementwise, words, packed_dtype=BF16, unpacked_dtype=F32)
            lo.append(unpack(index=0).astype(BF16))
            hi.append(unpack(index=1).astype(BF16))
        x = jnp.concatenate(lo + hi, axis=1)
        gather(*nxt, part=0)
        h_gate = _dot(x, wgu_b[:, 0:D_EXPERT])
        gather(*nxt, part=1)
        h_up = _dot(x, wgu_b[:, D_EXPERT:])
        gather(*nxt, part=2)
        g = gb[...]
        g_hi = g.astype(BF16)
        g_lo = (g - g_hi.astype(F32)).astype(BF16)
        pick = (lax.broadcasted_iota(jnp.int32, (128, D_EXPERT), 0) == tile_e[j]).astype(BF16)
        g_col = _dot(g_hi, pick) + _dot(g_lo, pick)
        a = (_silu(h_gate) * h_up * g_col).astype(BF16)
        gather(*nxt, part=3)
        scatter(*prv, part=0)
        y_lo = _dot(a, wd_b[:, 0:D_MODEL // 2])
        scatter(*prv, part=1)
        y_hi = _dot(a, wd_b[:, D_MODEL // 2:])
        scatter(*prv, part=2)
        for c in range(4):
            yb[MT_STRIDE * c:MT_STRIDE * c + MT, :] = y_lo[:, 128 * c:128 * (c + 1)]
            yb[MT_STRIDE * (c + 4):MT_STRIDE * (c + 4) + MT, :] = y_hi[:, 128 * c:128 * (c + 1)]
        scatter(*prv, part=3)

    gather(0, xbuf[0], gbuf[0])
    n_pairs = (n_tiles + 1) // 2

    def pair(i, carry):
        step(2 * i, 0)
        step(2 * i + 1, 1)
        return carry

    lax.fori_loop(0, n_pairs, pair, 0)
    scatter(jnp.maximum(2 * n_pairs - 1, 0), ybuf[1])

    out_cp = pltpu.make_async_copy(acc_v.at[pl.ds(0, 8 * T_HALF)],
                                   out_hbm.at[pl.ds(pl.multiple_of(h * (8 * T_HALF), 8), 8 * T_HALF)],
                                   sem.at[2])
    out_cp.start()
    out_cp.wait()


def _moe(slot_words, cnt, off, src, gate, wg, wu, wd):
    any_spec = pl.BlockSpec(memory_space=pl.ANY)
    tile_buf = lambda rows, dt: pltpu.VMEM((rows * MT_STRIDE, 128), dt)
    return pl.pallas_call(
        _moe_kernel,
        grid_spec=pltpu.PrefetchScalarGridSpec(
            num_scalar_prefetch=3,
            grid=(2,),
            in_specs=[any_spec] * 5,
            out_specs=any_spec,
            scratch_shapes=[pltpu.VMEM((4 * T_HALF + 8, 128), jnp.uint32),
                            pltpu.VMEM((T_HALF + 8, 128), F32),
                            pltpu.VMEM((8 * T_HALF + 8, 128), F32),
                            pltpu.VMEM((2, D_MODEL, D_EXPERT), F32),
                            pltpu.VMEM((2, D_MODEL, D_EXPERT), F32),
                            pltpu.VMEM((2, D_EXPERT, D_MODEL), F32),
                            pltpu.VMEM((D_MODEL, 2 * D_EXPERT), BF16),
                            pltpu.VMEM((D_EXPERT, D_MODEL), BF16),
                            tile_buf(4, jnp.uint32), tile_buf(4, jnp.uint32),
                            pltpu.VMEM((MT, 128), F32), pltpu.VMEM((MT, 128), F32),
                            tile_buf(8, F32), tile_buf(8, F32),
                            pltpu.SMEM((SLOT_CAP,), jnp.int32),
                            pltpu.SMEM((NT_MAX + 2,), jnp.int32),
                            pltpu.SMEM((NT_MAX + 2,), jnp.int32),
                            pltpu.SMEM((N_EXPERTS,), jnp.int32),
                            pltpu.SMEM((2,), jnp.int32),
                            pltpu.SemaphoreType.DMA((3,)),
                            pltpu.SemaphoreType.DMA((2, 3))]),
        out_shape=jax.ShapeDtypeStruct((8 * T_ALL, 128), F32),
        compiler_params=_cparams(("arbitrary",)),
        name="moe",
    )(slot_words, cnt, off, src, gate, wg, wu, wd)


def _finalize_kernel(acc_ref, base_ref, mod_ref, oc_ref, ol_ref):
    def write(o_ref):
        for c in range(8):
            cs = slice(128 * c, 128 * (c + 1))
            o_ref[:, cs] = base_ref[:, cs] + mod_ref[5:6, cs] * acc_ref[pl.ds(c, TBP, stride=8), :]

    is_ctx = pl.program_id(0) < NP_CTX
    pl.when(is_ctx)(lambda: write(oc_ref))
    pl.when(jnp.logical_not(is_ctx))(lambda: write(ol_ref))


def _finalize(acc, base, mod3):
    return pl.pallas_call(
        _finalize_kernel,
        grid=(T_ALL // TBP,),
        in_specs=[pl.BlockSpec((8 * TBP, 128), lambda i: (i, 0)),
                  pl.BlockSpec((TBP, D_MODEL), lambda i: (i, 0)),
                  pl.BlockSpec((None, 6, D_MODEL), lambda i: (_mod_row(i), 0, 0))],
        out_specs=_split_specs(D_MODEL),
        out_shape=[jax.ShapeDtypeStruct((T_CTX, D_MODEL), F32), jax.ShapeDtypeStruct((T_LAT, D_MODEL), F32)],
        compiler_params=_cparams(("arbitrary",)),
        name="finalize",
    )(acc, base, mod3)


def _head_indicators(width):
    head = jnp.arange(width) // DH_ATT
    ind = (head[:, None] == jnp.arange(128)[None, :]).astype(BF16)
    return ind, ind.T


def kernel(x_prompt, x_sample, cache_attn_k, cache_attn_v, state_mlstm_c, state_mlstm_n, state_mlstm_m, c, c_ctx, w_mod, b_mod, norm1_w, norm2_w, w_in, q_norm_w, k_norm_w, b_gates, m_norm_w, w_out, router_w, router_bias, w_gate, w_up, w_down, ws_gate, ws_up, ws_down):
    x = (x_prompt.reshape(T_CTX, D_MODEL), x_sample.reshape(T_LAT, D_MODEL))
    mod3 = _modulation(c, c_ctx, w_mod[0], b_mod[0])

    w_main = w_in[0, :, :P_MAIN].astype(BF16)
    w_gates_t = w_in[0, :, P_MAIN:].T.astype(BF16)
    qa, ka, va, qm, km, vm, om, gt = _inproj(*x, mod3, norm1_w, w_main, w_gates_t)

    qw = jnp.tile(q_norm_w, (1, N_HEADS_ATT))
    kw = jnp.tile(k_norm_w, (1, N_KV_HEADS))
    inds = _head_indicators(512) + _head_indicators(128)
    qn_c, kn_c = _qkprep(qa, ka, qw, kw, inds, None, first_block=0, n_blocks=N_CTX_BLOCKS)
    qn_l, kn_l = _qkprep(qa, ka, qw, kw, inds, _rope_tables(), first_block=N_CTX_BLOCKS,
                         n_blocks=T_LAT // TB)
    att_c = _attention(qn_c, kn_c, va, None, n_batch=BATCH, seq=SEQ, v_first_block=0)
    att_l = _attention(qn_l, kn_l, va, (cache_attn_k, cache_attn_v), n_batch=DEC_BATCH, seq=DEC_SEQ,
                       v_first_block=T_CTX // DEC_SEQ)

    g3 = gt.reshape(16, T_ALL // CHUNK, CHUNK).transpose(1, 0, 2)
    bg = b_gates.reshape(16, 1)
    hm_c, c_new, n_new, m_new = _mlstm(qm, km, vm, g3, bg, om, m_norm_w, None,
                                       n_batch=BATCH, seq=SEQ, first_block=0)
    m0 = jnp.broadcast_to(state_mlstm_m.reshape(DEC_BATCH, 8, 1), (DEC_BATCH, 8, CHUNK))
    hm_l = _mlstm(qm, km, vm, g3, bg, om, m_norm_w, (state_mlstm_c, state_mlstm_n, m0),
                  n_batch=DEC_BATCH, seq=DEC_SEQ, first_block=N_CTX_BLOCKS)

    rw_t = router_w[0].T
    rw_hi = rw_t.astype(BF16)
    rw_lo = (rw_t - rw_hi.astype(F32)).astype(BF16)
    base, hn2, logits_t = _outproj((att_c, att_l), (hm_c, hm_l), x, mod3, norm2_w, w_out[0].astype(BF16),
                                   rw_hi, rw_lo, ws_gate[0].astype(BF16), ws_up[0].astype(BF16),
                                   ws_down[0].astype(BF16))
    gate, idx8 = _router(logits_t, router_bias.reshape(N_EXPERTS, 1))
    slot_words, cnt, off = _plan(idx8)
    acc = _moe(slot_words.reshape(TOP_K * T_HALF), cnt[:, :, 0].reshape(2 * N_EXPERTS),
               off[:, :, 0].reshape(2 * N_EXPERTS), hn2, gate, w_gate[0], w_up[0], w_down[0])
    out_c, out_l = _finalize(acc, base, mod3)

    y_prompt = out_c.reshape(BATCH, SEQ, D_MODEL)
    y_sample = out_l.reshape(DEC_BATCH, DEC_SEQ, D_MODEL)
    to_cache = lambda a: a[:T_CTX].reshape(BATCH, SEQ, N_KV_HEADS, DH_ATT).transpose(0, 2, 1, 3)[:, None]
    new_k = to_cache(kn_c)
    new_v = to_cache(va)
    new_m = m_new[:, :, 0].reshape(BATCH, 1, 2, N_HEADS_M)
    return (y_prompt, y_sample, new_k, new_v, c_new[:, None], n_new[:, None], new_m)
```

```python
import functools

import jax
import jax.numpy as jnp
from jax import lax
from jax.experimental import pallas as pl
from jax.experimental.pallas import tpu as pltpu

F32 = jnp.float32
BF16 = jnp.bfloat16

D_MODEL = 1024
BATCH = 32
SEQ = 256
DEC_BATCH = 2
DEC_SEQ = 1024
PAST_LEN = 256
GRID_W = 64
N_HEADS_ATT = 8
N_KV_HEADS = 2
DH_ATT = 64
D_ATT = 512
ROPE_THETA = 10000.0
N_HEADS_M = 4
DH_M = 128
D_M = 512
CHUNK = 128
N_EXPERTS = 64
TOP_K = 8
N_GROUPS = 8
TOPK_GROUPS = 4
D_EXPERT = 256
ROUTED_SCALE = 2.5
EPS = 1e-6
NEG_INIT = -1e30

T_CTX = BATCH * SEQ
T_LAT = DEC_BATCH * DEC_SEQ
T_ALL = T_CTX + T_LAT
TB = 256
N_CTX_BLOCKS = T_CTX // TB
TBP = 512
NP_CTX = T_CTX // TBP
NP_LAT_PER_BATCH = DEC_SEQ // TBP
TB_MOE = 1024
T_HALF = T_ALL // 2
MT = 256
MT_STRIDE = MT + 8
NT_MAX = TOP_K * T_HALF // MT + N_EXPERTS
SLOT_CAP = (NT_MAX + 2) * MT
RMW_BATCH = 16
INV_UNROLL = 16
assert SLOT_CAP < 2 ** 16 and (T_HALF // 2) % INV_UNROLL == 0 and MT % RMW_BATCH == 0
P_MAIN = 2816
VMEM_LIMIT = 56 * 1024 * 1024

_NT = (((1,), (1,)), ((), ()))
_TN = (((0,), (0,)), ((), ()))


def _cparams(sem):
    return pltpu.CompilerParams(dimension_semantics=sem, vmem_limit_bytes=VMEM_LIMIT)


def _split3(x):
    hi = x.astype(BF16)
    r1 = x - hi.astype(F32)
    mid = r1.astype(BF16)
    lo = (r1 - mid.astype(F32)).astype(BF16)
    return hi, mid, lo


def _dot(a, b):
    return jnp.dot(a, b, preferred_element_type=F32)


def _dot3(x, m_bf16):
    hi, mid, lo = _split3(x)
    return _dot(hi, m_bf16) + _dot(mid, m_bf16) + _dot(lo, m_bf16)


def _silu(x):
    return x * jax.nn.sigmoid(x)


def _mod_row(i):
    return jnp.where(i < NP_CTX, 0, 1 + (i - NP_CTX) // NP_LAT_PER_BATCH)


def _mod_kernel(ct_ref, w_ref, b_ref, o_ref):
    s = _silu(ct_ref[...])
    w = w_ref[...]
    rows = [jnp.sum(w * s[:, r:r + 1], axis=0, keepdims=True) for r in range(3)]
    rows.append(jnp.zeros((5, w.shape[1]), F32))
    o_ref[...] = jnp.concatenate(rows, axis=0) + b_ref[...]


def _modulation(c, c_ctx, w_mod, b_mod):
    cvec = jnp.concatenate([c_ctx[None, :], c, jnp.zeros((5, D_MODEL), F32)], axis=0)
    nb = 1024
    out = pl.pallas_call(
        _mod_kernel,
        grid=(6 * D_MODEL // nb,),
        in_specs=[pl.BlockSpec((D_MODEL, 8), lambda j: (0, 0)),
                  pl.BlockSpec((D_MODEL, nb), lambda j: (0, j)),
                  pl.BlockSpec((1, nb), lambda j: (0, j))],
        out_specs=pl.BlockSpec((8, nb), lambda j: (0, j)),
        out_shape=jax.ShapeDtypeStruct((8, 6 * D_MODEL), F32),
        compiler_params=_cparams(("arbitrary",)),
        name="modulation",
    )(cvec.T, w_mod, b_mod[None, :])
    return out.reshape(8, 6, D_MODEL)


def _ctx_or_lat(ctx_ref, lat_ref):
    return jnp.where(pl.program_id(0) < NP_CTX, ctx_ref[...], lat_ref[...])


def _split_specs(width):
    return [pl.BlockSpec((TBP, width), lambda i: (jnp.minimum(i, NP_CTX - 1), 0)),
            pl.BlockSpec((TBP, width), lambda i: (jnp.maximum(i - NP_CTX, 0), 0))]


def _head_norm(x, ind, ind_t, w_row):
    ss = _dot3(x * x, ind)
    inv = lax.rsqrt(ss * (1.0 / DH_ATT) + EPS)
    return x * _dot3(inv, ind_t) * w_row


def _rope(x, cos, sin_signed):
    lane = lax.broadcasted_iota(jnp.int32, x.shape, 1)
    partner = jnp.where((lane % 32) < 16, pltpu.roll(x, 128 - 16, 1), pltpu.roll(x, 16, 1))
    return x * cos + partner * sin_signed


def _inproj_kernel(xc_ref, xl_ref, mod_ref, nw_ref, w_ref, wgt_ref, qw_ref, kw_ref, iq_ref, iqt_ref,
                   ik_ref, ikt_ref, cos_ref, sin_ref,
                   qn_ref, kn_ref, va_ref, qm_ref, km_ref, vm_ref, om_ref, gt_ref):
    x = _ctx_or_lat(xc_ref, xl_ref)
    y = x * lax.rsqrt(jnp.mean(x * x, axis=-1, keepdims=True) + EPS) * nw_ref[...]
    hn = y * (1.0 + mod_ref[1:2, :]) + mod_ref[0:1, :]
    hb = hn.astype(BF16)
    qn = _head_norm(_dot(hb, w_ref[:, 0:512]), iq_ref[...], iqt_ref[...], qw_ref[...])
    kn = _head_norm(_dot(hb, w_ref[:, 512:640]), ik_ref[...], ikt_ref[...], kw_ref[...])
    is_lat = pl.program_id(0) >= NP_CTX
    cos, sin = cos_ref[...], sin_ref[...]
    q_rot = jnp.concatenate([_rope(qn[:, 128 * j:128 * (j + 1)], cos, sin) for j in range(4)], axis=1)
    qn_ref[...] = jnp.where(is_lat, q_rot, qn).astype(BF16)
    kn_ref[...] = jnp.where(is_lat, _rope(kn, cos, sin), kn)
    va_ref[...] = _dot(hb, w_ref[:, 640:768])
    qm_ref[...] = _dot(hb, w_ref[:, 768:1280]).astype(BF16)
    km_ref[...] = (_dot(hb, w_ref[:, 1280:1792]) * (DH_M ** -0.5)).astype(BF16)
    vm_ref[...] = _dot(hb, w_ref[:, 1792:2304]).astype(BF16)
    om_ref[...] = _dot(hb, w_ref[:, 2304:2816])
    gt_ref[...] = lax.dot_general(wgt_ref[...], hb, _NT, preferred_element_type=F32)


def _inproj(x_ctx, x_lat, mod3, norm1_w, w_main, w_gates_t, qw, kw, inds, rope_tabs):
    tok = lambda w: pl.BlockSpec((TBP, w), lambda i: (i, 0))
    full = lambda a: pl.BlockSpec(a.shape, lambda i: (0,) * a.ndim)
    sd = lambda w, dt: jax.ShapeDtypeStruct((T_ALL, w), dt)
    rope_spec = pl.BlockSpec((TBP, 128), lambda i: (jnp.maximum(i - NP_CTX, 0) % NP_LAT_PER_BATCH, 0))
    return pl.pallas_call(
        _inproj_kernel,
        grid=(T_ALL // TBP,),
        in_specs=_split_specs(D_MODEL) + [
                  pl.BlockSpec((None, 6, D_MODEL), lambda i: (_mod_row(i), 0, 0)),
                  full(norm1_w), full(w_main), full(w_gates_t), full(qw), full(kw)]
                 + [full(a) for a in inds] + [rope_spec, rope_spec],
        out_specs=[tok(512), tok(128), tok(128), tok(512), tok(512), tok(512), tok(512),
                   pl.BlockSpec((16, TBP), lambda i: (0, i))],
        out_shape=[sd(512, BF16), sd(128, F32), sd(128, F32), sd(512, BF16), sd(512, BF16),
                   sd(512, BF16), sd(512, F32), jax.ShapeDtypeStruct((16, T_ALL), F32)],
        compiler_params=_cparams(("arbitrary",)),
        name="inproj",
    )(x_ctx, x_lat, mod3, norm1_w, w_main, w_gates_t, qw, kw, *inds, *rope_tabs)


def _rope_tables():
    t = jnp.arange(DEC_SEQ)
    pos = jnp.stack([t // GRID_W, t % GRID_W], axis=1).astype(F32)
    n_freq = DH_ATT // 4
    inv_freq = ROPE_THETA ** (-jnp.arange(n_freq, dtype=F32) / n_freq)
    ang = pos[:, :, None] * inv_freq
    cos, sin = jnp.cos(ang), jnp.sin(ang)
    cos_h = jnp.stack([cos, cos], axis=2).reshape(DEC_SEQ, DH_ATT)
    sin_h = jnp.stack([-sin, sin], axis=2).reshape(DEC_SEQ, DH_ATT)
    return jnp.tile(cos_h, (1, 2)), jnp.tile(sin_h, (1, 2))


def _attn_kernel(*refs, has_cache):
    if has_cache:
        q_ref, k_ref, v_ref, kc_ref, vc_ref, o_ref = refs
    else:
        q_ref, k_ref, v_ref, o_ref = refs
    q = q_ref[...]
    k = k_ref[...].astype(BF16)
    v = v_ref[...].astype(BF16)
    qb = q.shape[0]
    scale = DH_ATT ** -0.5
    outs = []
    for g in range(N_KV_HEADS):
        kg = k[:, DH_ATT * g:DH_ATT * (g + 1)]
        vg = v[:, DH_ATT * g:DH_ATT * (g + 1)]
        qg = jnp.concatenate([q[:, 256 * g + DH_ATT * j:256 * g + DH_ATT * (j + 1)] for j in range(4)], axis=0)
        s = lax.dot_general(qg, kg, _NT, preferred_element_type=F32) * scale
        m = jnp.max(s, axis=-1, keepdims=True)
        if has_cache:
            kc = kc_ref[g].astype(BF16)
            vc = vc_ref[g].astype(BF16)
            sc = lax.dot_general(qg, kc, _NT, preferred_element_type=F32) * scale
            m = jnp.maximum(m, jnp.max(sc, axis=-1, keepdims=True))
        p = jnp.exp(s - m)
        den = jnp.sum(p, axis=-1, keepdims=True)
        o = _dot(p.astype(BF16), vg)
        if has_cache:
            pc = jnp.exp(sc - m)
            den = den + jnp.sum(pc, axis=-1, keepdims=True)
            o = o + _dot(pc.astype(BF16), vc)
        o = o / den
        outs += [o[qb * j:qb * (j + 1), :] for j in range(4)]
    o_ref[...] = jnp.concatenate(outs, axis=1).astype(BF16)


def _attention(qn, kn, v_all, cache, *, n_batch, seq, first_row):
    qblocks = seq // TB
    kv_spec = pl.BlockSpec((seq, 128), lambda b, i: (b + first_row // seq, 0))
    in_specs = [pl.BlockSpec((TB, 512), lambda b, i: (first_row // TB + b * qblocks + i, 0)), kv_spec, kv_spec]
    args = [qn, kn, v_all]
    if cache is not None:
        cspec = pl.BlockSpec((None, None, N_KV_HEADS, PAST_LEN, DH_ATT), lambda b, i: (b, 0, 0, 0, 0))
        in_specs += [cspec, cspec]
        args += list(cache)
    return pl.pallas_call(
        functools.partial(_attn_kernel, has_cache=cache is not None),
        grid=(n_batch, qblocks),
        in_specs=in_specs,
        out_specs=pl.BlockSpec((TB, 512), lambda b, i: (b * qblocks + i, 0)),
        out_shape=jax.ShapeDtypeStruct((n_batch * seq, 512), BF16),
        compiler_params=_cparams(("arbitrary", "arbitrary")),
        name="attention_lat" if cache is not None else "attention_ctx",
    )(*args)


def _log_sigmoid(x):
    return jnp.minimum(x, 0.0) - jnp.log1p(jnp.exp(-jnp.abs(x)))


def _col_bcast(cols, j):
    return jnp.broadcast_to(cols[:, j:j + 1], (CHUNK, CHUNK))


def _mlstm_kernel(*refs, has_state, n_chunks):
    n_in = 10 if has_state else 7
    n_out = 1 if has_state else 4
    q_ref, k_ref, v_ref, g_ref, bg_ref, om_ref, nw_ref = refs[:7]
    hm_ref = refs[n_in]
    scratch = refs[n_in + n_out:]
    st = scratch[0:8]
    ms = scratch[8:16]
    hdir = scratch[16:18]
    rows_sc, cmb_sc, bb_sc, kt_sc = scratch[18:22]
    neg_inf = F32(-jnp.inf)
    zeros112 = jnp.zeros((CHUNK - 16, CHUNK), F32)
    sub = lax.broadcasted_iota(jnp.int32, (CHUNK, CHUNK), 0)
    lan = lax.broadcasted_iota(jnp.int32, (CHUNK, CHUNK), 1)
    ones = jnp.ones((CHUNK, CHUNK), BF16)

    for c in range(n_chunks):
        for h in range(N_HEADS_M):
            kc = k_ref[CHUNK * c:CHUNK * (c + 1), DH_M * h:DH_M * (h + 1)]
            kt_sc[c * N_HEADS_M + h] = kc.astype(F32).T.astype(BF16)
        pre = g_ref[c] + bg_ref[...]
        logf = _log_sigmoid(pre)
        for d in range(2):
            within = (lan <= sub) if d == 0 else (lan >= sub)
            bcum = _dot3(logf, ((sub <= lan) if d == 0 else (sub >= lan)).astype(BF16))
            li4 = pre[4 * d:4 * d + 4, :]
            lf4 = logf[8 + 4 * d:12 + 4 * d, :]
            b4 = bcum[8 + 4 * d:12 + 4 * d, :]
            r4 = li4 - b4
            blast4 = b4[:, CHUNK - 1:CHUNK] if d == 0 else b4[:, 0:1]
            wlog4 = blast4 + r4
            wmax4 = jnp.max(wlog4, axis=-1, keepdims=True)
            full = lambda a: jnp.broadcast_to(a, (4, CHUNK))
            rows_sc[2 * c + d] = jnp.concatenate(
                [r4, wlog4, full(blast4), full(wmax4), jnp.zeros((16, CHUNK), F32)], axis=0)
            for h in range(N_HEADS_M):
                cummax = jnp.max(jnp.where(within, r4[h:h + 1, :], neg_inf), axis=-1, keepdims=True)
                cmb_sc[(2 * c + d) * N_HEADS_M + h] = jnp.broadcast_to(cummax, (CHUNK, CHUNK))
                bb_sc[(2 * c + d) * N_HEADS_M + h] = _dot3(jnp.where(within, lf4[h:h + 1, :], 0.0), ones)

    for d in range(2):
        if has_state:
            c0_ref, n0_ref, m0_ref = refs[7:10]
            ncols = jnp.concatenate([n0_ref[d], jnp.zeros((12, DH_M), F32), zeros112], axis=0).T
        for h in range(N_HEADS_M):
            idx = 4 * d + h
            if has_state:
                st[idx][:, 0:DH_M] = c0_ref[d, h]
                st[idx][:, DH_M:] = _col_bcast(ncols, h)
                ms[idx][0:1, :] = m0_ref[idx:idx + 1, :]
            else:
                st[idx][...] = jnp.zeros((DH_M, 2 * DH_M), F32)
                ms[idx][0:1, :] = jnp.full((1, CHUNK), NEG_INIT, F32)

    def chunk_step(i, carry):
        for d in range(2):
            c = i if d == 0 else n_chunks - 1 - i
            r0 = pl.multiple_of(c * CHUNK, CHUNK)
            rows = rows_sc[2 * c + d]
            mask = (lan <= sub) if d == 0 else (lan >= sub)
            heads = []
            for h in range(N_HEADS_M):
                idx = 4 * d + h
                hs = slice(DH_M * h, DH_M * (h + 1))
                qc = q_ref[pl.ds(r0, CHUNK), hs]
                kt = kt_sc[c * N_HEADS_M + h]
                state = st[idx][...]
                heads.append(dict(idx=idx, hs=hs, qc=qc, kt=kt, state=state,
                                  s_raw=_dot(qc, kt), qs=_dot(qc, state.astype(BF16))))
            for h, hd in enumerate(heads):
                mp = ms[hd["idx"]][0:1, :]
                m_b = jnp.maximum(mp, cmb_sc[(2 * c + d) * N_HEADS_M + h])
                s = hd["s_raw"] * jnp.exp(jnp.where(mask, rows[h:h + 1, :] - m_b, neg_inf))
                s_hi = s.astype(BF16)
                vc = v_ref[pl.ds(r0, CHUNK), hd["hs"]]
                v_ones = jnp.concatenate([vc, ones], axis=1)
                hd.update(mp=mp, m_b=m_b, v_ones=v_ones, sv=_dot(s_hi, v_ones),
                          s_lo_sum=_dot((s - s_hi.astype(F32)).astype(BF16), ones))
            for h, hd in enumerate(heads):
                mp, m_b = hd["mp"], hd["m_b"]
                gw_b = jnp.exp(mp - m_b)
                en_b = jnp.exp(-(bb_sc[(2 * c + d) * N_HEADS_M + h] + m_b))
                num = hd["sv"][:, 0:DH_M] + gw_b * hd["qs"][:, 0:DH_M]
                den = hd["sv"][:, DH_M:] + hd["s_lo_sum"] + gw_b * hd["qs"][:, DH_M:]
                hdir[d][pl.ds(r0, CHUNK), hd["hs"]] = num / jnp.maximum(jnp.abs(den), en_b)
            for h, hd in enumerate(heads):
                mp = hd["mp"]
                gend = rows[8 + h:9 + h, :] + mp
                mnew = jnp.maximum(gend, rows[12 + h:13 + h, :])
                w_row = jnp.exp(rows[4 + h:5 + h, :] - mnew)
                kw_t = (hd["kt"].astype(F32) * w_row).astype(BF16)
                dec = jnp.exp(gend - mnew)
                st[hd["idx"]][...] = (jnp.concatenate([dec, dec], axis=1) * hd["state"]
                                      + _dot(kw_t, hd["v_ones"]))
                ms[hd["idx"]][0:1, :] = mnew
        return carry

    lax.fori_loop(0, n_chunks, chunk_step, 0)

    for h in range(N_HEADS_M):
        hs = slice(DH_M * h, DH_M * (h + 1))
        hh = hdir[0][:, hs] + hdir[1][:, hs]
        y = hh * lax.rsqrt(jnp.mean(hh * hh, axis=-1, keepdims=True) + EPS) * nw_ref[:, hs]
        hm_ref[:, hs] = (jax.nn.sigmoid(om_ref[:, hs]) * y).astype(BF16)
    if not has_state:
        c_ref, n_ref, m_ref = refs[n_in + 1:n_in + 4]
        for d in range(2):
            for h in range(N_HEADS_M):
                idx = 4 * d + h
                c_ref[d, h] = st[idx][:, 0:DH_M]
                n_ref[d, h:h + 1, :] = st[idx][:, DH_M:].T[0:1, :]
                m_ref[idx:idx + 1, :] = ms[idx][0:1, :]


def _mlstm(qm, km, vm, g3, bg, om, nw, state, *, n_batch, seq, first_block):
    n_chunks = seq // CHUNK
    rb = seq // TB
    tok = lambda: pl.BlockSpec((seq, 512), lambda b: (b + first_block // rb, 0))
    full = lambda a: pl.BlockSpec(a.shape, lambda b: (0,) * a.ndim)
    in_specs = [tok(), tok(), tok(),
                pl.BlockSpec((n_chunks, 16, CHUNK), lambda b: (b + first_block // rb, 0, 0)),
                full(bg), tok(), full(nw)]
    args = [qm, km, vm, g3, bg, om, nw]
    hm_spec = pl.BlockSpec((seq, 512), lambda b: (b, 0))
    hm_shape = jax.ShapeDtypeStruct((n_batch * seq, 512), BF16)
    scratch = ([pltpu.VMEM((DH_M, 2 * DH_M), F32)] * 8 + [pltpu.VMEM((8, CHUNK), F32)] * 8
               + [pltpu.VMEM((seq, 512), F32)] * 2
               + [pltpu.VMEM((2 * n_chunks, 32, CHUNK), F32),
                  pltpu.VMEM((8 * n_chunks, CHUNK, CHUNK), F32),
                  pltpu.VMEM((8 * n_chunks, CHUNK, CHUNK), F32),
                  pltpu.VMEM((N_HEADS_M * n_chunks, DH_M, CHUNK), BF16)])
    if state is not None:
        c0, n0, m0 = state
        in_specs += [pl.BlockSpec((None, None, 2, N_HEADS_M, DH_M, DH_M), lambda b: (b, 0, 0, 0, 0, 0)),
                     pl.BlockSpec((None, None, 2, N_HEADS_M, DH_M), lambda b: (b, 0, 0, 0, 0)),
                     pl.BlockSpec((None, 8, CHUNK), lambda b: (b, 0, 0))]
        args += [c0, n0, m0]
        out_specs, out_shape = hm_spec, hm_shape
    else:
        out_specs = [hm_spec,
                     pl.BlockSpec((None, 2, N_HEADS_M, DH_M, DH_M), lambda b: (b, 0, 0, 0, 0)),
                     pl.BlockSpec((None, 2, N_HEADS_M, DH_M), lambda b: (b, 0, 0, 0)),
                     pl.BlockSpec((None, 8, CHUNK), lambda b: (b, 0, 0))]
        out_shape = [hm_shape,
                     jax.ShapeDtypeStruct((n_batch, 2, N_HEADS_M, DH_M, DH_M), F32),
                     jax.ShapeDtypeStruct((n_batch, 2, N_HEADS_M, DH_M), F32),
                     jax.ShapeDtypeStruct((n_batch, 8, CHUNK), F32)]
    return pl.pallas_call(
        functools.partial(_mlstm_kernel, has_state=state is not None, n_chunks=n_chunks),
        grid=(n_batch,),
        in_specs=in_specs,
        out_specs=out_specs,
        out_shape=out_shape,
        scratch_shapes=scratch,
        compiler_params=_cparams(("arbitrary",)),
        name="mlstm_lat" if state is not None else "mlstm_ctx",
    )(*args)


def _outproj_kernel(attc_ref, attl_ref, hmc_ref, hml_ref, xc_ref, xl_ref, mod_ref, nw_ref, wo_ref,
                    rwh_ref, rwl_ref, wsg_ref, wsu_ref, wsd_ref, base_ref, hn_ref, lt_ref):
    y = (_dot(_ctx_or_lat(attc_ref, attl_ref), wo_ref[0:D_ATT, :])
         + _dot(_ctx_or_lat(hmc_ref, hml_ref), wo_ref[D_ATT:, :]))
    x1 = _ctx_or_lat(xc_ref, xl_ref) + mod_ref[2:3, :] * y
    z = x1 * lax.rsqrt(jnp.mean(x1 * x1, axis=-1, keepdims=True) + EPS) * nw_ref[...]
    hn = z * (1.0 + mod_ref[4:5, :]) + mod_ref[3:4, :]
    hb = hn.astype(BF16)
    hl = (hn - hb.astype(F32)).astype(BF16)
    nt = lambda w, t: lax.dot_general(w, t, _NT, preferred_element_type=F32)
    lt_ref[...] = nt(rwh_ref[...], hb) + nt(rwl_ref[...], hb) + nt(rwh_ref[...], hl)
    a = _silu(_dot(hb, wsg_ref[...])) * _dot(hb, wsu_ref[...])
    shared = _dot(a.astype(BF16), wsd_ref[...])
    base_ref[...] = x1 + mod_ref[5:6, :] * shared
    packed = pltpu.pack_elementwise([hn[:, :512], hn[:, 512:]], packed_dtype=BF16)
    for c in range(4):
        hn_ref[pl.ds(c, TBP, stride=4), :] = packed[:, 128 * c:128 * (c + 1)]


def _outproj(att, hm, x, mod3, norm2_w, w_out, rw_hi, rw_lo, wsg, wsu, wsd):
    tok = lambda w: pl.BlockSpec((TBP, w), lambda i: (i, 0))
    full = lambda a: pl.BlockSpec(a.shape, lambda i: (0,) * a.ndim)
    return pl.pallas_call(
        _outproj_kernel,
        grid=(T_ALL // TBP,),
        in_specs=_split_specs(512) + _split_specs(512) + _split_specs(D_MODEL) + [
                  pl.BlockSpec((None, 6, D_MODEL), lambda i: (_mod_row(i), 0, 0)),
                  full(norm2_w), full(w_out), full(rw_hi), full(rw_lo), full(wsg), full(wsu), full(wsd)],
        out_specs=[tok(D_MODEL), pl.BlockSpec((4 * TBP, 128), lambda i: (i, 0)),
                   pl.BlockSpec((N_EXPERTS, TBP), lambda i: (0, i))],
        out_shape=[jax.ShapeDtypeStruct((T_ALL, D_MODEL), F32),
                   jax.ShapeDtypeStruct((4 * T_ALL, 128), jnp.uint32),
                   jax.ShapeDtypeStruct((N_EXPERTS, T_ALL), F32)],
        compiler_params=_cparams(("arbitrary",)),
        name="outproj",
    )(*att, *hm, *x, mod3, norm2_w, w_out, rw_hi, rw_lo, wsg, wsu, wsd)


def _first_max(vals, ids, limit):
    m = functools.reduce(jnp.maximum, [jnp.max(v, axis=0, keepdims=True) for v in vals])
    cand = [jnp.min(jnp.where(v == m, i, limit), axis=0, keepdims=True) for v, i in zip(vals, ids)]
    return m, functools.reduce(jnp.minimum, cand)


def _router_kernel(lt_ref, bias_ref, gate_ref, idx_ref):
    n = lt_ref.shape[1]
    score = jax.nn.sigmoid(lt_ref[...])
    biased = score + bias_ref[...]
    sub = lax.broadcasted_iota(jnp.int32, (8, n), 0).astype(F32)
    neg_inf = F32(-jnp.inf)
    slabs = [biased[8 * g:8 * (g + 1), :] for g in range(N_GROUPS)]
    gs = []
    for sl in slabs:
        m1, i1 = _first_max([sl], [sub], 8.0)
        m2 = jnp.max(jnp.where(sub == i1, neg_inf, sl), axis=0, keepdims=True)
        gs.append(m1 + m2)
    cur = jnp.concatenate(gs, axis=0)
    gsel = jnp.zeros((8, n), F32)
    for _ in range(TOPK_GROUPS):
        _, i = _first_max([cur], [sub], 8.0)
        hit = sub == i
        gsel = jnp.where(hit, 1.0, gsel)
        cur = jnp.where(hit, neg_inf, cur)
    vals = [jnp.where(gsel[g:g + 1, :] > 0.0, slabs[g], neg_inf) for g in range(N_GROUPS)]
    ids = [sub + 8.0 * g for g in range(N_GROUPS)]
    picked = [jnp.zeros((8, n), F32) for _ in range(N_GROUPS)]
    order = []
    for _ in range(TOP_K):
        _, i = _first_max(vals, ids, float(N_EXPERTS))
        order.append(i)
        hits = [idg == i for idg in ids]
        picked = [jnp.where(hh, score[8 * g:8 * (g + 1), :], p) for g, (p, hh) in enumerate(zip(picked, hits))]
        vals = [jnp.where(hh, neg_inf, v) for v, hh in zip(vals, hits)]
    total = functools.reduce(jnp.add, [jnp.sum(p, axis=0, keepdims=True) for p in picked])
    gate_t = jnp.concatenate([p / total * ROUTED_SCALE for p in picked]
                             + [jnp.zeros((128 - N_EXPERTS, n), F32)], axis=0)
    gate_ref[...] = gate_t.T
    idx_ref[...] = jnp.concatenate(order, axis=0).astype(jnp.int32)


def _router(logits_t, bias_col):
    return pl.pallas_call(
        _router_kernel,
        grid=(T_ALL // TB_MOE,),
        in_specs=[pl.BlockSpec((N_EXPERTS, TB_MOE), lambda i: (0, i)),
                  pl.BlockSpec((N_EXPERTS, 1), lambda i: (0, 0))],
        out_specs=[pl.BlockSpec((TB_MOE, 128), lambda i: (i, 0)),
                   pl.BlockSpec((TOP_K, TB_MOE), lambda i: (0, i))],
        out_shape=[jax.ShapeDtypeStruct((T_ALL, 128), F32),
                   jax.ShapeDtypeStruct((TOP_K, T_ALL), jnp.int32)],
        compiler_params=_cparams(("arbitrary",)),
        name="router",
    )(logits_t, bias_col)


def _plan_kernel(idx_ref, slot_ref, cnt_ref, off_ref, pos_sc):
    n_tiles = T_HALF // 128
    eid = lax.broadcasted_iota(jnp.int32, (N_EXPERTS, 128), 0)
    tri = (lax.broadcasted_iota(jnp.int32, (128, 128), 0)
           <= lax.broadcasted_iota(jnp.int32, (128, 128), 1)).astype(BF16)
    carry = jnp.zeros((N_EXPERTS, 1), F32)
    for j in range(n_tiles):
        it = idx_ref[:, 128 * j:128 * (j + 1)]
        sel = jnp.zeros((N_EXPERTS, 128), F32)
        for k in range(TOP_K):
            sel = jnp.where(it[k:k + 1, :] == eid, 1.0, sel)
        inc = _dot(sel.astype(BF16), tri) + carry
        carry = inc[:, 127:128]
        pos_sc[:, 128 * j:128 * (j + 1)] = inc - 1.0
    count = jnp.broadcast_to(carry, (N_EXPERTS, 128))
    padded = jnp.floor((count + (MT - 1.0)) * (1.0 / MT)) * MT
    before = (lax.broadcasted_iota(jnp.int32, (N_EXPERTS, N_EXPERTS), 1)
              < lax.broadcasted_iota(jnp.int32, (N_EXPERTS, N_EXPERTS), 0)).astype(BF16)
    hi, mid, lo = _split3(padded)
    off = _dot(before, hi) + _dot(before, mid) + _dot(before, lo)
    cnt_ref[...] = count.astype(jnp.int32)
    off_ref[...] = off.astype(jnp.int32)

    def slots(j):
        it = idx_ref[:, 128 * j:128 * (j + 1)]
        val = off + pos_sc[:, 128 * j:128 * (j + 1)]
        rows = [jnp.sum(jnp.where(it[k:k + 1, :] == eid, val, 0.0), axis=0, keepdims=True)
                for k in range(TOP_K)]
        return jnp.concatenate(rows, axis=0).astype(jnp.int32)

    for j in range(n_tiles // 2):
        slot_ref[:, 128 * j:128 * (j + 1)] = slots(j) | (slots(j + n_tiles // 2) << 16)


def _plan(idx8):
    return pl.pallas_call(
        _plan_kernel,
        grid=(2,),
        in_specs=[pl.BlockSpec((TOP_K, T_HALF), lambda h: (0, h))],
        out_specs=[pl.BlockSpec((None, TOP_K, T_HALF // 2), lambda h: (h, 0, 0)),
                   pl.BlockSpec((None, N_EXPERTS, 128), lambda h: (h, 0, 0)),
                   pl.BlockSpec((None, N_EXPERTS, 128), lambda h: (h, 0, 0))],
        out_shape=[jax.ShapeDtypeStruct((2, TOP_K, T_HALF // 2), jnp.int32),
                   jax.ShapeDtypeStruct((2, N_EXPERTS, 128), jnp.int32),
                   jax.ShapeDtypeStruct((2, N_EXPERTS, 128), jnp.int32)],
        scratch_shapes=[pltpu.VMEM((N_EXPERTS, T_HALF), F32)],
        compiler_params=_cparams(("arbitrary",)),
        name="plan",
    )(idx8)


def _prepare_half(slot_ref, cnt_ref, off_ref, tok_ref, tile_e, tile_first, elist, h):
    def per_expert(x, carry):
        j, q = carry
        n = cnt_ref[h * N_EXPERTS + x]
        first = off_ref[h * N_EXPERTS + x]
        tiles = (n + MT - 1) // MT
        elist[q] = x

        def mark(i, c):
            tile_e[j + i] = x
            tile_first[j + i] = jnp.where(i == 0, 1, 0)
            return c

        lax.fori_loop(0, tiles, mark, 0)

        def pad(p, c):
            tok_ref[first + p] = T_HALF
            return c

        lax.fori_loop(n, tiles * MT, pad, 0)
        return j + tiles, q + jnp.where(tiles > 0, 1, 0)

    n_tiles, n_live = lax.fori_loop(0, N_EXPERTS, per_expert, (0, 0))
    for extra in range(2):
        tile_e[n_tiles + extra] = 0
        tile_first[n_tiles + extra] = 0

    def pad_tail(p, c):
        tok_ref[n_tiles * MT + p] = T_HALF
        return c

    lax.fori_loop(0, 2 * MT, pad_tail, 0)
    return n_tiles, n_live


def _invert_slots(slot_ref, tok_ref, h):
    half = T_HALF // 2
    for k in range(TOP_K):
        def body(i, carry, k=k):
            words = [slot_ref[(h * TOP_K + k) * half + i * INV_UNROLL + u] for u in range(INV_UNROLL)]
            for u in range(INV_UNROLL):
                t = i * INV_UNROLL + u
                tok_ref[words[u] & 0xFFFF] = t
                tok_ref[lax.shift_right_logical(words[u], 16)] = t + half
            return carry

        lax.fori_loop(0, half // INV_UNROLL, body, 0)


def _moe_kernel(slot_ref, cnt_ref, off_ref, src_hbm, gate_hbm, wg_hbm, wu_hbm, wd_hbm, out_hbm,
                src_v, gate_v, acc_v, wg_l, wu_l, wd_l, wgu_b, wd_b, xbuf0, xbuf1, gbuf0, gbuf1,
                ybuf0, ybuf1, tok_ref, tile_e, tile_first, elist, live, sem, wsem):
    h = pl.program_id(0)
    xbuf, gbuf, ybuf = (xbuf0, xbuf1), (gbuf0, gbuf1), (ybuf0, ybuf1)

    src_cp = pltpu.make_async_copy(src_hbm.at[pl.ds(pl.multiple_of(h * (4 * T_HALF), 8), 4 * T_HALF)],
                                   src_v.at[pl.ds(0, 4 * T_HALF)], sem.at[0])
    gate_cp = pltpu.make_async_copy(gate_hbm.at[pl.ds(pl.multiple_of(h * T_HALF, 8), T_HALF)],
                                    gate_v.at[pl.ds(0, T_HALF)], sem.at[1])
    src_cp.start()
    gate_cp.start()

    def weight_copies(x, slot):
        return [pltpu.make_async_copy(w_hbm.at[x], w_l.at[slot], wsem.at[slot, i])
                for i, (w_hbm, w_l) in enumerate(((wg_hbm, wg_l), (wu_hbm, wu_l), (wd_hbm, wd_l)))]

    n_tiles, n_live = _prepare_half(slot_ref, cnt_ref, off_ref, tok_ref, tile_e, tile_first, elist, h)
    live[0] = 0
    live[1] = n_live

    @pl.when(n_live > 0)
    def _():
        for cp in weight_copies(elist[0], 0):
            cp.start()

    acc_v[...] = jnp.zeros(acc_v.shape, F32)
    zero = jnp.zeros((8, 128), F32)
    src_v[pl.ds(4 * T_HALF, 8), :] = pltpu.pack_elementwise([zero, zero], packed_dtype=BF16)
    gate_v[pl.ds(T_HALF, 8), :] = zero
    ybuf0[...] = jnp.zeros(ybuf0.shape, F32)
    ybuf1[...] = jnp.zeros(ybuf1.shape, F32)
    _invert_slots(slot_ref, tok_ref, h)
    src_cp.wait()
    gate_cp.wait()

    def switch_expert():
        q = live[0]
        slot = q % 2
        for cp in weight_copies(elist[q], slot):
            cp.wait()
        wgu_b[:, 0:D_EXPERT] = wg_l[slot].astype(BF16)
        wgu_b[:, D_EXPERT:] = wu_l[slot].astype(BF16)
        wd_b[...] = wd_l[slot].astype(BF16)
        live[0] = q + 1

        @pl.when(q + 1 < live[1])
        def _():
            for cp in weight_copies(elist[q + 1], 1 - slot):
                cp.start()

    def gather(j, xb, gb, rows=(0, MT)):
        base = j * MT
        for m in range(*rows):
            t = tok_ref[base + m]
            xb[pl.ds(m, 4, stride=MT_STRIDE), :] = src_v[pl.ds(pl.multiple_of(t * 4, 4), 4), :]
            gb[m:m + 1, :] = gate_v[pl.ds(t, 1), :]

    def scatter(j, yb, rows=(0, MT)):
        base = j * MT
        for b in range(rows[0] // RMW_BATCH, rows[1] // RMW_BATCH):
            ms = [b * RMW_BATCH + u for u in range(RMW_BATCH)]
            targets = [pl.ds(pl.multiple_of(tok_ref[base + m] * 8, 8), 8) for m in ms]
            vals = [acc_v[r, :] + yb[pl.ds(m, 8, stride=MT_STRIDE), :] for r, m in zip(targets, ms)]
            for r, v in zip(targets, vals):
                acc_v[r, :] = v

    quarters = [(q * MT // 4, (q + 1) * MT // 4) for q in range(4)]

    def step(j, p):
        pl.when(tile_first[j] == 1)(switch_expert)
        xb, gb, yb = xbuf[p], gbuf[p], ybuf[p]
        nxt = (j + 1, xbuf[1 - p], gbuf[1 - p])
        prv = (jnp.maximum(j - 1, 0), ybuf[1 - p])
        lo, hi = [], []
        for c in range(4):
            words = xb[MT_STRIDE * c:MT_STRIDE * c + MT, :]
            unpack = functools.partial(pltpu.unpack_elementwise, words, packed_dtype=BF16, unpacked_dtype=F32)
            lo.append(unpack(index=0).astype(BF16))
            hi.append(unpack(index=1).astype(BF16))
        x = jnp.concatenate(lo + hi, axis=1)
        gather(*nxt, rows=quarters[0])
        h_gate = _dot(x, wgu_b[:, 0:D_EXPERT])
        gather(*nxt, rows=quarters[1])
        h_up = _dot(x, wgu_b[:, D_EXPERT:])
        gather(*nxt, rows=quarters[2])
        g = gb[...]
        g_hi = g.astype(BF16)
        g_lo = (g - g_hi.astype(F32)).astype(BF16)
        pick = (lax.broadcasted_iota(jnp.int32, (128, D_EXPERT), 0) == tile_e[j]).astype(BF16)
        g_col = _dot(g_hi, pick) + _dot(g_lo, pick)
        a = (_silu(h_gate) * h_up * g_col).astype(BF16)
        gather(*nxt, rows=quarters[3])
        scatter(*prv, rows=quarters[0])
        y_lo = _dot(a, wd_b[:, 0:D_MODEL // 2])
        scatter(*prv, rows=quarters[1])
        y_hi = _dot(a, wd_b[:, D_MODEL // 2:])
        scatter(*prv, rows=quarters[2])
        for c in range(4):
            yb[MT_STRIDE * c:MT_STRIDE * c + MT, :] = y_lo[:, 128 * c:128 * (c + 1)]
            yb[MT_STRIDE * (c + 4):MT_STRIDE * (c + 4) + MT, :] = y_hi[:, 128 * c:128 * (c + 1)]
        scatter(*prv, rows=quarters[3])

    gather(0, xbuf[0], gbuf[0])
    n_pairs = (n_tiles + 1) // 2

    def pair(i, carry):
        step(2 * i, 0)
        step(2 * i + 1, 1)
        return carry

    lax.fori_loop(0, n_pairs, pair, 0)
    scatter(jnp.maximum(2 * n_pairs - 1, 0), ybuf[1])

    out_cp = pltpu.make_async_copy(acc_v.at[pl.ds(0, 8 * T_HALF)],
                                   out_hbm.at[pl.ds(pl.multiple_of(h * (8 * T_HALF), 8), 8 * T_HALF)],
                                   sem.at[2])
    out_cp.start()
    out_cp.wait()


def _moe(slot_words, cnt, off, src, gate, wg, wu, wd):
    any_spec = pl.BlockSpec(memory_space=pl.ANY)
    tile_buf = lambda rows, dt: pltpu.VMEM((rows * MT_STRIDE, 128), dt)
    return pl.pallas_call(
        _moe_kernel,
        grid_spec=pltpu.PrefetchScalarGridSpec(
            num_scalar_prefetch=3,
            grid=(2,),
            in_specs=[any_spec] * 5,
            out_specs=any_spec,
            scratch_shapes=[pltpu.VMEM((4 * T_HALF + 8, 128), jnp.uint32),
                            pltpu.VMEM((T_HALF + 8, 128), F32),
                            pltpu.VMEM((8 * T_HALF + 8, 128), F32),
                            pltpu.VMEM((2, D_MODEL, D_EXPERT), F32),
                            pltpu.VMEM((2, D_MODEL, D_EXPERT), F32),
                            pltpu.VMEM((2, D_EXPERT, D_MODEL), F32),
                            pltpu.VMEM((D_MODEL, 2 * D_EXPERT), BF16),
                            pltpu.VMEM((D_EXPERT, D_MODEL), BF16),
                            tile_buf(4, jnp.uint32), tile_buf(4, jnp.uint32),
                            pltpu.VMEM((MT, 128), F32), pltpu.VMEM((MT, 128), F32),
                            tile_buf(8, F32), tile_buf(8, F32),
                            pltpu.SMEM((SLOT_CAP,), jnp.int32),
                            pltpu.SMEM((NT_MAX + 2,), jnp.int32),
                            pltpu.SMEM((NT_MAX + 2,), jnp.int32),
                            pltpu.SMEM((N_EXPERTS,), jnp.int32),
                            pltpu.SMEM((2,), jnp.int32),
                            pltpu.SemaphoreType.DMA((3,)),
                            pltpu.SemaphoreType.DMA((2, 3))]),
        out_shape=jax.ShapeDtypeStruct((8 * T_ALL, 128), F32),
        compiler_params=_cparams(("arbitrary",)),
        name="moe",
    )(slot_words, cnt, off, src, gate, wg, wu, wd)


def _finalize_kernel(acc_ref, base_ref, mod_ref, oc_ref, ol_ref):
    def write(o_ref):
        for c in range(8):
            cs = slice(128 * c, 128 * (c + 1))
            o_ref[:, cs] = base_ref[:, cs] + mod_ref[5:6, cs] * acc_ref[pl.ds(c, TBP, stride=8), :]

    is_ctx = pl.program_id(0) < NP_CTX
    pl.when(is_ctx)(lambda: write(oc_ref))
    pl.when(jnp.logical_not(is_ctx))(lambda: write(ol_ref))


def _finalize(acc, base, mod3):
    return pl.pallas_call(
        _finalize_kernel,
        grid=(T_ALL // TBP,),
        in_specs=[pl.BlockSpec((8 * TBP, 128), lambda i: (i, 0)),
                  pl.BlockSpec((TBP, D_MODEL), lambda i: (i, 0)),
                  pl.BlockSpec((None, 6, D_MODEL), lambda i: (_mod_row(i), 0, 0))],
        out_specs=_split_specs(D_MODEL),
        out_shape=[jax.ShapeDtypeStruct((T_CTX, D_MODEL), F32), jax.ShapeDtypeStruct((T_LAT, D_MODEL), F32)],
        compiler_params=_cparams(("arbitrary",)),
        name="finalize",
    )(acc, base, mod3)


def _head_indicators(width):
    head = jnp.arange(width) // DH_ATT
    ind = (head[:, None] == jnp.arange(128)[None, :]).astype(BF16)
    return ind, ind.T


def kernel(x_prompt, x_sample, cache_attn_k, cache_attn_v, state_mlstm_c, state_mlstm_n, state_mlstm_m, c, c_ctx, w_mod, b_mod, norm1_w, norm2_w, w_in, q_norm_w, k_norm_w, b_gates, m_norm_w, w_out, router_w, router_bias, w_gate, w_up, w_down, ws_gate, ws_up, ws_down):
    x = (x_prompt.reshape(T_CTX, D_MODEL), x_sample.reshape(T_LAT, D_MODEL))
    mod3 = _modulation(c, c_ctx, w_mod[0], b_mod[0])

    w_main = w_in[0, :, :P_MAIN].astype(BF16)
    w_gates_t = w_in[0, :, P_MAIN:].T.astype(BF16)
    qw = jnp.tile(q_norm_w, (1, N_HEADS_ATT))
    kw = jnp.tile(k_norm_w, (1, N_KV_HEADS))
    inds = _head_indicators(512) + _head_indicators(128)
    qn, kn, va, qm, km, vm, om, gt = _inproj(*x, mod3, norm1_w, w_main, w_gates_t, qw, kw, inds, _rope_tables())

    att_c = _attention(qn, kn, va, None, n_batch=BATCH, seq=SEQ, first_row=0)
    att_l = _attention(qn, kn, va, (cache_attn_k, cache_attn_v), n_batch=DEC_BATCH, seq=DEC_SEQ,
                       first_row=T_CTX)

    g3 = gt.reshape(16, T_ALL // CHUNK, CHUNK).transpose(1, 0, 2)
    bg = b_gates.reshape(16, 1)
    hm_c, c_new, n_new, m_new = _mlstm(qm, km, vm, g3, bg, om, m_norm_w, None,
                                       n_batch=BATCH, seq=SEQ, first_block=0)
    m0 = jnp.broadcast_to(state_mlstm_m.reshape(DEC_BATCH, 8, 1), (DEC_BATCH, 8, CHUNK))
    hm_l = _mlstm(qm, km, vm, g3, bg, om, m_norm_w, (state_mlstm_c, state_mlstm_n, m0),
                  n_batch=DEC_BATCH, seq=DEC_SEQ, first_block=N_CTX_BLOCKS)

    rw_t = router_w[0].T
    rw_hi = rw_t.astype(BF16)
    rw_lo = (rw_t - rw_hi.astype(F32)).astype(BF16)
    base, hn2, logits_t = _outproj((att_c, att_l), (hm_c, hm_l), x, mod3, norm2_w, w_out[0].astype(BF16),
                                   rw_hi, rw_lo, ws_gate[0].astype(BF16), ws_up[0].astype(BF16),
                                   ws_down[0].astype(BF16))
    gate, idx8 = _router(logits_t, router_bias.reshape(N_EXPERTS, 1))
    slot_words, cnt, off = _plan(idx8)
    acc = _moe(slot_words.reshape(TOP_K * T_HALF), cnt[:, :, 0].reshape(2 * N_EXPERTS),
               off[:, :, 0].reshape(2 * N_EXPERTS), hn2, gate, w_gate[0], w_up[0], w_down[0])
    out_c, out_l = _finalize(acc, base, mod3)

    y_prompt = out_c.reshape(BATCH, SEQ, D_MODEL)
    y_sample = out_l.reshape(DEC_BATCH, DEC_SEQ, D_MODEL)
    to_cache = lambda a: a[:T_CTX].reshape(BATCH, SEQ, N_KV_HEADS, DH_ATT).transpose(0, 2, 1, 3)[:, None]
    new_k = to_cache(kn)
    new_v = to_cache(va)
    new_m = m_new[:, :, 0].reshape(BATCH, 1, 2, N_HEADS_M)
    return (y_prompt, y_sample, new_k, new_v, c_new[:, None], n_new[:, None], new_m)
```

```python
import functools

import jax
import jax.numpy as jnp
from jax import lax
from jax.experimental import pallas as pl
from jax.experimental.pallas import tpu as pltpu

F32 = jnp.float32
BF16 = jnp.bfloat16

D_MODEL = 1024
BATCH = 32
SEQ = 256
DEC_BATCH = 2
DEC_SEQ = 1024
PAST_LEN = 256
GRID_W = 64
N_HEADS_ATT = 8
N_KV_HEADS = 2
DH_ATT = 64
D_ATT = 512
ROPE_THETA = 10000.0
N_HEADS_M = 4
DH_M = 128
D_M = 512
CHUNK = 128
N_EXPERTS = 64
TOP_K = 8
N_GROUPS = 8
TOPK_GROUPS = 4
D_EXPERT = 256
ROUTED_SCALE = 2.5
EPS = 1e-6
NEG_INIT = -1e30

T_CTX = BATCH * SEQ
T_LAT = DEC_BATCH * DEC_SEQ
T_ALL = T_CTX + T_LAT
TB = 256
N_CTX_BLOCKS = T_CTX // TB
TBP = 512
NP_CTX = T_CTX // TBP
NP_LAT_PER_BATCH = DEC_SEQ // TBP
TB_MOE = 1024
T_HALF = T_ALL // 2
MT = 256
MT_STRIDE = MT + 8
NT_MAX = TOP_K * T_HALF // MT + N_EXPERTS
SLOT_CAP = (NT_MAX + 2) * MT
RMW_BATCH = 16
INV_UNROLL = 16
assert SLOT_CAP < 2 ** 16 and (T_HALF // 2) % INV_UNROLL == 0 and MT % RMW_BATCH == 0
P_MAIN = 2816
VMEM_LIMIT = 56 * 1024 * 1024

_NT = (((1,), (1,)), ((), ()))
_TN = (((0,), (0,)), ((), ()))


def _cparams(sem):
    return pltpu.CompilerParams(dimension_semantics=sem, vmem_limit_bytes=VMEM_LIMIT)


def _split3(x):
    hi = x.astype(BF16)
    r1 = x - hi.astype(F32)
    mid = r1.astype(BF16)
    lo = (r1 - mid.astype(F32)).astype(BF16)
    return hi, mid, lo


def _dot(a, b):
    return jnp.dot(a, b, preferred_element_type=F32)


def _dot3(x, m_bf16):
    hi, mid, lo = _split3(x)
    return _dot(hi, m_bf16) + _dot(mid, m_bf16) + _dot(lo, m_bf16)


def _dot2(x, m_bf16):
    hi = x.astype(BF16)
    lo = (x - hi.astype(F32)).astype(BF16)
    return _dot(hi, m_bf16) + _dot(lo, m_bf16)


def _silu(x):
    return x * jax.nn.sigmoid(x)


def _mod_row(i):
    return jnp.where(i < NP_CTX, 0, 1 + (i - NP_CTX) // NP_LAT_PER_BATCH)


def _mod_kernel(ct_ref, w_ref, b_ref, o_ref):
    s = _silu(ct_ref[...])
    w = w_ref[...]
    rows = [jnp.sum(w * s[:, r:r + 1], axis=0, keepdims=True) for r in range(3)]
    rows.append(jnp.zeros((5, w.shape[1]), F32))
    o_ref[...] = jnp.concatenate(rows, axis=0) + b_ref[...]


def _modulation(c, c_ctx, w_mod, b_mod):
    cvec = jnp.concatenate([c_ctx[None, :], c, jnp.zeros((5, D_MODEL), F32)], axis=0)
    nb = 1024
    out = pl.pallas_call(
        _mod_kernel,
        grid=(6 * D_MODEL // nb,),
        in_specs=[pl.BlockSpec((D_MODEL, 8), lambda j: (0, 0)),
                  pl.BlockSpec((D_MODEL, nb), lambda j: (0, j)),
                  pl.BlockSpec((1, nb), lambda j: (0, j))],
        out_specs=pl.BlockSpec((8, nb), lambda j: (0, j)),
        out_shape=jax.ShapeDtypeStruct((8, 6 * D_MODEL), F32),
        compiler_params=_cparams(("arbitrary",)),
        name="modulation",
    )(cvec.T, w_mod, b_mod[None, :])
    return out.reshape(8, 6, D_MODEL)


def _ctx_or_lat(ctx_ref, lat_ref):
    return jnp.where(pl.program_id(0) < NP_CTX, ctx_ref[...], lat_ref[...])


def _split_specs(width):
    return [pl.BlockSpec((TBP, width), lambda i: (jnp.minimum(i, NP_CTX - 1), 0)),
            pl.BlockSpec((TBP, width), lambda i: (jnp.maximum(i - NP_CTX, 0), 0))]


def _head_norm(x, ind, ind_t, w_row):
    ss = _dot2(x * x, ind)
    inv = lax.rsqrt(ss * (1.0 / DH_ATT) + EPS)
    return x * _dot2(inv, ind_t) * w_row


def _rope(x, cos, sin_signed):
    lane = lax.broadcasted_iota(jnp.int32, x.shape, 1)
    partner = jnp.where((lane % 32) < 16, pltpu.roll(x, 128 - 16, 1), pltpu.roll(x, 16, 1))
    return x * cos + partner * sin_signed


def _inproj_kernel(xc_ref, xl_ref, mod_ref, nw_ref, w_ref, wgt_ref, qw_ref, kw_ref, iq_ref, iqt_ref,
                   ik_ref, ikt_ref, cos_ref, sin_ref,
                   qn_ref, kn_ref, va_ref, qm_ref, km_ref, vm_ref, om_ref, gt_ref):
    x = _ctx_or_lat(xc_ref, xl_ref)
    y = x * lax.rsqrt(jnp.mean(x * x, axis=-1, keepdims=True) + EPS) * nw_ref[...]
    hn = y * (1.0 + mod_ref[1:2, :]) + mod_ref[0:1, :]
    hb = hn.astype(BF16)
    qn = _head_norm(_dot(hb, w_ref[:, 0:512]), iq_ref[...], iqt_ref[...], qw_ref[...])
    kn = _head_norm(_dot(hb, w_ref[:, 512:640]), ik_ref[...], ikt_ref[...], kw_ref[...])
    qn_ref[...] = qn.astype(BF16)
    kn_ref[...] = kn

    @pl.when(pl.program_id(0) >= NP_CTX)
    def _():
        cos, sin = cos_ref[...], sin_ref[...]
        for j in range(4):
            qn_ref[:, 128 * j:128 * (j + 1)] = _rope(qn[:, 128 * j:128 * (j + 1)], cos, sin).astype(BF16)
        kn_ref[...] = _rope(kn, cos, sin)

    va_ref[...] = _dot(hb, w_ref[:, 640:768])
    qm_ref[...] = _dot(hb, w_ref[:, 768:1280]).astype(BF16)
    km_ref[...] = (_dot(hb, w_ref[:, 1280:1792]) * (DH_M ** -0.5)).astype(BF16)
    vm_ref[...] = _dot(hb, w_ref[:, 1792:2304]).astype(BF16)
    om_ref[...] = _dot(hb, w_ref[:, 2304:2816])
    gt_ref[...] = lax.dot_general(wgt_ref[...], hb, _NT, preferred_element_type=F32)


def _inproj(x_ctx, x_lat, mod3, norm1_w, w_main, w_gates_t, qw, kw, inds, rope_tabs):
    tok = lambda w: pl.BlockSpec((TBP, w), lambda i: (i, 0))
    full = lambda a: pl.BlockSpec(a.shape, lambda i: (0,) * a.ndim)
    sd = lambda w, dt: jax.ShapeDtypeStruct((T_ALL, w), dt)
    rope_spec = pl.BlockSpec((TBP, 128), lambda i: (jnp.maximum(i - NP_CTX, 0) % NP_LAT_PER_BATCH, 0))
    return pl.pallas_call(
        _inproj_kernel,
        grid=(T_ALL // TBP,),
        in_specs=_split_specs(D_MODEL) + [
                  pl.BlockSpec((None, 6, D_MODEL), lambda i: (_mod_row(i), 0, 0)),
                  full(norm1_w), full(w_main), full(w_gates_t), full(qw), full(kw)]
                 + [full(a) for a in inds] + [rope_spec, rope_spec],
        out_specs=[tok(512), tok(128), tok(128), tok(512), tok(512), tok(512), tok(512),
                   pl.BlockSpec((16, TBP), lambda i: (0, i))],
        out_shape=[sd(512, BF16), sd(128, F32), sd(128, F32), sd(512, BF16), sd(512, BF16),
                   sd(512, BF16), sd(512, F32), jax.ShapeDtypeStruct((16, T_ALL), F32)],
        compiler_params=_cparams(("arbitrary",)),
        name="inproj",
    )(x_ctx, x_lat, mod3, norm1_w, w_main, w_gates_t, qw, kw, *inds, *rope_tabs)


def _rope_tables():
    t = jnp.arange(DEC_SEQ)
    pos = jnp.stack([t // GRID_W, t % GRID_W], axis=1).astype(F32)
    n_freq = DH_ATT // 4
    inv_freq = ROPE_THETA ** (-jnp.arange(n_freq, dtype=F32) / n_freq)
    ang = pos[:, :, None] * inv_freq
    cos, sin = jnp.cos(ang), jnp.sin(ang)
    cos_h = jnp.stack([cos, cos], axis=2).reshape(DEC_SEQ, DH_ATT)
    sin_h = jnp.stack([-sin, sin], axis=2).reshape(DEC_SEQ, DH_ATT)
    return jnp.tile(cos_h, (1, 2)), jnp.tile(sin_h, (1, 2))


def _attn_kernel(*refs, has_cache):
    if has_cache:
        q_ref, k_ref, v_ref, kc_ref, vc_ref, o_ref = refs
    else:
        q_ref, k_ref, v_ref, o_ref = refs
    q = q_ref[...]
    k = k_ref[...].astype(BF16)
    v = v_ref[...].astype(BF16)
    qb = q.shape[0]
    scale = DH_ATT ** -0.5
    outs = []
    for g in range(N_KV_HEADS):
        kg = k[:, DH_ATT * g:DH_ATT * (g + 1)]
        vg = v[:, DH_ATT * g:DH_ATT * (g + 1)]
        qg = jnp.concatenate([q[:, 256 * g + DH_ATT * j:256 * g + DH_ATT * (j + 1)] for j in range(4)], axis=0)
        s = lax.dot_general(qg, kg, _NT, preferred_element_type=F32) * scale
        m = jnp.max(s, axis=-1, keepdims=True)
        if has_cache:
            kc = kc_ref[g].astype(BF16)
            vc = vc_ref[g].astype(BF16)
            sc = lax.dot_general(qg, kc, _NT, preferred_element_type=F32) * scale
            m = jnp.maximum(m, jnp.max(sc, axis=-1, keepdims=True))
        p = jnp.exp(s - m)
        den = jnp.sum(p, axis=-1, keepdims=True)
        o = _dot(p.astype(BF16), vg)
        if has_cache:
            pc = jnp.exp(sc - m)
            den = den + jnp.sum(pc, axis=-1, keepdims=True)
            o = o + _dot(pc.astype(BF16), vc)
        o = o / den
        outs += [o[qb * j:qb * (j + 1), :] for j in range(4)]
    o_ref[...] = jnp.concatenate(outs, axis=1).astype(BF16)


def _attention(qn, kn, v_all, cache, *, n_batch, seq, first_row):
    qblocks = seq // TB
    kv_spec = pl.BlockSpec((seq, 128), lambda b, i: (b + first_row // seq, 0))
    in_specs = [pl.BlockSpec((TB, 512), lambda b, i: (first_row // TB + b * qblocks + i, 0)), kv_spec, kv_spec]
    args = [qn, kn, v_all]
    if cache is not None:
        cspec = pl.BlockSpec((None, None, N_KV_HEADS, PAST_LEN, DH_ATT), lambda b, i: (b, 0, 0, 0, 0))
        in_specs += [cspec, cspec]
        args += list(cache)
    return pl.pallas_call(
        functools.partial(_attn_kernel, has_cache=cache is not None),
        grid=(n_batch, qblocks),
        in_specs=in_specs,
        out_specs=pl.BlockSpec((TB, 512), lambda b, i: (b * qblocks + i, 0)),
        out_shape=jax.ShapeDtypeStruct((n_batch * seq, 512), BF16),
        compiler_params=_cparams(("arbitrary", "arbitrary")),
        name="attention_lat" if cache is not None else "attention_ctx",
    )(*args)


def _log_sigmoid(x):
    return jnp.minimum(x, 0.0) - jnp.log1p(jnp.exp(-jnp.abs(x)))


def _col_bcast(cols, j):
    return jnp.broadcast_to(cols[:, j:j + 1], (CHUNK, CHUNK))


def _mlstm_kernel(*refs, has_state, n_chunks):
    n_in = 10 if has_state else 7
    n_out = 1 if has_state else 4
    q_ref, k_ref, v_ref, g_ref, bg_ref, om_ref, nw_ref = refs[:7]
    hm_ref = refs[n_in]
    scratch = refs[n_in + n_out:]
    st = scratch[0:8]
    ms = scratch[8:16]
    hdir = scratch[16:18]
    rows_sc, cmb_sc, bb_sc, kt_sc = scratch[18:22]
    neg_inf = F32(-jnp.inf)
    zeros112 = jnp.zeros((CHUNK - 16, CHUNK), F32)
    sub = lax.broadcasted_iota(jnp.int32, (CHUNK, CHUNK), 0)
    lan = lax.broadcasted_iota(jnp.int32, (CHUNK, CHUNK), 1)
    ones = jnp.ones((CHUNK, CHUNK), BF16)

    for c in range(n_chunks):
        for h in range(N_HEADS_M):
            kc = k_ref[CHUNK * c:CHUNK * (c + 1), DH_M * h:DH_M * (h + 1)]
            kt_sc[c * N_HEADS_M + h] = kc.astype(F32).T.astype(BF16)
        pre = g_ref[c] + bg_ref[...]
        logf = _log_sigmoid(pre)
        for d in range(2):
            within = (lan <= sub) if d == 0 else (lan >= sub)
            bcum = _dot3(logf, ((sub <= lan) if d == 0 else (sub >= lan)).astype(BF16))
            li4 = pre[4 * d:4 * d + 4, :]
            lf4 = logf[8 + 4 * d:12 + 4 * d, :]
            b4 = bcum[8 + 4 * d:12 + 4 * d, :]
            r4 = li4 - b4
            blast4 = b4[:, CHUNK - 1:CHUNK] if d == 0 else b4[:, 0:1]
            wlog4 = blast4 + r4
            wmax4 = jnp.max(wlog4, axis=-1, keepdims=True)
            full = lambda a: jnp.broadcast_to(a, (4, CHUNK))
            rows_sc[2 * c + d] = jnp.concatenate(
                [r4, wlog4, full(blast4), full(wmax4), jnp.zeros((16, CHUNK), F32)], axis=0)
            for h in range(N_HEADS_M):
                cummax = jnp.max(jnp.where(within, r4[h:h + 1, :], neg_inf), axis=-1, keepdims=True)
                cmb_sc[(2 * c + d) * N_HEADS_M + h] = jnp.broadcast_to(cummax, (CHUNK, CHUNK))
                bb_sc[(2 * c + d) * N_HEADS_M + h] = _dot2(jnp.where(within, lf4[h:h + 1, :], 0.0), ones)

    for d in range(2):
        if has_state:
            c0_ref, n0_ref, m0_ref = refs[7:10]
            ncols = jnp.concatenate([n0_ref[d], jnp.zeros((12, DH_M), F32), zeros112], axis=0).T
        for h in range(N_HEADS_M):
            idx = 4 * d + h
            if has_state:
                st[idx][:, 0:DH_M] = c0_ref[d, h]
                st[idx][:, DH_M:] = _col_bcast(ncols, h)
                ms[idx][0:1, :] = m0_ref[idx:idx + 1, :]
            else:
                st[idx][...] = jnp.zeros((DH_M, 2 * DH_M), F32)
                ms[idx][0:1, :] = jnp.full((1, CHUNK), NEG_INIT, F32)

    def chunk_step(i, carry):
        for d in range(2):
            c = i if d == 0 else n_chunks - 1 - i
            r0 = pl.multiple_of(c * CHUNK, CHUNK)
            rows = rows_sc[2 * c + d]
            mask = (lan <= sub) if d == 0 else (lan >= sub)
            heads = []
            for h in range(N_HEADS_M):
                idx = 4 * d + h
                hs = slice(DH_M * h, DH_M * (h + 1))
                qc = q_ref[pl.ds(r0, CHUNK), hs]
                kt = kt_sc[c * N_HEADS_M + h]
                state = st[idx][...]
                heads.append(dict(idx=idx, hs=hs, qc=qc, kt=kt, state=state,
                                  s_raw=_dot(qc, kt), qs=_dot(qc, state.astype(BF16))))
            for h, hd in enumerate(heads):
                mp = ms[hd["idx"]][0:1, :]
                m_b = jnp.maximum(mp, cmb_sc[(2 * c + d) * N_HEADS_M + h])
                s = hd["s_raw"] * jnp.exp(jnp.where(mask, rows[h:h + 1, :] - m_b, neg_inf))
                s_hi = s.astype(BF16)
                vc = v_ref[pl.ds(r0, CHUNK), hd["hs"]]
                v_ones = jnp.concatenate([vc, ones], axis=1)
                hd.update(mp=mp, m_b=m_b, v_ones=v_ones, sv=_dot(s_hi, v_ones),
                          s_lo_sum=_dot((s - s_hi.astype(F32)).astype(BF16), ones))
            for h, hd in enumerate(heads):
                mp, m_b = hd["mp"], hd["m_b"]
                gw_b = jnp.exp(mp - m_b)
                en_b = jnp.exp(-(bb_sc[(2 * c + d) * N_HEADS_M + h] + m_b))
                num = hd["sv"][:, 0:DH_M] + gw_b * hd["qs"][:, 0:DH_M]
                den = hd["sv"][:, DH_M:] + hd["s_lo_sum"] + gw_b * hd["qs"][:, DH_M:]
                hdir[d][pl.ds(r0, CHUNK), hd["hs"]] = num / jnp.maximum(jnp.abs(den), en_b)
            for h, hd in enumerate(heads):
                mp = hd["mp"]
                gend = rows[8 + h:9 + h, :] + mp
                mnew = jnp.maximum(gend, rows[12 + h:13 + h, :])
                w_row = jnp.exp(rows[4 + h:5 + h, :] - mnew)
                kw_t = (hd["kt"].astype(F32) * w_row).astype(BF16)
                dec = jnp.exp(gend - mnew)
                st[hd["idx"]][...] = (jnp.concatenate([dec, dec], axis=1) * hd["state"]
                                      + _dot(kw_t, hd["v_ones"]))
                ms[hd["idx"]][0:1, :] = mnew
        return carry

    lax.fori_loop(0, n_chunks, chunk_step, 0)

    for h in range(N_HEADS_M):
        hs = slice(DH_M * h, DH_M * (h + 1))
        hh = hdir[0][:, hs] + hdir[1][:, hs]
        y = hh * lax.rsqrt(jnp.mean(hh * hh, axis=-1, keepdims=True) + EPS) * nw_ref[:, hs]
        hm_ref[:, hs] = (jax.nn.sigmoid(om_ref[:, hs]) * y).astype(BF16)
    if not has_state:
        c_ref, n_ref, m_ref = refs[n_in + 1:n_in + 4]
        for d in range(2):
            for h in range(N_HEADS_M):
                idx = 4 * d + h
                c_ref[d, h] = st[idx][:, 0:DH_M]
                n_ref[d, h:h + 1, :] = st[idx][:, DH_M:].T[0:1, :]
                m_ref[idx:idx + 1, :] = ms[idx][0:1, :]


def _mlstm(qm, km, vm, g3, bg, om, nw, state, *, n_batch, seq, first_block):
    n_chunks = seq // CHUNK
    rb = seq // TB
    tok = lambda: pl.BlockSpec((seq, 512), lambda b: (b + first_block // rb, 0))
    full = lambda a: pl.BlockSpec(a.shape, lambda b: (0,) * a.ndim)
    in_specs = [tok(), tok(), tok(),
                pl.BlockSpec((n_chunks, 16, CHUNK), lambda b: (b + first_block // rb, 0, 0)),
                full(bg), tok(), full(nw)]
    args = [qm, km, vm, g3, bg, om, nw]
    hm_spec = pl.BlockSpec((seq, 512), lambda b: (b, 0))
    hm_shape = jax.ShapeDtypeStruct((n_batch * seq, 512), BF16)
    scratch = ([pltpu.VMEM((DH_M, 2 * DH_M), F32)] * 8 + [pltpu.VMEM((8, CHUNK), F32)] * 8
               + [pltpu.VMEM((seq, 512), F32)] * 2
               + [pltpu.VMEM((2 * n_chunks, 32, CHUNK), F32),
                  pltpu.VMEM((8 * n_chunks, CHUNK, CHUNK), F32),
                  pltpu.VMEM((8 * n_chunks, CHUNK, CHUNK), F32),
                  pltpu.VMEM((N_HEADS_M * n_chunks, DH_M, CHUNK), BF16)])
    if state is not None:
        c0, n0, m0 = state
        in_specs += [pl.BlockSpec((None, None, 2, N_HEADS_M, DH_M, DH_M), lambda b: (b, 0, 0, 0, 0, 0)),
                     pl.BlockSpec((None, None, 2, N_HEADS_M, DH_M), lambda b: (b, 0, 0, 0, 0)),
                     pl.BlockSpec((None, 8, CHUNK), lambda b: (b, 0, 0))]
        args += [c0, n0, m0]
        out_specs, out_shape = hm_spec, hm_shape
    else:
        out_specs = [hm_spec,
                     pl.BlockSpec((None, 2, N_HEADS_M, DH_M, DH_M), lambda b: (b, 0, 0, 0, 0)),
                     pl.BlockSpec((None, 2, N_HEADS_M, DH_M), lambda b: (b, 0, 0, 0)),
                     pl.BlockSpec((None, 8, CHUNK), lambda b: (b, 0, 0))]
        out_shape = [hm_shape,
                     jax.ShapeDtypeStruct((n_batch, 2, N_HEADS_M, DH_M, DH_M), F32),
                     jax.ShapeDtypeStruct((n_batch, 2, N_HEADS_M, DH_M), F32),
                     jax.ShapeDtypeStruct((n_batch, 8, CHUNK), F32)]
    return pl.pallas_call(
        functools.partial(_mlstm_kernel, has_state=state is not None, n_chunks=n_chunks),
        grid=(n_batch,),
        in_specs=in_specs,
        out_specs=out_specs,
        out_shape=out_shape,
        scratch_shapes=scratch,
        compiler_params=_cparams(("arbitrary",)),
        name="mlstm_lat" if state is not None else "mlstm_ctx",
    )(*args)


def _outproj_kernel(attc_ref, attl_ref, hmc_ref, hml_ref, xc_ref, xl_ref, mod_ref, nw_ref, wo_ref,
                    rwh_ref, rwl_ref, wsg_ref, wsu_ref, wsd_ref, base_ref, hn_ref, lt_ref):
    y = (_dot(_ctx_or_lat(attc_ref, attl_ref), wo_ref[0:D_ATT, :])
         + _dot(_ctx_or_lat(hmc_ref, hml_ref), wo_ref[D_ATT:, :]))
    x1 = _ctx_or_lat(xc_ref, xl_ref) + mod_ref[2:3, :] * y
    z = x1 * lax.rsqrt(jnp.mean(x1 * x1, axis=-1, keepdims=True) + EPS) * nw_ref[...]
    hn = z * (1.0 + mod_ref[4:5, :]) + mod_ref[3:4, :]
    hb = hn.astype(BF16)
    hl = (hn - hb.astype(F32)).astype(BF16)
    nt = lambda w, t: lax.dot_general(w, t, _NT, preferred_element_type=F32)
    lt_ref[...] = nt(rwh_ref[...], hb) + nt(rwl_ref[...], hb) + nt(rwh_ref[...], hl)
    a = _silu(_dot(hb, wsg_ref[...])) * _dot(hb, wsu_ref[...])
    shared = _dot(a.astype(BF16), wsd_ref[...])
    base_ref[...] = x1 + mod_ref[5:6, :] * shared
    packed = pltpu.pack_elementwise([hn[:, :512], hn[:, 512:]], packed_dtype=BF16)
    for c in range(4):
        hn_ref[pl.ds(c, TBP, stride=4), :] = packed[:, 128 * c:128 * (c + 1)]


def _outproj(att, hm, x, mod3, norm2_w, w_out, rw_hi, rw_lo, wsg, wsu, wsd):
    tok = lambda w: pl.BlockSpec((TBP, w), lambda i: (i, 0))
    full = lambda a: pl.BlockSpec(a.shape, lambda i: (0,) * a.ndim)
    return pl.pallas_call(
        _outproj_kernel,
        grid=(T_ALL // TBP,),
        in_specs=_split_specs(512) + _split_specs(512) + _split_specs(D_MODEL) + [
                  pl.BlockSpec((None, 6, D_MODEL), lambda i: (_mod_row(i), 0, 0)),
                  full(norm2_w), full(w_out), full(rw_hi), full(rw_lo), full(wsg), full(wsu), full(wsd)],
        out_specs=[tok(D_MODEL), pl.BlockSpec((4 * TBP, 128), lambda i: (i, 0)),
                   pl.BlockSpec((N_EXPERTS, TBP), lambda i: (0, i))],
        out_shape=[jax.ShapeDtypeStruct((T_ALL, D_MODEL), F32),
                   jax.ShapeDtypeStruct((4 * T_ALL, 128), jnp.uint32),
                   jax.ShapeDtypeStruct((N_EXPERTS, T_ALL), F32)],
        compiler_params=_cparams(("arbitrary",)),
        name="outproj",
    )(*att, *hm, *x, mod3, norm2_w, w_out, rw_hi, rw_lo, wsg, wsu, wsd)


def _first_max(vals, ids, limit):
    m = functools.reduce(jnp.maximum, [jnp.max(v, axis=0, keepdims=True) for v in vals])
    cand = [jnp.min(jnp.where(v == m, i, limit), axis=0, keepdims=True) for v, i in zip(vals, ids)]
    return m, functools.reduce(jnp.minimum, cand)


def _router_kernel(lt_ref, bias_ref, gate_ref, idx_ref):
    n = lt_ref.shape[1]
    score = jax.nn.sigmoid(lt_ref[...])
    biased = score + bias_ref[...]
    sub = lax.broadcasted_iota(jnp.int32, (8, n), 0).astype(F32)
    neg_inf = F32(-jnp.inf)
    slabs = [biased[8 * g:8 * (g + 1), :] for g in range(N_GROUPS)]
    gs = []
    for sl in slabs:
        m1, i1 = _first_max([sl], [sub], 8.0)
        m2 = jnp.max(jnp.where(sub == i1, neg_inf, sl), axis=0, keepdims=True)
        gs.append(m1 + m2)
    cur = jnp.concatenate(gs, axis=0)
    gsel = jnp.zeros((8, n), F32)
    for _ in range(TOPK_GROUPS):
        _, i = _first_max([cur], [sub], 8.0)
        hit = sub == i
        gsel = jnp.where(hit, 1.0, gsel)
        cur = jnp.where(hit, neg_inf, cur)
    vals = [jnp.where(gsel[g:g + 1, :] > 0.0, slabs[g], neg_inf) for g in range(N_GROUPS)]
    ids = [sub + 8.0 * g for g in range(N_GROUPS)]
    picked = [jnp.zeros((8, n), F32) for _ in range(N_GROUPS)]
    order = []
    for _ in range(TOP_K):
        _, i = _first_max(vals, ids, float(N_EXPERTS))
        order.append(i)
        hits = [idg == i for idg in ids]
        picked = [jnp.where(hh, score[8 * g:8 * (g + 1), :], p) for g, (p, hh) in enumerate(zip(picked, hits))]
        vals = [jnp.where(hh, neg_inf, v) for v, hh in zip(vals, hits)]
    total = functools.reduce(jnp.add, [jnp.sum(p, axis=0, keepdims=True) for p in picked])
    gate_t = jnp.concatenate([p / total * ROUTED_SCALE for p in picked]
                             + [jnp.zeros((128 - N_EXPERTS, n), F32)], axis=0)
    gate_ref[...] = gate_t.T
    idx_ref[...] = jnp.concatenate(order, axis=0).astype(jnp.int32)


def _router(logits_t, bias_col):
    return pl.pallas_call(
        _router_kernel,
        grid=(T_ALL // TB_MOE,),
        in_specs=[pl.BlockSpec((N_EXPERTS, TB_MOE), lambda i: (0, i)),
                  pl.BlockSpec((N_EXPERTS, 1), lambda i: (0, 0))],
        out_specs=[pl.BlockSpec((TB_MOE, 128), lambda i: (i, 0)),
                   pl.BlockSpec((TOP_K, TB_MOE), lambda i: (0, i))],
        out_shape=[jax.ShapeDtypeStruct((T_ALL, 128), F32),
                   jax.ShapeDtypeStruct((TOP_K, T_ALL), jnp.int32)],
        compiler_params=_cparams(("arbitrary",)),
        name="router",
    )(logits_t, bias_col)


def _plan_kernel(idx_ref, slot_ref, cnt_ref, off_ref, pos_sc):
    n_tiles = T_HALF // 128
    eid = lax.broadcasted_iota(jnp.int32, (N_EXPERTS, 128), 0)
    tri = (lax.broadcasted_iota(jnp.int32, (128, 128), 0)
           <= lax.broadcasted_iota(jnp.int32, (128, 128), 1)).astype(BF16)
    carry = jnp.zeros((N_EXPERTS, 1), F32)
    for j in range(n_tiles):
        it = idx_ref[:, 128 * j:128 * (j + 1)]
        sel = jnp.zeros((N_EXPERTS, 128), F32)
        for k in range(TOP_K):
            sel = jnp.where(it[k:k + 1, :] == eid, 1.0, sel)
        inc = _dot(sel.astype(BF16), tri) + carry
        carry = inc[:, 127:128]
        pos_sc[:, 128 * j:128 * (j + 1)] = inc - 1.0
    count = jnp.broadcast_to(carry, (N_EXPERTS, 128))
    padded = jnp.floor((count + (MT - 1.0)) * (1.0 / MT)) * MT
    before = (lax.broadcasted_iota(jnp.int32, (N_EXPERTS, N_EXPERTS), 1)
              < lax.broadcasted_iota(jnp.int32, (N_EXPERTS, N_EXPERTS), 0)).astype(BF16)
    hi, mid, lo = _split3(padded)
    off = _dot(before, hi) + _dot(before, mid) + _dot(before, lo)
    cnt_ref[...] = count.astype(jnp.int32)
    off_ref[...] = off.astype(jnp.int32)

    def slots(j):
        it = idx_ref[:, 128 * j:128 * (j + 1)]
        val = off + pos_sc[:, 128 * j:128 * (j + 1)]
        rows = [jnp.sum(jnp.where(it[k:k + 1, :] == eid, val, 0.0), axis=0, keepdims=True)
                for k in range(TOP_K)]
        return jnp.concatenate(rows, axis=0).astype(jnp.int32)

    for j in range(n_tiles // 2):
        slot_ref[:, 128 * j:128 * (j + 1)] = slots(j) | (slots(j + n_tiles // 2) << 16)


def _plan(idx8):
    return pl.pallas_call(
        _plan_kernel,
        grid=(2,),
        in_specs=[pl.BlockSpec((TOP_K, T_HALF), lambda h: (0, h))],
        out_specs=[pl.BlockSpec((None, TOP_K, T_HALF // 2), lambda h: (h, 0, 0)),
                   pl.BlockSpec((None, N_EXPERTS, 128), lambda h: (h, 0, 0)),
                   pl.BlockSpec((None, N_EXPERTS, 128), lambda h: (h, 0, 0))],
        out_shape=[jax.ShapeDtypeStruct((2, TOP_K, T_HALF // 2), jnp.int32),
                   jax.ShapeDtypeStruct((2, N_EXPERTS, 128), jnp.int32),
                   jax.ShapeDtypeStruct((2, N_EXPERTS, 128), jnp.int32)],
        scratch_shapes=[pltpu.VMEM((N_EXPERTS, T_HALF), F32)],
        compiler_params=_cparams(("arbitrary",)),
        name="plan",
    )(idx8)


def _prepare_half(slot_ref, cnt_ref, off_ref, tok_ref, tile_e, tile_first, elist, h):
    def per_expert(x, carry):
        j, q = carry
        n = cnt_ref[h * N_EXPERTS + x]
        first = off_ref[h * N_EXPERTS + x]
        tiles = (n + MT - 1) // MT
        elist[q] = x

        def mark(i, c):
            tile_e[j + i] = x
            tile_first[j + i] = jnp.where(i == 0, 1, 0)
            return c

        lax.fori_loop(0, tiles, mark, 0)

        def pad(p, c):
            tok_ref[first + p] = T_HALF
            return c

        lax.fori_loop(n, tiles * MT, pad, 0)
        return j + tiles, q + jnp.where(tiles > 0, 1, 0)

    n_tiles, n_live = lax.fori_loop(0, N_EXPERTS, per_expert, (0, 0))
    for extra in range(2):
        tile_e[n_tiles + extra] = 0
        tile_first[n_tiles + extra] = 0

    def pad_tail(p, c):
        tok_ref[n_tiles * MT + p] = T_HALF
        return c

    lax.fori_loop(0, 2 * MT, pad_tail, 0)
    return n_tiles, n_live


def _invert_slots(slot_ref, tok_ref, h):
    half = T_HALF // 2
    for k in range(TOP_K):
        def body(i, carry, k=k):
            words = [slot_ref[(h * TOP_K + k) * half + i * INV_UNROLL + u] for u in range(INV_UNROLL)]
            for u in range(INV_UNROLL):
                t = i * INV_UNROLL + u
                tok_ref[words[u] & 0xFFFF] = t
                tok_ref[lax.shift_right_logical(words[u], 16)] = t + half
            return carry

        lax.fori_loop(0, half // INV_UNROLL, body, 0)


def _moe_kernel(slot_ref, cnt_ref, off_ref, src_hbm, gate_hbm, wg_hbm, wu_hbm, wd_hbm, out_hbm,
                src_v, gate_v, acc_v, wg_l, wu_l, wd_l, wgu_b, wd_b, xbuf0, xbuf1, gbuf0, gbuf1,
                ybuf0, ybuf1, tok_ref, tile_e, tile_first, elist, live, sem, wsem):
    h = pl.program_id(0)
    xbuf, gbuf, ybuf = (xbuf0, xbuf1), (gbuf0, gbuf1), (ybuf0, ybuf1)

    src_cp = pltpu.make_async_copy(src_hbm.at[pl.ds(pl.multiple_of(h * (4 * T_HALF), 8), 4 * T_HALF)],
                                   src_v.at[pl.ds(0, 4 * T_HALF)], sem.at[0])
    gate_cp = pltpu.make_async_copy(gate_hbm.at[pl.ds(pl.multiple_of(h * T_HALF, 8), T_HALF)],
                                    gate_v.at[pl.ds(0, T_HALF)], sem.at[1])
    src_cp.start()
    gate_cp.start()

    def weight_copies(x, slot):
        return [pltpu.make_async_copy(w_hbm.at[x], w_l.at[slot], wsem.at[slot, i])
                for i, (w_hbm, w_l) in enumerate(((wg_hbm, wg_l), (wu_hbm, wu_l), (wd_hbm, wd_l)))]

    n_tiles, n_live = _prepare_half(slot_ref, cnt_ref, off_ref, tok_ref, tile_e, tile_first, elist, h)
    live[0] = 0
    live[1] = n_live

    @pl.when(n_live > 0)
    def _():
        for cp in weight_copies(elist[0], 0):
            cp.start()

    acc_v[...] = jnp.zeros(acc_v.shape, F32)
    zero = jnp.zeros((8, 128), F32)
    src_v[pl.ds(4 * T_HALF, 8), :] = pltpu.pack_elementwise([zero, zero], packed_dtype=BF16)
    gate_v[pl.ds(T_HALF, 8), :] = zero
    ybuf0[...] = jnp.zeros(ybuf0.shape, F32)
    ybuf1[...] = jnp.zeros(ybuf1.shape, F32)
    _invert_slots(slot_ref, tok_ref, h)
    src_cp.wait()
    gate_cp.wait()

    def switch_expert():
        q = live[0]
        slot = q % 2
        for cp in weight_copies(elist[q], slot):
            cp.wait()
        wgu_b[:, 0:D_EXPERT] = wg_l[slot].astype(BF16)
        wgu_b[:, D_EXPERT:] = wu_l[slot].astype(BF16)
        wd_b[...] = wd_l[slot].astype(BF16)
        live[0] = q + 1

        @pl.when(q + 1 < live[1])
        def _():
            for cp in weight_copies(elist[q + 1], 1 - slot):
                cp.start()

    def gather(j, xb, gb, rows=(0, MT)):
        base = j * MT
        for m in range(*rows):
            t = tok_ref[base + m]
            xb[pl.ds(m, 4, stride=MT_STRIDE), :] = src_v[pl.ds(pl.multiple_of(t * 4, 4), 4), :]
            gb[m:m + 1, :] = gate_v[pl.ds(t, 1), :]

    def scatter(j, yb, rows=(0, MT)):
        base = j * MT
        for b in range(rows[0] // RMW_BATCH, rows[1] // RMW_BATCH):
            ms = [b * RMW_BATCH + u for u in range(RMW_BATCH)]
            targets = [pl.ds(pl.multiple_of(tok_ref[base + m] * 8, 8), 8) for m in ms]
            vals = [acc_v[r, :] + yb[pl.ds(m, 8, stride=MT_STRIDE), :] for r, m in zip(targets, ms)]
            for r, v in zip(targets, vals):
                acc_v[r, :] = v

    quarters = [(q * MT // 4, (q + 1) * MT // 4) for q in range(4)]

    def step(j, p):
        pl.when(tile_first[j] == 1)(switch_expert)
        xb, gb, yb = xbuf[p], gbuf[p], ybuf[p]
        nxt = (j + 1, xbuf[1 - p], gbuf[1 - p])
        prv = (jnp.maximum(j - 1, 0), ybuf[1 - p])
        lo, hi = [], []
        for c in range(4):
            words = xb[MT_STRIDE * c:MT_STRIDE * c + MT, :]
            unpack = functools.partial(pltpu.unpack_elementwise, words, packed_dtype=BF16, unpacked_dtype=F32)
            lo.append(unpack(index=0).astype(BF16))
            hi.append(unpack(index=1).astype(BF16))
        x = jnp.concatenate(lo + hi, axis=1)
        gather(*nxt, rows=quarters[0])
        h_gate = _dot(x, wgu_b[:, 0:D_EXPERT])
        gather(*nxt, rows=quarters[1])
        h_up = _dot(x, wgu_b[:, D_EXPERT:])
        gather(*nxt, rows=quarters[2])
        g = gb[...]
        g_hi = g.astype(BF16)
        g_lo = (g - g_hi.astype(F32)).astype(BF16)
        pick = (lax.broadcasted_iota(jnp.int32, (128, D_EXPERT), 0) == tile_e[j]).astype(BF16)
        g_col = _dot(g_hi, pick) + _dot(g_lo, pick)
        a = (_silu(h_gate) * h_up * g_col).astype(BF16)
        gather(*nxt, rows=quarters[3])
        scatter(*prv, rows=quarters[0])
        y_lo = _dot(a, wd_b[:, 0:D_MODEL // 2])
        scatter(*prv, rows=quarters[1])
        y_hi = _dot(a, wd_b[:, D_MODEL // 2:])
        scatter(*prv, rows=quarters[2])
        for c in range(4):
            yb[MT_STRIDE * c:MT_STRIDE * c + MT, :] = y_lo[:, 128 * c:128 * (c + 1)]
            yb[MT_STRIDE * (c + 4):MT_STRIDE * (c + 4) + MT, :] = y_hi[:, 128 * c:128 * (c + 1)]
        scatter(*prv, rows=quarters[3])

    gather(0, xbuf[0], gbuf[0])
    n_pairs = (n_tiles + 1) // 2

    def pair(i, carry):
        step(2 * i, 0)
        step(2 * i + 1, 1)
        return carry

    lax.fori_loop(0, n_pairs, pair, 0)
    scatter(jnp.maximum(2 * n_pairs - 1, 0), ybuf[1])

    out_cp = pltpu.make_async_copy(acc_v.at[pl.ds(0, 8 * T_HALF)],
                                   out_hbm.at[pl.ds(pl.multiple_of(h * (8 * T_HALF), 8), 8 * T_HALF)],
                                   sem.at[2])
    out_cp.start()
    out_cp.wait()


def _moe(slot_words, cnt, off, src, gate, wg, wu, wd):
    any_spec = pl.BlockSpec(memory_space=pl.ANY)
    tile_buf = lambda rows, dt: pltpu.VMEM((rows * MT_STRIDE, 128), dt)
    return pl.pallas_call(
        _moe_kernel,
        grid_spec=pltpu.PrefetchScalarGridSpec(
            num_scalar_prefetch=3,
            grid=(2,),
            in_specs=[any_spec] * 5,
            out_specs=any_spec,
            scratch_shapes=[pltpu.VMEM((4 * T_HALF + 8, 128), jnp.uint32),
                            pltpu.VMEM((T_HALF + 8, 128), F32),
                            pltpu.VMEM((8 * T_HALF + 8, 128), F32),
                            pltpu.VMEM((2, D_MODEL, D_EXPERT), F32),
                            pltpu.VMEM((2, D_MODEL, D_EXPERT), F32),
                            pltpu.VMEM((2, D_EXPERT, D_MODEL), F32),
                            pltpu.VMEM((D_MODEL, 2 * D_EXPERT), BF16),
                            pltpu.VMEM((D_EXPERT, D_MODEL), BF16),
                            tile_buf(4, jnp.uint32), tile_buf(4, jnp.uint32),
                            pltpu.VMEM((MT, 128), F32), pltpu.VMEM((MT, 128), F32),
                            tile_buf(8, F32), tile_buf(8, F32),
                            pltpu.SMEM((SLOT_CAP,), jnp.int32),
                            pltpu.SMEM((NT_MAX + 2,), jnp.int32),
                            pltpu.SMEM((NT_MAX + 2,), jnp.int32),
                            pltpu.SMEM((N_EXPERTS,), jnp.int32),
                            pltpu.SMEM((2,), jnp.int32),
                            pltpu.SemaphoreType.DMA((3,)),
                            pltpu.SemaphoreType.DMA((2, 3))]),
        out_shape=jax.ShapeDtypeStruct((8 * T_ALL, 128), F32),
        compiler_params=_cparams(("arbitrary",)),
        name="moe",
    )(slot_words, cnt, off, src, gate, wg, wu, wd)


def _finalize_kernel(acc_ref, base_ref, mod_ref, oc_ref, ol_ref):
    def write(o_ref):
        for c in range(8):
            cs = slice(128 * c, 128 * (c + 1))
            o_ref[:, cs] = base_ref[:, cs] + mod_ref[5:6, cs] * acc_ref[pl.ds(c, TBP, stride=8), :]

    is_ctx = pl.program_id(0) < NP_CTX
    pl.when(is_ctx)(lambda: write(oc_ref))
    pl.when(jnp.logical_not(is_ctx))(lambda: write(ol_ref))


def _finalize(acc, base, mod3):
    return pl.pallas_call(
        _finalize_kernel,
        grid=(T_ALL // TBP,),
        in_specs=[pl.BlockSpec((8 * TBP, 128), lambda i: (i, 0)),
                  pl.BlockSpec((TBP, D_MODEL), lambda i: (i, 0)),
                  pl.BlockSpec((None, 6, D_MODEL), lambda i: (_mod_row(i), 0, 0))],
        out_specs=_split_specs(D_MODEL),
        out_shape=[jax.ShapeDtypeStruct((T_CTX, D_MODEL), F32), jax.ShapeDtypeStruct((T_LAT, D_MODEL), F32)],
        compiler_params=_cparams(("arbitrary",)),
        name="finalize",
    )(acc, base, mod3)


def _head_indicators(width):
    head = jnp.arange(width) // DH_ATT
    ind = (head[:, None] == jnp.arange(128)[None, :]).astype(BF16)
    return ind, ind.T


def kernel(x_prompt, x_sample, cache_attn_k, cache_attn_v, state_mlstm_c, state_mlstm_n, state_mlstm_m, c, c_ctx, w_mod, b_mod, norm1_w, norm2_w, w_in, q_norm_w, k_norm_w, b_gates, m_norm_w, w_out, router_w, router_bias, w_gate, w_up, w_down, ws_gate, ws_up, ws_down):
    x = (x_prompt.reshape(T_CTX, D_MODEL), x_sample.reshape(T_LAT, D_MODEL))
    mod3 = _modulation(c, c_ctx, w_mod[0], b_mod[0])

    w_main = w_in[0, :, :P_MAIN].astype(BF16)
    w_gates_t = w_in[0, :, P_MAIN:].T.astype(BF16)
    qw = jnp.tile(q_norm_w, (1, N_HEADS_ATT))
    kw = jnp.tile(k_norm_w, (1, N_KV_HEADS))
    inds = _head_indicators(512) + _head_indicators(128)
    qn, kn, va, qm, km, vm, om, gt = _inproj(*x, mod3, norm1_w, w_main, w_gates_t, qw, kw, inds, _rope_tables())

    att_c = _attention(qn, kn, va, None, n_batch=BATCH, seq=SEQ, first_row=0)
    att_l = _attention(qn, kn, va, (cache_attn_k, cache_attn_v), n_batch=DEC_BATCH, seq=DEC_SEQ,
                       first_row=T_CTX)

    g3 = gt.reshape(16, T_ALL // CHUNK, CHUNK).transpose(1, 0, 2)
    bg = b_gates.reshape(16, 1)
    hm_c, c_new, n_new, m_new = _mlstm(qm, km, vm, g3, bg, om, m_norm_w, None,
                                       n_batch=BATCH, seq=SEQ, first_block=0)
    m0 = jnp.broadcast_to(state_mlstm_m.reshape(DEC_BATCH, 8, 1), (DEC_BATCH, 8, CHUNK))
    hm_l = _mlstm(qm, km, vm, g3, bg, om, m_norm_w, (state_mlstm_c, state_mlstm_n, m0),
                  n_batch=DEC_BATCH, seq=DEC_SEQ, first_block=N_CTX_BLOCKS)

    rw_t = router_w[0].T
    rw_hi = rw_t.astype(BF16)
    rw_lo = (rw_t - rw_hi.astype(F32)).astype(BF16)
    base, hn2, logits_t = _outproj((att_c, att_l), (hm_c, hm_l), x, mod3, norm2_w, w_out[0].astype(BF16),
                                   rw_hi, rw_lo, ws_gate[0].astype(BF16), ws_up[0].astype(BF16),
                                   ws_down[0].astype(BF16))
    gate, idx8 = _router(logits_t, router_bias.reshape(N_EXPERTS, 1))
    slot_words, cnt, off = _plan(idx8)
    acc = _moe(slot_words.reshape(TOP_K * T_HALF), cnt[:, :, 0].reshape(2 * N_EXPERTS),
               off[:, :, 0].reshape(2 * N_EXPERTS), hn2, gate, w_gate[0], w_up[0], w_down[0])
    out_c, out_l = _finalize(acc, base, mod3)

    y_prompt = out_c.reshape(BATCH, SEQ, D_MODEL)
    y_sample = out_l.reshape(DEC_BATCH, DEC_SEQ, D_MODEL)
    to_cache = lambda a: a[:T_CTX].reshape(BATCH, SEQ, N_KV_HEADS, DH_ATT).transpose(0, 2, 1, 3)[:, None]
    new_k = to_cache(kn)
    new_v = to_cache(va)
    new_m = m_new[:, :, 0].reshape(BATCH, 1, 2, N_HEADS_M)
    return (y_prompt, y_sample, new_k, new_v, c_new[:, None], n_new[:, None], new_m)
```

```python
import functools

import jax
import jax.numpy as jnp
from jax import lax
from jax.experimental import pallas as pl
from jax.experimental.pallas import tpu as pltpu

F32 = jnp.float32
BF16 = jnp.bfloat16

D_MODEL = 1024
BATCH = 32
SEQ = 256
DEC_BATCH = 2
DEC_SEQ = 1024
PAST_LEN = 256
GRID_W = 64
N_HEADS_ATT = 8
N_KV_HEADS = 2
DH_ATT = 64
D_ATT = 512
ROPE_THETA = 10000.0
N_HEADS_M = 4
DH_M = 128
D_M = 512
CHUNK = 128
N_EXPERTS = 64
TOP_K = 8
N_GROUPS = 8
TOPK_GROUPS = 4
D_EXPERT = 256
ROUTED_SCALE = 2.5
EPS = 1e-6
NEG_INIT = -1e30

T_CTX = BATCH * SEQ
T_LAT = DEC_BATCH * DEC_SEQ
T_ALL = T_CTX + T_LAT
TB = 256
N_CTX_BLOCKS = T_CTX // TB
TBP = 512
NP_CTX = T_CTX // TBP
NP_LAT_PER_BATCH = DEC_SEQ // TBP
TB_MOE = 1024
T_HALF = T_ALL // 2
MT = 256
MT_STRIDE = MT + 8
NT_MAX = TOP_K * T_HALF // MT + N_EXPERTS
SLOT_CAP = (NT_MAX + 2) * MT
RMW_BATCH = 16
INV_UNROLL = 16
W_SLOTS = 3
assert SLOT_CAP < 2 ** 16 and (T_HALF // 2) % INV_UNROLL == 0 and MT % RMW_BATCH == 0
P_MAIN = 2816
VMEM_LIMIT = 56 * 1024 * 1024

_NT = (((1,), (1,)), ((), ()))
_TN = (((0,), (0,)), ((), ()))


def _cparams(sem):
    return pltpu.CompilerParams(dimension_semantics=sem, vmem_limit_bytes=VMEM_LIMIT)


def _split3(x):
    hi = x.astype(BF16)
    r1 = x - hi.astype(F32)
    mid = r1.astype(BF16)
    lo = (r1 - mid.astype(F32)).astype(BF16)
    return hi, mid, lo


def _dot(a, b):
    return jnp.dot(a, b, preferred_element_type=F32)


def _dot3(x, m_bf16):
    hi, mid, lo = _split3(x)
    return _dot(hi, m_bf16) + _dot(mid, m_bf16) + _dot(lo, m_bf16)


def _dot2(x, m_bf16):
    hi = x.astype(BF16)
    lo = (x - hi.astype(F32)).astype(BF16)
    return _dot(hi, m_bf16) + _dot(lo, m_bf16)


def _silu(x):
    return x * jax.nn.sigmoid(x)


def _mod_row(i):
    return jnp.where(i < NP_CTX, 0, 1 + (i - NP_CTX) // NP_LAT_PER_BATCH)


def _mod_kernel(ct_ref, w_ref, b_ref, o_ref):
    s = _silu(ct_ref[...])
    w = w_ref[...]
    rows = [jnp.sum(w * s[:, r:r + 1], axis=0, keepdims=True) for r in range(3)]
    rows.append(jnp.zeros((5, w.shape[1]), F32))
    o_ref[...] = jnp.concatenate(rows, axis=0) + b_ref[...]


def _modulation(c, c_ctx, w_mod, b_mod):
    cvec = jnp.concatenate([c_ctx[None, :], c, jnp.zeros((5, D_MODEL), F32)], axis=0)
    nb = 1024
    out = pl.pallas_call(
        _mod_kernel,
        grid=(6 * D_MODEL // nb,),
        in_specs=[pl.BlockSpec((D_MODEL, 8), lambda j: (0, 0)),
                  pl.BlockSpec((D_MODEL, nb), lambda j: (0, j)),
                  pl.BlockSpec((1, nb), lambda j: (0, j))],
        out_specs=pl.BlockSpec((8, nb), lambda j: (0, j)),
        out_shape=jax.ShapeDtypeStruct((8, 6 * D_MODEL), F32),
        compiler_params=_cparams(("arbitrary",)),
        name="modulation",
    )(cvec.T, w_mod, b_mod[None, :])
    return out.reshape(8, 6, D_MODEL)


def _ctx_or_lat(ctx_ref, lat_ref):
    return jnp.where(pl.program_id(0) < NP_CTX, ctx_ref[...], lat_ref[...])


def _split_specs(width):
    return [pl.BlockSpec((TBP, width), lambda i: (jnp.minimum(i, NP_CTX - 1), 0)),
            pl.BlockSpec((TBP, width), lambda i: (jnp.maximum(i - NP_CTX, 0), 0))]


def _head_norm(x, ind, ind_t, w_row):
    ss = _dot2(x * x, ind)
    inv = lax.rsqrt(ss * (1.0 / DH_ATT) + EPS)
    return x * _dot2(inv, ind_t) * w_row


def _rope(x, cos, sin_signed):
    lane = lax.broadcasted_iota(jnp.int32, x.shape, 1)
    partner = jnp.where((lane % 32) < 16, pltpu.roll(x, 128 - 16, 1), pltpu.roll(x, 16, 1))
    return x * cos + partner * sin_signed


def _inproj_kernel(xc_ref, xl_ref, mod_ref, nw_ref, w_ref, wgt_ref, qw_ref, kw_ref, iq_ref, iqt_ref,
                   ik_ref, ikt_ref, cos_ref, sin_ref,
                   qn_ref, kn_ref, va_ref, qm_ref, km_ref, vm_ref, om_ref, gt_ref):
    x = _ctx_or_lat(xc_ref, xl_ref)
    y = x * lax.rsqrt(jnp.mean(x * x, axis=-1, keepdims=True) + EPS) * nw_ref[...]
    hn = y * (1.0 + mod_ref[1:2, :]) + mod_ref[0:1, :]
    hb = hn.astype(BF16)
    qn = _head_norm(_dot(hb, w_ref[:, 0:512]), iq_ref[...], iqt_ref[...], qw_ref[...])
    kn = _head_norm(_dot(hb, w_ref[:, 512:640]), ik_ref[...], ikt_ref[...], kw_ref[...])
    qn_ref[...] = qn.astype(BF16)
    kn_ref[...] = kn

    @pl.when(pl.program_id(0) >= NP_CTX)
    def _():
        cos, sin = cos_ref[...], sin_ref[...]
        for j in range(4):
            qn_ref[:, 128 * j:128 * (j + 1)] = _rope(qn[:, 128 * j:128 * (j + 1)], cos, sin).astype(BF16)
        kn_ref[...] = _rope(kn, cos, sin)

    va_ref[...] = _dot(hb, w_ref[:, 640:768])
    qm_ref[...] = _dot(hb, w_ref[:, 768:1280]).astype(BF16)
    km_ref[...] = (_dot(hb, w_ref[:, 1280:1792]) * (DH_M ** -0.5)).astype(BF16)
    vm_ref[...] = _dot(hb, w_ref[:, 1792:2304]).astype(BF16)
    om_ref[...] = _dot(hb, w_ref[:, 2304:2816])
    gt_ref[...] = lax.dot_general(wgt_ref[...], hb, _NT, preferred_element_type=F32)


def _inproj(x_ctx, x_lat, mod3, norm1_w, w_main, w_gates_t, qw, kw, inds, rope_tabs):
    tok = lambda w: pl.BlockSpec((TBP, w), lambda i: (i, 0))
    full = lambda a: pl.BlockSpec(a.shape, lambda i: (0,) * a.ndim)
    sd = lambda w, dt: jax.ShapeDtypeStruct((T_ALL, w), dt)
    rope_spec = pl.BlockSpec((TBP, 128), lambda i: (jnp.maximum(i - NP_CTX, 0) % NP_LAT_PER_BATCH, 0))
    return pl.pallas_call(
        _inproj_kernel,
        grid=(T_ALL // TBP,),
        in_specs=_split_specs(D_MODEL) + [
                  pl.BlockSpec((None, 6, D_MODEL), lambda i: (_mod_row(i), 0, 0)),
                  full(norm1_w), full(w_main), full(w_gates_t), full(qw), full(kw)]
                 + [full(a) for a in inds] + [rope_spec, rope_spec],
        out_specs=[tok(512), tok(128), tok(128), tok(512), tok(512), tok(512), tok(512),
                   pl.BlockSpec((16, TBP), lambda i: (0, i))],
        out_shape=[sd(512, BF16), sd(128, F32), sd(128, F32), sd(512, BF16), sd(512, BF16),
                   sd(512, BF16), sd(512, F32), jax.ShapeDtypeStruct((16, T_ALL), F32)],
        compiler_params=_cparams(("arbitrary",)),
        name="inproj",
    )(x_ctx, x_lat, mod3, norm1_w, w_main, w_gates_t, qw, kw, *inds, *rope_tabs)


def _rope_tables():
    t = jnp.arange(DEC_SEQ)
    pos = jnp.stack([t // GRID_W, t % GRID_W], axis=1).astype(F32)
    n_freq = DH_ATT // 4
    inv_freq = ROPE_THETA ** (-jnp.arange(n_freq, dtype=F32) / n_freq)
    ang = pos[:, :, None] * inv_freq
    cos, sin = jnp.cos(ang), jnp.sin(ang)
    cos_h = jnp.stack([cos, cos], axis=2).reshape(DEC_SEQ, DH_ATT)
    sin_h = jnp.stack([-sin, sin], axis=2).reshape(DEC_SEQ, DH_ATT)
    return jnp.tile(cos_h, (1, 2)), jnp.tile(sin_h, (1, 2))


def _attn_kernel(*refs, has_cache):
    if has_cache:
        q_ref, k_ref, v_ref, kc_ref, vc_ref, o_ref = refs
    else:
        q_ref, k_ref, v_ref, o_ref = refs
    q = q_ref[...]
    k = k_ref[...].astype(BF16)
    v = v_ref[...].astype(BF16)
    qb = q.shape[0]
    scale = DH_ATT ** -0.5
    outs = []
    for g in range(N_KV_HEADS):
        kg = k[:, DH_ATT * g:DH_ATT * (g + 1)]
        vg = v[:, DH_ATT * g:DH_ATT * (g + 1)]
        qg = jnp.concatenate([q[:, 256 * g + DH_ATT * j:256 * g + DH_ATT * (j + 1)] for j in range(4)], axis=0)
        s = lax.dot_general(qg, kg, _NT, preferred_element_type=F32) * scale
        m = jnp.max(s, axis=-1, keepdims=True)
        if has_cache:
            kc = kc_ref[g].astype(BF16)
            vc = vc_ref[g].astype(BF16)
            sc = lax.dot_general(qg, kc, _NT, preferred_element_type=F32) * scale
            m = jnp.maximum(m, jnp.max(sc, axis=-1, keepdims=True))
        p = jnp.exp(s - m)
        den = jnp.sum(p, axis=-1, keepdims=True)
        o = _dot(p.astype(BF16), vg)
        if has_cache:
            pc = jnp.exp(sc - m)
            den = den + jnp.sum(pc, axis=-1, keepdims=True)
            o = o + _dot(pc.astype(BF16), vc)
        o = o / den
        outs += [o[qb * j:qb * (j + 1), :] for j in range(4)]
    o_ref[...] = jnp.concatenate(outs, axis=1).astype(BF16)


def _attention(qn, kn, v_all, cache, *, n_batch, seq, first_row):
    qblocks = seq // TB
    kv_spec = pl.BlockSpec((seq, 128), lambda b, i: (b + first_row // seq, 0))
    in_specs = [pl.BlockSpec((TB, 512), lambda b, i: (first_row // TB + b * qblocks + i, 0)), kv_spec, kv_spec]
    args = [qn, kn, v_all]
    if cache is not None:
        cspec = pl.BlockSpec((None, None, N_KV_HEADS, PAST_LEN, DH_ATT), lambda b, i: (b, 0, 0, 0, 0))
        in_specs += [cspec, cspec]
        args += list(cache)
    return pl.pallas_call(
        functools.partial(_attn_kernel, has_cache=cache is not None),
        grid=(n_batch, qblocks),
        in_specs=in_specs,
        out_specs=pl.BlockSpec((TB, 512), lambda b, i: (b * qblocks + i, 0)),
        out_shape=jax.ShapeDtypeStruct((n_batch * seq, 512), BF16),
        compiler_params=_cparams(("arbitrary", "arbitrary")),
        name="attention_lat" if cache is not None else "attention_ctx",
    )(*args)


def _log_sigmoid(x):
    return jnp.minimum(x, 0.0) - jnp.log1p(jnp.exp(-jnp.abs(x)))


def _col_bcast(cols, j):
    return jnp.broadcast_to(cols[:, j:j + 1], (CHUNK, CHUNK))


def _mlstm_kernel(*refs, has_state, n_chunks):
    n_in = 10 if has_state else 7
    n_out = 1 if has_state else 4
    q_ref, k_ref, v_ref, g_ref, bg_ref, om_ref, nw_ref = refs[:7]
    hm_ref = refs[n_in]
    scratch = refs[n_in + n_out:]
    st = scratch[0:8]
    ms = scratch[8:16]
    hdir = scratch[16:18]
    rows_sc, cmb_sc, bb_sc, kt_sc = scratch[18:22]
    neg_inf = F32(-jnp.inf)
    zeros112 = jnp.zeros((CHUNK - 16, CHUNK), F32)
    sub = lax.broadcasted_iota(jnp.int32, (CHUNK, CHUNK), 0)
    lan = lax.broadcasted_iota(jnp.int32, (CHUNK, CHUNK), 1)
    ones = jnp.ones((CHUNK, CHUNK), BF16)

    stats = []
    for c in range(n_chunks):
        pre = g_ref[c] + bg_ref[...]
        logf = _log_sigmoid(pre)
        for d in range(2):
            bcum = _dot3(logf, ((sub <= lan) if d == 0 else (sub >= lan)).astype(BF16))
            li4 = pre[4 * d:4 * d + 4, :]
            lf4 = logf[8 + 4 * d:12 + 4 * d, :]
            b4 = bcum[8 + 4 * d:12 + 4 * d, :]
            stats.append((c, d, li4 - b4, lf4, b4))
    for c, d, r4, lf4, b4 in stats:
        blast4 = b4[:, CHUNK - 1:CHUNK] if d == 0 else b4[:, 0:1]
        wlog4 = blast4 + r4
        wmax4 = jnp.max(wlog4, axis=-1, keepdims=True)
        full = lambda a: jnp.broadcast_to(a, (4, CHUNK))
        rows_sc[2 * c + d] = jnp.concatenate(
            [r4, wlog4, full(blast4), full(wmax4), jnp.zeros((16, CHUNK), F32)], axis=0)
    for c in range(n_chunks):
        for h in range(N_HEADS_M):
            kc = k_ref[CHUNK * c:CHUNK * (c + 1), DH_M * h:DH_M * (h + 1)]
            kt_sc[c * N_HEADS_M + h] = kc.astype(F32).T.astype(BF16)
    for c, d, r4, lf4, b4 in stats:
        within = (lan <= sub) if d == 0 else (lan >= sub)
        for h in range(N_HEADS_M):
            cummax = jnp.max(jnp.where(within, r4[h:h + 1, :], neg_inf), axis=-1, keepdims=True)
            cmb_sc[(2 * c + d) * N_HEADS_M + h] = jnp.broadcast_to(cummax, (CHUNK, CHUNK))
    for c, d, r4, lf4, b4 in stats:
        within = (lan <= sub) if d == 0 else (lan >= sub)
        for h in range(N_HEADS_M):
            bb_sc[(2 * c + d) * N_HEADS_M + h] = _dot2(jnp.where(within, lf4[h:h + 1, :], 0.0), ones)

    for d in range(2):
        if has_state:
            c0_ref, n0_ref, m0_ref = refs[7:10]
            ncols = jnp.concatenate([n0_ref[d], jnp.zeros((12, DH_M), F32), zeros112], axis=0).T
        for h in range(N_HEADS_M):
            idx = 4 * d + h
            if has_state:
                st[idx][:, 0:DH_M] = c0_ref[d, h]
                st[idx][:, DH_M:] = _col_bcast(ncols, h)
                ms[idx][0:1, :] = m0_ref[idx:idx + 1, :]
            else:
                st[idx][...] = jnp.zeros((DH_M, 2 * DH_M), F32)
                ms[idx][0:1, :] = jnp.full((1, CHUNK), NEG_INIT, F32)

    def chunk_step(i, carry):
        for d in range(2):
            c = i if d == 0 else n_chunks - 1 - i
            r0 = pl.multiple_of(c * CHUNK, CHUNK)
            rows = rows_sc[2 * c + d]
            mask = (lan <= sub) if d == 0 else (lan >= sub)
            heads = []
            for h in range(N_HEADS_M):
                idx = 4 * d + h
                hs = slice(DH_M * h, DH_M * (h + 1))
                qc = q_ref[pl.ds(r0, CHUNK), hs]
                kt = kt_sc[c * N_HEADS_M + h]
                state = st[idx][...]
                heads.append(dict(idx=idx, hs=hs, qc=qc, kt=kt, state=state,
                                  s_raw=_dot(qc, kt), qs=_dot(qc, state.astype(BF16))))
            for h, hd in enumerate(heads):
                mp = ms[hd["idx"]][0:1, :]
                m_b = jnp.maximum(mp, cmb_sc[(2 * c + d) * N_HEADS_M + h])
                s = hd["s_raw"] * jnp.exp(jnp.where(mask, rows[h:h + 1, :] - m_b, neg_inf))
                s_hi = s.astype(BF16)
                vc = v_ref[pl.ds(r0, CHUNK), hd["hs"]]
                v_ones = jnp.concatenate([vc, ones], axis=1)
                hd.update(mp=mp, m_b=m_b, v_ones=v_ones, sv=_dot(s_hi, v_ones),
                          s_lo_sum=_dot((s - s_hi.astype(F32)).astype(BF16), ones))
            for h, hd in enumerate(heads):
                mp, m_b = hd["mp"], hd["m_b"]
                gw_b = jnp.exp(mp - m_b)
                en_b = jnp.exp(-(bb_sc[(2 * c + d) * N_HEADS_M + h] + m_b))
                num = hd["sv"][:, 0:DH_M] + gw_b * hd["qs"][:, 0:DH_M]
                den = hd["sv"][:, DH_M:] + hd["s_lo_sum"] + gw_b * hd["qs"][:, DH_M:]
                hdir[d][pl.ds(r0, CHUNK), hd["hs"]] = num / jnp.maximum(jnp.abs(den), en_b)
            for h, hd in enumerate(heads):
                mp = hd["mp"]
                gend = rows[8 + h:9 + h, :] + mp
                mnew = jnp.maximum(gend, rows[12 + h:13 + h, :])
                w_row = jnp.exp(rows[4 + h:5 + h, :] - mnew)
                kw_t = (hd["kt"].astype(F32) * w_row).astype(BF16)
                dec = jnp.exp(gend - mnew)
                st[hd["idx"]][...] = (jnp.concatenate([dec, dec], axis=1) * hd["state"]
                                      + _dot(kw_t, hd["v_ones"]))
                ms[hd["idx"]][0:1, :] = mnew
        return carry

    lax.fori_loop(0, n_chunks, chunk_step, 0)

    for h in range(N_HEADS_M):
        hs = slice(DH_M * h, DH_M * (h + 1))
        hh = hdir[0][:, hs] + hdir[1][:, hs]
        y = hh * lax.rsqrt(jnp.mean(hh * hh, axis=-1, keepdims=True) + EPS) * nw_ref[:, hs]
        hm_ref[:, hs] = (jax.nn.sigmoid(om_ref[:, hs]) * y).astype(BF16)
    if not has_state:
        c_ref, n_ref, m_ref = refs[n_in + 1:n_in + 4]
        for d in range(2):
            for h in range(N_HEADS_M):
                idx = 4 * d + h
                c_ref[d, h] = st[idx][:, 0:DH_M]
                n_ref[d, h:h + 1, :] = st[idx][:, DH_M:].T[0:1, :]
                m_ref[idx:idx + 1, :] = ms[idx][0:1, :]


def _mlstm(qm, km, vm, g3, bg, om, nw, state, *, n_batch, seq, first_block):
    n_chunks = seq // CHUNK
    rb = seq // TB
    tok = lambda: pl.BlockSpec((seq, 512), lambda b: (b + first_block // rb, 0))
    full = lambda a: pl.BlockSpec(a.shape, lambda b: (0,) * a.ndim)
    in_specs = [tok(), tok(), tok(),
                pl.BlockSpec((n_chunks, 16, CHUNK), lambda b: (b + first_block // rb, 0, 0)),
                full(bg), tok(), full(nw)]
    args = [qm, km, vm, g3, bg, om, nw]
    hm_spec = pl.BlockSpec((seq, 512), lambda b: (b, 0))
    hm_shape = jax.ShapeDtypeStruct((n_batch * seq, 512), BF16)
    scratch = ([pltpu.VMEM((DH_M, 2 * DH_M), F32)] * 8 + [pltpu.VMEM((8, CHUNK), F32)] * 8
               + [pltpu.VMEM((seq, 512), F32)] * 2
               + [pltpu.VMEM((2 * n_chunks, 32, CHUNK), F32),
                  pltpu.VMEM((8 * n_chunks, CHUNK, CHUNK), F32),
                  pltpu.VMEM((8 * n_chunks, CHUNK, CHUNK), F32),
                  pltpu.VMEM((N_HEADS_M * n_chunks, DH_M, CHUNK), BF16)])
    if state is not None:
        c0, n0, m0 = state
        in_specs += [pl.BlockSpec((None, None, 2, N_HEADS_M, DH_M, DH_M), lambda b: (b, 0, 0, 0, 0, 0)),
                     pl.BlockSpec((None, None, 2, N_HEADS_M, DH_M), lambda b: (b, 0, 0, 0, 0)),
                     pl.BlockSpec((None, 8, CHUNK), lambda b: (b, 0, 0))]
        args += [c0, n0, m0]
        out_specs, out_shape = hm_spec, hm_shape
    else:
        out_specs = [hm_spec,
                     pl.BlockSpec((None, 2, N_HEADS_M, DH_M, DH_M), lambda b: (b, 0, 0, 0, 0)),
                     pl.BlockSpec((None, 2, N_HEADS_M, DH_M), lambda b: (b, 0, 0, 0)),
                     pl.BlockSpec((None, 8, CHUNK), lambda b: (b, 0, 0))]
        out_shape = [hm_shape,
                     jax.ShapeDtypeStruct((n_batch, 2, N_HEADS_M, DH_M, DH_M), F32),
                     jax.ShapeDtypeStruct((n_batch, 2, N_HEADS_M, DH_M), F32),
                     jax.ShapeDtypeStruct((n_batch, 8, CHUNK), F32)]
    return pl.pallas_call(
        functools.partial(_mlstm_kernel, has_state=state is not None, n_chunks=n_chunks),
        grid=(n_batch,),
        in_specs=in_specs,
        out_specs=out_specs,
        out_shape=out_shape,
        scratch_shapes=scratch,
        compiler_params=_cparams(("arbitrary",)),
        name="mlstm_lat" if state is not None else "mlstm_ctx",
    )(*args)


def _outproj_kernel(attc_ref, attl_ref, hmc_ref, hml_ref, xc_ref, xl_ref, mod_ref, nw_ref, wo_ref,
                    rwh_ref, rwl_ref, wsg_ref, wsu_ref, wsd_ref, base_ref, hn_ref, lt_ref):
    y = (_dot(_ctx_or_lat(attc_ref, attl_ref), wo_ref[0:D_ATT, :])
         + _dot(_ctx_or_lat(hmc_ref, hml_ref), wo_ref[D_ATT:, :]))
    x1 = _ctx_or_lat(xc_ref, xl_ref) + mod_ref[2:3, :] * y
    z = x1 * lax.rsqrt(jnp.mean(x1 * x1, axis=-1, keepdims=True) + EPS) * nw_ref[...]
    hn = z * (1.0 + mod_ref[4:5, :]) + mod_ref[3:4, :]
    hb = hn.astype(BF16)
    hl = (hn - hb.astype(F32)).astype(BF16)
    nt = lambda w, t: lax.dot_general(w, t, _NT, preferred_element_type=F32)
    lt_ref[...] = nt(rwh_ref[...], hb) + nt(rwl_ref[...], hb) + nt(rwh_ref[...], hl)
    a = _silu(_dot(hb, wsg_ref[...])) * _dot(hb, wsu_ref[...])
    shared = _dot(a.astype(BF16), wsd_ref[...])
    base_ref[...] = x1 + mod_ref[5:6, :] * shared
    packed = pltpu.pack_elementwise([hn[:, :512], hn[:, 512:]], packed_dtype=BF16)
    for c in range(4):
        hn_ref[pl.ds(c, TBP, stride=4), :] = packed[:, 128 * c:128 * (c + 1)]


def _outproj(att, hm, x, mod3, norm2_w, w_out, rw_hi, rw_lo, wsg, wsu, wsd):
    tok = lambda w: pl.BlockSpec((TBP, w), lambda i: (i, 0))
    full = lambda a: pl.BlockSpec(a.shape, lambda i: (0,) * a.ndim)
    return pl.pallas_call(
        _outproj_kernel,
        grid=(T_ALL // TBP,),
        in_specs=_split_specs(512) + _split_specs(512) + _split_specs(D_MODEL) + [
                  pl.BlockSpec((None, 6, D_MODEL), lambda i: (_mod_row(i), 0, 0)),
                  full(norm2_w), full(w_out), full(rw_hi), full(rw_lo), full(wsg), full(wsu), full(wsd)],
        out_specs=[tok(D_MODEL), pl.BlockSpec((4 * TBP, 128), lambda i: (i, 0)),
                   pl.BlockSpec((N_EXPERTS, TBP), lambda i: (0, i))],
        out_shape=[jax.ShapeDtypeStruct((T_ALL, D_MODEL), F32),
                   jax.ShapeDtypeStruct((4 * T_ALL, 128), jnp.uint32),
                   jax.ShapeDtypeStruct((N_EXPERTS, T_ALL), F32)],
        compiler_params=_cparams(("arbitrary",)),
        name="outproj",
    )(*att, *hm, *x, mod3, norm2_w, w_out, rw_hi, rw_lo, wsg, wsu, wsd)


def _first_max(vals, ids, limit):
    m = functools.reduce(jnp.maximum, [jnp.max(v, axis=0, keepdims=True) for v in vals])
    cand = [jnp.min(jnp.where(v == m, i, limit), axis=0, keepdims=True) for v, i in zip(vals, ids)]
    return m, functools.reduce(jnp.minimum, cand)


def _router_kernel(lt_ref, bias_ref, gate_ref, idx_ref):
    n = lt_ref.shape[1]
    score = jax.nn.sigmoid(lt_ref[...])
    biased = score + bias_ref[...]
    sub = lax.broadcasted_iota(jnp.int32, (8, n), 0).astype(F32)
    neg_inf = F32(-jnp.inf)
    slabs = [biased[8 * g:8 * (g + 1), :] for g in range(N_GROUPS)]
    gs = []
    for sl in slabs:
        m1, i1 = _first_max([sl], [sub], 8.0)
        m2 = jnp.max(jnp.where(sub == i1, neg_inf, sl), axis=0, keepdims=True)
        gs.append(m1 + m2)
    cur = jnp.concatenate(gs, axis=0)
    gsel = jnp.zeros((8, n), F32)
    for _ in range(TOPK_GROUPS):
        _, i = _first_max([cur], [sub], 8.0)
        hit = sub == i
        gsel = jnp.where(hit, 1.0, gsel)
        cur = jnp.where(hit, neg_inf, cur)
    vals = [jnp.where(gsel[g:g + 1, :] > 0.0, slabs[g], neg_inf) for g in range(N_GROUPS)]
    ids = [sub + 8.0 * g for g in range(N_GROUPS)]
    picked = [jnp.zeros((8, n), F32) for _ in range(N_GROUPS)]
    order = []
    for _ in range(TOP_K):
        _, i = _first_max(vals, ids, float(N_EXPERTS))
        order.append(i)
        hits = [idg == i for idg in ids]
        picked = [jnp.where(hh, score[8 * g:8 * (g + 1), :], p) for g, (p, hh) in enumerate(zip(picked, hits))]
        vals = [jnp.where(hh, neg_inf, v) for v, hh in zip(vals, hits)]
    total = functools.reduce(jnp.add, [jnp.sum(p, axis=0, keepdims=True) for p in picked])
    gate_t = jnp.concatenate([p / total * ROUTED_SCALE for p in picked]
                             + [jnp.zeros((128 - N_EXPERTS, n), F32)], axis=0)
    gate_ref[...] = gate_t.T
    idx_ref[...] = jnp.concatenate(order, axis=0).astype(jnp.int32)


def _router(logits_t, bias_col):
    return pl.pallas_call(
        _router_kernel,
        grid=(T_ALL // TB_MOE,),
        in_specs=[pl.BlockSpec((N_EXPERTS, TB_MOE), lambda i: (0, i)),
                  pl.BlockSpec((N_EXPERTS, 1), lambda i: (0, 0))],
        out_specs=[pl.BlockSpec((TB_MOE, 128), lambda i: (i, 0)),
                   pl.BlockSpec((TOP_K, TB_MOE), lambda i: (0, i))],
        out_shape=[jax.ShapeDtypeStruct((T_ALL, 128), F32),
                   jax.ShapeDtypeStruct((TOP_K, T_ALL), jnp.int32)],
        compiler_params=_cparams(("arbitrary",)),
        name="router",
    )(logits_t, bias_col)


def _plan_kernel(idx_ref, slot_ref, cnt_ref, off_ref, pos_sc):
    n_tiles = T_HALF // 128
    eid = lax.broadcasted_iota(jnp.int32, (N_EXPERTS, 128), 0)
    tri = (lax.broadcasted_iota(jnp.int32, (128, 128), 0)
           <= lax.broadcasted_iota(jnp.int32, (128, 128), 1)).astype(BF16)
    carry = jnp.zeros((N_EXPERTS, 1), F32)
    for j in range(n_tiles):
        it = idx_ref[:, 128 * j:128 * (j + 1)]
        sel = jnp.zeros((N_EXPERTS, 128), F32)
        for k in range(TOP_K):
            sel = jnp.where(it[k:k + 1, :] == eid, 1.0, sel)
        inc = _dot(sel.astype(BF16), tri) + carry
        carry = inc[:, 127:128]
        pos_sc[:, 128 * j:128 * (j + 1)] = inc - 1.0
    count = jnp.broadcast_to(carry, (N_EXPERTS, 128))
    padded = jnp.floor((count + (MT - 1.0)) * (1.0 / MT)) * MT
    before = (lax.broadcasted_iota(jnp.int32, (N_EXPERTS, N_EXPERTS), 1)
              < lax.broadcasted_iota(jnp.int32, (N_EXPERTS, N_EXPERTS), 0)).astype(BF16)
    hi, mid, lo = _split3(padded)
    off = _dot(before, hi) + _dot(before, mid) + _dot(before, lo)
    cnt_ref[...] = count.astype(jnp.int32)
    off_ref[...] = off.astype(jnp.int32)

    def slots(j):
        it = idx_ref[:, 128 * j:128 * (j + 1)]
        val = off + pos_sc[:, 128 * j:128 * (j + 1)]
        rows = [jnp.sum(jnp.where(it[k:k + 1, :] == eid, val, 0.0), axis=0, keepdims=True)
                for k in range(TOP_K)]
        return jnp.concatenate(rows, axis=0).astype(jnp.int32)

    for j in range(n_tiles // 2):
        slot_ref[:, 128 * j:128 * (j + 1)] = slots(j) | (slots(j + n_tiles // 2) << 16)


def _plan(idx8):
    return pl.pallas_call(
        _plan_kernel,
        grid=(2,),
        in_specs=[pl.BlockSpec((TOP_K, T_HALF), lambda h: (0, h))],
        out_specs=[pl.BlockSpec((None, TOP_K, T_HALF // 2), lambda h: (h, 0, 0)),
                   pl.BlockSpec((None, N_EXPERTS, 128), lambda h: (h, 0, 0)),
                   pl.BlockSpec((None, N_EXPERTS, 128), lambda h: (h, 0, 0))],
        out_shape=[jax.ShapeDtypeStruct((2, TOP_K, T_HALF // 2), jnp.int32),
                   jax.ShapeDtypeStruct((2, N_EXPERTS, 128), jnp.int32),
                   jax.ShapeDtypeStruct((2, N_EXPERTS, 128), jnp.int32)],
        scratch_shapes=[pltpu.VMEM((N_EXPERTS, T_HALF), F32)],
        compiler_params=_cparams(("arbitrary",)),
        name="plan",
    )(idx8)


def _prepare_half(slot_ref, cnt_ref, off_ref, tok_ref, tile_e, tile_first, elist, h):
    def per_expert(x, carry):
        j, q = carry
        n = cnt_ref[h * N_EXPERTS + x]
        first = off_ref[h * N_EXPERTS + x]
        tiles = (n + MT - 1) // MT
        elist[q] = x

        def mark(i, c):
            tile_e[j + i] = x
            tile_first[j + i] = jnp.where(i == 0, 1, 0)
            return c

        lax.fori_loop(0, tiles, mark, 0)

        def pad(p, c):
            tok_ref[first + p] = T_HALF
            return c

        lax.fori_loop(n, tiles * MT, pad, 0)
        return j + tiles, q + jnp.where(tiles > 0, 1, 0)

    n_tiles, n_live = lax.fori_loop(0, N_EXPERTS, per_expert, (0, 0))
    for extra in range(2):
        tile_e[n_tiles + extra] = 0
        tile_first[n_tiles + extra] = 0

    def pad_tail(p, c):
        tok_ref[n_tiles * MT + p] = T_HALF
        return c

    lax.fori_loop(0, 2 * MT, pad_tail, 0)
    return n_tiles, n_live


def _invert_slots(slot_ref, tok_ref, h):
    half = T_HALF // 2
    for k in range(TOP_K):
        def body(i, carry, k=k):
            words = [slot_ref[(h * TOP_K + k) * half + i * INV_UNROLL + u] for u in range(INV_UNROLL)]
            for u in range(INV_UNROLL):
                t = i * INV_UNROLL + u
                tok_ref[words[u] & 0xFFFF] = t
                tok_ref[lax.shift_right_logical(words[u], 16)] = t + half
            return carry

        lax.fori_loop(0, half // INV_UNROLL, body, 0)


def _moe_kernel(slot_ref, cnt_ref, off_ref, src_hbm, gate_hbm, wg_hbm, wu_hbm, wd_hbm, out_hbm,
                src_v, gate_v, acc_v, wg_l, wu_l, wd_l, wgu_b, wd_b, xbuf0, xbuf1, gbuf0, gbuf1,
                ybuf0, ybuf1, tok_ref, tile_e, tile_first, elist, live, sem, wsem):
    h = pl.program_id(0)
    xbuf, gbuf, ybuf = (xbuf0, xbuf1), (gbuf0, gbuf1), (ybuf0, ybuf1)

    src_cp = pltpu.make_async_copy(src_hbm.at[pl.ds(pl.multiple_of(h * (4 * T_HALF), 8), 4 * T_HALF)],
                                   src_v.at[pl.ds(0, 4 * T_HALF)], sem.at[0])
    gate_cp = pltpu.make_async_copy(gate_hbm.at[pl.ds(pl.multiple_of(h * T_HALF, 8), T_HALF)],
                                    gate_v.at[pl.ds(0, T_HALF)], sem.at[1])
    src_cp.start()
    gate_cp.start()

    def weight_copies(x, slot):
        return [pltpu.make_async_copy(w_hbm.at[x], w_l.at[slot], wsem.at[slot, i])
                for i, (w_hbm, w_l) in enumerate(((wg_hbm, wg_l), (wu_hbm, wu_l), (wd_hbm, wd_l)))]

    n_tiles, n_live = _prepare_half(slot_ref, cnt_ref, off_ref, tok_ref, tile_e, tile_first, elist, h)
    live[0] = 0
    live[1] = n_live

    for ahead in range(W_SLOTS - 1):
        @pl.when(ahead < n_live)
        def _(ahead=ahead):
            for cp in weight_copies(elist[ahead], ahead):
                cp.start()

    acc_v[...] = jnp.zeros(acc_v.shape, F32)
    zero = jnp.zeros((8, 128), F32)
    src_v[pl.ds(4 * T_HALF, 8), :] = pltpu.pack_elementwise([zero, zero], packed_dtype=BF16)
    gate_v[pl.ds(T_HALF, 8), :] = zero
    ybuf0[...] = jnp.zeros(ybuf0.shape, F32)
    ybuf1[...] = jnp.zeros(ybuf1.shape, F32)
    _invert_slots(slot_ref, tok_ref, h)
    src_cp.wait()
    gate_cp.wait()

    def switch_expert():
        q = live[0]
        slot = q % W_SLOTS
        for cp in weight_copies(elist[q], slot):
            cp.wait()
        wgu_b[:, 0:D_EXPERT] = wg_l[slot].astype(BF16)
        wgu_b[:, D_EXPERT:] = wu_l[slot].astype(BF16)
        wd_b[...] = wd_l[slot].astype(BF16)
        live[0] = q + 1
        nxt = q + W_SLOTS - 1

        @pl.when(nxt < live[1])
        def _():
            for cp in weight_copies(elist[nxt], nxt % W_SLOTS):
                cp.start()

    def gather(j, xb, gb, rows=(0, MT)):
        base = j * MT
        for m in range(*rows):
            t = tok_ref[base + m]
            xb[pl.ds(m, 4, stride=MT_STRIDE), :] = src_v[pl.ds(pl.multiple_of(t * 4, 4), 4), :]
            gb[m:m + 1, :] = gate_v[pl.ds(t, 1), :]

    def scatter(j, yb, rows=(0, MT)):
        base = j * MT
        for b in range(rows[0] // RMW_BATCH, rows[1] // RMW_BATCH):
            ms = [b * RMW_BATCH + u for u in range(RMW_BATCH)]
            targets = [pl.ds(pl.multiple_of(tok_ref[base + m] * 8, 8), 8) for m in ms]
            vals = [acc_v[r, :] + yb[pl.ds(m, 8, stride=MT_STRIDE), :] for r, m in zip(targets, ms)]
            for r, v in zip(targets, vals):
                acc_v[r, :] = v

    quarters = [(q * MT // 4, (q + 1) * MT // 4) for q in range(4)]

    def step(j, p):
        pl.when(tile_first[j] == 1)(switch_expert)
        xb, gb, yb = xbuf[p], gbuf[p], ybuf[p]
        nxt = (j + 1, xbuf[1 - p], gbuf[1 - p])
        prv = (jnp.maximum(j - 1, 0), ybuf[1 - p])
        lo, hi = [], []
        for c in range(4):
            words = xb[MT_STRIDE * c:MT_STRIDE * c + MT, :]
            unpack = functools.partial(pltpu.unpack_elementwise, words, packed_dtype=BF16, unpacked_dtype=F32)
            lo.append(unpack(index=0).astype(BF16))
            hi.append(unpack(index=1).astype(BF16))
        x = jnp.concatenate(lo + hi, axis=1)
        gather(*nxt, rows=quarters[0])
        h_gate = _dot(x, wgu_b[:, 0:D_EXPERT])
        gather(*nxt, rows=quarters[1])
        h_up = _dot(x, wgu_b[:, D_EXPERT:])
        gather(*nxt, rows=quarters[2])
        g = gb[...]
        g_hi = g.astype(BF16)
        g_lo = (g - g_hi.astype(F32)).astype(BF16)
        pick = (lax.broadcasted_iota(jnp.int32, (128, D_EXPERT), 0) == tile_e[j]).astype(BF16)
        g_col = _dot(g_hi, pick) + _dot(g_lo, pick)
        a = (_silu(h_gate) * h_up * g_col).astype(BF16)
        gather(*nxt, rows=quarters[3])
        scatter(*prv, rows=quarters[0])
        y_lo = _dot(a, wd_b[:, 0:D_MODEL // 2])
        scatter(*prv, rows=quarters[1])
        y_hi = _dot(a, wd_b[:, D_MODEL // 2:])
        scatter(*prv, rows=quarters[2])
        for c in range(4):
            yb[MT_STRIDE * c:MT_STRIDE * c + MT, :] = y_lo[:, 128 * c:128 * (c + 1)]
            yb[MT_STRIDE * (c + 4):MT_STRIDE * (c + 4) + MT, :] = y_hi[:, 128 * c:128 * (c + 1)]
        scatter(*prv, rows=quarters[3])

    gather(0, xbuf[0], gbuf[0])
    n_pairs = (n_tiles + 1) // 2

    def pair(i, carry):
        step(2 * i, 0)
        step(2 * i + 1, 1)
        return carry

    lax.fori_loop(0, n_pairs, pair, 0)
    scatter(jnp.maximum(2 * n_pairs - 1, 0), ybuf[1])

    out_cp = pltpu.make_async_copy(acc_v.at[pl.ds(0, 8 * T_HALF)],
                                   out_hbm.at[pl.ds(pl.multiple_of(h * (8 * T_HALF), 8), 8 * T_HALF)],
                                   sem.at[2])
    out_cp.start()
    out_cp.wait()


def _moe(slot_words, cnt, off, src, gate, wg, wu, wd):
    any_spec = pl.BlockSpec(memory_space=pl.ANY)
    tile_buf = lambda rows, dt: pltpu.VMEM((rows * MT_STRIDE, 128), dt)
    return pl.pallas_call(
        _moe_kernel,
        grid_spec=pltpu.PrefetchScalarGridSpec(
            num_scalar_prefetch=3,
            grid=(2,),
            in_specs=[any_spec] * 5,
            out_specs=any_spec,
            scratch_shapes=[pltpu.VMEM((4 * T_HALF + 8, 128), jnp.uint32),
                            pltpu.VMEM((T_HALF + 8, 128), F32),
                            pltpu.VMEM((8 * T_HALF + 8, 128), F32),
                            pltpu.VMEM((W_SLOTS, D_MODEL, D_EXPERT), F32),
                            pltpu.VMEM((W_SLOTS, D_MODEL, D_EXPERT), F32),
                            pltpu.VMEM((W_SLOTS, D_EXPERT, D_MODEL), F32),
                            pltpu.VMEM((D_MODEL, 2 * D_EXPERT), BF16),
                            pltpu.VMEM((D_EXPERT, D_MODEL), BF16),
                            tile_buf(4, jnp.uint32), tile_buf(4, jnp.uint32),
                            pltpu.VMEM((MT, 128), F32), pltpu.VMEM((MT, 128), F32),
                            tile_buf(8, F32), tile_buf(8, F32),
                            pltpu.SMEM((SLOT_CAP,), jnp.int32),
                            pltpu.SMEM((NT_MAX + 2,), jnp.int32),
                            pltpu.SMEM((NT_MAX + 2,), jnp.int32),
                            pltpu.SMEM((N_EXPERTS,), jnp.int32),
                            pltpu.SMEM((2,), jnp.int32),
                            pltpu.SemaphoreType.DMA((3,)),
                            pltpu.SemaphoreType.DMA((W_SLOTS, 3))]),
        out_shape=jax.ShapeDtypeStruct((8 * T_ALL, 128), F32),
        compiler_params=_cparams(("arbitrary",)),
        name="moe",
    )(slot_words, cnt, off, src, gate, wg, wu, wd)


def _finalize_kernel(acc_ref, base_ref, mod_ref, oc_ref, ol_ref):
    def write(o_ref):
        for c in range(8):
            cs = slice(128 * c, 128 * (c + 1))
            o_ref[:, cs] = base_ref[:, cs] + mod_ref[5:6, cs] * acc_ref[pl.ds(c, TBP, stride=8), :]

    is_ctx = pl.program_id(0) < NP_CTX
    pl.when(is_ctx)(lambda: write(oc_ref))
    pl.when(jnp.logical_not(is_ctx))(lambda: write(ol_ref))


def _finalize(acc, base, mod3):
    return pl.pallas_call(
        _finalize_kernel,
        grid=(T_ALL // TBP,),
        in_specs=[pl.BlockSpec((8 * TBP, 128), lambda i: (i, 0)),
                  pl.BlockSpec((TBP, D_MODEL), lambda i: (i, 0)),
                  pl.BlockSpec((None, 6, D_MODEL), lambda i: (_mod_row(i), 0, 0))],
        out_specs=_split_specs(D_MODEL),
        out_shape=[jax.ShapeDtypeStruct((T_CTX, D_MODEL), F32), jax.ShapeDtypeStruct((T_LAT, D_MODEL), F32)],
        compiler_params=_cparams(("arbitrary",)),
        name="finalize",
    )(acc, base, mod3)


def _head_indicators(width):
    head = jnp.arange(width) // DH_ATT
    ind = (head[:, None] == jnp.arange(128)[None, :]).astype(BF16)
    return ind, ind.T


def kernel(x_prompt, x_sample, cache_attn_k, cache_attn_v, state_mlstm_c, state_mlstm_n, state_mlstm_m, c, c_ctx, w_mod, b_mod, norm1_w, norm2_w, w_in, q_norm_w, k_norm_w, b_gates, m_norm_w, w_out, router_w, router_bias, w_gate, w_up, w_down, ws_gate, ws_up, ws_down):
    x = (x_prompt.reshape(T_CTX, D_MODEL), x_sample.reshape(T_LAT, D_MODEL))
    mod3 = _modulation(c, c_ctx, w_mod[0], b_mod[0])

    w_main = w_in[0, :, :P_MAIN].astype(BF16)
    w_gates_t = w_in[0, :, P_MAIN:].T.astype(BF16)
    qw = jnp.tile(q_norm_w, (1, N_HEADS_ATT))
    kw = jnp.tile(k_norm_w, (1, N_KV_HEADS))
    inds = _head_indicators(512) + _head_indicators(128)
    qn, kn, va, qm, km, vm, om, gt = _inproj(*x, mod3, norm1_w, w_main, w_gates_t, qw, kw, inds, _rope_tables())

    att_c = _attention(qn, kn, va, None, n_batch=BATCH, seq=SEQ, first_row=0)
    att_l = _attention(qn, kn, va, (cache_attn_k, cache_attn_v), n_batch=DEC_BATCH, seq=DEC_SEQ,
                       first_row=T_CTX)

    g3 = gt.reshape(16, T_ALL // CHUNK, CHUNK).transpose(1, 0, 2)
    bg = b_gates.reshape(16, 1)
    hm_c, c_new, n_new, m_new = _mlstm(qm, km, vm, g3, bg, om, m_norm_w, None,
                                       n_batch=BATCH, seq=SEQ, first_block=0)
    m0 = jnp.broadcast_to(state_mlstm_m.reshape(DEC_BATCH, 8, 1), (DEC_BATCH, 8, CHUNK))
    hm_l = _mlstm(qm, km, vm, g3, bg, om, m_norm_w, (state_mlstm_c, state_mlstm_n, m0),
                  n_batch=DEC_BATCH, seq=DEC_SEQ, first_block=N_CTX_BLOCKS)

    rw_t = router_w[0].T
    rw_hi = rw_t.astype(BF16)
    rw_lo = (rw_t - rw_hi.astype(F32)).astype(BF16)
    base, hn2, logits_t = _outproj((att_c, att_l), (hm_c, hm_l), x, mod3, norm2_w, w_out[0].astype(BF16),
                                   rw_hi, rw_lo, ws_gate[0].astype(BF16), ws_up[0].astype(BF16),
                                   ws_down[0].astype(BF16))
    gate, idx8 = _router(logits_t, router_bias.reshape(N_EXPERTS, 1))
    slot_words, cnt, off = _plan(idx8)
    acc = _moe(slot_words.reshape(TOP_K * T_HALF), cnt[:, :, 0].reshape(2 * N_EXPERTS),
               off[:, :, 0].reshape(2 * N_EXPERTS), hn2, gate, w_gate[0], w_up[0], w_down[0])
    out_c, out_l = _finalize(acc, base, mod3)

    y_prompt = out_c.reshape(BATCH, SEQ, D_MODEL)
    y_sample = out_l.reshape(DEC_BATCH, DEC_SEQ, D_MODEL)
    to_cache = lambda a: a[:T_CTX].reshape(BATCH, SEQ, N_KV_HEADS, DH_ATT).transpose(0, 2, 1, 3)[:, None]
    new_k = to_cache(kn)
    new_v = to_cache(va)
    new_m = m_new[:, :, 0].reshape(BATCH, 1, 2, N_HEADS_M)
    return (y_prompt, y_sample, new_k, new_v, c_new[:, None], n_new[:, None], new_m)
```

```python
import functools

import jax
import jax.numpy as jnp
from jax import lax
from jax.experimental import pallas as pl
from jax.experimental.pallas import tpu as pltpu

F32 = jnp.float32
BF16 = jnp.bfloat16

D_MODEL = 1024
BATCH = 32
SEQ = 256
DEC_BATCH = 2
DEC_SEQ = 1024
PAST_LEN = 256
GRID_W = 64
N_HEADS_ATT = 8
N_KV_HEADS = 2
DH_ATT = 64
D_ATT = 512
ROPE_THETA = 10000.0
N_HEADS_M = 4
DH_M = 128
D_M = 512
CHUNK = 128
N_EXPERTS = 64
TOP_K = 8
N_GROUPS = 8
TOPK_GROUPS = 4
D_EXPERT = 256
ROUTED_SCALE = 2.5
EPS = 1e-6
NEG_INIT = -1e30

T_CTX = BATCH * SEQ
T_LAT = DEC_BATCH * DEC_SEQ
T_ALL = T_CTX + T_LAT
TB = 256
N_CTX_BLOCKS = T_CTX // TB
TBP = 512
NP_CTX = T_CTX // TBP
NP_LAT_PER_BATCH = DEC_SEQ // TBP
TB_MOE = 1024
T_HALF = T_ALL // 2
MT = 256
MT_STRIDE = MT + 8
NT_MAX = TOP_K * T_HALF // MT + N_EXPERTS
SLOT_CAP = (NT_MAX + 2) * MT
RMW_BATCH = 16
INV_UNROLL = 16
W_SLOTS = 3
assert SLOT_CAP < 2 ** 16 and (T_HALF // 2) % INV_UNROLL == 0 and MT % RMW_BATCH == 0
P_MAIN = 2816
VMEM_LIMIT = 56 * 1024 * 1024

_NT = (((1,), (1,)), ((), ()))
_TN = (((0,), (0,)), ((), ()))


def _cparams(sem):
    return pltpu.CompilerParams(dimension_semantics=sem, vmem_limit_bytes=VMEM_LIMIT)


def _split3(x):
    hi = x.astype(BF16)
    r1 = x - hi.astype(F32)
    mid = r1.astype(BF16)
    lo = (r1 - mid.astype(F32)).astype(BF16)
    return hi, mid, lo


def _dot(a, b):
    return jnp.dot(a, b, preferred_element_type=F32)


def _dot3(x, m_bf16):
    hi, mid, lo = _split3(x)
    return _dot(hi, m_bf16) + _dot(mid, m_bf16) + _dot(lo, m_bf16)


def _dot2(x, m_bf16):
    hi = x.astype(BF16)
    lo = (x - hi.astype(F32)).astype(BF16)
    return _dot(hi, m_bf16) + _dot(lo, m_bf16)


def _silu(x):
    return x * jax.nn.sigmoid(x)


def _mod_row(i):
    return jnp.where(i < NP_CTX, 0, 1 + (i - NP_CTX) // NP_LAT_PER_BATCH)


def _mod_kernel(ct_ref, w_ref, b_ref, o_ref):
    s = _silu(ct_ref[...])
    w = w_ref[...]
    rows = [jnp.sum(w * s[:, r:r + 1], axis=0, keepdims=True) for r in range(3)]
    rows.append(jnp.zeros((5, w.shape[1]), F32))
    o_ref[...] = jnp.concatenate(rows, axis=0) + b_ref[...]


def _modulation(c, c_ctx, w_mod, b_mod):
    cvec = jnp.concatenate([c_ctx[None, :], c, jnp.zeros((5, D_MODEL), F32)], axis=0)
    nb = 1024
    out = pl.pallas_call(
        _mod_kernel,
        grid=(6 * D_MODEL // nb,),
        in_specs=[pl.BlockSpec((D_MODEL, 8), lambda j: (0, 0)),
                  pl.BlockSpec((D_MODEL, nb), lambda j: (0, j)),
                  pl.BlockSpec((1, nb), lambda j: (0, j))],
        out_specs=pl.BlockSpec((8, nb), lambda j: (0, j)),
        out_shape=jax.ShapeDtypeStruct((8, 6 * D_MODEL), F32),
        compiler_params=_cparams(("arbitrary",)),
        name="modulation",
    )(cvec.T, w_mod, b_mod[None, :])
    return out.reshape(8, 6, D_MODEL)


def _ctx_or_lat(ctx_ref, lat_ref):
    return jnp.where(pl.program_id(0) < NP_CTX, ctx_ref[...], lat_ref[...])


def _split_specs(width):
    return [pl.BlockSpec((TBP, width), lambda i: (jnp.minimum(i, NP_CTX - 1), 0)),
            pl.BlockSpec((TBP, width), lambda i: (jnp.maximum(i - NP_CTX, 0), 0))]


def _head_norm(x, ind, ind_t, w_row):
    ss = _dot2(x * x, ind)
    inv = lax.rsqrt(ss * (1.0 / DH_ATT) + EPS)
    return x * _dot2(inv, ind_t) * w_row


def _rope(x, cos, sin_signed):
    lane = lax.broadcasted_iota(jnp.int32, x.shape, 1)
    partner = jnp.where((lane % 32) < 16, pltpu.roll(x, 128 - 16, 1), pltpu.roll(x, 16, 1))
    return x * cos + partner * sin_signed


def _inproj_kernel(xc_ref, xl_ref, mod_ref, nw_ref, w_ref, wgt_ref, qw_ref, kw_ref, iq_ref, iqt_ref,
                   ik_ref, ikt_ref, cos_ref, sin_ref,
                   qn_ref, kn_ref, va_ref, qm_ref, km_ref, vm_ref, om_ref, gt_ref):
    x = _ctx_or_lat(xc_ref, xl_ref)
    y = x * lax.rsqrt(jnp.mean(x * x, axis=-1, keepdims=True) + EPS) * nw_ref[...]
    hn = y * (1.0 + mod_ref[1:2, :]) + mod_ref[0:1, :]
    hb = hn.astype(BF16)
    qn = _head_norm(_dot(hb, w_ref[:, 0:512]), iq_ref[...], iqt_ref[...], qw_ref[...])
    kn = _head_norm(_dot(hb, w_ref[:, 512:640]), ik_ref[...], ikt_ref[...], kw_ref[...])
    qn_ref[...] = qn.astype(BF16)
    kn_ref[...] = kn

    @pl.when(pl.program_id(0) >= NP_CTX)
    def _():
        cos, sin = cos_ref[...], sin_ref[...]
        for j in range(4):
            qn_ref[:, 128 * j:128 * (j + 1)] = _rope(qn[:, 128 * j:128 * (j + 1)], cos, sin).astype(BF16)
        kn_ref[...] = _rope(kn, cos, sin)

    va_ref[...] = _dot(hb, w_ref[:, 640:768])
    qm_ref[...] = _dot(hb, w_ref[:, 768:1280]).astype(BF16)
    km_ref[...] = (_dot(hb, w_ref[:, 1280:1792]) * (DH_M ** -0.5)).astype(BF16)
    vm_ref[...] = _dot(hb, w_ref[:, 1792:2304]).astype(BF16)
    om_ref[...] = _dot(hb, w_ref[:, 2304:2816])
    gt_ref[...] = lax.dot_general(wgt_ref[...], hb, _NT, preferred_element_type=F32)


def _inproj(x_ctx, x_lat, mod3, norm1_w, w_main, w_gates_t, qw, kw, inds, rope_tabs):
    tok = lambda w: pl.BlockSpec((TBP, w), lambda i: (i, 0))
    full = lambda a: pl.BlockSpec(a.shape, lambda i: (0,) * a.ndim)
    sd = lambda w, dt: jax.ShapeDtypeStruct((T_ALL, w), dt)
    rope_spec = pl.BlockSpec((TBP, 128), lambda i: (jnp.maximum(i - NP_CTX, 0) % NP_LAT_PER_BATCH, 0))
    return pl.pallas_call(
        _inproj_kernel,
        grid=(T_ALL // TBP,),
        in_specs=_split_specs(D_MODEL) + [
                  pl.BlockSpec((None, 6, D_MODEL), lambda i: (_mod_row(i), 0, 0)),
                  full(norm1_w), full(w_main), full(w_gates_t), full(qw), full(kw)]
                 + [full(a) for a in inds] + [rope_spec, rope_spec],
        out_specs=[tok(512), tok(128), tok(128), tok(512), tok(512), tok(512), tok(512),
                   pl.BlockSpec((16, TBP), lambda i: (0, i))],
        out_shape=[sd(512, BF16), sd(128, F32), sd(128, F32), sd(512, BF16), sd(512, BF16),
                   sd(512, BF16), sd(512, F32), jax.ShapeDtypeStruct((16, T_ALL), F32)],
        compiler_params=_cparams(("arbitrary",)),
        name="inproj",
    )(x_ctx, x_lat, mod3, norm1_w, w_main, w_gates_t, qw, kw, *inds, *rope_tabs)


def _rope_tables():
    t = jnp.arange(DEC_SEQ)
    pos = jnp.stack([t // GRID_W, t % GRID_W], axis=1).astype(F32)
    n_freq = DH_ATT // 4
    inv_freq = ROPE_THETA ** (-jnp.arange(n_freq, dtype=F32) / n_freq)
    ang = pos[:, :, None] * inv_freq
    cos, sin = jnp.cos(ang), jnp.sin(ang)
    cos_h = jnp.stack([cos, cos], axis=2).reshape(DEC_SEQ, DH_ATT)
    sin_h = jnp.stack([-sin, sin], axis=2).reshape(DEC_SEQ, DH_ATT)
    return jnp.tile(cos_h, (1, 2)), jnp.tile(sin_h, (1, 2))


def _attn_kernel(*refs, has_cache):
    if has_cache:
        q_ref, k_ref, v_ref, kc_ref, vc_ref, o_ref = refs
    else:
        q_ref, k_ref, v_ref, o_ref = refs
    q = q_ref[...]
    k = k_ref[...].astype(BF16)
    v = v_ref[...].astype(BF16)
    qb = q.shape[0]
    scale = DH_ATT ** -0.5
    outs = []
    for g in range(N_KV_HEADS):
        kg = k[:, DH_ATT * g:DH_ATT * (g + 1)]
        vg = v[:, DH_ATT * g:DH_ATT * (g + 1)]
        qg = jnp.concatenate([q[:, 256 * g + DH_ATT * j:256 * g + DH_ATT * (j + 1)] for j in range(4)], axis=0)
        s = lax.dot_general(qg, kg, _NT, preferred_element_type=F32) * scale
        m = jnp.max(s, axis=-1, keepdims=True)
        if has_cache:
            kc = kc_ref[g].astype(BF16)
            vc = vc_ref[g].astype(BF16)
            sc = lax.dot_general(qg, kc, _NT, preferred_element_type=F32) * scale
            m = jnp.maximum(m, jnp.max(sc, axis=-1, keepdims=True))
        p = jnp.exp(s - m)
        den = jnp.sum(p, axis=-1, keepdims=True)
        o = _dot(p.astype(BF16), vg)
        if has_cache:
            pc = jnp.exp(sc - m)
            den = den + jnp.sum(pc, axis=-1, keepdims=True)
            o = o + _dot(pc.astype(BF16), vc)
        o = o / den
        outs += [o[qb * j:qb * (j + 1), :] for j in range(4)]
    o_ref[...] = jnp.concatenate(outs, axis=1).astype(BF16)


def _attention(qn, kn, v_all, cache, *, n_batch, seq, first_row):
    qblocks = seq // TB
    kv_spec = pl.BlockSpec((seq, 128), lambda b, i: (b + first_row // seq, 0))
    in_specs = [pl.BlockSpec((TB, 512), lambda b, i: (first_row // TB + b * qblocks + i, 0)), kv_spec, kv_spec]
    args = [qn, kn, v_all]
    if cache is not None:
        cspec = pl.BlockSpec((None, None, N_KV_HEADS, PAST_LEN, DH_ATT), lambda b, i: (b, 0, 0, 0, 0))
        in_specs += [cspec, cspec]
        args += list(cache)
    return pl.pallas_call(
        functools.partial(_attn_kernel, has_cache=cache is not None),
        grid=(n_batch, qblocks),
        in_specs=in_specs,
        out_specs=pl.BlockSpec((TB, 512), lambda b, i: (b * qblocks + i, 0)),
        out_shape=jax.ShapeDtypeStruct((n_batch * seq, 512), BF16),
        compiler_params=_cparams(("arbitrary", "arbitrary")),
        name="attention_lat" if cache is not None else "attention_ctx",
    )(*args)


def _log_sigmoid(x):
    return jnp.minimum(x, 0.0) - jnp.log1p(jnp.exp(-jnp.abs(x)))


def _col_bcast(cols, j):
    return jnp.broadcast_to(cols[:, j:j + 1], (CHUNK, CHUNK))


def _mlstm_kernel(*refs, has_state, n_chunks):
    n_in = 10 if has_state else 7
    n_out = 1 if has_state else 4
    q_ref, k_ref, v_ref, g_ref, bg_ref, om_ref, nw_ref = refs[:7]
    hm_ref = refs[n_in]
    scratch = refs[n_in + n_out:]
    st = scratch[0:8]
    ms = scratch[8:16]
    hdir = scratch[16:18]
    rows_sc, cmb_sc, bb_sc, kt_sc = scratch[18:22]
    neg_inf = F32(-jnp.inf)
    zeros112 = jnp.zeros((CHUNK - 16, CHUNK), F32)
    sub = lax.broadcasted_iota(jnp.int32, (CHUNK, CHUNK), 0)
    lan = lax.broadcasted_iota(jnp.int32, (CHUNK, CHUNK), 1)
    ones = jnp.ones((CHUNK, CHUNK), BF16)

    stats = []
    for c in range(n_chunks):
        pre = g_ref[c] + bg_ref[...]
        logf = _log_sigmoid(pre)
        for d in range(2):
            bcum = _dot3(logf, ((sub <= lan) if d == 0 else (sub >= lan)).astype(BF16))
            li4 = pre[4 * d:4 * d + 4, :]
            lf4 = logf[8 + 4 * d:12 + 4 * d, :]
            b4 = bcum[8 + 4 * d:12 + 4 * d, :]
            stats.append((c, d, li4 - b4, lf4, b4))
    for c, d, r4, lf4, b4 in stats:
        blast4 = b4[:, CHUNK - 1:CHUNK] if d == 0 else b4[:, 0:1]
        wlog4 = blast4 + r4
        wmax4 = jnp.max(wlog4, axis=-1, keepdims=True)
        full = lambda a: jnp.broadcast_to(a, (4, CHUNK))
        rows_sc[2 * c + d] = jnp.concatenate(
            [r4, wlog4, full(blast4), full(wmax4), jnp.zeros((16, CHUNK), F32)], axis=0)
    for c in range(n_chunks):
        for h in range(N_HEADS_M):
            kc = k_ref[CHUNK * c:CHUNK * (c + 1), DH_M * h:DH_M * (h + 1)]
            kt_sc[c * N_HEADS_M + h] = kc.astype(F32).T.astype(BF16)
    for c, d, r4, lf4, b4 in stats:
        within = (lan <= sub) if d == 0 else (lan >= sub)
        for h in range(N_HEADS_M):
            cummax = jnp.max(jnp.where(within, r4[h:h + 1, :], neg_inf), axis=-1, keepdims=True)
            cmb_sc[(2 * c + d) * N_HEADS_M + h] = jnp.broadcast_to(cummax, (CHUNK, CHUNK))
    for c, d, r4, lf4, b4 in stats:
        within = (lan <= sub) if d == 0 else (lan >= sub)
        for h in range(N_HEADS_M):
            bb_sc[(2 * c + d) * N_HEADS_M + h] = _dot2(jnp.where(within, lf4[h:h + 1, :], 0.0), ones)

    for d in range(2):
        if has_state:
            c0_ref, n0_ref, m0_ref = refs[7:10]
            ncols = jnp.concatenate([n0_ref[d], jnp.zeros((12, DH_M), F32), zeros112], axis=0).T
        for h in range(N_HEADS_M):
            idx = 4 * d + h
            if has_state:
                st[idx][:, 0:DH_M] = c0_ref[d, h]
                st[idx][:, DH_M:] = _col_bcast(ncols, h)
                ms[idx][0:1, :] = m0_ref[idx:idx + 1, :]
            else:
                st[idx][...] = jnp.zeros((DH_M, 2 * DH_M), F32)
                ms[idx][0:1, :] = jnp.full((1, CHUNK), NEG_INIT, F32)

    def chunk_step(i, carry):
        for d in range(2):
            c = i if d == 0 else n_chunks - 1 - i
            r0 = pl.multiple_of(c * CHUNK, CHUNK)
            rows = rows_sc[2 * c + d]
            mask = (lan <= sub) if d == 0 else (lan >= sub)
            heads = []
            for h in range(N_HEADS_M):
                idx = 4 * d + h
                hs = slice(DH_M * h, DH_M * (h + 1))
                qc = q_ref[pl.ds(r0, CHUNK), hs]
                kt = kt_sc[c * N_HEADS_M + h]
                state = st[idx][...]
                heads.append(dict(idx=idx, hs=hs, qc=qc, kt=kt, state=state,
                                  s_raw=_dot(qc, kt), qs=_dot(qc, state.astype(BF16))))
            for h, hd in enumerate(heads):
                mp = ms[hd["idx"]][0:1, :]
                m_b = jnp.maximum(mp, cmb_sc[(2 * c + d) * N_HEADS_M + h])
                s = hd["s_raw"] * jnp.exp(jnp.where(mask, rows[h:h + 1, :] - m_b, neg_inf))
                s_hi = s.astype(BF16)
                vc = v_ref[pl.ds(r0, CHUNK), hd["hs"]]
                v_ones = jnp.concatenate([vc, ones], axis=1)
                hd.update(mp=mp, m_b=m_b, v_ones=v_ones, sv=_dot(s_hi, v_ones),
                          s_lo_sum=_dot((s - s_hi.astype(F32)).astype(BF16), ones))
            for h, hd in enumerate(heads):
                mp, m_b = hd["mp"], hd["m_b"]
                gw_b = jnp.exp(mp - m_b)
                en_b = jnp.exp(-(bb_sc[(2 * c + d) * N_HEADS_M + h] + m_b))
                num = hd["sv"][:, 0:DH_M] + gw_b * hd["qs"][:, 0:DH_M]
                den = hd["sv"][:, DH_M:] + hd["s_lo_sum"] + gw_b * hd["qs"][:, DH_M:]
                hdir[d][pl.ds(r0, CHUNK), hd["hs"]] = num / jnp.maximum(jnp.abs(den), en_b)
            for h, hd in enumerate(heads):
                mp = hd["mp"]
                gend = rows[8 + h:9 + h, :] + mp
                mnew = jnp.maximum(gend, rows[12 + h:13 + h, :])
                w_row = jnp.exp(rows[4 + h:5 + h, :] - mnew)
                kw_t = (hd["kt"].astype(F32) * w_row).astype(BF16)
                dec = jnp.exp(gend - mnew)
                st[hd["idx"]][...] = (jnp.concatenate([dec, dec], axis=1) * hd["state"]
                                      + _dot(kw_t, hd["v_ones"]))
                ms[hd["idx"]][0:1, :] = mnew
        return carry

    lax.fori_loop(0, n_chunks, chunk_step, 0)

    for h in range(N_HEADS_M):
        hs = slice(DH_M * h, DH_M * (h + 1))
        hh = hdir[0][:, hs] + hdir[1][:, hs]
        y = hh * lax.rsqrt(jnp.mean(hh * hh, axis=-1, keepdims=True) + EPS) * nw_ref[:, hs]
        hm_ref[:, hs] = (jax.nn.sigmoid(om_ref[:, hs]) * y).astype(BF16)
    if not has_state:
        c_ref, n_ref, m_ref = refs[n_in + 1:n_in + 4]
        for d in range(2):
            for h in range(N_HEADS_M):
                idx = 4 * d + h
                c_ref[d, h] = st[idx][:, 0:DH_M]
                n_ref[d, h:h + 1, :] = st[idx][:, DH_M:].T[0:1, :]
                m_ref[idx:idx + 1, :] = ms[idx][0:1, :]


def _mlstm(qm, km, vm, g3, bg, om, nw, state, *, n_batch, seq, first_block):
    n_chunks = seq // CHUNK
    rb = seq // TB
    tok = lambda: pl.BlockSpec((seq, 512), lambda b: (b + first_block // rb, 0))
    full = lambda a: pl.BlockSpec(a.shape, lambda b: (0,) * a.ndim)
    in_specs = [tok(), tok(), tok(),
                pl.BlockSpec((n_chunks, 16, CHUNK), lambda b: (b + first_block // rb, 0, 0)),
                full(bg), tok(), full(nw)]
    args = [qm, km, vm, g3, bg, om, nw]
    hm_spec = pl.BlockSpec((seq, 512), lambda b: (b, 0))
    hm_shape = jax.ShapeDtypeStruct((n_batch * seq, 512), BF16)
    scratch = ([pltpu.VMEM((DH_M, 2 * DH_M), F32)] * 8 + [pltpu.VMEM((8, CHUNK), F32)] * 8
               + [pltpu.VMEM((seq, 512), F32)] * 2
               + [pltpu.VMEM((2 * n_chunks, 32, CHUNK), F32),
                  pltpu.VMEM((8 * n_chunks, CHUNK, CHUNK), F32),
                  pltpu.VMEM((8 * n_chunks, CHUNK, CHUNK), F32),
                  pltpu.VMEM((N_HEADS_M * n_chunks, DH_M, CHUNK), BF16)])
    if state is not None:
        c0, n0, m0 = state
        in_specs += [pl.BlockSpec((None, None, 2, N_HEADS_M, DH_M, DH_M), lambda b: (b, 0, 0, 0, 0, 0)),
                     pl.BlockSpec((None, None, 2, N_HEADS_M, DH_M), lambda b: (b, 0, 0, 0, 0)),
                     pl.BlockSpec((None, 8, CHUNK), lambda b: (b, 0, 0))]
        args += [c0, n0, m0]
        out_specs, out_shape = hm_spec, hm_shape
    else:
        out_specs = [hm_spec,
                     pl.BlockSpec((None, 2, N_HEADS_M, DH_M, DH_M), lambda b: (b, 0, 0, 0, 0)),
                     pl.BlockSpec((None, 2, N_HEADS_M, DH_M), lambda b: (b, 0, 0, 0)),
                     pl.BlockSpec((None, 8, CHUNK), lambda b: (b, 0, 0))]
        out_shape = [hm_shape,
                     jax.ShapeDtypeStruct((n_batch, 2, N_HEADS_M, DH_M, DH_M), F32),
                     jax.ShapeDtypeStruct((n_batch, 2, N_HEADS_M, DH_M), F32),
                     jax.ShapeDtypeStruct((n_batch, 8, CHUNK), F32)]
    return pl.pallas_call(
        functools.partial(_mlstm_kernel, has_state=state is not None, n_chunks=n_chunks),
        grid=(n_batch,),
        in_specs=in_specs,
        out_specs=out_specs,
        out_shape=out_shape,
        scratch_shapes=scratch,
        compiler_params=_cparams(("arbitrary",)),
        name="mlstm_lat" if state is not None else "mlstm_ctx",
    )(*args)


def _outproj_kernel(attc_ref, attl_ref, hmc_ref, hml_ref, xc_ref, xl_ref, mod_ref, nw_ref, wo_ref,
                    rwh_ref, rwl_ref, wsg_ref, wsu_ref, wsd_ref, base_ref, hn_ref, lt_ref):
    y = (_dot(_ctx_or_lat(attc_ref, attl_ref), wo_ref[0:D_ATT, :])
         + _dot(_ctx_or_lat(hmc_ref, hml_ref), wo_ref[D_ATT:, :]))
    x1 = _ctx_or_lat(xc_ref, xl_ref) + mod_ref[2:3, :] * y
    z = x1 * lax.rsqrt(jnp.mean(x1 * x1, axis=-1, keepdims=True) + EPS) * nw_ref[...]
    hn = z * (1.0 + mod_ref[4:5, :]) + mod_ref[3:4, :]
    hb = hn.astype(BF16)
    hl = (hn - hb.astype(F32)).astype(BF16)
    nt = lambda w, t: lax.dot_general(w, t, _NT, preferred_element_type=F32)
    lt_ref[...] = nt(rwh_ref[...], hb) + nt(rwl_ref[...], hb) + nt(rwh_ref[...], hl)
    a = _silu(_dot(hb, wsg_ref[...])) * _dot(hb, wsu_ref[...])
    shared = _dot(a.astype(BF16), wsd_ref[...])
    base_ref[...] = x1 + mod_ref[5:6, :] * shared
    packed = pltpu.pack_elementwise([hn[:, :512], hn[:, 512:]], packed_dtype=BF16)
    for c in range(4):
        hn_ref[pl.ds(c, TBP, stride=4), :] = packed[:, 128 * c:128 * (c + 1)]


def _outproj(att, hm, x, mod3, norm2_w, w_out, rw_hi, rw_lo, wsg, wsu, wsd):
    tok = lambda w: pl.BlockSpec((TBP, w), lambda i: (i, 0))
    full = lambda a: pl.BlockSpec(a.shape, lambda i: (0,) * a.ndim)
    return pl.pallas_call(
        _outproj_kernel,
        grid=(T_ALL // TBP,),
        in_specs=_split_specs(512) + _split_specs(512) + _split_specs(D_MODEL) + [
                  pl.BlockSpec((None, 6, D_MODEL), lambda i: (_mod_row(i), 0, 0)),
                  full(norm2_w), full(w_out), full(rw_hi), full(rw_lo), full(wsg), full(wsu), full(wsd)],
        out_specs=[tok(D_MODEL), pl.BlockSpec((4 * TBP, 128), lambda i: (i, 0)),
                   pl.BlockSpec((N_EXPERTS, TBP), lambda i: (0, i))],
        out_shape=[jax.ShapeDtypeStruct((T_ALL, D_MODEL), F32),
                   jax.ShapeDtypeStruct((4 * T_ALL, 128), jnp.uint32),
                   jax.ShapeDtypeStruct((N_EXPERTS, T_ALL), F32)],
        compiler_params=_cparams(("arbitrary",)),
        name="outproj",
    )(*att, *hm, *x, mod3, norm2_w, w_out, rw_hi, rw_lo, wsg, wsu, wsd)


def _first_max(vals, ids, limit):
    m = functools.reduce(jnp.maximum, [jnp.max(v, axis=0, keepdims=True) for v in vals])
    cand = [jnp.min(jnp.where(v == m, i, limit), axis=0, keepdims=True) for v, i in zip(vals, ids)]
    return m, functools.reduce(jnp.minimum, cand)


def _router_kernel(lt_ref, bias_ref, gate_ref, idx_ref):
    n = lt_ref.shape[1]
    score = jax.nn.sigmoid(lt_ref[...])
    biased = score + bias_ref[...]
    sub = lax.broadcasted_iota(jnp.int32, (8, n), 0).astype(F32)
    neg_inf = F32(-jnp.inf)
    slabs = [biased[8 * g:8 * (g + 1), :] for g in range(N_GROUPS)]
    gs = []
    for sl in slabs:
        m1, i1 = _first_max([sl], [sub], 8.0)
        m2 = jnp.max(jnp.where(sub == i1, neg_inf, sl), axis=0, keepdims=True)
        gs.append(m1 + m2)
    cur = jnp.concatenate(gs, axis=0)
    gsel = jnp.zeros((8, n), F32)
    for _ in range(TOPK_GROUPS):
        _, i = _first_max([cur], [sub], 8.0)
        hit = sub == i
        gsel = jnp.where(hit, 1.0, gsel)
        cur = jnp.where(hit, neg_inf, cur)
    vals = [jnp.where(gsel[g:g + 1, :] > 0.0, slabs[g], neg_inf) for g in range(N_GROUPS)]
    ids = [sub + 8.0 * g for g in range(N_GROUPS)]
    picked = [jnp.zeros((8, n), F32) for _ in range(N_GROUPS)]
    order = []
    for _ in range(TOP_K):
        _, i = _first_max(vals, ids, float(N_EXPERTS))
        order.append(i)
        hits = [idg == i for idg in ids]
        picked = [jnp.where(hh, score[8 * g:8 * (g + 1), :], p) for g, (p, hh) in enumerate(zip(picked, hits))]
        vals = [jnp.where(hh, neg_inf, v) for v, hh in zip(vals, hits)]
    total = functools.reduce(jnp.add, [jnp.sum(p, axis=0, keepdims=True) for p in picked])
    gate_t = jnp.concatenate([p / total * ROUTED_SCALE for p in picked]
                             + [jnp.zeros((128 - N_EXPERTS, n), F32)], axis=0)
    gate_ref[...] = gate_t.T
    idx_ref[...] = jnp.concatenate(order, axis=0).astype(jnp.int32)


def _router(logits_t, bias_col):
    return pl.pallas_call(
        _router_kernel,
        grid=(T_ALL // TB_MOE,),
        in_specs=[pl.BlockSpec((N_EXPERTS, TB_MOE), lambda i: (0, i)),
                  pl.BlockSpec((N_EXPERTS, 1), lambda i: (0, 0))],
        out_specs=[pl.BlockSpec((TB_MOE, 128), lambda i: (i, 0)),
                   pl.BlockSpec((TOP_K, TB_MOE), lambda i: (0, i))],
        out_shape=[jax.ShapeDtypeStruct((T_ALL, 128), F32),
                   jax.ShapeDtypeStruct((TOP_K, T_ALL), jnp.int32)],
        compiler_params=_cparams(("arbitrary",)),
        name="router",
    )(logits_t, bias_col)


def _plan_kernel(idx_ref, slot_ref, cnt_ref, off_ref, pos_sc):
    n_tiles = T_HALF // 128
    eid = lax.broadcasted_iota(jnp.int32, (N_EXPERTS, 128), 0)
    tri = (lax.broadcasted_iota(jnp.int32, (128, 128), 0)
           <= lax.broadcasted_iota(jnp.int32, (128, 128), 1)).astype(BF16)
    carry = jnp.zeros((N_EXPERTS, 1), F32)
    for j in range(n_tiles):
        it = idx_ref[:, 128 * j:128 * (j + 1)]
        sel = jnp.zeros((N_EXPERTS, 128), F32)
        for k in range(TOP_K):
            sel = jnp.where(it[k:k + 1, :] == eid, 1.0, sel)
        inc = _dot(sel.astype(BF16), tri) + carry
        carry = inc[:, 127:128]
        pos_sc[:, 128 * j:128 * (j + 1)] = inc - 1.0
    count = jnp.broadcast_to(carry, (N_EXPERTS, 128))
    padded = jnp.floor((count + (MT - 1.0)) * (1.0 / MT)) * MT
    before = (lax.broadcasted_iota(jnp.int32, (N_EXPERTS, N_EXPERTS), 1)
              < lax.broadcasted_iota(jnp.int32, (N_EXPERTS, N_EXPERTS), 0)).astype(BF16)
    hi, mid, lo = _split3(padded)
    off = _dot(before, hi) + _dot(before, mid) + _dot(before, lo)
    cnt_ref[...] = count.astype(jnp.int32)
    off_ref[...] = off.astype(jnp.int32)

    def slots(j):
        it = idx_ref[:, 128 * j:128 * (j + 1)]
        val = off + pos_sc[:, 128 * j:128 * (j + 1)]
        rows = [jnp.sum(jnp.where(it[k:k + 1, :] == eid, val, 0.0), axis=0, keepdims=True)
                for k in range(TOP_K)]
        return jnp.concatenate(rows, axis=0).astype(jnp.int32)

    for j in range(n_tiles // 2):
        slot_ref[:, 128 * j:128 * (j + 1)] = slots(j) | (slots(j + n_tiles // 2) << 16)


def _plan(idx8):
    return pl.pallas_call(
        _plan_kernel,
        grid=(2,),
        in_specs=[pl.BlockSpec((TOP_K, T_HALF), lambda h: (0, h))],
        out_specs=[pl.BlockSpec((None, TOP_K, T_HALF // 2), lambda h: (h, 0, 0)),
                   pl.BlockSpec((None, N_EXPERTS, 128), lambda h: (h, 0, 0)),
                   pl.BlockSpec((None, N_EXPERTS, 128), lambda h: (h, 0, 0))],
        out_shape=[jax.ShapeDtypeStruct((2, TOP_K, T_HALF // 2), jnp.int32),
                   jax.ShapeDtypeStruct((2, N_EXPERTS, 128), jnp.int32),
                   jax.ShapeDtypeStruct((2, N_EXPERTS, 128), jnp.int32)],
        scratch_shapes=[pltpu.VMEM((N_EXPERTS, T_HALF), F32)],
        compiler_params=_cparams(("arbitrary",)),
        name="plan",
    )(idx8)


def _prepare_half(slot_ref, cnt_ref, off_ref, tok_ref, tile_e, tile_first, elist, h):
    def per_expert(x, carry):
        j, q = carry
        n = cnt_ref[h * N_EXPERTS + x]
        first = off_ref[h * N_EXPERTS + x]
        tiles = (n + MT - 1) // MT
        elist[q] = x

        def mark(i, c):
            tile_e[j + i] = x
            tile_first[j + i] = jnp.where(i == 0, 1, 0)
            return c

        lax.fori_loop(0, tiles, mark, 0)

        @pl.when(tiles > 0)
        def _():
            last = first + (tiles - 1) * MT
            for u in range(MT):
                tok_ref[last + u] = T_HALF

        return j + tiles, q + jnp.where(tiles > 0, 1, 0)

    n_tiles, n_live = lax.fori_loop(0, N_EXPERTS, per_expert, (0, 0))
    for extra in range(2):
        tile_e[n_tiles + extra] = 0
        tile_first[n_tiles + extra] = 0
        for u in range(MT):
            tok_ref[(n_tiles + extra) * MT + u] = T_HALF
    return n_tiles, n_live


def _invert_slots(slot_ref, tok_ref, acc_v, h):
    half = T_HALF // 2
    trips = half // INV_UNROLL
    zero_rows = 8 * T_HALF // (TOP_K * trips)
    assert zero_rows * TOP_K * trips == 8 * T_HALF and zero_rows % 8 == 0
    for k in range(TOP_K):
        def body(i, carry, k=k):
            words = [slot_ref[(h * TOP_K + k) * half + i * INV_UNROLL + u] for u in range(INV_UNROLL)]
            for u in range(INV_UNROLL):
                t = i * INV_UNROLL + u
                tok_ref[words[u] & 0xFFFF] = t
                tok_ref[lax.shift_right_logical(words[u], 16)] = t + half
            row = pl.multiple_of((k * trips + i) * zero_rows, 8)
            acc_v[pl.ds(row, zero_rows), :] = jnp.zeros((zero_rows, 128), F32)
            return carry

        lax.fori_loop(0, trips, body, 0)
    acc_v[pl.ds(8 * T_HALF, 8), :] = jnp.zeros((8, 128), F32)


def _moe_kernel(slot_ref, cnt_ref, off_ref, src_hbm, gate_hbm, wg_hbm, wu_hbm, wd_hbm, out_hbm,
                src_v, gate_v, acc_v, wg_l, wu_l, wd_l, wgu_b, wd_b, xbuf0, xbuf1, gbuf0, gbuf1,
                ybuf0, ybuf1, tok_ref, tile_e, tile_first, elist, live, sem, wsem):
    h = pl.program_id(0)
    xbuf, gbuf, ybuf = (xbuf0, xbuf1), (gbuf0, gbuf1), (ybuf0, ybuf1)

    src_cp = pltpu.make_async_copy(src_hbm.at[pl.ds(pl.multiple_of(h * (4 * T_HALF), 8), 4 * T_HALF)],
                                   src_v.at[pl.ds(0, 4 * T_HALF)], sem.at[0])
    gate_cp = pltpu.make_async_copy(gate_hbm.at[pl.ds(pl.multiple_of(h * T_HALF, 8), T_HALF)],
                                    gate_v.at[pl.ds(0, T_HALF)], sem.at[1])
    src_cp.start()
    gate_cp.start()

    def weight_copies(x, slot):
        return [pltpu.make_async_copy(w_hbm.at[x], w_l.at[slot], wsem.at[slot, i])
                for i, (w_hbm, w_l) in enumerate(((wg_hbm, wg_l), (wu_hbm, wu_l), (wd_hbm, wd_l)))]

    n_tiles, n_live = _prepare_half(slot_ref, cnt_ref, off_ref, tok_ref, tile_e, tile_first, elist, h)
    live[0] = 0
    live[1] = n_live

    for ahead in range(W_SLOTS - 1):
        @pl.when(ahead < n_live)
        def _(ahead=ahead):
            for cp in weight_copies(elist[ahead], ahead):
                cp.start()

    zero = jnp.zeros((8, 128), F32)
    src_v[pl.ds(4 * T_HALF, 8), :] = pltpu.pack_elementwise([zero, zero], packed_dtype=BF16)
    gate_v[pl.ds(T_HALF, 8), :] = zero
    ybuf0[...] = jnp.zeros(ybuf0.shape, F32)
    ybuf1[...] = jnp.zeros(ybuf1.shape, F32)
    _invert_slots(slot_ref, tok_ref, acc_v, h)
    src_cp.wait()
    gate_cp.wait()

    def switch_expert():
        q = live[0]
        slot = q % W_SLOTS
        for cp in weight_copies(elist[q], slot):
            cp.wait()
        wgu_b[:, 0:D_EXPERT] = wg_l[slot].astype(BF16)
        wgu_b[:, D_EXPERT:] = wu_l[slot].astype(BF16)
        wd_b[...] = wd_l[slot].astype(BF16)
        live[0] = q + 1
        nxt = q + W_SLOTS - 1

        @pl.when(nxt < live[1])
        def _():
            for cp in weight_copies(elist[nxt], nxt % W_SLOTS):
                cp.start()

    def gather(j, xb, gb, rows=(0, MT)):
        base = j * MT
        for m in range(*rows):
            t = tok_ref[base + m]
            xb[pl.ds(m, 4, stride=MT_STRIDE), :] = src_v[pl.ds(pl.multiple_of(t * 4, 4), 4), :]
            gb[m:m + 1, :] = gate_v[pl.ds(t, 1), :]

    def scatter(j, yb, rows=(0, MT)):
        base = j * MT
        for b in range(rows[0] // RMW_BATCH, rows[1] // RMW_BATCH):
            ms = [b * RMW_BATCH + u for u in range(RMW_BATCH)]
            targets = [pl.ds(pl.multiple_of(tok_ref[base + m] * 8, 8), 8) for m in ms]
            vals = [acc_v[r, :] + yb[pl.ds(m, 8, stride=MT_STRIDE), :] for r, m in zip(targets, ms)]
            for r, v in zip(targets, vals):
                acc_v[r, :] = v

    quarters = [(q * MT // 4, (q + 1) * MT // 4) for q in range(4)]

    def step(j, p):
        pl.when(tile_first[j] == 1)(switch_expert)
        xb, gb, yb = xbuf[p], gbuf[p], ybuf[p]
        nxt = (j + 1, xbuf[1 - p], gbuf[1 - p])
        prv = (jnp.maximum(j - 1, 0), ybuf[1 - p])
        lo, hi = [], []
        for c in range(4):
            words = xb[MT_STRIDE * c:MT_STRIDE * c + MT, :]
            unpack = functools.partial(pltpu.unpack_elementwise, words, packed_dtype=BF16, unpacked_dtype=F32)
            lo.append(unpack(index=0).astype(BF16))
            hi.append(unpack(index=1).astype(BF16))
        x = jnp.concatenate(lo + hi, axis=1)
        gather(*nxt, rows=quarters[0])
        h_gate = _dot(x, wgu_b[:, 0:D_EXPERT])
        gather(*nxt, rows=quarters[1])
        h_up = _dot(x, wgu_b[:, D_EXPERT:])
        gather(*nxt, rows=quarters[2])
        g = gb[...]
        g_hi = g.astype(BF16)
        g_lo = (g - g_hi.astype(F32)).astype(BF16)
        pick = (lax.broadcasted_iota(jnp.int32, (128, D_EXPERT), 0) == tile_e[j]).astype(BF16)
        g_col = _dot(g_hi, pick) + _dot(g_lo, pick)
        a = (_silu(h_gate) * h_up * g_col).astype(BF16)
        gather(*nxt, rows=quarters[3])
        scatter(*prv, rows=quarters[0])
        y_lo = _dot(a, wd_b[:, 0:D_MODEL // 2])
        scatter(*prv, rows=quarters[1])
        y_hi = _dot(a, wd_b[:, D_MODEL // 2:])
        scatter(*prv, rows=quarters[2])
        for c in range(4):
            yb[MT_STRIDE * c:MT_STRIDE * c + MT, :] = y_lo[:, 128 * c:128 * (c + 1)]
            yb[MT_STRIDE * (c + 4):MT_STRIDE * (c + 4) + MT, :] = y_hi[:, 128 * c:128 * (c + 1)]
        scatter(*prv, rows=quarters[3])

    gather(0, xbuf[0], gbuf[0])
    n_pairs = (n_tiles + 1) // 2

    def pair(i, carry):
        step(2 * i, 0)
        step(2 * i + 1, 1)
        return carry

    lax.fori_loop(0, n_pairs, pair, 0)
    scatter(jnp.maximum(2 * n_pairs - 1, 0), ybuf[1])

    out_cp = pltpu.make_async_copy(acc_v.at[pl.ds(0, 8 * T_HALF)],
                                   out_hbm.at[pl.ds(pl.multiple_of(h * (8 * T_HALF), 8), 8 * T_HALF)],
                                   sem.at[2])
    out_cp.start()
    out_cp.wait()


def _moe(slot_words, cnt, off, src, gate, wg, wu, wd):
    any_spec = pl.BlockSpec(memory_space=pl.ANY)
    tile_buf = lambda rows, dt: pltpu.VMEM((rows * MT_STRIDE, 128), dt)
    return pl.pallas_call(
        _moe_kernel,
        grid_spec=pltpu.PrefetchScalarGridSpec(
            num_scalar_prefetch=3,
            grid=(2,),
            in_specs=[any_spec] * 5,
            out_specs=any_spec,
            scratch_shapes=[pltpu.VMEM((4 * T_HALF + 8, 128), jnp.uint32),
                            pltpu.VMEM((T_HALF + 8, 128), F32),
                            pltpu.VMEM((8 * T_HALF + 8, 128), F32),
                            pltpu.VMEM((W_SLOTS, D_MODEL, D_EXPERT), F32),
                            pltpu.VMEM((W_SLOTS, D_MODEL, D_EXPERT), F32),
                            pltpu.VMEM((W_SLOTS, D_EXPERT, D_MODEL), F32),
                            pltpu.VMEM((D_MODEL, 2 * D_EXPERT), BF16),
                            pltpu.VMEM((D_EXPERT, D_MODEL), BF16),
                            tile_buf(4, jnp.uint32), tile_buf(4, jnp.uint32),
                            pltpu.VMEM((MT, 128), F32), pltpu.VMEM((MT, 128), F32),
                            tile_buf(8, F32), tile_buf(8, F32),
                            pltpu.SMEM((SLOT_CAP,), jnp.int32),
                            pltpu.SMEM((NT_MAX + 2,), jnp.int32),
                            pltpu.SMEM((NT_MAX + 2,), jnp.int32),
                            pltpu.SMEM((N_EXPERTS,), jnp.int32),
                            pltpu.SMEM((2,), jnp.int32),
                            pltpu.SemaphoreType.DMA((3,)),
                            pltpu.SemaphoreType.DMA((W_SLOTS, 3))]),
        out_shape=jax.ShapeDtypeStruct((8 * T_ALL, 128), F32),
        compiler_params=_cparams(("arbitrary",)),
        name="moe",
    )(slot_words, cnt, off, src, gate, wg, wu, wd)


def _finalize_kernel(acc_ref, base_ref, mod_ref, oc_ref, ol_ref):
    def write(o_ref):
        for c in range(8):
            cs = slice(128 * c, 128 * (c + 1))
            o_ref[:, cs] = base_ref[:, cs] + mod_ref[5:6, cs] * acc_ref[pl.ds(c, TBP, stride=8), :]

    is_ctx = pl.program_id(0) < NP_CTX
    pl.when(is_ctx)(lambda: write(oc_ref))
    pl.when(jnp.logical_not(is_ctx))(lambda: write(ol_ref))


def _finalize(acc, base, mod3):
    return pl.pallas_call(
        _finalize_kernel,
        grid=(T_ALL // TBP,),
        in_specs=[pl.BlockSpec((8 * TBP, 128), lambda i: (i, 0)),
                  pl.BlockSpec((TBP, D_MODEL), lambda i: (i, 0)),
                  pl.BlockSpec((None, 6, D_MODEL), lambda i: (_mod_row(i), 0, 0))],
        out_specs=_split_specs(D_MODEL),
        out_shape=[jax.ShapeDtypeStruct((T_CTX, D_MODEL), F32), jax.ShapeDtypeStruct((T_LAT, D_MODEL), F32)],
        compiler_params=_cparams(("arbitrary",)),
        name="finalize",
    )(acc, base, mod3)


def _head_indicators(width):
    head = jnp.arange(width) // DH_ATT
    ind = (head[:, None] == jnp.arange(128)[None, :]).astype(BF16)
    return ind, ind.T


def kernel(x_prompt, x_sample, cache_attn_k, cache_attn_v, state_mlstm_c, state_mlstm_n, state_mlstm_m, c, c_ctx, w_mod, b_mod, norm1_w, norm2_w, w_in, q_norm_w, k_norm_w, b_gates, m_norm_w, w_out, router_w, router_bias, w_gate, w_up, w_down, ws_gate, ws_up, ws_down):
    x = (x_prompt.reshape(T_CTX, D_MODEL), x_sample.reshape(T_LAT, D_MODEL))
    mod3 = _modulation(c, c_ctx, w_mod[0], b_mod[0])

    w_main = w_in[0, :, :P_MAIN].astype(BF16)
    w_gates_t = w_in[0, :, P_MAIN:].T.astype(BF16)
    qw = jnp.tile(q_norm_w, (1, N_HEADS_ATT))
    kw = jnp.tile(k_norm_w, (1, N_KV_HEADS))
    inds = _head_indicators(512) + _head_indicators(128)
    qn, kn, va, qm, km, vm, om, gt = _inproj(*x, mod3, norm1_w, w_main, w_gates_t, qw, kw, inds, _rope_tables())

    att_c = _attention(qn, kn, va, None, n_batch=BATCH, seq=SEQ, first_row=0)
    att_l = _attention(qn, kn, va, (cache_attn_k, cache_attn_v), n_batch=DEC_BATCH, seq=DEC_SEQ,
                       first_row=T_CTX)

    g3 = gt.reshape(16, T_ALL // CHUNK, CHUNK).transpose(1, 0, 2)
    bg = b_gates.reshape(16, 1)
    hm_c, c_new, n_new, m_new = _mlstm(qm, km, vm, g3, bg, om, m_norm_w, None,
                                       n_batch=BATCH, seq=SEQ, first_block=0)
    m0 = jnp.broadcast_to(state_mlstm_m.reshape(DEC_BATCH, 8, 1), (DEC_BATCH, 8, CHUNK))
    hm_l = _mlstm(qm, km, vm, g3, bg, om, m_norm_w, (state_mlstm_c, state_mlstm_n, m0),
                  n_batch=DEC_BATCH, seq=DEC_SEQ, first_block=N_CTX_BLOCKS)

    rw_t = router_w[0].T
    rw_hi = rw_t.astype(BF16)
    rw_lo = (rw_t - rw_hi.astype(F32)).astype(BF16)
    base, hn2, logits_t = _outproj((att_c, att_l), (hm_c, hm_l), x, mod3, norm2_w, w_out[0].astype(BF16),
                                   rw_hi, rw_lo, ws_gate[0].astype(BF16), ws_up[0].astype(BF16),
                                   ws_down[0].astype(BF16))
    gate, idx8 = _router(logits_t, router_bias.reshape(N_EXPERTS, 1))
    slot_words, cnt, off = _plan(idx8)
    acc = _moe(slot_words.reshape(TOP_K * T_HALF), cnt[:, :, 0].reshape(2 * N_EXPERTS),
               off[:, :, 0].reshape(2 * N_EXPERTS), hn2, gate, w_gate[0], w_up[0], w_down[0])
    out_c, out_l = _finalize(acc, base, mod3)

    y_prompt = out_c.reshape(BATCH, SEQ, D_MODEL)
    y_sample = out_l.reshape(DEC_BATCH, DEC_SEQ, D_MODEL)
    to_cache = lambda a: a[:T_CTX].reshape(BATCH, SEQ, N_KV_HEADS, DH_ATT).transpose(0, 2, 1, 3)[:, None]
    new_k = to_cache(kn)
    new_v = to_cache(va)
    new_m = m_new[:, :, 0].reshape(BATCH, 1, 2, N_HEADS_M)
    return (y_prompt, y_sample, new_k, new_v, c_new[:, None], n_new[:, None], new_m)
```

```python
import functools

import jax
import jax.numpy as jnp
from jax import lax
from jax.experimental import pallas as pl
from jax.experimental.pallas import tpu as pltpu

F32 = jnp.float32
BF16 = jnp.bfloat16

D_MODEL = 1024
BATCH = 32
SEQ = 256
DEC_BATCH = 2
DEC_SEQ = 1024
PAST_LEN = 256
GRID_W = 64
N_HEADS_ATT = 8
N_KV_HEADS = 2
DH_ATT = 64
D_ATT = 512
ROPE_THETA = 10000.0
N_HEADS_M = 4
DH_M = 128
D_M = 512
CHUNK = 128
N_EXPERTS = 64
TOP_K = 8
N_GROUPS = 8
TOPK_GROUPS = 4
D_EXPERT = 256
ROUTED_SCALE = 2.5
EPS = 1e-6
NEG_INIT = -1e30

T_CTX = BATCH * SEQ
T_LAT = DEC_BATCH * DEC_SEQ
T_ALL = T_CTX + T_LAT
TB = 256
N_CTX_BLOCKS = T_CTX // TB
TBP = 512
NP_CTX = T_CTX // TBP
NP_LAT_PER_BATCH = DEC_SEQ // TBP
TB_MOE = 1024
T_HALF = T_ALL // 2
MT = 256
MT_STRIDE = MT + 8
NT_MAX = TOP_K * T_HALF // MT + N_EXPERTS
SLOT_CAP = (NT_MAX + 2) * MT
RMW_BATCH = 16
INV_UNROLL = 16
W_SLOTS = 3
assert SLOT_CAP < 2 ** 16 and (T_HALF // 2) % INV_UNROLL == 0 and MT % RMW_BATCH == 0
P_MAIN = 2816
VMEM_LIMIT = 56 * 1024 * 1024

_NT = (((1,), (1,)), ((), ()))
_TN = (((0,), (0,)), ((), ()))


def _cparams(sem):
    return pltpu.CompilerParams(dimension_semantics=sem, vmem_limit_bytes=VMEM_LIMIT)


def _split3(x):
    hi = x.astype(BF16)
    r1 = x - hi.astype(F32)
    mid = r1.astype(BF16)
    lo = (r1 - mid.astype(F32)).astype(BF16)
    return hi, mid, lo


def _dot(a, b):
    return jnp.dot(a, b, preferred_element_type=F32)


def _dot3(x, m_bf16):
    hi, mid, lo = _split3(x)
    return _dot(hi, m_bf16) + _dot(mid, m_bf16) + _dot(lo, m_bf16)


def _dot2(x, m_bf16):
    hi = x.astype(BF16)
    lo = (x - hi.astype(F32)).astype(BF16)
    return _dot(hi, m_bf16) + _dot(lo, m_bf16)


def _silu(x):
    return x * jax.nn.sigmoid(x)


def _mod_row(i):
    return jnp.where(i < NP_CTX, 0, 1 + (i - NP_CTX) // NP_LAT_PER_BATCH)


def _mod_kernel(ct_ref, w_ref, b_ref, o_ref):
    s = _silu(ct_ref[...])
    w = w_ref[...]
    rows = [jnp.sum(w * s[:, r:r + 1], axis=0, keepdims=True) for r in range(3)]
    rows.append(jnp.zeros((5, w.shape[1]), F32))
    o_ref[...] = jnp.concatenate(rows, axis=0) + b_ref[...]


def _modulation(c, c_ctx, w_mod, b_mod):
    cvec = jnp.concatenate([c_ctx[None, :], c, jnp.zeros((5, D_MODEL), F32)], axis=0)
    nb = 1024
    out = pl.pallas_call(
        _mod_kernel,
        grid=(6 * D_MODEL // nb,),
        in_specs=[pl.BlockSpec((D_MODEL, 8), lambda j: (0, 0)),
                  pl.BlockSpec((D_MODEL, nb), lambda j: (0, j)),
                  pl.BlockSpec((1, nb), lambda j: (0, j))],
        out_specs=pl.BlockSpec((8, nb), lambda j: (0, j)),
        out_shape=jax.ShapeDtypeStruct((8, 6 * D_MODEL), F32),
        compiler_params=_cparams(("arbitrary",)),
        name="modulation",
    )(cvec.T, w_mod, b_mod[None, :])
    return out.reshape(8, 6, D_MODEL)


def _ctx_or_lat(ctx_ref, lat_ref):
    return jnp.where(pl.program_id(0) < NP_CTX, ctx_ref[...], lat_ref[...])


def _split_specs(width):
    return [pl.BlockSpec((TBP, width), lambda i: (jnp.minimum(i, NP_CTX - 1), 0)),
            pl.BlockSpec((TBP, width), lambda i: (jnp.maximum(i - NP_CTX, 0), 0))]


def _head_norm(x, ind, ind_t, w_row):
    ss = _dot2(x * x, ind)
    inv = lax.rsqrt(ss * (1.0 / DH_ATT) + EPS)
    return x * _dot2(inv, ind_t) * w_row


def _rope(x, cos, sin_signed):
    lane = lax.broadcasted_iota(jnp.int32, x.shape, 1)
    partner = jnp.where((lane % 32) < 16, pltpu.roll(x, 128 - 16, 1), pltpu.roll(x, 16, 1))
    return x * cos + partner * sin_signed


def _inproj_kernel(xc_ref, xl_ref, mod_ref, nw_ref, w_ref, wgt_ref, qw_ref, kw_ref, iq_ref, iqt_ref,
                   ik_ref, ikt_ref, cos_ref, sin_ref,
                   qn_ref, kn_ref, va_ref, qm_ref, km_ref, vm_ref, om_ref, gt_ref):
    x = _ctx_or_lat(xc_ref, xl_ref)
    y = x * lax.rsqrt(jnp.mean(x * x, axis=-1, keepdims=True) + EPS) * nw_ref[...]
    hn = y * (1.0 + mod_ref[1:2, :]) + mod_ref[0:1, :]
    hb = hn.astype(BF16)
    qn = _head_norm(_dot(hb, w_ref[:, 0:512]), iq_ref[...], iqt_ref[...], qw_ref[...])
    kn = _head_norm(_dot(hb, w_ref[:, 512:640]), ik_ref[...], ikt_ref[...], kw_ref[...])
    qn_ref[...] = qn.astype(BF16)
    kn_ref[...] = kn

    @pl.when(pl.program_id(0) >= NP_CTX)
    def _():
        cos, sin = cos_ref[...], sin_ref[...]
        for j in range(4):
            qn_ref[:, 128 * j:128 * (j + 1)] = _rope(qn[:, 128 * j:128 * (j + 1)], cos, sin).astype(BF16)
        kn_ref[...] = _rope(kn, cos, sin)

    va_ref[...] = _dot(hb, w_ref[:, 640:768])
    qm_ref[...] = _dot(hb, w_ref[:, 768:1280]).astype(BF16)
    km_ref[...] = (_dot(hb, w_ref[:, 1280:1792]) * (DH_M ** -0.5)).astype(BF16)
    vm_ref[...] = _dot(hb, w_ref[:, 1792:2304]).astype(BF16)
    om_ref[...] = _dot(hb, w_ref[:, 2304:2816])
    gt_ref[...] = lax.dot_general(wgt_ref[...], hb, _NT, preferred_element_type=F32)


def _inproj(x_ctx, x_lat, mod3, norm1_w, w_main, w_gates_t, qw, kw, inds, rope_tabs):
    tok = lambda w: pl.BlockSpec((TBP, w), lambda i: (i, 0))
    full = lambda a: pl.BlockSpec(a.shape, lambda i: (0,) * a.ndim)
    sd = lambda w, dt: jax.ShapeDtypeStruct((T_ALL, w), dt)
    rope_spec = pl.BlockSpec((TBP, 128), lambda i: (jnp.maximum(i - NP_CTX, 0) % NP_LAT_PER_BATCH, 0))
    return pl.pallas_call(
        _inproj_kernel,
        grid=(T_ALL // TBP,),
        in_specs=_split_specs(D_MODEL) + [
                  pl.BlockSpec((None, 6, D_MODEL), lambda i: (_mod_row(i), 0, 0)),
                  full(norm1_w), full(w_main), full(w_gates_t), full(qw), full(kw)]
                 + [full(a) for a in inds] + [rope_spec, rope_spec],
        out_specs=[tok(512), tok(128), tok(128), tok(512), tok(512), tok(512), tok(512),
                   pl.BlockSpec((16, TBP), lambda i: (0, i))],
        out_shape=[sd(512, BF16), sd(128, F32), sd(128, F32), sd(512, BF16), sd(512, BF16),
                   sd(512, BF16), sd(512, F32), jax.ShapeDtypeStruct((16, T_ALL), F32)],
        compiler_params=_cparams(("arbitrary",)),
        name="inproj",
    )(x_ctx, x_lat, mod3, norm1_w, w_main, w_gates_t, qw, kw, *inds, *rope_tabs)


def _rope_tables():
    t = jnp.arange(DEC_SEQ)
    pos = jnp.stack([t // GRID_W, t % GRID_W], axis=1).astype(F32)
    n_freq = DH_ATT // 4
    inv_freq = ROPE_THETA ** (-jnp.arange(n_freq, dtype=F32) / n_freq)
    ang = pos[:, :, None] * inv_freq
    cos, sin = jnp.cos(ang), jnp.sin(ang)
    cos_h = jnp.stack([cos, cos], axis=2).reshape(DEC_SEQ, DH_ATT)
    sin_h = jnp.stack([-sin, sin], axis=2).reshape(DEC_SEQ, DH_ATT)
    return jnp.tile(cos_h, (1, 2)), jnp.tile(sin_h, (1, 2))


def _attn_kernel(*refs, has_cache):
    if has_cache:
        q_ref, k_ref, v_ref, kc_ref, vc_ref, o_ref = refs
    else:
        q_ref, k_ref, v_ref, o_ref = refs
    q = q_ref[...] * jnp.asarray(DH_ATT ** -0.5, BF16)
    k = k_ref[...].astype(BF16)
    v = v_ref[...].astype(BF16)
    qb = q.shape[0]
    low_half = lax.broadcasted_iota(jnp.int32, (1, 128), 1) < DH_ATT
    for g in range(N_KV_HEADS):
        kg = k[:, DH_ATT * g:DH_ATT * (g + 1)]
        vg = v[:, DH_ATT * g:DH_ATT * (g + 1)]
        if has_cache:
            kg = jnp.concatenate([kg, kc_ref[g].astype(BF16)], axis=0)
            vg = jnp.concatenate([vg, vc_ref[g].astype(BF16)], axis=0)
        n_keys = kg.shape[0]
        zero = jnp.zeros_like(kg)
        k2 = jnp.concatenate([jnp.concatenate([kg, zero], axis=1), jnp.concatenate([zero, kg], axis=1)], axis=0)
        v2 = jnp.concatenate([jnp.concatenate([vg, zero], axis=1), jnp.concatenate([zero, vg], axis=1)], axis=0)
        ones2 = jnp.concatenate([jnp.broadcast_to(jnp.where(low_half, 1.0, 0.0), (n_keys, 128)),
                                 jnp.broadcast_to(jnp.where(low_half, 0.0, 1.0), (n_keys, 128))],
                                axis=0).astype(BF16)
        qp = jnp.concatenate([q[:, 256 * g:256 * g + 128], q[:, 256 * g + 128:256 * g + 256]], axis=0)
        s = lax.dot_general(qp, k2, _NT, preferred_element_type=F32)
        halves = [s[:, 0:n_keys], s[:, n_keys:]]
        p = jnp.concatenate([jnp.exp(sh - jnp.max(sh, axis=-1, keepdims=True)) for sh in halves],
                            axis=1).astype(BF16)
        ov = _dot(p, jnp.concatenate([v2, ones2], axis=1))
        o = ov[:, 0:128] / ov[:, 128:]
        o_ref[:, 256 * g:256 * g + 128] = o[0:qb, :].astype(BF16)
        o_ref[:, 256 * g + 128:256 * g + 256] = o[qb:, :].astype(BF16)


def _attention(qn, kn, v_all, cache, *, n_batch, seq, first_row):
    qblocks = seq // TB
    kv_spec = pl.BlockSpec((seq, 128), lambda b, i: (b + first_row // seq, 0))
    in_specs = [pl.BlockSpec((TB, 512), lambda b, i: (first_row // TB + b * qblocks + i, 0)), kv_spec, kv_spec]
    args = [qn, kn, v_all]
    if cache is not None:
        cspec = pl.BlockSpec((None, None, N_KV_HEADS, PAST_LEN, DH_ATT), lambda b, i: (b, 0, 0, 0, 0))
        in_specs += [cspec, cspec]
        args += list(cache)
    return pl.pallas_call(
        functools.partial(_attn_kernel, has_cache=cache is not None),
        grid=(n_batch, qblocks),
        in_specs=in_specs,
        out_specs=pl.BlockSpec((TB, 512), lambda b, i: (b * qblocks + i, 0)),
        out_shape=jax.ShapeDtypeStruct((n_batch * seq, 512), BF16),
        compiler_params=_cparams(("arbitrary", "arbitrary")),
        name="attention_lat" if cache is not None else "attention_ctx",
    )(*args)


def _log_sigmoid(x):
    return jnp.minimum(x, 0.0) - jnp.log1p(jnp.exp(-jnp.abs(x)))


def _col_bcast(cols, j):
    return jnp.broadcast_to(cols[:, j:j + 1], (CHUNK, CHUNK))


def _mlstm_kernel(*refs, has_state, n_chunks):
    n_in = 10 if has_state else 7
    n_out = 1 if has_state else 4
    q_ref, k_ref, v_ref, g_ref, bg_ref, om_ref, nw_ref = refs[:7]
    hm_ref = refs[n_in]
    scratch = refs[n_in + n_out:]
    st = scratch[0:8]
    ms = scratch[8:16]
    hdir = scratch[16:18]
    rows_sc, cmb_sc, bb_sc, kt_sc = scratch[18:22]
    neg_inf = F32(-jnp.inf)
    zeros112 = jnp.zeros((CHUNK - 16, CHUNK), F32)
    sub = lax.broadcasted_iota(jnp.int32, (CHUNK, CHUNK), 0)
    lan = lax.broadcasted_iota(jnp.int32, (CHUNK, CHUNK), 1)
    ones = jnp.ones((CHUNK, CHUNK), BF16)

    stats = []
    for c in range(n_chunks):
        pre = g_ref[c] + bg_ref[...]
        logf = _log_sigmoid(pre)
        for d in range(2):
            bcum = _dot3(logf, ((sub <= lan) if d == 0 else (sub >= lan)).astype(BF16))
            li4 = pre[4 * d:4 * d + 4, :]
            lf4 = logf[8 + 4 * d:12 + 4 * d, :]
            b4 = bcum[8 + 4 * d:12 + 4 * d, :]
            stats.append((c, d, li4 - b4, lf4, b4))
    for c, d, r4, lf4, b4 in stats:
        blast4 = b4[:, CHUNK - 1:CHUNK] if d == 0 else b4[:, 0:1]
        wlog4 = blast4 + r4
        wmax4 = jnp.max(wlog4, axis=-1, keepdims=True)
        full = lambda a: jnp.broadcast_to(a, (4, CHUNK))
        rows_sc[2 * c + d] = jnp.concatenate(
            [r4, wlog4, full(blast4), full(wmax4), jnp.zeros((16, CHUNK), F32)], axis=0)
    for c in range(n_chunks):
        for h in range(N_HEADS_M):
            kc = k_ref[CHUNK * c:CHUNK * (c + 1), DH_M * h:DH_M * (h + 1)]
            kt_sc[c * N_HEADS_M + h] = kc.astype(F32).T.astype(BF16)
    for c, d, r4, lf4, b4 in stats:
        within = (lan <= sub) if d == 0 else (lan >= sub)
        for h in range(N_HEADS_M):
            cummax = jnp.max(jnp.where(within, r4[h:h + 1, :], neg_inf), axis=-1, keepdims=True)
            cmb_sc[(2 * c + d) * N_HEADS_M + h] = jnp.broadcast_to(cummax, (CHUNK, CHUNK))
    for c, d, r4, lf4, b4 in stats:
        within = (lan <= sub) if d == 0 else (lan >= sub)
        for h in range(N_HEADS_M):
            bb_sc[(2 * c + d) * N_HEADS_M + h] = _dot2(jnp.where(within, lf4[h:h + 1, :], 0.0), ones)

    for d in range(2):
        if has_state:
            c0_ref, n0_ref, m0_ref = refs[7:10]
            ncols = jnp.concatenate([n0_ref[d], jnp.zeros((12, DH_M), F32), zeros112], axis=0).T
        for h in range(N_HEADS_M):
            idx = 4 * d + h
            if has_state:
                st[idx][:, 0:DH_M] = c0_ref[d, h]
                st[idx][:, DH_M:] = _col_bcast(ncols, h)
                ms[idx][0:1, :] = m0_ref[idx:idx + 1, :]
            else:
                st[idx][...] = jnp.zeros((DH_M, 2 * DH_M), F32)
                ms[idx][0:1, :] = jnp.full((1, CHUNK), NEG_INIT, F32)

    def chunk_step(i, carry):
        for d in range(2):
            c = i if d == 0 else n_chunks - 1 - i
            r0 = pl.multiple_of(c * CHUNK, CHUNK)
            rows = rows_sc[2 * c + d]
            mask = (lan <= sub) if d == 0 else (lan >= sub)
            heads = []
            for h in range(N_HEADS_M):
                idx = 4 * d + h
                hs = slice(DH_M * h, DH_M * (h + 1))
                qc = q_ref[pl.ds(r0, CHUNK), hs]
                kt = kt_sc[c * N_HEADS_M + h]
                state = st[idx][...]
                heads.append(dict(idx=idx, hs=hs, qc=qc, kt=kt, state=state,
                                  s_raw=_dot(qc, kt), qs=_dot(qc, state.astype(BF16))))
            for h, hd in enumerate(heads):
                mp = ms[hd["idx"]][0:1, :]
                m_b = jnp.maximum(mp, cmb_sc[(2 * c + d) * N_HEADS_M + h])
                s = hd["s_raw"] * jnp.exp(jnp.where(mask, rows[h:h + 1, :] - m_b, neg_inf))
                s_hi = s.astype(BF16)
                vc = v_ref[pl.ds(r0, CHUNK), hd["hs"]]
                v_ones = jnp.concatenate([vc, ones], axis=1)
                hd.update(mp=mp, m_b=m_b, v_ones=v_ones, sv=_dot(s_hi, v_ones),
                          s_lo_sum=_dot((s - s_hi.astype(F32)).astype(BF16), ones))
            for h, hd in enumerate(heads):
                mp, m_b = hd["mp"], hd["m_b"]
                gw_b = jnp.exp(mp - m_b)
                en_b = jnp.exp(-(bb_sc[(2 * c + d) * N_HEADS_M + h] + m_b))
                num = hd["sv"][:, 0:DH_M] + gw_b * hd["qs"][:, 0:DH_M]
                den = hd["sv"][:, DH_M:] + hd["s_lo_sum"] + gw_b * hd["qs"][:, DH_M:]
                hdir[d][pl.ds(r0, CHUNK), hd["hs"]] = num / jnp.maximum(jnp.abs(den), en_b)
            for h, hd in enumerate(heads):
                mp = hd["mp"]
                gend = rows[8 + h:9 + h, :] + mp
                mnew = jnp.maximum(gend, rows[12 + h:13 + h, :])
                w_row = jnp.exp(rows[4 + h:5 + h, :] - mnew)
                kw_t = (hd["kt"].astype(F32) * w_row).astype(BF16)
                dec = jnp.exp(gend - mnew)
                st[hd["idx"]][...] = (jnp.concatenate([dec, dec], axis=1) * hd["state"]
                                      + _dot(kw_t, hd["v_ones"]))
                ms[hd["idx"]][0:1, :] = mnew
        return carry

    lax.fori_loop(0, n_chunks, chunk_step, 0)

    for h in range(N_HEADS_M):
        hs = slice(DH_M * h, DH_M * (h + 1))
        hh = hdir[0][:, hs] + hdir[1][:, hs]
        y = hh * lax.rsqrt(jnp.mean(hh * hh, axis=-1, keepdims=True) + EPS) * nw_ref[:, hs]
        hm_ref[:, hs] = (jax.nn.sigmoid(om_ref[:, hs]) * y).astype(BF16)
    if not has_state:
        c_ref, n_ref, m_ref = refs[n_in + 1:n_in + 4]
        for d in range(2):
            for h in range(N_HEADS_M):
                idx = 4 * d + h
                c_ref[d, h] = st[idx][:, 0:DH_M]
                n_ref[d, h:h + 1, :] = st[idx][:, DH_M:].T[0:1, :]
                m_ref[idx:idx + 1, :] = ms[idx][0:1, :]


def _mlstm(qm, km, vm, g3, bg, om, nw, state, *, n_batch, seq, first_block):
    n_chunks = seq // CHUNK
    rb = seq // TB
    tok = lambda: pl.BlockSpec((seq, 512), lambda b: (b + first_block // rb, 0))
    full = lambda a: pl.BlockSpec(a.shape, lambda b: (0,) * a.ndim)
    in_specs = [tok(), tok(), tok(),
                pl.BlockSpec((n_chunks, 16, CHUNK), lambda b: (b + first_block // rb, 0, 0)),
                full(bg), tok(), full(nw)]
    args = [qm, km, vm, g3, bg, om, nw]
    hm_spec = pl.BlockSpec((seq, 512), lambda b: (b, 0))
    hm_shape = jax.ShapeDtypeStruct((n_batch * seq, 512), BF16)
    scratch = ([pltpu.VMEM((DH_M, 2 * DH_M), F32)] * 8 + [pltpu.VMEM((8, CHUNK), F32)] * 8
               + [pltpu.VMEM((seq, 512), F32)] * 2
               + [pltpu.VMEM((2 * n_chunks, 32, CHUNK), F32),
                  pltpu.VMEM((8 * n_chunks, CHUNK, CHUNK), F32),
                  pltpu.VMEM((8 * n_chunks, CHUNK, CHUNK), F32),
                  pltpu.VMEM((N_HEADS_M * n_chunks, DH_M, CHUNK), BF16)])
    if state is not None:
        c0, n0, m0 = state
        in_specs += [pl.BlockSpec((None, None, 2, N_HEADS_M, DH_M, DH_M), lambda b: (b, 0, 0, 0, 0, 0)),
                     pl.BlockSpec((None, None, 2, N_HEADS_M, DH_M), lambda b: (b, 0, 0, 0, 0)),
                     pl.BlockSpec((None, 8, CHUNK), lambda b: (b, 0, 0))]
        args += [c0, n0, m0]
        out_specs, out_shape = hm_spec, hm_shape
    else:
        out_specs = [hm_spec,
                     pl.BlockSpec((None, 2, N_HEADS_M, DH_M, DH_M), lambda b: (b, 0, 0, 0, 0)),
                     pl.BlockSpec((None, 2, N_HEADS_M, DH_M), lambda b: (b, 0, 0, 0)),
                     pl.BlockSpec((None, 8, CHUNK), lambda b: (b, 0, 0))]
        out_shape = [hm_shape,
                     jax.ShapeDtypeStruct((n_batch, 2, N_HEADS_M, DH_M, DH_M), F32),
                     jax.ShapeDtypeStruct((n_batch, 2, N_HEADS_M, DH_M), F32),
                     jax.ShapeDtypeStruct((n_batch, 8, CHUNK), F32)]
    return pl.pallas_call(
        functools.partial(_mlstm_kernel, has_state=state is not None, n_chunks=n_chunks),
        grid=(n_batch,),
        in_specs=in_specs,
        out_specs=out_specs,
        out_shape=out_shape,
        scratch_shapes=scratch,
        compiler_params=_cparams(("arbitrary",)),
        name="mlstm_lat" if state is not None else "mlstm_ctx",
    )(*args)


def _outproj_kernel(attc_ref, attl_ref, hmc_ref, hml_ref, xc_ref, xl_ref, mod_ref, nw_ref, wo_ref,
                    rwh_ref, rwl_ref, wsg_ref, wsu_ref, wsd_ref, base_ref, hn_ref, lt_ref):
    y = (_dot(_ctx_or_lat(attc_ref, attl_ref), wo_ref[0:D_ATT, :])
         + _dot(_ctx_or_lat(hmc_ref, hml_ref), wo_ref[D_ATT:, :]))
    x1 = _ctx_or_lat(xc_ref, xl_ref) + mod_ref[2:3, :] * y
    z = x1 * lax.rsqrt(jnp.mean(x1 * x1, axis=-1, keepdims=True) + EPS) * nw_ref[...]
    hn = z * (1.0 + mod_ref[4:5, :]) + mod_ref[3:4, :]
    hb = hn.astype(BF16)
    hl = (hn - hb.astype(F32)).astype(BF16)
    nt = lambda w, t: lax.dot_general(w, t, _NT, preferred_element_type=F32)
    lt_ref[...] = nt(rwh_ref[...], hb) + nt(rwl_ref[...], hb) + nt(rwh_ref[...], hl)
    a = _silu(_dot(hb, wsg_ref[...])) * _dot(hb, wsu_ref[...])
    shared = _dot(a.astype(BF16), wsd_ref[...])
    base_ref[...] = x1 + mod_ref[5:6, :] * shared
    packed = pltpu.pack_elementwise([hn[:, :512], hn[:, 512:]], packed_dtype=BF16)
    for c in range(4):
        hn_ref[pl.ds(c, TBP, stride=4), :] = packed[:, 128 * c:128 * (c + 1)]


def _outproj(att, hm, x, mod3, norm2_w, w_out, rw_hi, rw_lo, wsg, wsu, wsd):
    tok = lambda w: pl.BlockSpec((TBP, w), lambda i: (i, 0))
    full = lambda a: pl.BlockSpec(a.shape, lambda i: (0,) * a.ndim)
    return pl.pallas_call(
        _outproj_kernel,
        grid=(T_ALL // TBP,),
        in_specs=_split_specs(512) + _split_specs(512) + _split_specs(D_MODEL) + [
                  pl.BlockSpec((None, 6, D_MODEL), lambda i: (_mod_row(i), 0, 0)),
                  full(norm2_w), full(w_out), full(rw_hi), full(rw_lo), full(wsg), full(wsu), full(wsd)],
        out_specs=[tok(D_MODEL), pl.BlockSpec((4 * TBP, 128), lambda i: (i, 0)),
                   pl.BlockSpec((N_EXPERTS, TBP), lambda i: (0, i))],
        out_shape=[jax.ShapeDtypeStruct((T_ALL, D_MODEL), F32),
                   jax.ShapeDtypeStruct((4 * T_ALL, 128), jnp.uint32),
                   jax.ShapeDtypeStruct((N_EXPERTS, T_ALL), F32)],
        compiler_params=_cparams(("arbitrary",)),
        name="outproj",
    )(*att, *hm, *x, mod3, norm2_w, w_out, rw_hi, rw_lo, wsg, wsu, wsd)


def _first_max(vals, ids, limit):
    m = functools.reduce(jnp.maximum, [jnp.max(v, axis=0, keepdims=True) for v in vals])
    cand = [jnp.min(jnp.where(v == m, i, limit), axis=0, keepdims=True) for v, i in zip(vals, ids)]
    return m, functools.reduce(jnp.minimum, cand)


def _router_kernel(lt_ref, bias_ref, gate_ref, idx_ref):
    n = lt_ref.shape[1]
    score = jax.nn.sigmoid(lt_ref[...])
    biased = score + bias_ref[...]
    sub = lax.broadcasted_iota(jnp.int32, (8, n), 0).astype(F32)
    neg_inf = F32(-jnp.inf)
    slabs = [biased[8 * g:8 * (g + 1), :] for g in range(N_GROUPS)]
    gs = []
    for sl in slabs:
        m1, i1 = _first_max([sl], [sub], 8.0)
        m2 = jnp.max(jnp.where(sub == i1, neg_inf, sl), axis=0, keepdims=True)
        gs.append(m1 + m2)
    cur = jnp.concatenate(gs, axis=0)
    gsel = jnp.zeros((8, n), F32)
    for _ in range(TOPK_GROUPS):
        _, i = _first_max([cur], [sub], 8.0)
        hit = sub == i
        gsel = jnp.where(hit, 1.0, gsel)
        cur = jnp.where(hit, neg_inf, cur)
    vals = [jnp.where(gsel[g:g + 1, :] > 0.0, slabs[g], neg_inf) for g in range(N_GROUPS)]
    ids = [sub + 8.0 * g for g in range(N_GROUPS)]
    picked = [jnp.zeros((8, n), F32) for _ in range(N_GROUPS)]
    order = []
    for _ in range(TOP_K):
        _, i = _first_max(vals, ids, float(N_EXPERTS))
        order.append(i)
        hits = [idg == i for idg in ids]
        picked = [jnp.where(hh, score[8 * g:8 * (g + 1), :], p) for g, (p, hh) in enumerate(zip(picked, hits))]
        vals = [jnp.where(hh, neg_inf, v) for v, hh in zip(vals, hits)]
    total = functools.reduce(jnp.add, [jnp.sum(p, axis=0, keepdims=True) for p in picked])
    gate_t = jnp.concatenate([p / total * ROUTED_SCALE for p in picked]
                             + [jnp.zeros((128 - N_EXPERTS, n), F32)], axis=0)
    gate_ref[...] = gate_t.T
    idx_ref[...] = jnp.concatenate(order, axis=0).astype(jnp.int32)


def _router(logits_t, bias_col):
    return pl.pallas_call(
        _router_kernel,
        grid=(T_ALL // TB_MOE,),
        in_specs=[pl.BlockSpec((N_EXPERTS, TB_MOE), lambda i: (0, i)),
                  pl.BlockSpec((N_EXPERTS, 1), lambda i: (0, 0))],
        out_specs=[pl.BlockSpec((TB_MOE, 128), lambda i: (i, 0)),
                   pl.BlockSpec((TOP_K, TB_MOE), lambda i: (0, i))],
        out_shape=[jax.ShapeDtypeStruct((T_ALL, 128), F32),
                   jax.ShapeDtypeStruct((TOP_K, T_ALL), jnp.int32)],
        compiler_params=_cparams(("arbitrary",)),
        name="router",
    )(logits_t, bias_col)


def _plan_kernel(idx_ref, slot_ref, cnt_ref, off_ref, pos_sc):
    n_tiles = T_HALF // 128
    eid = lax.broadcasted_iota(jnp.int32, (N_EXPERTS, 128), 0)
    tri = (lax.broadcasted_iota(jnp.int32, (128, 128), 0)
           <= lax.broadcasted_iota(jnp.int32, (128, 128), 1)).astype(BF16)
    carry = jnp.zeros((N_EXPERTS, 1), F32)
    for j in range(n_tiles):
        it = idx_ref[:, 128 * j:128 * (j + 1)]
        sel = jnp.zeros((N_EXPERTS, 128), F32)
        for k in range(TOP_K):
            sel = jnp.where(it[k:k + 1, :] == eid, 1.0, sel)
        inc = _dot(sel.astype(BF16), tri) + carry
        carry = inc[:, 127:128]
        pos_sc[:, 128 * j:128 * (j + 1)] = inc - 1.0
    count = jnp.broadcast_to(carry, (N_EXPERTS, 128))
    padded = jnp.floor((count + (MT - 1.0)) * (1.0 / MT)) * MT
    before = (lax.broadcasted_iota(jnp.int32, (N_EXPERTS, N_EXPERTS), 1)
              < lax.broadcasted_iota(jnp.int32, (N_EXPERTS, N_EXPERTS), 0)).astype(BF16)
    hi, mid, lo = _split3(padded)
    off = _dot(before, hi) + _dot(before, mid) + _dot(before, lo)
    cnt_ref[...] = count.astype(jnp.int32)
    off_ref[...] = off.astype(jnp.int32)

    def slots(j):
        it = idx_ref[:, 128 * j:128 * (j + 1)]
        val = off + pos_sc[:, 128 * j:128 * (j + 1)]
        rows = [jnp.sum(jnp.where(it[k:k + 1, :] == eid, val, 0.0), axis=0, keepdims=True)
                for k in range(TOP_K)]
        return jnp.concatenate(rows, axis=0).astype(jnp.int32)

    for j in range(n_tiles // 2):
        slot_ref[:, 128 * j:128 * (j + 1)] = slots(j) | (slots(j + n_tiles // 2) << 16)


def _plan(idx8):
    return pl.pallas_call(
        _plan_kernel,
        grid=(2,),
        in_specs=[pl.BlockSpec((TOP_K, T_HALF), lambda h: (0, h))],
        out_specs=[pl.BlockSpec((None, TOP_K, T_HALF // 2), lambda h: (h, 0, 0)),
                   pl.BlockSpec((None, N_EXPERTS, 128), lambda h: (h, 0, 0)),
                   pl.BlockSpec((None, N_EXPERTS, 128), lambda h: (h, 0, 0))],
        out_shape=[jax.ShapeDtypeStruct((2, TOP_K, T_HALF // 2), jnp.int32),
                   jax.ShapeDtypeStruct((2, N_EXPERTS, 128), jnp.int32),
                   jax.ShapeDtypeStruct((2, N_EXPERTS, 128), jnp.int32)],
        scratch_shapes=[pltpu.VMEM((N_EXPERTS, T_HALF), F32)],
        compiler_params=_cparams(("arbitrary",)),
        name="plan",
    )(idx8)


def _prepare_half(slot_ref, cnt_ref, off_ref, tok_ref, tile_e, tile_first, elist, h):
    def per_expert(x, carry):
        j, q = carry
        n = cnt_ref[h * N_EXPERTS + x]
        first = off_ref[h * N_EXPERTS + x]
        tiles = (n + MT - 1) // MT
        elist[q] = x

        def mark(i, c):
            tile_e[j + i] = x
            tile_first[j + i] = jnp.where(i == 0, 1, 0)
            return c

        lax.fori_loop(0, tiles, mark, 0)

        @pl.when(tiles > 0)
        def _():
            last = first + (tiles - 1) * MT
            for u in range(MT):
                tok_ref[last + u] = T_HALF

        return j + tiles, q + jnp.where(tiles > 0, 1, 0)

    n_tiles, n_live = lax.fori_loop(0, N_EXPERTS, per_expert, (0, 0))
    for extra in range(2):
        tile_e[n_tiles + extra] = 0
        tile_first[n_tiles + extra] = 0
        for u in range(MT):
            tok_ref[(n_tiles + extra) * MT + u] = T_HALF
    return n_tiles, n_live


def _invert_slots(slot_ref, tok_ref, acc_v, h):
    half = T_HALF // 2
    trips = half // INV_UNROLL
    zero_rows = 8 * T_HALF // (TOP_K * trips)
    assert zero_rows * TOP_K * trips == 8 * T_HALF and zero_rows % 8 == 0
    for k in range(TOP_K):
        def body(i, carry, k=k):
            words = [slot_ref[(h * TOP_K + k) * half + i * INV_UNROLL + u] for u in range(INV_UNROLL)]
            for u in range(INV_UNROLL):
                t = i * INV_UNROLL + u
                tok_ref[words[u] & 0xFFFF] = t
                tok_ref[lax.shift_right_logical(words[u], 16)] = t + half
            row = pl.multiple_of((k * trips + i) * zero_rows, 8)
            acc_v[pl.ds(row, zero_rows), :] = jnp.zeros((zero_rows, 128), F32)
            return carry

        lax.fori_loop(0, trips, body, 0)
    acc_v[pl.ds(8 * T_HALF, 8), :] = jnp.zeros((8, 128), F32)


def _moe_kernel(slot_ref, cnt_ref, off_ref, src_hbm, gate_hbm, wg_hbm, wu_hbm, wd_hbm, out_hbm,
                src_v, gate_v, acc_v, wg_l, wu_l, wd_l, wgu_b, wd_b, xbuf0, xbuf1, gbuf0, gbuf1,
                ybuf0, ybuf1, tok_ref, tile_e, tile_first, elist, live, sem, wsem):
    h = pl.program_id(0)
    xbuf, gbuf, ybuf = (xbuf0, xbuf1), (gbuf0, gbuf1), (ybuf0, ybuf1)

    src_cp = pltpu.make_async_copy(src_hbm.at[pl.ds(pl.multiple_of(h * (4 * T_HALF), 8), 4 * T_HALF)],
                                   src_v.at[pl.ds(0, 4 * T_HALF)], sem.at[0])
    gate_cp = pltpu.make_async_copy(gate_hbm.at[pl.ds(pl.multiple_of(h * T_HALF, 8), T_HALF)],
                                    gate_v.at[pl.ds(0, T_HALF)], sem.at[1])
    src_cp.start()
    gate_cp.start()

    def weight_copies(x, slot):
        return [pltpu.make_async_copy(w_hbm.at[x], w_l.at[slot], wsem.at[slot, i])
                for i, (w_hbm, w_l) in enumerate(((wg_hbm, wg_l), (wu_hbm, wu_l), (wd_hbm, wd_l)))]

    n_tiles, n_live = _prepare_half(slot_ref, cnt_ref, off_ref, tok_ref, tile_e, tile_first, elist, h)
    live[0] = 0
    live[1] = n_live

    for ahead in range(W_SLOTS - 1):
        @pl.when(ahead < n_live)
        def _(ahead=ahead):
            for cp in weight_copies(elist[ahead], ahead):
                cp.start()

    zero = jnp.zeros((8, 128), F32)
    src_v[pl.ds(4 * T_HALF, 8), :] = pltpu.pack_elementwise([zero, zero], packed_dtype=BF16)
    gate_v[pl.ds(T_HALF, 8), :] = zero
    ybuf0[...] = jnp.zeros(ybuf0.shape, F32)
    ybuf1[...] = jnp.zeros(ybuf1.shape, F32)
    _invert_slots(slot_ref, tok_ref, acc_v, h)
    src_cp.wait()
    gate_cp.wait()

    def switch_expert():
        q = live[0]
        slot = q % W_SLOTS
        for cp in weight_copies(elist[q], slot):
            cp.wait()
        wgu_b[:, 0:D_EXPERT] = wg_l[slot].astype(BF16)
        wgu_b[:, D_EXPERT:] = wu_l[slot].astype(BF16)
        wd_b[...] = wd_l[slot].astype(BF16)
        live[0] = q + 1
        nxt = q + W_SLOTS - 1

        @pl.when(nxt < live[1])
        def _():
            for cp in weight_copies(elist[nxt], nxt % W_SLOTS):
                cp.start()

    def gather(j, xb, gb, rows=(0, MT)):
        base = j * MT
        for m in range(*rows):
            t = tok_ref[base + m]
            xb[pl.ds(m, 4, stride=MT_STRIDE), :] = src_v[pl.ds(pl.multiple_of(t * 4, 4), 4), :]
            gb[m:m + 1, :] = gate_v[pl.ds(t, 1), :]

    def scatter(j, yb, rows=(0, MT)):
        base = j * MT
        for b in range(rows[0] // RMW_BATCH, rows[1] // RMW_BATCH):
            ms = [b * RMW_BATCH + u for u in range(RMW_BATCH)]
            targets = [pl.ds(pl.multiple_of(tok_ref[base + m] * 8, 8), 8) for m in ms]
            vals = [acc_v[r, :] + yb[pl.ds(m, 8, stride=MT_STRIDE), :] for r, m in zip(targets, ms)]
            for r, v in zip(targets, vals):
                acc_v[r, :] = v

    quarters = [(q * MT // 4, (q + 1) * MT // 4) for q in range(4)]

    def step(j, p):
        pl.when(tile_first[j] == 1)(switch_expert)
        xb, gb, yb = xbuf[p], gbuf[p], ybuf[p]
        nxt = (j + 1, xbuf[1 - p], gbuf[1 - p])
        prv = (jnp.maximum(j - 1, 0), ybuf[1 - p])
        lo, hi = [], []
        for c in range(4):
            words = xb[MT_STRIDE * c:MT_STRIDE * c + MT, :]
            unpack = functools.partial(pltpu.unpack_elementwise, words, packed_dtype=BF16, unpacked_dtype=F32)
            lo.append(unpack(index=0).astype(BF16))
            hi.append(unpack(index=1).astype(BF16))
        x = jnp.concatenate(lo + hi, axis=1)
        gather(*nxt, rows=quarters[0])
        h_gate = _dot(x, wgu_b[:, 0:D_EXPERT])
        gather(*nxt, rows=quarters[1])
        h_up = _dot(x, wgu_b[:, D_EXPERT:])
        gather(*nxt, rows=quarters[2])
        g = gb[...]
        g_hi = g.astype(BF16)
        g_lo = (g - g_hi.astype(F32)).astype(BF16)
        pick = (lax.broadcasted_iota(jnp.int32, (128, D_EXPERT), 0) == tile_e[j]).astype(BF16)
        g_col = _dot(g_hi, pick) + _dot(g_lo, pick)
        a = (_silu(h_gate) * h_up * g_col).astype(BF16)
        gather(*nxt, rows=quarters[3])
        scatter(*prv, rows=quarters[0])
        y_lo = _dot(a, wd_b[:, 0:D_MODEL // 2])
        scatter(*prv, rows=quarters[1])
        y_hi = _dot(a, wd_b[:, D_MODEL // 2:])
        scatter(*prv, rows=quarters[2])
        for c in range(4):
            yb[MT_STRIDE * c:MT_STRIDE * c + MT, :] = y_lo[:, 128 * c:128 * (c + 1)]
            yb[MT_STRIDE * (c + 4):MT_STRIDE * (c + 4) + MT, :] = y_hi[:, 128 * c:128 * (c + 1)]
        scatter(*prv, rows=quarters[3])

    gather(0, xbuf[0], gbuf[0])
    n_pairs = (n_tiles + 1) // 2

    def pair(i, carry):
        step(2 * i, 0)
        step(2 * i + 1, 1)
        return carry

    lax.fori_loop(0, n_pairs, pair, 0)
    scatter(jnp.maximum(2 * n_pairs - 1, 0), ybuf[1])

    out_cp = pltpu.make_async_copy(acc_v.at[pl.ds(0, 8 * T_HALF)],
                                   out_hbm.at[pl.ds(pl.multiple_of(h * (8 * T_HALF), 8), 8 * T_HALF)],
                                   sem.at[2])
    out_cp.start()
    out_cp.wait()


def _moe(slot_words, cnt, off, src, gate, wg, wu, wd):
    any_spec = pl.BlockSpec(memory_space=pl.ANY)
    tile_buf = lambda rows, dt: pltpu.VMEM((rows * MT_STRIDE, 128), dt)
    return pl.pallas_call(
        _moe_kernel,
        grid_spec=pltpu.PrefetchScalarGridSpec(
            num_scalar_prefetch=3,
            grid=(2,),
            in_specs=[any_spec] * 5,
            out_specs=any_spec,
            scratch_shapes=[pltpu.VMEM((4 * T_HALF + 8, 128), jnp.uint32),
                            pltpu.VMEM((T_HALF + 8, 128), F32),
                            pltpu.VMEM((8 * T_HALF + 8, 128), F32),
                            pltpu.VMEM((W_SLOTS, D_MODEL, D_EXPERT), F32),
                            pltpu.VMEM((W_SLOTS, D_MODEL, D_EXPERT), F32),
                            pltpu.VMEM((W_SLOTS, D_EXPERT, D_MODEL), F32),
                            pltpu.VMEM((D_MODEL, 2 * D_EXPERT), BF16),
                            pltpu.VMEM((D_EXPERT, D_MODEL), BF16),
                            tile_buf(4, jnp.uint32), tile_buf(4, jnp.uint32),
                            pltpu.VMEM((MT, 128), F32), pltpu.VMEM((MT, 128), F32),
                            tile_buf(8, F32), tile_buf(8, F32),
                            pltpu.SMEM((SLOT_CAP,), jnp.int32),
                            pltpu.SMEM((NT_MAX + 2,), jnp.int32),
                            pltpu.SMEM((NT_MAX + 2,), jnp.int32),
                            pltpu.SMEM((N_EXPERTS,), jnp.int32),
                            pltpu.SMEM((2,), jnp.int32),
                            pltpu.SemaphoreType.DMA((3,)),
                            pltpu.SemaphoreType.DMA((W_SLOTS, 3))]),
        out_shape=jax.ShapeDtypeStruct((8 * T_ALL, 128), F32),
        compiler_params=_cparams(("arbitrary",)),
        name="moe",
    )(slot_words, cnt, off, src, gate, wg, wu, wd)


def _finalize_kernel(acc_ref, base_ref, mod_ref, oc_ref, ol_ref):
    def write(o_ref):
        for c in range(8):
            cs = slice(128 * c, 128 * (c + 1))
            o_ref[:, cs] = base_ref[:, cs] + mod_ref[5:6, cs] * acc_ref[pl.ds(c, TBP, stride=8), :]

    is_ctx = pl.program_id(0) < NP_CTX
    pl.when(is_ctx)(lambda: write(oc_ref))
    pl.when(jnp.logical_not(is_ctx))(lambda: write(ol_ref))


def _finalize(acc, base, mod3):
    return pl.pallas_call(
        _finalize_kernel,
        grid=(T_ALL // TBP,),
        in_specs=[pl.BlockSpec((8 * TBP, 128), lambda i: (i, 0)),
                  pl.BlockSpec((TBP, D_MODEL), lambda i: (i, 0)),
                  pl.BlockSpec((None, 6, D_MODEL), lambda i: (_mod_row(i), 0, 0))],
        out_specs=_split_specs(D_MODEL),
        out_shape=[jax.ShapeDtypeStruct((T_CTX, D_MODEL), F32), jax.ShapeDtypeStruct((T_LAT, D_MODEL), F32)],
        compiler_params=_cparams(("arbitrary",)),
        name="finalize",
    )(acc, base, mod3)


def _head_indicators(width):
    head = jnp.arange(width) // DH_ATT
    ind = (head[:, None] == jnp.arange(128)[None, :]).astype(BF16)
    return ind, ind.T


def kernel(x_prompt, x_sample, cache_attn_k, cache_attn_v, state_mlstm_c, state_mlstm_n, state_mlstm_m, c, c_ctx, w_mod, b_mod, norm1_w, norm2_w, w_in, q_norm_w, k_norm_w, b_gates, m_norm_w, w_out, router_w, router_bias, w_gate, w_up, w_down, ws_gate, ws_up, ws_down):
    x = (x_prompt.reshape(T_CTX, D_MODEL), x_sample.reshape(T_LAT, D_MODEL))
    mod3 = _modulation(c, c_ctx, w_mod[0], b_mod[0])

    w_main = w_in[0, :, :P_MAIN].astype(BF16)
    w_gates_t = w_in[0, :, P_MAIN:].T.astype(BF16)
    qw = jnp.tile(q_norm_w, (1, N_HEADS_ATT))
    kw = jnp.tile(k_norm_w, (1, N_KV_HEADS))
    inds = _head_indicators(512) + _head_indicators(128)
    qn, kn, va, qm, km, vm, om, gt = _inproj(*x, mod3, norm1_w, w_main, w_gates_t, qw, kw, inds, _rope_tables())

    att_c = _attention(qn, kn, va, None, n_batch=BATCH, seq=SEQ, first_row=0)
    att_l = _attention(qn, kn, va, (cache_attn_k, cache_attn_v), n_batch=DEC_BATCH, seq=DEC_SEQ,
                       first_row=T_CTX)

    g3 = gt.reshape(16, T_ALL // CHUNK, CHUNK).transpose(1, 0, 2)
    bg = b_gates.reshape(16, 1)
    hm_c, c_new, n_new, m_new = _mlstm(qm, km, vm, g3, bg, om, m_norm_w, None,
                                       n_batch=BATCH, seq=SEQ, first_block=0)
    m0 = jnp.broadcast_to(state_mlstm_m.reshape(DEC_BATCH, 8, 1), (DEC_BATCH, 8, CHUNK))
    hm_l = _mlstm(qm, km, vm, g3, bg, om, m_norm_w, (state_mlstm_c, state_mlstm_n, m0),
                  n_batch=DEC_BATCH, seq=DEC_SEQ, first_block=N_CTX_BLOCKS)

    rw_t = router_w[0].T
    rw_hi = rw_t.astype(BF16)
    rw_lo = (rw_t - rw_hi.astype(F32)).astype(BF16)
    base, hn2, logits_t = _outproj((att_c, att_l), (hm_c, hm_l), x, mod3, norm2_w, w_out[0].astype(BF16),
                                   rw_hi, rw_lo, ws_gate[0].astype(BF16), ws_up[0].astype(BF16),
                                   ws_down[0].astype(BF16))
    gate, idx8 = _router(logits_t, router_bias.reshape(N_EXPERTS, 1))
    slot_words, cnt, off = _plan(idx8)
    acc = _moe(slot_words.reshape(TOP_K * T_HALF), cnt[:, :, 0].reshape(2 * N_EXPERTS),
               off[:, :, 0].reshape(2 * N_EXPERTS), hn2, gate, w_gate[0], w_up[0], w_down[0])
    out_c, out_l = _finalize(acc, base, mod3)

    y_prompt = out_c.reshape(BATCH, SEQ, D_MODEL)
    y_sample = out_l.reshape(DEC_BATCH, DEC_SEQ, D_MODEL)
    to_cache = lambda a: a[:T_CTX].reshape(BATCH, SEQ, N_KV_HEADS, DH_ATT).transpose(0, 2, 1, 3)[:, None]
    new_k = to_cache(kn)
    new_v = to_cache(va)
    new_m = m_new[:, :, 0].reshape(BATCH, 1, 2, N_HEADS_M)
    return (y_prompt, y_sample, new_k, new_v, c_new[:, None], n_new[:, None], new_m)
```

```python
import functools

import jax
import jax.numpy as jnp
from jax import lax
from jax.experimental import pallas as pl
from jax.experimental.pallas import tpu as pltpu

F32 = jnp.float32
BF16 = jnp.bfloat16

D_MODEL = 1024
BATCH = 32
SEQ = 256
DEC_BATCH = 2
DEC_SEQ = 1024
PAST_LEN = 256
GRID_W = 64
N_HEADS_ATT = 8
N_KV_HEADS = 2
DH_ATT = 64
D_ATT = 512
ROPE_THETA = 10000.0
N_HEADS_M = 4
DH_M = 128
D_M = 512
CHUNK = 128
N_EXPERTS = 64
TOP_K = 8
N_GROUPS = 8
TOPK_GROUPS = 4
D_EXPERT = 256
ROUTED_SCALE = 2.5
EPS = 1e-6
NEG_INIT = -1e30

T_CTX = BATCH * SEQ
T_LAT = DEC_BATCH * DEC_SEQ
T_ALL = T_CTX + T_LAT
TB = 256
N_CTX_BLOCKS = T_CTX // TB
TBP = 512
NP_CTX = T_CTX // TBP
NP_LAT_PER_BATCH = DEC_SEQ // TBP
TB_MOE = 1024
T_HALF = T_ALL // 2
MT = 256
MT_STRIDE = MT + 8
NT_MAX = TOP_K * T_HALF // MT + N_EXPERTS
SLOT_CAP = (NT_MAX + 2) * MT
RMW_BATCH = 16
INV_UNROLL = 16
W_SLOTS = 2
FIN = 256
assert SLOT_CAP < 2 ** 16 and (T_HALF // 2) % INV_UNROLL == 0 and MT % RMW_BATCH == 0
P_MAIN = 2816
VMEM_LIMIT = 56 * 1024 * 1024

_NT = (((1,), (1,)), ((), ()))
_TN = (((0,), (0,)), ((), ()))


def _cparams(sem):
    return pltpu.CompilerParams(dimension_semantics=sem, vmem_limit_bytes=VMEM_LIMIT)


def _split3(x):
    hi = x.astype(BF16)
    r1 = x - hi.astype(F32)
    mid = r1.astype(BF16)
    lo = (r1 - mid.astype(F32)).astype(BF16)
    return hi, mid, lo


def _dot(a, b):
    return jnp.dot(a, b, preferred_element_type=F32)


def _dot3(x, m_bf16):
    hi, mid, lo = _split3(x)
    return _dot(hi, m_bf16) + _dot(mid, m_bf16) + _dot(lo, m_bf16)


def _dot2(x, m_bf16):
    hi = x.astype(BF16)
    lo = (x - hi.astype(F32)).astype(BF16)
    return _dot(hi, m_bf16) + _dot(lo, m_bf16)


def _silu(x):
    return x * jax.nn.sigmoid(x)


def _mod_row(i):
    return jnp.where(i < NP_CTX, 0, 1 + (i - NP_CTX) // NP_LAT_PER_BATCH)


def _mod_kernel(ct_ref, w_ref, b_ref, o_ref):
    s = _silu(ct_ref[...])
    w = w_ref[...]
    rows = [jnp.sum(w * s[:, r:r + 1], axis=0, keepdims=True) for r in range(3)]
    rows.append(jnp.zeros((5, w.shape[1]), F32))
    o_ref[...] = jnp.concatenate(rows, axis=0) + b_ref[...]


def _modulation(c, c_ctx, w_mod, b_mod):
    cvec = jnp.concatenate([c_ctx[None, :], c, jnp.zeros((5, D_MODEL), F32)], axis=0)
    nb = 1024
    out = pl.pallas_call(
        _mod_kernel,
        grid=(6 * D_MODEL // nb,),
        in_specs=[pl.BlockSpec((D_MODEL, 8), lambda j: (0, 0)),
                  pl.BlockSpec((D_MODEL, nb), lambda j: (0, j)),
                  pl.BlockSpec((1, nb), lambda j: (0, j))],
        out_specs=pl.BlockSpec((8, nb), lambda j: (0, j)),
        out_shape=jax.ShapeDtypeStruct((8, 6 * D_MODEL), F32),
        compiler_params=_cparams(("arbitrary",)),
        name="modulation",
    )(cvec.T, w_mod, b_mod[None, :])
    return out.reshape(8, 6, D_MODEL)


def _ctx_or_lat(ctx_ref, lat_ref):
    return jnp.where(pl.program_id(0) < NP_CTX, ctx_ref[...], lat_ref[...])


def _split_specs(width):
    return [pl.BlockSpec((TBP, width), lambda i: (jnp.minimum(i, NP_CTX - 1), 0)),
            pl.BlockSpec((TBP, width), lambda i: (jnp.maximum(i - NP_CTX, 0), 0))]


def _head_norm(x, ind, ind_t, w_row):
    ss = _dot2(x * x, ind)
    inv = lax.rsqrt(ss * (1.0 / DH_ATT) + EPS)
    return x * _dot2(inv, ind_t) * w_row


def _rope(x, cos, sin_signed):
    lane = lax.broadcasted_iota(jnp.int32, x.shape, 1)
    partner = jnp.where((lane % 32) < 16, pltpu.roll(x, 128 - 16, 1), pltpu.roll(x, 16, 1))
    return x * cos + partner * sin_signed


def _inproj_kernel(xc_ref, xl_ref, mod_ref, nw_ref, w_ref, wgt_ref, qw_ref, kw_ref, iq_ref, iqt_ref,
                   ik_ref, ikt_ref, cos_ref, sin_ref,
                   qn_ref, kn_ref, va_ref, qm_ref, km_ref, vm_ref, om_ref, gt_ref):
    x = _ctx_or_lat(xc_ref, xl_ref)
    y = x * lax.rsqrt(jnp.mean(x * x, axis=-1, keepdims=True) + EPS) * nw_ref[...]
    hn = y * (1.0 + mod_ref[1:2, :]) + mod_ref[0:1, :]
    hb = hn.astype(BF16)
    qn = _head_norm(_dot(hb, w_ref[:, 0:512]), iq_ref[...], iqt_ref[...], qw_ref[...])
    kn = _head_norm(_dot(hb, w_ref[:, 512:640]), ik_ref[...], ikt_ref[...], kw_ref[...])
    qn_ref[...] = qn.astype(BF16)
    kn_ref[...] = kn

    @pl.when(pl.program_id(0) >= NP_CTX)
    def _():
        cos, sin = cos_ref[...], sin_ref[...]
        for j in range(4):
            qn_ref[:, 128 * j:128 * (j + 1)] = _rope(qn[:, 128 * j:128 * (j + 1)], cos, sin).astype(BF16)
        kn_ref[...] = _rope(kn, cos, sin)

    va_ref[...] = _dot(hb, w_ref[:, 640:768])
    qm_ref[...] = _dot(hb, w_ref[:, 768:1280]).astype(BF16)
    km_ref[...] = (_dot(hb, w_ref[:, 1280:1792]) * (DH_M ** -0.5)).astype(BF16)
    vm_ref[...] = _dot(hb, w_ref[:, 1792:2304]).astype(BF16)
    om_ref[...] = _dot(hb, w_ref[:, 2304:2816])
    gt_ref[...] = lax.dot_general(wgt_ref[...], hb, _NT, preferred_element_type=F32)


def _inproj(x_ctx, x_lat, mod3, norm1_w, w_main, w_gates_t, qw, kw, inds, rope_tabs):
    tok = lambda w: pl.BlockSpec((TBP, w), lambda i: (i, 0))
    full = lambda a: pl.BlockSpec(a.shape, lambda i: (0,) * a.ndim)
    sd = lambda w, dt: jax.ShapeDtypeStruct((T_ALL, w), dt)
    rope_spec = pl.BlockSpec((TBP, 128), lambda i: (jnp.maximum(i - NP_CTX, 0) % NP_LAT_PER_BATCH, 0))
    return pl.pallas_call(
        _inproj_kernel,
        grid=(T_ALL // TBP,),
        in_specs=_split_specs(D_MODEL) + [
                  pl.BlockSpec((None, 6, D_MODEL), lambda i: (_mod_row(i), 0, 0)),
                  full(norm1_w), full(w_main), full(w_gates_t), full(qw), full(kw)]
                 + [full(a) for a in inds] + [rope_spec, rope_spec],
        out_specs=[tok(512), tok(128), tok(128), tok(512), tok(512), tok(512), tok(512),
                   pl.BlockSpec((16, TBP), lambda i: (0, i))],
        out_shape=[sd(512, BF16), sd(128, F32), sd(128, F32), sd(512, BF16), sd(512, BF16),
                   sd(512, BF16), sd(512, F32), jax.ShapeDtypeStruct((16, T_ALL), F32)],
        compiler_params=_cparams(("arbitrary",)),
        name="inproj",
    )(x_ctx, x_lat, mod3, norm1_w, w_main, w_gates_t, qw, kw, *inds, *rope_tabs)


def _rope_tables():
    t = jnp.arange(DEC_SEQ)
    pos = jnp.stack([t // GRID_W, t % GRID_W], axis=1).astype(F32)
    n_freq = DH_ATT // 4
    inv_freq = ROPE_THETA ** (-jnp.arange(n_freq, dtype=F32) / n_freq)
    ang = pos[:, :, None] * inv_freq
    cos, sin = jnp.cos(ang), jnp.sin(ang)
    cos_h = jnp.stack([cos, cos], axis=2).reshape(DEC_SEQ, DH_ATT)
    sin_h = jnp.stack([-sin, sin], axis=2).reshape(DEC_SEQ, DH_ATT)
    return jnp.tile(cos_h, (1, 2)), jnp.tile(sin_h, (1, 2))


def _attn_kernel(*refs, has_cache):
    if has_cache:
        q_ref, k_ref, v_ref, kc_ref, vc_ref, o_ref = refs
    else:
        q_ref, k_ref, v_ref, o_ref = refs
    q = q_ref[...] * jnp.asarray(DH_ATT ** -0.5, BF16)
    k = k_ref[...].astype(BF16)
    v = v_ref[...].astype(BF16)
    qb = q.shape[0]
    low_half = lax.broadcasted_iota(jnp.int32, (1, 128), 1) < DH_ATT
    for g in range(N_KV_HEADS):
        kg = k[:, DH_ATT * g:DH_ATT * (g + 1)]
        vg = v[:, DH_ATT * g:DH_ATT * (g + 1)]
        if has_cache:
            kg = jnp.concatenate([kg, kc_ref[g].astype(BF16)], axis=0)
            vg = jnp.concatenate([vg, vc_ref[g].astype(BF16)], axis=0)
        n_keys = kg.shape[0]
        zero = jnp.zeros_like(kg)
        k2 = jnp.concatenate([jnp.concatenate([kg, zero], axis=1), jnp.concatenate([zero, kg], axis=1)], axis=0)
        v2 = jnp.concatenate([jnp.concatenate([vg, zero], axis=1), jnp.concatenate([zero, vg], axis=1)], axis=0)
        ones2 = jnp.concatenate([jnp.broadcast_to(jnp.where(low_half, 1.0, 0.0), (n_keys, 128)),
                                 jnp.broadcast_to(jnp.where(low_half, 0.0, 1.0), (n_keys, 128))],
                                axis=0).astype(BF16)
        qp = jnp.concatenate([q[:, 256 * g:256 * g + 128], q[:, 256 * g + 128:256 * g + 256]], axis=0)
        s = lax.dot_general(qp, k2, _NT, preferred_element_type=F32)
        halves = [s[:, 0:n_keys], s[:, n_keys:]]
        p = jnp.concatenate([jnp.exp(sh - jnp.max(sh, axis=-1, keepdims=True)) for sh in halves],
                            axis=1).astype(BF16)
        ov = _dot(p, jnp.concatenate([v2, ones2], axis=1))
        o = ov[:, 0:128] / ov[:, 128:]
        o_ref[:, 256 * g:256 * g + 128] = o[0:qb, :].astype(BF16)
        o_ref[:, 256 * g + 128:256 * g + 256] = o[qb:, :].astype(BF16)


def _attention(qn, kn, v_all, cache, *, n_batch, seq, first_row):
    qblocks = seq // TB
    kv_spec = pl.BlockSpec((seq, 128), lambda b, i: (b + first_row // seq, 0))
    in_specs = [pl.BlockSpec((TB, 512), lambda b, i: (first_row // TB + b * qblocks + i, 0)), kv_spec, kv_spec]
    args = [qn, kn, v_all]
    if cache is not None:
        cspec = pl.BlockSpec((None, None, N_KV_HEADS, PAST_LEN, DH_ATT), lambda b, i: (b, 0, 0, 0, 0))
        in_specs += [cspec, cspec]
        args += list(cache)
    return pl.pallas_call(
        functools.partial(_attn_kernel, has_cache=cache is not None),
        grid=(n_batch, qblocks),
        in_specs=in_specs,
        out_specs=pl.BlockSpec((TB, 512), lambda b, i: (b * qblocks + i, 0)),
        out_shape=jax.ShapeDtypeStruct((n_batch * seq, 512), BF16),
        compiler_params=_cparams(("arbitrary", "arbitrary")),
        name="attention_lat" if cache is not None else "attention_ctx",
    )(*args)


def _log_sigmoid(x):
    return jnp.minimum(x, 0.0) - jnp.log1p(jnp.exp(-jnp.abs(x)))


def _col_bcast(cols, j):
    return jnp.broadcast_to(cols[:, j:j + 1], (CHUNK, CHUNK))


def _mlstm_kernel(*refs, has_state, n_chunks):
    n_in = 10 if has_state else 7
    n_out = 1 if has_state else 4
    q_ref, k_ref, v_ref, g_ref, bg_ref, om_ref, nw_ref = refs[:7]
    hm_ref = refs[n_in]
    scratch = refs[n_in + n_out:]
    st = scratch[0:8]
    ms = scratch[8:16]
    hdir = scratch[16:18]
    rows_sc, cmb_sc, bb_sc, kt_sc = scratch[18:22]
    neg_inf = F32(-jnp.inf)
    zeros112 = jnp.zeros((CHUNK - 16, CHUNK), F32)
    sub = lax.broadcasted_iota(jnp.int32, (CHUNK, CHUNK), 0)
    lan = lax.broadcasted_iota(jnp.int32, (CHUNK, CHUNK), 1)
    ones = jnp.ones((CHUNK, CHUNK), BF16)

    stats = []
    for c in range(n_chunks):
        pre = g_ref[c] + bg_ref[...]
        logf = _log_sigmoid(pre)
        for d in range(2):
            bcum = _dot3(logf, ((sub <= lan) if d == 0 else (sub >= lan)).astype(BF16))
            li4 = pre[4 * d:4 * d + 4, :]
            lf4 = logf[8 + 4 * d:12 + 4 * d, :]
            b4 = bcum[8 + 4 * d:12 + 4 * d, :]
            stats.append((c, d, li4 - b4, lf4, b4))
    for c, d, r4, lf4, b4 in stats:
        blast4 = b4[:, CHUNK - 1:CHUNK] if d == 0 else b4[:, 0:1]
        wlog4 = blast4 + r4
        wmax4 = jnp.max(wlog4, axis=-1, keepdims=True)
        full = lambda a: jnp.broadcast_to(a, (4, CHUNK))
        rows_sc[2 * c + d] = jnp.concatenate(
            [r4, wlog4, full(blast4), full(wmax4), jnp.zeros((16, CHUNK), F32)], axis=0)
    for c in range(n_chunks):
        for h in range(N_HEADS_M):
            kc = k_ref[CHUNK * c:CHUNK * (c + 1), DH_M * h:DH_M * (h + 1)]
            kt_sc[c * N_HEADS_M + h] = kc.astype(F32).T.astype(BF16)
    for c, d, r4, lf4, b4 in stats:
        within = (lan <= sub) if d == 0 else (lan >= sub)
        for h in range(N_HEADS_M):
            cummax = jnp.max(jnp.where(within, r4[h:h + 1, :], neg_inf), axis=-1, keepdims=True)
            cmb_sc[(2 * c + d) * N_HEADS_M + h] = jnp.broadcast_to(cummax, (CHUNK, CHUNK))
    for c, d, r4, lf4, b4 in stats:
        within = (lan <= sub) if d == 0 else (lan >= sub)
        for h in range(N_HEADS_M):
            bb_sc[(2 * c + d) * N_HEADS_M + h] = _dot2(jnp.where(within, lf4[h:h + 1, :], 0.0), ones)

    for d in range(2):
        if has_state:
            c0_ref, n0_ref, m0_ref = refs[7:10]
            ncols = jnp.concatenate([n0_ref[d], jnp.zeros((12, DH_M), F32), zeros112], axis=0).T
        for h in range(N_HEADS_M):
            idx = 4 * d + h
            if has_state:
                st[idx][:, 0:DH_M] = c0_ref[d, h]
                st[idx][:, DH_M:] = _col_bcast(ncols, h)
                ms[idx][0:1, :] = m0_ref[idx:idx + 1, :]
            else:
                st[idx][...] = jnp.zeros((DH_M, 2 * DH_M), F32)
                ms[idx][0:1, :] = jnp.full((1, CHUNK), NEG_INIT, F32)

    def chunk_step(i, carry):
        for d in range(2):
            c = i if d == 0 else n_chunks - 1 - i
            r0 = pl.multiple_of(c * CHUNK, CHUNK)
            rows = rows_sc[2 * c + d]
            mask = (lan <= sub) if d == 0 else (lan >= sub)
            heads = []
            for h in range(N_HEADS_M):
                idx = 4 * d + h
                hs = slice(DH_M * h, DH_M * (h + 1))
                qc = q_ref[pl.ds(r0, CHUNK), hs]
                kt = kt_sc[c * N_HEADS_M + h]
                state = st[idx][...]
                heads.append(dict(idx=idx, hs=hs, qc=qc, kt=kt, state=state,
                                  s_raw=_dot(qc, kt), qs=_dot(qc, state.astype(BF16))))
            for h, hd in enumerate(heads):
                mp = ms[hd["idx"]][0:1, :]
                m_b = jnp.maximum(mp, cmb_sc[(2 * c + d) * N_HEADS_M + h])
                s = hd["s_raw"] * jnp.exp(jnp.where(mask, rows[h:h + 1, :] - m_b, neg_inf))
                s_hi = s.astype(BF16)
                vc = v_ref[pl.ds(r0, CHUNK), hd["hs"]]
                v_ones = jnp.concatenate([vc, ones], axis=1)
                hd.update(mp=mp, m_b=m_b, v_ones=v_ones, sv=_dot(s_hi, v_ones),
                          s_lo_sum=_dot((s - s_hi.astype(F32)).astype(BF16), ones))
            for h, hd in enumerate(heads):
                mp, m_b = hd["mp"], hd["m_b"]
                gw_b = jnp.exp(mp - m_b)
                en_b = jnp.exp(-(bb_sc[(2 * c + d) * N_HEADS_M + h] + m_b))
                num = hd["sv"][:, 0:DH_M] + gw_b * hd["qs"][:, 0:DH_M]
                den = hd["sv"][:, DH_M:] + hd["s_lo_sum"] + gw_b * hd["qs"][:, DH_M:]
                hdir[d][pl.ds(r0, CHUNK), hd["hs"]] = num / jnp.maximum(jnp.abs(den), en_b)
            for h, hd in enumerate(heads):
                mp = hd["mp"]
                gend = rows[8 + h:9 + h, :] + mp
                mnew = jnp.maximum(gend, rows[12 + h:13 + h, :])
                w_row = jnp.exp(rows[4 + h:5 + h, :] - mnew)
                kw_t = (hd["kt"].astype(F32) * w_row).astype(BF16)
                dec = jnp.exp(gend - mnew)
                st[hd["idx"]][...] = (jnp.concatenate([dec, dec], axis=1) * hd["state"]
                                      + _dot(kw_t, hd["v_ones"]))
                ms[hd["idx"]][0:1, :] = mnew
        return carry

    lax.fori_loop(0, n_chunks, chunk_step, 0)

    for h in range(N_HEADS_M):
        hs = slice(DH_M * h, DH_M * (h + 1))
        hh = hdir[0][:, hs] + hdir[1][:, hs]
        y = hh * lax.rsqrt(jnp.mean(hh * hh, axis=-1, keepdims=True) + EPS) * nw_ref[:, hs]
        hm_ref[:, hs] = (jax.nn.sigmoid(om_ref[:, hs]) * y).astype(BF16)
    if not has_state:
        c_ref, n_ref, m_ref = refs[n_in + 1:n_in + 4]
        for d in range(2):
            for h in range(N_HEADS_M):
                idx = 4 * d + h
                c_ref[d, h] = st[idx][:, 0:DH_M]
                n_ref[d, h:h + 1, :] = st[idx][:, DH_M:].T[0:1, :]
                m_ref[idx:idx + 1, :] = ms[idx][0:1, :]


def _mlstm(qm, km, vm, g3, bg, om, nw, state, *, n_batch, seq, first_block):
    n_chunks = seq // CHUNK
    rb = seq // TB
    tok = lambda: pl.BlockSpec((seq, 512), lambda b: (b + first_block // rb, 0))
    full = lambda a: pl.BlockSpec(a.shape, lambda b: (0,) * a.ndim)
    in_specs = [tok(), tok(), tok(),
                pl.BlockSpec((n_chunks, 16, CHUNK), lambda b: (b + first_block // rb, 0, 0)),
                full(bg), tok(), full(nw)]
    args = [qm, km, vm, g3, bg, om, nw]
    hm_spec = pl.BlockSpec((seq, 512), lambda b: (b, 0))
    hm_shape = jax.ShapeDtypeStruct((n_batch * seq, 512), BF16)
    scratch = ([pltpu.VMEM((DH_M, 2 * DH_M), F32)] * 8 + [pltpu.VMEM((8, CHUNK), F32)] * 8
               + [pltpu.VMEM((seq, 512), F32)] * 2
               + [pltpu.VMEM((2 * n_chunks, 32, CHUNK), F32),
                  pltpu.VMEM((8 * n_chunks, CHUNK, CHUNK), F32),
                  pltpu.VMEM((8 * n_chunks, CHUNK, CHUNK), F32),
                  pltpu.VMEM((N_HEADS_M * n_chunks, DH_M, CHUNK), BF16)])
    if state is not None:
        c0, n0, m0 = state
        in_specs += [pl.BlockSpec((None, None, 2, N_HEADS_M, DH_M, DH_M), lambda b: (b, 0, 0, 0, 0, 0)),
                     pl.BlockSpec((None, None, 2, N_HEADS_M, DH_M), lambda b: (b, 0, 0, 0, 0)),
                     pl.BlockSpec((None, 8, CHUNK), lambda b: (b, 0, 0))]
        args += [c0, n0, m0]
        out_specs, out_shape = hm_spec, hm_shape
    else:
        out_specs = [hm_spec,
                     pl.BlockSpec((None, 2, N_HEADS_M, DH_M, DH_M), lambda b: (b, 0, 0, 0, 0)),
                     pl.BlockSpec((None, 2, N_HEADS_M, DH_M), lambda b: (b, 0, 0, 0)),
                     pl.BlockSpec((None, 8, CHUNK), lambda b: (b, 0, 0))]
        out_shape = [hm_shape,
                     jax.ShapeDtypeStruct((n_batch, 2, N_HEADS_M, DH_M, DH_M), F32),
                     jax.ShapeDtypeStruct((n_batch, 2, N_HEADS_M, DH_M), F32),
                     jax.ShapeDtypeStruct((n_batch, 8, CHUNK), F32)]
    return pl.pallas_call(
        functools.partial(_mlstm_kernel, has_state=state is not None, n_chunks=n_chunks),
        grid=(n_batch,),
        in_specs=in_specs,
        out_specs=out_specs,
        out_shape=out_shape,
        scratch_shapes=scratch,
        compiler_params=_cparams(("arbitrary",)),
        name="mlstm_lat" if state is not None else "mlstm_ctx",
    )(*args)


def _outproj_kernel(attc_ref, attl_ref, hmc_ref, hml_ref, xc_ref, xl_ref, mod_ref, nw_ref, wo_ref,
                    rwh_ref, rwl_ref, wsg_ref, wsu_ref, wsd_ref, base_ref, hn_ref, lt_ref):
    y = (_dot(_ctx_or_lat(attc_ref, attl_ref), wo_ref[0:D_ATT, :])
         + _dot(_ctx_or_lat(hmc_ref, hml_ref), wo_ref[D_ATT:, :]))
    x1 = _ctx_or_lat(xc_ref, xl_ref) + mod_ref[2:3, :] * y
    z = x1 * lax.rsqrt(jnp.mean(x1 * x1, axis=-1, keepdims=True) + EPS) * nw_ref[...]
    hn = z * (1.0 + mod_ref[4:5, :]) + mod_ref[3:4, :]
    hb = hn.astype(BF16)
    hl = (hn - hb.astype(F32)).astype(BF16)
    nt = lambda w, t: lax.dot_general(w, t, _NT, preferred_element_type=F32)
    lt_ref[...] = nt(rwh_ref[...], hb) + nt(rwl_ref[...], hb) + nt(rwh_ref[...], hl)
    a = _silu(_dot(hb, wsg_ref[...])) * _dot(hb, wsu_ref[...])
    shared = _dot(a.astype(BF16), wsd_ref[...])
    base_ref[...] = x1 + mod_ref[5:6, :] * shared
    packed = pltpu.pack_elementwise([hn[:, :512], hn[:, 512:]], packed_dtype=BF16)
    for c in range(4):
        hn_ref[pl.ds(c, TBP, stride=4), :] = packed[:, 128 * c:128 * (c + 1)]


def _outproj(att, hm, x, mod3, norm2_w, w_out, rw_hi, rw_lo, wsg, wsu, wsd):
    tok = lambda w: pl.BlockSpec((TBP, w), lambda i: (i, 0))
    full = lambda a: pl.BlockSpec(a.shape, lambda i: (0,) * a.ndim)
    return pl.pallas_call(
        _outproj_kernel,
        grid=(T_ALL // TBP,),
        in_specs=_split_specs(512) + _split_specs(512) + _split_specs(D_MODEL) + [
                  pl.BlockSpec((None, 6, D_MODEL), lambda i: (_mod_row(i), 0, 0)),
                  full(norm2_w), full(w_out), full(rw_hi), full(rw_lo), full(wsg), full(wsu), full(wsd)],
        out_specs=[tok(D_MODEL), pl.BlockSpec((4 * TBP, 128), lambda i: (i, 0)),
                   pl.BlockSpec((N_EXPERTS, TBP), lambda i: (0, i))],
        out_shape=[jax.ShapeDtypeStruct((T_ALL, D_MODEL), F32),
                   jax.ShapeDtypeStruct((4 * T_ALL, 128), jnp.uint32),
                   jax.ShapeDtypeStruct((N_EXPERTS, T_ALL), F32)],
        compiler_params=_cparams(("arbitrary",)),
        name="outproj",
    )(*att, *hm, *x, mod3, norm2_w, w_out, rw_hi, rw_lo, wsg, wsu, wsd)


def _first_max(vals, ids, limit):
    m = functools.reduce(jnp.maximum, [jnp.max(v, axis=0, keepdims=True) for v in vals])
    cand = [jnp.min(jnp.where(v == m, i, limit), axis=0, keepdims=True) for v, i in zip(vals, ids)]
    return m, functools.reduce(jnp.minimum, cand)


def _router_kernel(lt_ref, bias_ref, gate_ref, idx_ref):
    n = lt_ref.shape[1]
    score = jax.nn.sigmoid(lt_ref[...])
    biased = score + bias_ref[...]
    sub = lax.broadcasted_iota(jnp.int32, (8, n), 0).astype(F32)
    neg_inf = F32(-jnp.inf)
    slabs = [biased[8 * g:8 * (g + 1), :] for g in range(N_GROUPS)]
    gs = []
    for sl in slabs:
        m1, i1 = _first_max([sl], [sub], 8.0)
        m2 = jnp.max(jnp.where(sub == i1, neg_inf, sl), axis=0, keepdims=True)
        gs.append(m1 + m2)
    cur = jnp.concatenate(gs, axis=0)
    gsel = jnp.zeros((8, n), F32)
    for _ in range(TOPK_GROUPS):
        _, i = _first_max([cur], [sub], 8.0)
        hit = sub == i
        gsel = jnp.where(hit, 1.0, gsel)
        cur = jnp.where(hit, neg_inf, cur)
    vals = [jnp.where(gsel[g:g + 1, :] > 0.0, slabs[g], neg_inf) for g in range(N_GROUPS)]
    ids = [sub + 8.0 * g for g in range(N_GROUPS)]
    picked = [jnp.zeros((8, n), F32) for _ in range(N_GROUPS)]
    order = []
    for _ in range(TOP_K):
        _, i = _first_max(vals, ids, float(N_EXPERTS))
        order.append(i)
        hits = [idg == i for idg in ids]
        picked = [jnp.where(hh, score[8 * g:8 * (g + 1), :], p) for g, (p, hh) in enumerate(zip(picked, hits))]
        vals = [jnp.where(hh, neg_inf, v) for v, hh in zip(vals, hits)]
    total = functools.reduce(jnp.add, [jnp.sum(p, axis=0, keepdims=True) for p in picked])
    gate_t = jnp.concatenate([p / total * ROUTED_SCALE for p in picked]
                             + [jnp.zeros((128 - N_EXPERTS, n), F32)], axis=0)
    gate_ref[...] = gate_t.T
    idx_ref[...] = jnp.concatenate(order, axis=0).astype(jnp.int32)


def _router(logits_t, bias_col):
    return pl.pallas_call(
        _router_kernel,
        grid=(T_ALL // TB_MOE,),
        in_specs=[pl.BlockSpec((N_EXPERTS, TB_MOE), lambda i: (0, i)),
                  pl.BlockSpec((N_EXPERTS, 1), lambda i: (0, 0))],
        out_specs=[pl.BlockSpec((TB_MOE, 128), lambda i: (i, 0)),
                   pl.BlockSpec((TOP_K, TB_MOE), lambda i: (0, i))],
        out_shape=[jax.ShapeDtypeStruct((T_ALL, 128), F32),
                   jax.ShapeDtypeStruct((TOP_K, T_ALL), jnp.int32)],
        compiler_params=_cparams(("arbitrary",)),
        name="router",
    )(logits_t, bias_col)


def _plan_kernel(idx_ref, slot_ref, cnt_ref, off_ref, pos_sc):
    n_tiles = T_HALF // 128
    eid = lax.broadcasted_iota(jnp.int32, (N_EXPERTS, 128), 0)
    tri = (lax.broadcasted_iota(jnp.int32, (128, 128), 0)
           <= lax.broadcasted_iota(jnp.int32, (128, 128), 1)).astype(BF16)
    carry = jnp.zeros((N_EXPERTS, 1), F32)
    for j in range(n_tiles):
        it = idx_ref[:, 128 * j:128 * (j + 1)]
        sel = jnp.zeros((N_EXPERTS, 128), F32)
        for k in range(TOP_K):
            sel = jnp.where(it[k:k + 1, :] == eid, 1.0, sel)
        inc = _dot(sel.astype(BF16), tri) + carry
        carry = inc[:, 127:128]
        pos_sc[:, 128 * j:128 * (j + 1)] = inc - 1.0
    count = jnp.broadcast_to(carry, (N_EXPERTS, 128))
    padded = jnp.floor((count + (MT - 1.0)) * (1.0 / MT)) * MT
    before = (lax.broadcasted_iota(jnp.int32, (N_EXPERTS, N_EXPERTS), 1)
              < lax.broadcasted_iota(jnp.int32, (N_EXPERTS, N_EXPERTS), 0)).astype(BF16)
    hi, mid, lo = _split3(padded)
    off = _dot(before, hi) + _dot(before, mid) + _dot(before, lo)
    cnt_ref[...] = count.astype(jnp.int32)
    off_ref[...] = off.astype(jnp.int32)

    def slots(j):
        it = idx_ref[:, 128 * j:128 * (j + 1)]
        val = off + pos_sc[:, 128 * j:128 * (j + 1)]
        rows = [jnp.sum(jnp.where(it[k:k + 1, :] == eid, val, 0.0), axis=0, keepdims=True)
                for k in range(TOP_K)]
        return jnp.concatenate(rows, axis=0).astype(jnp.int32)

    for j in range(n_tiles // 2):
        slot_ref[:, 128 * j:128 * (j + 1)] = slots(j) | (slots(j + n_tiles // 2) << 16)


def _plan(idx8):
    return pl.pallas_call(
        _plan_kernel,
        grid=(2,),
        in_specs=[pl.BlockSpec((TOP_K, T_HALF), lambda h: (0, h))],
        out_specs=[pl.BlockSpec((None, TOP_K, T_HALF // 2), lambda h: (h, 0, 0)),
                   pl.BlockSpec((None, N_EXPERTS, 128), lambda h: (h, 0, 0)),
                   pl.BlockSpec((None, N_EXPERTS, 128), lambda h: (h, 0, 0))],
        out_shape=[jax.ShapeDtypeStruct((2, TOP_K, T_HALF // 2), jnp.int32),
                   jax.ShapeDtypeStruct((2, N_EXPERTS, 128), jnp.int32),
                   jax.ShapeDtypeStruct((2, N_EXPERTS, 128), jnp.int32)],
        scratch_shapes=[pltpu.VMEM((N_EXPERTS, T_HALF), F32)],
        compiler_params=_cparams(("arbitrary",)),
        name="plan",
    )(idx8)


def _prepare_half(slot_ref, cnt_ref, off_ref, tok_ref, tile_e, tile_first, elist, h):
    def per_expert(x, carry):
        j, q = carry
        n = cnt_ref[h * N_EXPERTS + x]
        first = off_ref[h * N_EXPERTS + x]
        tiles = (n + MT - 1) // MT
        elist[q] = x

        def mark(i, c):
            tile_e[j + i] = x
            tile_first[j + i] = jnp.where(i == 0, 1, 0)
            return c

        lax.fori_loop(0, tiles, mark, 0)

        @pl.when(tiles > 0)
        def _():
            last = first + (tiles - 1) * MT
            for u in range(MT):
                tok_ref[last + u] = T_HALF

        return j + tiles, q + jnp.where(tiles > 0, 1, 0)

    n_tiles, n_live = lax.fori_loop(0, N_EXPERTS, per_expert, (0, 0))
    for extra in range(2):
        tile_e[n_tiles + extra] = 0
        tile_first[n_tiles + extra] = 0
        for u in range(MT):
            tok_ref[(n_tiles + extra) * MT + u] = T_HALF
    return n_tiles, n_live


def _invert_slots(slot_ref, tok_ref, acc_v, h):
    half = T_HALF // 2
    trips = half // INV_UNROLL
    zero_rows = 8 * T_HALF // (TOP_K * trips)
    assert zero_rows * TOP_K * trips == 8 * T_HALF and zero_rows % 8 == 0
    for k in range(TOP_K):
        def body(i, carry, k=k):
            words = [slot_ref[(h * TOP_K + k) * half + i * INV_UNROLL + u] for u in range(INV_UNROLL)]
            for u in range(INV_UNROLL):
                t = i * INV_UNROLL + u
                tok_ref[words[u] & 0xFFFF] = t
                tok_ref[lax.shift_right_logical(words[u], 16)] = t + half
            row = pl.multiple_of((k * trips + i) * zero_rows, 8)
            acc_v[pl.ds(row, zero_rows), :] = jnp.zeros((zero_rows, 128), F32)
            return carry

        lax.fori_loop(0, trips, body, 0)
    acc_v[pl.ds(8 * T_HALF, 8), :] = jnp.zeros((8, 128), F32)


def _moe_kernel(slot_ref, cnt_ref, off_ref, src_hbm, gate_hbm, wg_hbm, wu_hbm, wd_hbm, base_hbm, mod_ref,
                outc_hbm, outl_hbm,
                src_v, gate_v, acc_v, wg_l, wu_l, wd_l, wgu_b, wd_b, xbuf0, xbuf1, gbuf0, gbuf1,
                ybuf0, ybuf1, tok_ref, tile_e, tile_first, elist, live, base_buf, out_buf, sem, wsem, fsem):
    h = pl.program_id(0)
    xbuf, gbuf, ybuf = (xbuf0, xbuf1), (gbuf0, gbuf1), (ybuf0, ybuf1)

    src_cp = pltpu.make_async_copy(src_hbm.at[pl.ds(pl.multiple_of(h * (4 * T_HALF), 8), 4 * T_HALF)],
                                   src_v.at[pl.ds(0, 4 * T_HALF)], sem.at[0])
    gate_cp = pltpu.make_async_copy(gate_hbm.at[pl.ds(pl.multiple_of(h * T_HALF, 8), T_HALF)],
                                    gate_v.at[pl.ds(0, T_HALF)], sem.at[1])
    src_cp.start()
    gate_cp.start()

    def weight_copies(x, slot):
        return [pltpu.make_async_copy(w_hbm.at[x], w_l.at[slot], wsem.at[slot, i])
                for i, (w_hbm, w_l) in enumerate(((wg_hbm, wg_l), (wu_hbm, wu_l), (wd_hbm, wd_l)))]

    n_tiles, n_live = _prepare_half(slot_ref, cnt_ref, off_ref, tok_ref, tile_e, tile_first, elist, h)
    live[0] = 0
    live[1] = n_live

    for ahead in range(W_SLOTS - 1):
        @pl.when(ahead < n_live)
        def _(ahead=ahead):
            for cp in weight_copies(elist[ahead], ahead):
                cp.start()

    zero = jnp.zeros((8, 128), F32)
    src_v[pl.ds(4 * T_HALF, 8), :] = pltpu.pack_elementwise([zero, zero], packed_dtype=BF16)
    gate_v[pl.ds(T_HALF, 8), :] = zero
    ybuf0[...] = jnp.zeros(ybuf0.shape, F32)
    ybuf1[...] = jnp.zeros(ybuf1.shape, F32)
    _invert_slots(slot_ref, tok_ref, acc_v, h)
    src_cp.wait()
    gate_cp.wait()

    def switch_expert():
        q = live[0]
        slot = q % W_SLOTS
        for cp in weight_copies(elist[q], slot):
            cp.wait()
        wgu_b[:, 0:D_EXPERT] = wg_l[slot].astype(BF16)
        wgu_b[:, D_EXPERT:] = wu_l[slot].astype(BF16)
        wd_b[...] = wd_l[slot].astype(BF16)
        live[0] = q + 1
        nxt = q + W_SLOTS - 1

        @pl.when(nxt < live[1])
        def _():
            for cp in weight_copies(elist[nxt], nxt % W_SLOTS):
                cp.start()

    def gather(j, xb, gb, rows=(0, MT)):
        base = j * MT
        for m in range(*rows):
            t = tok_ref[base + m]
            xb[pl.ds(m, 4, stride=MT_STRIDE), :] = src_v[pl.ds(pl.multiple_of(t * 4, 4), 4), :]
            gb[m:m + 1, :] = gate_v[pl.ds(t, 1), :]

    def scatter(j, yb, rows=(0, MT)):
        base = j * MT
        for b in range(rows[0] // RMW_BATCH, rows[1] // RMW_BATCH):
            ms = [b * RMW_BATCH + u for u in range(RMW_BATCH)]
            targets = [pl.ds(pl.multiple_of(tok_ref[base + m] * 8, 8), 8) for m in ms]
            vals = [acc_v[r, :] + yb[pl.ds(m, 8, stride=MT_STRIDE), :] for r, m in zip(targets, ms)]
            for r, v in zip(targets, vals):
                acc_v[r, :] = v

    quarters = [(q * MT // 4, (q + 1) * MT // 4) for q in range(4)]

    def step(j, p):
        pl.when(tile_first[j] == 1)(switch_expert)
        xb, gb, yb = xbuf[p], gbuf[p], ybuf[p]
        nxt = (j + 1, xbuf[1 - p], gbuf[1 - p])
        prv = (jnp.maximum(j - 1, 0), ybuf[1 - p])
        lo, hi = [], []
        for c in range(4):
            words = xb[MT_STRIDE * c:MT_STRIDE * c + MT, :]
            unpack = functools.partial(pltpu.unpack_elementwise, words, packed_dtype=BF16, unpacked_dtype=F32)
            lo.append(unpack(index=0).astype(BF16))
            hi.append(unpack(index=1).astype(BF16))
        x = jnp.concatenate(lo + hi, axis=1)
        gather(*nxt, rows=quarters[0])
        h_gate = _dot(x, wgu_b[:, 0:D_EXPERT])
        gather(*nxt, rows=quarters[1])
        h_up = _dot(x, wgu_b[:, D_EXPERT:])
        gather(*nxt, rows=quarters[2])
        g = gb[...]
        g_hi = g.astype(BF16)
        g_lo = (g - g_hi.astype(F32)).astype(BF16)
        pick = (lax.broadcasted_iota(jnp.int32, (128, D_EXPERT), 0) == tile_e[j]).astype(BF16)
        g_col = _dot(g_hi, pick) + _dot(g_lo, pick)
        a = (_silu(h_gate) * h_up * g_col).astype(BF16)
        gather(*nxt, rows=quarters[3])
        scatter(*prv, rows=quarters[0])
        y_lo = _dot(a, wd_b[:, 0:D_MODEL // 2])
        scatter(*prv, rows=quarters[1])
        y_hi = _dot(a, wd_b[:, D_MODEL // 2:])
        scatter(*prv, rows=quarters[2])
        for c in range(4):
            yb[MT_STRIDE * c:MT_STRIDE * c + MT, :] = y_lo[:, 128 * c:128 * (c + 1)]
            yb[MT_STRIDE * (c + 4):MT_STRIDE * (c + 4) + MT, :] = y_hi[:, 128 * c:128 * (c + 1)]
        scatter(*prv, rows=quarters[3])

    gather(0, xbuf[0], gbuf[0])
    n_pairs = (n_tiles + 1) // 2

    def pair(i, carry):
        step(2 * i, 0)
        step(2 * i + 1, 1)
        return carry

    lax.fori_loop(0, n_pairs, pair, 0)
    scatter(jnp.maximum(2 * n_pairs - 1, 0), ybuf[1])

    n_fin = T_HALF // FIN

    def first_token(c):
        return h * T_HALF + c * FIN

    def base_copy(c, slot):
        return pltpu.make_async_copy(base_hbm.at[pl.ds(pl.multiple_of(first_token(c), FIN), FIN)],
                                     base_buf.at[slot], fsem.at[0, slot])

    def on_out_copy(c, action):
        t0 = first_token(c)
        slot = c % 2

        @pl.when(t0 < T_CTX)
        def _():
            action(pltpu.make_async_copy(out_buf.at[slot], outc_hbm.at[pl.ds(pl.multiple_of(t0, FIN), FIN)],
                                         fsem.at[1, slot]))

        @pl.when(t0 >= T_CTX)
        def _():
            action(pltpu.make_async_copy(out_buf.at[slot],
                                         outl_hbm.at[pl.ds(pl.multiple_of(t0 - T_CTX, FIN), FIN)],
                                         fsem.at[1, slot]))

    base_copy(0, 0).start()

    def combine(c, carry):
        slot = c % 2
        base_copy(c, slot).wait()

        @pl.when(c + 1 < n_fin)
        def _():
            base_copy(c + 1, 1 - slot).start()

        @pl.when(c >= 2)
        def _():
            on_out_copy(c - 2, lambda cp: cp.wait())

        t0 = first_token(c)
        mod_row = jnp.where(t0 < T_CTX, 0, 1 + (t0 - T_CTX) // DEC_SEQ)
        acc_row = pl.multiple_of(8 * FIN * c, 8)
        for cc in range(8):
            cs = slice(128 * cc, 128 * (cc + 1))
            routed = acc_v[pl.ds(acc_row + cc, FIN, stride=8), :]
            out_buf[slot, :, cs] = base_buf[slot, :, cs] + mod_ref[mod_row, 5:6, cs] * routed
        on_out_copy(c, lambda cp: cp.start())
        return carry

    lax.fori_loop(0, n_fin, combine, 0)
    on_out_copy(n_fin - 2, lambda cp: cp.wait())
    on_out_copy(n_fin - 1, lambda cp: cp.wait())


def _moe(slot_words, cnt, off, src, gate, wg, wu, wd, base, mod3):
    any_spec = pl.BlockSpec(memory_space=pl.ANY)
    tile_buf = lambda rows, dt: pltpu.VMEM((rows * MT_STRIDE, 128), dt)
    return pl.pallas_call(
        _moe_kernel,
        grid_spec=pltpu.PrefetchScalarGridSpec(
            num_scalar_prefetch=3,
            grid=(2,),
            in_specs=[any_spec] * 6 + [pl.BlockSpec(mod3.shape, lambda h, *_: (0, 0, 0))],
            out_specs=[any_spec, any_spec],
            scratch_shapes=[pltpu.VMEM((4 * T_HALF + 8, 128), jnp.uint32),
                            pltpu.VMEM((T_HALF + 8, 128), F32),
                            pltpu.VMEM((8 * T_HALF + 8, 128), F32),
                            pltpu.VMEM((W_SLOTS, D_MODEL, D_EXPERT), F32),
                            pltpu.VMEM((W_SLOTS, D_MODEL, D_EXPERT), F32),
                            pltpu.VMEM((W_SLOTS, D_EXPERT, D_MODEL), F32),
                            pltpu.VMEM((D_MODEL, 2 * D_EXPERT), BF16),
                            pltpu.VMEM((D_EXPERT, D_MODEL), BF16),
                            tile_buf(4, jnp.uint32), tile_buf(4, jnp.uint32),
                            pltpu.VMEM((MT, 128), F32), pltpu.VMEM((MT, 128), F32),
                            tile_buf(8, F32), tile_buf(8, F32),
                            pltpu.SMEM((SLOT_CAP,), jnp.int32),
                            pltpu.SMEM((NT_MAX + 2,), jnp.int32),
                            pltpu.SMEM((NT_MAX + 2,), jnp.int32),
                            pltpu.SMEM((N_EXPERTS,), jnp.int32),
                            pltpu.SMEM((2,), jnp.int32),
                            pltpu.VMEM((2, FIN, D_MODEL), F32),
                            pltpu.VMEM((2, FIN, D_MODEL), F32),
                            pltpu.SemaphoreType.DMA((2,)),
                            pltpu.SemaphoreType.DMA((W_SLOTS, 3)),
                            pltpu.SemaphoreType.DMA((2, 2))]),
        out_shape=[jax.ShapeDtypeStruct((T_CTX, D_MODEL), F32), jax.ShapeDtypeStruct((T_LAT, D_MODEL), F32)],
        compiler_params=_cparams(("arbitrary",)),
        name="moe",
    )(slot_words, cnt, off, src, gate, wg, wu, wd, base, mod3)


def _head_indicators(width):
    head = jnp.arange(width) // DH_ATT
    ind = (head[:, None] == jnp.arange(128)[None, :]).astype(BF16)
    return ind, ind.T


def kernel(x_prompt, x_sample, cache_attn_k, cache_attn_v, state_mlstm_c, state_mlstm_n, state_mlstm_m, c, c_ctx, w_mod, b_mod, norm1_w, norm2_w, w_in, q_norm_w, k_norm_w, b_gates, m_norm_w, w_out, router_w, router_bias, w_gate, w_up, w_down, ws_gate, ws_up, ws_down):
    x = (x_prompt.reshape(T_CTX, D_MODEL), x_sample.reshape(T_LAT, D_MODEL))
    mod3 = _modulation(c, c_ctx, w_mod[0], b_mod[0])

    w_main = w_in[0, :, :P_MAIN].astype(BF16)
    w_gates_t = w_in[0, :, P_MAIN:].T.astype(BF16)
    qw = jnp.tile(q_norm_w, (1, N_HEADS_ATT))
    kw = jnp.tile(k_norm_w, (1, N_KV_HEADS))
    inds = _head_indicators(512) + _head_indicators(128)
    qn, kn, va, qm, km, vm, om, gt = _inproj(*x, mod3, norm1_w, w_main, w_gates_t, qw, kw, inds, _rope_tables())

    att_c = _attention(qn, kn, va, None, n_batch=BATCH, seq=SEQ, first_row=0)
    att_l = _attention(qn, kn, va, (cache_attn_k, cache_attn_v), n_batch=DEC_BATCH, seq=DEC_SEQ,
                       first_row=T_CTX)

    g3 = gt.reshape(16, T_ALL // CHUNK, CHUNK).transpose(1, 0, 2)
    bg = b_gates.reshape(16, 1)
    hm_c, c_new, n_new, m_new = _mlstm(qm, km, vm, g3, bg, om, m_norm_w, None,
                                       n_batch=BATCH, seq=SEQ, first_block=0)
    m0 = jnp.broadcast_to(state_mlstm_m.reshape(DEC_BATCH, 8, 1), (DEC_BATCH, 8, CHUNK))
    hm_l = _mlstm(qm, km, vm, g3, bg, om, m_norm_w, (state_mlstm_c, state_mlstm_n, m0),
                  n_batch=DEC_BATCH, seq=DEC_SEQ, first_block=N_CTX_BLOCKS)

    rw_t = router_w[0].T
    rw_hi = rw_t.astype(BF16)
    rw_lo = (rw_t - rw_hi.astype(F32)).astype(BF16)
    base, hn2, logits_t = _outproj((att_c, att_l), (hm_c, hm_l), x, mod3, norm2_w, w_out[0].astype(BF16),
                                   rw_hi, rw_lo, ws_gate[0].astype(BF16), ws_up[0].astype(BF16),
                                   ws_down[0].astype(BF16))
    gate, idx8 = _router(logits_t, router_bias.reshape(N_EXPERTS, 1))
    slot_words, cnt, off = _plan(idx8)
    out_c, out_l = _moe(slot_words.reshape(TOP_K * T_HALF), cnt[:, :, 0].reshape(2 * N_EXPERTS),
                        off[:, :, 0].reshape(2 * N_EXPERTS), hn2, gate, w_gate[0], w_up[0], w_down[0],
                        base, mod3)

    y_prompt = out_c.reshape(BATCH, SEQ, D_MODEL)
    y_sample = out_l.reshape(DEC_BATCH, DEC_SEQ, D_MODEL)
    to_cache = lambda a: a[:T_CTX].reshape(BATCH, SEQ, N_KV_HEADS, DH_ATT).transpose(0, 2, 1, 3)[:, None]
    new_k = to_cache(kn)
    new_v = to_cache(va)
    new_m = m_new[:, :, 0].reshape(BATCH, 1, 2, N_HEADS_M)
    return (y_prompt, y_sample, new_k, new_v, c_new[:, None], n_new[:, None], new_m)
```

```python
import functools

import jax
import jax.numpy as jnp
from jax import lax
from jax.experimental import pallas as pl
from jax.experimental.pallas import tpu as pltpu

F32 = jnp.float32
BF16 = jnp.bfloat16

D_MODEL = 1024
BATCH = 32
SEQ = 256
DEC_BATCH = 2
DEC_SEQ = 1024
PAST_LEN = 256
GRID_W = 64
N_HEADS_ATT = 8
N_KV_HEADS = 2
DH_ATT = 64
D_ATT = 512
ROPE_THETA = 10000.0
N_HEADS_M = 4
DH_M = 128
D_M = 512
CHUNK = 128
N_EXPERTS = 64
TOP_K = 8
N_GROUPS = 8
TOPK_GROUPS = 4
D_EXPERT = 256
ROUTED_SCALE = 2.5
EPS = 1e-6
NEG_INIT = -1e30

T_CTX = BATCH * SEQ
T_LAT = DEC_BATCH * DEC_SEQ
T_ALL = T_CTX + T_LAT
TB = 256
N_CTX_BLOCKS = T_CTX // TB
TBP = 512
NP_CTX = T_CTX // TBP
NP_LAT_PER_BATCH = DEC_SEQ // TBP
TB_MOE = 1024
T_HALF = T_ALL // 2
MT = 256
MT_STRIDE = MT + 8
NT_MAX = TOP_K * T_HALF // MT + N_EXPERTS
SLOT_CAP = (NT_MAX + 2) * MT
RMW_BATCH = 16
INV_UNROLL = 16
W_SLOTS = 2
FIN = 128
FIN_BASE_SLOTS = 6
FIN_OUT_SLOTS = 4
assert FIN_OUT_SLOTS <= FIN_BASE_SLOTS
assert SLOT_CAP < 2 ** 16 and (T_HALF // 2) % INV_UNROLL == 0 and MT % RMW_BATCH == 0
P_MAIN = 2816
VMEM_LIMIT = 56 * 1024 * 1024

_NT = (((1,), (1,)), ((), ()))
_TN = (((0,), (0,)), ((), ()))


def _cparams(sem):
    return pltpu.CompilerParams(dimension_semantics=sem, vmem_limit_bytes=VMEM_LIMIT)


def _split3(x):
    hi = x.astype(BF16)
    r1 = x - hi.astype(F32)
    mid = r1.astype(BF16)
    lo = (r1 - mid.astype(F32)).astype(BF16)
    return hi, mid, lo


def _dot(a, b):
    return jnp.dot(a, b, preferred_element_type=F32)


def _dot3(x, m_bf16):
    hi, mid, lo = _split3(x)
    return _dot(hi, m_bf16) + _dot(mid, m_bf16) + _dot(lo, m_bf16)


def _dot2(x, m_bf16):
    hi = x.astype(BF16)
    lo = (x - hi.astype(F32)).astype(BF16)
    return _dot(hi, m_bf16) + _dot(lo, m_bf16)


def _silu(x):
    return x * jax.nn.sigmoid(x)


def _mod_row(i):
    return jnp.where(i < NP_CTX, 0, 1 + (i - NP_CTX) // NP_LAT_PER_BATCH)


def _mod_kernel(ct_ref, w_ref, b_ref, o_ref):
    s = _silu(ct_ref[...])
    w = w_ref[...]
    rows = [jnp.sum(w * s[:, r:r + 1], axis=0, keepdims=True) for r in range(3)]
    rows.append(jnp.zeros((5, w.shape[1]), F32))
    o_ref[...] = jnp.concatenate(rows, axis=0) + b_ref[...]


def _modulation(c, c_ctx, w_mod, b_mod):
    cvec = jnp.concatenate([c_ctx[None, :], c, jnp.zeros((5, D_MODEL), F32)], axis=0)
    nb = 1024
    out = pl.pallas_call(
        _mod_kernel,
        grid=(6 * D_MODEL // nb,),
        in_specs=[pl.BlockSpec((D_MODEL, 8), lambda j: (0, 0)),
                  pl.BlockSpec((D_MODEL, nb), lambda j: (0, j)),
                  pl.BlockSpec((1, nb), lambda j: (0, j))],
        out_specs=pl.BlockSpec((8, nb), lambda j: (0, j)),
        out_shape=jax.ShapeDtypeStruct((8, 6 * D_MODEL), F32),
        compiler_params=_cparams(("arbitrary",)),
        name="modulation",
    )(cvec.T, w_mod, b_mod[None, :])
    return out.reshape(8, 6, D_MODEL)


def _ctx_or_lat(ctx_ref, lat_ref):
    return jnp.where(pl.program_id(0) < NP_CTX, ctx_ref[...], lat_ref[...])


def _split_specs(width):
    return [pl.BlockSpec((TBP, width), lambda i: (jnp.minimum(i, NP_CTX - 1), 0)),
            pl.BlockSpec((TBP, width), lambda i: (jnp.maximum(i - NP_CTX, 0), 0))]


def _head_norm(x, ind, ind_t, w_row):
    ss = _dot2(x * x, ind)
    inv = lax.rsqrt(ss * (1.0 / DH_ATT) + EPS)
    return x * _dot2(inv, ind_t) * w_row


def _rope(x, cos, sin_signed):
    lane = lax.broadcasted_iota(jnp.int32, x.shape, 1)
    partner = jnp.where((lane % 32) < 16, pltpu.roll(x, 128 - 16, 1), pltpu.roll(x, 16, 1))
    return x * cos + partner * sin_signed


def _inproj_kernel(xc_ref, xl_ref, mod_ref, nw_ref, w_ref, wgt_ref, qw_ref, kw_ref, iq_ref, iqt_ref,
                   ik_ref, ikt_ref, cos_ref, sin_ref,
                   qn_ref, kn_ref, va_ref, qm_ref, km_ref, vm_ref, om_ref, gt_ref):
    x = _ctx_or_lat(xc_ref, xl_ref)
    y = x * lax.rsqrt(jnp.mean(x * x, axis=-1, keepdims=True) + EPS) * nw_ref[...]
    hn = y * (1.0 + mod_ref[1:2, :]) + mod_ref[0:1, :]
    hb = hn.astype(BF16)
    qn = _head_norm(_dot(hb, w_ref[:, 0:512]), iq_ref[...], iqt_ref[...], qw_ref[...])
    kn = _head_norm(_dot(hb, w_ref[:, 512:640]), ik_ref[...], ikt_ref[...], kw_ref[...])
    qn_ref[...] = qn.astype(BF16)
    kn_ref[...] = kn

    @pl.when(pl.program_id(0) >= NP_CTX)
    def _():
        cos, sin = cos_ref[...], sin_ref[...]
        for j in range(4):
            qn_ref[:, 128 * j:128 * (j + 1)] = _rope(qn[:, 128 * j:128 * (j + 1)], cos, sin).astype(BF16)
        kn_ref[...] = _rope(kn, cos, sin)

    va_ref[...] = _dot(hb, w_ref[:, 640:768])
    qm_ref[...] = _dot(hb, w_ref[:, 768:1280]).astype(BF16)
    km_ref[...] = (_dot(hb, w_ref[:, 1280:1792]) * (DH_M ** -0.5)).astype(BF16)
    vm_ref[...] = _dot(hb, w_ref[:, 1792:2304]).astype(BF16)
    om_ref[...] = _dot(hb, w_ref[:, 2304:2816])
    gt_ref[...] = lax.dot_general(wgt_ref[...], hb, _NT, preferred_element_type=F32)


def _inproj(x_ctx, x_lat, mod3, norm1_w, w_main, w_gates_t, qw, kw, inds, rope_tabs):
    tok = lambda w: pl.BlockSpec((TBP, w), lambda i: (i, 0))
    full = lambda a: pl.BlockSpec(a.shape, lambda i: (0,) * a.ndim)
    sd = lambda w, dt: jax.ShapeDtypeStruct((T_ALL, w), dt)
    rope_spec = pl.BlockSpec((TBP, 128), lambda i: (jnp.maximum(i - NP_CTX, 0) % NP_LAT_PER_BATCH, 0))
    return pl.pallas_call(
        _inproj_kernel,
        grid=(T_ALL // TBP,),
        in_specs=_split_specs(D_MODEL) + [
                  pl.BlockSpec((None, 6, D_MODEL), lambda i: (_mod_row(i), 0, 0)),
                  full(norm1_w), full(w_main), full(w_gates_t), full(qw), full(kw)]
                 + [full(a) for a in inds] + [rope_spec, rope_spec],
        out_specs=[tok(512), tok(128), tok(128), tok(512), tok(512), tok(512), tok(512),
                   pl.BlockSpec((16, TBP), lambda i: (0, i))],
        out_shape=[sd(512, BF16), sd(128, F32), sd(128, F32), sd(512, BF16), sd(512, BF16),
                   sd(512, BF16), sd(512, F32), jax.ShapeDtypeStruct((16, T_ALL), F32)],
        compiler_params=_cparams(("arbitrary",)),
        name="inproj",
    )(x_ctx, x_lat, mod3, norm1_w, w_main, w_gates_t, qw, kw, *inds, *rope_tabs)


def _rope_tables():
    t = jnp.arange(DEC_SEQ)
    pos = jnp.stack([t // GRID_W, t % GRID_W], axis=1).astype(F32)
    n_freq = DH_ATT // 4
    inv_freq = ROPE_THETA ** (-jnp.arange(n_freq, dtype=F32) / n_freq)
    ang = pos[:, :, None] * inv_freq
    cos, sin = jnp.cos(ang), jnp.sin(ang)
    cos_h = jnp.stack([cos, cos], axis=2).reshape(DEC_SEQ, DH_ATT)
    sin_h = jnp.stack([-sin, sin], axis=2).reshape(DEC_SEQ, DH_ATT)
    return jnp.tile(cos_h, (1, 2)), jnp.tile(sin_h, (1, 2))


def _attn_kernel(*refs, has_cache):
    if has_cache:
        q_ref, k_ref, v_ref, kc_ref, vc_ref, o_ref = refs
    else:
        q_ref, k_ref, v_ref, o_ref = refs
    q = q_ref[...] * jnp.asarray(DH_ATT ** -0.5, BF16)
    k = k_ref[...].astype(BF16)
    v = v_ref[...].astype(BF16)
    qb = q.shape[0]
    low_half = lax.broadcasted_iota(jnp.int32, (1, 128), 1) < DH_ATT
    for g in range(N_KV_HEADS):
        kg = k[:, DH_ATT * g:DH_ATT * (g + 1)]
        vg = v[:, DH_ATT * g:DH_ATT * (g + 1)]
        if has_cache:
            kg = jnp.concatenate([kg, kc_ref[g].astype(BF16)], axis=0)
            vg = jnp.concatenate([vg, vc_ref[g].astype(BF16)], axis=0)
        n_keys = kg.shape[0]
        zero = jnp.zeros_like(kg)
        k2 = jnp.concatenate([jnp.concatenate([kg, zero], axis=1), jnp.concatenate([zero, kg], axis=1)], axis=0)
        v2 = jnp.concatenate([jnp.concatenate([vg, zero], axis=1), jnp.concatenate([zero, vg], axis=1)], axis=0)
        ones2 = jnp.concatenate([jnp.broadcast_to(jnp.where(low_half, 1.0, 0.0), (n_keys, 128)),
                                 jnp.broadcast_to(jnp.where(low_half, 0.0, 1.0), (n_keys, 128))],
                                axis=0).astype(BF16)
        qp = jnp.concatenate([q[:, 256 * g:256 * g + 128], q[:, 256 * g + 128:256 * g + 256]], axis=0)
        s = lax.dot_general(qp, k2, _NT, preferred_element_type=F32)
        halves = [s[:, 0:n_keys], s[:, n_keys:]]
        p = jnp.concatenate([jnp.exp(sh - jnp.max(sh, axis=-1, keepdims=True)) for sh in halves],
                            axis=1).astype(BF16)
        ov = _dot(p, jnp.concatenate([v2, ones2], axis=1))
        o = ov[:, 0:128] / ov[:, 128:]
        o_ref[:, 256 * g:256 * g + 128] = o[0:qb, :].astype(BF16)
        o_ref[:, 256 * g + 128:256 * g + 256] = o[qb:, :].astype(BF16)


def _attention(qn, kn, v_all, cache, *, n_batch, seq, first_row):
    qblocks = seq // TB
    kv_spec = pl.BlockSpec((seq, 128), lambda b, i: (b + first_row // seq, 0))
    in_specs = [pl.BlockSpec((TB, 512), lambda b, i: (first_row // TB + b * qblocks + i, 0)), kv_spec, kv_spec]
    args = [qn, kn, v_all]
    if cache is not None:
        cspec = pl.BlockSpec((None, None, N_KV_HEADS, PAST_LEN, DH_ATT), lambda b, i: (b, 0, 0, 0, 0))
        in_specs += [cspec, cspec]
        args += list(cache)
    return pl.pallas_call(
        functools.partial(_attn_kernel, has_cache=cache is not None),
        grid=(n_batch, qblocks),
        in_specs=in_specs,
        out_specs=pl.BlockSpec((TB, 512), lambda b, i: (b * qblocks + i, 0)),
        out_shape=jax.ShapeDtypeStruct((n_batch * seq, 512), BF16),
        compiler_params=_cparams(("arbitrary", "arbitrary")),
        name="attention_lat" if cache is not None else "attention_ctx",
    )(*args)


def _log_sigmoid(x):
    return jnp.minimum(x, 0.0) - jnp.log1p(jnp.exp(-jnp.abs(x)))


def _col_bcast(cols, j):
    return jnp.broadcast_to(cols[:, j:j + 1], (CHUNK, CHUNK))


def _mlstm_kernel(*refs, has_state, n_chunks):
    n_in = 10 if has_state else 7
    n_out = 1 if has_state else 4
    q_ref, k_ref, v_ref, g_ref, bg_ref, om_ref, nw_ref = refs[:7]
    hm_ref = refs[n_in]
    scratch = refs[n_in + n_out:]
    st = scratch[0:8]
    ms = scratch[8:16]
    hdir = scratch[16:18]
    rows_sc, cmb_sc, bb_sc, kt_sc = scratch[18:22]
    neg_inf = F32(-jnp.inf)
    zeros112 = jnp.zeros((CHUNK - 16, CHUNK), F32)
    sub = lax.broadcasted_iota(jnp.int32, (CHUNK, CHUNK), 0)
    lan = lax.broadcasted_iota(jnp.int32, (CHUNK, CHUNK), 1)
    ones = jnp.ones((CHUNK, CHUNK), BF16)

    stats = []
    for c in range(n_chunks):
        pre = g_ref[c] + bg_ref[...]
        logf = _log_sigmoid(pre)
        for d in range(2):
            bcum = _dot3(logf, ((sub <= lan) if d == 0 else (sub >= lan)).astype(BF16))
            li4 = pre[4 * d:4 * d + 4, :]
            lf4 = logf[8 + 4 * d:12 + 4 * d, :]
            b4 = bcum[8 + 4 * d:12 + 4 * d, :]
            stats.append((c, d, li4 - b4, lf4, b4))
    for c, d, r4, lf4, b4 in stats:
        blast4 = b4[:, CHUNK - 1:CHUNK] if d == 0 else b4[:, 0:1]
        wlog4 = blast4 + r4
        wmax4 = jnp.max(wlog4, axis=-1, keepdims=True)
        full = lambda a: jnp.broadcast_to(a, (4, CHUNK))
        rows_sc[2 * c + d] = jnp.concatenate(
            [r4, wlog4, full(blast4), full(wmax4), jnp.zeros((16, CHUNK), F32)], axis=0)
    for c in range(n_chunks):
        for h in range(N_HEADS_M):
            kc = k_ref[CHUNK * c:CHUNK * (c + 1), DH_M * h:DH_M * (h + 1)]
            kt_sc[c * N_HEADS_M + h] = kc.astype(F32).T.astype(BF16)
    for c, d, r4, lf4, b4 in stats:
        within = (lan <= sub) if d == 0 else (lan >= sub)
        for h in range(N_HEADS_M):
            cummax = jnp.max(jnp.where(within, r4[h:h + 1, :], neg_inf), axis=-1, keepdims=True)
            cmb_sc[(2 * c + d) * N_HEADS_M + h] = jnp.broadcast_to(cummax, (CHUNK, CHUNK))
    for c, d, r4, lf4, b4 in stats:
        within = (lan <= sub) if d == 0 else (lan >= sub)
        for h in range(N_HEADS_M):
            bb_sc[(2 * c + d) * N_HEADS_M + h] = _dot2(jnp.where(within, lf4[h:h + 1, :], 0.0), ones)

    for d in range(2):
        if has_state:
            c0_ref, n0_ref, m0_ref = refs[7:10]
            ncols = jnp.concatenate([n0_ref[d], jnp.zeros((12, DH_M), F32), zeros112], axis=0).T
        for h in range(N_HEADS_M):
            idx = 4 * d + h
            if has_state:
                st[idx][:, 0:DH_M] = c0_ref[d, h]
                st[idx][:, DH_M:] = _col_bcast(ncols, h)
                ms[idx][0:1, :] = m0_ref[idx:idx + 1, :]
            else:
                st[idx][...] = jnp.zeros((DH_M, 2 * DH_M), F32)
                ms[idx][0:1, :] = jnp.full((1, CHUNK), NEG_INIT, F32)

    def chunk_step(i, carry):
        for d in range(2):
            c = i if d == 0 else n_chunks - 1 - i
            r0 = pl.multiple_of(c * CHUNK, CHUNK)
            rows = rows_sc[2 * c + d]
            mask = (lan <= sub) if d == 0 else (lan >= sub)
            heads = []
            for h in range(N_HEADS_M):
                idx = 4 * d + h
                hs = slice(DH_M * h, DH_M * (h + 1))
                qc = q_ref[pl.ds(r0, CHUNK), hs]
                kt = kt_sc[c * N_HEADS_M + h]
                state = st[idx][...]
                heads.append(dict(idx=idx, hs=hs, qc=qc, kt=kt, state=state,
                                  s_raw=_dot(qc, kt), qs=_dot(qc, state.astype(BF16))))
            for h, hd in enumerate(heads):
                mp = ms[hd["idx"]][0:1, :]
                m_b = jnp.maximum(mp, cmb_sc[(2 * c + d) * N_HEADS_M + h])
                s = hd["s_raw"] * jnp.exp(jnp.where(mask, rows[h:h + 1, :] - m_b, neg_inf))
                s_hi = s.astype(BF16)
                vc = v_ref[pl.ds(r0, CHUNK), hd["hs"]]
                v_ones = jnp.concatenate([vc, ones], axis=1)
                hd.update(mp=mp, m_b=m_b, v_ones=v_ones, sv=_dot(s_hi, v_ones),
                          s_lo_sum=_dot((s - s_hi.astype(F32)).astype(BF16), ones))
            for h, hd in enumerate(heads):
                mp, m_b = hd["mp"], hd["m_b"]
                gw_b = jnp.exp(mp - m_b)
                en_b = jnp.exp(-(bb_sc[(2 * c + d) * N_HEADS_M + h] + m_b))
                num = hd["sv"][:, 0:DH_M] + gw_b * hd["qs"][:, 0:DH_M]
                den = hd["sv"][:, DH_M:] + hd["s_lo_sum"] + gw_b * hd["qs"][:, DH_M:]
                hdir[d][pl.ds(r0, CHUNK), hd["hs"]] = num / jnp.maximum(jnp.abs(den), en_b)
            for h, hd in enumerate(heads):
                mp = hd["mp"]
                gend = rows[8 + h:9 + h, :] + mp
                mnew = jnp.maximum(gend, rows[12 + h:13 + h, :])
                w_row = jnp.exp(rows[4 + h:5 + h, :] - mnew)
                kw_t = (hd["kt"].astype(F32) * w_row).astype(BF16)
                dec = jnp.exp(gend - mnew)
                st[hd["idx"]][...] = (jnp.concatenate([dec, dec], axis=1) * hd["state"]
                                      + _dot(kw_t, hd["v_ones"]))
                ms[hd["idx"]][0:1, :] = mnew
        return carry

    lax.fori_loop(0, n_chunks, chunk_step, 0)

    for h in range(N_HEADS_M):
        hs = slice(DH_M * h, DH_M * (h + 1))
        hh = hdir[0][:, hs] + hdir[1][:, hs]
        y = hh * lax.rsqrt(jnp.mean(hh * hh, axis=-1, keepdims=True) + EPS) * nw_ref[:, hs]
        hm_ref[:, hs] = (jax.nn.sigmoid(om_ref[:, hs]) * y).astype(BF16)
    if not has_state:
        c_ref, n_ref, m_ref = refs[n_in + 1:n_in + 4]
        for d in range(2):
            for h in range(N_HEADS_M):
                idx = 4 * d + h
                c_ref[d, h] = st[idx][:, 0:DH_M]
                n_ref[d, h:h + 1, :] = st[idx][:, DH_M:].T[0:1, :]
                m_ref[idx:idx + 1, :] = ms[idx][0:1, :]


def _mlstm(qm, km, vm, g3, bg, om, nw, state, *, n_batch, seq, first_block):
    n_chunks = seq // CHUNK
    rb = seq // TB
    tok = lambda: pl.BlockSpec((seq, 512), lambda b: (b + first_block // rb, 0))
    full = lambda a: pl.BlockSpec(a.shape, lambda b: (0,) * a.ndim)
    in_specs = [tok(), tok(), tok(),
                pl.BlockSpec((n_chunks, 16, CHUNK), lambda b: (b + first_block // rb, 0, 0)),
                full(bg), tok(), full(nw)]
    args = [qm, km, vm, g3, bg, om, nw]
    hm_spec = pl.BlockSpec((seq, 512), lambda b: (b, 0))
    hm_shape = jax.ShapeDtypeStruct((n_batch * seq, 512), BF16)
    scratch = ([pltpu.VMEM((DH_M, 2 * DH_M), F32)] * 8 + [pltpu.VMEM((8, CHUNK), F32)] * 8
               + [pltpu.VMEM((seq, 512), F32)] * 2
               + [pltpu.VMEM((2 * n_chunks, 32, CHUNK), F32),
                  pltpu.VMEM((8 * n_chunks, CHUNK, CHUNK), F32),
                  pltpu.VMEM((8 * n_chunks, CHUNK, CHUNK), F32),
                  pltpu.VMEM((N_HEADS_M * n_chunks, DH_M, CHUNK), BF16)])
    if state is not None:
        c0, n0, m0 = state
        in_specs += [pl.BlockSpec((None, None, 2, N_HEADS_M, DH_M, DH_M), lambda b: (b, 0, 0, 0, 0, 0)),
                     pl.BlockSpec((None, None, 2, N_HEADS_M, DH_M), lambda b: (b, 0, 0, 0, 0)),
                     pl.BlockSpec((None, 8, CHUNK), lambda b: (b, 0, 0))]
        args += [c0, n0, m0]
        out_specs, out_shape = hm_spec, hm_shape
    else:
        out_specs = [hm_spec,
                     pl.BlockSpec((None, 2, N_HEADS_M, DH_M, DH_M), lambda b: (b, 0, 0, 0, 0)),
                     pl.BlockSpec((None, 2, N_HEADS_M, DH_M), lambda b: (b, 0, 0, 0)),
                     pl.BlockSpec((None, 8, CHUNK), lambda b: (b, 0, 0))]
        out_shape = [hm_shape,
                     jax.ShapeDtypeStruct((n_batch, 2, N_HEADS_M, DH_M, DH_M), F32),
                     jax.ShapeDtypeStruct((n_batch, 2, N_HEADS_M, DH_M), F32),
                     jax.ShapeDtypeStruct((n_batch, 8, CHUNK), F32)]
    return pl.pallas_call(
        functools.partial(_mlstm_kernel, has_state=state is not None, n_chunks=n_chunks),
        grid=(n_batch,),
        in_specs=in_specs,
        out_specs=out_specs,
        out_shape=out_shape,
        scratch_shapes=scratch,
        compiler_params=_cparams(("arbitrary",)),
        name="mlstm_lat" if state is not None else "mlstm_ctx",
    )(*args)


def _outproj_kernel(attc_ref, attl_ref, hmc_ref, hml_ref, xc_ref, xl_ref, mod_ref, nw_ref, wo_ref,
                    rwh_ref, rwl_ref, wsg_ref, wsu_ref, wsd_ref, base_ref, hn_ref, lt_ref):
    y = (_dot(_ctx_or_lat(attc_ref, attl_ref), wo_ref[0:D_ATT, :])
         + _dot(_ctx_or_lat(hmc_ref, hml_ref), wo_ref[D_ATT:, :]))
    x1 = _ctx_or_lat(xc_ref, xl_ref) + mod_ref[2:3, :] * y
    z = x1 * lax.rsqrt(jnp.mean(x1 * x1, axis=-1, keepdims=True) + EPS) * nw_ref[...]
    hn = z * (1.0 + mod_ref[4:5, :]) + mod_ref[3:4, :]
    hb = hn.astype(BF16)
    hl = (hn - hb.astype(F32)).astype(BF16)
    nt = lambda w, t: lax.dot_general(w, t, _NT, preferred_element_type=F32)
    lt_ref[...] = nt(rwh_ref[...], hb) + nt(rwl_ref[...], hb) + nt(rwh_ref[...], hl)
    a = _silu(_dot(hb, wsg_ref[...])) * _dot(hb, wsu_ref[...])
    shared = _dot(a.astype(BF16), wsd_ref[...])
    base_ref[...] = x1 + mod_ref[5:6, :] * shared
    packed = pltpu.pack_elementwise([hn[:, :512], hn[:, 512:]], packed_dtype=BF16)
    for c in range(4):
        hn_ref[pl.ds(c, TBP, stride=4), :] = packed[:, 128 * c:128 * (c + 1)]


def _outproj(att, hm, x, mod3, norm2_w, w_out, rw_hi, rw_lo, wsg, wsu, wsd):
    tok = lambda w: pl.BlockSpec((TBP, w), lambda i: (i, 0))
    full = lambda a: pl.BlockSpec(a.shape, lambda i: (0,) * a.ndim)
    return pl.pallas_call(
        _outproj_kernel,
        grid=(T_ALL // TBP,),
        in_specs=_split_specs(512) + _split_specs(512) + _split_specs(D_MODEL) + [
                  pl.BlockSpec((None, 6, D_MODEL), lambda i: (_mod_row(i), 0, 0)),
                  full(norm2_w), full(w_out), full(rw_hi), full(rw_lo), full(wsg), full(wsu), full(wsd)],
        out_specs=[tok(D_MODEL), pl.BlockSpec((4 * TBP, 128), lambda i: (i, 0)),
                   pl.BlockSpec((N_EXPERTS, TBP), lambda i: (0, i))],
        out_shape=[jax.ShapeDtypeStruct((T_ALL, D_MODEL), F32),
                   jax.ShapeDtypeStruct((4 * T_ALL, 128), jnp.uint32),
                   jax.ShapeDtypeStruct((N_EXPERTS, T_ALL), F32)],
        compiler_params=_cparams(("arbitrary",)),
        name="outproj",
    )(*att, *hm, *x, mod3, norm2_w, w_out, rw_hi, rw_lo, wsg, wsu, wsd)


def _first_max(vals, ids, limit):
    m = functools.reduce(jnp.maximum, [jnp.max(v, axis=0, keepdims=True) for v in vals])
    cand = [jnp.min(jnp.where(v == m, i, limit), axis=0, keepdims=True) for v, i in zip(vals, ids)]
    return m, functools.reduce(jnp.minimum, cand)


def _router_kernel(lt_ref, bias_ref, gate_ref, idx_ref):
    n = lt_ref.shape[1]
    score = jax.nn.sigmoid(lt_ref[...])
    biased = score + bias_ref[...]
    sub = lax.broadcasted_iota(jnp.int32, (8, n), 0).astype(F32)
    neg_inf = F32(-jnp.inf)
    slabs = [biased[8 * g:8 * (g + 1), :] for g in range(N_GROUPS)]
    gs = []
    for sl in slabs:
        m1, i1 = _first_max([sl], [sub], 8.0)
        m2 = jnp.max(jnp.where(sub == i1, neg_inf, sl), axis=0, keepdims=True)
        gs.append(m1 + m2)
    cur = jnp.concatenate(gs, axis=0)
    gsel = jnp.zeros((8, n), F32)
    for _ in range(TOPK_GROUPS):
        _, i = _first_max([cur], [sub], 8.0)
        hit = sub == i
        gsel = jnp.where(hit, 1.0, gsel)
        cur = jnp.where(hit, neg_inf, cur)
    vals = [jnp.where(gsel[g:g + 1, :] > 0.0, slabs[g], neg_inf) for g in range(N_GROUPS)]
    ids = [sub + 8.0 * g for g in range(N_GROUPS)]
    picked = [jnp.zeros((8, n), F32) for _ in range(N_GROUPS)]
    order = []
    for _ in range(TOP_K):
        _, i = _first_max(vals, ids, float(N_EXPERTS))
        order.append(i)
        hits = [idg == i for idg in ids]
        picked = [jnp.where(hh, score[8 * g:8 * (g + 1), :], p) for g, (p, hh) in enumerate(zip(picked, hits))]
        vals = [jnp.where(hh, neg_inf, v) for v, hh in zip(vals, hits)]
    total = functools.reduce(jnp.add, [jnp.sum(p, axis=0, keepdims=True) for p in picked])
    gate_t = jnp.concatenate([p / total * ROUTED_SCALE for p in picked]
                             + [jnp.zeros((128 - N_EXPERTS, n), F32)], axis=0)
    gate_ref[...] = gate_t.T
    idx_ref[...] = jnp.concatenate(order, axis=0).astype(jnp.int32)


def _router(logits_t, bias_col):
    return pl.pallas_call(
        _router_kernel,
        grid=(T_ALL // TB_MOE,),
        in_specs=[pl.BlockSpec((N_EXPERTS, TB_MOE), lambda i: (0, i)),
                  pl.BlockSpec((N_EXPERTS, 1), lambda i: (0, 0))],
        out_specs=[pl.BlockSpec((TB_MOE, 128), lambda i: (i, 0)),
                   pl.BlockSpec((TOP_K, TB_MOE), lambda i: (0, i))],
        out_shape=[jax.ShapeDtypeStruct((T_ALL, 128), F32),
                   jax.ShapeDtypeStruct((TOP_K, T_ALL), jnp.int32)],
        compiler_params=_cparams(("arbitrary",)),
        name="router",
    )(logits_t, bias_col)


def _plan_kernel(idx_ref, slot_ref, cnt_ref, off_ref, pos_sc):
    n_tiles = T_HALF // 128
    eid = lax.broadcasted_iota(jnp.int32, (N_EXPERTS, 128), 0)
    tri = (lax.broadcasted_iota(jnp.int32, (128, 128), 0)
           <= lax.broadcasted_iota(jnp.int32, (128, 128), 1)).astype(BF16)
    carry = jnp.zeros((N_EXPERTS, 1), F32)
    for j in range(n_tiles):
        it = idx_ref[:, 128 * j:128 * (j + 1)]
        sel = jnp.zeros((N_EXPERTS, 128), F32)
        for k in range(TOP_K):
            sel = jnp.where(it[k:k + 1, :] == eid, 1.0, sel)
        inc = _dot(sel.astype(BF16), tri) + carry
        carry = inc[:, 127:128]
        pos_sc[:, 128 * j:128 * (j + 1)] = inc - 1.0
    count = jnp.broadcast_to(carry, (N_EXPERTS, 128))
    padded = jnp.floor((count + (MT - 1.0)) * (1.0 / MT)) * MT
    before = (lax.broadcasted_iota(jnp.int32, (N_EXPERTS, N_EXPERTS), 1)
              < lax.broadcasted_iota(jnp.int32, (N_EXPERTS, N_EXPERTS), 0)).astype(BF16)
    hi, mid, lo = _split3(padded)
    off = _dot(before, hi) + _dot(before, mid) + _dot(before, lo)
    cnt_ref[...] = count.astype(jnp.int32)
    off_ref[...] = off.astype(jnp.int32)

    def slots(j):
        it = idx_ref[:, 128 * j:128 * (j + 1)]
        val = off + pos_sc[:, 128 * j:128 * (j + 1)]
        rows = [jnp.sum(jnp.where(it[k:k + 1, :] == eid, val, 0.0), axis=0, keepdims=True)
                for k in range(TOP_K)]
        return jnp.concatenate(rows, axis=0).astype(jnp.int32)

    for j in range(n_tiles // 2):
        slot_ref[:, 128 * j:128 * (j + 1)] = slots(j) | (slots(j + n_tiles // 2) << 16)


def _plan(idx8):
    return pl.pallas_call(
        _plan_kernel,
        grid=(2,),
        in_specs=[pl.BlockSpec((TOP_K, T_HALF), lambda h: (0, h))],
        out_specs=[pl.BlockSpec((None, TOP_K, T_HALF // 2), lambda h: (h, 0, 0)),
                   pl.BlockSpec((None, N_EXPERTS, 128), lambda h: (h, 0, 0)),
                   pl.BlockSpec((None, N_EXPERTS, 128), lambda h: (h, 0, 0))],
        out_shape=[jax.ShapeDtypeStruct((2, TOP_K, T_HALF // 2), jnp.int32),
                   jax.ShapeDtypeStruct((2, N_EXPERTS, 128), jnp.int32),
                   jax.ShapeDtypeStruct((2, N_EXPERTS, 128), jnp.int32)],
        scratch_shapes=[pltpu.VMEM((N_EXPERTS, T_HALF), F32)],
        compiler_params=_cparams(("arbitrary",)),
        name="plan",
    )(idx8)


def _prepare_half(slot_ref, cnt_ref, off_ref, tok_ref, tile_e, tile_first, elist, h):
    def per_expert(x, carry):
        j, q = carry
        n = cnt_ref[h * N_EXPERTS + x]
        first = off_ref[h * N_EXPERTS + x]
        tiles = (n + MT - 1) // MT
        elist[q] = x

        def mark(i, c):
            tile_e[j + i] = x
            tile_first[j + i] = jnp.where(i == 0, 1, 0)
            return c

        lax.fori_loop(0, tiles, mark, 0)

        @pl.when(tiles > 0)
        def _():
            last = first + (tiles - 1) * MT
            for u in range(MT):
                tok_ref[last + u] = T_HALF

        return j + tiles, q + jnp.where(tiles > 0, 1, 0)

    n_tiles, n_live = lax.fori_loop(0, N_EXPERTS, per_expert, (0, 0))
    for extra in range(2):
        tile_e[n_tiles + extra] = 0
        tile_first[n_tiles + extra] = 0
        for u in range(MT):
            tok_ref[(n_tiles + extra) * MT + u] = T_HALF
    return n_tiles, n_live


def _invert_slots(slot_ref, tok_ref, acc_v, h):
    half = T_HALF // 2
    trips = half // INV_UNROLL
    zero_rows = 8 * T_HALF // (TOP_K * trips)
    assert zero_rows * TOP_K * trips == 8 * T_HALF and zero_rows % 8 == 0
    for k in range(TOP_K):
        def body(i, carry, k=k):
            words = [slot_ref[(h * TOP_K + k) * half + i * INV_UNROLL + u] for u in range(INV_UNROLL)]
            for u in range(INV_UNROLL):
                t = i * INV_UNROLL + u
                tok_ref[words[u] & 0xFFFF] = t
                tok_ref[lax.shift_right_logical(words[u], 16)] = t + half
            row = pl.multiple_of((k * trips + i) * zero_rows, 8)
            acc_v[pl.ds(row, zero_rows), :] = jnp.zeros((zero_rows, 128), F32)
            return carry

        lax.fori_loop(0, trips, body, 0)
    acc_v[pl.ds(8 * T_HALF, 8), :] = jnp.zeros((8, 128), F32)


def _moe_kernel(slot_ref, cnt_ref, off_ref, src_hbm, gate_hbm, wg_hbm, wu_hbm, wd_hbm, base_hbm, mod_ref,
                outc_hbm, outl_hbm,
                src_v, gate_v, acc_v, wg_l, wu_l, wd_l, wgu_b, wd_b, xbuf0, xbuf1, gbuf0, gbuf1,
                ybuf0, ybuf1, tok_ref, tile_e, tile_first, elist, live, base_buf, out_buf, sem, wsem, fsem):
    h = pl.program_id(0)
    xbuf, gbuf, ybuf = (xbuf0, xbuf1), (gbuf0, gbuf1), (ybuf0, ybuf1)

    src_cp = pltpu.make_async_copy(src_hbm.at[pl.ds(pl.multiple_of(h * (4 * T_HALF), 8), 4 * T_HALF)],
                                   src_v.at[pl.ds(0, 4 * T_HALF)], sem.at[0])
    gate_cp = pltpu.make_async_copy(gate_hbm.at[pl.ds(pl.multiple_of(h * T_HALF, 8), T_HALF)],
                                    gate_v.at[pl.ds(0, T_HALF)], sem.at[1])
    src_cp.start()
    gate_cp.start()

    def weight_copies(x, slot):
        return [pltpu.make_async_copy(w_hbm.at[x], w_l.at[slot], wsem.at[slot, i])
                for i, (w_hbm, w_l) in enumerate(((wg_hbm, wg_l), (wu_hbm, wu_l), (wd_hbm, wd_l)))]

    n_tiles, n_live = _prepare_half(slot_ref, cnt_ref, off_ref, tok_ref, tile_e, tile_first, elist, h)
    live[0] = 0
    live[1] = n_live

    for ahead in range(W_SLOTS - 1):
        @pl.when(ahead < n_live)
        def _(ahead=ahead):
            for cp in weight_copies(elist[ahead], ahead):
                cp.start()

    zero = jnp.zeros((8, 128), F32)
    src_v[pl.ds(4 * T_HALF, 8), :] = pltpu.pack_elementwise([zero, zero], packed_dtype=BF16)
    gate_v[pl.ds(T_HALF, 8), :] = zero
    ybuf0[...] = jnp.zeros(ybuf0.shape, F32)
    ybuf1[...] = jnp.zeros(ybuf1.shape, F32)
    _invert_slots(slot_ref, tok_ref, acc_v, h)
    src_cp.wait()
    gate_cp.wait()

    def switch_expert():
        q = live[0]
        slot = q % W_SLOTS
        for cp in weight_copies(elist[q], slot):
            cp.wait()
        wgu_b[:, 0:D_EXPERT] = wg_l[slot].astype(BF16)
        wgu_b[:, D_EXPERT:] = wu_l[slot].astype(BF16)
        wd_b[...] = wd_l[slot].astype(BF16)
        live[0] = q + 1
        nxt = q + W_SLOTS - 1

        @pl.when(nxt < live[1])
        def _():
            for cp in weight_copies(elist[nxt], nxt % W_SLOTS):
                cp.start()

    def gather(j, xb, gb, rows=(0, MT)):
        base = j * MT
        for m in range(*rows):
            t = tok_ref[base + m]
            xb[pl.ds(m, 4, stride=MT_STRIDE), :] = src_v[pl.ds(pl.multiple_of(t * 4, 4), 4), :]
            gb[m:m + 1, :] = gate_v[pl.ds(t, 1), :]

    def scatter(j, yb, rows=(0, MT)):
        base = j * MT
        for b in range(rows[0] // RMW_BATCH, rows[1] // RMW_BATCH):
            ms = [b * RMW_BATCH + u for u in range(RMW_BATCH)]
            targets = [pl.ds(pl.multiple_of(tok_ref[base + m] * 8, 8), 8) for m in ms]
            vals = [acc_v[r, :] + yb[pl.ds(m, 8, stride=MT_STRIDE), :] for r, m in zip(targets, ms)]
            for r, v in zip(targets, vals):
                acc_v[r, :] = v

    quarters = [(q * MT // 4, (q + 1) * MT // 4) for q in range(4)]

    def step(j, p):
        pl.when(tile_first[j] == 1)(switch_expert)
        xb, gb, yb = xbuf[p], gbuf[p], ybuf[p]
        nxt = (j + 1, xbuf[1 - p], gbuf[1 - p])
        prv = (jnp.maximum(j - 1, 0), ybuf[1 - p])
        lo, hi = [], []
        for c in range(4):
            words = xb[MT_STRIDE * c:MT_STRIDE * c + MT, :]
            unpack = functools.partial(pltpu.unpack_elementwise, words, packed_dtype=BF16, unpacked_dtype=F32)
            lo.append(unpack(index=0).astype(BF16))
            hi.append(unpack(index=1).astype(BF16))
        x = jnp.concatenate(lo + hi, axis=1)
        gather(*nxt, rows=quarters[0])
        h_gate = _dot(x, wgu_b[:, 0:D_EXPERT])
        gather(*nxt, rows=quarters[1])
        h_up = _dot(x, wgu_b[:, D_EXPERT:])
        gather(*nxt, rows=quarters[2])
        g = gb[...]
        g_hi = g.astype(BF16)
        g_lo = (g - g_hi.astype(F32)).astype(BF16)
        pick = (lax.broadcasted_iota(jnp.int32, (128, D_EXPERT), 0) == tile_e[j]).astype(BF16)
        g_col = _dot(g_hi, pick) + _dot(g_lo, pick)
        a = (_silu(h_gate) * h_up * g_col).astype(BF16)
        gather(*nxt, rows=quarters[3])
        scatter(*prv, rows=quarters[0])
        y_lo = _dot(a, wd_b[:, 0:D_MODEL // 2])
        scatter(*prv, rows=quarters[1])
        y_hi = _dot(a, wd_b[:, D_MODEL // 2:])
        scatter(*prv, rows=quarters[2])
        for c in range(4):
            yb[MT_STRIDE * c:MT_STRIDE * c + MT, :] = y_lo[:, 128 * c:128 * (c + 1)]
            yb[MT_STRIDE * (c + 4):MT_STRIDE * (c + 4) + MT, :] = y_hi[:, 128 * c:128 * (c + 1)]
        scatter(*prv, rows=quarters[3])

    gather(0, xbuf[0], gbuf[0])
    n_pairs = (n_tiles + 1) // 2

    def pair(i, carry):
        step(2 * i, 0)
        step(2 * i + 1, 1)
        return carry

    lax.fori_loop(0, n_pairs, pair, 0)
    scatter(jnp.maximum(2 * n_pairs - 1, 0), ybuf[1])

    n_fin = T_HALF // FIN

    def first_token(c):
        return h * T_HALF + c * FIN

    def base_copy(c, slot):
        return pltpu.make_async_copy(base_hbm.at[pl.ds(pl.multiple_of(first_token(c), FIN), FIN)],
                                     base_buf.at[slot], fsem.at[0, slot])

    def on_out_copy(c, action):
        t0 = first_token(c)
        slot = c % FIN_OUT_SLOTS

        @pl.when(t0 < T_CTX)
        def _():
            action(pltpu.make_async_copy(out_buf.at[slot], outc_hbm.at[pl.ds(pl.multiple_of(t0, FIN), FIN)],
                                         fsem.at[1, slot]))

        @pl.when(t0 >= T_CTX)
        def _():
            action(pltpu.make_async_copy(out_buf.at[slot],
                                         outl_hbm.at[pl.ds(pl.multiple_of(t0 - T_CTX, FIN), FIN)],
                                         fsem.at[1, slot]))

    for c in range(FIN_BASE_SLOTS - 1):
        base_copy(c, c).start()

    def combine(c, carry):
        slot = c % FIN_BASE_SLOTS
        oslot = c % FIN_OUT_SLOTS
        base_copy(c, slot).wait()
        ahead = c + FIN_BASE_SLOTS - 1

        @pl.when(ahead < n_fin)
        def _():
            base_copy(ahead, ahead % FIN_BASE_SLOTS).start()

        @pl.when(c >= FIN_OUT_SLOTS)
        def _():
            on_out_copy(c - FIN_OUT_SLOTS, lambda cp: cp.wait())

        t0 = first_token(c)
        mod_row = jnp.where(t0 < T_CTX, 0, 1 + (t0 - T_CTX) // DEC_SEQ)
        acc_row = pl.multiple_of(8 * FIN * c, 8)
        for cc in range(8):
            cs = slice(128 * cc, 128 * (cc + 1))
            routed = acc_v[pl.ds(acc_row + cc, FIN, stride=8), :]
            out_buf[oslot, :, cs] = base_buf[slot, :, cs] + mod_ref[mod_row, 5:6, cs] * routed
        on_out_copy(c, lambda cp: cp.start())
        return carry

    lax.fori_loop(0, n_fin, combine, 0)
    for c in range(n_fin - FIN_OUT_SLOTS, n_fin):
        on_out_copy(c, lambda cp: cp.wait())


def _moe(slot_words, cnt, off, src, gate, wg, wu, wd, base, mod3):
    any_spec = pl.BlockSpec(memory_space=pl.ANY)
    tile_buf = lambda rows, dt: pltpu.VMEM((rows * MT_STRIDE, 128), dt)
    return pl.pallas_call(
        _moe_kernel,
        grid_spec=pltpu.PrefetchScalarGridSpec(
            num_scalar_prefetch=3,
            grid=(2,),
            in_specs=[any_spec] * 6 + [pl.BlockSpec(mod3.shape, lambda h, *_: (0, 0, 0))],
            out_specs=[any_spec, any_spec],
            scratch_shapes=[pltpu.VMEM((4 * T_HALF + 8, 128), jnp.uint32),
                            pltpu.VMEM((T_HALF + 8, 128), F32),
                            pltpu.VMEM((8 * T_HALF + 8, 128), F32),
                            pltpu.VMEM((W_SLOTS, D_MODEL, D_EXPERT), F32),
                            pltpu.VMEM((W_SLOTS, D_MODEL, D_EXPERT), F32),
                            pltpu.VMEM((W_SLOTS, D_EXPERT, D_MODEL), F32),
                            pltpu.VMEM((D_MODEL, 2 * D_EXPERT), BF16),
                            pltpu.VMEM((D_EXPERT, D_MODEL), BF16),
                            tile_buf(4, jnp.uint32), tile_buf(4, jnp.uint32),
                            pltpu.VMEM((MT, 128), F32), pltpu.VMEM((MT, 128), F32),
                            tile_buf(8, F32), tile_buf(8, F32),
                            pltpu.SMEM((SLOT_CAP,), jnp.int32),
                            pltpu.SMEM((NT_MAX + 2,), jnp.int32),
                            pltpu.SMEM((NT_MAX + 2,), jnp.int32),
                            pltpu.SMEM((N_EXPERTS,), jnp.int32),
                            pltpu.SMEM((2,), jnp.int32),
                            pltpu.VMEM((FIN_BASE_SLOTS, FIN, D_MODEL), F32),
                            pltpu.VMEM((FIN_OUT_SLOTS, FIN, D_MODEL), F32),
                            pltpu.SemaphoreType.DMA((2,)),
                            pltpu.SemaphoreType.DMA((W_SLOTS, 3)),
                            pltpu.SemaphoreType.DMA((2, FIN_BASE_SLOTS))]),
        out_shape=[jax.ShapeDtypeStruct((T_CTX, D_MODEL), F32), jax.ShapeDtypeStruct((T_LAT, D_MODEL), F32)],
        compiler_params=_cparams(("arbitrary",)),
        name="moe",
    )(slot_words, cnt, off, src, gate, wg, wu, wd, base, mod3)


def _head_indicators(width):
    head = jnp.arange(width) // DH_ATT
    ind = (head[:, None] == jnp.arange(128)[None, :]).astype(BF16)
    return ind, ind.T


def kernel(x_prompt, x_sample, cache_attn_k, cache_attn_v, state_mlstm_c, state_mlstm_n, state_mlstm_m, c, c_ctx, w_mod, b_mod, norm1_w, norm2_w, w_in, q_norm_w, k_norm_w, b_gates, m_norm_w, w_out, router_w, router_bias, w_gate, w_up, w_down, ws_gate, ws_up, ws_down):
    x = (x_prompt.reshape(T_CTX, D_MODEL), x_sample.reshape(T_LAT, D_MODEL))
    mod3 = _modulation(c, c_ctx, w_mod[0], b_mod[0])

    w_main = w_in[0, :, :P_MAIN].astype(BF16)
    w_gates_t = w_in[0, :, P_MAIN:].T.astype(BF16)
    qw = jnp.tile(q_norm_w, (1, N_HEADS_ATT))
    kw = jnp.tile(k_norm_w, (1, N_KV_HEADS))
    inds = _head_indicators(512) + _head_indicators(128)
    qn, kn, va, qm, km, vm, om, gt = _inproj(*x, mod3, norm1_w, w_main, w_gates_t, qw, kw, inds, _rope_tables())

    att_c = _attention(qn, kn, va, None, n_batch=BATCH, seq=SEQ, first_row=0)
    att_l = _attention(qn, kn, va, (cache_attn_k, cache_attn_v), n_batch=DEC_BATCH, seq=DEC_SEQ,
                       first_row=T_CTX)

    g3 = gt.reshape(16, T_ALL // CHUNK, CHUNK).transpose(1, 0, 2)
    bg = b_gates.reshape(16, 1)
    hm_c, c_new, n_new, m_new = _mlstm(qm, km, vm, g3, bg, om, m_norm_w, None,
                                       n_batch=BATCH, seq=SEQ, first_block=0)
    m0 = jnp.broadcast_to(state_mlstm_m.reshape(DEC_BATCH, 8, 1), (DEC_BATCH, 8, CHUNK))
    hm_l = _mlstm(qm, km, vm, g3, bg, om, m_norm_w, (state_mlstm_c, state_mlstm_n, m0),
                  n_batch=DEC_BATCH, seq=DEC_SEQ, first_block=N_CTX_BLOCKS)

    rw_t = router_w[0].T
    rw_hi = rw_t.astype(BF16)
    rw_lo = (rw_t - rw_hi.astype(F32)).astype(BF16)
    base, hn2, logits_t = _outproj((att_c, att_l), (hm_c, hm_l), x, mod3, norm2_w, w_out[0].astype(BF16),
                                   rw_hi, rw_lo, ws_gate[0].astype(BF16), ws_up[0].astype(BF16),
                                   ws_down[0].astype(BF16))
    gate, idx8 = _router(logits_t, router_bias.reshape(N_EXPERTS, 1))
    slot_words, cnt, off = _plan(idx8)
    out_c, out_l = _moe(slot_words.reshape(TOP_K * T_HALF), cnt[:, :, 0].reshape(2 * N_EXPERTS),
                        off[:, :, 0].reshape(2 * N_EXPERTS), hn2, gate, w_gate[0], w_up[0], w_down[0],
                        base, mod3)

    y_prompt = out_c.reshape(BATCH, SEQ, D_MODEL)
    y_sample = out_l.reshape(DEC_BATCH, DEC_SEQ, D_MODEL)
    to_cache = lambda a: a[:T_CTX].reshape(BATCH, SEQ, N_KV_HEADS, DH_ATT).transpose(0, 2, 1, 3)[:, None]
    new_k = to_cache(kn)
    new_v = to_cache(va)
    new_m = m_new[:, :, 0].reshape(BATCH, 1, 2, N_HEADS_M)
    return (y_prompt, y_sample, new_k, new_v, c_new[:, None], n_new[:, None], new_m)
```

```python
import functools

import jax
import jax.numpy as jnp
from jax import lax
from jax.experimental import pallas as pl
from jax.experimental.pallas import tpu as pltpu

F32 = jnp.float32
BF16 = jnp.bfloat16

D_MODEL = 1024
BATCH = 32
SEQ = 256
DEC_BATCH = 2
DEC_SEQ = 1024
PAST_LEN = 256
GRID_W = 64
N_HEADS_ATT = 8
N_KV_HEADS = 2
DH_ATT = 64
D_ATT = 512
ROPE_THETA = 10000.0
N_HEADS_M = 4
DH_M = 128
D_M = 512
CHUNK = 128
N_EXPERTS = 64
TOP_K = 8
N_GROUPS = 8
TOPK_GROUPS = 4
D_EXPERT = 256
ROUTED_SCALE = 2.5
EPS = 1e-6
NEG_INIT = -1e30

T_CTX = BATCH * SEQ
T_LAT = DEC_BATCH * DEC_SEQ
T_ALL = T_CTX + T_LAT
TB = 256
N_CTX_BLOCKS = T_CTX // TB
TBP = 512
NP_CTX = T_CTX // TBP
NP_LAT_PER_BATCH = DEC_SEQ // TBP
TB_MOE = 1024
T_HALF = T_ALL // 2
MT = 256
MT_STRIDE = MT + 8
NT_MAX = TOP_K * T_HALF // MT + N_EXPERTS
SLOT_CAP = (NT_MAX + 2) * MT
RMW_BATCH = 16
INV_UNROLL = 16
W_SLOTS = 2
FIN = 128
FIN_BASE_SLOTS = 6
FIN_OUT_SLOTS = 4
assert FIN_OUT_SLOTS <= FIN_BASE_SLOTS
assert SLOT_CAP < 2 ** 16 and T_HALF % INV_UNROLL == 0 and MT % RMW_BATCH == 0
P_MAIN = 2816
VMEM_LIMIT = 56 * 1024 * 1024

_NT = (((1,), (1,)), ((), ()))
_TN = (((0,), (0,)), ((), ()))


def _cparams(sem):
    return pltpu.CompilerParams(dimension_semantics=sem, vmem_limit_bytes=VMEM_LIMIT)


def _split3(x):
    hi = x.astype(BF16)
    r1 = x - hi.astype(F32)
    mid = r1.astype(BF16)
    lo = (r1 - mid.astype(F32)).astype(BF16)
    return hi, mid, lo


def _dot(a, b):
    return jnp.dot(a, b, preferred_element_type=F32)


def _dot3(x, m_bf16):
    hi, mid, lo = _split3(x)
    return _dot(hi, m_bf16) + _dot(mid, m_bf16) + _dot(lo, m_bf16)


def _dot2(x, m_bf16):
    hi = x.astype(BF16)
    lo = (x - hi.astype(F32)).astype(BF16)
    return _dot(hi, m_bf16) + _dot(lo, m_bf16)


def _silu(x):
    return x * jax.nn.sigmoid(x)


def _mod_row(i):
    return jnp.where(i < NP_CTX, 0, 1 + (i - NP_CTX) // NP_LAT_PER_BATCH)


def _mod_kernel(ct_ref, w_ref, b_ref, o_ref):
    s = _silu(ct_ref[...])
    w = w_ref[...]
    rows = [jnp.sum(w * s[:, r:r + 1], axis=0, keepdims=True) for r in range(3)]
    rows.append(jnp.zeros((5, w.shape[1]), F32))
    o_ref[...] = jnp.concatenate(rows, axis=0) + b_ref[...]


def _modulation(c, c_ctx, w_mod, b_mod):
    cvec = jnp.concatenate([c_ctx[None, :], c, jnp.zeros((5, D_MODEL), F32)], axis=0)
    nb = 1024
    out = pl.pallas_call(
        _mod_kernel,
        grid=(6 * D_MODEL // nb,),
        in_specs=[pl.BlockSpec((D_MODEL, 8), lambda j: (0, 0)),
                  pl.BlockSpec((D_MODEL, nb), lambda j: (0, j)),
                  pl.BlockSpec((1, nb), lambda j: (0, j))],
        out_specs=pl.BlockSpec((8, nb), lambda j: (0, j)),
        out_shape=jax.ShapeDtypeStruct((8, 6 * D_MODEL), F32),
        compiler_params=_cparams(("arbitrary",)),
        name="modulation",
    )(cvec.T, w_mod, b_mod[None, :])
    return out.reshape(8, 6, D_MODEL)


def _ctx_or_lat(ctx_ref, lat_ref):
    return jnp.where(pl.program_id(0) < NP_CTX, ctx_ref[...], lat_ref[...])


def _split_specs(width):
    return [pl.BlockSpec((TBP, width), lambda i: (jnp.minimum(i, NP_CTX - 1), 0)),
            pl.BlockSpec((TBP, width), lambda i: (jnp.maximum(i - NP_CTX, 0), 0))]


def _head_norm(x, ind, ind_t, w_row):
    ss = _dot2(x * x, ind)
    inv = lax.rsqrt(ss * (1.0 / DH_ATT) + EPS)
    return x * _dot2(inv, ind_t) * w_row


def _rope(x, cos, sin_signed):
    lane = lax.broadcasted_iota(jnp.int32, x.shape, 1)
    partner = jnp.where((lane % 32) < 16, pltpu.roll(x, 128 - 16, 1), pltpu.roll(x, 16, 1))
    return x * cos + partner * sin_signed


def _inproj_kernel(xc_ref, xl_ref, mod_ref, nw_ref, w_ref, wgt_ref, qw_ref, kw_ref, iq_ref, iqt_ref,
                   ik_ref, ikt_ref, cos_ref, sin_ref,
                   qn_ref, kn_ref, va_ref, qm_ref, km_ref, vm_ref, om_ref, gt_ref):
    x = _ctx_or_lat(xc_ref, xl_ref)
    y = x * lax.rsqrt(jnp.mean(x * x, axis=-1, keepdims=True) + EPS) * nw_ref[...]
    hn = y * (1.0 + mod_ref[1:2, :]) + mod_ref[0:1, :]
    hb = hn.astype(BF16)
    qn = _head_norm(_dot(hb, w_ref[:, 0:512]), iq_ref[...], iqt_ref[...], qw_ref[...])
    kn = _head_norm(_dot(hb, w_ref[:, 512:640]), ik_ref[...], ikt_ref[...], kw_ref[...])
    qn_ref[...] = qn.astype(BF16)
    kn_ref[...] = kn

    @pl.when(pl.program_id(0) >= NP_CTX)
    def _():
        cos, sin = cos_ref[...], sin_ref[...]
        for j in range(4):
            qn_ref[:, 128 * j:128 * (j + 1)] = _rope(qn[:, 128 * j:128 * (j + 1)], cos, sin).astype(BF16)
        kn_ref[...] = _rope(kn, cos, sin)

    va_ref[...] = _dot(hb, w_ref[:, 640:768])
    qm_ref[...] = _dot(hb, w_ref[:, 768:1280]).astype(BF16)
    km_ref[...] = (_dot(hb, w_ref[:, 1280:1792]) * (DH_M ** -0.5)).astype(BF16)
    vm_ref[...] = _dot(hb, w_ref[:, 1792:2304]).astype(BF16)
    om_ref[...] = _dot(hb, w_ref[:, 2304:2816])
    gt_ref[...] = lax.dot_general(wgt_ref[...], hb, _NT, preferred_element_type=F32)


def _inproj(x_ctx, x_lat, mod3, norm1_w, w_main, w_gates_t, qw, kw, inds, rope_tabs):
    tok = lambda w: pl.BlockSpec((TBP, w), lambda i: (i, 0))
    full = lambda a: pl.BlockSpec(a.shape, lambda i: (0,) * a.ndim)
    sd = lambda w, dt: jax.ShapeDtypeStruct((T_ALL, w), dt)
    rope_spec = pl.BlockSpec((TBP, 128), lambda i: (jnp.maximum(i - NP_CTX, 0) % NP_LAT_PER_BATCH, 0))
    return pl.pallas_call(
        _inproj_kernel,
        grid=(T_ALL // TBP,),
        in_specs=_split_specs(D_MODEL) + [
                  pl.BlockSpec((None, 6, D_MODEL), lambda i: (_mod_row(i), 0, 0)),
                  full(norm1_w), full(w_main), full(w_gates_t), full(qw), full(kw)]
                 + [full(a) for a in inds] + [rope_spec, rope_spec],
        out_specs=[tok(512), tok(128), tok(128), tok(512), tok(512), tok(512), tok(512),
                   pl.BlockSpec((16, TBP), lambda i: (0, i))],
        out_shape=[sd(512, BF16), sd(128, F32), sd(128, F32), sd(512, BF16), sd(512, BF16),
                   sd(512, BF16), sd(512, F32), jax.ShapeDtypeStruct((16, T_ALL), F32)],
        compiler_params=_cparams(("arbitrary",)),
        name="inproj",
    )(x_ctx, x_lat, mod3, norm1_w, w_main, w_gates_t, qw, kw, *inds, *rope_tabs)


def _rope_tables():
    t = jnp.arange(DEC_SEQ)
    pos = jnp.stack([t // GRID_W, t % GRID_W], axis=1).astype(F32)
    n_freq = DH_ATT // 4
    inv_freq = ROPE_THETA ** (-jnp.arange(n_freq, dtype=F32) / n_freq)
    ang = pos[:, :, None] * inv_freq
    cos, sin = jnp.cos(ang), jnp.sin(ang)
    cos_h = jnp.stack([cos, cos], axis=2).reshape(DEC_SEQ, DH_ATT)
    sin_h = jnp.stack([-sin, sin], axis=2).reshape(DEC_SEQ, DH_ATT)
    return jnp.tile(cos_h, (1, 2)), jnp.tile(sin_h, (1, 2))


def _attn_kernel(*refs, has_cache):
    if has_cache:
        q_ref, k_ref, v_ref, kc_ref, vc_ref, o_ref = refs
    else:
        q_ref, k_ref, v_ref, o_ref = refs
    q = q_ref[...] * jnp.asarray(DH_ATT ** -0.5, BF16)
    k = k_ref[...].astype(BF16)
    v = v_ref[...].astype(BF16)
    qb = q.shape[0]
    low_half = lax.broadcasted_iota(jnp.int32, (1, 128), 1) < DH_ATT
    for g in range(N_KV_HEADS):
        kg = k[:, DH_ATT * g:DH_ATT * (g + 1)]
        vg = v[:, DH_ATT * g:DH_ATT * (g + 1)]
        if has_cache:
            kg = jnp.concatenate([kg, kc_ref[g].astype(BF16)], axis=0)
            vg = jnp.concatenate([vg, vc_ref[g].astype(BF16)], axis=0)
        n_keys = kg.shape[0]
        zero = jnp.zeros_like(kg)
        k2 = jnp.concatenate([jnp.concatenate([kg, zero], axis=1), jnp.concatenate([zero, kg], axis=1)], axis=0)
        v2 = jnp.concatenate([jnp.concatenate([vg, zero], axis=1), jnp.concatenate([zero, vg], axis=1)], axis=0)
        ones2 = jnp.concatenate([jnp.broadcast_to(jnp.where(low_half, 1.0, 0.0), (n_keys, 128)),
                                 jnp.broadcast_to(jnp.where(low_half, 0.0, 1.0), (n_keys, 128))],
                                axis=0).astype(BF16)
        qp = jnp.concatenate([q[:, 256 * g:256 * g + 128], q[:, 256 * g + 128:256 * g + 256]], axis=0)
        s = lax.dot_general(qp, k2, _NT, preferred_element_type=F32)
        halves = [s[:, 0:n_keys], s[:, n_keys:]]
        p = jnp.concatenate([jnp.exp(sh - jnp.max(sh, axis=-1, keepdims=True)) for sh in halves],
                            axis=1).astype(BF16)
        ov = _dot(p, jnp.concatenate([v2, ones2], axis=1))
        o = ov[:, 0:128] / ov[:, 128:]
        o_ref[:, 256 * g:256 * g + 128] = o[0:qb, :].astype(BF16)
        o_ref[:, 256 * g + 128:256 * g + 256] = o[qb:, :].astype(BF16)


def _attention(qn, kn, v_all, cache, *, n_batch, seq, first_row):
    qblocks = seq // TB
    kv_spec = pl.BlockSpec((seq, 128), lambda b, i: (b + first_row // seq, 0))
    in_specs = [pl.BlockSpec((TB, 512), lambda b, i: (first_row // TB + b * qblocks + i, 0)), kv_spec, kv_spec]
    args = [qn, kn, v_all]
    if cache is not None:
        cspec = pl.BlockSpec((None, None, N_KV_HEADS, PAST_LEN, DH_ATT), lambda b, i: (b, 0, 0, 0, 0))
        in_specs += [cspec, cspec]
        args += list(cache)
    return pl.pallas_call(
        functools.partial(_attn_kernel, has_cache=cache is not None),
        grid=(n_batch, qblocks),
        in_specs=in_specs,
        out_specs=pl.BlockSpec((TB, 512), lambda b, i: (b * qblocks + i, 0)),
        out_shape=jax.ShapeDtypeStruct((n_batch * seq, 512), BF16),
        compiler_params=_cparams(("arbitrary", "arbitrary")),
        name="attention_lat" if cache is not None else "attention_ctx",
    )(*args)


def _log_sigmoid(x):
    return jnp.minimum(x, 0.0) - jnp.log1p(jnp.exp(-jnp.abs(x)))


def _col_bcast(cols, j):
    return jnp.broadcast_to(cols[:, j:j + 1], (CHUNK, CHUNK))


def _mlstm_kernel(*refs, has_state, n_chunks):
    n_in = 10 if has_state else 7
    n_out = 1 if has_state else 4
    q_ref, k_ref, v_ref, g_ref, bg_ref, om_ref, nw_ref = refs[:7]
    hm_ref = refs[n_in]
    scratch = refs[n_in + n_out:]
    st = scratch[0:8]
    ms = scratch[8:16]
    hdir = scratch[16:18]
    rows_sc, cmb_sc, bb_sc, kt_sc = scratch[18:22]
    neg_inf = F32(-jnp.inf)
    zeros112 = jnp.zeros((CHUNK - 16, CHUNK), F32)
    sub = lax.broadcasted_iota(jnp.int32, (CHUNK, CHUNK), 0)
    lan = lax.broadcasted_iota(jnp.int32, (CHUNK, CHUNK), 1)
    ones = jnp.ones((CHUNK, CHUNK), BF16)

    stats = []
    for c in range(n_chunks):
        pre = g_ref[c] + bg_ref[...]
        logf = _log_sigmoid(pre)
        for d in range(2):
            bcum = _dot3(logf, ((sub <= lan) if d == 0 else (sub >= lan)).astype(BF16))
            li4 = pre[4 * d:4 * d + 4, :]
            lf4 = logf[8 + 4 * d:12 + 4 * d, :]
            b4 = bcum[8 + 4 * d:12 + 4 * d, :]
            stats.append((c, d, li4 - b4, lf4, b4))
    for c, d, r4, lf4, b4 in stats:
        blast4 = b4[:, CHUNK - 1:CHUNK] if d == 0 else b4[:, 0:1]
        wlog4 = blast4 + r4
        wmax4 = jnp.max(wlog4, axis=-1, keepdims=True)
        full = lambda a: jnp.broadcast_to(a, (4, CHUNK))
        rows_sc[2 * c + d] = jnp.concatenate(
            [r4, wlog4, full(blast4), full(wmax4), jnp.zeros((16, CHUNK), F32)], axis=0)
    for c in range(n_chunks):
        for h in range(N_HEADS_M):
            kc = k_ref[CHUNK * c:CHUNK * (c + 1), DH_M * h:DH_M * (h + 1)]
            kt_sc[c * N_HEADS_M + h] = kc.astype(F32).T.astype(BF16)
    for c, d, r4, lf4, b4 in stats:
        within = (lan <= sub) if d == 0 else (lan >= sub)
        for h in range(N_HEADS_M):
            cummax = jnp.max(jnp.where(within, r4[h:h + 1, :], neg_inf), axis=-1, keepdims=True)
            cmb_sc[(2 * c + d) * N_HEADS_M + h] = jnp.broadcast_to(cummax, (CHUNK, CHUNK))
    for c, d, r4, lf4, b4 in stats:
        within = (lan <= sub) if d == 0 else (lan >= sub)
        for h in range(N_HEADS_M):
            bb_sc[(2 * c + d) * N_HEADS_M + h] = _dot2(jnp.where(within, lf4[h:h + 1, :], 0.0), ones)

    for d in range(2):
        if has_state:
            c0_ref, n0_ref, m0_ref = refs[7:10]
            ncols = jnp.concatenate([n0_ref[d], jnp.zeros((12, DH_M), F32), zeros112], axis=0).T
        for h in range(N_HEADS_M):
            idx = 4 * d + h
            if has_state:
                st[idx][:, 0:DH_M] = c0_ref[d, h]
                st[idx][:, DH_M:] = _col_bcast(ncols, h)
                ms[idx][0:1, :] = m0_ref[idx:idx + 1, :]
            else:
                st[idx][...] = jnp.zeros((DH_M, 2 * DH_M), F32)
                ms[idx][0:1, :] = jnp.full((1, CHUNK), NEG_INIT, F32)

    def chunk_step(i, carry):
        for d in range(2):
            c = i if d == 0 else n_chunks - 1 - i
            r0 = pl.multiple_of(c * CHUNK, CHUNK)
            rows = rows_sc[2 * c + d]
            mask = (lan <= sub) if d == 0 else (lan >= sub)
            heads = []
            for h in range(N_HEADS_M):
                idx = 4 * d + h
                hs = slice(DH_M * h, DH_M * (h + 1))
                qc = q_ref[pl.ds(r0, CHUNK), hs]
                kt = kt_sc[c * N_HEADS_M + h]
                state = st[idx][...]
                heads.append(dict(idx=idx, hs=hs, qc=qc, kt=kt, state=state,
                                  s_raw=_dot(qc, kt), qs=_dot(qc, state.astype(BF16))))
            for h, hd in enumerate(heads):
                mp = ms[hd["idx"]][0:1, :]
                m_b = jnp.maximum(mp, cmb_sc[(2 * c + d) * N_HEADS_M + h])
                s = hd["s_raw"] * jnp.exp(jnp.where(mask, rows[h:h + 1, :] - m_b, neg_inf))
                s_hi = s.astype(BF16)
                vc = v_ref[pl.ds(r0, CHUNK), hd["hs"]]
                v_ones = jnp.concatenate([vc, ones], axis=1)
                hd.update(mp=mp, m_b=m_b, v_ones=v_ones, sv=_dot(s_hi, v_ones),
                          s_lo_sum=_dot((s - s_hi.astype(F32)).astype(BF16), ones))
            for h, hd in enumerate(heads):
                mp, m_b = hd["mp"], hd["m_b"]
                gw_b = jnp.exp(mp - m_b)
                en_b = jnp.exp(-(bb_sc[(2 * c + d) * N_HEADS_M + h] + m_b))
                num = hd["sv"][:, 0:DH_M] + gw_b * hd["qs"][:, 0:DH_M]
                den = hd["sv"][:, DH_M:] + hd["s_lo_sum"] + gw_b * hd["qs"][:, DH_M:]
                hdir[d][pl.ds(r0, CHUNK), hd["hs"]] = num / jnp.maximum(jnp.abs(den), en_b)
            for h, hd in enumerate(heads):
                mp = hd["mp"]
                gend = rows[8 + h:9 + h, :] + mp
                mnew = jnp.maximum(gend, rows[12 + h:13 + h, :])
                w_row = jnp.exp(rows[4 + h:5 + h, :] - mnew)
                kw_t = (hd["kt"].astype(F32) * w_row).astype(BF16)
                dec = jnp.exp(gend - mnew)
                st[hd["idx"]][...] = (jnp.concatenate([dec, dec], axis=1) * hd["state"]
                                      + _dot(kw_t, hd["v_ones"]))
                ms[hd["idx"]][0:1, :] = mnew
        return carry

    lax.fori_loop(0, n_chunks, chunk_step, 0)

    for h in range(N_HEADS_M):
        hs = slice(DH_M * h, DH_M * (h + 1))
        hh = hdir[0][:, hs] + hdir[1][:, hs]
        y = hh * lax.rsqrt(jnp.mean(hh * hh, axis=-1, keepdims=True) + EPS) * nw_ref[:, hs]
        hm_ref[:, hs] = (jax.nn.sigmoid(om_ref[:, hs]) * y).astype(BF16)
    if not has_state:
        c_ref, n_ref, m_ref = refs[n_in + 1:n_in + 4]
        for d in range(2):
            for h in range(N_HEADS_M):
                idx = 4 * d + h
                c_ref[d, h] = st[idx][:, 0:DH_M]
                n_ref[d, h:h + 1, :] = st[idx][:, DH_M:].T[0:1, :]
                m_ref[idx:idx + 1, :] = ms[idx][0:1, :]


def _mlstm(qm, km, vm, g3, bg, om, nw, state, *, n_batch, seq, first_block):
    n_chunks = seq // CHUNK
    rb = seq // TB
    tok = lambda: pl.BlockSpec((seq, 512), lambda b: (b + first_block // rb, 0))
    full = lambda a: pl.BlockSpec(a.shape, lambda b: (0,) * a.ndim)
    in_specs = [tok(), tok(), tok(),
                pl.BlockSpec((n_chunks, 16, CHUNK), lambda b: (b + first_block // rb, 0, 0)),
                full(bg), tok(), full(nw)]
    args = [qm, km, vm, g3, bg, om, nw]
    hm_spec = pl.BlockSpec((seq, 512), lambda b: (b, 0))
    hm_shape = jax.ShapeDtypeStruct((n_batch * seq, 512), BF16)
    scratch = ([pltpu.VMEM((DH_M, 2 * DH_M), F32)] * 8 + [pltpu.VMEM((8, CHUNK), F32)] * 8
               + [pltpu.VMEM((seq, 512), F32)] * 2
               + [pltpu.VMEM((2 * n_chunks, 32, CHUNK), F32),
                  pltpu.VMEM((8 * n_chunks, CHUNK, CHUNK), F32),
                  pltpu.VMEM((8 * n_chunks, CHUNK, CHUNK), F32),
                  pltpu.VMEM((N_HEADS_M * n_chunks, DH_M, CHUNK), BF16)])
    if state is not None:
        c0, n0, m0 = state
        in_specs += [pl.BlockSpec((None, None, 2, N_HEADS_M, DH_M, DH_M), lambda b: (b, 0, 0, 0, 0, 0)),
                     pl.BlockSpec((None, None, 2, N_HEADS_M, DH_M), lambda b: (b, 0, 0, 0, 0)),
                     pl.BlockSpec((None, 8, CHUNK), lambda b: (b, 0, 0))]
        args += [c0, n0, m0]
        out_specs, out_shape = hm_spec, hm_shape
    else:
        out_specs = [hm_spec,
                     pl.BlockSpec((None, 2, N_HEADS_M, DH_M, DH_M), lambda b: (b, 0, 0, 0, 0)),
                     pl.BlockSpec((None, 2, N_HEADS_M, DH_M), lambda b: (b, 0, 0, 0)),
                     pl.BlockSpec((None, 8, CHUNK), lambda b: (b, 0, 0))]
        out_shape = [hm_shape,
                     jax.ShapeDtypeStruct((n_batch, 2, N_HEADS_M, DH_M, DH_M), F32),
                     jax.ShapeDtypeStruct((n_batch, 2, N_HEADS_M, DH_M), F32),
                     jax.ShapeDtypeStruct((n_batch, 8, CHUNK), F32)]
    return pl.pallas_call(
        functools.partial(_mlstm_kernel, has_state=state is not None, n_chunks=n_chunks),
        grid=(n_batch,),
        in_specs=in_specs,
        out_specs=out_specs,
        out_shape=out_shape,
        scratch_shapes=scratch,
        compiler_params=_cparams(("arbitrary",)),
        name="mlstm_lat" if state is not None else "mlstm_ctx",
    )(*args)


def _outproj_kernel(attc_ref, attl_ref, hmc_ref, hml_ref, xc_ref, xl_ref, mod_ref, nw_ref, wo_ref,
                    rwh_ref, rwl_ref, wsg_ref, wsu_ref, wsd_ref, base_ref, hn_ref, lt_ref):
    y = (_dot(_ctx_or_lat(attc_ref, attl_ref), wo_ref[0:D_ATT, :])
         + _dot(_ctx_or_lat(hmc_ref, hml_ref), wo_ref[D_ATT:, :]))
    x1 = _ctx_or_lat(xc_ref, xl_ref) + mod_ref[2:3, :] * y
    z = x1 * lax.rsqrt(jnp.mean(x1 * x1, axis=-1, keepdims=True) + EPS) * nw_ref[...]
    hn = z * (1.0 + mod_ref[4:5, :]) + mod_ref[3:4, :]
    hb = hn.astype(BF16)
    hl = (hn - hb.astype(F32)).astype(BF16)
    nt = lambda w, t: lax.dot_general(w, t, _NT, preferred_element_type=F32)
    lt_ref[...] = nt(rwh_ref[...], hb) + nt(rwl_ref[...], hb) + nt(rwh_ref[...], hl)
    a = _silu(_dot(hb, wsg_ref[...])) * _dot(hb, wsu_ref[...])
    shared = _dot(a.astype(BF16), wsd_ref[...])
    base_ref[...] = x1 + mod_ref[5:6, :] * shared
    packed = pltpu.pack_elementwise([hn[:, :512], hn[:, 512:]], packed_dtype=BF16)
    for c in range(4):
        hn_ref[pl.ds(c, TBP, stride=4), :] = packed[:, 128 * c:128 * (c + 1)]


def _outproj(att, hm, x, mod3, norm2_w, w_out, rw_hi, rw_lo, wsg, wsu, wsd):
    tok = lambda w: pl.BlockSpec((TBP, w), lambda i: (i, 0))
    full = lambda a: pl.BlockSpec(a.shape, lambda i: (0,) * a.ndim)
    return pl.pallas_call(
        _outproj_kernel,
        grid=(T_ALL // TBP,),
        in_specs=_split_specs(512) + _split_specs(512) + _split_specs(D_MODEL) + [
                  pl.BlockSpec((None, 6, D_MODEL), lambda i: (_mod_row(i), 0, 0)),
                  full(norm2_w), full(w_out), full(rw_hi), full(rw_lo), full(wsg), full(wsu), full(wsd)],
        out_specs=[tok(D_MODEL), pl.BlockSpec((4 * TBP, 128), lambda i: (i, 0)),
                   pl.BlockSpec((N_EXPERTS, TBP), lambda i: (0, i))],
        out_shape=[jax.ShapeDtypeStruct((T_ALL, D_MODEL), F32),
                   jax.ShapeDtypeStruct((4 * T_ALL, 128), jnp.uint32),
                   jax.ShapeDtypeStruct((N_EXPERTS, T_ALL), F32)],
        compiler_params=_cparams(("arbitrary",)),
        name="outproj",
    )(*att, *hm, *x, mod3, norm2_w, w_out, rw_hi, rw_lo, wsg, wsu, wsd)


def _first_max(vals, ids, limit):
    m = functools.reduce(jnp.maximum, [jnp.max(v, axis=0, keepdims=True) for v in vals])
    cand = [jnp.min(jnp.where(v == m, i, limit), axis=0, keepdims=True) for v, i in zip(vals, ids)]
    return m, functools.reduce(jnp.minimum, cand)


def _router_kernel(lt_ref, bias_ref, gate_ref, idx_ref):
    n = lt_ref.shape[1]
    score = jax.nn.sigmoid(lt_ref[...])
    biased = score + bias_ref[...]
    sub = lax.broadcasted_iota(jnp.int32, (8, n), 0).astype(F32)
    neg_inf = F32(-jnp.inf)
    slabs = [biased[8 * g:8 * (g + 1), :] for g in range(N_GROUPS)]
    gs = []
    for sl in slabs:
        m1, i1 = _first_max([sl], [sub], 8.0)
        m2 = jnp.max(jnp.where(sub == i1, neg_inf, sl), axis=0, keepdims=True)
        gs.append(m1 + m2)
    cur = jnp.concatenate(gs, axis=0)
    gsel = jnp.zeros((8, n), F32)
    for _ in range(TOPK_GROUPS):
        _, i = _first_max([cur], [sub], 8.0)
        hit = sub == i
        gsel = jnp.where(hit, 1.0, gsel)
        cur = jnp.where(hit, neg_inf, cur)
    vals = [jnp.where(gsel[g:g + 1, :] > 0.0, slabs[g], neg_inf) for g in range(N_GROUPS)]
    ids = [sub + 8.0 * g for g in range(N_GROUPS)]
    picked = [jnp.zeros((8, n), F32) for _ in range(N_GROUPS)]
    order = []
    for _ in range(TOP_K):
        _, i = _first_max(vals, ids, float(N_EXPERTS))
        order.append(i)
        hits = [idg == i for idg in ids]
        picked = [jnp.where(hh, score[8 * g:8 * (g + 1), :], p) for g, (p, hh) in enumerate(zip(picked, hits))]
        vals = [jnp.where(hh, neg_inf, v) for v, hh in zip(vals, hits)]
    total = functools.reduce(jnp.add, [jnp.sum(p, axis=0, keepdims=True) for p in picked])
    gate_t = jnp.concatenate([p / total * ROUTED_SCALE for p in picked]
                             + [jnp.zeros((128 - N_EXPERTS, n), F32)], axis=0)
    gate_ref[...] = gate_t.T
    idx_ref[...] = jnp.concatenate(order, axis=0).astype(jnp.int32)


def _router(logits_t, bias_col):
    return pl.pallas_call(
        _router_kernel,
        grid=(T_ALL // TB_MOE,),
        in_specs=[pl.BlockSpec((N_EXPERTS, TB_MOE), lambda i: (0, i)),
                  pl.BlockSpec((N_EXPERTS, 1), lambda i: (0, 0))],
        out_specs=[pl.BlockSpec((TB_MOE, 128), lambda i: (i, 0)),
                   pl.BlockSpec((TOP_K, TB_MOE), lambda i: (0, i))],
        out_shape=[jax.ShapeDtypeStruct((T_ALL, 128), F32),
                   jax.ShapeDtypeStruct((TOP_K, T_ALL), jnp.int32)],
        compiler_params=_cparams(("arbitrary",)),
        name="router",
    )(logits_t, bias_col)


def _plan_kernel(idx_ref, slot_ref, cnt_ref, off_ref, pos_sc):
    n_tiles = T_HALF // 128
    eid = lax.broadcasted_iota(jnp.int32, (N_EXPERTS, 128), 0)
    tri = (lax.broadcasted_iota(jnp.int32, (128, 128), 0)
           <= lax.broadcasted_iota(jnp.int32, (128, 128), 1)).astype(BF16)
    carry = jnp.zeros((N_EXPERTS, 1), F32)
    for j in range(n_tiles):
        it = idx_ref[:, 128 * j:128 * (j + 1)]
        sel = jnp.zeros((N_EXPERTS, 128), F32)
        for k in range(TOP_K):
            sel = jnp.where(it[k:k + 1, :] == eid, 1.0, sel)
        inc = _dot(sel.astype(BF16), tri) + carry
        carry = inc[:, 127:128]
        pos_sc[:, 128 * j:128 * (j + 1)] = inc - 1.0
    count = jnp.broadcast_to(carry, (N_EXPERTS, 128))
    padded = jnp.floor((count + (MT - 1.0)) * (1.0 / MT)) * MT
    before = (lax.broadcasted_iota(jnp.int32, (N_EXPERTS, N_EXPERTS), 1)
              < lax.broadcasted_iota(jnp.int32, (N_EXPERTS, N_EXPERTS), 0)).astype(BF16)
    hi, mid, lo = _split3(padded)
    off = _dot(before, hi) + _dot(before, mid) + _dot(before, lo)
    cnt_ref[...] = count.astype(jnp.int32)
    off_ref[...] = off.astype(jnp.int32)

    def slots(j):
        it = idx_ref[:, 128 * j:128 * (j + 1)]
        val = off + pos_sc[:, 128 * j:128 * (j + 1)]
        rows = [jnp.sum(jnp.where(it[k:k + 1, :] == eid, val, 0.0), axis=0, keepdims=True)
                for k in range(TOP_K)]
        return jnp.concatenate(rows, axis=0).astype(jnp.int32)

    for j in range(n_tiles):
        s = slots(j)
        slot_ref[:, 128 * j:128 * (j + 1)] = s[0:TOP_K // 2, :] | (s[TOP_K // 2:, :] << 16)


def _plan(idx8):
    return pl.pallas_call(
        _plan_kernel,
        grid=(2,),
        in_specs=[pl.BlockSpec((TOP_K, T_HALF), lambda h: (0, h))],
        out_specs=[pl.BlockSpec((None, TOP_K // 2, T_HALF), lambda h: (h, 0, 0)),
                   pl.BlockSpec((None, N_EXPERTS, 128), lambda h: (h, 0, 0)),
                   pl.BlockSpec((None, N_EXPERTS, 128), lambda h: (h, 0, 0))],
        out_shape=[jax.ShapeDtypeStruct((2, TOP_K // 2, T_HALF), jnp.int32),
                   jax.ShapeDtypeStruct((2, N_EXPERTS, 128), jnp.int32),
                   jax.ShapeDtypeStruct((2, N_EXPERTS, 128), jnp.int32)],
        scratch_shapes=[pltpu.VMEM((N_EXPERTS, T_HALF), F32)],
        compiler_params=_cparams(("arbitrary",)),
        name="plan",
    )(idx8)


def _prepare_half(slot_ref, cnt_ref, off_ref, tok_ref, tile_e, tile_first, elist, h):
    def per_expert(x, carry):
        j, q = carry
        n = cnt_ref[h * N_EXPERTS + x]
        first = off_ref[h * N_EXPERTS + x]
        tiles = (n + MT - 1) // MT
        elist[q] = x

        def mark(i, c):
            tile_e[j + i] = x
            tile_first[j + i] = jnp.where(i == 0, 1, 0)
            return c

        lax.fori_loop(0, tiles, mark, 0)

        @pl.when(tiles > 0)
        def _():
            last = first + (tiles - 1) * MT
            for u in range(MT):
                tok_ref[last + u] = T_HALF

        return j + tiles, q + jnp.where(tiles > 0, 1, 0)

    n_tiles, n_live = lax.fori_loop(0, N_EXPERTS, per_expert, (0, 0))
    for extra in range(2):
        tile_e[n_tiles + extra] = 0
        tile_first[n_tiles + extra] = 0
        for u in range(MT):
            tok_ref[(n_tiles + extra) * MT + u] = T_HALF
    return n_tiles, n_live


def _invert_slots(slot_ref, tok_ref, acc_v, h):
    rounds = TOP_K // 2
    trips = T_HALF // INV_UNROLL
    zero_rows = 8 * T_HALF // (rounds * trips)
    assert zero_rows * rounds * trips == 8 * T_HALF and zero_rows % 8 == 0
    for k in range(rounds):
        def body(i, carry, k=k):
            words = [slot_ref[(h * rounds + k) * T_HALF + i * INV_UNROLL + u] for u in range(INV_UNROLL)]
            for u in range(INV_UNROLL):
                t = i * INV_UNROLL + u
                tok_ref[words[u] & 0xFFFF] = t
                tok_ref[lax.shift_right_logical(words[u], 16)] = t
            row = pl.multiple_of((k * trips + i) * zero_rows, 8)
            acc_v[pl.ds(row, zero_rows), :] = jnp.zeros((zero_rows, 128), F32)
            return carry

        lax.fori_loop(0, trips, body, 0)
    acc_v[pl.ds(8 * T_HALF, 8), :] = jnp.zeros((8, 128), F32)


def _moe_kernel(slot_ref, cnt_ref, off_ref, src_hbm, gate_hbm, wg_hbm, wu_hbm, wd_hbm, base_hbm, mod_ref,
                outc_hbm, outl_hbm,
                src_v, gate_v, acc_v, wg_l, wu_l, wd_l, wgu_b, wd_b, xbuf0, xbuf1, gbuf0, gbuf1,
                ybuf0, ybuf1, tok_ref, tile_e, tile_first, elist, live, base_buf, out_buf, sem, wsem, fsem):
    h = pl.program_id(0)
    xbuf, gbuf, ybuf = (xbuf0, xbuf1), (gbuf0, gbuf1), (ybuf0, ybuf1)

    src_cp = pltpu.make_async_copy(src_hbm.at[pl.ds(pl.multiple_of(h * (4 * T_HALF), 8), 4 * T_HALF)],
                                   src_v.at[pl.ds(0, 4 * T_HALF)], sem.at[0])
    gate_cp = pltpu.make_async_copy(gate_hbm.at[pl.ds(pl.multiple_of(h * T_HALF, 8), T_HALF)],
                                    gate_v.at[pl.ds(0, T_HALF)], sem.at[1])
    src_cp.start()
    gate_cp.start()

    def weight_copies(x, slot):
        return [pltpu.make_async_copy(w_hbm.at[x], w_l.at[slot], wsem.at[slot, i])
                for i, (w_hbm, w_l) in enumerate(((wg_hbm, wg_l), (wu_hbm, wu_l), (wd_hbm, wd_l)))]

    n_tiles, n_live = _prepare_half(slot_ref, cnt_ref, off_ref, tok_ref, tile_e, tile_first, elist, h)
    live[0] = 0
    live[1] = n_live

    for ahead in range(W_SLOTS - 1):
        @pl.when(ahead < n_live)
        def _(ahead=ahead):
            for cp in weight_copies(elist[ahead], ahead):
                cp.start()

    zero = jnp.zeros((8, 128), F32)
    src_v[pl.ds(4 * T_HALF, 8), :] = pltpu.pack_elementwise([zero, zero], packed_dtype=BF16)
    gate_v[pl.ds(T_HALF, 8), :] = zero
    ybuf0[...] = jnp.zeros(ybuf0.shape, F32)
    ybuf1[...] = jnp.zeros(ybuf1.shape, F32)
    _invert_slots(slot_ref, tok_ref, acc_v, h)
    src_cp.wait()
    gate_cp.wait()

    def switch_expert():
        q = live[0]
        slot = q % W_SLOTS
        for cp in weight_copies(elist[q], slot):
            cp.wait()
        wgu_b[:, 0:D_EXPERT] = wg_l[slot].astype(BF16)
        wgu_b[:, D_EXPERT:] = wu_l[slot].astype(BF16)
        wd_b[...] = wd_l[slot].astype(BF16)
        live[0] = q + 1
        nxt = q + W_SLOTS - 1

        @pl.when(nxt < live[1])
        def _():
            for cp in weight_copies(elist[nxt], nxt % W_SLOTS):
                cp.start()

    def gather(j, xb, gb, rows=(0, MT)):
        base = j * MT
        for m in range(*rows):
            t = tok_ref[base + m]
            xb[pl.ds(m, 4, stride=MT_STRIDE), :] = src_v[pl.ds(pl.multiple_of(t * 4, 4), 4), :]
            gb[m:m + 1, :] = gate_v[pl.ds(t, 1), :]

    def scatter(j, yb, rows=(0, MT)):
        base = j * MT
        for b in range(rows[0] // RMW_BATCH, rows[1] // RMW_BATCH):
            ms = [b * RMW_BATCH + u for u in range(RMW_BATCH)]
            targets = [pl.ds(pl.multiple_of(tok_ref[base + m] * 8, 8), 8) for m in ms]
            vals = [acc_v[r, :] + yb[pl.ds(m, 8, stride=MT_STRIDE), :] for r, m in zip(targets, ms)]
            for r, v in zip(targets, vals):
                acc_v[r, :] = v

    quarters = [(q * MT // 4, (q + 1) * MT // 4) for q in range(4)]

    def step(j, p):
        pl.when(tile_first[j] == 1)(switch_expert)
        xb, gb, yb = xbuf[p], gbuf[p], ybuf[p]
        nxt = (j + 1, xbuf[1 - p], gbuf[1 - p])
        prv = (jnp.maximum(j - 1, 0), ybuf[1 - p])
        lo, hi = [], []
        for c in range(4):
            words = xb[MT_STRIDE * c:MT_STRIDE * c + MT, :]
            unpack = functools.partial(pltpu.unpack_elementwise, words, packed_dtype=BF16, unpacked_dtype=F32)
            lo.append(unpack(index=0).astype(BF16))
            hi.append(unpack(index=1).astype(BF16))
        x = jnp.concatenate(lo + hi, axis=1)
        gather(*nxt, rows=quarters[0])
        h_gate = _dot(x, wgu_b[:, 0:D_EXPERT])
        gather(*nxt, rows=quarters[1])
        h_up = _dot(x, wgu_b[:, D_EXPERT:])
        gather(*nxt, rows=quarters[2])
        g = gb[...]
        g_hi = g.astype(BF16)
        g_lo = (g - g_hi.astype(F32)).astype(BF16)
        pick = (lax.broadcasted_iota(jnp.int32, (128, D_EXPERT), 0) == tile_e[j]).astype(BF16)
        g_col = _dot(g_hi, pick) + _dot(g_lo, pick)
        a = (_silu(h_gate) * h_up * g_col).astype(BF16)
        gather(*nxt, rows=quarters[3])
        scatter(*prv, rows=quarters[0])
        y_lo = _dot(a, wd_b[:, 0:D_MODEL // 2])
        scatter(*prv, rows=quarters[1])
        y_hi = _dot(a, wd_b[:, D_MODEL // 2:])
        scatter(*prv, rows=quarters[2])
        for c in range(4):
            yb[MT_STRIDE * c:MT_STRIDE * c + MT, :] = y_lo[:, 128 * c:128 * (c + 1)]
            yb[MT_STRIDE * (c + 4):MT_STRIDE * (c + 4) + MT, :] = y_hi[:, 128 * c:128 * (c + 1)]
        scatter(*prv, rows=quarters[3])

    gather(0, xbuf[0], gbuf[0])
    n_pairs = (n_tiles + 1) // 2

    def pair(i, carry):
        step(2 * i, 0)
        step(2 * i + 1, 1)
        return carry

    lax.fori_loop(0, n_pairs, pair, 0)
    scatter(jnp.maximum(2 * n_pairs - 1, 0), ybuf[1])

    n_fin = T_HALF // FIN

    def first_token(c):
        return h * T_HALF + c * FIN

    def base_copy(c, slot):
        return pltpu.make_async_copy(base_hbm.at[pl.ds(pl.multiple_of(first_token(c), FIN), FIN)],
                                     base_buf.at[slot], fsem.at[0, slot])

    def on_out_copy(c, action):
        t0 = first_token(c)
        slot = c % FIN_OUT_SLOTS

        @pl.when(t0 < T_CTX)
        def _():
            action(pltpu.make_async_copy(out_buf.at[slot], outc_hbm.at[pl.ds(pl.multiple_of(t0, FIN), FIN)],
                                         fsem.at[1, slot]))

        @pl.when(t0 >= T_CTX)
        def _():
            action(pltpu.make_async_copy(out_buf.at[slot],
                                         outl_hbm.at[pl.ds(pl.multiple_of(t0 - T_CTX, FIN), FIN)],
                                         fsem.at[1, slot]))

    for c in range(FIN_BASE_SLOTS - 1):
        base_copy(c, c).start()

    def combine(c, carry):
        slot = c % FIN_BASE_SLOTS
        oslot = c % FIN_OUT_SLOTS
        base_copy(c, slot).wait()
        ahead = c + FIN_BASE_SLOTS - 1

        @pl.when(ahead < n_fin)
        def _():
            base_copy(ahead, ahead % FIN_BASE_SLOTS).start()

        @pl.when(c >= FIN_OUT_SLOTS)
        def _():
            on_out_copy(c - FIN_OUT_SLOTS, lambda cp: cp.wait())

        t0 = first_token(c)
        mod_row = jnp.where(t0 < T_CTX, 0, 1 + (t0 - T_CTX) // DEC_SEQ)
        acc_row = pl.multiple_of(8 * FIN * c, 8)
        for cc in range(8):
            cs = slice(128 * cc, 128 * (cc + 1))
            routed = acc_v[pl.ds(acc_row + cc, FIN, stride=8), :]
            out_buf[oslot, :, cs] = base_buf[slot, :, cs] + mod_ref[mod_row, 5:6, cs] * routed
        on_out_copy(c, lambda cp: cp.start())
        return carry

    lax.fori_loop(0, n_fin, combine, 0)
    for c in range(n_fin - FIN_OUT_SLOTS, n_fin):
        on_out_copy(c, lambda cp: cp.wait())


def _moe(slot_words, cnt, off, src, gate, wg, wu, wd, base, mod3):
    any_spec = pl.BlockSpec(memory_space=pl.ANY)
    tile_buf = lambda rows, dt: pltpu.VMEM((rows * MT_STRIDE, 128), dt)
    return pl.pallas_call(
        _moe_kernel,
        grid_spec=pltpu.PrefetchScalarGridSpec(
            num_scalar_prefetch=3,
            grid=(2,),
            in_specs=[any_spec] * 6 + [pl.BlockSpec(mod3.shape, lambda h, *_: (0, 0, 0))],
            out_specs=[any_spec, any_spec],
            scratch_shapes=[pltpu.VMEM((4 * T_HALF + 8, 128), jnp.uint32),
                            pltpu.VMEM((T_HALF + 8, 128), F32),
                            pltpu.VMEM((8 * T_HALF + 8, 128), F32),
                            pltpu.VMEM((W_SLOTS, D_MODEL, D_EXPERT), F32),
                            pltpu.VMEM((W_SLOTS, D_MODEL, D_EXPERT), F32),
                            pltpu.VMEM((W_SLOTS, D_EXPERT, D_MODEL), F32),
                            pltpu.VMEM((D_MODEL, 2 * D_EXPERT), BF16),
                            pltpu.VMEM((D_EXPERT, D_MODEL), BF16),
                            tile_buf(4, jnp.uint32), tile_buf(4, jnp.uint32),
                            pltpu.VMEM((MT, 128), F32), pltpu.VMEM((MT, 128), F32),
                            tile_buf(8, F32), tile_buf(8, F32),
                            pltpu.SMEM((SLOT_CAP,), jnp.int32),
                            pltpu.SMEM((NT_MAX + 2,), jnp.int32),
                            pltpu.SMEM((NT_MAX + 2,), jnp.int32),
                            pltpu.SMEM((N_EXPERTS,), jnp.int32),
                            pltpu.SMEM((2,), jnp.int32),
                            pltpu.VMEM((FIN_BASE_SLOTS, FIN, D_MODEL), F32),
                            pltpu.VMEM((FIN_OUT_SLOTS, FIN, D_MODEL), F32),
                            pltpu.SemaphoreType.DMA((2,)),
                            pltpu.SemaphoreType.DMA((W_SLOTS, 3)),
                            pltpu.SemaphoreType.DMA((2, FIN_BASE_SLOTS))]),
        out_shape=[jax.ShapeDtypeStruct((T_CTX, D_MODEL), F32), jax.ShapeDtypeStruct((T_LAT, D_MODEL), F32)],
        compiler_params=_cparams(("arbitrary",)),
        name="moe",
    )(slot_words, cnt, off, src, gate, wg, wu, wd, base, mod3)


def _head_indicators(width):
    head = jnp.arange(width) // DH_ATT
    ind = (head[:, None] == jnp.arange(128)[None, :]).astype(BF16)
    return ind, ind.T


def kernel(x_prompt, x_sample, cache_attn_k, cache_attn_v, state_mlstm_c, state_mlstm_n, state_mlstm_m, c, c_ctx, w_mod, b_mod, norm1_w, norm2_w, w_in, q_norm_w, k_norm_w, b_gates, m_norm_w, w_out, router_w, router_bias, w_gate, w_up, w_down, ws_gate, ws_up, ws_down):
    x = (x_prompt.reshape(T_CTX, D_MODEL), x_sample.reshape(T_LAT, D_MODEL))
    mod3 = _modulation(c, c_ctx, w_mod[0], b_mod[0])

    w_main = w_in[0, :, :P_MAIN].astype(BF16)
    w_gates_t = w_in[0, :, P_MAIN:].T.astype(BF16)
    qw = jnp.tile(q_norm_w, (1, N_HEADS_ATT))
    kw = jnp.tile(k_norm_w, (1, N_KV_HEADS))
    inds = _head_indicators(512) + _head_indicators(128)
    qn, kn, va, qm, km, vm, om, gt = _inproj(*x, mod3, norm1_w, w_main, w_gates_t, qw, kw, inds, _rope_tables())

    att_c = _attention(qn, kn, va, None, n_batch=BATCH, seq=SEQ, first_row=0)
    att_l = _attention(qn, kn, va, (cache_attn_k, cache_attn_v), n_batch=DEC_BATCH, seq=DEC_SEQ,
                       first_row=T_CTX)

    g3 = gt.reshape(16, T_ALL // CHUNK, CHUNK).transpose(1, 0, 2)
    bg = b_gates.reshape(16, 1)
    hm_c, c_new, n_new, m_new = _mlstm(qm, km, vm, g3, bg, om, m_norm_w, None,
                                       n_batch=BATCH, seq=SEQ, first_block=0)
    m0 = jnp.broadcast_to(state_mlstm_m.reshape(DEC_BATCH, 8, 1), (DEC_BATCH, 8, CHUNK))
    hm_l = _mlstm(qm, km, vm, g3, bg, om, m_norm_w, (state_mlstm_c, state_mlstm_n, m0),
                  n_batch=DEC_BATCH, seq=DEC_SEQ, first_block=N_CTX_BLOCKS)

    rw_t = router_w[0].T
    rw_hi = rw_t.astype(BF16)
    rw_lo = (rw_t - rw_hi.astype(F32)).astype(BF16)
    base, hn2, logits_t = _outproj((att_c, att_l), (hm_c, hm_l), x, mod3, norm2_w, w_out[0].astype(BF16),
                                   rw_hi, rw_lo, ws_gate[0].astype(BF16), ws_up[0].astype(BF16),
                                   ws_down[0].astype(BF16))
    gate, idx8 = _router(logits_t, router_bias.reshape(N_EXPERTS, 1))
    slot_words, cnt, off = _plan(idx8)
    out_c, out_l = _moe(slot_words.reshape(TOP_K * T_HALF), cnt[:, :, 0].reshape(2 * N_EXPERTS),
                        off[:, :, 0].reshape(2 * N_EXPERTS), hn2, gate, w_gate[0], w_up[0], w_down[0],
                        base, mod3)

    y_prompt = out_c.reshape(BATCH, SEQ, D_MODEL)
    y_sample = out_l.reshape(DEC_BATCH, DEC_SEQ, D_MODEL)
    to_cache = lambda a: a[:T_CTX].reshape(BATCH, SEQ, N_KV_HEADS, DH_ATT).transpose(0, 2, 1, 3)[:, None]
    new_k = to_cache(kn)
    new_v = to_cache(va)
    new_m = m_new[:, :, 0].reshape(BATCH, 1, 2, N_HEADS_M)
    return (y_prompt, y_sample, new_k, new_v, c_new[:, None], n_new[:, None], new_m)
```

```python
import functools

import jax
import jax.numpy as jnp
from jax import lax
from jax.experimental import pallas as pl
from jax.experimental.pallas import tpu as pltpu

F32 = jnp.float32
BF16 = jnp.bfloat16

D_MODEL = 1024
BATCH = 32
SEQ = 256
DEC_BATCH = 2
DEC_SEQ = 1024
PAST_LEN = 256
GRID_W = 64
N_HEADS_ATT = 8
N_KV_HEADS = 2
DH_ATT = 64
D_ATT = 512
ROPE_THETA = 10000.0
N_HEADS_M = 4
DH_M = 128
D_M = 512
CHUNK = 128
N_EXPERTS = 64
TOP_K = 8
N_GROUPS = 8
TOPK_GROUPS = 4
D_EXPERT = 256
ROUTED_SCALE = 2.5
EPS = 1e-6
NEG_INIT = -1e30

T_CTX = BATCH * SEQ
T_LAT = DEC_BATCH * DEC_SEQ
T_ALL = T_CTX + T_LAT
TB = 256
N_CTX_BLOCKS = T_CTX // TB
TBP = 512
NP_CTX = T_CTX // TBP
NP_LAT_PER_BATCH = DEC_SEQ // TBP
TB_MOE = 1024
T_HALF = T_ALL // 2
MT = 256
MT_STRIDE = MT + 8
NT_MAX = TOP_K * T_HALF // MT + N_EXPERTS
SLOT_CAP = (NT_MAX + 2) * MT
RMW_BATCH = 16
INV_UNROLL = 16
W_SLOTS = 2
FIN = 128
FIN_BASE_SLOTS = 6
FIN_OUT_SLOTS = 4
assert FIN_OUT_SLOTS <= FIN_BASE_SLOTS
assert SLOT_CAP < 2 ** 16 and T_HALF % INV_UNROLL == 0 and MT % RMW_BATCH == 0
P_MAIN = 2816
VMEM_LIMIT = 56 * 1024 * 1024

_NT = (((1,), (1,)), ((), ()))
_TN = (((0,), (0,)), ((), ()))


def _cparams(sem):
    return pltpu.CompilerParams(dimension_semantics=sem, vmem_limit_bytes=VMEM_LIMIT)


def _split3(x):
    hi = x.astype(BF16)
    r1 = x - hi.astype(F32)
    mid = r1.astype(BF16)
    lo = (r1 - mid.astype(F32)).astype(BF16)
    return hi, mid, lo


def _dot(a, b):
    return jnp.dot(a, b, preferred_element_type=F32)


def _dot3(x, m_bf16):
    hi, mid, lo = _split3(x)
    return _dot(hi, m_bf16) + _dot(mid, m_bf16) + _dot(lo, m_bf16)


def _dot2(x, m_bf16):
    hi = x.astype(BF16)
    lo = (x - hi.astype(F32)).astype(BF16)
    return _dot(hi, m_bf16) + _dot(lo, m_bf16)


def _silu(x):
    return x * jax.nn.sigmoid(x)


def _mod_row(i):
    return jnp.where(i < NP_CTX, 0, 1 + (i - NP_CTX) // NP_LAT_PER_BATCH)


def _mod_kernel(ct_ref, w_ref, b_ref, o_ref):
    s = _silu(ct_ref[...])
    w = w_ref[...]
    rows = [jnp.sum(w * s[:, r:r + 1], axis=0, keepdims=True) for r in range(3)]
    rows.append(jnp.zeros((5, w.shape[1]), F32))
    o_ref[...] = jnp.concatenate(rows, axis=0) + b_ref[...]


def _modulation(c, c_ctx, w_mod, b_mod):
    cvec = jnp.concatenate([c_ctx[None, :], c, jnp.zeros((5, D_MODEL), F32)], axis=0)
    nb = 1024
    out = pl.pallas_call(
        _mod_kernel,
        grid=(6 * D_MODEL // nb,),
        in_specs=[pl.BlockSpec((D_MODEL, 8), lambda j: (0, 0)),
                  pl.BlockSpec((D_MODEL, nb), lambda j: (0, j)),
                  pl.BlockSpec((1, nb), lambda j: (0, j))],
        out_specs=pl.BlockSpec((8, nb), lambda j: (0, j)),
        out_shape=jax.ShapeDtypeStruct((8, 6 * D_MODEL), F32),
        compiler_params=_cparams(("arbitrary",)),
        name="modulation",
    )(cvec.T, w_mod, b_mod[None, :])
    return out.reshape(8, 6, D_MODEL)


def _ctx_or_lat(ctx_ref, lat_ref):
    return jnp.where(pl.program_id(0) < NP_CTX, ctx_ref[...], lat_ref[...])


def _split_specs(width):
    return [pl.BlockSpec((TBP, width), lambda i: (jnp.minimum(i, NP_CTX - 1), 0)),
            pl.BlockSpec((TBP, width), lambda i: (jnp.maximum(i - NP_CTX, 0), 0))]


def _head_norm(x, ind, ind_t, w_row):
    ss = _dot2(x * x, ind)
    inv = lax.rsqrt(ss * (1.0 / DH_ATT) + EPS)
    return x * _dot2(inv, ind_t) * w_row


def _rope(x, cos, sin_signed):
    lane = lax.broadcasted_iota(jnp.int32, x.shape, 1)
    partner = jnp.where((lane % 32) < 16, pltpu.roll(x, 128 - 16, 1), pltpu.roll(x, 16, 1))
    return x * cos + partner * sin_signed


def _inproj_kernel(xc_ref, xl_ref, mod_ref, nw_ref, w_ref, wgt_ref, qw_ref, kw_ref, iq_ref, iqt_ref,
                   ik_ref, ikt_ref, cos_ref, sin_ref,
                   qn_ref, kn_ref, va_ref, qm_ref, km_ref, vm_ref, om_ref, gt_ref, w_b):
    @pl.when(pl.program_id(0) == 0)
    def _():
        for j in range(P_MAIN // 128):
            w_b[:, 128 * j:128 * (j + 1)] = w_ref[:, 128 * j:128 * (j + 1)].astype(BF16)

    x = _ctx_or_lat(xc_ref, xl_ref)
    y = x * lax.rsqrt(jnp.mean(x * x, axis=-1, keepdims=True) + EPS) * nw_ref[...]
    hn = y * (1.0 + mod_ref[1:2, :]) + mod_ref[0:1, :]
    hb = hn.astype(BF16)
    qn = _head_norm(_dot(hb, w_b[:,0:512]), iq_ref[...], iqt_ref[...], qw_ref[...])
    kn = _head_norm(_dot(hb, w_b[:,512:640]), ik_ref[...], ikt_ref[...], kw_ref[...])
    qn_ref[...] = qn.astype(BF16)
    kn_ref[...] = kn

    @pl.when(pl.program_id(0) >= NP_CTX)
    def _():
        cos, sin = cos_ref[...], sin_ref[...]
        for j in range(4):
            qn_ref[:, 128 * j:128 * (j + 1)] = _rope(qn[:, 128 * j:128 * (j + 1)], cos, sin).astype(BF16)
        kn_ref[...] = _rope(kn, cos, sin)

    va_ref[...] = _dot(hb, w_b[:,640:768])
    qm_ref[...] = _dot(hb, w_b[:,768:1280]).astype(BF16)
    km_ref[...] = (_dot(hb, w_b[:,1280:1792]) * (DH_M ** -0.5)).astype(BF16)
    vm_ref[...] = _dot(hb, w_b[:,1792:2304]).astype(BF16)
    om_ref[...] = _dot(hb, w_b[:,2304:2816])
    gt_ref[...] = lax.dot_general(wgt_ref[...], hb, _NT, preferred_element_type=F32)


def _inproj(x_ctx, x_lat, mod3, norm1_w, w_main, w_gates_t, qw, kw, inds, rope_tabs):
    tok = lambda w: pl.BlockSpec((TBP, w), lambda i: (i, 0))
    full = lambda a: pl.BlockSpec(a.shape, lambda i: (0,) * a.ndim)
    sd = lambda w, dt: jax.ShapeDtypeStruct((T_ALL, w), dt)
    rope_spec = pl.BlockSpec((TBP, 128), lambda i: (jnp.maximum(i - NP_CTX, 0) % NP_LAT_PER_BATCH, 0))
    return pl.pallas_call(
        _inproj_kernel,
        grid=(T_ALL // TBP,),
        in_specs=_split_specs(D_MODEL) + [
                  pl.BlockSpec((None, 6, D_MODEL), lambda i: (_mod_row(i), 0, 0)),
                  full(norm1_w), full(w_main), full(w_gates_t), full(qw), full(kw)]
                 + [full(a) for a in inds] + [rope_spec, rope_spec],
        out_specs=[tok(512), tok(128), tok(128), tok(512), tok(512), tok(512), tok(512),
                   pl.BlockSpec((16, TBP), lambda i: (0, i))],
        out_shape=[sd(512, BF16), sd(128, F32), sd(128, F32), sd(512, BF16), sd(512, BF16),
                   sd(512, BF16), sd(512, F32), jax.ShapeDtypeStruct((16, T_ALL), F32)],
        scratch_shapes=[pltpu.VMEM((D_MODEL, P_MAIN), BF16)],
        compiler_params=_cparams(("arbitrary",)),
        name="inproj",
    )(x_ctx, x_lat, mod3, norm1_w, w_main, w_gates_t, qw, kw, *inds, *rope_tabs)


def _rope_tables():
    t = jnp.arange(DEC_SEQ)
    pos = jnp.stack([t // GRID_W, t % GRID_W], axis=1).astype(F32)
    n_freq = DH_ATT // 4
    inv_freq = ROPE_THETA ** (-jnp.arange(n_freq, dtype=F32) / n_freq)
    ang = pos[:, :, None] * inv_freq
    cos, sin = jnp.cos(ang), jnp.sin(ang)
    cos_h = jnp.stack([cos, cos], axis=2).reshape(DEC_SEQ, DH_ATT)
    sin_h = jnp.stack([-sin, sin], axis=2).reshape(DEC_SEQ, DH_ATT)
    return jnp.tile(cos_h, (1, 2)), jnp.tile(sin_h, (1, 2))


def _attn_kernel(*refs, has_cache):
    if has_cache:
        q_ref, k_ref, v_ref, kc_ref, vc_ref, o_ref = refs
    else:
        q_ref, k_ref, v_ref, o_ref = refs
    q = q_ref[...] * jnp.asarray(DH_ATT ** -0.5, BF16)
    k = k_ref[...].astype(BF16)
    v = v_ref[...].astype(BF16)
    qb = q.shape[0]
    low_half = lax.broadcasted_iota(jnp.int32, (1, 128), 1) < DH_ATT
    for g in range(N_KV_HEADS):
        kg = k[:, DH_ATT * g:DH_ATT * (g + 1)]
        vg = v[:, DH_ATT * g:DH_ATT * (g + 1)]
        if has_cache:
            kg = jnp.concatenate([kg, kc_ref[g].astype(BF16)], axis=0)
            vg = jnp.concatenate([vg, vc_ref[g].astype(BF16)], axis=0)
        n_keys = kg.shape[0]
        zero = jnp.zeros_like(kg)
        k2 = jnp.concatenate([jnp.concatenate([kg, zero], axis=1), jnp.concatenate([zero, kg], axis=1)], axis=0)
        v2 = jnp.concatenate([jnp.concatenate([vg, zero], axis=1), jnp.concatenate([zero, vg], axis=1)], axis=0)
        ones2 = jnp.concatenate([jnp.broadcast_to(jnp.where(low_half, 1.0, 0.0), (n_keys, 128)),
                                 jnp.broadcast_to(jnp.where(low_half, 0.0, 1.0), (n_keys, 128))],
                                axis=0).astype(BF16)
        qp = jnp.concatenate([q[:, 256 * g:256 * g + 128], q[:, 256 * g + 128:256 * g + 256]], axis=0)
        s = lax.dot_general(qp, k2, _NT, preferred_element_type=F32)
        halves = [s[:, 0:n_keys], s[:, n_keys:]]
        p = jnp.concatenate([jnp.exp(sh - jnp.max(sh, axis=-1, keepdims=True)) for sh in halves],
                            axis=1).astype(BF16)
        ov = _dot(p, jnp.concatenate([v2, ones2], axis=1))
        o = ov[:, 0:128] / ov[:, 128:]
        o_ref[:, 256 * g:256 * g + 128] = o[0:qb, :].astype(BF16)
        o_ref[:, 256 * g + 128:256 * g + 256] = o[qb:, :].astype(BF16)


def _attention(qn, kn, v_all, cache, *, n_batch, seq, first_row):
    qblocks = seq // TB
    kv_spec = pl.BlockSpec((seq, 128), lambda b, i: (b + first_row // seq, 0))
    in_specs = [pl.BlockSpec((TB, 512), lambda b, i: (first_row // TB + b * qblocks + i, 0)), kv_spec, kv_spec]
    args = [qn, kn, v_all]
    if cache is not None:
        cspec = pl.BlockSpec((None, None, N_KV_HEADS, PAST_LEN, DH_ATT), lambda b, i: (b, 0, 0, 0, 0))
        in_specs += [cspec, cspec]
        args += list(cache)
    return pl.pallas_call(
        functools.partial(_attn_kernel, has_cache=cache is not None),
        grid=(n_batch, qblocks),
        in_specs=in_specs,
        out_specs=pl.BlockSpec((TB, 512), lambda b, i: (b * qblocks + i, 0)),
        out_shape=jax.ShapeDtypeStruct((n_batch * seq, 512), BF16),
        compiler_params=_cparams(("arbitrary", "arbitrary")),
        name="attention_lat" if cache is not None else "attention_ctx",
    )(*args)


def _log_sigmoid(x):
    return jnp.minimum(x, 0.0) - jnp.log1p(jnp.exp(-jnp.abs(x)))


def _col_bcast(cols, j):
    return jnp.broadcast_to(cols[:, j:j + 1], (CHUNK, CHUNK))


def _mlstm_kernel(*refs, has_state, n_chunks):
    n_in = 10 if has_state else 7
    n_out = 1 if has_state else 4
    q_ref, k_ref, v_ref, g_ref, bg_ref, om_ref, nw_ref = refs[:7]
    hm_ref = refs[n_in]
    scratch = refs[n_in + n_out:]
    st = scratch[0:8]
    ms = scratch[8:16]
    hdir = scratch[16:18]
    rows_sc, cmb_sc, bb_sc, kt_sc = scratch[18:22]
    neg_inf = F32(-jnp.inf)
    zeros112 = jnp.zeros((CHUNK - 16, CHUNK), F32)
    sub = lax.broadcasted_iota(jnp.int32, (CHUNK, CHUNK), 0)
    lan = lax.broadcasted_iota(jnp.int32, (CHUNK, CHUNK), 1)
    ones = jnp.ones((CHUNK, CHUNK), BF16)

    stats = []
    for c in range(n_chunks):
        pre = g_ref[c] + bg_ref[...]
        logf = _log_sigmoid(pre)
        for d in range(2):
            bcum = _dot3(logf, ((sub <= lan) if d == 0 else (sub >= lan)).astype(BF16))
            li4 = pre[4 * d:4 * d + 4, :]
            lf4 = logf[8 + 4 * d:12 + 4 * d, :]
            b4 = bcum[8 + 4 * d:12 + 4 * d, :]
            stats.append((c, d, li4 - b4, lf4, b4))
    for c, d, r4, lf4, b4 in stats:
        blast4 = b4[:, CHUNK - 1:CHUNK] if d == 0 else b4[:, 0:1]
        wlog4 = blast4 + r4
        wmax4 = jnp.max(wlog4, axis=-1, keepdims=True)
        full = lambda a: jnp.broadcast_to(a, (4, CHUNK))
        rows_sc[2 * c + d] = jnp.concatenate(
            [r4, wlog4, full(blast4), full(wmax4), jnp.zeros((16, CHUNK), F32)], axis=0)
    for c in range(n_chunks):
        for h in range(N_HEADS_M):
            kc = k_ref[CHUNK * c:CHUNK * (c + 1), DH_M * h:DH_M * (h + 1)]
            kt_sc[c * N_HEADS_M + h] = kc.astype(F32).T.astype(BF16)
    for c, d, r4, lf4, b4 in stats:
        within = (lan <= sub) if d == 0 else (lan >= sub)
        for h in range(N_HEADS_M):
            cummax = jnp.max(jnp.where(within, r4[h:h + 1, :], neg_inf), axis=-1, keepdims=True)
            cmb_sc[(2 * c + d) * N_HEADS_M + h] = jnp.broadcast_to(cummax, (CHUNK, CHUNK))
    for c, d, r4, lf4, b4 in stats:
        within = (lan <= sub) if d == 0 else (lan >= sub)
        for h in range(N_HEADS_M):
            bb_sc[(2 * c + d) * N_HEADS_M + h] = _dot2(jnp.where(within, lf4[h:h + 1, :], 0.0), ones)

    for d in range(2):
        if has_state:
            c0_ref, n0_ref, m0_ref = refs[7:10]
            ncols = jnp.concatenate([n0_ref[d], jnp.zeros((12, DH_M), F32), zeros112], axis=0).T
        for h in range(N_HEADS_M):
            idx = 4 * d + h
            if has_state:
                st[idx][:, 0:DH_M] = c0_ref[d, h]
                st[idx][:, DH_M:] = _col_bcast(ncols, h)
                ms[idx][0:1, :] = m0_ref[idx:idx + 1, :]
            else:
                st[idx][...] = jnp.zeros((DH_M, 2 * DH_M), F32)
                ms[idx][0:1, :] = jnp.full((1, CHUNK), NEG_INIT, F32)

    def chunk_step(i, carry):
        for d in range(2):
            c = i if d == 0 else n_chunks - 1 - i
            r0 = pl.multiple_of(c * CHUNK, CHUNK)
            rows = rows_sc[2 * c + d]
            mask = (lan <= sub) if d == 0 else (lan >= sub)
            heads = []
            for h in range(N_HEADS_M):
                idx = 4 * d + h
                hs = slice(DH_M * h, DH_M * (h + 1))
                qc = q_ref[pl.ds(r0, CHUNK), hs]
                kt = kt_sc[c * N_HEADS_M + h]
                state = st[idx][...]
                heads.append(dict(idx=idx, hs=hs, qc=qc, kt=kt, state=state,
                                  s_raw=_dot(qc, kt), qs=_dot(qc, state.astype(BF16))))
            for h, hd in enumerate(heads):
                mp = ms[hd["idx"]][0:1, :]
                m_b = jnp.maximum(mp, cmb_sc[(2 * c + d) * N_HEADS_M + h])
                s = hd["s_raw"] * jnp.exp(jnp.where(mask, rows[h:h + 1, :] - m_b, neg_inf))
                s_hi = s.astype(BF16)
                vc = v_ref[pl.ds(r0, CHUNK), hd["hs"]]
                v_ones = jnp.concatenate([vc, ones], axis=1)
                hd.update(mp=mp, m_b=m_b, v_ones=v_ones, sv=_dot(s_hi, v_ones),
                          s_lo_sum=_dot((s - s_hi.astype(F32)).astype(BF16), ones))
            for h, hd in enumerate(heads):
                mp, m_b = hd["mp"], hd["m_b"]
                gw_b = jnp.exp(mp - m_b)
                en_b = jnp.exp(-(bb_sc[(2 * c + d) * N_HEADS_M + h] + m_b))
                num = hd["sv"][:, 0:DH_M] + gw_b * hd["qs"][:, 0:DH_M]
                den = hd["sv"][:, DH_M:] + hd["s_lo_sum"] + gw_b * hd["qs"][:, DH_M:]
                hdir[d][pl.ds(r0, CHUNK), hd["hs"]] = num / jnp.maximum(jnp.abs(den), en_b)
            for h, hd in enumerate(heads):
                mp = hd["mp"]
                gend = rows[8 + h:9 + h, :] + mp
                mnew = jnp.maximum(gend, rows[12 + h:13 + h, :])
                w_row = jnp.exp(rows[4 + h:5 + h, :] - mnew)
                kw_t = (hd["kt"].astype(F32) * w_row).astype(BF16)
                dec = jnp.exp(gend - mnew)
                st[hd["idx"]][...] = (jnp.concatenate([dec, dec], axis=1) * hd["state"]
                                      + _dot(kw_t, hd["v_ones"]))
                ms[hd["idx"]][0:1, :] = mnew
        return carry

    lax.fori_loop(0, n_chunks, chunk_step, 0)

    for h in range(N_HEADS_M):
        hs = slice(DH_M * h, DH_M * (h + 1))
        hh = hdir[0][:, hs] + hdir[1][:, hs]
        y = hh * lax.rsqrt(jnp.mean(hh * hh, axis=-1, keepdims=True) + EPS) * nw_ref[:, hs]
        hm_ref[:, hs] = (jax.nn.sigmoid(om_ref[:, hs]) * y).astype(BF16)
    if not has_state:
        c_ref, n_ref, m_ref = refs[n_in + 1:n_in + 4]
        for d in range(2):
            for h in range(N_HEADS_M):
                idx = 4 * d + h
                c_ref[d, h] = st[idx][:, 0:DH_M]
                n_ref[d, h:h + 1, :] = st[idx][:, DH_M:].T[0:1, :]
                m_ref[idx:idx + 1, :] = ms[idx][0:1, :]


def _mlstm(qm, km, vm, g3, bg, om, nw, state, *, n_batch, seq, first_block):
    n_chunks = seq // CHUNK
    rb = seq // TB
    tok = lambda: pl.BlockSpec((seq, 512), lambda b: (b + first_block // rb, 0))
    full = lambda a: pl.BlockSpec(a.shape, lambda b: (0,) * a.ndim)
    in_specs = [tok(), tok(), tok(),
                pl.BlockSpec((n_chunks, 16, CHUNK), lambda b: (b + first_block // rb, 0, 0)),
                full(bg), tok(), full(nw)]
    args = [qm, km, vm, g3, bg, om, nw]
    hm_spec = pl.BlockSpec((seq, 512), lambda b: (b, 0))
    hm_shape = jax.ShapeDtypeStruct((n_batch * seq, 512), BF16)
    scratch = ([pltpu.VMEM((DH_M, 2 * DH_M), F32)] * 8 + [pltpu.VMEM((8, CHUNK), F32)] * 8
               + [pltpu.VMEM((seq, 512), F32)] * 2
               + [pltpu.VMEM((2 * n_chunks, 32, CHUNK), F32),
                  pltpu.VMEM((8 * n_chunks, CHUNK, CHUNK), F32),
                  pltpu.VMEM((8 * n_chunks, CHUNK, CHUNK), F32),
                  pltpu.VMEM((N_HEADS_M * n_chunks, DH_M, CHUNK), BF16)])
    if state is not None:
        c0, n0, m0 = state
        in_specs += [pl.BlockSpec((None, None, 2, N_HEADS_M, DH_M, DH_M), lambda b: (b, 0, 0, 0, 0, 0)),
                     pl.BlockSpec((None, None, 2, N_HEADS_M, DH_M), lambda b: (b, 0, 0, 0, 0)),
                     pl.BlockSpec((None, 8, CHUNK), lambda b: (b, 0, 0))]
        args += [c0, n0, m0]
        out_specs, out_shape = hm_spec, hm_shape
    else:
        out_specs = [hm_spec,
                     pl.BlockSpec((None, 2, N_HEADS_M, DH_M, DH_M), lambda b: (b, 0, 0, 0, 0)),
                     pl.BlockSpec((None, 2, N_HEADS_M, DH_M), lambda b: (b, 0, 0, 0)),
                     pl.BlockSpec((None, 8, CHUNK), lambda b: (b, 0, 0))]
        out_shape = [hm_shape,
                     jax.ShapeDtypeStruct((n_batch, 2, N_HEADS_M, DH_M, DH_M), F32),
                     jax.ShapeDtypeStruct((n_batch, 2, N_HEADS_M, DH_M), F32),
                     jax.ShapeDtypeStruct((n_batch, 8, CHUNK), F32)]
    return pl.pallas_call(
        functools.partial(_mlstm_kernel, has_state=state is not None, n_chunks=n_chunks),
        grid=(n_batch,),
        in_specs=in_specs,
        out_specs=out_specs,
        out_shape=out_shape,
        scratch_shapes=scratch,
        compiler_params=_cparams(("arbitrary",)),
        name="mlstm_lat" if state is not None else "mlstm_ctx",
    )(*args)


def _outproj_kernel(attc_ref, attl_ref, hmc_ref, hml_ref, xc_ref, xl_ref, mod_ref, nw_ref, wo_ref,
                    rwh_ref, rwl_ref, wsg_ref, wsu_ref, wsd_ref, base_ref, hn_ref, lt_ref):
    y = (_dot(_ctx_or_lat(attc_ref, attl_ref), wo_ref[0:D_ATT, :])
         + _dot(_ctx_or_lat(hmc_ref, hml_ref), wo_ref[D_ATT:, :]))
    x1 = _ctx_or_lat(xc_ref, xl_ref) + mod_ref[2:3, :] * y
    z = x1 * lax.rsqrt(jnp.mean(x1 * x1, axis=-1, keepdims=True) + EPS) * nw_ref[...]
    hn = z * (1.0 + mod_ref[4:5, :]) + mod_ref[3:4, :]
    hb = hn.astype(BF16)
    hl = (hn - hb.astype(F32)).astype(BF16)
    nt = lambda w, t: lax.dot_general(w, t, _NT, preferred_element_type=F32)
    lt_ref[...] = nt(rwh_ref[...], hb) + nt(rwl_ref[...], hb) + nt(rwh_ref[...], hl)
    a = _silu(_dot(hb, wsg_ref[...])) * _dot(hb, wsu_ref[...])
    shared = _dot(a.astype(BF16), wsd_ref[...])
    base_ref[...] = x1 + mod_ref[5:6, :] * shared
    packed = pltpu.pack_elementwise([hn[:, :512], hn[:, 512:]], packed_dtype=BF16)
    for c in range(4):
        hn_ref[pl.ds(c, TBP, stride=4), :] = packed[:, 128 * c:128 * (c + 1)]


def _outproj(att, hm, x, mod3, norm2_w, w_out, rw_hi, rw_lo, wsg, wsu, wsd):
    tok = lambda w: pl.BlockSpec((TBP, w), lambda i: (i, 0))
    full = lambda a: pl.BlockSpec(a.shape, lambda i: (0,) * a.ndim)
    return pl.pallas_call(
        _outproj_kernel,
        grid=(T_ALL // TBP,),
        in_specs=_split_specs(512) + _split_specs(512) + _split_specs(D_MODEL) + [
                  pl.BlockSpec((None, 6, D_MODEL), lambda i: (_mod_row(i), 0, 0)),
                  full(norm2_w), full(w_out), full(rw_hi), full(rw_lo), full(wsg), full(wsu), full(wsd)],
        out_specs=[tok(D_MODEL), pl.BlockSpec((4 * TBP, 128), lambda i: (i, 0)),
                   pl.BlockSpec((N_EXPERTS, TBP), lambda i: (0, i))],
        out_shape=[jax.ShapeDtypeStruct((T_ALL, D_MODEL), F32),
                   jax.ShapeDtypeStruct((4 * T_ALL, 128), jnp.uint32),
                   jax.ShapeDtypeStruct((N_EXPERTS, T_ALL), F32)],
        compiler_params=_cparams(("arbitrary",)),
        name="outproj",
    )(*att, *hm, *x, mod3, norm2_w, w_out, rw_hi, rw_lo, wsg, wsu, wsd)


def _first_max(vals, ids, limit):
    m = functools.reduce(jnp.maximum, [jnp.max(v, axis=0, keepdims=True) for v in vals])
    cand = [jnp.min(jnp.where(v == m, i, limit), axis=0, keepdims=True) for v, i in zip(vals, ids)]
    return m, functools.reduce(jnp.minimum, cand)


def _router_kernel(lt_ref, bias_ref, gate_ref, idx_ref):
    n = lt_ref.shape[1]
    score = jax.nn.sigmoid(lt_ref[...])
    biased = score + bias_ref[...]
    sub = lax.broadcasted_iota(jnp.int32, (8, n), 0).astype(F32)
    neg_inf = F32(-jnp.inf)
    slabs = [biased[8 * g:8 * (g + 1), :] for g in range(N_GROUPS)]
    gs = []
    for sl in slabs:
        m1, i1 = _first_max([sl], [sub], 8.0)
        m2 = jnp.max(jnp.where(sub == i1, neg_inf, sl), axis=0, keepdims=True)
        gs.append(m1 + m2)
    cur = jnp.concatenate(gs, axis=0)
    gsel = jnp.zeros((8, n), F32)
    for _ in range(TOPK_GROUPS):
        _, i = _first_max([cur], [sub], 8.0)
        hit = sub == i
        gsel = jnp.where(hit, 1.0, gsel)
        cur = jnp.where(hit, neg_inf, cur)
    vals = [jnp.where(gsel[g:g + 1, :] > 0.0, slabs[g], neg_inf) for g in range(N_GROUPS)]
    ids = [sub + 8.0 * g for g in range(N_GROUPS)]
    picked = [jnp.zeros((8, n), F32) for _ in range(N_GROUPS)]
    order = []
    for _ in range(TOP_K):
        _, i = _first_max(vals, ids, float(N_EXPERTS))
        order.append(i)
        hits = [idg == i for idg in ids]
        picked = [jnp.where(hh, score[8 * g:8 * (g + 1), :], p) for g, (p, hh) in enumerate(zip(picked, hits))]
        vals = [jnp.where(hh, neg_inf, v) for v, hh in zip(vals, hits)]
    total = functools.reduce(jnp.add, [jnp.sum(p, axis=0, keepdims=True) for p in picked])
    gate_t = jnp.concatenate([p / total * ROUTED_SCALE for p in picked]
                             + [jnp.zeros((128 - N_EXPERTS, n), F32)], axis=0)
    gate_ref[...] = gate_t.T
    idx_ref[...] = jnp.concatenate(order, axis=0).astype(jnp.int32)


def _router(logits_t, bias_col):
    return pl.pallas_call(
        _router_kernel,
        grid=(T_ALL // TB_MOE,),
        in_specs=[pl.BlockSpec((N_EXPERTS, TB_MOE), lambda i: (0, i)),
                  pl.BlockSpec((N_EXPERTS, 1), lambda i: (0, 0))],
        out_specs=[pl.BlockSpec((TB_MOE, 128), lambda i: (i, 0)),
                   pl.BlockSpec((TOP_K, TB_MOE), lambda i: (0, i))],
        out_shape=[jax.ShapeDtypeStruct((T_ALL, 128), F32),
                   jax.ShapeDtypeStruct((TOP_K, T_ALL), jnp.int32)],
        compiler_params=_cparams(("arbitrary",)),
        name="router",
    )(logits_t, bias_col)


def _plan_kernel(idx_ref, slot_ref, cnt_ref, off_ref, pos_sc):
    n_tiles = T_HALF // 128
    eid = lax.broadcasted_iota(jnp.int32, (N_EXPERTS, 128), 0)
    tri = (lax.broadcasted_iota(jnp.int32, (128, 128), 0)
           <= lax.broadcasted_iota(jnp.int32, (128, 128), 1)).astype(BF16)
    carry = jnp.zeros((N_EXPERTS, 1), F32)
    for j in range(n_tiles):
        it = idx_ref[:, 128 * j:128 * (j + 1)]
        sel = jnp.zeros((N_EXPERTS, 128), F32)
        for k in range(TOP_K):
            sel = jnp.where(it[k:k + 1, :] == eid, 1.0, sel)
        inc = _dot(sel.astype(BF16), tri) + carry
        carry = inc[:, 127:128]
        pos_sc[:, 128 * j:128 * (j + 1)] = inc - 1.0
    count = jnp.broadcast_to(carry, (N_EXPERTS, 128))
    padded = jnp.floor((count + (MT - 1.0)) * (1.0 / MT)) * MT
    before = (lax.broadcasted_iota(jnp.int32, (N_EXPERTS, N_EXPERTS), 1)
              < lax.broadcasted_iota(jnp.int32, (N_EXPERTS, N_EXPERTS), 0)).astype(BF16)
    hi, mid, lo = _split3(padded)
    off = _dot(before, hi) + _dot(before, mid) + _dot(before, lo)
    cnt_ref[...] = count.astype(jnp.int32)
    off_ref[...] = off.astype(jnp.int32)

    def slots(j):
        it = idx_ref[:, 128 * j:128 * (j + 1)]
        val = off + pos_sc[:, 128 * j:128 * (j + 1)]
        rows = [jnp.sum(jnp.where(it[k:k + 1, :] == eid, val, 0.0), axis=0, keepdims=True)
                for k in range(TOP_K)]
        return jnp.concatenate(rows, axis=0).astype(jnp.int32)

    for j in range(n_tiles):
        s = slots(j)
        slot_ref[:, 128 * j:128 * (j + 1)] = s[0:TOP_K // 2, :] | (s[TOP_K // 2:, :] << 16)


def _plan(idx8):
    return pl.pallas_call(
        _plan_kernel,
        grid=(2,),
        in_specs=[pl.BlockSpec((TOP_K, T_HALF), lambda h: (0, h))],
        out_specs=[pl.BlockSpec((None, TOP_K // 2, T_HALF), lambda h: (h, 0, 0)),
                   pl.BlockSpec((None, N_EXPERTS, 128), lambda h: (h, 0, 0)),
                   pl.BlockSpec((None, N_EXPERTS, 128), lambda h: (h, 0, 0))],
        out_shape=[jax.ShapeDtypeStruct((2, TOP_K // 2, T_HALF), jnp.int32),
                   jax.ShapeDtypeStruct((2, N_EXPERTS, 128), jnp.int32),
                   jax.ShapeDtypeStruct((2, N_EXPERTS, 128), jnp.int32)],
        scratch_shapes=[pltpu.VMEM((N_EXPERTS, T_HALF), F32)],
        compiler_params=_cparams(("arbitrary",)),
        name="plan",
    )(idx8)


def _prepare_half(slot_ref, cnt_ref, off_ref, tok_ref, tile_e, tile_first, elist, h):
    def per_expert(x, carry):
        j, q = carry
        n = cnt_ref[h * N_EXPERTS + x]
        first = off_ref[h * N_EXPERTS + x]
        tiles = (n + MT - 1) // MT
        elist[q] = x

        def mark(i, c):
            tile_e[j + i] = x
            tile_first[j + i] = jnp.where(i == 0, 1, 0)
            return c

        lax.fori_loop(0, tiles, mark, 0)

        @pl.when(tiles > 0)
        def _():
            last = first + (tiles - 1) * MT
            for u in range(MT):
                tok_ref[last + u] = T_HALF

        return j + tiles, q + jnp.where(tiles > 0, 1, 0)

    n_tiles, n_live = lax.fori_loop(0, N_EXPERTS, per_expert, (0, 0))
    for extra in range(2):
        tile_e[n_tiles + extra] = 0
        tile_first[n_tiles + extra] = 0
        for u in range(MT):
            tok_ref[(n_tiles + extra) * MT + u] = T_HALF
    return n_tiles, n_live


def _invert_slots(slot_ref, tok_ref, acc_v, h):
    rounds = TOP_K // 2
    trips = T_HALF // INV_UNROLL
    zero_rows = 8 * T_HALF // (rounds * trips)
    assert zero_rows * rounds * trips == 8 * T_HALF and zero_rows % 8 == 0
    for k in range(rounds):
        def body(i, carry, k=k):
            words = [slot_ref[(h * rounds + k) * T_HALF + i * INV_UNROLL + u] for u in range(INV_UNROLL)]
            for u in range(INV_UNROLL):
                t = i * INV_UNROLL + u
                tok_ref[words[u] & 0xFFFF] = t
                tok_ref[lax.shift_right_logical(words[u], 16)] = t
            row = pl.multiple_of((k * trips + i) * zero_rows, 8)
            acc_v[pl.ds(row, zero_rows), :] = jnp.zeros((zero_rows, 128), F32)
            return carry

        lax.fori_loop(0, trips, body, 0)
    acc_v[pl.ds(8 * T_HALF, 8), :] = jnp.zeros((8, 128), F32)


def _moe_kernel(slot_ref, cnt_ref, off_ref, src_hbm, gate_hbm, wg_hbm, wu_hbm, wd_hbm, base_hbm, mod_ref,
                outc_hbm, outl_hbm,
                src_v, gate_v, acc_v, wg_l, wu_l, wd_l, wgu_b, wd_b, xbuf0, xbuf1, gbuf0, gbuf1,
                ybuf0, ybuf1, tok_ref, tile_e, tile_first, elist, live, base_buf, out_buf, sem, wsem, fsem):
    h = pl.program_id(0)
    xbuf, gbuf, ybuf = (xbuf0, xbuf1), (gbuf0, gbuf1), (ybuf0, ybuf1)

    src_cp = pltpu.make_async_copy(src_hbm.at[pl.ds(pl.multiple_of(h * (4 * T_HALF), 8), 4 * T_HALF)],
                                   src_v.at[pl.ds(0, 4 * T_HALF)], sem.at[0])
    gate_cp = pltpu.make_async_copy(gate_hbm.at[pl.ds(pl.multiple_of(h * T_HALF, 8), T_HALF)],
                                    gate_v.at[pl.ds(0, T_HALF)], sem.at[1])
    src_cp.start()
    gate_cp.start()

    def weight_copies(x, slot):
        return [pltpu.make_async_copy(w_hbm.at[x], w_l.at[slot], wsem.at[slot, i])
                for i, (w_hbm, w_l) in enumerate(((wg_hbm, wg_l), (wu_hbm, wu_l), (wd_hbm, wd_l)))]

    n_tiles, n_live = _prepare_half(slot_ref, cnt_ref, off_ref, tok_ref, tile_e, tile_first, elist, h)
    live[0] = 0
    live[1] = n_live

    for ahead in range(W_SLOTS - 1):
        @pl.when(ahead < n_live)
        def _(ahead=ahead):
            for cp in weight_copies(elist[ahead], ahead):
                cp.start()

    zero = jnp.zeros((8, 128), F32)
    src_v[pl.ds(4 * T_HALF, 8), :] = pltpu.pack_elementwise([zero, zero], packed_dtype=BF16)
    gate_v[pl.ds(T_HALF, 8), :] = zero
    ybuf0[...] = jnp.zeros(ybuf0.shape, F32)
    ybuf1[...] = jnp.zeros(ybuf1.shape, F32)
    _invert_slots(slot_ref, tok_ref, acc_v, h)
    src_cp.wait()
    gate_cp.wait()

    def switch_expert():
        q = live[0]
        slot = q % W_SLOTS
        for cp in weight_copies(elist[q], slot):
            cp.wait()
        wgu_b[:, 0:D_EXPERT] = wg_l[slot].astype(BF16)
        wgu_b[:, D_EXPERT:] = wu_l[slot].astype(BF16)
        wd_b[...] = wd_l[slot].astype(BF16)
        live[0] = q + 1
        nxt = q + W_SLOTS - 1

        @pl.when(nxt < live[1])
        def _():
            for cp in weight_copies(elist[nxt], nxt % W_SLOTS):
                cp.start()

    def gather(j, xb, gb, rows=(0, MT)):
        base = j * MT
        for m in range(*rows):
            t = tok_ref[base + m]
            xb[pl.ds(m, 4, stride=MT_STRIDE), :] = src_v[pl.ds(pl.multiple_of(t * 4, 4), 4), :]
            gb[m:m + 1, :] = gate_v[pl.ds(t, 1), :]

    def scatter(j, yb, rows=(0, MT)):
        base = j * MT
        for b in range(rows[0] // RMW_BATCH, rows[1] // RMW_BATCH):
            ms = [b * RMW_BATCH + u for u in range(RMW_BATCH)]
            targets = [pl.ds(pl.multiple_of(tok_ref[base + m] * 8, 8), 8) for m in ms]
            vals = [acc_v[r, :] + yb[pl.ds(m, 8, stride=MT_STRIDE), :] for r, m in zip(targets, ms)]
            for r, v in zip(targets, vals):
                acc_v[r, :] = v

    quarters = [(q * MT // 4, (q + 1) * MT // 4) for q in range(4)]

    def step(j, p):
        pl.when(tile_first[j] == 1)(switch_expert)
        xb, gb, yb = xbuf[p], gbuf[p], ybuf[p]
        nxt = (j + 1, xbuf[1 - p], gbuf[1 - p])
        prv = (jnp.maximum(j - 1, 0), ybuf[1 - p])
        lo, hi = [], []
        for c in range(4):
            words = xb[MT_STRIDE * c:MT_STRIDE * c + MT, :]
            unpack = functools.partial(pltpu.unpack_elementwise, words, packed_dtype=BF16, unpacked_dtype=F32)
            lo.append(unpack(index=0).astype(BF16))
            hi.append(unpack(index=1).astype(BF16))
        x = jnp.concatenate(lo + hi, axis=1)
        gather(*nxt, rows=quarters[0])
        h_gate = _dot(x, wgu_b[:, 0:D_EXPERT])
        gather(*nxt, rows=quarters[1])
        h_up = _dot(x, wgu_b[:, D_EXPERT:])
        gather(*nxt, rows=quarters[2])
        g = gb[...]
        g_hi = g.astype(BF16)
        g_lo = (g - g_hi.astype(F32)).astype(BF16)
        pick = (lax.broadcasted_iota(jnp.int32, (128, D_EXPERT), 0) == tile_e[j]).astype(BF16)
        g_col = _dot(g_hi, pick) + _dot(g_lo, pick)
        a = (_silu(h_gate) * h_up * g_col).astype(BF16)
        gather(*nxt, rows=quarters[3])
        scatter(*prv, rows=quarters[0])
        y_lo = _dot(a, wd_b[:, 0:D_MODEL // 2])
        scatter(*prv, rows=quarters[1])
        y_hi = _dot(a, wd_b[:, D_MODEL // 2:])
        scatter(*prv, rows=quarters[2])
        for c in range(4):
            yb[MT_STRIDE * c:MT_STRIDE * c + MT, :] = y_lo[:, 128 * c:128 * (c + 1)]
            yb[MT_STRIDE * (c + 4):MT_STRIDE * (c + 4) + MT, :] = y_hi[:, 128 * c:128 * (c + 1)]
        scatter(*prv, rows=quarters[3])

    gather(0, xbuf[0], gbuf[0])
    n_pairs = (n_tiles + 1) // 2

    def pair(i, carry):
        step(2 * i, 0)
        step(2 * i + 1, 1)
        return carry

    lax.fori_loop(0, n_pairs, pair, 0)
    scatter(jnp.maximum(2 * n_pairs - 1, 0), ybuf[1])

    n_fin = T_HALF // FIN

    def first_token(c):
        return h * T_HALF + c * FIN

    def base_copy(c, slot):
        return pltpu.make_async_copy(base_hbm.at[pl.ds(pl.multiple_of(first_token(c), FIN), FIN)],
                                     base_buf.at[slot], fsem.at[0, slot])

    def on_out_copy(c, action):
        t0 = first_token(c)
        slot = c % FIN_OUT_SLOTS

        @pl.when(t0 < T_CTX)
        def _():
            action(pltpu.make_async_copy(out_buf.at[slot], outc_hbm.at[pl.ds(pl.multiple_of(t0, FIN), FIN)],
                                         fsem.at[1, slot]))

        @pl.when(t0 >= T_CTX)
        def _():
            action(pltpu.make_async_copy(out_buf.at[slot],
                                         outl_hbm.at[pl.ds(pl.multiple_of(t0 - T_CTX, FIN), FIN)],
                                         fsem.at[1, slot]))

    for c in range(FIN_BASE_SLOTS - 1):
        base_copy(c, c).start()

    def combine(c, carry):
        slot = c % FIN_BASE_SLOTS
        oslot = c % FIN_OUT_SLOTS
        base_copy(c, slot).wait()
        ahead = c + FIN_BASE_SLOTS - 1

        @pl.when(ahead < n_fin)
        def _():
            base_copy(ahead, ahead % FIN_BASE_SLOTS).start()

        @pl.when(c >= FIN_OUT_SLOTS)
        def _():
            on_out_copy(c - FIN_OUT_SLOTS, lambda cp: cp.wait())

        t0 = first_token(c)
        mod_row = jnp.where(t0 < T_CTX, 0, 1 + (t0 - T_CTX) // DEC_SEQ)
        acc_row = pl.multiple_of(8 * FIN * c, 8)
        for cc in range(8):
            cs = slice(128 * cc, 128 * (cc + 1))
            routed = acc_v[pl.ds(acc_row + cc, FIN, stride=8), :]
            out_buf[oslot, :, cs] = base_buf[slot, :, cs] + mod_ref[mod_row, 5:6, cs] * routed
        on_out_copy(c, lambda cp: cp.start())
        return carry

    lax.fori_loop(0, n_fin, combine, 0)
    for c in range(n_fin - FIN_OUT_SLOTS, n_fin):
        on_out_copy(c, lambda cp: cp.wait())


def _moe(slot_words, cnt, off, src, gate, wg, wu, wd, base, mod3):
    any_spec = pl.BlockSpec(memory_space=pl.ANY)
    tile_buf = lambda rows, dt: pltpu.VMEM((rows * MT_STRIDE, 128), dt)
    return pl.pallas_call(
        _moe_kernel,
        grid_spec=pltpu.PrefetchScalarGridSpec(
            num_scalar_prefetch=3,
            grid=(2,),
            in_specs=[any_spec] * 6 + [pl.BlockSpec(mod3.shape, lambda h, *_: (0, 0, 0))],
            out_specs=[any_spec, any_spec],
            scratch_shapes=[pltpu.VMEM((4 * T_HALF + 8, 128), jnp.uint32),
                            pltpu.VMEM((T_HALF + 8, 128), F32),
                            pltpu.VMEM((8 * T_HALF + 8, 128), F32),
                            pltpu.VMEM((W_SLOTS, D_MODEL, D_EXPERT), F32),
                            pltpu.VMEM((W_SLOTS, D_MODEL, D_EXPERT), F32),
                            pltpu.VMEM((W_SLOTS, D_EXPERT, D_MODEL), F32),
                            pltpu.VMEM((D_MODEL, 2 * D_EXPERT), BF16),
                            pltpu.VMEM((D_EXPERT, D_MODEL), BF16),
                            tile_buf(4, jnp.uint32), tile_buf(4, jnp.uint32),
                            pltpu.VMEM((MT, 128), F32), pltpu.VMEM((MT, 128), F32),
                            tile_buf(8, F32), tile_buf(8, F32),
                            pltpu.SMEM((SLOT_CAP,), jnp.int32),
                            pltpu.SMEM((NT_MAX + 2,), jnp.int32),
                            pltpu.SMEM((NT_MAX + 2,), jnp.int32),
                            pltpu.SMEM((N_EXPERTS,), jnp.int32),
                            pltpu.SMEM((2,), jnp.int32),
                            pltpu.VMEM((FIN_BASE_SLOTS, FIN, D_MODEL), F32),
                            pltpu.VMEM((FIN_OUT_SLOTS, FIN, D_MODEL), F32),
                            pltpu.SemaphoreType.DMA((2,)),
                            pltpu.SemaphoreType.DMA((W_SLOTS, 3)),
                            pltpu.SemaphoreType.DMA((2, FIN_BASE_SLOTS))]),
        out_shape=[jax.ShapeDtypeStruct((T_CTX, D_MODEL), F32), jax.ShapeDtypeStruct((T_LAT, D_MODEL), F32)],
        compiler_params=_cparams(("arbitrary",)),
        name="moe",
    )(slot_words, cnt, off, src, gate, wg, wu, wd, base, mod3)


def _head_indicators(width):
    head = jnp.arange(width) // DH_ATT
    ind = (head[:, None] == jnp.arange(128)[None, :]).astype(BF16)
    return ind, ind.T


def kernel(x_prompt, x_sample, cache_attn_k, cache_attn_v, state_mlstm_c, state_mlstm_n, state_mlstm_m, c, c_ctx, w_mod, b_mod, norm1_w, norm2_w, w_in, q_norm_w, k_norm_w, b_gates, m_norm_w, w_out, router_w, router_bias, w_gate, w_up, w_down, ws_gate, ws_up, ws_down):
    x = (x_prompt.reshape(T_CTX, D_MODEL), x_sample.reshape(T_LAT, D_MODEL))
    mod3 = _modulation(c, c_ctx, w_mod[0], b_mod[0])

    w_main = w_in[0]
    w_gates_t = w_in[0, :, P_MAIN:].T.astype(BF16)
    qw = jnp.tile(q_norm_w, (1, N_HEADS_ATT))
    kw = jnp.tile(k_norm_w, (1, N_KV_HEADS))
    inds = _head_indicators(512) + _head_indicators(128)
    qn, kn, va, qm, km, vm, om, gt = _inproj(*x, mod3, norm1_w, w_main, w_gates_t, qw, kw, inds, _rope_tables())

    att_c = _attention(qn, kn, va, None, n_batch=BATCH, seq=SEQ, first_row=0)
    att_l = _attention(qn, kn, va, (cache_attn_k, cache_attn_v), n_batch=DEC_BATCH, seq=DEC_SEQ,
                       first_row=T_CTX)

    g3 = gt.reshape(16, T_ALL // CHUNK, CHUNK).transpose(1, 0, 2)
    bg = b_gates.reshape(16, 1)
    hm_c, c_new, n_new, m_new = _mlstm(qm, km, vm, g3, bg, om, m_norm_w, None,
                                       n_batch=BATCH, seq=SEQ, first_block=0)
    m0 = jnp.broadcast_to(state_mlstm_m.reshape(DEC_BATCH, 8, 1), (DEC_BATCH, 8, CHUNK))
    hm_l = _mlstm(qm, km, vm, g3, bg, om, m_norm_w, (state_mlstm_c, state_mlstm_n, m0),
                  n_batch=DEC_BATCH, seq=DEC_SEQ, first_block=N_CTX_BLOCKS)

    rw_t = router_w[0].T
    rw_hi = rw_t.astype(BF16)
    rw_lo = (rw_t - rw_hi.astype(F32)).astype(BF16)
    base, hn2, logits_t = _outproj((att_c, att_l), (hm_c, hm_l), x, mod3, norm2_w, w_out[0].astype(BF16),
                                   rw_hi, rw_lo, ws_gate[0].astype(BF16), ws_up[0].astype(BF16),
                                   ws_down[0].astype(BF16))
    gate, idx8 = _router(logits_t, router_bias.reshape(N_EXPERTS, 1))
    slot_words, cnt, off = _plan(idx8)
    out_c, out_l = _moe(slot_words.reshape(TOP_K * T_HALF), cnt[:, :, 0].reshape(2 * N_EXPERTS),
                        off[:, :, 0].reshape(2 * N_EXPERTS), hn2, gate, w_gate[0], w_up[0], w_down[0],
                        base, mod3)

    y_prompt = out_c.reshape(BATCH, SEQ, D_MODEL)
    y_sample = out_l.reshape(DEC_BATCH, DEC_SEQ, D_MODEL)
    to_cache = lambda a: a[:T_CTX].reshape(BATCH, SEQ, N_KV_HEADS, DH_ATT).transpose(0, 2, 1, 3)[:, None]
    new_k = to_cache(kn)
    new_v = to_cache(va)
    new_m = m_new[:, :, 0].reshape(BATCH, 1, 2, N_HEADS_M)
    return (y_prompt, y_sample, new_k, new_v, c_new[:, None], n_new[:, None], new_m)
```

```python
import functools

import jax
import jax.numpy as jnp
from jax import lax
from jax.experimental import pallas as pl
from jax.experimental.pallas import tpu as pltpu

F32 = jnp.float32
BF16 = jnp.bfloat16

D_MODEL = 1024
BATCH = 32
SEQ = 256
DEC_BATCH = 2
DEC_SEQ = 1024
PAST_LEN = 256
GRID_W = 64
N_HEADS_ATT = 8
N_KV_HEADS = 2
DH_ATT = 64
D_ATT = 512
ROPE_THETA = 10000.0
N_HEADS_M = 4
DH_M = 128
D_M = 512
CHUNK = 128
N_EXPERTS = 64
TOP_K = 8
N_GROUPS = 8
TOPK_GROUPS = 4
D_EXPERT = 256
ROUTED_SCALE = 2.5
EPS = 1e-6
NEG_INIT = -1e30

T_CTX = BATCH * SEQ
T_LAT = DEC_BATCH * DEC_SEQ
T_ALL = T_CTX + T_LAT
TB = 256
N_CTX_BLOCKS = T_CTX // TB
TBP = 512
NP_CTX = T_CTX // TBP
NP_LAT_PER_BATCH = DEC_SEQ // TBP
TB_MOE = 1024
T_HALF = T_ALL // 2
MT = 256
MT_STRIDE = MT + 8
NT_MAX = TOP_K * T_HALF // MT + N_EXPERTS
SLOT_CAP = (NT_MAX + 2) * MT
RMW_BATCH = 16
INV_UNROLL = 16
W_SLOTS = 2
FIN = 128
FIN_BASE_SLOTS = 6
FIN_OUT_SLOTS = 4
assert FIN_OUT_SLOTS <= FIN_BASE_SLOTS
assert SLOT_CAP < 2 ** 16 and T_HALF % INV_UNROLL == 0 and MT % RMW_BATCH == 0
P_MAIN = 2816
VMEM_LIMIT = 56 * 1024 * 1024

_NT = (((1,), (1,)), ((), ()))
_TN = (((0,), (0,)), ((), ()))


def _cparams(sem):
    return pltpu.CompilerParams(dimension_semantics=sem, vmem_limit_bytes=VMEM_LIMIT)


def _split3(x):
    hi = x.astype(BF16)
    r1 = x - hi.astype(F32)
    mid = r1.astype(BF16)
    lo = (r1 - mid.astype(F32)).astype(BF16)
    return hi, mid, lo


def _dot(a, b):
    return jnp.dot(a, b, preferred_element_type=F32)


def _dot3(x, m_bf16):
    hi, mid, lo = _split3(x)
    return _dot(hi, m_bf16) + _dot(mid, m_bf16) + _dot(lo, m_bf16)


def _dot2(x, m_bf16):
    hi = x.astype(BF16)
    lo = (x - hi.astype(F32)).astype(BF16)
    return _dot(hi, m_bf16) + _dot(lo, m_bf16)


def _silu(x):
    return x * jax.nn.sigmoid(x)


def _mod_row(i):
    return jnp.where(i < NP_CTX, 0, 1 + (i - NP_CTX) // NP_LAT_PER_BATCH)


def _mod_kernel(ct_ref, w_ref, b_ref, o_ref):
    s = _silu(ct_ref[...])
    w = w_ref[...]
    rows = [jnp.sum(w * s[:, r:r + 1], axis=0, keepdims=True) for r in range(3)]
    rows.append(jnp.zeros((5, w.shape[1]), F32))
    o_ref[...] = jnp.concatenate(rows, axis=0) + b_ref[...]


def _modulation(c, c_ctx, w_mod, b_mod):
    cvec = jnp.concatenate([c_ctx[None, :], c, jnp.zeros((5, D_MODEL), F32)], axis=0)
    nb = 1024
    out = pl.pallas_call(
        _mod_kernel,
        grid=(6 * D_MODEL // nb,),
        in_specs=[pl.BlockSpec((D_MODEL, 8), lambda j: (0, 0)),
                  pl.BlockSpec((D_MODEL, nb), lambda j: (0, j)),
                  pl.BlockSpec((1, nb), lambda j: (0, j))],
        out_specs=pl.BlockSpec((8, nb), lambda j: (0, j)),
        out_shape=jax.ShapeDtypeStruct((8, 6 * D_MODEL), F32),
        compiler_params=_cparams(("arbitrary",)),
        name="modulation",
    )(cvec.T, w_mod, b_mod[None, :])
    return out.reshape(8, 6, D_MODEL)


def _ctx_or_lat(ctx_ref, lat_ref):
    return jnp.where(pl.program_id(0) < NP_CTX, ctx_ref[...], lat_ref[...])


def _split_specs(width):
    return [pl.BlockSpec((TBP, width), lambda i: (jnp.minimum(i, NP_CTX - 1), 0)),
            pl.BlockSpec((TBP, width), lambda i: (jnp.maximum(i - NP_CTX, 0), 0))]


def _head_norm(x, ind, ind_t, w_row):
    ss = _dot2(x * x, ind)
    inv = lax.rsqrt(ss * (1.0 / DH_ATT) + EPS)
    return x * _dot2(inv, ind_t) * w_row


def _rope(x, cos, sin_signed):
    lane = lax.broadcasted_iota(jnp.int32, x.shape, 1)
    partner = jnp.where((lane % 32) < 16, pltpu.roll(x, 128 - 16, 1), pltpu.roll(x, 16, 1))
    return x * cos + partner * sin_signed


def _inproj_kernel(xc_ref, xl_ref, mod_ref, nw_ref, w_ref, wgt_ref, qw_ref, kw_ref, iq_ref, iqt_ref,
                   ik_ref, ikt_ref, cos_ref, sin_ref,
                   qn_ref, kn_ref, va_ref, qm_ref, km_ref, vm_ref, om_ref, gt_ref, kc_ref, vc_ref, w_b):
    @pl.when(pl.program_id(0) == 0)
    def _():
        for j in range(P_MAIN // 128):
            w_b[:, 128 * j:128 * (j + 1)] = w_ref[:, 128 * j:128 * (j + 1)].astype(BF16)

    x = _ctx_or_lat(xc_ref, xl_ref)
    y = x * lax.rsqrt(jnp.mean(x * x, axis=-1, keepdims=True) + EPS) * nw_ref[...]
    hn = y * (1.0 + mod_ref[1:2, :]) + mod_ref[0:1, :]
    hb = hn.astype(BF16)
    qn = _head_norm(_dot(hb, w_b[:,0:512]), iq_ref[...], iqt_ref[...], qw_ref[...])
    kn = _head_norm(_dot(hb, w_b[:,512:640]), ik_ref[...], ikt_ref[...], kw_ref[...])
    qn_ref[...] = qn.astype(BF16)
    kn_ref[...] = kn

    @pl.when(pl.program_id(0) >= NP_CTX)
    def _():
        cos, sin = cos_ref[...], sin_ref[...]
        for j in range(4):
            qn_ref[:, 128 * j:128 * (j + 1)] = _rope(qn[:, 128 * j:128 * (j + 1)], cos, sin).astype(BF16)
        kn_ref[...] = _rope(kn, cos, sin)

    va = _dot(hb, w_b[:,640:768])
    va_ref[...] = va

    @pl.when(pl.program_id(0) < NP_CTX)
    def _():
        for bb in range(TBP // SEQ):
            for g in range(N_KV_HEADS):
                rows, lanes = slice(SEQ * bb, SEQ * (bb + 1)), slice(DH_ATT * g, DH_ATT * (g + 1))
                kc_ref[bb, g] = kn[rows, lanes]
                vc_ref[bb, g] = va[rows, lanes]

    qm_ref[...] = _dot(hb, w_b[:,768:1280]).astype(BF16)
    km_ref[...] = (_dot(hb, w_b[:,1280:1792]) * (DH_M ** -0.5)).astype(BF16)
    vm_ref[...] = _dot(hb, w_b[:,1792:2304]).astype(BF16)
    om_ref[...] = _dot(hb, w_b[:,2304:2816])
    gt_ref[...] = lax.dot_general(wgt_ref[...], hb, _NT, preferred_element_type=F32)


def _inproj(x_ctx, x_lat, mod3, norm1_w, w_main, w_gates_t, qw, kw, inds, rope_tabs):
    tok = lambda w: pl.BlockSpec((TBP, w), lambda i: (i, 0))
    full = lambda a: pl.BlockSpec(a.shape, lambda i: (0,) * a.ndim)
    sd = lambda w, dt: jax.ShapeDtypeStruct((T_ALL, w), dt)
    rope_spec = pl.BlockSpec((TBP, 128), lambda i: (jnp.maximum(i - NP_CTX, 0) % NP_LAT_PER_BATCH, 0))
    cache_spec = pl.BlockSpec((TBP // SEQ, N_KV_HEADS, SEQ, DH_ATT),
                              lambda i: (jnp.minimum(i, NP_CTX - 1), 0, 0, 0))
    cache_shape = jax.ShapeDtypeStruct((BATCH, N_KV_HEADS, SEQ, DH_ATT), F32)
    return pl.pallas_call(
        _inproj_kernel,
        grid=(T_ALL // TBP,),
        in_specs=_split_specs(D_MODEL) + [
                  pl.BlockSpec((None, 6, D_MODEL), lambda i: (_mod_row(i), 0, 0)),
                  full(norm1_w), full(w_main), full(w_gates_t), full(qw), full(kw)]
                 + [full(a) for a in inds] + [rope_spec, rope_spec],
        out_specs=[tok(512), tok(128), tok(128), tok(512), tok(512), tok(512), tok(512),
                   pl.BlockSpec((16, TBP), lambda i: (0, i)), cache_spec, cache_spec],
        out_shape=[sd(512, BF16), sd(128, F32), sd(128, F32), sd(512, BF16), sd(512, BF16),
                   sd(512, BF16), sd(512, F32), jax.ShapeDtypeStruct((16, T_ALL), F32),
                   cache_shape, cache_shape],
        scratch_shapes=[pltpu.VMEM((D_MODEL, P_MAIN), BF16)],
        compiler_params=_cparams(("arbitrary",)),
        name="inproj",
    )(x_ctx, x_lat, mod3, norm1_w, w_main, w_gates_t, qw, kw, *inds, *rope_tabs)


def _rope_tables():
    t = jnp.arange(DEC_SEQ)
    pos = jnp.stack([t // GRID_W, t % GRID_W], axis=1).astype(F32)
    n_freq = DH_ATT // 4
    inv_freq = ROPE_THETA ** (-jnp.arange(n_freq, dtype=F32) / n_freq)
    ang = pos[:, :, None] * inv_freq
    cos, sin = jnp.cos(ang), jnp.sin(ang)
    cos_h = jnp.stack([cos, cos], axis=2).reshape(DEC_SEQ, DH_ATT)
    sin_h = jnp.stack([-sin, sin], axis=2).reshape(DEC_SEQ, DH_ATT)
    return jnp.tile(cos_h, (1, 2)), jnp.tile(sin_h, (1, 2))


def _attn_kernel(*refs, has_cache):
    if has_cache:
        q_ref, k_ref, v_ref, kc_ref, vc_ref, o_ref = refs
    else:
        q_ref, k_ref, v_ref, o_ref = refs
    q = q_ref[...] * jnp.asarray(DH_ATT ** -0.5, BF16)
    k = k_ref[...].astype(BF16)
    v = v_ref[...].astype(BF16)
    qb = q.shape[0]
    low_half = lax.broadcasted_iota(jnp.int32, (1, 128), 1) < DH_ATT
    for g in range(N_KV_HEADS):
        kg = k[:, DH_ATT * g:DH_ATT * (g + 1)]
        vg = v[:, DH_ATT * g:DH_ATT * (g + 1)]
        if has_cache:
            kg = jnp.concatenate([kg, kc_ref[g].astype(BF16)], axis=0)
            vg = jnp.concatenate([vg, vc_ref[g].astype(BF16)], axis=0)
        n_keys = kg.shape[0]
        zero = jnp.zeros_like(kg)
        k2 = jnp.concatenate([jnp.concatenate([kg, zero], axis=1), jnp.concatenate([zero, kg], axis=1)], axis=0)
        v2 = jnp.concatenate([jnp.concatenate([vg, zero], axis=1), jnp.concatenate([zero, vg], axis=1)], axis=0)
        ones2 = jnp.concatenate([jnp.broadcast_to(jnp.where(low_half, 1.0, 0.0), (n_keys, 128)),
                                 jnp.broadcast_to(jnp.where(low_half, 0.0, 1.0), (n_keys, 128))],
                                axis=0).astype(BF16)
        qp = jnp.concatenate([q[:, 256 * g:256 * g + 128], q[:, 256 * g + 128:256 * g + 256]], axis=0)
        s = lax.dot_general(qp, k2, _NT, preferred_element_type=F32)
        halves = [s[:, 0:n_keys], s[:, n_keys:]]
        p = jnp.concatenate([jnp.exp(sh - jnp.max(sh, axis=-1, keepdims=True)) for sh in halves],
                            axis=1).astype(BF16)
        ov = _dot(p, jnp.concatenate([v2, ones2], axis=1))
        o = ov[:, 0:128] / ov[:, 128:]
        o_ref[:, 256 * g:256 * g + 128] = o[0:qb, :].astype(BF16)
        o_ref[:, 256 * g + 128:256 * g + 256] = o[qb:, :].astype(BF16)


def _attention(qn, kn, v_all, cache, *, n_batch, seq, first_row):
    qblocks = seq // TB
    kv_spec = pl.BlockSpec((seq, 128), lambda b, i: (b + first_row // seq, 0))
    in_specs = [pl.BlockSpec((TB, 512), lambda b, i: (first_row // TB + b * qblocks + i, 0)), kv_spec, kv_spec]
    args = [qn, kn, v_all]
    if cache is not None:
        cspec = pl.BlockSpec((None, None, N_KV_HEADS, PAST_LEN, DH_ATT), lambda b, i: (b, 0, 0, 0, 0))
        in_specs += [cspec, cspec]
        args += list(cache)
    return pl.pallas_call(
        functools.partial(_attn_kernel, has_cache=cache is not None),
        grid=(n_batch, qblocks),
        in_specs=in_specs,
        out_specs=pl.BlockSpec((TB, 512), lambda b, i: (b * qblocks + i, 0)),
        out_shape=jax.ShapeDtypeStruct((n_batch * seq, 512), BF16),
        compiler_params=_cparams(("arbitrary", "arbitrary")),
        name="attention_lat" if cache is not None else "attention_ctx",
    )(*args)


def _log_sigmoid(x):
    return jnp.minimum(x, 0.0) - jnp.log1p(jnp.exp(-jnp.abs(x)))


def _col_bcast(cols, j):
    return jnp.broadcast_to(cols[:, j:j + 1], (CHUNK, CHUNK))


def _mlstm_kernel(*refs, has_state, n_chunks):
    n_in = 10 if has_state else 7
    n_out = 1 if has_state else 4
    q_ref, k_ref, v_ref, g_ref, bg_ref, om_ref, nw_ref = refs[:7]
    hm_ref = refs[n_in]
    scratch = refs[n_in + n_out:]
    st = scratch[0:8]
    ms = scratch[8:16]
    hdir = scratch[16:18]
    rows_sc, cmb_sc, bb_sc, kt_sc = scratch[18:22]
    neg_inf = F32(-jnp.inf)
    zeros112 = jnp.zeros((CHUNK - 16, CHUNK), F32)
    sub = lax.broadcasted_iota(jnp.int32, (CHUNK, CHUNK), 0)
    lan = lax.broadcasted_iota(jnp.int32, (CHUNK, CHUNK), 1)
    ones = jnp.ones((CHUNK, CHUNK), BF16)

    stats = []
    for c in range(n_chunks):
        pre = g_ref[c] + bg_ref[...]
        logf = _log_sigmoid(pre)
        for d in range(2):
            bcum = _dot3(logf, ((sub <= lan) if d == 0 else (sub >= lan)).astype(BF16))
            li4 = pre[4 * d:4 * d + 4, :]
            lf4 = logf[8 + 4 * d:12 + 4 * d, :]
            b4 = bcum[8 + 4 * d:12 + 4 * d, :]
            stats.append((c, d, li4 - b4, lf4, b4))
    for c, d, r4, lf4, b4 in stats:
        blast4 = b4[:, CHUNK - 1:CHUNK] if d == 0 else b4[:, 0:1]
        wlog4 = blast4 + r4
        wmax4 = jnp.max(wlog4, axis=-1, keepdims=True)
        full = lambda a: jnp.broadcast_to(a, (4, CHUNK))
        rows_sc[2 * c + d] = jnp.concatenate(
            [r4, wlog4, full(blast4), full(wmax4), jnp.zeros((16, CHUNK), F32)], axis=0)
    for c in range(n_chunks):
        for h in range(N_HEADS_M):
            kc = k_ref[CHUNK * c:CHUNK * (c + 1), DH_M * h:DH_M * (h + 1)]
            kt_sc[c * N_HEADS_M + h] = kc.astype(F32).T.astype(BF16)
    for c, d, r4, lf4, b4 in stats:
        within = (lan <= sub) if d == 0 else (lan >= sub)
        for h in range(N_HEADS_M):
            cummax = jnp.max(jnp.where(within, r4[h:h + 1, :], neg_inf), axis=-1, keepdims=True)
            cmb_sc[(2 * c + d) * N_HEADS_M + h] = jnp.broadcast_to(cummax, (CHUNK, CHUNK))
    for c, d, r4, lf4, b4 in stats:
        within = (lan <= sub) if d == 0 else (lan >= sub)
        for h in range(N_HEADS_M):
            bb_sc[(2 * c + d) * N_HEADS_M + h] = _dot2(jnp.where(within, lf4[h:h + 1, :], 0.0), ones)

    for d in range(2):
        if has_state:
            c0_ref, n0_ref, m0_ref = refs[7:10]
            ncols = jnp.concatenate([n0_ref[d], jnp.zeros((12, DH_M), F32), zeros112], axis=0).T
        for h in range(N_HEADS_M):
            idx = 4 * d + h
            if has_state:
                st[idx][:, 0:DH_M] = c0_ref[d, h]
                st[idx][:, DH_M:] = _col_bcast(ncols, h)
                ms[idx][0:1, :] = m0_ref[idx:idx + 1, :]
            else:
                st[idx][...] = jnp.zeros((DH_M, 2 * DH_M), F32)
                ms[idx][0:1, :] = jnp.full((1, CHUNK), NEG_INIT, F32)

    def chunk_step(i, carry):
        for d in range(2):
            c = i if d == 0 else n_chunks - 1 - i
            r0 = pl.multiple_of(c * CHUNK, CHUNK)
            rows = rows_sc[2 * c + d]
            mask = (lan <= sub) if d == 0 else (lan >= sub)
            heads = []
            for h in range(N_HEADS_M):
                idx = 4 * d + h
                hs = slice(DH_M * h, DH_M * (h + 1))
                qc = q_ref[pl.ds(r0, CHUNK), hs]
                kt = kt_sc[c * N_HEADS_M + h]
                state = st[idx][...]
                heads.append(dict(idx=idx, hs=hs, qc=qc, kt=kt, state=state,
                                  s_raw=_dot(qc, kt), qs=_dot(qc, state.astype(BF16))))
            for h, hd in enumerate(heads):
                mp = ms[hd["idx"]][0:1, :]
                m_b = jnp.maximum(mp, cmb_sc[(2 * c + d) * N_HEADS_M + h])
                s = hd["s_raw"] * jnp.exp(jnp.where(mask, rows[h:h + 1, :] - m_b, neg_inf))
                s_hi = s.astype(BF16)
                vc = v_ref[pl.ds(r0, CHUNK), hd["hs"]]
                v_ones = jnp.concatenate([vc, ones], axis=1)
                hd.update(mp=mp, m_b=m_b, v_ones=v_ones, sv=_dot(s_hi, v_ones),
                          s_lo_sum=_dot((s - s_hi.astype(F32)).astype(BF16), ones))
            for h, hd in enumerate(heads):
                mp, m_b = hd["mp"], hd["m_b"]
                gw_b = jnp.exp(mp - m_b)
                en_b = jnp.exp(-(bb_sc[(2 * c + d) * N_HEADS_M + h] + m_b))
                num = hd["sv"][:, 0:DH_M] + gw_b * hd["qs"][:, 0:DH_M]
                den = hd["sv"][:, DH_M:] + hd["s_lo_sum"] + gw_b * hd["qs"][:, DH_M:]
                hdir[d][pl.ds(r0, CHUNK), hd["hs"]] = num / jnp.maximum(jnp.abs(den), en_b)
            for h, hd in enumerate(heads):
                mp = hd["mp"]
                gend = rows[8 + h:9 + h, :] + mp
                mnew = jnp.maximum(gend, rows[12 + h:13 + h, :])
                w_row = jnp.exp(rows[4 + h:5 + h, :] - mnew)
                kw_t = (hd["kt"].astype(F32) * w_row).astype(BF16)
                dec = jnp.exp(gend - mnew)
                st[hd["idx"]][...] = (jnp.concatenate([dec, dec], axis=1) * hd["state"]
                                      + _dot(kw_t, hd["v_ones"]))
                ms[hd["idx"]][0:1, :] = mnew
        return carry

    lax.fori_loop(0, n_chunks, chunk_step, 0)

    for h in range(N_HEADS_M):
        hs = slice(DH_M * h, DH_M * (h + 1))
        hh = hdir[0][:, hs] + hdir[1][:, hs]
        y = hh * lax.rsqrt(jnp.mean(hh * hh, axis=-1, keepdims=True) + EPS) * nw_ref[:, hs]
        hm_ref[:, hs] = (jax.nn.sigmoid(om_ref[:, hs]) * y).astype(BF16)
    if not has_state:
        c_ref, n_ref, m_ref = refs[n_in + 1:n_in + 4]
        for d in range(2):
            for h in range(N_HEADS_M):
                idx = 4 * d + h
                c_ref[d, h] = st[idx][:, 0:DH_M]
                n_ref[d, h:h + 1, :] = st[idx][:, DH_M:].T[0:1, :]
                m_ref[idx:idx + 1, :] = ms[idx][0:1, :]


def _mlstm(qm, km, vm, g3, bg, om, nw, state, *, n_batch, seq, first_block):
    n_chunks = seq // CHUNK
    rb = seq // TB
    tok = lambda: pl.BlockSpec((seq, 512), lambda b: (b + first_block // rb, 0))
    full = lambda a: pl.BlockSpec(a.shape, lambda b: (0,) * a.ndim)
    in_specs = [tok(), tok(), tok(),
                pl.BlockSpec((n_chunks, 16, CHUNK), lambda b: (b + first_block // rb, 0, 0)),
                full(bg), tok(), full(nw)]
    args = [qm, km, vm, g3, bg, om, nw]
    hm_spec = pl.BlockSpec((seq, 512), lambda b: (b, 0))
    hm_shape = jax.ShapeDtypeStruct((n_batch * seq, 512), BF16)
    scratch = ([pltpu.VMEM((DH_M, 2 * DH_M), F32)] * 8 + [pltpu.VMEM((8, CHUNK), F32)] * 8
               + [pltpu.VMEM((seq, 512), F32)] * 2
               + [pltpu.VMEM((2 * n_chunks, 32, CHUNK), F32),
                  pltpu.VMEM((8 * n_chunks, CHUNK, CHUNK), F32),
                  pltpu.VMEM((8 * n_chunks, CHUNK, CHUNK), F32),
                  pltpu.VMEM((N_HEADS_M * n_chunks, DH_M, CHUNK), BF16)])
    if state is not None:
        c0, n0, m0 = state
        in_specs += [pl.BlockSpec((None, None, 2, N_HEADS_M, DH_M, DH_M), lambda b: (b, 0, 0, 0, 0, 0)),
                     pl.BlockSpec((None, None, 2, N_HEADS_M, DH_M), lambda b: (b, 0, 0, 0, 0)),
                     pl.BlockSpec((None, 8, CHUNK), lambda b: (b, 0, 0))]
        args += [c0, n0, m0]
        out_specs, out_shape = hm_spec, hm_shape
    else:
        out_specs = [hm_spec,
                     pl.BlockSpec((None, 2, N_HEADS_M, DH_M, DH_M), lambda b: (b, 0, 0, 0, 0)),
                     pl.BlockSpec((None, 2, N_HEADS_M, DH_M), lambda b: (b, 0, 0, 0)),
                     pl.BlockSpec((None, 8, CHUNK), lambda b: (b, 0, 0))]
        out_shape = [hm_shape,
                     jax.ShapeDtypeStruct((n_batch, 2, N_HEADS_M, DH_M, DH_M), F32),
                     jax.ShapeDtypeStruct((n_batch, 2, N_HEADS_M, DH_M), F32),
                     jax.ShapeDtypeStruct((n_batch, 8, CHUNK), F32)]
    return pl.pallas_call(
        functools.partial(_mlstm_kernel, has_state=state is not None, n_chunks=n_chunks),
        grid=(n_batch,),
        in_specs=in_specs,
        out_specs=out_specs,
        out_shape=out_shape,
        scratch_shapes=scratch,
        compiler_params=_cparams(("arbitrary",)),
        name="mlstm_lat" if state is not None else "mlstm_ctx",
    )(*args)


def _outproj_kernel(attc_ref, attl_ref, hmc_ref, hml_ref, xc_ref, xl_ref, mod_ref, nw_ref, wo_ref,
                    rwh_ref, rwl_ref, wsg_ref, wsu_ref, wsd_ref, base_ref, hn_ref, lt_ref):
    y = (_dot(_ctx_or_lat(attc_ref, attl_ref), wo_ref[0:D_ATT, :])
         + _dot(_ctx_or_lat(hmc_ref, hml_ref), wo_ref[D_ATT:, :]))
    x1 = _ctx_or_lat(xc_ref, xl_ref) + mod_ref[2:3, :] * y
    z = x1 * lax.rsqrt(jnp.mean(x1 * x1, axis=-1, keepdims=True) + EPS) * nw_ref[...]
    hn = z * (1.0 + mod_ref[4:5, :]) + mod_ref[3:4, :]
    hb = hn.astype(BF16)
    hl = (hn - hb.astype(F32)).astype(BF16)
    nt = lambda w, t: lax.dot_general(w, t, _NT, preferred_element_type=F32)
    lt_ref[...] = nt(rwh_ref[...], hb) + nt(rwl_ref[...], hb) + nt(rwh_ref[...], hl)
    a = _silu(_dot(hb, wsg_ref[...])) * _dot(hb, wsu_ref[...])
    shared = _dot(a.astype(BF16), wsd_ref[...])
    base_ref[...] = x1 + mod_ref[5:6, :] * shared
    packed = pltpu.pack_elementwise([hn[:, :512], hn[:, 512:]], packed_dtype=BF16)
    for c in range(4):
        hn_ref[pl.ds(c, TBP, stride=4), :] = packed[:, 128 * c:128 * (c + 1)]


def _outproj(att, hm, x, mod3, norm2_w, w_out, rw_hi, rw_lo, wsg, wsu, wsd):
    tok = lambda w: pl.BlockSpec((TBP, w), lambda i: (i, 0))
    full = lambda a: pl.BlockSpec(a.shape, lambda i: (0,) * a.ndim)
    return pl.pallas_call(
        _outproj_kernel,
        grid=(T_ALL // TBP,),
        in_specs=_split_specs(512) + _split_specs(512) + _split_specs(D_MODEL) + [
                  pl.BlockSpec((None, 6, D_MODEL), lambda i: (_mod_row(i), 0, 0)),
                  full(norm2_w), full(w_out), full(rw_hi), full(rw_lo), full(wsg), full(wsu), full(wsd)],
        out_specs=[tok(D_MODEL), pl.BlockSpec((4 * TBP, 128), lambda i: (i, 0)),
                   pl.BlockSpec((N_EXPERTS, TBP), lambda i: (0, i))],
        out_shape=[jax.ShapeDtypeStruct((T_ALL, D_MODEL), F32),
                   jax.ShapeDtypeStruct((4 * T_ALL, 128), jnp.uint32),
                   jax.ShapeDtypeStruct((N_EXPERTS, T_ALL), F32)],
        compiler_params=_cparams(("arbitrary",)),
        name="outproj",
    )(*att, *hm, *x, mod3, norm2_w, w_out, rw_hi, rw_lo, wsg, wsu, wsd)


def _first_max(vals, ids, limit):
    m = functools.reduce(jnp.maximum, [jnp.max(v, axis=0, keepdims=True) for v in vals])
    cand = [jnp.min(jnp.where(v == m, i, limit), axis=0, keepdims=True) for v, i in zip(vals, ids)]
    return m, functools.reduce(jnp.minimum, cand)


def _router_kernel(lt_ref, bias_ref, gate_ref, idx_ref):
    n = lt_ref.shape[1]
    score = jax.nn.sigmoid(lt_ref[...])
    biased = score + bias_ref[...]
    sub = lax.broadcasted_iota(jnp.int32, (8, n), 0).astype(F32)
    neg_inf = F32(-jnp.inf)
    slabs = [biased[8 * g:8 * (g + 1), :] for g in range(N_GROUPS)]
    gs = []
    for sl in slabs:
        m1, i1 = _first_max([sl], [sub], 8.0)
        m2 = jnp.max(jnp.where(sub == i1, neg_inf, sl), axis=0, keepdims=True)
        gs.append(m1 + m2)
    cur = jnp.concatenate(gs, axis=0)
    gsel = jnp.zeros((8, n), F32)
    for _ in range(TOPK_GROUPS):
        _, i = _first_max([cur], [sub], 8.0)
        hit = sub == i
        gsel = jnp.where(hit, 1.0, gsel)
        cur = jnp.where(hit, neg_inf, cur)
    vals = [jnp.where(gsel[g:g + 1, :] > 0.0, slabs[g], neg_inf) for g in range(N_GROUPS)]
    ids = [sub + 8.0 * g for g in range(N_GROUPS)]
    picked = [jnp.zeros((8, n), F32) for _ in range(N_GROUPS)]
    order = []
    for _ in range(TOP_K):
        _, i = _first_max(vals, ids, float(N_EXPERTS))
        order.append(i)
        hits = [idg == i for idg in ids]
        picked = [jnp.where(hh, score[8 * g:8 * (g + 1), :], p) for g, (p, hh) in enumerate(zip(picked, hits))]
        vals = [jnp.where(hh, neg_inf, v) for v, hh in zip(vals, hits)]
    total = functools.reduce(jnp.add, [jnp.sum(p, axis=0, keepdims=True) for p in picked])
    gate_t = jnp.concatenate([p / total * ROUTED_SCALE for p in picked]
                             + [jnp.zeros((128 - N_EXPERTS, n), F32)], axis=0)
    gate_ref[...] = gate_t.T
    idx_ref[...] = jnp.concatenate(order, axis=0).astype(jnp.int32)


def _router(logits_t, bias_col):
    return pl.pallas_call(
        _router_kernel,
        grid=(T_ALL // TB_MOE,),
        in_specs=[pl.BlockSpec((N_EXPERTS, TB_MOE), lambda i: (0, i)),
                  pl.BlockSpec((N_EXPERTS, 1), lambda i: (0, 0))],
        out_specs=[pl.BlockSpec((TB_MOE, 128), lambda i: (i, 0)),
                   pl.BlockSpec((TOP_K, TB_MOE), lambda i: (0, i))],
        out_shape=[jax.ShapeDtypeStruct((T_ALL, 128), F32),
                   jax.ShapeDtypeStruct((TOP_K, T_ALL), jnp.int32)],
        compiler_params=_cparams(("arbitrary",)),
        name="router",
    )(logits_t, bias_col)


def _plan_kernel(idx_ref, slot_ref, cnt_ref, off_ref, pos_sc):
    n_tiles = T_HALF // 128
    eid = lax.broadcasted_iota(jnp.int32, (N_EXPERTS, 128), 0)
    tri = (lax.broadcasted_iota(jnp.int32, (128, 128), 0)
           <= lax.broadcasted_iota(jnp.int32, (128, 128), 1)).astype(BF16)
    carry = jnp.zeros((N_EXPERTS, 1), F32)
    for j in range(n_tiles):
        it = idx_ref[:, 128 * j:128 * (j + 1)]
        sel = jnp.zeros((N_EXPERTS, 128), F32)
        for k in range(TOP_K):
            sel = jnp.where(it[k:k + 1, :] == eid, 1.0, sel)
        inc = _dot(sel.astype(BF16), tri) + carry
        carry = inc[:, 127:128]
        pos_sc[:, 128 * j:128 * (j + 1)] = inc - 1.0
    count = jnp.broadcast_to(carry, (N_EXPERTS, 128))
    padded = jnp.floor((count + (MT - 1.0)) * (1.0 / MT)) * MT
    before = (lax.broadcasted_iota(jnp.int32, (N_EXPERTS, N_EXPERTS), 1)
              < lax.broadcasted_iota(jnp.int32, (N_EXPERTS, N_EXPERTS), 0)).astype(BF16)
    hi, mid, lo = _split3(padded)
    off = _dot(before, hi) + _dot(before, mid) + _dot(before, lo)
    cnt_ref[...] = count.astype(jnp.int32)
    off_ref[...] = off.astype(jnp.int32)

    def slots(j):
        it = idx_ref[:, 128 * j:128 * (j + 1)]
        val = off + pos_sc[:, 128 * j:128 * (j + 1)]
        rows = [jnp.sum(jnp.where(it[k:k + 1, :] == eid, val, 0.0), axis=0, keepdims=True)
                for k in range(TOP_K)]
        return jnp.concatenate(rows, axis=0).astype(jnp.int32)

    for j in range(n_tiles):
        s = slots(j)
        slot_ref[:, 128 * j:128 * (j + 1)] = s[0:TOP_K // 2, :] | (s[TOP_K // 2:, :] << 16)


def _plan(idx8):
    return pl.pallas_call(
        _plan_kernel,
        grid=(2,),
        in_specs=[pl.BlockSpec((TOP_K, T_HALF), lambda h: (0, h))],
        out_specs=[pl.BlockSpec((None, TOP_K // 2, T_HALF), lambda h: (h, 0, 0)),
                   pl.BlockSpec((None, N_EXPERTS, 128), lambda h: (h, 0, 0)),
                   pl.BlockSpec((None, N_EXPERTS, 128), lambda h: (h, 0, 0))],
        out_shape=[jax.ShapeDtypeStruct((2, TOP_K // 2, T_HALF), jnp.int32),
                   jax.ShapeDtypeStruct((2, N_EXPERTS, 128), jnp.int32),
                   jax.ShapeDtypeStruct((2, N_EXPERTS, 128), jnp.int32)],
        scratch_shapes=[pltpu.VMEM((N_EXPERTS, T_HALF), F32)],
        compiler_params=_cparams(("arbitrary",)),
        name="plan",
    )(idx8)


def _prepare_half(slot_ref, cnt_ref, off_ref, tok_ref, tile_e, tile_first, elist, h):
    def per_expert(x, carry):
        j, q = carry
        n = cnt_ref[h * N_EXPERTS + x]
        first = off_ref[h * N_EXPERTS + x]
        tiles = (n + MT - 1) // MT
        elist[q] = x

        def mark(i, c):
            tile_e[j + i] = x
            tile_first[j + i] = jnp.where(i == 0, 1, 0)
            return c

        lax.fori_loop(0, tiles, mark, 0)

        @pl.when(tiles > 0)
        def _():
            last = first + (tiles - 1) * MT
            for u in range(MT):
                tok_ref[last + u] = T_HALF

        return j + tiles, q + jnp.where(tiles > 0, 1, 0)

    n_tiles, n_live = lax.fori_loop(0, N_EXPERTS, per_expert, (0, 0))
    for extra in range(2):
        tile_e[n_tiles + extra] = 0
        tile_first[n_tiles + extra] = 0
        for u in range(MT):
            tok_ref[(n_tiles + extra) * MT + u] = T_HALF
    return n_tiles, n_live


def _invert_slots(slot_ref, tok_ref, acc_v, h):
    rounds = TOP_K // 2
    trips = T_HALF // INV_UNROLL
    zero_rows = 8 * T_HALF // (rounds * trips)
    assert zero_rows * rounds * trips == 8 * T_HALF and zero_rows % 8 == 0
    for k in range(rounds):
        def body(i, carry, k=k):
            words = [slot_ref[(h * rounds + k) * T_HALF + i * INV_UNROLL + u] for u in range(INV_UNROLL)]
            for u in range(INV_UNROLL):
                t = i * INV_UNROLL + u
                tok_ref[words[u] & 0xFFFF] = t
                tok_ref[lax.shift_right_logical(words[u], 16)] = t
            row = pl.multiple_of((k * trips + i) * zero_rows, 8)
            acc_v[pl.ds(row, zero_rows), :] = jnp.zeros((zero_rows, 128), F32)
            return carry

        lax.fori_loop(0, trips, body, 0)
    acc_v[pl.ds(8 * T_HALF, 8), :] = jnp.zeros((8, 128), F32)


def _moe_kernel(slot_ref, cnt_ref, off_ref, src_hbm, gate_hbm, wg_hbm, wu_hbm, wd_hbm, base_hbm, mod_ref,
                outc_hbm, outl_hbm,
                src_v, gate_v, acc_v, wg_l, wu_l, wd_l, wgu_b, wd_b, xbuf0, xbuf1, gbuf0, gbuf1,
                ybuf0, ybuf1, tok_ref, tile_e, tile_first, elist, live, base_buf, out_buf, sem, wsem, fsem):
    h = pl.program_id(0)
    xbuf, gbuf, ybuf = (xbuf0, xbuf1), (gbuf0, gbuf1), (ybuf0, ybuf1)

    src_cp = pltpu.make_async_copy(src_hbm.at[pl.ds(pl.multiple_of(h * (4 * T_HALF), 8), 4 * T_HALF)],
                                   src_v.at[pl.ds(0, 4 * T_HALF)], sem.at[0])
    gate_cp = pltpu.make_async_copy(gate_hbm.at[pl.ds(pl.multiple_of(h * T_HALF, 8), T_HALF)],
                                    gate_v.at[pl.ds(0, T_HALF)], sem.at[1])
    src_cp.start()
    gate_cp.start()

    def weight_copies(x, slot):
        return [pltpu.make_async_copy(w_hbm.at[x], w_l.at[slot], wsem.at[slot, i])
                for i, (w_hbm, w_l) in enumerate(((wg_hbm, wg_l), (wu_hbm, wu_l), (wd_hbm, wd_l)))]

    n_tiles, n_live = _prepare_half(slot_ref, cnt_ref, off_ref, tok_ref, tile_e, tile_first, elist, h)
    live[0] = 0
    live[1] = n_live

    for ahead in range(W_SLOTS - 1):
        @pl.when(ahead < n_live)
        def _(ahead=ahead):
            for cp in weight_copies(elist[ahead], ahead):
                cp.start()

    zero = jnp.zeros((8, 128), F32)
    src_v[pl.ds(4 * T_HALF, 8), :] = pltpu.pack_elementwise([zero, zero], packed_dtype=BF16)
    gate_v[pl.ds(T_HALF, 8), :] = zero
    ybuf0[...] = jnp.zeros(ybuf0.shape, F32)
    ybuf1[...] = jnp.zeros(ybuf1.shape, F32)
    _invert_slots(slot_ref, tok_ref, acc_v, h)
    src_cp.wait()
    gate_cp.wait()

    def switch_expert():
        q = live[0]
        slot = q % W_SLOTS
        for cp in weight_copies(elist[q], slot):
            cp.wait()
        wgu_b[:, 0:D_EXPERT] = wg_l[slot].astype(BF16)
        wgu_b[:, D_EXPERT:] = wu_l[slot].astype(BF16)
        wd_b[...] = wd_l[slot].astype(BF16)
        live[0] = q + 1
        nxt = q + W_SLOTS - 1

        @pl.when(nxt < live[1])
        def _():
            for cp in weight_copies(elist[nxt], nxt % W_SLOTS):
                cp.start()

    def gather(j, xb, gb, rows=(0, MT)):
        base = j * MT
        for m in range(*rows):
            t = tok_ref[base + m]
            xb[pl.ds(m, 4, stride=MT_STRIDE), :] = src_v[pl.ds(pl.multiple_of(t * 4, 4), 4), :]
            gb[m:m + 1, :] = gate_v[pl.ds(t, 1), :]

    def scatter(j, yb, rows=(0, MT)):
        base = j * MT
        for b in range(rows[0] // RMW_BATCH, rows[1] // RMW_BATCH):
            ms = [b * RMW_BATCH + u for u in range(RMW_BATCH)]
            targets = [pl.ds(pl.multiple_of(tok_ref[base + m] * 8, 8), 8) for m in ms]
            vals = [acc_v[r, :] + yb[pl.ds(m, 8, stride=MT_STRIDE), :] for r, m in zip(targets, ms)]
            for r, v in zip(targets, vals):
                acc_v[r, :] = v

    quarters = [(q * MT // 4, (q + 1) * MT // 4) for q in range(4)]

    def step(j, p):
        pl.when(tile_first[j] == 1)(switch_expert)
        xb, gb, yb = xbuf[p], gbuf[p], ybuf[p]
        nxt = (j + 1, xbuf[1 - p], gbuf[1 - p])
        prv = (jnp.maximum(j - 1, 0), ybuf[1 - p])
        lo, hi = [], []
        for c in range(4):
            words = xb[MT_STRIDE * c:MT_STRIDE * c + MT, :]
            unpack = functools.partial(pltpu.unpack_elementwise, words, packed_dtype=BF16, unpacked_dtype=F32)
            lo.append(unpack(index=0).astype(BF16))
            hi.append(unpack(index=1).astype(BF16))
        x = jnp.concatenate(lo + hi, axis=1)
        gather(*nxt, rows=quarters[0])
        h_gate = _dot(x, wgu_b[:, 0:D_EXPERT])
        gather(*nxt, rows=quarters[1])
        h_up = _dot(x, wgu_b[:, D_EXPERT:])
        gather(*nxt, rows=quarters[2])
        g = gb[...]
        g_hi = g.astype(BF16)
        g_lo = (g - g_hi.astype(F32)).astype(BF16)
        pick = (lax.broadcasted_iota(jnp.int32, (128, D_EXPERT), 0) == tile_e[j]).astype(BF16)
        g_col = _dot(g_hi, pick) + _dot(g_lo, pick)
        a = (_silu(h_gate) * h_up * g_col).astype(BF16)
        gather(*nxt, rows=quarters[3])
        scatter(*prv, rows=quarters[0])
        y_lo = _dot(a, wd_b[:, 0:D_MODEL // 2])
        scatter(*prv, rows=quarters[1])
        y_hi = _dot(a, wd_b[:, D_MODEL // 2:])
        scatter(*prv, rows=quarters[2])
        for c in range(4):
            yb[MT_STRIDE * c:MT_STRIDE * c + MT, :] = y_lo[:, 128 * c:128 * (c + 1)]
            yb[MT_STRIDE * (c + 4):MT_STRIDE * (c + 4) + MT, :] = y_hi[:, 128 * c:128 * (c + 1)]
        scatter(*prv, rows=quarters[3])

    gather(0, xbuf[0], gbuf[0])
    n_pairs = (n_tiles + 1) // 2

    def pair(i, carry):
        step(2 * i, 0)
        step(2 * i + 1, 1)
        return carry

    lax.fori_loop(0, n_pairs, pair, 0)
    scatter(jnp.maximum(2 * n_pairs - 1, 0), ybuf[1])

    n_fin = T_HALF // FIN

    def first_token(c):
        return h * T_HALF + c * FIN

    def base_copy(c, slot):
        return pltpu.make_async_copy(base_hbm.at[pl.ds(pl.multiple_of(first_token(c), FIN), FIN)],
                                     base_buf.at[slot], fsem.at[0, slot])

    def on_out_copy(c, action):
        t0 = first_token(c)
        slot = c % FIN_OUT_SLOTS

        @pl.when(t0 < T_CTX)
        def _():
            action(pltpu.make_async_copy(out_buf.at[slot], outc_hbm.at[pl.ds(pl.multiple_of(t0, FIN), FIN)],
                                         fsem.at[1, slot]))

        @pl.when(t0 >= T_CTX)
        def _():
            action(pltpu.make_async_copy(out_buf.at[slot],
                                         outl_hbm.at[pl.ds(pl.multiple_of(t0 - T_CTX, FIN), FIN)],
                                         fsem.at[1, slot]))

    for c in range(FIN_BASE_SLOTS - 1):
        base_copy(c, c).start()

    def combine(c, carry):
        slot = c % FIN_BASE_SLOTS
        oslot = c % FIN_OUT_SLOTS
        base_copy(c, slot).wait()
        ahead = c + FIN_BASE_SLOTS - 1

        @pl.when(ahead < n_fin)
        def _():
            base_copy(ahead, ahead % FIN_BASE_SLOTS).start()

        @pl.when(c >= FIN_OUT_SLOTS)
        def _():
            on_out_copy(c - FIN_OUT_SLOTS, lambda cp: cp.wait())

        t0 = first_token(c)
        mod_row = jnp.where(t0 < T_CTX, 0, 1 + (t0 - T_CTX) // DEC_SEQ)
        acc_row = pl.multiple_of(8 * FIN * c, 8)
        for cc in range(8):
            cs = slice(128 * cc, 128 * (cc + 1))
            routed = acc_v[pl.ds(acc_row + cc, FIN, stride=8), :]
            out_buf[oslot, :, cs] = base_buf[slot, :, cs] + mod_ref[mod_row, 5:6, cs] * routed
        on_out_copy(c, lambda cp: cp.start())
        return carry

    lax.fori_loop(0, n_fin, combine, 0)
    for c in range(n_fin - FIN_OUT_SLOTS, n_fin):
        on_out_copy(c, lambda cp: cp.wait())


def _moe(slot_words, cnt, off, src, gate, wg, wu, wd, base, mod3):
    any_spec = pl.BlockSpec(memory_space=pl.ANY)
    tile_buf = lambda rows, dt: pltpu.VMEM((rows * MT_STRIDE, 128), dt)
    return pl.pallas_call(
        _moe_kernel,
        grid_spec=pltpu.PrefetchScalarGridSpec(
            num_scalar_prefetch=3,
            grid=(2,),
            in_specs=[any_spec] * 6 + [pl.BlockSpec(mod3.shape, lambda h, *_: (0, 0, 0))],
            out_specs=[any_spec, any_spec],
            scratch_shapes=[pltpu.VMEM((4 * T_HALF + 8, 128), jnp.uint32),
                            pltpu.VMEM((T_HALF + 8, 128), F32),
                            pltpu.VMEM((8 * T_HALF + 8, 128), F32),
                            pltpu.VMEM((W_SLOTS, D_MODEL, D_EXPERT), F32),
                            pltpu.VMEM((W_SLOTS, D_MODEL, D_EXPERT), F32),
                            pltpu.VMEM((W_SLOTS, D_EXPERT, D_MODEL), F32),
                            pltpu.VMEM((D_MODEL, 2 * D_EXPERT), BF16),
                            pltpu.VMEM((D_EXPERT, D_MODEL), BF16),
                            tile_buf(4, jnp.uint32), tile_buf(4, jnp.uint32),
                            pltpu.VMEM((MT, 128), F32), pltpu.VMEM((MT, 128), F32),
                            tile_buf(8, F32), tile_buf(8, F32),
                            pltpu.SMEM((SLOT_CAP,), jnp.int32),
                            pltpu.SMEM((NT_MAX + 2,), jnp.int32),
                            pltpu.SMEM((NT_MAX + 2,), jnp.int32),
                            pltpu.SMEM((N_EXPERTS,), jnp.int32),
                            pltpu.SMEM((2,), jnp.int32),
                            pltpu.VMEM((FIN_BASE_SLOTS, FIN, D_MODEL), F32),
                            pltpu.VMEM((FIN_OUT_SLOTS, FIN, D_MODEL), F32),
                            pltpu.SemaphoreType.DMA((2,)),
                            pltpu.SemaphoreType.DMA((W_SLOTS, 3)),
                            pltpu.SemaphoreType.DMA((2, FIN_BASE_SLOTS))]),
        out_shape=[jax.ShapeDtypeStruct((T_CTX, D_MODEL), F32), jax.ShapeDtypeStruct((T_LAT, D_MODEL), F32)],
        compiler_params=_cparams(("arbitrary",)),
        name="moe",
    )(slot_words, cnt, off, src, gate, wg, wu, wd, base, mod3)


def _head_indicators(width):
    head = jnp.arange(width) // DH_ATT
    ind = (head[:, None] == jnp.arange(128)[None, :]).astype(BF16)
    return ind, ind.T


def kernel(x_prompt, x_sample, cache_attn_k, cache_attn_v, state_mlstm_c, state_mlstm_n, state_mlstm_m, c, c_ctx, w_mod, b_mod, norm1_w, norm2_w, w_in, q_norm_w, k_norm_w, b_gates, m_norm_w, w_out, router_w, router_bias, w_gate, w_up, w_down, ws_gate, ws_up, ws_down):
    x = (x_prompt.reshape(T_CTX, D_MODEL), x_sample.reshape(T_LAT, D_MODEL))
    mod3 = _modulation(c, c_ctx, w_mod[0], b_mod[0])

    w_main = w_in[0]
    w_gates_t = w_in[0, :, P_MAIN:].T.astype(BF16)
    qw = jnp.tile(q_norm_w, (1, N_HEADS_ATT))
    kw = jnp.tile(k_norm_w, (1, N_KV_HEADS))
    inds = _head_indicators(512) + _head_indicators(128)
    qn, kn, va, qm, km, vm, om, gt, new_k, new_v = _inproj(*x, mod3, norm1_w, w_main, w_gates_t, qw, kw, inds,
                                                           _rope_tables())

    att_c = _attention(qn, kn, va, None, n_batch=BATCH, seq=SEQ, first_row=0)
    att_l = _attention(qn, kn, va, (cache_attn_k, cache_attn_v), n_batch=DEC_BATCH, seq=DEC_SEQ,
                       first_row=T_CTX)

    g3 = gt.reshape(16, T_ALL // CHUNK, CHUNK).transpose(1, 0, 2)
    bg = b_gates.reshape(16, 1)
    hm_c, c_new, n_new, m_new = _mlstm(qm, km, vm, g3, bg, om, m_norm_w, None,
                                       n_batch=BATCH, seq=SEQ, first_block=0)
    m0 = jnp.broadcast_to(state_mlstm_m.reshape(DEC_BATCH, 8, 1), (DEC_BATCH, 8, CHUNK))
    hm_l = _mlstm(qm, km, vm, g3, bg, om, m_norm_w, (state_mlstm_c, state_mlstm_n, m0),
                  n_batch=DEC_BATCH, seq=DEC_SEQ, first_block=N_CTX_BLOCKS)

    rw_t = router_w[0].T
    rw_hi = rw_t.astype(BF16)
    rw_lo = (rw_t - rw_hi.astype(F32)).astype(BF16)
    base, hn2, logits_t = _outproj((att_c, att_l), (hm_c, hm_l), x, mod3, norm2_w, w_out[0].astype(BF16),
                                   rw_hi, rw_lo, ws_gate[0].astype(BF16), ws_up[0].astype(BF16),
                                   ws_down[0].astype(BF16))
    gate, idx8 = _router(logits_t, router_bias.reshape(N_EXPERTS, 1))
    slot_words, cnt, off = _plan(idx8)
    out_c, out_l = _moe(slot_words.reshape(TOP_K * T_HALF), cnt[:, :, 0].reshape(2 * N_EXPERTS),
                        off[:, :, 0].reshape(2 * N_EXPERTS), hn2, gate, w_gate[0], w_up[0], w_down[0],
                        base, mod3)

    y_prompt = out_c.reshape(BATCH, SEQ, D_MODEL)
    y_sample = out_l.reshape(DEC_BATCH, DEC_SEQ, D_MODEL)
    new_m = m_new[:, :, 0].reshape(BATCH, 1, 2, N_HEADS_M)
    return (y_prompt, y_sample, new_k[:, None], new_v[:, None], c_new[:, None], n_new[:, None], new_m)
```

```python
import functools

import jax
import jax.numpy as jnp
from jax import lax
from jax.experimental import pallas as pl
from jax.experimental.pallas import tpu as pltpu

F32 = jnp.float32
BF16 = jnp.bfloat16

D_MODEL = 1024
BATCH = 32
SEQ = 256
DEC_BATCH = 2
DEC_SEQ = 1024
PAST_LEN = 256
GRID_W = 64
N_HEADS_ATT = 8
N_KV_HEADS = 2
DH_ATT = 64
D_ATT = 512
ROPE_THETA = 10000.0
N_HEADS_M = 4
DH_M = 128
D_M = 512
CHUNK = 128
N_EXPERTS = 64
TOP_K = 8
N_GROUPS = 8
TOPK_GROUPS = 4
D_EXPERT = 256
ROUTED_SCALE = 2.5
EPS = 1e-6
NEG_INIT = -1e30

T_CTX = BATCH * SEQ
T_LAT = DEC_BATCH * DEC_SEQ
T_ALL = T_CTX + T_LAT
TB = 256
N_CTX_BLOCKS = T_CTX // TB
TBP = 512
NP_CTX = T_CTX // TBP
NP_LAT_PER_BATCH = DEC_SEQ // TBP
TB_MOE = 1024
T_HALF = T_ALL // 2
MT = 256
MT_STRIDE = MT + 8
NT_MAX = TOP_K * T_HALF // MT + N_EXPERTS
SLOT_CAP = -(-(NT_MAX + 2) * MT // 1024) * 1024
RMW_BATCH = 16
INV_UNROLL = 16
W_SLOTS = 2
FIN = 128
FIN_BASE_SLOTS = 6
FIN_OUT_SLOTS = 4
assert FIN_OUT_SLOTS <= FIN_BASE_SLOTS
assert SLOT_CAP < 2 ** 16 and T_HALF % INV_UNROLL == 0 and MT % RMW_BATCH == 0
P_MAIN = 2816
VMEM_LIMIT = 56 * 1024 * 1024

_NT = (((1,), (1,)), ((), ()))
_TN = (((0,), (0,)), ((), ()))


def _cparams(sem):
    return pltpu.CompilerParams(dimension_semantics=sem, vmem_limit_bytes=VMEM_LIMIT)


def _split3(x):
    hi = x.astype(BF16)
    r1 = x - hi.astype(F32)
    mid = r1.astype(BF16)
    lo = (r1 - mid.astype(F32)).astype(BF16)
    return hi, mid, lo


def _dot(a, b):
    return jnp.dot(a, b, preferred_element_type=F32)


def _dot3(x, m_bf16):
    hi, mid, lo = _split3(x)
    return _dot(hi, m_bf16) + _dot(mid, m_bf16) + _dot(lo, m_bf16)


def _dot2(x, m_bf16):
    hi = x.astype(BF16)
    lo = (x - hi.astype(F32)).astype(BF16)
    return _dot(hi, m_bf16) + _dot(lo, m_bf16)


def _silu(x):
    return x * jax.nn.sigmoid(x)


def _mod_row(i):
    return jnp.where(i < NP_CTX, 0, 1 + (i - NP_CTX) // NP_LAT_PER_BATCH)


def _mod_kernel(ct_ref, w_ref, b_ref, o_ref):
    s = _silu(ct_ref[...])
    w = w_ref[...]
    rows = [jnp.sum(w * s[:, r:r + 1], axis=0, keepdims=True) for r in range(3)]
    rows.append(jnp.zeros((5, w.shape[1]), F32))
    o_ref[...] = jnp.concatenate(rows, axis=0) + b_ref[...]


def _modulation(c, c_ctx, w_mod, b_mod):
    cvec = jnp.concatenate([c_ctx[None, :], c, jnp.zeros((5, D_MODEL), F32)], axis=0)
    nb = 1024
    out = pl.pallas_call(
        _mod_kernel,
        grid=(6 * D_MODEL // nb,),
        in_specs=[pl.BlockSpec((D_MODEL, 8), lambda j: (0, 0)),
                  pl.BlockSpec((D_MODEL, nb), lambda j: (0, j)),
                  pl.BlockSpec((1, nb), lambda j: (0, j))],
        out_specs=pl.BlockSpec((8, nb), lambda j: (0, j)),
        out_shape=jax.ShapeDtypeStruct((8, 6 * D_MODEL), F32),
        compiler_params=_cparams(("arbitrary",)),
        name="modulation",
    )(cvec.T, w_mod, b_mod[None, :])
    return out.reshape(8, 6, D_MODEL)


def _ctx_or_lat(ctx_ref, lat_ref):
    return jnp.where(pl.program_id(0) < NP_CTX, ctx_ref[...], lat_ref[...])


def _split_specs(width):
    return [pl.BlockSpec((TBP, width), lambda i: (jnp.minimum(i, NP_CTX - 1), 0)),
            pl.BlockSpec((TBP, width), lambda i: (jnp.maximum(i - NP_CTX, 0), 0))]


def _head_norm(x, ind, ind_t, w_row):
    ss = _dot2(x * x, ind)
    inv = lax.rsqrt(ss * (1.0 / DH_ATT) + EPS)
    return x * _dot2(inv, ind_t) * w_row


def _rope(x, cos, sin_signed):
    lane = lax.broadcasted_iota(jnp.int32, x.shape, 1)
    partner = jnp.where((lane % 32) < 16, pltpu.roll(x, 128 - 16, 1), pltpu.roll(x, 16, 1))
    return x * cos + partner * sin_signed


def _inproj_kernel(xc_ref, xl_ref, mod_ref, nw_ref, w_ref, wgt_ref, qw_ref, kw_ref, iq_ref, iqt_ref,
                   ik_ref, ikt_ref, cos_ref, sin_ref,
                   qn_ref, kn_ref, va_ref, qm_ref, km_ref, vm_ref, om_ref, gt_ref, kc_ref, vc_ref, w_b):
    @pl.when(pl.program_id(0) == 0)
    def _():
        for j in range(P_MAIN // 128):
            w_b[:, 128 * j:128 * (j + 1)] = w_ref[:, 128 * j:128 * (j + 1)].astype(BF16)

    x = _ctx_or_lat(xc_ref, xl_ref)
    y = x * lax.rsqrt(jnp.mean(x * x, axis=-1, keepdims=True) + EPS) * nw_ref[...]
    hn = y * (1.0 + mod_ref[1:2, :]) + mod_ref[0:1, :]
    hb = hn.astype(BF16)
    qn = _head_norm(_dot(hb, w_b[:,0:512]), iq_ref[...], iqt_ref[...], qw_ref[...])
    kn = _head_norm(_dot(hb, w_b[:,512:640]), ik_ref[...], ikt_ref[...], kw_ref[...])
    qn_ref[...] = qn.astype(BF16)
    kn_ref[...] = kn

    @pl.when(pl.program_id(0) >= NP_CTX)
    def _():
        cos, sin = cos_ref[...], sin_ref[...]
        for j in range(4):
            qn_ref[:, 128 * j:128 * (j + 1)] = _rope(qn[:, 128 * j:128 * (j + 1)], cos, sin).astype(BF16)
        kn_ref[...] = _rope(kn, cos, sin)

    va = _dot(hb, w_b[:,640:768])
    va_ref[...] = va

    @pl.when(pl.program_id(0) < NP_CTX)
    def _():
        for bb in range(TBP // SEQ):
            for g in range(N_KV_HEADS):
                rows, lanes = slice(SEQ * bb, SEQ * (bb + 1)), slice(DH_ATT * g, DH_ATT * (g + 1))
                kc_ref[bb, g] = kn[rows, lanes]
                vc_ref[bb, g] = va[rows, lanes]

    qm_ref[...] = _dot(hb, w_b[:,768:1280]).astype(BF16)
    km_ref[...] = (_dot(hb, w_b[:,1280:1792]) * (DH_M ** -0.5)).astype(BF16)
    vm_ref[...] = _dot(hb, w_b[:,1792:2304]).astype(BF16)
    om_ref[...] = _dot(hb, w_b[:,2304:2816])
    gt_ref[...] = lax.dot_general(wgt_ref[...], hb, _NT, preferred_element_type=F32)


def _inproj(x_ctx, x_lat, mod3, norm1_w, w_main, w_gates_t, qw, kw, inds, rope_tabs):
    tok = lambda w: pl.BlockSpec((TBP, w), lambda i: (i, 0))
    full = lambda a: pl.BlockSpec(a.shape, lambda i: (0,) * a.ndim)
    sd = lambda w, dt: jax.ShapeDtypeStruct((T_ALL, w), dt)
    rope_spec = pl.BlockSpec((TBP, 128), lambda i: (jnp.maximum(i - NP_CTX, 0) % NP_LAT_PER_BATCH, 0))
    cache_spec = pl.BlockSpec((TBP // SEQ, N_KV_HEADS, SEQ, DH_ATT),
                              lambda i: (jnp.minimum(i, NP_CTX - 1), 0, 0, 0))
    cache_shape = jax.ShapeDtypeStruct((BATCH, N_KV_HEADS, SEQ, DH_ATT), F32)
    return pl.pallas_call(
        _inproj_kernel,
        grid=(T_ALL // TBP,),
        in_specs=_split_specs(D_MODEL) + [
                  pl.BlockSpec((None, 6, D_MODEL), lambda i: (_mod_row(i), 0, 0)),
                  full(norm1_w), full(w_main), full(w_gates_t), full(qw), full(kw)]
                 + [full(a) for a in inds] + [rope_spec, rope_spec],
        out_specs=[tok(512), tok(128), tok(128), tok(512), tok(512), tok(512), tok(512),
                   pl.BlockSpec((16, TBP), lambda i: (0, i)), cache_spec, cache_spec],
        out_shape=[sd(512, BF16), sd(128, F32), sd(128, F32), sd(512, BF16), sd(512, BF16),
                   sd(512, BF16), sd(512, F32), jax.ShapeDtypeStruct((16, T_ALL), F32),
                   cache_shape, cache_shape],
        scratch_shapes=[pltpu.VMEM((D_MODEL, P_MAIN), BF16)],
        compiler_params=_cparams(("arbitrary",)),
        name="inproj",
    )(x_ctx, x_lat, mod3, norm1_w, w_main, w_gates_t, qw, kw, *inds, *rope_tabs)


def _rope_tables():
    t = jnp.arange(DEC_SEQ)
    pos = jnp.stack([t // GRID_W, t % GRID_W], axis=1).astype(F32)
    n_freq = DH_ATT // 4
    inv_freq = ROPE_THETA ** (-jnp.arange(n_freq, dtype=F32) / n_freq)
    ang = pos[:, :, None] * inv_freq
    cos, sin = jnp.cos(ang), jnp.sin(ang)
    cos_h = jnp.stack([cos, cos], axis=2).reshape(DEC_SEQ, DH_ATT)
    sin_h = jnp.stack([-sin, sin], axis=2).reshape(DEC_SEQ, DH_ATT)
    return jnp.tile(cos_h, (1, 2)), jnp.tile(sin_h, (1, 2))


def _attn_kernel(*refs, has_cache):
    if has_cache:
        q_ref, k_ref, v_ref, kc_ref, vc_ref, o_ref = refs
    else:
        q_ref, k_ref, v_ref, o_ref = refs
    q = q_ref[...] * jnp.asarray(DH_ATT ** -0.5, BF16)
    k = k_ref[...].astype(BF16)
    v = v_ref[...].astype(BF16)
    qb = q.shape[0]
    low_half = lax.broadcasted_iota(jnp.int32, (1, 128), 1) < DH_ATT
    for g in range(N_KV_HEADS):
        kg = k[:, DH_ATT * g:DH_ATT * (g + 1)]
        vg = v[:, DH_ATT * g:DH_ATT * (g + 1)]
        if has_cache:
            kg = jnp.concatenate([kg, kc_ref[g].astype(BF16)], axis=0)
            vg = jnp.concatenate([vg, vc_ref[g].astype(BF16)], axis=0)
        n_keys = kg.shape[0]
        zero = jnp.zeros_like(kg)
        k2 = jnp.concatenate([jnp.concatenate([kg, zero], axis=1), jnp.concatenate([zero, kg], axis=1)], axis=0)
        v2 = jnp.concatenate([jnp.concatenate([vg, zero], axis=1), jnp.concatenate([zero, vg], axis=1)], axis=0)
        ones2 = jnp.concatenate([jnp.broadcast_to(jnp.where(low_half, 1.0, 0.0), (n_keys, 128)),
                                 jnp.broadcast_to(jnp.where(low_half, 0.0, 1.0), (n_keys, 128))],
                                axis=0).astype(BF16)
        qp = jnp.concatenate([q[:, 256 * g:256 * g + 128], q[:, 256 * g + 128:256 * g + 256]], axis=0)
        s = lax.dot_general(qp, k2, _NT, preferred_element_type=F32)
        halves = [s[:, 0:n_keys], s[:, n_keys:]]
        p = jnp.concatenate([jnp.exp(sh - jnp.max(sh, axis=-1, keepdims=True)) for sh in halves],
                            axis=1).astype(BF16)
        ov = _dot(p, jnp.concatenate([v2, ones2], axis=1))
        o = ov[:, 0:128] / ov[:, 128:]
        o_ref[:, 256 * g:256 * g + 128] = o[0:qb, :].astype(BF16)
        o_ref[:, 256 * g + 128:256 * g + 256] = o[qb:, :].astype(BF16)


def _attention(qn, kn, v_all, cache, *, n_batch, seq, first_row):
    qblocks = seq // TB
    kv_spec = pl.BlockSpec((seq, 128), lambda b, i: (b + first_row // seq, 0))
    in_specs = [pl.BlockSpec((TB, 512), lambda b, i: (first_row // TB + b * qblocks + i, 0)), kv_spec, kv_spec]
    args = [qn, kn, v_all]
    if cache is not None:
        cspec = pl.BlockSpec((None, None, N_KV_HEADS, PAST_LEN, DH_ATT), lambda b, i: (b, 0, 0, 0, 0))
        in_specs += [cspec, cspec]
        args += list(cache)
    return pl.pallas_call(
        functools.partial(_attn_kernel, has_cache=cache is not None),
        grid=(n_batch, qblocks),
        in_specs=in_specs,
        out_specs=pl.BlockSpec((TB, 512), lambda b, i: (b * qblocks + i, 0)),
        out_shape=jax.ShapeDtypeStruct((n_batch * seq, 512), BF16),
        compiler_params=_cparams(("arbitrary", "arbitrary")),
        name="attention_lat" if cache is not None else "attention_ctx",
    )(*args)


def _log_sigmoid(x):
    return jnp.minimum(x, 0.0) - jnp.log1p(jnp.exp(-jnp.abs(x)))


def _col_bcast(cols, j):
    return jnp.broadcast_to(cols[:, j:j + 1], (CHUNK, CHUNK))


def _mlstm_kernel(*refs, has_state, n_chunks):
    n_in = 10 if has_state else 7
    n_out = 1 if has_state else 4
    q_ref, k_ref, v_ref, g_ref, bg_ref, om_ref, nw_ref = refs[:7]
    hm_ref = refs[n_in]
    scratch = refs[n_in + n_out:]
    st = scratch[0:8]
    ms = scratch[8:16]
    hdir = scratch[16:18]
    rows_sc, cmb_sc, bb_sc, kt_sc = scratch[18:22]
    neg_inf = F32(-jnp.inf)
    zeros112 = jnp.zeros((CHUNK - 16, CHUNK), F32)
    sub = lax.broadcasted_iota(jnp.int32, (CHUNK, CHUNK), 0)
    lan = lax.broadcasted_iota(jnp.int32, (CHUNK, CHUNK), 1)
    ones = jnp.ones((CHUNK, CHUNK), BF16)

    stats = []
    for c in range(n_chunks):
        pre = g_ref[c] + bg_ref[...]
        logf = _log_sigmoid(pre)
        for d in range(2):
            bcum = _dot3(logf, ((sub <= lan) if d == 0 else (sub >= lan)).astype(BF16))
            li4 = pre[4 * d:4 * d + 4, :]
            lf4 = logf[8 + 4 * d:12 + 4 * d, :]
            b4 = bcum[8 + 4 * d:12 + 4 * d, :]
            stats.append((c, d, li4 - b4, lf4, b4))
    for c, d, r4, lf4, b4 in stats:
        blast4 = b4[:, CHUNK - 1:CHUNK] if d == 0 else b4[:, 0:1]
        wlog4 = blast4 + r4
        wmax4 = jnp.max(wlog4, axis=-1, keepdims=True)
        full = lambda a: jnp.broadcast_to(a, (4, CHUNK))
        rows_sc[2 * c + d] = jnp.concatenate(
            [r4, wlog4, full(blast4), full(wmax4), jnp.zeros((16, CHUNK), F32)], axis=0)
    for c in range(n_chunks):
        for h in range(N_HEADS_M):
            kc = k_ref[CHUNK * c:CHUNK * (c + 1), DH_M * h:DH_M * (h + 1)]
            kt_sc[c * N_HEADS_M + h] = kc.astype(F32).T.astype(BF16)
    for c, d, r4, lf4, b4 in stats:
        within = (lan <= sub) if d == 0 else (lan >= sub)
        for h in range(N_HEADS_M):
            cummax = jnp.max(jnp.where(within, r4[h:h + 1, :], neg_inf), axis=-1, keepdims=True)
            cmb_sc[(2 * c + d) * N_HEADS_M + h] = jnp.broadcast_to(cummax, (CHUNK, CHUNK))
    for c, d, r4, lf4, b4 in stats:
        within = (lan <= sub) if d == 0 else (lan >= sub)
        for h in range(N_HEADS_M):
            bb_sc[(2 * c + d) * N_HEADS_M + h] = _dot2(jnp.where(within, lf4[h:h + 1, :], 0.0), ones)

    for d in range(2):
        if has_state:
            c0_ref, n0_ref, m0_ref = refs[7:10]
            ncols = jnp.concatenate([n0_ref[d], jnp.zeros((12, DH_M), F32), zeros112], axis=0).T
        for h in range(N_HEADS_M):
            idx = 4 * d + h
            if has_state:
                st[idx][:, 0:DH_M] = c0_ref[d, h]
                st[idx][:, DH_M:] = _col_bcast(ncols, h)
                ms[idx][0:1, :] = m0_ref[idx:idx + 1, :]
            else:
                st[idx][...] = jnp.zeros((DH_M, 2 * DH_M), F32)
                ms[idx][0:1, :] = jnp.full((1, CHUNK), NEG_INIT, F32)

    def chunk_step(i, carry):
        for d in range(2):
            c = i if d == 0 else n_chunks - 1 - i
            r0 = pl.multiple_of(c * CHUNK, CHUNK)
            rows = rows_sc[2 * c + d]
            mask = (lan <= sub) if d == 0 else (lan >= sub)
            heads = []
            for h in range(N_HEADS_M):
                idx = 4 * d + h
                hs = slice(DH_M * h, DH_M * (h + 1))
                qc = q_ref[pl.ds(r0, CHUNK), hs]
                kt = kt_sc[c * N_HEADS_M + h]
                state = st[idx][...]
                heads.append(dict(idx=idx, hs=hs, qc=qc, kt=kt, state=state,
                                  s_raw=_dot(qc, kt), qs=_dot(qc, state.astype(BF16))))
            for h, hd in enumerate(heads):
                mp = ms[hd["idx"]][0:1, :]
                m_b = jnp.maximum(mp, cmb_sc[(2 * c + d) * N_HEADS_M + h])
                s = hd["s_raw"] * jnp.exp(jnp.where(mask, rows[h:h + 1, :] - m_b, neg_inf))
                s_hi = s.astype(BF16)
                vc = v_ref[pl.ds(r0, CHUNK), hd["hs"]]
                v_ones = jnp.concatenate([vc, ones], axis=1)
                hd.update(mp=mp, m_b=m_b, v_ones=v_ones, sv=_dot(s_hi, v_ones),
                          s_lo_sum=_dot((s - s_hi.astype(F32)).astype(BF16), ones))
            for h, hd in enumerate(heads):
                mp, m_b = hd["mp"], hd["m_b"]
                gw_b = jnp.exp(mp - m_b)
                en_b = jnp.exp(-(bb_sc[(2 * c + d) * N_HEADS_M + h] + m_b))
                num = hd["sv"][:, 0:DH_M] + gw_b * hd["qs"][:, 0:DH_M]
                den = hd["sv"][:, DH_M:] + hd["s_lo_sum"] + gw_b * hd["qs"][:, DH_M:]
                hdir[d][pl.ds(r0, CHUNK), hd["hs"]] = num / jnp.maximum(jnp.abs(den), en_b)
            for h, hd in enumerate(heads):
                mp = hd["mp"]
                gend = rows[8 + h:9 + h, :] + mp
                mnew = jnp.maximum(gend, rows[12 + h:13 + h, :])
                w_row = jnp.exp(rows[4 + h:5 + h, :] - mnew)
                kw_t = (hd["kt"].astype(F32) * w_row).astype(BF16)
                dec = jnp.exp(gend - mnew)
                st[hd["idx"]][...] = (jnp.concatenate([dec, dec], axis=1) * hd["state"]
                                      + _dot(kw_t, hd["v_ones"]))
                ms[hd["idx"]][0:1, :] = mnew
        return carry

    lax.fori_loop(0, n_chunks, chunk_step, 0)

    for h in range(N_HEADS_M):
        hs = slice(DH_M * h, DH_M * (h + 1))
        hh = hdir[0][:, hs] + hdir[1][:, hs]
        y = hh * lax.rsqrt(jnp.mean(hh * hh, axis=-1, keepdims=True) + EPS) * nw_ref[:, hs]
        hm_ref[:, hs] = (jax.nn.sigmoid(om_ref[:, hs]) * y).astype(BF16)
    if not has_state:
        c_ref, n_ref, m_ref = refs[n_in + 1:n_in + 4]
        for d in range(2):
            for h in range(N_HEADS_M):
                idx = 4 * d + h
                c_ref[d, h] = st[idx][:, 0:DH_M]
                n_ref[d, h:h + 1, :] = st[idx][:, DH_M:].T[0:1, :]
                m_ref[idx:idx + 1, :] = ms[idx][0:1, :]


def _mlstm(qm, km, vm, g3, bg, om, nw, state, *, n_batch, seq, first_block):
    n_chunks = seq // CHUNK
    rb = seq // TB
    tok = lambda: pl.BlockSpec((seq, 512), lambda b: (b + first_block // rb, 0))
    full = lambda a: pl.BlockSpec(a.shape, lambda b: (0,) * a.ndim)
    in_specs = [tok(), tok(), tok(),
                pl.BlockSpec((n_chunks, 16, CHUNK), lambda b: (b + first_block // rb, 0, 0)),
                full(bg), tok(), full(nw)]
    args = [qm, km, vm, g3, bg, om, nw]
    hm_spec = pl.BlockSpec((seq, 512), lambda b: (b, 0))
    hm_shape = jax.ShapeDtypeStruct((n_batch * seq, 512), BF16)
    scratch = ([pltpu.VMEM((DH_M, 2 * DH_M), F32)] * 8 + [pltpu.VMEM((8, CHUNK), F32)] * 8
               + [pltpu.VMEM((seq, 512), F32)] * 2
               + [pltpu.VMEM((2 * n_chunks, 32, CHUNK), F32),
                  pltpu.VMEM((8 * n_chunks, CHUNK, CHUNK), F32),
                  pltpu.VMEM((8 * n_chunks, CHUNK, CHUNK), F32),
                  pltpu.VMEM((N_HEADS_M * n_chunks, DH_M, CHUNK), BF16)])
    if state is not None:
        c0, n0, m0 = state
        in_specs += [pl.BlockSpec((None, None, 2, N_HEADS_M, DH_M, DH_M), lambda b: (b, 0, 0, 0, 0, 0)),
                     pl.BlockSpec((None, None, 2, N_HEADS_M, DH_M), lambda b: (b, 0, 0, 0, 0)),
                     pl.BlockSpec((None, 8, CHUNK), lambda b: (b, 0, 0))]
        args += [c0, n0, m0]
        out_specs, out_shape = hm_spec, hm_shape
    else:
        out_specs = [hm_spec,
                     pl.BlockSpec((None, 2, N_HEADS_M, DH_M, DH_M), lambda b: (b, 0, 0, 0, 0)),
                     pl.BlockSpec((None, 2, N_HEADS_M, DH_M), lambda b: (b, 0, 0, 0)),
                     pl.BlockSpec((None, 8, CHUNK), lambda b: (b, 0, 0))]
        out_shape = [hm_shape,
                     jax.ShapeDtypeStruct((n_batch, 2, N_HEADS_M, DH_M, DH_M), F32),
                     jax.ShapeDtypeStruct((n_batch, 2, N_HEADS_M, DH_M), F32),
                     jax.ShapeDtypeStruct((n_batch, 8, CHUNK), F32)]
    return pl.pallas_call(
        functools.partial(_mlstm_kernel, has_state=state is not None, n_chunks=n_chunks),
        grid=(n_batch,),
        in_specs=in_specs,
        out_specs=out_specs,
        out_shape=out_shape,
        scratch_shapes=scratch,
        compiler_params=_cparams(("arbitrary",)),
        name="mlstm_lat" if state is not None else "mlstm_ctx",
    )(*args)


def _outproj_kernel(attc_ref, attl_ref, hmc_ref, hml_ref, xc_ref, xl_ref, mod_ref, nw_ref, wo_ref,
                    rwh_ref, rwl_ref, wsg_ref, wsu_ref, wsd_ref, base_ref, hn_ref, lt_ref):
    y = (_dot(_ctx_or_lat(attc_ref, attl_ref), wo_ref[0:D_ATT, :])
         + _dot(_ctx_or_lat(hmc_ref, hml_ref), wo_ref[D_ATT:, :]))
    x1 = _ctx_or_lat(xc_ref, xl_ref) + mod_ref[2:3, :] * y
    z = x1 * lax.rsqrt(jnp.mean(x1 * x1, axis=-1, keepdims=True) + EPS) * nw_ref[...]
    hn = z * (1.0 + mod_ref[4:5, :]) + mod_ref[3:4, :]
    hb = hn.astype(BF16)
    hl = (hn - hb.astype(F32)).astype(BF16)
    nt = lambda w, t: lax.dot_general(w, t, _NT, preferred_element_type=F32)
    lt_ref[...] = nt(rwh_ref[...], hb) + nt(rwl_ref[...], hb) + nt(rwh_ref[...], hl)
    a = _silu(_dot(hb, wsg_ref[...])) * _dot(hb, wsu_ref[...])
    shared = _dot(a.astype(BF16), wsd_ref[...])
    base_ref[...] = x1 + mod_ref[5:6, :] * shared
    packed = pltpu.pack_elementwise([hn[:, :512], hn[:, 512:]], packed_dtype=BF16)
    for c in range(4):
        hn_ref[pl.ds(c, TBP, stride=4), :] = packed[:, 128 * c:128 * (c + 1)]


def _outproj(att, hm, x, mod3, norm2_w, w_out, rw_hi, rw_lo, wsg, wsu, wsd):
    tok = lambda w: pl.BlockSpec((TBP, w), lambda i: (i, 0))
    full = lambda a: pl.BlockSpec(a.shape, lambda i: (0,) * a.ndim)
    return pl.pallas_call(
        _outproj_kernel,
        grid=(T_ALL // TBP,),
        in_specs=_split_specs(512) + _split_specs(512) + _split_specs(D_MODEL) + [
                  pl.BlockSpec((None, 6, D_MODEL), lambda i: (_mod_row(i), 0, 0)),
                  full(norm2_w), full(w_out), full(rw_hi), full(rw_lo), full(wsg), full(wsu), full(wsd)],
        out_specs=[tok(D_MODEL), pl.BlockSpec((4 * TBP, 128), lambda i: (i, 0)),
                   pl.BlockSpec((N_EXPERTS, TBP), lambda i: (0, i))],
        out_shape=[jax.ShapeDtypeStruct((T_ALL, D_MODEL), F32),
                   jax.ShapeDtypeStruct((4 * T_ALL, 128), jnp.uint32),
                   jax.ShapeDtypeStruct((N_EXPERTS, T_ALL), F32)],
        compiler_params=_cparams(("arbitrary",)),
        name="outproj",
    )(*att, *hm, *x, mod3, norm2_w, w_out, rw_hi, rw_lo, wsg, wsu, wsd)


def _first_max(vals, ids, limit):
    m = functools.reduce(jnp.maximum, [jnp.max(v, axis=0, keepdims=True) for v in vals])
    cand = [jnp.min(jnp.where(v == m, i, limit), axis=0, keepdims=True) for v, i in zip(vals, ids)]
    return m, functools.reduce(jnp.minimum, cand)


def _router_kernel(lt_ref, bias_ref, gate_ref, idx_ref):
    n = lt_ref.shape[1]
    score = jax.nn.sigmoid(lt_ref[...])
    biased = score + bias_ref[...]
    sub = lax.broadcasted_iota(jnp.int32, (8, n), 0).astype(F32)
    neg_inf = F32(-jnp.inf)
    slabs = [biased[8 * g:8 * (g + 1), :] for g in range(N_GROUPS)]
    gs = []
    for sl in slabs:
        m1, i1 = _first_max([sl], [sub], 8.0)
        m2 = jnp.max(jnp.where(sub == i1, neg_inf, sl), axis=0, keepdims=True)
        gs.append(m1 + m2)
    cur = jnp.concatenate(gs, axis=0)
    gsel = jnp.zeros((8, n), F32)
    for _ in range(TOPK_GROUPS):
        _, i = _first_max([cur], [sub], 8.0)
        hit = sub == i
        gsel = jnp.where(hit, 1.0, gsel)
        cur = jnp.where(hit, neg_inf, cur)
    vals = [jnp.where(gsel[g:g + 1, :] > 0.0, slabs[g], neg_inf) for g in range(N_GROUPS)]
    ids = [sub + 8.0 * g for g in range(N_GROUPS)]
    picked = [jnp.zeros((8, n), F32) for _ in range(N_GROUPS)]
    order = []
    for _ in range(TOP_K):
        _, i = _first_max(vals, ids, float(N_EXPERTS))
        order.append(i)
        hits = [idg == i for idg in ids]
        picked = [jnp.where(hh, score[8 * g:8 * (g + 1), :], p) for g, (p, hh) in enumerate(zip(picked, hits))]
        vals = [jnp.where(hh, neg_inf, v) for v, hh in zip(vals, hits)]
    total = functools.reduce(jnp.add, [jnp.sum(p, axis=0, keepdims=True) for p in picked])
    gate_t = jnp.concatenate([p / total * ROUTED_SCALE for p in picked]
                             + [jnp.zeros((128 - N_EXPERTS, n), F32)], axis=0)
    gate_ref[...] = gate_t.T
    idx_ref[...] = jnp.concatenate(order, axis=0).astype(jnp.int32)


def _router(logits_t, bias_col):
    return pl.pallas_call(
        _router_kernel,
        grid=(T_ALL // TB_MOE,),
        in_specs=[pl.BlockSpec((N_EXPERTS, TB_MOE), lambda i: (0, i)),
                  pl.BlockSpec((N_EXPERTS, 1), lambda i: (0, 0))],
        out_specs=[pl.BlockSpec((TB_MOE, 128), lambda i: (i, 0)),
                   pl.BlockSpec((TOP_K, TB_MOE), lambda i: (0, i))],
        out_shape=[jax.ShapeDtypeStruct((T_ALL, 128), F32),
                   jax.ShapeDtypeStruct((TOP_K, T_ALL), jnp.int32)],
        compiler_params=_cparams(("arbitrary",)),
        name="router",
    )(logits_t, bias_col)


def _plan_kernel(idx_ref, slot_ref, cnt_ref, off_ref, pos_sc):
    n_tiles = T_HALF // 128
    eid = lax.broadcasted_iota(jnp.int32, (N_EXPERTS, 128), 0)
    tri = (lax.broadcasted_iota(jnp.int32, (128, 128), 0)
           <= lax.broadcasted_iota(jnp.int32, (128, 128), 1)).astype(BF16)
    carry = jnp.zeros((N_EXPERTS, 1), F32)
    for j in range(n_tiles):
        it = idx_ref[:, 128 * j:128 * (j + 1)]
        sel = jnp.zeros((N_EXPERTS, 128), F32)
        for k in range(TOP_K):
            sel = jnp.where(it[k:k + 1, :] == eid, 1.0, sel)
        inc = _dot(sel.astype(BF16), tri) + carry
        carry = inc[:, 127:128]
        pos_sc[:, 128 * j:128 * (j + 1)] = inc - 1.0
    count = jnp.broadcast_to(carry, (N_EXPERTS, 128))
    padded = jnp.floor((count + (MT - 1.0)) * (1.0 / MT)) * MT
    before = (lax.broadcasted_iota(jnp.int32, (N_EXPERTS, N_EXPERTS), 1)
              < lax.broadcasted_iota(jnp.int32, (N_EXPERTS, N_EXPERTS), 0)).astype(BF16)
    hi, mid, lo = _split3(padded)
    off = _dot(before, hi) + _dot(before, mid) + _dot(before, lo)
    cnt_ref[...] = count.astype(jnp.int32)
    off_ref[...] = off.astype(jnp.int32)

    def slots(j):
        it = idx_ref[:, 128 * j:128 * (j + 1)]
        val = off + pos_sc[:, 128 * j:128 * (j + 1)]
        rows = [jnp.sum(jnp.where(it[k:k + 1, :] == eid, val, 0.0), axis=0, keepdims=True)
                for k in range(TOP_K)]
        return jnp.concatenate(rows, axis=0).astype(jnp.int32)

    for j in range(n_tiles):
        s = slots(j)
        slot_ref[:, 128 * j:128 * (j + 1)] = s[0:TOP_K // 2, :] | (s[TOP_K // 2:, :] << 16)


def _plan(idx8):
    return pl.pallas_call(
        _plan_kernel,
        grid=(2,),
        in_specs=[pl.BlockSpec((TOP_K, T_HALF), lambda h: (0, h))],
        out_specs=[pl.BlockSpec((None, TOP_K // 2, T_HALF), lambda h: (h, 0, 0)),
                   pl.BlockSpec((None, N_EXPERTS, 128), lambda h: (h, 0, 0)),
                   pl.BlockSpec((None, N_EXPERTS, 128), lambda h: (h, 0, 0))],
        out_shape=[jax.ShapeDtypeStruct((2, TOP_K // 2, T_HALF), jnp.int32),
                   jax.ShapeDtypeStruct((2, N_EXPERTS, 128), jnp.int32),
                   jax.ShapeDtypeStruct((2, N_EXPERTS, 128), jnp.int32)],
        scratch_shapes=[pltpu.VMEM((N_EXPERTS, T_HALF), F32)],
        compiler_params=_cparams(("arbitrary",)),
        name="plan",
    )(idx8)


def _prepare_half(slot_ref, cnt_ref, off_ref, tok_ref, tile_e, tile_first, elist, h):
    def per_expert(x, carry):
        j, q = carry
        n = cnt_ref[h * N_EXPERTS + x]
        tiles = (n + MT - 1) // MT
        elist[q] = x

        def mark(i, c):
            tile_e[j + i] = x
            tile_first[j + i] = jnp.where(i == 0, 1, 0)
            return c

        lax.fori_loop(0, tiles, mark, 0)

        return j + tiles, q + jnp.where(tiles > 0, 1, 0)

    n_tiles, n_live = lax.fori_loop(0, N_EXPERTS, per_expert, (0, 0))
    for extra in range(2):
        tile_e[n_tiles + extra] = 0
        tile_first[n_tiles + extra] = 0
    return n_tiles, n_live


def _invert_slots(slot_ref, tok_ref, acc_v, h):
    rounds = TOP_K // 2
    trips = T_HALF // INV_UNROLL
    zero_rows = 8 * T_HALF // (rounds * trips)
    assert zero_rows * rounds * trips == 8 * T_HALF and zero_rows % 8 == 0
    for k in range(rounds):
        def body(i, carry, k=k):
            words = [slot_ref[(h * rounds + k) * T_HALF + i * INV_UNROLL + u] for u in range(INV_UNROLL)]
            for u in range(INV_UNROLL):
                t = i * INV_UNROLL + u
                tok_ref[words[u] & 0xFFFF] = t
                tok_ref[lax.shift_right_logical(words[u], 16)] = t
            row = pl.multiple_of((k * trips + i) * zero_rows, 8)
            acc_v[pl.ds(row, zero_rows), :] = jnp.zeros((zero_rows, 128), F32)
            return carry

        lax.fori_loop(0, trips, body, 0)
    acc_v[pl.ds(8 * T_HALF, 8), :] = jnp.zeros((8, 128), F32)


def _moe_kernel(slot_ref, cnt_ref, off_ref, src_hbm, gate_hbm, wg_hbm, wu_hbm, wd_hbm, base_hbm, mod_ref,
                outc_hbm, outl_hbm,
                src_v, gate_v, acc_v, wg_l, wu_l, wd_l, wgu_b, wd_b, xbuf0, xbuf1, gbuf0, gbuf1,
                ybuf0, ybuf1, tok_ref, tile_e, tile_first, elist, live, base_buf, out_buf, pad_v,
                sem, wsem, fsem):
    h = pl.program_id(0)
    xbuf, gbuf, ybuf = (xbuf0, xbuf1), (gbuf0, gbuf1), (ybuf0, ybuf1)

    src_cp = pltpu.make_async_copy(src_hbm.at[pl.ds(pl.multiple_of(h * (4 * T_HALF), 8), 4 * T_HALF)],
                                   src_v.at[pl.ds(0, 4 * T_HALF)], sem.at[0])
    gate_cp = pltpu.make_async_copy(gate_hbm.at[pl.ds(pl.multiple_of(h * T_HALF, 8), T_HALF)],
                                    gate_v.at[pl.ds(0, T_HALF)], sem.at[1])
    src_cp.start()
    gate_cp.start()
    pad_v[...] = jnp.full(pad_v.shape, T_HALF, jnp.int32)
    pad_cp = pltpu.make_async_copy(pad_v, tok_ref, sem.at[2])
    pad_cp.start()

    def weight_copies(x, slot):
        return [pltpu.make_async_copy(w_hbm.at[x], w_l.at[slot], wsem.at[slot, i])
                for i, (w_hbm, w_l) in enumerate(((wg_hbm, wg_l), (wu_hbm, wu_l), (wd_hbm, wd_l)))]

    n_tiles, n_live = _prepare_half(slot_ref, cnt_ref, off_ref, tok_ref, tile_e, tile_first, elist, h)
    live[0] = 0
    live[1] = n_live

    for ahead in range(W_SLOTS - 1):
        @pl.when(ahead < n_live)
        def _(ahead=ahead):
            for cp in weight_copies(elist[ahead], ahead):
                cp.start()

    zero = jnp.zeros((8, 128), F32)
    src_v[pl.ds(4 * T_HALF, 8), :] = pltpu.pack_elementwise([zero, zero], packed_dtype=BF16)
    gate_v[pl.ds(T_HALF, 8), :] = zero
    ybuf0[...] = jnp.zeros(ybuf0.shape, F32)
    ybuf1[...] = jnp.zeros(ybuf1.shape, F32)
    pad_cp.wait()
    _invert_slots(slot_ref, tok_ref, acc_v, h)
    src_cp.wait()
    gate_cp.wait()

    def switch_expert():
        q = live[0]
        slot = q % W_SLOTS
        for cp in weight_copies(elist[q], slot):
            cp.wait()
        wgu_b[:, 0:D_EXPERT] = wg_l[slot].astype(BF16)
        wgu_b[:, D_EXPERT:] = wu_l[slot].astype(BF16)
        wd_b[...] = wd_l[slot].astype(BF16)
        live[0] = q + 1
        nxt = q + W_SLOTS - 1

        @pl.when(nxt < live[1])
        def _():
            for cp in weight_copies(elist[nxt], nxt % W_SLOTS):
                cp.start()

    def gather(j, xb, gb, rows=(0, MT)):
        base = j * MT
        for m in range(*rows):
            t = tok_ref[base + m]
            xb[pl.ds(m, 4, stride=MT_STRIDE), :] = src_v[pl.ds(pl.multiple_of(t * 4, 4), 4), :]
            gb[m:m + 1, :] = gate_v[pl.ds(t, 1), :]

    def scatter(j, yb, rows=(0, MT)):
        base = j * MT
        for b in range(rows[0] // RMW_BATCH, rows[1] // RMW_BATCH):
            ms = [b * RMW_BATCH + u for u in range(RMW_BATCH)]
            targets = [pl.ds(pl.multiple_of(tok_ref[base + m] * 8, 8), 8) for m in ms]
            vals = [acc_v[r, :] + yb[pl.ds(m, 8, stride=MT_STRIDE), :] for r, m in zip(targets, ms)]
            for r, v in zip(targets, vals):
                acc_v[r, :] = v

    quarters = [(q * MT // 4, (q + 1) * MT // 4) for q in range(4)]

    def step(j, p):
        pl.when(tile_first[j] == 1)(switch_expert)
        xb, gb, yb = xbuf[p], gbuf[p], ybuf[p]
        nxt = (j + 1, xbuf[1 - p], gbuf[1 - p])
        prv = (jnp.maximum(j - 1, 0), ybuf[1 - p])
        lo, hi = [], []
        for c in range(4):
            words = xb[MT_STRIDE * c:MT_STRIDE * c + MT, :]
            unpack = functools.partial(pltpu.unpack_elementwise, words, packed_dtype=BF16, unpacked_dtype=F32)
            lo.append(unpack(index=0).astype(BF16))
            hi.append(unpack(index=1).astype(BF16))
        x = jnp.concatenate(lo + hi, axis=1)
        gather(*nxt, rows=quarters[0])
        h_gate = _dot(x, wgu_b[:, 0:D_EXPERT])
        gather(*nxt, rows=quarters[1])
        h_up = _dot(x, wgu_b[:, D_EXPERT:])
        gather(*nxt, rows=quarters[2])
        g = gb[...]
        g_hi = g.astype(BF16)
        g_lo = (g - g_hi.astype(F32)).astype(BF16)
        pick = (lax.broadcasted_iota(jnp.int32, (128, D_EXPERT), 0) == tile_e[j]).astype(BF16)
        g_col = _dot(g_hi, pick) + _dot(g_lo, pick)
        a = (_silu(h_gate) * h_up * g_col).astype(BF16)
        gather(*nxt, rows=quarters[3])
        scatter(*prv, rows=quarters[0])
        y_lo = _dot(a, wd_b[:, 0:D_MODEL // 2])
        scatter(*prv, rows=quarters[1])
        y_hi = _dot(a, wd_b[:, D_MODEL // 2:])
        scatter(*prv, rows=quarters[2])
        for c in range(4):
            yb[MT_STRIDE * c:MT_STRIDE * c + MT, :] = y_lo[:, 128 * c:128 * (c + 1)]
            yb[MT_STRIDE * (c + 4):MT_STRIDE * (c + 4) + MT, :] = y_hi[:, 128 * c:128 * (c + 1)]
        scatter(*prv, rows=quarters[3])

    gather(0, xbuf[0], gbuf[0])
    n_pairs = (n_tiles + 1) // 2

    def pair(i, carry):
        step(2 * i, 0)
        step(2 * i + 1, 1)
        return carry

    lax.fori_loop(0, n_pairs, pair, 0)
    scatter(jnp.maximum(2 * n_pairs - 1, 0), ybuf[1])

    n_fin = T_HALF // FIN

    def first_token(c):
        return h * T_HALF + c * FIN

    def base_copy(c, slot):
        return pltpu.make_async_copy(base_hbm.at[pl.ds(pl.multiple_of(first_token(c), FIN), FIN)],
                                     base_buf.at[slot], fsem.at[0, slot])

    def on_out_copy(c, action):
        t0 = first_token(c)
        slot = c % FIN_OUT_SLOTS

        @pl.when(t0 < T_CTX)
        def _():
            action(pltpu.make_async_copy(out_buf.at[slot], outc_hbm.at[pl.ds(pl.multiple_of(t0, FIN), FIN)],
                                         fsem.at[1, slot]))

        @pl.when(t0 >= T_CTX)
        def _():
            action(pltpu.make_async_copy(out_buf.at[slot],
                                         outl_hbm.at[pl.ds(pl.multiple_of(t0 - T_CTX, FIN), FIN)],
                                         fsem.at[1, slot]))

    for c in range(FIN_BASE_SLOTS - 1):
        base_copy(c, c).start()

    def combine(c, carry):
        slot = c % FIN_BASE_SLOTS
        oslot = c % FIN_OUT_SLOTS
        base_copy(c, slot).wait()
        ahead = c + FIN_BASE_SLOTS - 1

        @pl.when(ahead < n_fin)
        def _():
            base_copy(ahead, ahead % FIN_BASE_SLOTS).start()

        @pl.when(c >= FIN_OUT_SLOTS)
        def _():
            on_out_copy(c - FIN_OUT_SLOTS, lambda cp: cp.wait())

        t0 = first_token(c)
        mod_row = jnp.where(t0 < T_CTX, 0, 1 + (t0 - T_CTX) // DEC_SEQ)
        acc_row = pl.multiple_of(8 * FIN * c, 8)
        for cc in range(8):
            cs = slice(128 * cc, 128 * (cc + 1))
            routed = acc_v[pl.ds(acc_row + cc, FIN, stride=8), :]
            out_buf[oslot, :, cs] = base_buf[slot, :, cs] + mod_ref[mod_row, 5:6, cs] * routed
        on_out_copy(c, lambda cp: cp.start())
        return carry

    lax.fori_loop(0, n_fin, combine, 0)
    for c in range(n_fin - FIN_OUT_SLOTS, n_fin):
        on_out_copy(c, lambda cp: cp.wait())


def _moe(slot_words, cnt, off, src, gate, wg, wu, wd, base, mod3):
    any_spec = pl.BlockSpec(memory_space=pl.ANY)
    tile_buf = lambda rows, dt: pltpu.VMEM((rows * MT_STRIDE, 128), dt)
    return pl.pallas_call(
        _moe_kernel,
        grid_spec=pltpu.PrefetchScalarGridSpec(
            num_scalar_prefetch=3,
            grid=(2,),
            in_specs=[any_spec] * 6 + [pl.BlockSpec(mod3.shape, lambda h, *_: (0, 0, 0))],
            out_specs=[any_spec, any_spec],
            scratch_shapes=[pltpu.VMEM((4 * T_HALF + 8, 128), jnp.uint32),
                            pltpu.VMEM((T_HALF + 8, 128), F32),
                            pltpu.VMEM((8 * T_HALF + 8, 128), F32),
                            pltpu.VMEM((W_SLOTS, D_MODEL, D_EXPERT), F32),
                            pltpu.VMEM((W_SLOTS, D_MODEL, D_EXPERT), F32),
                            pltpu.VMEM((W_SLOTS, D_EXPERT, D_MODEL), F32),
                            pltpu.VMEM((D_MODEL, 2 * D_EXPERT), BF16),
                            pltpu.VMEM((D_EXPERT, D_MODEL), BF16),
                            tile_buf(4, jnp.uint32), tile_buf(4, jnp.uint32),
                            pltpu.VMEM((MT, 128), F32), pltpu.VMEM((MT, 128), F32),
                            tile_buf(8, F32), tile_buf(8, F32),
                            pltpu.SMEM((SLOT_CAP,), jnp.int32),
                            pltpu.SMEM((NT_MAX + 2,), jnp.int32),
                            pltpu.SMEM((NT_MAX + 2,), jnp.int32),
                            pltpu.SMEM((N_EXPERTS,), jnp.int32),
                            pltpu.SMEM((2,), jnp.int32),
                            pltpu.VMEM((FIN_BASE_SLOTS, FIN, D_MODEL), F32),
                            pltpu.VMEM((FIN_OUT_SLOTS, FIN, D_MODEL), F32),
                            pltpu.VMEM((SLOT_CAP,), jnp.int32),
                            pltpu.SemaphoreType.DMA((3,)),
                            pltpu.SemaphoreType.DMA((W_SLOTS, 3)),
                            pltpu.SemaphoreType.DMA((2, FIN_BASE_SLOTS))]),
        out_shape=[jax.ShapeDtypeStruct((T_CTX, D_MODEL), F32), jax.ShapeDtypeStruct((T_LAT, D_MODEL), F32)],
        compiler_params=_cparams(("arbitrary",)),
        name="moe",
    )(slot_words, cnt, off, src, gate, wg, wu, wd, base, mod3)


def _head_indicators(width):
    head = jnp.arange(width) // DH_ATT
    ind = (head[:, None] == jnp.arange(128)[None, :]).astype(BF16)
    return ind, ind.T


def kernel(x_prompt, x_sample, cache_attn_k, cache_attn_v, state_mlstm_c, state_mlstm_n, state_mlstm_m, c, c_ctx, w_mod, b_mod, norm1_w, norm2_w, w_in, q_norm_w, k_norm_w, b_gates, m_norm_w, w_out, router_w, router_bias, w_gate, w_up, w_down, ws_gate, ws_up, ws_down):
    x = (x_prompt.reshape(T_CTX, D_MODEL), x_sample.reshape(T_LAT, D_MODEL))
    mod3 = _modulation(c, c_ctx, w_mod[0], b_mod[0])

    w_main = w_in[0]
    w_gates_t = w_in[0, :, P_MAIN:].T.astype(BF16)
    qw = jnp.tile(q_norm_w, (1, N_HEADS_ATT))
    kw = jnp.tile(k_norm_w, (1, N_KV_HEADS))
    inds = _head_indicators(512) + _head_indicators(128)
    qn, kn, va, qm, km, vm, om, gt, new_k, new_v = _inproj(*x, mod3, norm1_w, w_main, w_gates_t, qw, kw, inds,
                                                           _rope_tables())

    att_c = _attention(qn, kn, va, None, n_batch=BATCH, seq=SEQ, first_row=0)
    att_l = _attention(qn, kn, va, (cache_attn_k, cache_attn_v), n_batch=DEC_BATCH, seq=DEC_SEQ,
                       first_row=T_CTX)

    g3 = gt.reshape(16, T_ALL // CHUNK, CHUNK).transpose(1, 0, 2)
    bg = b_gates.reshape(16, 1)
    hm_c, c_new, n_new, m_new = _mlstm(qm, km, vm, g3, bg, om, m_norm_w, None,
                                       n_batch=BATCH, seq=SEQ, first_block=0)
    m0 = jnp.broadcast_to(state_mlstm_m.reshape(DEC_BATCH, 8, 1), (DEC_BATCH, 8, CHUNK))
    hm_l = _mlstm(qm, km, vm, g3, bg, om, m_norm_w, (state_mlstm_c, state_mlstm_n, m0),
                  n_batch=DEC_BATCH, seq=DEC_SEQ, first_block=N_CTX_BLOCKS)

    rw_t = router_w[0].T
    rw_hi = rw_t.astype(BF16)
    rw_lo = (rw_t - rw_hi.astype(F32)).astype(BF16)
    base, hn2, logits_t = _outproj((att_c, att_l), (hm_c, hm_l), x, mod3, norm2_w, w_out[0].astype(BF16),
                                   rw_hi, rw_lo, ws_gate[0].astype(BF16), ws_up[0].astype(BF16),
                                   ws_down[0].astype(BF16))
    gate, idx8 = _router(logits_t, router_bias.reshape(N_EXPERTS, 1))
    slot_words, cnt, off = _plan(idx8)
    out_c, out_l = _moe(slot_words.reshape(TOP_K * T_HALF), cnt[:, :, 0].reshape(2 * N_EXPERTS),
                        off[:, :, 0].reshape(2 * N_EXPERTS), hn2, gate, w_gate[0], w_up[0], w_down[0],
                        base, mod3)

    y_prompt = out_c.reshape(BATCH, SEQ, D_MODEL)
    y_sample = out_l.reshape(DEC_BATCH, DEC_SEQ, D_MODEL)
    new_m = m_new[:, :, 0].reshape(BATCH, 1, 2, N_HEADS_M)
    return (y_prompt, y_sample, new_k[:, None], new_v[:, None], c_new[:, None], n_new[:, None], new_m)
```

```python
import functools

import jax
import jax.numpy as jnp
from jax import lax
from jax.experimental import pallas as pl
from jax.experimental.pallas import tpu as pltpu

F32 = jnp.float32
BF16 = jnp.bfloat16

D_MODEL = 1024
BATCH = 32
SEQ = 256
DEC_BATCH = 2
DEC_SEQ = 1024
PAST_LEN = 256
GRID_W = 64
N_HEADS_ATT = 8
N_KV_HEADS = 2
DH_ATT = 64
D_ATT = 512
ROPE_THETA = 10000.0
N_HEADS_M = 4
DH_M = 128
D_M = 512
CHUNK = 128
N_EXPERTS = 64
TOP_K = 8
N_GROUPS = 8
TOPK_GROUPS = 4
D_EXPERT = 256
ROUTED_SCALE = 2.5
EPS = 1e-6
NEG_INIT = -1e30

T_CTX = BATCH * SEQ
T_LAT = DEC_BATCH * DEC_SEQ
T_ALL = T_CTX + T_LAT
TB = 256
N_CTX_BLOCKS = T_CTX // TB
TBP = 512
NP_CTX = T_CTX // TBP
NP_LAT_PER_BATCH = DEC_SEQ // TBP
TB_MOE = 1024
T_HALF = T_ALL // 2
MT = 256
MT_STRIDE = MT + 8
NT_MAX = TOP_K * T_HALF // MT + N_EXPERTS
SLOT_CAP = -(-(NT_MAX + 2) * MT // 1024) * 1024
RMW_BATCH = 16
INV_UNROLL = 16
W_SLOTS = 2
FIN = 128
FIN_BASE_SLOTS = 6
FIN_OUT_SLOTS = 4
assert FIN_OUT_SLOTS <= FIN_BASE_SLOTS
assert SLOT_CAP < 2 ** 16 and T_HALF % INV_UNROLL == 0 and MT % RMW_BATCH == 0
P_MAIN = 2816
VMEM_LIMIT = 56 * 1024 * 1024

_NT = (((1,), (1,)), ((), ()))


def _cparams(sem):
    return pltpu.CompilerParams(dimension_semantics=sem, vmem_limit_bytes=VMEM_LIMIT)


def _split3(x):
    hi = x.astype(BF16)
    r1 = x - hi.astype(F32)
    mid = r1.astype(BF16)
    lo = (r1 - mid.astype(F32)).astype(BF16)
    return hi, mid, lo


def _dot(a, b):
    return jnp.dot(a, b, preferred_element_type=F32)


def _dot3(x, m_bf16):
    hi, mid, lo = _split3(x)
    return _dot(hi, m_bf16) + _dot(mid, m_bf16) + _dot(lo, m_bf16)


def _dot2(x, m_bf16):
    hi = x.astype(BF16)
    lo = (x - hi.astype(F32)).astype(BF16)
    return _dot(hi, m_bf16) + _dot(lo, m_bf16)


def _silu(x):
    return x * jax.nn.sigmoid(x)


def _mod_row(i):
    return jnp.where(i < NP_CTX, 0, 1 + (i - NP_CTX) // NP_LAT_PER_BATCH)


def _mod_kernel(ct_ref, w_ref, b_ref, o_ref):
    s = _silu(ct_ref[...])
    w = w_ref[...]
    rows = [jnp.sum(w * s[:, r:r + 1], axis=0, keepdims=True) for r in range(3)]
    rows.append(jnp.zeros((5, w.shape[1]), F32))
    o_ref[...] = jnp.concatenate(rows, axis=0) + b_ref[...]


def _modulation(c, c_ctx, w_mod, b_mod):
    cvec = jnp.concatenate([c_ctx[None, :], c, jnp.zeros((5, D_MODEL), F32)], axis=0)
    nb = 1024
    out = pl.pallas_call(
        _mod_kernel,
        grid=(6 * D_MODEL // nb,),
        in_specs=[pl.BlockSpec((D_MODEL, 8), lambda j: (0, 0)),
                  pl.BlockSpec((D_MODEL, nb), lambda j: (0, j)),
                  pl.BlockSpec((1, nb), lambda j: (0, j))],
        out_specs=pl.BlockSpec((8, nb), lambda j: (0, j)),
        out_shape=jax.ShapeDtypeStruct((8, 6 * D_MODEL), F32),
        compiler_params=_cparams(("arbitrary",)),
        name="modulation",
    )(cvec.T, w_mod, b_mod[None, :])
    return out.reshape(8, 6, D_MODEL)


def _ctx_or_lat(ctx_ref, lat_ref):
    return jnp.where(pl.program_id(0) < NP_CTX, ctx_ref[...], lat_ref[...])


def _split_specs(width):
    return [pl.BlockSpec((TBP, width), lambda i: (jnp.minimum(i, NP_CTX - 1), 0)),
            pl.BlockSpec((TBP, width), lambda i: (jnp.maximum(i - NP_CTX, 0), 0))]


def _head_norm(x, ind, ind_t, w_row):
    ss = _dot2(x * x, ind)
    inv = lax.rsqrt(ss * (1.0 / DH_ATT) + EPS)
    return x * _dot2(inv, ind_t) * w_row


def _rope(x, cos, sin_signed):
    lane = lax.broadcasted_iota(jnp.int32, x.shape, 1)
    partner = jnp.where((lane % 32) < 16, pltpu.roll(x, 128 - 16, 1), pltpu.roll(x, 16, 1))
    return x * cos + partner * sin_signed


def _inproj_kernel(xc_ref, xl_ref, mod_ref, nw_ref, w_ref, wgt_ref, qw_ref, kw_ref, iq_ref, iqt_ref,
                   ik_ref, ikt_ref, cos_ref, sin_ref,
                   qn_ref, kn_ref, va_ref, qm_ref, km_ref, vm_ref, om_ref, gt_ref, kc_ref, vc_ref, w_b):
    @pl.when(pl.program_id(0) == 0)
    def _():
        for j in range(P_MAIN // 128):
            w_b[:, 128 * j:128 * (j + 1)] = w_ref[:, 128 * j:128 * (j + 1)].astype(BF16)

    x = _ctx_or_lat(xc_ref, xl_ref)
    y = x * lax.rsqrt(jnp.mean(x * x, axis=-1, keepdims=True) + EPS) * nw_ref[...]
    hn = y * (1.0 + mod_ref[1:2, :]) + mod_ref[0:1, :]
    hb = hn.astype(BF16)
    cols = lambda a, b: w_b[:, a:b]
    qn = _head_norm(_dot(hb, cols(0, 512)), iq_ref[...], iqt_ref[...], qw_ref[...])
    kn = _head_norm(_dot(hb, cols(512, 640)), ik_ref[...], ikt_ref[...], kw_ref[...])
    qn_ref[...] = qn.astype(BF16)
    kn_ref[...] = kn

    @pl.when(pl.program_id(0) >= NP_CTX)
    def _():
        cos, sin = cos_ref[...], sin_ref[...]
        for j in range(4):
            qn_ref[:, 128 * j:128 * (j + 1)] = _rope(qn[:, 128 * j:128 * (j + 1)], cos, sin).astype(BF16)
        kn_ref[...] = _rope(kn, cos, sin)

    va = _dot(hb, cols(640, 768))
    va_ref[...] = va

    @pl.when(pl.program_id(0) < NP_CTX)
    def _():
        for bb in range(TBP // SEQ):
            for g in range(N_KV_HEADS):
                rows, lanes = slice(SEQ * bb, SEQ * (bb + 1)), slice(DH_ATT * g, DH_ATT * (g + 1))
                kc_ref[bb, g] = kn[rows, lanes]
                vc_ref[bb, g] = va[rows, lanes]

    qm_ref[...] = _dot(hb, cols(768, 1280)).astype(BF16)
    km_ref[...] = (_dot(hb, cols(1280, 1792)) * (DH_M ** -0.5)).astype(BF16)
    vm_ref[...] = _dot(hb, cols(1792, 2304)).astype(BF16)
    om_ref[...] = _dot(hb, cols(2304, 2816))
    gt_ref[...] = lax.dot_general(wgt_ref[...], hb, _NT, preferred_element_type=F32)


def _inproj(x_ctx, x_lat, mod3, norm1_w, w_main, w_gates_t, qw, kw, inds, rope_tabs):
    tok = lambda w: pl.BlockSpec((TBP, w), lambda i: (i, 0))
    full = lambda a: pl.BlockSpec(a.shape, lambda i: (0,) * a.ndim)
    sd = lambda w, dt: jax.ShapeDtypeStruct((T_ALL, w), dt)
    rope_spec = pl.BlockSpec((TBP, 128), lambda i: (jnp.maximum(i - NP_CTX, 0) % NP_LAT_PER_BATCH, 0))
    cache_spec = pl.BlockSpec((TBP // SEQ, N_KV_HEADS, SEQ, DH_ATT),
                              lambda i: (jnp.minimum(i, NP_CTX - 1), 0, 0, 0))
    cache_shape = jax.ShapeDtypeStruct((BATCH, N_KV_HEADS, SEQ, DH_ATT), F32)
    return pl.pallas_call(
        _inproj_kernel,
        grid=(T_ALL // TBP,),
        in_specs=_split_specs(D_MODEL) + [
                  pl.BlockSpec((None, 6, D_MODEL), lambda i: (_mod_row(i), 0, 0)),
                  full(norm1_w), full(w_main), full(w_gates_t), full(qw), full(kw)]
                 + [full(a) for a in inds] + [rope_spec, rope_spec],
        out_specs=[tok(512), tok(128), tok(128), tok(512), tok(512), tok(512), tok(512),
                   pl.BlockSpec((16, TBP), lambda i: (0, i)), cache_spec, cache_spec],
        out_shape=[sd(512, BF16), sd(128, F32), sd(128, F32), sd(512, BF16), sd(512, BF16),
                   sd(512, BF16), sd(512, F32), jax.ShapeDtypeStruct((16, T_ALL), F32),
                   cache_shape, cache_shape],
        scratch_shapes=[pltpu.VMEM((D_MODEL, P_MAIN), BF16)],
        compiler_params=_cparams(("arbitrary",)),
        name="inproj",
    )(x_ctx, x_lat, mod3, norm1_w, w_main, w_gates_t, qw, kw, *inds, *rope_tabs)


def _rope_tables():
    t = jnp.arange(DEC_SEQ)
    pos = jnp.stack([t // GRID_W, t % GRID_W], axis=1).astype(F32)
    n_freq = DH_ATT // 4
    inv_freq = ROPE_THETA ** (-jnp.arange(n_freq, dtype=F32) / n_freq)
    ang = pos[:, :, None] * inv_freq
    cos, sin = jnp.cos(ang), jnp.sin(ang)
    cos_h = jnp.stack([cos, cos], axis=2).reshape(DEC_SEQ, DH_ATT)
    sin_h = jnp.stack([-sin, sin], axis=2).reshape(DEC_SEQ, DH_ATT)
    return jnp.tile(cos_h, (1, 2)), jnp.tile(sin_h, (1, 2))


def _attn_kernel(*refs, has_cache):
    if has_cache:
        q_ref, k_ref, v_ref, kc_ref, vc_ref, o_ref = refs
    else:
        q_ref, k_ref, v_ref, o_ref = refs
    q = q_ref[...] * jnp.asarray(DH_ATT ** -0.5, BF16)
    k = k_ref[...].astype(BF16)
    v = v_ref[...].astype(BF16)
    qb = q.shape[0]
    low_half = lax.broadcasted_iota(jnp.int32, (1, 128), 1) < DH_ATT
    for g in range(N_KV_HEADS):
        kg = k[:, DH_ATT * g:DH_ATT * (g + 1)]
        vg = v[:, DH_ATT * g:DH_ATT * (g + 1)]
        if has_cache:
            kg = jnp.concatenate([kg, kc_ref[g].astype(BF16)], axis=0)
            vg = jnp.concatenate([vg, vc_ref[g].astype(BF16)], axis=0)
        n_keys = kg.shape[0]
        zero = jnp.zeros_like(kg)
        k2 = jnp.concatenate([jnp.concatenate([kg, zero], axis=1), jnp.concatenate([zero, kg], axis=1)], axis=0)
        v2 = jnp.concatenate([jnp.concatenate([vg, zero], axis=1), jnp.concatenate([zero, vg], axis=1)], axis=0)
        ones2 = jnp.concatenate([jnp.broadcast_to(jnp.where(low_half, 1.0, 0.0), (n_keys, 128)),
                                 jnp.broadcast_to(jnp.where(low_half, 0.0, 1.0), (n_keys, 128))],
                                axis=0).astype(BF16)
        qp = jnp.concatenate([q[:, 256 * g:256 * g + 128], q[:, 256 * g + 128:256 * g + 256]], axis=0)
        s = lax.dot_general(qp, k2, _NT, preferred_element_type=F32)
        halves = [s[:, 0:n_keys], s[:, n_keys:]]
        p = jnp.concatenate([jnp.exp(sh - jnp.max(sh, axis=-1, keepdims=True)) for sh in halves],
                            axis=1).astype(BF16)
        ov = _dot(p, jnp.concatenate([v2, ones2], axis=1))
        o = ov[:, 0:128] / ov[:, 128:]
        o_ref[:, 256 * g:256 * g + 128] = o[0:qb, :].astype(BF16)
        o_ref[:, 256 * g + 128:256 * g + 256] = o[qb:, :].astype(BF16)


def _attention(qn, kn, v_all, cache, *, n_batch, seq, first_row):
    qblocks = seq // TB
    kv_spec = pl.BlockSpec((seq, 128), lambda b, i: (b + first_row // seq, 0))
    in_specs = [pl.BlockSpec((TB, 512), lambda b, i: (first_row // TB + b * qblocks + i, 0)), kv_spec, kv_spec]
    args = [qn, kn, v_all]
    if cache is not None:
        cspec = pl.BlockSpec((None, None, N_KV_HEADS, PAST_LEN, DH_ATT), lambda b, i: (b, 0, 0, 0, 0))
        in_specs += [cspec, cspec]
        args += list(cache)
    return pl.pallas_call(
        functools.partial(_attn_kernel, has_cache=cache is not None),
        grid=(n_batch, qblocks),
        in_specs=in_specs,
        out_specs=pl.BlockSpec((TB, 512), lambda b, i: (b * qblocks + i, 0)),
        out_shape=jax.ShapeDtypeStruct((n_batch * seq, 512), BF16),
        compiler_params=_cparams(("arbitrary", "arbitrary")),
        name="attention_lat" if cache is not None else "attention_ctx",
    )(*args)


def _log_sigmoid(x):
    return jnp.minimum(x, 0.0) - jnp.log1p(jnp.exp(-jnp.abs(x)))


def _col_bcast(cols, j):
    return jnp.broadcast_to(cols[:, j:j + 1], (CHUNK, CHUNK))


def _mlstm_kernel(*refs, has_state, n_chunks):
    n_in = 10 if has_state else 7
    n_out = 1 if has_state else 4
    q_ref, k_ref, v_ref, g_ref, bg_ref, om_ref, nw_ref = refs[:7]
    hm_ref = refs[n_in]
    scratch = refs[n_in + n_out:]
    st = scratch[0:8]
    ms = scratch[8:16]
    hdir = scratch[16:18]
    rows_sc, cmb_sc, bb_sc, kt_sc = scratch[18:22]
    neg_inf = F32(-jnp.inf)
    zeros112 = jnp.zeros((CHUNK - 16, CHUNK), F32)
    sub = lax.broadcasted_iota(jnp.int32, (CHUNK, CHUNK), 0)
    lan = lax.broadcasted_iota(jnp.int32, (CHUNK, CHUNK), 1)
    ones = jnp.ones((CHUNK, CHUNK), BF16)

    stats = []
    for c in range(n_chunks):
        pre = g_ref[c] + bg_ref[...]
        logf = _log_sigmoid(pre)
        for d in range(2):
            bcum = _dot3(logf, ((sub <= lan) if d == 0 else (sub >= lan)).astype(BF16))
            li4 = pre[4 * d:4 * d + 4, :]
            lf4 = logf[8 + 4 * d:12 + 4 * d, :]
            b4 = bcum[8 + 4 * d:12 + 4 * d, :]
            stats.append((c, d, li4 - b4, lf4, b4))
    for c, d, r4, lf4, b4 in stats:
        blast4 = b4[:, CHUNK - 1:CHUNK] if d == 0 else b4[:, 0:1]
        wlog4 = blast4 + r4
        wmax4 = jnp.max(wlog4, axis=-1, keepdims=True)
        full = lambda a: jnp.broadcast_to(a, (4, CHUNK))
        rows_sc[2 * c + d] = jnp.concatenate(
            [r4, wlog4, full(blast4), full(wmax4), jnp.zeros((16, CHUNK), F32)], axis=0)
    for c in range(n_chunks):
        for h in range(N_HEADS_M):
            kc = k_ref[CHUNK * c:CHUNK * (c + 1), DH_M * h:DH_M * (h + 1)]
            kt_sc[c * N_HEADS_M + h] = kc.astype(F32).T.astype(BF16)
    for c, d, r4, lf4, b4 in stats:
        within = (lan <= sub) if d == 0 else (lan >= sub)
        for h in range(N_HEADS_M):
            cummax = jnp.max(jnp.where(within, r4[h:h + 1, :], neg_inf), axis=-1, keepdims=True)
            cmb_sc[(2 * c + d) * N_HEADS_M + h] = jnp.broadcast_to(cummax, (CHUNK, CHUNK))
    for c, d, r4, lf4, b4 in stats:
        within = (lan <= sub) if d == 0 else (lan >= sub)
        for h in range(N_HEADS_M):
            bb_sc[(2 * c + d) * N_HEADS_M + h] = _dot2(jnp.where(within, lf4[h:h + 1, :], 0.0), ones)

    for d in range(2):
        if has_state:
            c0_ref, n0_ref, m0_ref = refs[7:10]
            ncols = jnp.concatenate([n0_ref[d], jnp.zeros((12, DH_M), F32), zeros112], axis=0).T
        for h in range(N_HEADS_M):
            idx = 4 * d + h
            if has_state:
                st[idx][:, 0:DH_M] = c0_ref[d, h]
                st[idx][:, DH_M:] = _col_bcast(ncols, h)
                ms[idx][0:1, :] = m0_ref[idx:idx + 1, :]
            else:
                st[idx][...] = jnp.zeros((DH_M, 2 * DH_M), F32)
                ms[idx][0:1, :] = jnp.full((1, CHUNK), NEG_INIT, F32)

    def chunk_step(i, carry):
        for d in range(2):
            c = i if d == 0 else n_chunks - 1 - i
            r0 = pl.multiple_of(c * CHUNK, CHUNK)
            rows = rows_sc[2 * c + d]
            mask = (lan <= sub) if d == 0 else (lan >= sub)
            heads = []
            for h in range(N_HEADS_M):
                idx = 4 * d + h
                hs = slice(DH_M * h, DH_M * (h + 1))
                qc = q_ref[pl.ds(r0, CHUNK), hs]
                kt = kt_sc[c * N_HEADS_M + h]
                state = st[idx][...]
                heads.append(dict(idx=idx, hs=hs, qc=qc, kt=kt, state=state,
                                  s_raw=_dot(qc, kt), qs=_dot(qc, state.astype(BF16))))
            for h, hd in enumerate(heads):
                mp = ms[hd["idx"]][0:1, :]
                m_b = jnp.maximum(mp, cmb_sc[(2 * c + d) * N_HEADS_M + h])
                s = hd["s_raw"] * jnp.exp(jnp.where(mask, rows[h:h + 1, :] - m_b, neg_inf))
                s_hi = s.astype(BF16)
                vc = v_ref[pl.ds(r0, CHUNK), hd["hs"]]
                v_ones = jnp.concatenate([vc, ones], axis=1)
                hd.update(mp=mp, m_b=m_b, v_ones=v_ones, sv=_dot(s_hi, v_ones),
                          s_lo_sum=_dot((s - s_hi.astype(F32)).astype(BF16), ones))
            for h, hd in enumerate(heads):
                mp, m_b = hd["mp"], hd["m_b"]
                gw_b = jnp.exp(mp - m_b)
                en_b = jnp.exp(-(bb_sc[(2 * c + d) * N_HEADS_M + h] + m_b))
                num = hd["sv"][:, 0:DH_M] + gw_b * hd["qs"][:, 0:DH_M]
                den = hd["sv"][:, DH_M:] + hd["s_lo_sum"] + gw_b * hd["qs"][:, DH_M:]
                hdir[d][pl.ds(r0, CHUNK), hd["hs"]] = num / jnp.maximum(jnp.abs(den), en_b)
            for h, hd in enumerate(heads):
                mp = hd["mp"]
                gend = rows[8 + h:9 + h, :] + mp
                mnew = jnp.maximum(gend, rows[12 + h:13 + h, :])
                w_row = jnp.exp(rows[4 + h:5 + h, :] - mnew)
                kw_t = (hd["kt"].astype(F32) * w_row).astype(BF16)
                dec = jnp.exp(gend - mnew)
                st[hd["idx"]][...] = (jnp.concatenate([dec, dec], axis=1) * hd["state"]
                                      + _dot(kw_t, hd["v_ones"]))
                ms[hd["idx"]][0:1, :] = mnew
        return carry

    lax.fori_loop(0, n_chunks, chunk_step, 0)

    for h in range(N_HEADS_M):
        hs = slice(DH_M * h, DH_M * (h + 1))
        hh = hdir[0][:, hs] + hdir[1][:, hs]
        y = hh * lax.rsqrt(jnp.mean(hh * hh, axis=-1, keepdims=True) + EPS) * nw_ref[:, hs]
        hm_ref[:, hs] = (jax.nn.sigmoid(om_ref[:, hs]) * y).astype(BF16)
    if not has_state:
        c_ref, n_ref, m_ref = refs[n_in + 1:n_in + 4]
        for d in range(2):
            for h in range(N_HEADS_M):
                idx = 4 * d + h
                c_ref[d, h] = st[idx][:, 0:DH_M]
                n_ref[d, h:h + 1, :] = st[idx][:, DH_M:].T[0:1, :]
                m_ref[idx:idx + 1, :] = ms[idx][0:1, :]


def _mlstm(qm, km, vm, g3, bg, om, nw, state, *, n_batch, seq, first_block):
    n_chunks = seq // CHUNK
    rb = seq // TB
    tok = lambda: pl.BlockSpec((seq, 512), lambda b: (b + first_block // rb, 0))
    full = lambda a: pl.BlockSpec(a.shape, lambda b: (0,) * a.ndim)
    in_specs = [tok(), tok(), tok(),
                pl.BlockSpec((n_chunks, 16, CHUNK), lambda b: (b + first_block // rb, 0, 0)),
                full(bg), tok(), full(nw)]
    args = [qm, km, vm, g3, bg, om, nw]
    hm_spec = pl.BlockSpec((seq, 512), lambda b: (b, 0))
    hm_shape = jax.ShapeDtypeStruct((n_batch * seq, 512), BF16)
    scratch = ([pltpu.VMEM((DH_M, 2 * DH_M), F32)] * 8 + [pltpu.VMEM((8, CHUNK), F32)] * 8
               + [pltpu.VMEM((seq, 512), F32)] * 2
               + [pltpu.VMEM((2 * n_chunks, 32, CHUNK), F32),
                  pltpu.VMEM((8 * n_chunks, CHUNK, CHUNK), F32),
                  pltpu.VMEM((8 * n_chunks, CHUNK, CHUNK), F32),
                  pltpu.VMEM((N_HEADS_M * n_chunks, DH_M, CHUNK), BF16)])
    if state is not None:
        c0, n0, m0 = state
        in_specs += [pl.BlockSpec((None, None, 2, N_HEADS_M, DH_M, DH_M), lambda b: (b, 0, 0, 0, 0, 0)),
                     pl.BlockSpec((None, None, 2, N_HEADS_M, DH_M), lambda b: (b, 0, 0, 0, 0)),
                     pl.BlockSpec((None, 8, CHUNK), lambda b: (b, 0, 0))]
        args += [c0, n0, m0]
        out_specs, out_shape = hm_spec, hm_shape
    else:
        out_specs = [hm_spec,
                     pl.BlockSpec((None, 2, N_HEADS_M, DH_M, DH_M), lambda b: (b, 0, 0, 0, 0)),
                     pl.BlockSpec((None, 2, N_HEADS_M, DH_M), lambda b: (b, 0, 0, 0)),
                     pl.BlockSpec((None, 8, CHUNK), lambda b: (b, 0, 0))]
        out_shape = [hm_shape,
                     jax.ShapeDtypeStruct((n_batch, 2, N_HEADS_M, DH_M, DH_M), F32),
                     jax.ShapeDtypeStruct((n_batch, 2, N_HEADS_M, DH_M), F32),
                     jax.ShapeDtypeStruct((n_batch, 8, CHUNK), F32)]
    return pl.pallas_call(
        functools.partial(_mlstm_kernel, has_state=state is not None, n_chunks=n_chunks),
        grid=(n_batch,),
        in_specs=in_specs,
        out_specs=out_specs,
        out_shape=out_shape,
        scratch_shapes=scratch,
        compiler_params=_cparams(("arbitrary",)),
        name="mlstm_lat" if state is not None else "mlstm_ctx",
    )(*args)


def _outproj_kernel(attc_ref, attl_ref, hmc_ref, hml_ref, xc_ref, xl_ref, mod_ref, nw_ref, wo_ref,
                    rwh_ref, rwl_ref, wsg_ref, wsu_ref, wsd_ref, base_ref, hn_ref, lt_ref):
    y = (_dot(_ctx_or_lat(attc_ref, attl_ref), wo_ref[0:D_ATT, :])
         + _dot(_ctx_or_lat(hmc_ref, hml_ref), wo_ref[D_ATT:, :]))
    x1 = _ctx_or_lat(xc_ref, xl_ref) + mod_ref[2:3, :] * y
    z = x1 * lax.rsqrt(jnp.mean(x1 * x1, axis=-1, keepdims=True) + EPS) * nw_ref[...]
    hn = z * (1.0 + mod_ref[4:5, :]) + mod_ref[3:4, :]
    hb = hn.astype(BF16)
    hl = (hn - hb.astype(F32)).astype(BF16)
    nt = lambda w, t: lax.dot_general(w, t, _NT, preferred_element_type=F32)
    lt_ref[...] = nt(rwh_ref[...], hb) + nt(rwl_ref[...], hb) + nt(rwh_ref[...], hl)
    a = _silu(_dot(hb, wsg_ref[...])) * _dot(hb, wsu_ref[...])
    shared = _dot(a.astype(BF16), wsd_ref[...])
    base_ref[...] = x1 + mod_ref[5:6, :] * shared
    packed = pltpu.pack_elementwise([hn[:, :512], hn[:, 512:]], packed_dtype=BF16)
    for c in range(4):
        hn_ref[pl.ds(c, TBP, stride=4), :] = packed[:, 128 * c:128 * (c + 1)]


def _outproj(att, hm, x, mod3, norm2_w, w_out, rw_hi, rw_lo, wsg, wsu, wsd):
    tok = lambda w: pl.BlockSpec((TBP, w), lambda i: (i, 0))
    full = lambda a: pl.BlockSpec(a.shape, lambda i: (0,) * a.ndim)
    return pl.pallas_call(
        _outproj_kernel,
        grid=(T_ALL // TBP,),
        in_specs=_split_specs(512) + _split_specs(512) + _split_specs(D_MODEL) + [
                  pl.BlockSpec((None, 6, D_MODEL), lambda i: (_mod_row(i), 0, 0)),
                  full(norm2_w), full(w_out), full(rw_hi), full(rw_lo), full(wsg), full(wsu), full(wsd)],
        out_specs=[tok(D_MODEL), pl.BlockSpec((4 * TBP, 128), lambda i: (i, 0)),
                   pl.BlockSpec((N_EXPERTS, TBP), lambda i: (0, i))],
        out_shape=[jax.ShapeDtypeStruct((T_ALL, D_MODEL), F32),
                   jax.ShapeDtypeStruct((4 * T_ALL, 128), jnp.uint32),
                   jax.ShapeDtypeStruct((N_EXPERTS, T_ALL), F32)],
        compiler_params=_cparams(("arbitrary",)),
        name="outproj",
    )(*att, *hm, *x, mod3, norm2_w, w_out, rw_hi, rw_lo, wsg, wsu, wsd)


def _first_max(vals, ids, limit):
    m = functools.reduce(jnp.maximum, [jnp.max(v, axis=0, keepdims=True) for v in vals])
    cand = [jnp.min(jnp.where(v == m, i, limit), axis=0, keepdims=True) for v, i in zip(vals, ids)]
    return m, functools.reduce(jnp.minimum, cand)


def _router_kernel(lt_ref, bias_ref, gate_ref, idx_ref):
    n = lt_ref.shape[1]
    score = jax.nn.sigmoid(lt_ref[...])
    biased = score + bias_ref[...]
    sub = lax.broadcasted_iota(jnp.int32, (8, n), 0).astype(F32)
    neg_inf = F32(-jnp.inf)
    slabs = [biased[8 * g:8 * (g + 1), :] for g in range(N_GROUPS)]
    gs = []
    for sl in slabs:
        m1, i1 = _first_max([sl], [sub], 8.0)
        m2 = jnp.max(jnp.where(sub == i1, neg_inf, sl), axis=0, keepdims=True)
        gs.append(m1 + m2)
    cur = jnp.concatenate(gs, axis=0)
    gsel = jnp.zeros((8, n), F32)
    for _ in range(TOPK_GROUPS):
        _, i = _first_max([cur], [sub], 8.0)
        hit = sub == i
        gsel = jnp.where(hit, 1.0, gsel)
        cur = jnp.where(hit, neg_inf, cur)
    vals = [jnp.where(gsel[g:g + 1, :] > 0.0, slabs[g], neg_inf) for g in range(N_GROUPS)]
    ids = [sub + 8.0 * g for g in range(N_GROUPS)]
    picked = [jnp.zeros((8, n), F32) for _ in range(N_GROUPS)]
    order = []
    for _ in range(TOP_K):
        _, i = _first_max(vals, ids, float(N_EXPERTS))
        order.append(i)
        hits = [idg == i for idg in ids]
        picked = [jnp.where(hh, score[8 * g:8 * (g + 1), :], p) for g, (p, hh) in enumerate(zip(picked, hits))]
        vals = [jnp.where(hh, neg_inf, v) for v, hh in zip(vals, hits)]
    total = functools.reduce(jnp.add, [jnp.sum(p, axis=0, keepdims=True) for p in picked])
    gate_t = jnp.concatenate([p / total * ROUTED_SCALE for p in picked]
                             + [jnp.zeros((128 - N_EXPERTS, n), F32)], axis=0)
    gate_ref[...] = gate_t.T
    idx_ref[...] = jnp.concatenate(order, axis=0).astype(jnp.int32)


def _router(logits_t, bias_col):
    return pl.pallas_call(
        _router_kernel,
        grid=(T_ALL // TB_MOE,),
        in_specs=[pl.BlockSpec((N_EXPERTS, TB_MOE), lambda i: (0, i)),
                  pl.BlockSpec((N_EXPERTS, 1), lambda i: (0, 0))],
        out_specs=[pl.BlockSpec((TB_MOE, 128), lambda i: (i, 0)),
                   pl.BlockSpec((TOP_K, TB_MOE), lambda i: (0, i))],
        out_shape=[jax.ShapeDtypeStruct((T_ALL, 128), F32),
                   jax.ShapeDtypeStruct((TOP_K, T_ALL), jnp.int32)],
        compiler_params=_cparams(("arbitrary",)),
        name="router",
    )(logits_t, bias_col)


def _plan_kernel(idx_ref, slot_ref, cnt_ref, pos_sc):
    n_tiles = T_HALF // 128
    eid = lax.broadcasted_iota(jnp.int32, (N_EXPERTS, 128), 0)
    tri = (lax.broadcasted_iota(jnp.int32, (128, 128), 0)
           <= lax.broadcasted_iota(jnp.int32, (128, 128), 1)).astype(BF16)
    carry = jnp.zeros((N_EXPERTS, 1), F32)
    for j in range(n_tiles):
        it = idx_ref[:, 128 * j:128 * (j + 1)]
        sel = jnp.zeros((N_EXPERTS, 128), F32)
        for k in range(TOP_K):
            sel = jnp.where(it[k:k + 1, :] == eid, 1.0, sel)
        inc = _dot(sel.astype(BF16), tri) + carry
        carry = inc[:, 127:128]
        pos_sc[:, 128 * j:128 * (j + 1)] = inc - 1.0
    count = jnp.broadcast_to(carry, (N_EXPERTS, 128))
    padded = jnp.floor((count + (MT - 1.0)) * (1.0 / MT)) * MT
    before = (lax.broadcasted_iota(jnp.int32, (N_EXPERTS, N_EXPERTS), 1)
              < lax.broadcasted_iota(jnp.int32, (N_EXPERTS, N_EXPERTS), 0)).astype(BF16)
    hi, mid, lo = _split3(padded)
    off = _dot(before, hi) + _dot(before, mid) + _dot(before, lo)
    cnt_ref[...] = count.astype(jnp.int32)

    def slots(j):
        it = idx_ref[:, 128 * j:128 * (j + 1)]
        val = off + pos_sc[:, 128 * j:128 * (j + 1)]
        rows = [jnp.sum(jnp.where(it[k:k + 1, :] == eid, val, 0.0), axis=0, keepdims=True)
                for k in range(TOP_K)]
        return jnp.concatenate(rows, axis=0).astype(jnp.int32)

    for j in range(n_tiles):
        s = slots(j)
        slot_ref[:, 128 * j:128 * (j + 1)] = s[0:TOP_K // 2, :] | (s[TOP_K // 2:, :] << 16)


def _plan(idx8):
    return pl.pallas_call(
        _plan_kernel,
        grid=(2,),
        in_specs=[pl.BlockSpec((TOP_K, T_HALF), lambda h: (0, h))],
        out_specs=[pl.BlockSpec((None, TOP_K // 2, T_HALF), lambda h: (h, 0, 0)),
                   pl.BlockSpec((None, N_EXPERTS, 128), lambda h: (h, 0, 0))],
        out_shape=[jax.ShapeDtypeStruct((2, TOP_K // 2, T_HALF), jnp.int32),
                   jax.ShapeDtypeStruct((2, N_EXPERTS, 128), jnp.int32)],
        scratch_shapes=[pltpu.VMEM((N_EXPERTS, T_HALF), F32)],
        compiler_params=_cparams(("arbitrary",)),
        name="plan",
    )(idx8)


def _prepare_half(cnt_ref, tile_e, tile_first, elist, h):
    def per_expert(x, carry):
        j, q = carry
        n = cnt_ref[h * N_EXPERTS + x]
        tiles = (n + MT - 1) // MT
        elist[q] = x

        def mark(i, c):
            tile_e[j + i] = x
            tile_first[j + i] = jnp.where(i == 0, 1, 0)
            return c

        lax.fori_loop(0, tiles, mark, 0)

        return j + tiles, q + jnp.where(tiles > 0, 1, 0)

    n_tiles, n_live = lax.fori_loop(0, N_EXPERTS, per_expert, (0, 0))
    for extra in range(2):
        tile_e[n_tiles + extra] = 0
        tile_first[n_tiles + extra] = 0
    return n_tiles, n_live


def _invert_slots(slot_ref, tok_ref, acc_v, h):
    rounds = TOP_K // 2
    trips = T_HALF // INV_UNROLL
    zero_rows = 8 * T_HALF // (rounds * trips)
    assert zero_rows * rounds * trips == 8 * T_HALF and zero_rows % 8 == 0
    for k in range(rounds):
        def body(i, carry, k=k):
            words = [slot_ref[(h * rounds + k) * T_HALF + i * INV_UNROLL + u] for u in range(INV_UNROLL)]
            for u in range(INV_UNROLL):
                t = i * INV_UNROLL + u
                tok_ref[words[u] & 0xFFFF] = t
                tok_ref[lax.shift_right_logical(words[u], 16)] = t
            row = pl.multiple_of((k * trips + i) * zero_rows, 8)
            acc_v[pl.ds(row, zero_rows), :] = jnp.zeros((zero_rows, 128), F32)
            return carry

        lax.fori_loop(0, trips, body, 0)
    acc_v[pl.ds(8 * T_HALF, 8), :] = jnp.zeros((8, 128), F32)


def _moe_kernel(slot_ref, cnt_ref, src_hbm, gate_hbm, wg_hbm, wu_hbm, wd_hbm, base_hbm, mod_ref,
                outc_hbm, outl_hbm,
                src_v, gate_v, acc_v, wg_l, wu_l, wd_l, wgu_b, wd_b, xbuf0, xbuf1, gbuf0, gbuf1,
                ybuf0, ybuf1, tok_ref, tile_e, tile_first, elist, live, base_buf, out_buf, pad_v,
                sem, wsem, fsem):
    h = pl.program_id(0)
    xbuf, gbuf, ybuf = (xbuf0, xbuf1), (gbuf0, gbuf1), (ybuf0, ybuf1)

    src_cp = pltpu.make_async_copy(src_hbm.at[pl.ds(pl.multiple_of(h * (4 * T_HALF), 8), 4 * T_HALF)],
                                   src_v.at[pl.ds(0, 4 * T_HALF)], sem.at[0])
    gate_cp = pltpu.make_async_copy(gate_hbm.at[pl.ds(pl.multiple_of(h * T_HALF, 8), T_HALF)],
                                    gate_v.at[pl.ds(0, T_HALF)], sem.at[1])
    src_cp.start()
    gate_cp.start()
    pad_v[...] = jnp.full(pad_v.shape, T_HALF, jnp.int32)
    pad_cp = pltpu.make_async_copy(pad_v, tok_ref, sem.at[2])
    pad_cp.start()

    def weight_copies(x, slot):
        return [pltpu.make_async_copy(w_hbm.at[x], w_l.at[slot], wsem.at[slot, i])
                for i, (w_hbm, w_l) in enumerate(((wg_hbm, wg_l), (wu_hbm, wu_l), (wd_hbm, wd_l)))]

    n_tiles, n_live = _prepare_half(cnt_ref, tile_e, tile_first, elist, h)
    live[0] = 0
    live[1] = n_live

    for ahead in range(W_SLOTS - 1):
        @pl.when(ahead < n_live)
        def _(ahead=ahead):
            for cp in weight_copies(elist[ahead], ahead):
                cp.start()

    zero = jnp.zeros((8, 128), F32)
    src_v[pl.ds(4 * T_HALF, 8), :] = pltpu.pack_elementwise([zero, zero], packed_dtype=BF16)
    gate_v[pl.ds(T_HALF, 8), :] = zero
    ybuf0[...] = jnp.zeros(ybuf0.shape, F32)
    ybuf1[...] = jnp.zeros(ybuf1.shape, F32)
    pad_cp.wait()
    _invert_slots(slot_ref, tok_ref, acc_v, h)
    src_cp.wait()
    gate_cp.wait()

    def switch_expert():
        q = live[0]
        slot = q % W_SLOTS
        for cp in weight_copies(elist[q], slot):
            cp.wait()
        wgu_b[:, 0:D_EXPERT] = wg_l[slot].astype(BF16)
        wgu_b[:, D_EXPERT:] = wu_l[slot].astype(BF16)
        wd_b[...] = wd_l[slot].astype(BF16)
        live[0] = q + 1
        nxt = q + W_SLOTS - 1

        @pl.when(nxt < live[1])
        def _():
            for cp in weight_copies(elist[nxt], nxt % W_SLOTS):
                cp.start()

    def gather(j, xb, gb, rows=(0, MT)):
        base = j * MT
        for m in range(*rows):
            t = tok_ref[base + m]
            xb[pl.ds(m, 4, stride=MT_STRIDE), :] = src_v[pl.ds(pl.multiple_of(t * 4, 4), 4), :]
            gb[m:m + 1, :] = gate_v[pl.ds(t, 1), :]

    def scatter(j, yb, rows=(0, MT)):
        base = j * MT
        for b in range(rows[0] // RMW_BATCH, rows[1] // RMW_BATCH):
            ms = [b * RMW_BATCH + u for u in range(RMW_BATCH)]
            targets = [pl.ds(pl.multiple_of(tok_ref[base + m] * 8, 8), 8) for m in ms]
            vals = [acc_v[r, :] + yb[pl.ds(m, 8, stride=MT_STRIDE), :] for r, m in zip(targets, ms)]
            for r, v in zip(targets, vals):
                acc_v[r, :] = v

    quarters = [(q * MT // 4, (q + 1) * MT // 4) for q in range(4)]

    def step(j, p):
        pl.when(tile_first[j] == 1)(switch_expert)
        xb, gb, yb = xbuf[p], gbuf[p], ybuf[p]
        nxt = (j + 1, xbuf[1 - p], gbuf[1 - p])
        prv = (jnp.maximum(j - 1, 0), ybuf[1 - p])
        lo, hi = [], []
        for c in range(4):
            words = xb[MT_STRIDE * c:MT_STRIDE * c + MT, :]
            unpack = functools.partial(pltpu.unpack_elementwise, words, packed_dtype=BF16, unpacked_dtype=F32)
            lo.append(unpack(index=0).astype(BF16))
            hi.append(unpack(index=1).astype(BF16))
        x = jnp.concatenate(lo + hi, axis=1)
        gather(*nxt, rows=quarters[0])
        h_gate = _dot(x, wgu_b[:, 0:D_EXPERT])
        gather(*nxt, rows=quarters[1])
        h_up = _dot(x, wgu_b[:, D_EXPERT:])
        gather(*nxt, rows=quarters[2])
        g = gb[...]
        g_hi = g.astype(BF16)
        g_lo = (g - g_hi.astype(F32)).astype(BF16)
        pick = (lax.broadcasted_iota(jnp.int32, (128, D_EXPERT), 0) == tile_e[j]).astype(BF16)
        g_col = _dot(g_hi, pick) + _dot(g_lo, pick)
        a = (_silu(h_gate) * h_up * g_col).astype(BF16)
        gather(*nxt, rows=quarters[3])
        scatter(*prv, rows=quarters[0])
        y_lo = _dot(a, wd_b[:, 0:D_MODEL // 2])
        scatter(*prv, rows=quarters[1])
        y_hi = _dot(a, wd_b[:, D_MODEL // 2:])
        scatter(*prv, rows=quarters[2])
        for c in range(4):
            yb[MT_STRIDE * c:MT_STRIDE * c + MT, :] = y_lo[:, 128 * c:128 * (c + 1)]
            yb[MT_STRIDE * (c + 4):MT_STRIDE * (c + 4) + MT, :] = y_hi[:, 128 * c:128 * (c + 1)]
        scatter(*prv, rows=quarters[3])

    gather(0, xbuf[0], gbuf[0])
    n_pairs = (n_tiles + 1) // 2

    def pair(i, carry):
        step(2 * i, 0)
        step(2 * i + 1, 1)
        return carry

    lax.fori_loop(0, n_pairs, pair, 0)
    scatter(jnp.maximum(2 * n_pairs - 1, 0), ybuf[1])

    n_fin = T_HALF // FIN

    def first_token(c):
        return h * T_HALF + c * FIN

    def base_copy(c, slot):
        return pltpu.make_async_copy(base_hbm.at[pl.ds(pl.multiple_of(first_token(c), FIN), FIN)],
                                     base_buf.at[slot], fsem.at[0, slot])

    def on_out_copy(c, action):
        t0 = first_token(c)
        slot = c % FIN_OUT_SLOTS

        @pl.when(t0 < T_CTX)
        def _():
            action(pltpu.make_async_copy(out_buf.at[slot], outc_hbm.at[pl.ds(pl.multiple_of(t0, FIN), FIN)],
                                         fsem.at[1, slot]))

        @pl.when(t0 >= T_CTX)
        def _():
            action(pltpu.make_async_copy(out_buf.at[slot],
                                         outl_hbm.at[pl.ds(pl.multiple_of(t0 - T_CTX, FIN), FIN)],
                                         fsem.at[1, slot]))

    for c in range(FIN_BASE_SLOTS - 1):
        base_copy(c, c).start()

    def combine(c, carry):
        slot = c % FIN_BASE_SLOTS
        oslot = c % FIN_OUT_SLOTS
        base_copy(c, slot).wait()
        ahead = c + FIN_BASE_SLOTS - 1

        @pl.when(ahead < n_fin)
        def _():
            base_copy(ahead, ahead % FIN_BASE_SLOTS).start()

        @pl.when(c >= FIN_OUT_SLOTS)
        def _():
            on_out_copy(c - FIN_OUT_SLOTS, lambda cp: cp.wait())

        t0 = first_token(c)
        mod_row = jnp.where(t0 < T_CTX, 0, 1 + (t0 - T_CTX) // DEC_SEQ)
        acc_row = pl.multiple_of(8 * FIN * c, 8)
        for cc in range(8):
            cs = slice(128 * cc, 128 * (cc + 1))
            routed = acc_v[pl.ds(acc_row + cc, FIN, stride=8), :]
            out_buf[oslot, :, cs] = base_buf[slot, :, cs] + mod_ref[mod_row, 5:6, cs] * routed
        on_out_copy(c, lambda cp: cp.start())
        return carry

    lax.fori_loop(0, n_fin, combine, 0)
    for c in range(n_fin - FIN_OUT_SLOTS, n_fin):
        on_out_copy(c, lambda cp: cp.wait())


def _moe(slot_words, cnt, src, gate, wg, wu, wd, base, mod3):
    any_spec = pl.BlockSpec(memory_space=pl.ANY)
    tile_buf = lambda rows, dt: pltpu.VMEM((rows * MT_STRIDE, 128), dt)
    return pl.pallas_call(
        _moe_kernel,
        grid_spec=pltpu.PrefetchScalarGridSpec(
            num_scalar_prefetch=2,
            grid=(2,),
            in_specs=[any_spec] * 6 + [pl.BlockSpec(mod3.shape, lambda h, *_: (0, 0, 0))],
            out_specs=[any_spec, any_spec],
            scratch_shapes=[pltpu.VMEM((4 * T_HALF + 8, 128), jnp.uint32),
                            pltpu.VMEM((T_HALF + 8, 128), F32),
                            pltpu.VMEM((8 * T_HALF + 8, 128), F32),
                            pltpu.VMEM((W_SLOTS, D_MODEL, D_EXPERT), F32),
                            pltpu.VMEM((W_SLOTS, D_MODEL, D_EXPERT), F32),
                            pltpu.VMEM((W_SLOTS, D_EXPERT, D_MODEL), F32),
                            pltpu.VMEM((D_MODEL, 2 * D_EXPERT), BF16),
                            pltpu.VMEM((D_EXPERT, D_MODEL), BF16),
                            tile_buf(4, jnp.uint32), tile_buf(4, jnp.uint32),
                            pltpu.VMEM((MT, 128), F32), pltpu.VMEM((MT, 128), F32),
                            tile_buf(8, F32), tile_buf(8, F32),
                            pltpu.SMEM((SLOT_CAP,), jnp.int32),
                            pltpu.SMEM((NT_MAX + 2,), jnp.int32),
                            pltpu.SMEM((NT_MAX + 2,), jnp.int32),
                            pltpu.SMEM((N_EXPERTS,), jnp.int32),
                            pltpu.SMEM((2,), jnp.int32),
                            pltpu.VMEM((FIN_BASE_SLOTS, FIN, D_MODEL), F32),
                            pltpu.VMEM((FIN_OUT_SLOTS, FIN, D_MODEL), F32),
                            pltpu.VMEM((SLOT_CAP,), jnp.int32),
                            pltpu.SemaphoreType.DMA((3,)),
                            pltpu.SemaphoreType.DMA((W_SLOTS, 3)),
                            pltpu.SemaphoreType.DMA((2, FIN_BASE_SLOTS))]),
        out_shape=[jax.ShapeDtypeStruct((T_CTX, D_MODEL), F32), jax.ShapeDtypeStruct((T_LAT, D_MODEL), F32)],
        compiler_params=_cparams(("arbitrary",)),
        name="moe",
    )(slot_words, cnt, src, gate, wg, wu, wd, base, mod3)


def _head_indicators(width):
    head = jnp.arange(width) // DH_ATT
    ind = (head[:, None] == jnp.arange(128)[None, :]).astype(BF16)
    return ind, ind.T


def kernel(x_prompt, x_sample, cache_attn_k, cache_attn_v, state_mlstm_c, state_mlstm_n, state_mlstm_m, c, c_ctx, w_mod, b_mod, norm1_w, norm2_w, w_in, q_norm_w, k_norm_w, b_gates, m_norm_w, w_out, router_w, router_bias, w_gate, w_up, w_down, ws_gate, ws_up, ws_down):
    x = (x_prompt.reshape(T_CTX, D_MODEL), x_sample.reshape(T_LAT, D_MODEL))
    mod3 = _modulation(c, c_ctx, w_mod[0], b_mod[0])

    w_main = w_in[0]
    w_gates_t = w_in[0, :, P_MAIN:].T.astype(BF16)
    qw = jnp.tile(q_norm_w, (1, N_HEADS_ATT))
    kw = jnp.tile(k_norm_w, (1, N_KV_HEADS))
    inds = _head_indicators(512) + _head_indicators(128)
    qn, kn, va, qm, km, vm, om, gt, new_k, new_v = _inproj(*x, mod3, norm1_w, w_main, w_gates_t, qw, kw, inds,
                                                           _rope_tables())

    att_c = _attention(qn, kn, va, None, n_batch=BATCH, seq=SEQ, first_row=0)
    att_l = _attention(qn, kn, va, (cache_attn_k, cache_attn_v), n_batch=DEC_BATCH, seq=DEC_SEQ,
                       first_row=T_CTX)

    g3 = gt.reshape(16, T_ALL // CHUNK, CHUNK).transpose(1, 0, 2)
    bg = b_gates.reshape(16, 1)
    hm_c, c_new, n_new, m_new = _mlstm(qm, km, vm, g3, bg, om, m_norm_w, None,
                                       n_batch=BATCH, seq=SEQ, first_block=0)
    m0 = jnp.broadcast_to(state_mlstm_m.reshape(DEC_BATCH, 8, 1), (DEC_BATCH, 8, CHUNK))
    hm_l = _mlstm(qm, km, vm, g3, bg, om, m_norm_w, (state_mlstm_c, state_mlstm_n, m0),
                  n_batch=DEC_BATCH, seq=DEC_SEQ, first_block=N_CTX_BLOCKS)

    rw_t = router_w[0].T
    rw_hi = rw_t.astype(BF16)
    rw_lo = (rw_t - rw_hi.astype(F32)).astype(BF16)
    base, hn2, logits_t = _outproj((att_c, att_l), (hm_c, hm_l), x, mod3, norm2_w, w_out[0].astype(BF16),
                                   rw_hi, rw_lo, ws_gate[0].astype(BF16), ws_up[0].astype(BF16),
                                   ws_down[0].astype(BF16))
    gate, idx8 = _router(logits_t, router_bias.reshape(N_EXPERTS, 1))
    slot_words, cnt = _plan(idx8)
    out_c, out_l = _moe(slot_words.reshape(TOP_K * T_HALF), cnt[:, :, 0].reshape(2 * N_EXPERTS),
                        hn2, gate, w_gate[0], w_up[0], w_down[0], base, mod3)

    y_prompt = out_c.reshape(BATCH, SEQ, D_MODEL)
    y_sample = out_l.reshape(DEC_BATCH, DEC_SEQ, D_MODEL)
    new_m = m_new[:, :, 0].reshape(BATCH, 1, 2, N_HEADS_M)
    return (y_prompt, y_sample, new_k[:, None], new_v[:, None], c_new[:, None], n_new[:, None], new_m)
```

```python
import functools

import jax
import jax.numpy as jnp
from jax import lax
from jax.experimental import pallas as pl
from jax.experimental.pallas import tpu as pltpu

F32 = jnp.float32
BF16 = jnp.bfloat16

D_MODEL = 1024
BATCH = 32
SEQ = 256
DEC_BATCH = 2
DEC_SEQ = 1024
PAST_LEN = 256
GRID_W = 64
N_HEADS_ATT = 8
N_KV_HEADS = 2
DH_ATT = 64
D_ATT = 512
ROPE_THETA = 10000.0
N_HEADS_M = 4
DH_M = 128
D_M = 512
CHUNK = 128
N_EXPERTS = 64
TOP_K = 8
N_GROUPS = 8
TOPK_GROUPS = 4
D_EXPERT = 256
ROUTED_SCALE = 2.5
EPS = 1e-6
NEG_INIT = -1e30

T_CTX = BATCH * SEQ
T_LAT = DEC_BATCH * DEC_SEQ
T_ALL = T_CTX + T_LAT
TB = 256
N_CTX_BLOCKS = T_CTX // TB
TBP = 512
NP_CTX = T_CTX // TBP
NP_LAT_PER_BATCH = DEC_SEQ // TBP
TB_MOE = 1024
T_HALF = T_ALL // 2
MT = 256
MT_STRIDE = MT + 8
NT_MAX = TOP_K * T_HALF // MT + N_EXPERTS
SLOT_CAP = -(-(NT_MAX + 2) * MT // 1024) * 1024
RMW_BATCH = 16
INV_UNROLL = 16
W_SLOTS = 2
FIN = 128
FIN_BASE_SLOTS = 6
FIN_OUT_SLOTS = 4
assert FIN_OUT_SLOTS <= FIN_BASE_SLOTS
assert SLOT_CAP < 2 ** 16 and T_HALF % INV_UNROLL == 0 and MT % RMW_BATCH == 0
P_MAIN = 2816
VMEM_LIMIT = 56 * 1024 * 1024

_NT = (((1,), (1,)), ((), ()))


def _cparams(sem):
    return pltpu.CompilerParams(dimension_semantics=sem, vmem_limit_bytes=VMEM_LIMIT)


def _split3(x):
    hi = x.astype(BF16)
    r1 = x - hi.astype(F32)
    mid = r1.astype(BF16)
    lo = (r1 - mid.astype(F32)).astype(BF16)
    return hi, mid, lo


def _dot(a, b):
    return jnp.dot(a, b, preferred_element_type=F32)


def _dot3(x, m_bf16):
    hi, mid, lo = _split3(x)
    return _dot(hi, m_bf16) + _dot(mid, m_bf16) + _dot(lo, m_bf16)


def _dot2(x, m_bf16):
    hi = x.astype(BF16)
    lo = (x - hi.astype(F32)).astype(BF16)
    return _dot(hi, m_bf16) + _dot(lo, m_bf16)


def _silu(x):
    return x * jax.nn.sigmoid(x)


def _mod_row(i):
    return jnp.where(i < NP_CTX, 0, 1 + (i - NP_CTX) // NP_LAT_PER_BATCH)


def _mod_kernel(ct_ref, w_ref, b_ref, o_ref):
    s = _silu(ct_ref[...])
    w = w_ref[...]
    rows = [jnp.sum(w * s[:, r:r + 1], axis=0, keepdims=True) for r in range(3)]
    rows.append(jnp.zeros((5, w.shape[1]), F32))
    o_ref[...] = jnp.concatenate(rows, axis=0) + b_ref[...]


def _modulation(c, c_ctx, w_mod, b_mod):
    cvec = jnp.concatenate([c_ctx[None, :], c, jnp.zeros((5, D_MODEL), F32)], axis=0)
    nb = 1024
    out = pl.pallas_call(
        _mod_kernel,
        grid=(6 * D_MODEL // nb,),
        in_specs=[pl.BlockSpec((D_MODEL, 8), lambda j: (0, 0)),
                  pl.BlockSpec((D_MODEL, nb), lambda j: (0, j)),
                  pl.BlockSpec((1, nb), lambda j: (0, j))],
        out_specs=pl.BlockSpec((8, nb), lambda j: (0, j)),
        out_shape=jax.ShapeDtypeStruct((8, 6 * D_MODEL), F32),
        compiler_params=_cparams(("arbitrary",)),
        name="modulation",
    )(cvec.T, w_mod, b_mod[None, :])
    return out.reshape(8, 6, D_MODEL)


def _ctx_or_lat(ctx_ref, lat_ref):
    return jnp.where(pl.program_id(0) < NP_CTX, ctx_ref[...], lat_ref[...])


def _split_specs(width):
    return [pl.BlockSpec((TBP, width), lambda i: (jnp.minimum(i, NP_CTX - 1), 0)),
            pl.BlockSpec((TBP, width), lambda i: (jnp.maximum(i - NP_CTX, 0), 0))]


def _head_norm(x, ind, ind_t, w_row):
    ss = _dot2(x * x, ind)
    inv = lax.rsqrt(ss * (1.0 / DH_ATT) + EPS)
    return x * _dot2(inv, ind_t) * w_row


def _rope(x, cos, sin_signed):
    lane = lax.broadcasted_iota(jnp.int32, x.shape, 1)
    partner = jnp.where((lane % 32) < 16, pltpu.roll(x, 128 - 16, 1), pltpu.roll(x, 16, 1))
    return x * cos + partner * sin_signed


def _inproj_kernel(xc_ref, xl_ref, mod_ref, nw_ref, w_ref, wgt_ref, qw_ref, kw_ref, iq_ref, iqt_ref,
                   ik_ref, ikt_ref, cos_ref, sin_ref,
                   qn_ref, kn_ref, va_ref, qm_ref, km_ref, vm_ref, om_ref, gt_ref, kc_ref, vc_ref, w_b):
    @pl.when(pl.program_id(0) == 0)
    def _():
        for j in range(P_MAIN // 128):
            w_b[:, 128 * j:128 * (j + 1)] = w_ref[:, 128 * j:128 * (j + 1)].astype(BF16)

    x = _ctx_or_lat(xc_ref, xl_ref)
    y = x * lax.rsqrt(jnp.mean(x * x, axis=-1, keepdims=True) + EPS) * nw_ref[...]
    hn = y * (1.0 + mod_ref[1:2, :]) + mod_ref[0:1, :]
    hb = hn.astype(BF16)
    cols = lambda a, b: w_b[:, a:b]
    qn = _head_norm(_dot(hb, cols(0, 512)), iq_ref[...], iqt_ref[...], qw_ref[...])
    kn = _head_norm(_dot(hb, cols(512, 640)), ik_ref[...], ikt_ref[...], kw_ref[...])
    qn_ref[...] = qn.astype(BF16)
    kn_ref[...] = kn

    @pl.when(pl.program_id(0) >= NP_CTX)
    def _():
        cos, sin = cos_ref[...], sin_ref[...]
        for j in range(4):
            qn_ref[:, 128 * j:128 * (j + 1)] = _rope(qn[:, 128 * j:128 * (j + 1)], cos, sin).astype(BF16)
        kn_ref[...] = _rope(kn, cos, sin)

    va = _dot(hb, cols(640, 768))
    va_ref[...] = va

    @pl.when(pl.program_id(0) < NP_CTX)
    def _():
        for bb in range(TBP // SEQ):
            rows = slice(SEQ * bb, SEQ * (bb + 1))
            kn_t, va_t = kn[rows, :].T, va[rows, :].T
            for g in range(N_KV_HEADS):
                dims = slice(DH_ATT * g, DH_ATT * (g + 1))
                kc_ref[bb, g] = kn_t[dims, :]
                vc_ref[bb, g] = va_t[dims, :]

    qm_ref[...] = _dot(hb, cols(768, 1280)).astype(BF16)
    km_ref[...] = (_dot(hb, cols(1280, 1792)) * (DH_M ** -0.5)).astype(BF16)
    vm_ref[...] = _dot(hb, cols(1792, 2304)).astype(BF16)
    om_ref[...] = _dot(hb, cols(2304, 2816))
    gt_ref[...] = lax.dot_general(wgt_ref[...], hb, _NT, preferred_element_type=F32)


def _inproj(x_ctx, x_lat, mod3, norm1_w, w_main, w_gates_t, qw, kw, inds, rope_tabs):
    tok = lambda w: pl.BlockSpec((TBP, w), lambda i: (i, 0))
    full = lambda a: pl.BlockSpec(a.shape, lambda i: (0,) * a.ndim)
    sd = lambda w, dt: jax.ShapeDtypeStruct((T_ALL, w), dt)
    rope_spec = pl.BlockSpec((TBP, 128), lambda i: (jnp.maximum(i - NP_CTX, 0) % NP_LAT_PER_BATCH, 0))
    cache_spec = pl.BlockSpec((TBP // SEQ, N_KV_HEADS, DH_ATT, SEQ),
                              lambda i: (jnp.minimum(i, NP_CTX - 1), 0, 0, 0))
    cache_shape = jax.ShapeDtypeStruct((BATCH, N_KV_HEADS, DH_ATT, SEQ), F32)
    return pl.pallas_call(
        _inproj_kernel,
        grid=(T_ALL // TBP,),
        in_specs=_split_specs(D_MODEL) + [
                  pl.BlockSpec((None, 6, D_MODEL), lambda i: (_mod_row(i), 0, 0)),
                  full(norm1_w), full(w_main), full(w_gates_t), full(qw), full(kw)]
                 + [full(a) for a in inds] + [rope_spec, rope_spec],
        out_specs=[tok(512), tok(128), tok(128), tok(512), tok(512), tok(512), tok(512),
                   pl.BlockSpec((16, TBP), lambda i: (0, i)), cache_spec, cache_spec],
        out_shape=[sd(512, BF16), sd(128, F32), sd(128, F32), sd(512, BF16), sd(512, BF16),
                   sd(512, BF16), sd(512, F32), jax.ShapeDtypeStruct((16, T_ALL), F32),
                   cache_shape, cache_shape],
        scratch_shapes=[pltpu.VMEM((D_MODEL, P_MAIN), BF16)],
        compiler_params=_cparams(("arbitrary",)),
        name="inproj",
    )(x_ctx, x_lat, mod3, norm1_w, w_main, w_gates_t, qw, kw, *inds, *rope_tabs)


def _rope_tables():
    t = jnp.arange(DEC_SEQ)
    pos = jnp.stack([t // GRID_W, t % GRID_W], axis=1).astype(F32)
    n_freq = DH_ATT // 4
    inv_freq = ROPE_THETA ** (-jnp.arange(n_freq, dtype=F32) / n_freq)
    ang = pos[:, :, None] * inv_freq
    cos, sin = jnp.cos(ang), jnp.sin(ang)
    cos_h = jnp.stack([cos, cos], axis=2).reshape(DEC_SEQ, DH_ATT)
    sin_h = jnp.stack([-sin, sin], axis=2).reshape(DEC_SEQ, DH_ATT)
    return jnp.tile(cos_h, (1, 2)), jnp.tile(sin_h, (1, 2))


def _attn_kernel(*refs, has_cache):
    if has_cache:
        q_ref, k_ref, v_ref, kc_ref, vc_ref, o_ref = refs
    else:
        q_ref, k_ref, v_ref, o_ref = refs
    q = q_ref[...] * jnp.asarray(DH_ATT ** -0.5, BF16)
    k = k_ref[...].astype(BF16)
    v = v_ref[...].astype(BF16)
    qb = q.shape[0]
    low_half = lax.broadcasted_iota(jnp.int32, (1, 128), 1) < DH_ATT
    for g in range(N_KV_HEADS):
        kg = k[:, DH_ATT * g:DH_ATT * (g + 1)]
        vg = v[:, DH_ATT * g:DH_ATT * (g + 1)]
        if has_cache:
            kg = jnp.concatenate([kg, kc_ref[g].astype(BF16)], axis=0)
            vg = jnp.concatenate([vg, vc_ref[g].astype(BF16)], axis=0)
        n_keys = kg.shape[0]
        zero = jnp.zeros_like(kg)
        k2 = jnp.concatenate([jnp.concatenate([kg, zero], axis=1), jnp.concatenate([zero, kg], axis=1)], axis=0)
        v2 = jnp.concatenate([jnp.concatenate([vg, zero], axis=1), jnp.concatenate([zero, vg], axis=1)], axis=0)
        ones2 = jnp.concatenate([jnp.broadcast_to(jnp.where(low_half, 1.0, 0.0), (n_keys, 128)),
                                 jnp.broadcast_to(jnp.where(low_half, 0.0, 1.0), (n_keys, 128))],
                                axis=0).astype(BF16)
        qp = jnp.concatenate([q[:, 256 * g:256 * g + 128], q[:, 256 * g + 128:256 * g + 256]], axis=0)
        s = lax.dot_general(qp, k2, _NT, preferred_element_type=F32)
        halves = [s[:, 0:n_keys], s[:, n_keys:]]
        p = jnp.concatenate([jnp.exp(sh - jnp.max(sh, axis=-1, keepdims=True)) for sh in halves],
                            axis=1).astype(BF16)
        ov = _dot(p, jnp.concatenate([v2, ones2], axis=1))
        o = ov[:, 0:128] / ov[:, 128:]
        o_ref[:, 256 * g:256 * g + 128] = o[0:qb, :].astype(BF16)
        o_ref[:, 256 * g + 128:256 * g + 256] = o[qb:, :].astype(BF16)


def _attention(qn, kn, v_all, cache, *, n_batch, seq, first_row):
    qblocks = seq // TB
    kv_spec = pl.BlockSpec((seq, 128), lambda b, i: (b + first_row // seq, 0))
    in_specs = [pl.BlockSpec((TB, 512), lambda b, i: (first_row // TB + b * qblocks + i, 0)), kv_spec, kv_spec]
    args = [qn, kn, v_all]
    if cache is not None:
        cspec = pl.BlockSpec((None, None, N_KV_HEADS, PAST_LEN, DH_ATT), lambda b, i: (b, 0, 0, 0, 0))
        in_specs += [cspec, cspec]
        args += list(cache)
    return pl.pallas_call(
        functools.partial(_attn_kernel, has_cache=cache is not None),
        grid=(n_batch, qblocks),
        in_specs=in_specs,
        out_specs=pl.BlockSpec((TB, 512), lambda b, i: (b * qblocks + i, 0)),
        out_shape=jax.ShapeDtypeStruct((n_batch * seq, 512), BF16),
        compiler_params=_cparams(("arbitrary", "arbitrary")),
        name="attention_lat" if cache is not None else "attention_ctx",
    )(*args)


def _log_sigmoid(x):
    return jnp.minimum(x, 0.0) - jnp.log1p(jnp.exp(-jnp.abs(x)))


def _col_bcast(cols, j):
    return jnp.broadcast_to(cols[:, j:j + 1], (CHUNK, CHUNK))


def _mlstm_kernel(*refs, has_state, n_chunks):
    n_in = 10 if has_state else 7
    n_out = 1 if has_state else 4
    q_ref, k_ref, v_ref, g_ref, bg_ref, om_ref, nw_ref = refs[:7]
    hm_ref = refs[n_in]
    scratch = refs[n_in + n_out:]
    st = scratch[0:8]
    ms = scratch[8:16]
    hdir = scratch[16:18]
    rows_sc, cmb_sc, bb_sc, kt_sc = scratch[18:22]
    neg_inf = F32(-jnp.inf)
    zeros112 = jnp.zeros((CHUNK - 16, CHUNK), F32)
    sub = lax.broadcasted_iota(jnp.int32, (CHUNK, CHUNK), 0)
    lan = lax.broadcasted_iota(jnp.int32, (CHUNK, CHUNK), 1)
    ones = jnp.ones((CHUNK, CHUNK), BF16)

    stats = []
    for c in range(n_chunks):
        pre = g_ref[c] + bg_ref[...]
        logf = _log_sigmoid(pre)
        for d in range(2):
            bcum = _dot3(logf, ((sub <= lan) if d == 0 else (sub >= lan)).astype(BF16))
            li4 = pre[4 * d:4 * d + 4, :]
            lf4 = logf[8 + 4 * d:12 + 4 * d, :]
            b4 = bcum[8 + 4 * d:12 + 4 * d, :]
            stats.append((c, d, li4 - b4, lf4, b4))
    for c, d, r4, lf4, b4 in stats:
        blast4 = b4[:, CHUNK - 1:CHUNK] if d == 0 else b4[:, 0:1]
        wlog4 = blast4 + r4
        wmax4 = jnp.max(wlog4, axis=-1, keepdims=True)
        full = lambda a: jnp.broadcast_to(a, (4, CHUNK))
        rows_sc[2 * c + d] = jnp.concatenate(
            [r4, wlog4, full(blast4), full(wmax4), jnp.zeros((16, CHUNK), F32)], axis=0)
    for c in range(n_chunks):
        for h in range(N_HEADS_M):
            kc = k_ref[CHUNK * c:CHUNK * (c + 1), DH_M * h:DH_M * (h + 1)]
            kt_sc[c * N_HEADS_M + h] = kc.astype(F32).T.astype(BF16)
    for c, d, r4, lf4, b4 in stats:
        within = (lan <= sub) if d == 0 else (lan >= sub)
        for h in range(N_HEADS_M):
            cummax = jnp.max(jnp.where(within, r4[h:h + 1, :], neg_inf), axis=-1, keepdims=True)
            cmb_sc[(2 * c + d) * N_HEADS_M + h] = jnp.broadcast_to(cummax, (CHUNK, CHUNK))
    for c, d, r4, lf4, b4 in stats:
        within = (lan <= sub) if d == 0 else (lan >= sub)
        for h in range(N_HEADS_M):
            bb_sc[(2 * c + d) * N_HEADS_M + h] = _dot2(jnp.where(within, lf4[h:h + 1, :], 0.0), ones)

    for d in range(2):
        if has_state:
            c0_ref, n0_ref, m0_ref = refs[7:10]
            ncols = jnp.concatenate([n0_ref[d], jnp.zeros((12, DH_M), F32), zeros112], axis=0).T
        for h in range(N_HEADS_M):
            idx = 4 * d + h
            if has_state:
                st[idx][:, 0:DH_M] = c0_ref[d, h]
                st[idx][:, DH_M:] = _col_bcast(ncols, h)
                ms[idx][0:1, :] = m0_ref[idx:idx + 1, :]
            else:
                st[idx][...] = jnp.zeros((DH_M, 2 * DH_M), F32)
                ms[idx][0:1, :] = jnp.full((1, CHUNK), NEG_INIT, F32)

    def chunk_step(i, carry):
        for d in range(2):
            c = i if d == 0 else n_chunks - 1 - i
            r0 = pl.multiple_of(c * CHUNK, CHUNK)
            rows = rows_sc[2 * c + d]
            mask = (lan <= sub) if d == 0 else (lan >= sub)
            heads = []
            for h in range(N_HEADS_M):
                idx = 4 * d + h
                hs = slice(DH_M * h, DH_M * (h + 1))
                qc = q_ref[pl.ds(r0, CHUNK), hs]
                kt = kt_sc[c * N_HEADS_M + h]
                state = st[idx][...]
                heads.append(dict(idx=idx, hs=hs, qc=qc, kt=kt, state=state,
                                  s_raw=_dot(qc, kt), qs=_dot(qc, state.astype(BF16))))
            for h, hd in enumerate(heads):
                mp = ms[hd["idx"]][0:1, :]
                m_b = jnp.maximum(mp, cmb_sc[(2 * c + d) * N_HEADS_M + h])
                s = hd["s_raw"] * jnp.exp(jnp.where(mask, rows[h:h + 1, :] - m_b, neg_inf))
                s_hi = s.astype(BF16)
                vc = v_ref[pl.ds(r0, CHUNK), hd["hs"]]
                v_ones = jnp.concatenate([vc, ones], axis=1)
                hd.update(mp=mp, m_b=m_b, v_ones=v_ones, sv=_dot(s_hi, v_ones),
                          s_lo_sum=_dot((s - s_hi.astype(F32)).astype(BF16), ones))
            for h, hd in enumerate(heads):
                mp, m_b = hd["mp"], hd["m_b"]
                gw_b = jnp.exp(mp - m_b)
                en_b = jnp.exp(-(bb_sc[(2 * c + d) * N_HEADS_M + h] + m_b))
                num = hd["sv"][:, 0:DH_M] + gw_b * hd["qs"][:, 0:DH_M]
                den = hd["sv"][:, DH_M:] + hd["s_lo_sum"] + gw_b * hd["qs"][:, DH_M:]
                hdir[d][pl.ds(r0, CHUNK), hd["hs"]] = num / jnp.maximum(jnp.abs(den), en_b)
            for h, hd in enumerate(heads):
                mp = hd["mp"]
                gend = rows[8 + h:9 + h, :] + mp
                mnew = jnp.maximum(gend, rows[12 + h:13 + h, :])
                w_row = jnp.exp(rows[4 + h:5 + h, :] - mnew)
                kw_t = (hd["kt"].astype(F32) * w_row).astype(BF16)
                dec = jnp.exp(gend - mnew)
                st[hd["idx"]][...] = (jnp.concatenate([dec, dec], axis=1) * hd["state"]
                                      + _dot(kw_t, hd["v_ones"]))
                ms[hd["idx"]][0:1, :] = mnew
        return carry

    lax.fori_loop(0, n_chunks, chunk_step, 0)

    for h in range(N_HEADS_M):
        hs = slice(DH_M * h, DH_M * (h + 1))
        hh = hdir[0][:, hs] + hdir[1][:, hs]
        y = hh * lax.rsqrt(jnp.mean(hh * hh, axis=-1, keepdims=True) + EPS) * nw_ref[:, hs]
        hm_ref[:, hs] = (jax.nn.sigmoid(om_ref[:, hs]) * y).astype(BF16)
    if not has_state:
        c_ref, n_ref, m_ref = refs[n_in + 1:n_in + 4]
        for d in range(2):
            for h in range(N_HEADS_M):
                idx = 4 * d + h
                c_ref[d, h] = st[idx][:, 0:DH_M]
                n_ref[d, h:h + 1, :] = st[idx][:, DH_M:].T[0:1, :]
                m_ref[idx:idx + 1, :] = ms[idx][0:1, :]


def _mlstm(qm, km, vm, g3, bg, om, nw, state, *, n_batch, seq, first_block):
    n_chunks = seq // CHUNK
    rb = seq // TB
    tok = lambda: pl.BlockSpec((seq, 512), lambda b: (b + first_block // rb, 0))
    full = lambda a: pl.BlockSpec(a.shape, lambda b: (0,) * a.ndim)
    in_specs = [tok(), tok(), tok(),
                pl.BlockSpec((n_chunks, 16, CHUNK), lambda b: (b + first_block // rb, 0, 0)),
                full(bg), tok(), full(nw)]
    args = [qm, km, vm, g3, bg, om, nw]
    hm_spec = pl.BlockSpec((seq, 512), lambda b: (b, 0))
    hm_shape = jax.ShapeDtypeStruct((n_batch * seq, 512), BF16)
    scratch = ([pltpu.VMEM((DH_M, 2 * DH_M), F32)] * 8 + [pltpu.VMEM((8, CHUNK), F32)] * 8
               + [pltpu.VMEM((seq, 512), F32)] * 2
               + [pltpu.VMEM((2 * n_chunks, 32, CHUNK), F32),
                  pltpu.VMEM((8 * n_chunks, CHUNK, CHUNK), F32),
                  pltpu.VMEM((8 * n_chunks, CHUNK, CHUNK), F32),
                  pltpu.VMEM((N_HEADS_M * n_chunks, DH_M, CHUNK), BF16)])
    if state is not None:
        c0, n0, m0 = state
        in_specs += [pl.BlockSpec((None, None, 2, N_HEADS_M, DH_M, DH_M), lambda b: (b, 0, 0, 0, 0, 0)),
                     pl.BlockSpec((None, None, 2, N_HEADS_M, DH_M), lambda b: (b, 0, 0, 0, 0)),
                     pl.BlockSpec((None, 8, CHUNK), lambda b: (b, 0, 0))]
        args += [c0, n0, m0]
        out_specs, out_shape = hm_spec, hm_shape
    else:
        out_specs = [hm_spec,
                     pl.BlockSpec((None, 2, N_HEADS_M, DH_M, DH_M), lambda b: (b, 0, 0, 0, 0)),
                     pl.BlockSpec((None, 2, N_HEADS_M, DH_M), lambda b: (b, 0, 0, 0)),
                     pl.BlockSpec((None, 8, CHUNK), lambda b: (b, 0, 0))]
        out_shape = [hm_shape,
                     jax.ShapeDtypeStruct((n_batch, 2, N_HEADS_M, DH_M, DH_M), F32),
                     jax.ShapeDtypeStruct((n_batch, 2, N_HEADS_M, DH_M), F32),
                     jax.ShapeDtypeStruct((n_batch, 8, CHUNK), F32)]
    return pl.pallas_call(
        functools.partial(_mlstm_kernel, has_state=state is not None, n_chunks=n_chunks),
        grid=(n_batch,),
        in_specs=in_specs,
        out_specs=out_specs,
        out_shape=out_shape,
        scratch_shapes=scratch,
        compiler_params=_cparams(("arbitrary",)),
        name="mlstm_lat" if state is not None else "mlstm_ctx",
    )(*args)


def _outproj_kernel(attc_ref, attl_ref, hmc_ref, hml_ref, xc_ref, xl_ref, mod_ref, nw_ref, wo_ref,
                    rwh_ref, rwl_ref, wsg_ref, wsu_ref, wsd_ref, base_ref, hn_ref, lt_ref):
    y = (_dot(_ctx_or_lat(attc_ref, attl_ref), wo_ref[0:D_ATT, :])
         + _dot(_ctx_or_lat(hmc_ref, hml_ref), wo_ref[D_ATT:, :]))
    x1 = _ctx_or_lat(xc_ref, xl_ref) + mod_ref[2:3, :] * y
    z = x1 * lax.rsqrt(jnp.mean(x1 * x1, axis=-1, keepdims=True) + EPS) * nw_ref[...]
    hn = z * (1.0 + mod_ref[4:5, :]) + mod_ref[3:4, :]
    hb = hn.astype(BF16)
    hl = (hn - hb.astype(F32)).astype(BF16)
    nt = lambda w, t: lax.dot_general(w, t, _NT, preferred_element_type=F32)
    lt_ref[...] = nt(rwh_ref[...], hb) + nt(rwl_ref[...], hb) + nt(rwh_ref[...], hl)
    a = _silu(_dot(hb, wsg_ref[...])) * _dot(hb, wsu_ref[...])
    shared = _dot(a.astype(BF16), wsd_ref[...])
    base_ref[...] = x1 + mod_ref[5:6, :] * shared
    packed = pltpu.pack_elementwise([hn[:, :512], hn[:, 512:]], packed_dtype=BF16)
    for c in range(4):
        hn_ref[pl.ds(c, TBP, stride=4), :] = packed[:, 128 * c:128 * (c + 1)]


def _outproj(att, hm, x, mod3, norm2_w, w_out, rw_hi, rw_lo, wsg, wsu, wsd):
    tok = lambda w: pl.BlockSpec((TBP, w), lambda i: (i, 0))
    full = lambda a: pl.BlockSpec(a.shape, lambda i: (0,) * a.ndim)
    return pl.pallas_call(
        _outproj_kernel,
        grid=(T_ALL // TBP,),
        in_specs=_split_specs(512) + _split_specs(512) + _split_specs(D_MODEL) + [
                  pl.BlockSpec((None, 6, D_MODEL), lambda i: (_mod_row(i), 0, 0)),
                  full(norm2_w), full(w_out), full(rw_hi), full(rw_lo), full(wsg), full(wsu), full(wsd)],
        out_specs=[tok(D_MODEL), pl.BlockSpec((4 * TBP, 128), lambda i: (i, 0)),
                   pl.BlockSpec((N_EXPERTS, TBP), lambda i: (0, i))],
        out_shape=[jax.ShapeDtypeStruct((T_ALL, D_MODEL), F32),
                   jax.ShapeDtypeStruct((4 * T_ALL, 128), jnp.uint32),
                   jax.ShapeDtypeStruct((N_EXPERTS, T_ALL), F32)],
        compiler_params=_cparams(("arbitrary",)),
        name="outproj",
    )(*att, *hm, *x, mod3, norm2_w, w_out, rw_hi, rw_lo, wsg, wsu, wsd)


def _first_max(vals, ids, limit):
    m = functools.reduce(jnp.maximum, [jnp.max(v, axis=0, keepdims=True) for v in vals])
    cand = [jnp.min(jnp.where(v == m, i, limit), axis=0, keepdims=True) for v, i in zip(vals, ids)]
    return m, functools.reduce(jnp.minimum, cand)


def _router_kernel(lt_ref, bias_ref, gate_ref, idx_ref):
    n = lt_ref.shape[1]
    score = jax.nn.sigmoid(lt_ref[...])
    biased = score + bias_ref[...]
    sub = lax.broadcasted_iota(jnp.int32, (8, n), 0).astype(F32)
    neg_inf = F32(-jnp.inf)
    slabs = [biased[8 * g:8 * (g + 1), :] for g in range(N_GROUPS)]
    gs = []
    for sl in slabs:
        m1, i1 = _first_max([sl], [sub], 8.0)
        m2 = jnp.max(jnp.where(sub == i1, neg_inf, sl), axis=0, keepdims=True)
        gs.append(m1 + m2)
    cur = jnp.concatenate(gs, axis=0)
    gsel = jnp.zeros((8, n), F32)
    for _ in range(TOPK_GROUPS):
        _, i = _first_max([cur], [sub], 8.0)
        hit = sub == i
        gsel = jnp.where(hit, 1.0, gsel)
        cur = jnp.where(hit, neg_inf, cur)
    vals = [jnp.where(gsel[g:g + 1, :] > 0.0, slabs[g], neg_inf) for g in range(N_GROUPS)]
    ids = [sub + 8.0 * g for g in range(N_GROUPS)]
    picked = [jnp.zeros((8, n), F32) for _ in range(N_GROUPS)]
    order = []
    for _ in range(TOP_K):
        _, i = _first_max(vals, ids, float(N_EXPERTS))
        order.append(i)
        hits = [idg == i for idg in ids]
        picked = [jnp.where(hh, score[8 * g:8 * (g + 1), :], p) for g, (p, hh) in enumerate(zip(picked, hits))]
        vals = [jnp.where(hh, neg_inf, v) for v, hh in zip(vals, hits)]
    total = functools.reduce(jnp.add, [jnp.sum(p, axis=0, keepdims=True) for p in picked])
    gate_t = jnp.concatenate([p / total * ROUTED_SCALE for p in picked]
                             + [jnp.zeros((128 - N_EXPERTS, n), F32)], axis=0)
    gate_ref[...] = gate_t.T
    idx_ref[...] = jnp.concatenate(order, axis=0).astype(jnp.int32)


def _router(logits_t, bias_col):
    return pl.pallas_call(
        _router_kernel,
        grid=(T_ALL // TB_MOE,),
        in_specs=[pl.BlockSpec((N_EXPERTS, TB_MOE), lambda i: (0, i)),
                  pl.BlockSpec((N_EXPERTS, 1), lambda i: (0, 0))],
        out_specs=[pl.BlockSpec((TB_MOE, 128), lambda i: (i, 0)),
                   pl.BlockSpec((TOP_K, TB_MOE), lambda i: (0, i))],
        out_shape=[jax.ShapeDtypeStruct((T_ALL, 128), F32),
                   jax.ShapeDtypeStruct((TOP_K, T_ALL), jnp.int32)],
        compiler_params=_cparams(("arbitrary",)),
        name="router",
    )(logits_t, bias_col)


def _plan_kernel(idx_ref, slot_ref, cnt_ref, pos_sc):
    n_tiles = T_HALF // 128
    eid = lax.broadcasted_iota(jnp.int32, (N_EXPERTS, 128), 0)
    tri = (lax.broadcasted_iota(jnp.int32, (128, 128), 0)
           <= lax.broadcasted_iota(jnp.int32, (128, 128), 1)).astype(BF16)
    carry = jnp.zeros((N_EXPERTS, 1), F32)
    for j in range(n_tiles):
        it = idx_ref[:, 128 * j:128 * (j + 1)]
        sel = jnp.zeros((N_EXPERTS, 128), F32)
        for k in range(TOP_K):
            sel = jnp.where(it[k:k + 1, :] == eid, 1.0, sel)
        inc = _dot(sel.astype(BF16), tri) + carry
        carry = inc[:, 127:128]
        pos_sc[:, 128 * j:128 * (j + 1)] = inc - 1.0
    count = jnp.broadcast_to(carry, (N_EXPERTS, 128))
    padded = jnp.floor((count + (MT - 1.0)) * (1.0 / MT)) * MT
    before = (lax.broadcasted_iota(jnp.int32, (N_EXPERTS, N_EXPERTS), 1)
              < lax.broadcasted_iota(jnp.int32, (N_EXPERTS, N_EXPERTS), 0)).astype(BF16)
    hi, mid, lo = _split3(padded)
    off = _dot(before, hi) + _dot(before, mid) + _dot(before, lo)
    cnt_ref[...] = count.astype(jnp.int32)

    def slots(j):
        it = idx_ref[:, 128 * j:128 * (j + 1)]
        val = off + pos_sc[:, 128 * j:128 * (j + 1)]
        rows = [jnp.sum(jnp.where(it[k:k + 1, :] == eid, val, 0.0), axis=0, keepdims=True)
                for k in range(TOP_K)]
        return jnp.concatenate(rows, axis=0).astype(jnp.int32)

    for j in range(n_tiles):
        s = slots(j)
        slot_ref[:, 128 * j:128 * (j + 1)] = s[0:TOP_K // 2, :] | (s[TOP_K // 2:, :] << 16)


def _plan(idx8):
    return pl.pallas_call(
        _plan_kernel,
        grid=(2,),
        in_specs=[pl.BlockSpec((TOP_K, T_HALF), lambda h: (0, h))],
        out_specs=[pl.BlockSpec((None, TOP_K // 2, T_HALF), lambda h: (h, 0, 0)),
                   pl.BlockSpec((None, N_EXPERTS, 128), lambda h: (h, 0, 0))],
        out_shape=[jax.ShapeDtypeStruct((2, TOP_K // 2, T_HALF), jnp.int32),
                   jax.ShapeDtypeStruct((2, N_EXPERTS, 128), jnp.int32)],
        scratch_shapes=[pltpu.VMEM((N_EXPERTS, T_HALF), F32)],
        compiler_params=_cparams(("arbitrary",)),
        name="plan",
    )(idx8)


def _prepare_half(cnt_ref, tile_e, tile_first, elist, h):
    def per_expert(x, carry):
        j, q = carry
        n = cnt_ref[h * N_EXPERTS + x]
        tiles = (n + MT - 1) // MT
        elist[q] = x

        def mark(i, c):
            tile_e[j + i] = x
            tile_first[j + i] = jnp.where(i == 0, 1, 0)
            return c

        lax.fori_loop(0, tiles, mark, 0)

        return j + tiles, q + jnp.where(tiles > 0, 1, 0)

    n_tiles, n_live = lax.fori_loop(0, N_EXPERTS, per_expert, (0, 0))
    for extra in range(2):
        tile_e[n_tiles + extra] = 0
        tile_first[n_tiles + extra] = 0
    return n_tiles, n_live


def _invert_slots(slot_ref, tok_ref, acc_v, h):
    rounds = TOP_K // 2
    trips = T_HALF // INV_UNROLL
    zero_rows = 8 * T_HALF // (rounds * trips)
    assert zero_rows * rounds * trips == 8 * T_HALF and zero_rows % 8 == 0
    for k in range(rounds):
        def body(i, carry, k=k):
            words = [slot_ref[(h * rounds + k) * T_HALF + i * INV_UNROLL + u] for u in range(INV_UNROLL)]
            for u in range(INV_UNROLL):
                t = i * INV_UNROLL + u
                tok_ref[words[u] & 0xFFFF] = t
                tok_ref[lax.shift_right_logical(words[u], 16)] = t
            row = pl.multiple_of((k * trips + i) * zero_rows, 8)
            acc_v[pl.ds(row, zero_rows), :] = jnp.zeros((zero_rows, 128), F32)
            return carry

        lax.fori_loop(0, trips, body, 0)
    acc_v[pl.ds(8 * T_HALF, 8), :] = jnp.zeros((8, 128), F32)


def _moe_kernel(slot_ref, cnt_ref, src_hbm, gate_hbm, wg_hbm, wu_hbm, wd_hbm, base_hbm, mod_ref,
                outc_hbm, outl_hbm,
                src_v, gate_v, acc_v, wg_l, wu_l, wd_l, wgu_b, wd_b, xbuf0, xbuf1, gbuf0, gbuf1,
                ybuf0, ybuf1, tok_ref, tile_e, tile_first, elist, live, base_buf, out_buf, pad_v,
                sem, wsem, fsem):
    h = pl.program_id(0)
    xbuf, gbuf, ybuf = (xbuf0, xbuf1), (gbuf0, gbuf1), (ybuf0, ybuf1)

    src_cp = pltpu.make_async_copy(src_hbm.at[pl.ds(pl.multiple_of(h * (4 * T_HALF), 8), 4 * T_HALF)],
                                   src_v.at[pl.ds(0, 4 * T_HALF)], sem.at[0])
    gate_cp = pltpu.make_async_copy(gate_hbm.at[pl.ds(pl.multiple_of(h * T_HALF, 8), T_HALF)],
                                    gate_v.at[pl.ds(0, T_HALF)], sem.at[1])
    src_cp.start()
    gate_cp.start()
    pad_v[...] = jnp.full(pad_v.shape, T_HALF, jnp.int32)
    pad_cp = pltpu.make_async_copy(pad_v, tok_ref, sem.at[2])
    pad_cp.start()

    def weight_copies(x, slot):
        return [pltpu.make_async_copy(w_hbm.at[x], w_l.at[slot], wsem.at[slot, i])
                for i, (w_hbm, w_l) in enumerate(((wg_hbm, wg_l), (wu_hbm, wu_l), (wd_hbm, wd_l)))]

    n_tiles, n_live = _prepare_half(cnt_ref, tile_e, tile_first, elist, h)
    live[0] = 0
    live[1] = n_live

    for ahead in range(W_SLOTS - 1):
        @pl.when(ahead < n_live)
        def _(ahead=ahead):
            for cp in weight_copies(elist[ahead], ahead):
                cp.start()

    zero = jnp.zeros((8, 128), F32)
    src_v[pl.ds(4 * T_HALF, 8), :] = pltpu.pack_elementwise([zero, zero], packed_dtype=BF16)
    gate_v[pl.ds(T_HALF, 8), :] = zero
    ybuf0[...] = jnp.zeros(ybuf0.shape, F32)
    ybuf1[...] = jnp.zeros(ybuf1.shape, F32)
    pad_cp.wait()
    _invert_slots(slot_ref, tok_ref, acc_v, h)
    src_cp.wait()
    gate_cp.wait()

    def switch_expert():
        q = live[0]
        slot = q % W_SLOTS
        for cp in weight_copies(elist[q], slot):
            cp.wait()
        wgu_b[:, 0:D_EXPERT] = wg_l[slot].astype(BF16)
        wgu_b[:, D_EXPERT:] = wu_l[slot].astype(BF16)
        wd_b[...] = wd_l[slot].astype(BF16)
        live[0] = q + 1
        nxt = q + W_SLOTS - 1

        @pl.when(nxt < live[1])
        def _():
            for cp in weight_copies(elist[nxt], nxt % W_SLOTS):
                cp.start()

    def gather(j, xb, gb, rows=(0, MT)):
        base = j * MT
        for m in range(*rows):
            t = tok_ref[base + m]
            xb[pl.ds(m, 4, stride=MT_STRIDE), :] = src_v[pl.ds(pl.multiple_of(t * 4, 4), 4), :]
            gb[m:m + 1, :] = gate_v[pl.ds(t, 1), :]

    def scatter(j, yb, rows=(0, MT)):
        base = j * MT
        for b in range(rows[0] // RMW_BATCH, rows[1] // RMW_BATCH):
            ms = [b * RMW_BATCH + u for u in range(RMW_BATCH)]
            targets = [pl.ds(pl.multiple_of(tok_ref[base + m] * 8, 8), 8) for m in ms]
            vals = [acc_v[r, :] + yb[pl.ds(m, 8, stride=MT_STRIDE), :] for r, m in zip(targets, ms)]
            for r, v in zip(targets, vals):
                acc_v[r, :] = v

    quarters = [(q * MT // 4, (q + 1) * MT // 4) for q in range(4)]

    def step(j, p):
        pl.when(tile_first[j] == 1)(switch_expert)
        xb, gb, yb = xbuf[p], gbuf[p], ybuf[p]
        nxt = (j + 1, xbuf[1 - p], gbuf[1 - p])
        prv = (jnp.maximum(j - 1, 0), ybuf[1 - p])
        lo, hi = [], []
        for c in range(4):
            words = xb[MT_STRIDE * c:MT_STRIDE * c + MT, :]
            unpack = functools.partial(pltpu.unpack_elementwise, words, packed_dtype=BF16, unpacked_dtype=F32)
            lo.append(unpack(index=0).astype(BF16))
            hi.append(unpack(index=1).astype(BF16))
        x = jnp.concatenate(lo + hi, axis=1)
        gather(*nxt, rows=quarters[0])
        h_gate = _dot(x, wgu_b[:, 0:D_EXPERT])
        gather(*nxt, rows=quarters[1])
        h_up = _dot(x, wgu_b[:, D_EXPERT:])
        gather(*nxt, rows=quarters[2])
        g = gb[...]
        g_hi = g.astype(BF16)
        g_lo = (g - g_hi.astype(F32)).astype(BF16)
        pick = (lax.broadcasted_iota(jnp.int32, (128, D_EXPERT), 0) == tile_e[j]).astype(BF16)
        g_col = _dot(g_hi, pick) + _dot(g_lo, pick)
        a = (_silu(h_gate) * h_up * g_col).astype(BF16)
        gather(*nxt, rows=quarters[3])
        scatter(*prv, rows=quarters[0])
        y_lo = _dot(a, wd_b[:, 0:D_MODEL // 2])
        scatter(*prv, rows=quarters[1])
        y_hi = _dot(a, wd_b[:, D_MODEL // 2:])
        scatter(*prv, rows=quarters[2])
        for c in range(4):
            yb[MT_STRIDE * c:MT_STRIDE * c + MT, :] = y_lo[:, 128 * c:128 * (c + 1)]
            yb[MT_STRIDE * (c + 4):MT_STRIDE * (c + 4) + MT, :] = y_hi[:, 128 * c:128 * (c + 1)]
        scatter(*prv, rows=quarters[3])

    gather(0, xbuf[0], gbuf[0])
    n_pairs = (n_tiles + 1) // 2

    def pair(i, carry):
        step(2 * i, 0)
        step(2 * i + 1, 1)
        return carry

    lax.fori_loop(0, n_pairs, pair, 0)
    scatter(jnp.maximum(2 * n_pairs - 1, 0), ybuf[1])

    n_fin = T_HALF // FIN

    def first_token(c):
        return h * T_HALF + c * FIN

    def base_copy(c, slot):
        return pltpu.make_async_copy(base_hbm.at[pl.ds(pl.multiple_of(first_token(c), FIN), FIN)],
                                     base_buf.at[slot], fsem.at[0, slot])

    def on_out_copy(c, action):
        t0 = first_token(c)
        slot = c % FIN_OUT_SLOTS

        @pl.when(t0 < T_CTX)
        def _():
            action(pltpu.make_async_copy(out_buf.at[slot], outc_hbm.at[pl.ds(pl.multiple_of(t0, FIN), FIN)],
                                         fsem.at[1, slot]))

        @pl.when(t0 >= T_CTX)
        def _():
            action(pltpu.make_async_copy(out_buf.at[slot],
                                         outl_hbm.at[pl.ds(pl.multiple_of(t0 - T_CTX, FIN), FIN)],
                                         fsem.at[1, slot]))

    for c in range(FIN_BASE_SLOTS - 1):
        base_copy(c, c).start()

    def combine(c, carry):
        slot = c % FIN_BASE_SLOTS
        oslot = c % FIN_OUT_SLOTS
        base_copy(c, slot).wait()
        ahead = c + FIN_BASE_SLOTS - 1

        @pl.when(ahead < n_fin)
        def _():
            base_copy(ahead, ahead % FIN_BASE_SLOTS).start()

        @pl.when(c >= FIN_OUT_SLOTS)
        def _():
            on_out_copy(c - FIN_OUT_SLOTS, lambda cp: cp.wait())

        t0 = first_token(c)
        mod_row = jnp.where(t0 < T_CTX, 0, 1 + (t0 - T_CTX) // DEC_SEQ)
        acc_row = pl.multiple_of(8 * FIN * c, 8)
        for cc in range(8):
            cs = slice(128 * cc, 128 * (cc + 1))
            routed = acc_v[pl.ds(acc_row + cc, FIN, stride=8), :]
            out_buf[oslot, :, cs] = base_buf[slot, :, cs] + mod_ref[mod_row, 5:6, cs] * routed
        on_out_copy(c, lambda cp: cp.start())
        return carry

    lax.fori_loop(0, n_fin, combine, 0)
    for c in range(n_fin - FIN_OUT_SLOTS, n_fin):
        on_out_copy(c, lambda cp: cp.wait())


def _moe(slot_words, cnt, src, gate, wg, wu, wd, base, mod3):
    any_spec = pl.BlockSpec(memory_space=pl.ANY)
    tile_buf = lambda rows, dt: pltpu.VMEM((rows * MT_STRIDE, 128), dt)
    return pl.pallas_call(
        _moe_kernel,
        grid_spec=pltpu.PrefetchScalarGridSpec(
            num_scalar_prefetch=2,
            grid=(2,),
            in_specs=[any_spec] * 6 + [pl.BlockSpec(mod3.shape, lambda h, *_: (0, 0, 0))],
            out_specs=[any_spec, any_spec],
            scratch_shapes=[pltpu.VMEM((4 * T_HALF + 8, 128), jnp.uint32),
                            pltpu.VMEM((T_HALF + 8, 128), F32),
                            pltpu.VMEM((8 * T_HALF + 8, 128), F32),
                            pltpu.VMEM((W_SLOTS, D_MODEL, D_EXPERT), F32),
                            pltpu.VMEM((W_SLOTS, D_MODEL, D_EXPERT), F32),
                            pltpu.VMEM((W_SLOTS, D_EXPERT, D_MODEL), F32),
                            pltpu.VMEM((D_MODEL, 2 * D_EXPERT), BF16),
                            pltpu.VMEM((D_EXPERT, D_MODEL), BF16),
                            tile_buf(4, jnp.uint32), tile_buf(4, jnp.uint32),
                            pltpu.VMEM((MT, 128), F32), pltpu.VMEM((MT, 128), F32),
                            tile_buf(8, F32), tile_buf(8, F32),
                            pltpu.SMEM((SLOT_CAP,), jnp.int32),
                            pltpu.SMEM((NT_MAX + 2,), jnp.int32),
                            pltpu.SMEM((NT_MAX + 2,), jnp.int32),
                            pltpu.SMEM((N_EXPERTS,), jnp.int32),
                            pltpu.SMEM((2,), jnp.int32),
                            pltpu.VMEM((FIN_BASE_SLOTS, FIN, D_MODEL), F32),
                            pltpu.VMEM((FIN_OUT_SLOTS, FIN, D_MODEL), F32),
                            pltpu.VMEM((SLOT_CAP,), jnp.int32),
                            pltpu.SemaphoreType.DMA((3,)),
                            pltpu.SemaphoreType.DMA((W_SLOTS, 3)),
                            pltpu.SemaphoreType.DMA((2, FIN_BASE_SLOTS))]),
        out_shape=[jax.ShapeDtypeStruct((T_CTX, D_MODEL), F32), jax.ShapeDtypeStruct((T_LAT, D_MODEL), F32)],
        compiler_params=_cparams(("arbitrary",)),
        name="moe",
    )(slot_words, cnt, src, gate, wg, wu, wd, base, mod3)


def _head_indicators(width):
    head = jnp.arange(width) // DH_ATT
    ind = (head[:, None] == jnp.arange(128)[None, :]).astype(BF16)
    return ind, ind.T


def kernel(x_prompt, x_sample, cache_attn_k, cache_attn_v, state_mlstm_c, state_mlstm_n, state_mlstm_m, c, c_ctx, w_mod, b_mod, norm1_w, norm2_w, w_in, q_norm_w, k_norm_w, b_gates, m_norm_w, w_out, router_w, router_bias, w_gate, w_up, w_down, ws_gate, ws_up, ws_down):
    x = (x_prompt.reshape(T_CTX, D_MODEL), x_sample.reshape(T_LAT, D_MODEL))
    mod3 = _modulation(c, c_ctx, w_mod[0], b_mod[0])

    w_main = w_in[0]
    w_gates_t = w_in[0, :, P_MAIN:].T.astype(BF16)
    qw = jnp.tile(q_norm_w, (1, N_HEADS_ATT))
    kw = jnp.tile(k_norm_w, (1, N_KV_HEADS))
    inds = _head_indicators(512) + _head_indicators(128)
    qn, kn, va, qm, km, vm, om, gt, new_k, new_v = _inproj(*x, mod3, norm1_w, w_main, w_gates_t, qw, kw, inds,
                                                           _rope_tables())

    att_c = _attention(qn, kn, va, None, n_batch=BATCH, seq=SEQ, first_row=0)
    att_l = _attention(qn, kn, va, (cache_attn_k, cache_attn_v), n_batch=DEC_BATCH, seq=DEC_SEQ,
                       first_row=T_CTX)

    g3 = gt.reshape(16, T_ALL // CHUNK, CHUNK).transpose(1, 0, 2)
    bg = b_gates.reshape(16, 1)
    hm_c, c_new, n_new, m_new = _mlstm(qm, km, vm, g3, bg, om, m_norm_w, None,
                                       n_batch=BATCH, seq=SEQ, first_block=0)
    m0 = jnp.broadcast_to(state_mlstm_m.reshape(DEC_BATCH, 8, 1), (DEC_BATCH, 8, CHUNK))
    hm_l = _mlstm(qm, km, vm, g3, bg, om, m_norm_w, (state_mlstm_c, state_mlstm_n, m0),
                  n_batch=DEC_BATCH, seq=DEC_SEQ, first_block=N_CTX_BLOCKS)

    rw_t = router_w[0].T
    rw_hi = rw_t.astype(BF16)
    rw_lo = (rw_t - rw_hi.astype(F32)).astype(BF16)
    base, hn2, logits_t = _outproj((att_c, att_l), (hm_c, hm_l), x, mod3, norm2_w, w_out[0].astype(BF16),
                                   rw_hi, rw_lo, ws_gate[0].astype(BF16), ws_up[0].astype(BF16),
                                   ws_down[0].astype(BF16))
    gate, idx8 = _router(logits_t, router_bias.reshape(N_EXPERTS, 1))
    slot_words, cnt = _plan(idx8)
    out_c, out_l = _moe(slot_words.reshape(TOP_K * T_HALF), cnt[:, :, 0].reshape(2 * N_EXPERTS),
                        hn2, gate, w_gate[0], w_up[0], w_down[0], base, mod3)

    y_prompt = out_c.reshape(BATCH, SEQ, D_MODEL)
    y_sample = out_l.reshape(DEC_BATCH, DEC_SEQ, D_MODEL)
    new_m = m_new[:, :, 0].reshape(BATCH, 1, 2, N_HEADS_M)
    to_cache = lambda a: a.transpose(0, 1, 3, 2)[:, None]
    return (y_prompt, y_sample, to_cache(new_k), to_cache(new_v), c_new[:, None], n_new[:, None], new_m)
```

```python
import functools

import jax
import jax.numpy as jnp
from jax import lax
from jax.experimental import pallas as pl
from jax.experimental.pallas import tpu as pltpu

F32 = jnp.float32
BF16 = jnp.bfloat16

D_MODEL = 1024
BATCH = 32
SEQ = 256
DEC_BATCH = 2
DEC_SEQ = 1024
PAST_LEN = 256
GRID_W = 64
N_HEADS_ATT = 8
N_KV_HEADS = 2
DH_ATT = 64
D_ATT = 512
ROPE_THETA = 10000.0
N_HEADS_M = 4
DH_M = 128
D_M = 512
CHUNK = 128
N_EXPERTS = 64
TOP_K = 8
N_GROUPS = 8
TOPK_GROUPS = 4
D_EXPERT = 256
ROUTED_SCALE = 2.5
EPS = 1e-6
NEG_INIT = -1e30

T_CTX = BATCH * SEQ
T_LAT = DEC_BATCH * DEC_SEQ
T_ALL = T_CTX + T_LAT
TB = 256
N_CTX_BLOCKS = T_CTX // TB
TBP = 512
NP_CTX = T_CTX // TBP
NP_LAT_PER_BATCH = DEC_SEQ // TBP
TB_MOE = 1024
T_HALF = T_ALL // 2
MT = 256
MT_STRIDE = MT + 8
NT_MAX = TOP_K * T_HALF // MT + N_EXPERTS
SLOT_CAP = -(-(NT_MAX + 2) * MT // 1024) * 1024
RMW_BATCH = 16
INV_UNROLL = 16
W_SLOTS = 2
FIN = 128
FIN_BASE_SLOTS = 6
FIN_OUT_SLOTS = 4
assert FIN_OUT_SLOTS <= FIN_BASE_SLOTS
assert SLOT_CAP < 2 ** 16 and T_HALF % INV_UNROLL == 0 and MT % RMW_BATCH == 0
P_MAIN = 2816
VMEM_LIMIT = 56 * 1024 * 1024

_NT = (((1,), (1,)), ((), ()))


def _cparams(sem):
    return pltpu.CompilerParams(dimension_semantics=sem, vmem_limit_bytes=VMEM_LIMIT)


def _split3(x):
    hi = x.astype(BF16)
    r1 = x - hi.astype(F32)
    mid = r1.astype(BF16)
    lo = (r1 - mid.astype(F32)).astype(BF16)
    return hi, mid, lo


def _dot(a, b):
    return jnp.dot(a, b, preferred_element_type=F32)


def _dot3(x, m_bf16):
    hi, mid, lo = _split3(x)
    return _dot(hi, m_bf16) + _dot(mid, m_bf16) + _dot(lo, m_bf16)


def _dot2(x, m_bf16):
    hi = x.astype(BF16)
    lo = (x - hi.astype(F32)).astype(BF16)
    return _dot(hi, m_bf16) + _dot(lo, m_bf16)


def _silu(x):
    return x * jax.nn.sigmoid(x)


def _mod_row(i):
    return jnp.where(i < NP_CTX, 0, 1 + (i - NP_CTX) // NP_LAT_PER_BATCH)


def _mod_kernel(ct_ref, w_ref, b_ref, o_ref):
    s = _silu(ct_ref[...])
    w = w_ref[...]
    rows = [jnp.sum(w * s[:, r:r + 1], axis=0, keepdims=True) for r in range(3)]
    rows.append(jnp.zeros((5, w.shape[1]), F32))
    o_ref[...] = jnp.concatenate(rows, axis=0) + b_ref[...]


def _modulation(c, c_ctx, w_mod, b_mod):
    cvec = jnp.concatenate([c_ctx[None, :], c, jnp.zeros((5, D_MODEL), F32)], axis=0)
    nb = 1024
    out = pl.pallas_call(
        _mod_kernel,
        grid=(6 * D_MODEL // nb,),
        in_specs=[pl.BlockSpec((D_MODEL, 8), lambda j: (0, 0)),
                  pl.BlockSpec((D_MODEL, nb), lambda j: (0, j)),
                  pl.BlockSpec((1, nb), lambda j: (0, j))],
        out_specs=pl.BlockSpec((8, nb), lambda j: (0, j)),
        out_shape=jax.ShapeDtypeStruct((8, 6 * D_MODEL), F32),
        compiler_params=_cparams(("arbitrary",)),
        name="modulation",
    )(cvec.T, w_mod, b_mod[None, :])
    return out.reshape(8, 6, D_MODEL)


def _ctx_or_lat(ctx_ref, lat_ref):
    return jnp.where(pl.program_id(0) < NP_CTX, ctx_ref[...], lat_ref[...])


def _split_specs(width):
    return [pl.BlockSpec((TBP, width), lambda i: (jnp.minimum(i, NP_CTX - 1), 0)),
            pl.BlockSpec((TBP, width), lambda i: (jnp.maximum(i - NP_CTX, 0), 0))]


def _head_norm(x, ind, ind_t, w_row):
    ss = _dot2(x * x, ind)
    inv = lax.rsqrt(ss * (1.0 / DH_ATT) + EPS)
    return x * _dot2(inv, ind_t) * w_row


def _rope(x, cos, sin_signed):
    lane = lax.broadcasted_iota(jnp.int32, x.shape, 1)
    partner = jnp.where((lane % 32) < 16, pltpu.roll(x, 128 - 16, 1), pltpu.roll(x, 16, 1))
    return x * cos + partner * sin_signed


def _inproj_kernel(xc_ref, xl_ref, mod_ref, nw_ref, w_ref, wgt_ref, qw_ref, kw_ref, iq_ref, iqt_ref,
                   ik_ref, ikt_ref, cos_ref, sin_ref,
                   qn_ref, kn_ref, va_ref, qm_ref, km_ref, vm_ref, om_ref, gt_ref, kc_ref, vc_ref, w_b):
    @pl.when(pl.program_id(0) == 0)
    def _():
        for j in range(P_MAIN // 128):
            w_b[128 * j:128 * (j + 1), :] = w_ref[128 * j:128 * (j + 1), :].astype(BF16)

    x = _ctx_or_lat(xc_ref, xl_ref)
    y = x * lax.rsqrt(jnp.mean(x * x, axis=-1, keepdims=True) + EPS) * nw_ref[...]
    hn = y * (1.0 + mod_ref[1:2, :]) + mod_ref[0:1, :]
    hb = hn.astype(BF16)
    cols = lambda a, b: w_b[a:b, :]
    pdot = lambda t, w_rows: lax.dot_general(t, w_rows, _NT, preferred_element_type=F32)
    qn = _head_norm(pdot(hb, cols(0, 512)), iq_ref[...], iqt_ref[...], qw_ref[...])
    kn = _head_norm(pdot(hb, cols(512, 640)), ik_ref[...], ikt_ref[...], kw_ref[...])
    qn_ref[...] = qn.astype(BF16)
    kn_ref[...] = kn

    @pl.when(pl.program_id(0) >= NP_CTX)
    def _():
        cos, sin = cos_ref[...], sin_ref[...]
        for j in range(4):
            qn_ref[:, 128 * j:128 * (j + 1)] = _rope(qn[:, 128 * j:128 * (j + 1)], cos, sin).astype(BF16)
        kn_ref[...] = _rope(kn, cos, sin)

    va = pdot(hb, cols(640, 768))
    va_ref[...] = va

    @pl.when(pl.program_id(0) < NP_CTX)
    def _():
        for bb in range(TBP // SEQ):
            rows = slice(SEQ * bb, SEQ * (bb + 1))
            kn_t, va_t = kn[rows, :].T, va[rows, :].T
            for g in range(N_KV_HEADS):
                dims = slice(DH_ATT * g, DH_ATT * (g + 1))
                kc_ref[bb, g] = kn_t[dims, :]
                vc_ref[bb, g] = va_t[dims, :]

    qm_ref[...] = pdot(hb, cols(768, 1280)).astype(BF16)
    km_ref[...] = (pdot(hb, cols(1280, 1792)) * (DH_M ** -0.5)).astype(BF16)
    vm_ref[...] = pdot(hb, cols(1792, 2304)).astype(BF16)
    om_ref[...] = pdot(hb, cols(2304, 2816))
    gt_ref[...] = lax.dot_general(wgt_ref[...], hb, _NT, preferred_element_type=F32)


def _inproj(x_ctx, x_lat, mod3, norm1_w, w_main, w_gates_t, qw, kw, inds, rope_tabs):
    tok = lambda w: pl.BlockSpec((TBP, w), lambda i: (i, 0))
    full = lambda a: pl.BlockSpec(a.shape, lambda i: (0,) * a.ndim)
    sd = lambda w, dt: jax.ShapeDtypeStruct((T_ALL, w), dt)
    rope_spec = pl.BlockSpec((TBP, 128), lambda i: (jnp.maximum(i - NP_CTX, 0) % NP_LAT_PER_BATCH, 0))
    cache_spec = pl.BlockSpec((TBP // SEQ, N_KV_HEADS, DH_ATT, SEQ),
                              lambda i: (jnp.minimum(i, NP_CTX - 1), 0, 0, 0))
    cache_shape = jax.ShapeDtypeStruct((BATCH, N_KV_HEADS, DH_ATT, SEQ), F32)
    return pl.pallas_call(
        _inproj_kernel,
        grid=(T_ALL // TBP,),
        in_specs=_split_specs(D_MODEL) + [
                  pl.BlockSpec((None, 6, D_MODEL), lambda i: (_mod_row(i), 0, 0)),
                  full(norm1_w), full(w_main), full(w_gates_t), full(qw), full(kw)]
                 + [full(a) for a in inds] + [rope_spec, rope_spec],
        out_specs=[tok(512), tok(128), tok(128), tok(512), tok(512), tok(512), tok(512),
                   pl.BlockSpec((16, TBP), lambda i: (0, i)), cache_spec, cache_spec],
        out_shape=[sd(512, BF16), sd(128, F32), sd(128, F32), sd(512, BF16), sd(512, BF16),
                   sd(512, BF16), sd(512, F32), jax.ShapeDtypeStruct((16, T_ALL), F32),
                   cache_shape, cache_shape],
        scratch_shapes=[pltpu.VMEM((P_MAIN, D_MODEL), BF16)],
        compiler_params=_cparams(("arbitrary",)),
        name="inproj",
    )(x_ctx, x_lat, mod3, norm1_w, w_main, w_gates_t, qw, kw, *inds, *rope_tabs)


def _rope_tables():
    t = jnp.arange(DEC_SEQ)
    pos = jnp.stack([t // GRID_W, t % GRID_W], axis=1).astype(F32)
    n_freq = DH_ATT // 4
    inv_freq = ROPE_THETA ** (-jnp.arange(n_freq, dtype=F32) / n_freq)
    ang = pos[:, :, None] * inv_freq
    cos, sin = jnp.cos(ang), jnp.sin(ang)
    cos_h = jnp.stack([cos, cos], axis=2).reshape(DEC_SEQ, DH_ATT)
    sin_h = jnp.stack([-sin, sin], axis=2).reshape(DEC_SEQ, DH_ATT)
    return jnp.tile(cos_h, (1, 2)), jnp.tile(sin_h, (1, 2))


def _attn_kernel(*refs, has_cache):
    if has_cache:
        q_ref, k_ref, v_ref, kc_ref, vc_ref, o_ref = refs
    else:
        q_ref, k_ref, v_ref, o_ref = refs
    q = q_ref[...] * jnp.asarray(DH_ATT ** -0.5, BF16)
    k = k_ref[...].astype(BF16)
    v = v_ref[...].astype(BF16)
    qb = q.shape[0]
    low_half = lax.broadcasted_iota(jnp.int32, (1, 128), 1) < DH_ATT
    for g in range(N_KV_HEADS):
        kg = k[:, DH_ATT * g:DH_ATT * (g + 1)]
        vg = v[:, DH_ATT * g:DH_ATT * (g + 1)]
        if has_cache:
            kg = jnp.concatenate([kg, kc_ref[g].astype(BF16)], axis=0)
            vg = jnp.concatenate([vg, vc_ref[g].astype(BF16)], axis=0)
        n_keys = kg.shape[0]
        zero = jnp.zeros_like(kg)
        k2 = jnp.concatenate([jnp.concatenate([kg, zero], axis=1), jnp.concatenate([zero, kg], axis=1)], axis=0)
        v2 = jnp.concatenate([jnp.concatenate([vg, zero], axis=1), jnp.concatenate([zero, vg], axis=1)], axis=0)
        ones2 = jnp.concatenate([jnp.broadcast_to(jnp.where(low_half, 1.0, 0.0), (n_keys, 128)),
                                 jnp.broadcast_to(jnp.where(low_half, 0.0, 1.0), (n_keys, 128))],
                                axis=0).astype(BF16)
        qp = jnp.concatenate([q[:, 256 * g:256 * g + 128], q[:, 256 * g + 128:256 * g + 256]], axis=0)
        s = lax.dot_general(qp, k2, _NT, preferred_element_type=F32)
        halves = [s[:, 0:n_keys], s[:, n_keys:]]
        p = jnp.concatenate([jnp.exp(sh - jnp.max(sh, axis=-1, keepdims=True)) for sh in halves],
                            axis=1).astype(BF16)
        ov = _dot(p, jnp.concatenate([v2, ones2], axis=1))
        o = ov[:, 0:128] / ov[:, 128:]
        o_ref[:, 256 * g:256 * g + 128] = o[0:qb, :].astype(BF16)
        o_ref[:, 256 * g + 128:256 * g + 256] = o[qb:, :].astype(BF16)


def _attention(qn, kn, v_all, cache, *, n_batch, seq, first_row):
    qblocks = seq // TB
    kv_spec = pl.BlockSpec((seq, 128), lambda b, i: (b + first_row // seq, 0))
    in_specs = [pl.BlockSpec((TB, 512), lambda b, i: (first_row // TB + b * qblocks + i, 0)), kv_spec, kv_spec]
    args = [qn, kn, v_all]
    if cache is not None:
        cspec = pl.BlockSpec((None, None, N_KV_HEADS, PAST_LEN, DH_ATT), lambda b, i: (b, 0, 0, 0, 0))
        in_specs += [cspec, cspec]
        args += list(cache)
    return pl.pallas_call(
        functools.partial(_attn_kernel, has_cache=cache is not None),
        grid=(n_batch, qblocks),
        in_specs=in_specs,
        out_specs=pl.BlockSpec((TB, 512), lambda b, i: (b * qblocks + i, 0)),
        out_shape=jax.ShapeDtypeStruct((n_batch * seq, 512), BF16),
        compiler_params=_cparams(("arbitrary", "arbitrary")),
        name="attention_lat" if cache is not None else "attention_ctx",
    )(*args)


def _log_sigmoid(x):
    return jnp.minimum(x, 0.0) - jnp.log1p(jnp.exp(-jnp.abs(x)))


def _col_bcast(cols, j):
    return jnp.broadcast_to(cols[:, j:j + 1], (CHUNK, CHUNK))


def _mlstm_kernel(*refs, has_state, n_chunks):
    n_in = 10 if has_state else 7
    n_out = 1 if has_state else 4
    q_ref, k_ref, v_ref, g_ref, bg_ref, om_ref, nw_ref = refs[:7]
    hm_ref = refs[n_in]
    scratch = refs[n_in + n_out:]
    st = scratch[0:8]
    ms = scratch[8:16]
    hdir = scratch[16:18]
    rows_sc, cmb_sc, bb_sc, kt_sc = scratch[18:22]
    neg_inf = F32(-jnp.inf)
    zeros112 = jnp.zeros((CHUNK - 16, CHUNK), F32)
    sub = lax.broadcasted_iota(jnp.int32, (CHUNK, CHUNK), 0)
    lan = lax.broadcasted_iota(jnp.int32, (CHUNK, CHUNK), 1)
    ones = jnp.ones((CHUNK, CHUNK), BF16)

    stats = []
    for c in range(n_chunks):
        pre = g_ref[c] + bg_ref[...]
        logf = _log_sigmoid(pre)
        for d in range(2):
            bcum = _dot3(logf, ((sub <= lan) if d == 0 else (sub >= lan)).astype(BF16))
            li4 = pre[4 * d:4 * d + 4, :]
            lf4 = logf[8 + 4 * d:12 + 4 * d, :]
            b4 = bcum[8 + 4 * d:12 + 4 * d, :]
            stats.append((c, d, li4 - b4, lf4, b4))
    for c, d, r4, lf4, b4 in stats:
        blast4 = b4[:, CHUNK - 1:CHUNK] if d == 0 else b4[:, 0:1]
        wlog4 = blast4 + r4
        wmax4 = jnp.max(wlog4, axis=-1, keepdims=True)
        full = lambda a: jnp.broadcast_to(a, (4, CHUNK))
        rows_sc[2 * c + d] = jnp.concatenate(
            [r4, wlog4, full(blast4), full(wmax4), jnp.zeros((16, CHUNK), F32)], axis=0)
    for c in range(n_chunks):
        for h in range(N_HEADS_M):
            kc = k_ref[CHUNK * c:CHUNK * (c + 1), DH_M * h:DH_M * (h + 1)]
            kt_sc[c * N_HEADS_M + h] = kc.astype(F32).T.astype(BF16)
    for c, d, r4, lf4, b4 in stats:
        within = (lan <= sub) if d == 0 else (lan >= sub)
        for h in range(N_HEADS_M):
            cummax = jnp.max(jnp.where(within, r4[h:h + 1, :], neg_inf), axis=-1, keepdims=True)
            cmb_sc[(2 * c + d) * N_HEADS_M + h] = jnp.broadcast_to(cummax, (CHUNK, CHUNK))
    for c, d, r4, lf4, b4 in stats:
        within = (lan <= sub) if d == 0 else (lan >= sub)
        for h in range(N_HEADS_M):
            bb_sc[(2 * c + d) * N_HEADS_M + h] = _dot2(jnp.where(within, lf4[h:h + 1, :], 0.0), ones)

    for d in range(2):
        if has_state:
            c0_ref, n0_ref, m0_ref = refs[7:10]
            ncols = jnp.concatenate([n0_ref[d], jnp.zeros((12, DH_M), F32), zeros112], axis=0).T
        for h in range(N_HEADS_M):
            idx = 4 * d + h
            if has_state:
                st[idx][:, 0:DH_M] = c0_ref[d, h]
                st[idx][:, DH_M:] = _col_bcast(ncols, h)
                ms[idx][0:1, :] = m0_ref[idx:idx + 1, :]
            else:
                st[idx][...] = jnp.zeros((DH_M, 2 * DH_M), F32)
                ms[idx][0:1, :] = jnp.full((1, CHUNK), NEG_INIT, F32)

    def chunk_step(i, carry):
        for d in range(2):
            c = i if d == 0 else n_chunks - 1 - i
            r0 = pl.multiple_of(c * CHUNK, CHUNK)
            rows = rows_sc[2 * c + d]
            mask = (lan <= sub) if d == 0 else (lan >= sub)
            heads = []
            for h in range(N_HEADS_M):
                idx = 4 * d + h
                hs = slice(DH_M * h, DH_M * (h + 1))
                qc = q_ref[pl.ds(r0, CHUNK), hs]
                kt = kt_sc[c * N_HEADS_M + h]
                state = st[idx][...]
                heads.append(dict(idx=idx, hs=hs, qc=qc, kt=kt, state=state,
                                  s_raw=_dot(qc, kt), qs=_dot(qc, state.astype(BF16))))
            for h, hd in enumerate(heads):
                mp = ms[hd["idx"]][0:1, :]
                m_b = jnp.maximum(mp, cmb_sc[(2 * c + d) * N_HEADS_M + h])
                s = hd["s_raw"] * jnp.exp(jnp.where(mask, rows[h:h + 1, :] - m_b, neg_inf))
                s_hi = s.astype(BF16)
                vc = v_ref[pl.ds(r0, CHUNK), hd["hs"]]
                v_ones = jnp.concatenate([vc, ones], axis=1)
                hd.update(mp=mp, m_b=m_b, v_ones=v_ones, sv=_dot(s_hi, v_ones),
                          s_lo_sum=_dot((s - s_hi.astype(F32)).astype(BF16), ones))
            for h, hd in enumerate(heads):
                mp, m_b = hd["mp"], hd["m_b"]
                gw_b = jnp.exp(mp - m_b)
                en_b = jnp.exp(-(bb_sc[(2 * c + d) * N_HEADS_M + h] + m_b))
                num = hd["sv"][:, 0:DH_M] + gw_b * hd["qs"][:, 0:DH_M]
                den = hd["sv"][:, DH_M:] + hd["s_lo_sum"] + gw_b * hd["qs"][:, DH_M:]
                hdir[d][pl.ds(r0, CHUNK), hd["hs"]] = num / jnp.maximum(jnp.abs(den), en_b)
            for h, hd in enumerate(heads):
                mp = hd["mp"]
                gend = rows[8 + h:9 + h, :] + mp
                mnew = jnp.maximum(gend, rows[12 + h:13 + h, :])
                w_row = jnp.exp(rows[4 + h:5 + h, :] - mnew)
                kw_t = (hd["kt"].astype(F32) * w_row).astype(BF16)
                dec = jnp.exp(gend - mnew)
                st[hd["idx"]][...] = (jnp.concatenate([dec, dec], axis=1) * hd["state"]
                                      + _dot(kw_t, hd["v_ones"]))
                ms[hd["idx"]][0:1, :] = mnew
        return carry

    lax.fori_loop(0, n_chunks, chunk_step, 0)

    for h in range(N_HEADS_M):
        hs = slice(DH_M * h, DH_M * (h + 1))
        hh = hdir[0][:, hs] + hdir[1][:, hs]
        y = hh * lax.rsqrt(jnp.mean(hh * hh, axis=-1, keepdims=True) + EPS) * nw_ref[:, hs]
        hm_ref[:, hs] = (jax.nn.sigmoid(om_ref[:, hs]) * y).astype(BF16)
    if not has_state:
        c_ref, n_ref, m_ref = refs[n_in + 1:n_in + 4]
        for d in range(2):
            for h in range(N_HEADS_M):
                idx = 4 * d + h
                c_ref[d, h] = st[idx][:, 0:DH_M]
                n_ref[d, h:h + 1, :] = st[idx][:, DH_M:].T[0:1, :]
                m_ref[idx:idx + 1, :] = ms[idx][0:1, :]


def _mlstm(qm, km, vm, g3, bg, om, nw, state, *, n_batch, seq, first_block):
    n_chunks = seq // CHUNK
    rb = seq // TB
    tok = lambda: pl.BlockSpec((seq, 512), lambda b: (b + first_block // rb, 0))
    full = lambda a: pl.BlockSpec(a.shape, lambda b: (0,) * a.ndim)
    in_specs = [tok(), tok(), tok(),
                pl.BlockSpec((n_chunks, 16, CHUNK), lambda b: (b + first_block // rb, 0, 0)),
                full(bg), tok(), full(nw)]
    args = [qm, km, vm, g3, bg, om, nw]
    hm_spec = pl.BlockSpec((seq, 512), lambda b: (b, 0))
    hm_shape = jax.ShapeDtypeStruct((n_batch * seq, 512), BF16)
    scratch = ([pltpu.VMEM((DH_M, 2 * DH_M), F32)] * 8 + [pltpu.VMEM((8, CHUNK), F32)] * 8
               + [pltpu.VMEM((seq, 512), F32)] * 2
               + [pltpu.VMEM((2 * n_chunks, 32, CHUNK), F32),
                  pltpu.VMEM((8 * n_chunks, CHUNK, CHUNK), F32),
                  pltpu.VMEM((8 * n_chunks, CHUNK, CHUNK), F32),
                  pltpu.VMEM((N_HEADS_M * n_chunks, DH_M, CHUNK), BF16)])
    if state is not None:
        c0, n0, m0 = state
        in_specs += [pl.BlockSpec((None, None, 2, N_HEADS_M, DH_M, DH_M), lambda b: (b, 0, 0, 0, 0, 0)),
                     pl.BlockSpec((None, None, 2, N_HEADS_M, DH_M), lambda b: (b, 0, 0, 0, 0)),
                     pl.BlockSpec((None, 8, CHUNK), lambda b: (b, 0, 0))]
        args += [c0, n0, m0]
        out_specs, out_shape = hm_spec, hm_shape
    else:
        out_specs = [hm_spec,
                     pl.BlockSpec((None, 2, N_HEADS_M, DH_M, DH_M), lambda b: (b, 0, 0, 0, 0)),
                     pl.BlockSpec((None, 2, N_HEADS_M, DH_M), lambda b: (b, 0, 0, 0)),
                     pl.BlockSpec((None, 8, CHUNK), lambda b: (b, 0, 0))]
        out_shape = [hm_shape,
                     jax.ShapeDtypeStruct((n_batch, 2, N_HEADS_M, DH_M, DH_M), F32),
                     jax.ShapeDtypeStruct((n_batch, 2, N_HEADS_M, DH_M), F32),
                     jax.ShapeDtypeStruct((n_batch, 8, CHUNK), F32)]
    return pl.pallas_call(
        functools.partial(_mlstm_kernel, has_state=state is not None, n_chunks=n_chunks),
        grid=(n_batch,),
        in_specs=in_specs,
        out_specs=out_specs,
        out_shape=out_shape,
        scratch_shapes=scratch,
        compiler_params=_cparams(("arbitrary",)),
        name="mlstm_lat" if state is not None else "mlstm_ctx",
    )(*args)


def _outproj_kernel(attc_ref, attl_ref, hmc_ref, hml_ref, xc_ref, xl_ref, mod_ref, nw_ref, wo_ref,
                    rwh_ref, rwl_ref, wsg_ref, wsu_ref, wsd_ref, base_ref, hn_ref, lt_ref):
    y = (_dot(_ctx_or_lat(attc_ref, attl_ref), wo_ref[0:D_ATT, :])
         + _dot(_ctx_or_lat(hmc_ref, hml_ref), wo_ref[D_ATT:, :]))
    x1 = _ctx_or_lat(xc_ref, xl_ref) + mod_ref[2:3, :] * y
    z = x1 * lax.rsqrt(jnp.mean(x1 * x1, axis=-1, keepdims=True) + EPS) * nw_ref[...]
    hn = z * (1.0 + mod_ref[4:5, :]) + mod_ref[3:4, :]
    hb = hn.astype(BF16)
    hl = (hn - hb.astype(F32)).astype(BF16)
    nt = lambda w, t: lax.dot_general(w, t, _NT, preferred_element_type=F32)
    lt_ref[...] = nt(rwh_ref[...], hb) + nt(rwl_ref[...], hb) + nt(rwh_ref[...], hl)
    a = _silu(_dot(hb, wsg_ref[...])) * _dot(hb, wsu_ref[...])
    shared = _dot(a.astype(BF16), wsd_ref[...])
    base_ref[...] = x1 + mod_ref[5:6, :] * shared
    packed = pltpu.pack_elementwise([hn[:, :512], hn[:, 512:]], packed_dtype=BF16)
    for c in range(4):
        hn_ref[pl.ds(c, TBP, stride=4), :] = packed[:, 128 * c:128 * (c + 1)]


def _outproj(att, hm, x, mod3, norm2_w, w_out, rw_hi, rw_lo, wsg, wsu, wsd):
    tok = lambda w: pl.BlockSpec((TBP, w), lambda i: (i, 0))
    full = lambda a: pl.BlockSpec(a.shape, lambda i: (0,) * a.ndim)
    return pl.pallas_call(
        _outproj_kernel,
        grid=(T_ALL // TBP,),
        in_specs=_split_specs(512) + _split_specs(512) + _split_specs(D_MODEL) + [
                  pl.BlockSpec((None, 6, D_MODEL), lambda i: (_mod_row(i), 0, 0)),
                  full(norm2_w), full(w_out), full(rw_hi), full(rw_lo), full(wsg), full(wsu), full(wsd)],
        out_specs=[tok(D_MODEL), pl.BlockSpec((4 * TBP, 128), lambda i: (i, 0)),
                   pl.BlockSpec((N_EXPERTS, TBP), lambda i: (0, i))],
        out_shape=[jax.ShapeDtypeStruct((T_ALL, D_MODEL), F32),
                   jax.ShapeDtypeStruct((4 * T_ALL, 128), jnp.uint32),
                   jax.ShapeDtypeStruct((N_EXPERTS, T_ALL), F32)],
        compiler_params=_cparams(("arbitrary",)),
        name="outproj",
    )(*att, *hm, *x, mod3, norm2_w, w_out, rw_hi, rw_lo, wsg, wsu, wsd)


def _first_max(vals, ids, limit):
    m = functools.reduce(jnp.maximum, [jnp.max(v, axis=0, keepdims=True) for v in vals])
    cand = [jnp.min(jnp.where(v == m, i, limit), axis=0, keepdims=True) for v, i in zip(vals, ids)]
    return m, functools.reduce(jnp.minimum, cand)


def _router_kernel(lt_ref, bias_ref, gate_ref, idx_ref):
    n = lt_ref.shape[1]
    score = jax.nn.sigmoid(lt_ref[...])
    biased = score + bias_ref[...]
    sub = lax.broadcasted_iota(jnp.int32, (8, n), 0).astype(F32)
    neg_inf = F32(-jnp.inf)
    slabs = [biased[8 * g:8 * (g + 1), :] for g in range(N_GROUPS)]
    gs = []
    for sl in slabs:
        m1, i1 = _first_max([sl], [sub], 8.0)
        m2 = jnp.max(jnp.where(sub == i1, neg_inf, sl), axis=0, keepdims=True)
        gs.append(m1 + m2)
    cur = jnp.concatenate(gs, axis=0)
    gsel = jnp.zeros((8, n), F32)
    for _ in range(TOPK_GROUPS):
        _, i = _first_max([cur], [sub], 8.0)
        hit = sub == i
        gsel = jnp.where(hit, 1.0, gsel)
        cur = jnp.where(hit, neg_inf, cur)
    vals = [jnp.where(gsel[g:g + 1, :] > 0.0, slabs[g], neg_inf) for g in range(N_GROUPS)]
    ids = [sub + 8.0 * g for g in range(N_GROUPS)]
    picked = [jnp.zeros((8, n), F32) for _ in range(N_GROUPS)]
    order = []
    for _ in range(TOP_K):
        _, i = _first_max(vals, ids, float(N_EXPERTS))
        order.append(i)
        hits = [idg == i for idg in ids]
        picked = [jnp.where(hh, score[8 * g:8 * (g + 1), :], p) for g, (p, hh) in enumerate(zip(picked, hits))]
        vals = [jnp.where(hh, neg_inf, v) for v, hh in zip(vals, hits)]
    total = functools.reduce(jnp.add, [jnp.sum(p, axis=0, keepdims=True) for p in picked])
    gate_t = jnp.concatenate([p / total * ROUTED_SCALE for p in picked]
                             + [jnp.zeros((128 - N_EXPERTS, n), F32)], axis=0)
    gate_ref[...] = gate_t.T
    idx_ref[...] = jnp.concatenate(order, axis=0).astype(jnp.int32)


def _router(logits_t, bias_col):
    return pl.pallas_call(
        _router_kernel,
        grid=(T_ALL // TB_MOE,),
        in_specs=[pl.BlockSpec((N_EXPERTS, TB_MOE), lambda i: (0, i)),
                  pl.BlockSpec((N_EXPERTS, 1), lambda i: (0, 0))],
        out_specs=[pl.BlockSpec((TB_MOE, 128), lambda i: (i, 0)),
                   pl.BlockSpec((TOP_K, TB_MOE), lambda i: (0, i))],
        out_shape=[jax.ShapeDtypeStruct((T_ALL, 128), F32),
                   jax.ShapeDtypeStruct((TOP_K, T_ALL), jnp.int32)],
        compiler_params=_cparams(("arbitrary",)),
        name="router",
    )(logits_t, bias_col)


def _plan_kernel(idx_ref, slot_ref, cnt_ref, pos_sc):
    n_tiles = T_HALF // 128
    eid = lax.broadcasted_iota(jnp.int32, (N_EXPERTS, 128), 0)
    tri = (lax.broadcasted_iota(jnp.int32, (128, 128), 0)
           <= lax.broadcasted_iota(jnp.int32, (128, 128), 1)).astype(BF16)
    carry = jnp.zeros((N_EXPERTS, 1), F32)
    for j in range(n_tiles):
        it = idx_ref[:, 128 * j:128 * (j + 1)]
        sel = jnp.zeros((N_EXPERTS, 128), F32)
        for k in range(TOP_K):
            sel = jnp.where(it[k:k + 1, :] == eid, 1.0, sel)
        inc = _dot(sel.astype(BF16), tri) + carry
        carry = inc[:, 127:128]
        pos_sc[:, 128 * j:128 * (j + 1)] = inc - 1.0
    count = jnp.broadcast_to(carry, (N_EXPERTS, 128))
    padded = jnp.floor((count + (MT - 1.0)) * (1.0 / MT)) * MT
    before = (lax.broadcasted_iota(jnp.int32, (N_EXPERTS, N_EXPERTS), 1)
              < lax.broadcasted_iota(jnp.int32, (N_EXPERTS, N_EXPERTS), 0)).astype(BF16)
    hi, mid, lo = _split3(padded)
    off = _dot(before, hi) + _dot(before, mid) + _dot(before, lo)
    cnt_ref[...] = count.astype(jnp.int32)

    def slots(j):
        it = idx_ref[:, 128 * j:128 * (j + 1)]
        val = off + pos_sc[:, 128 * j:128 * (j + 1)]
        rows = [jnp.sum(jnp.where(it[k:k + 1, :] == eid, val, 0.0), axis=0, keepdims=True)
                for k in range(TOP_K)]
        return jnp.concatenate(rows, axis=0).astype(jnp.int32)

    for j in range(n_tiles):
        s = slots(j)
        slot_ref[:, 128 * j:128 * (j + 1)] = s[0:TOP_K // 2, :] | (s[TOP_K // 2:, :] << 16)


def _plan(idx8):
    return pl.pallas_call(
        _plan_kernel,
        grid=(2,),
        in_specs=[pl.BlockSpec((TOP_K, T_HALF), lambda h: (0, h))],
        out_specs=[pl.BlockSpec((None, TOP_K // 2, T_HALF), lambda h: (h, 0, 0)),
                   pl.BlockSpec((None, N_EXPERTS, 128), lambda h: (h, 0, 0))],
        out_shape=[jax.ShapeDtypeStruct((2, TOP_K // 2, T_HALF), jnp.int32),
                   jax.ShapeDtypeStruct((2, N_EXPERTS, 128), jnp.int32)],
        scratch_shapes=[pltpu.VMEM((N_EXPERTS, T_HALF), F32)],
        compiler_params=_cparams(("arbitrary",)),
        name="plan",
    )(idx8)


def _prepare_half(cnt_ref, tile_e, tile_first, elist, h):
    def per_expert(x, carry):
        j, q = carry
        n = cnt_ref[h * N_EXPERTS + x]
        tiles = (n + MT - 1) // MT
        elist[q] = x

        def mark(i, c):
            tile_e[j + i] = x
            tile_first[j + i] = jnp.where(i == 0, 1, 0)
            return c

        lax.fori_loop(0, tiles, mark, 0)

        return j + tiles, q + jnp.where(tiles > 0, 1, 0)

    n_tiles, n_live = lax.fori_loop(0, N_EXPERTS, per_expert, (0, 0))
    for extra in range(2):
        tile_e[n_tiles + extra] = 0
        tile_first[n_tiles + extra] = 0
    return n_tiles, n_live


def _invert_slots(slot_ref, tok_ref, acc_v, h):
    rounds = TOP_K // 2
    trips = T_HALF // INV_UNROLL
    zero_rows = 8 * T_HALF // (rounds * trips)
    assert zero_rows * rounds * trips == 8 * T_HALF and zero_rows % 8 == 0
    for k in range(rounds):
        def body(i, carry, k=k):
            words = [slot_ref[(h * rounds + k) * T_HALF + i * INV_UNROLL + u] for u in range(INV_UNROLL)]
            for u in range(INV_UNROLL):
                t = i * INV_UNROLL + u
                tok_ref[words[u] & 0xFFFF] = t
                tok_ref[lax.shift_right_logical(words[u], 16)] = t
            row = pl.multiple_of((k * trips + i) * zero_rows, 8)
            acc_v[pl.ds(row, zero_rows), :] = jnp.zeros((zero_rows, 128), F32)
            return carry

        lax.fori_loop(0, trips, body, 0)
    acc_v[pl.ds(8 * T_HALF, 8), :] = jnp.zeros((8, 128), F32)


def _moe_kernel(slot_ref, cnt_ref, src_hbm, gate_hbm, wg_hbm, wu_hbm, wd_hbm, base_hbm, mod_ref,
                outc_hbm, outl_hbm,
                src_v, gate_v, acc_v, wg_l, wu_l, wd_l, wgu_b, wd_b, xbuf0, xbuf1, gbuf0, gbuf1,
                ybuf0, ybuf1, tok_ref, tile_e, tile_first, elist, live, base_buf, out_buf, pad_v,
                sem, wsem, fsem):
    h = pl.program_id(0)
    xbuf, gbuf, ybuf = (xbuf0, xbuf1), (gbuf0, gbuf1), (ybuf0, ybuf1)

    src_cp = pltpu.make_async_copy(src_hbm.at[pl.ds(pl.multiple_of(h * (4 * T_HALF), 8), 4 * T_HALF)],
                                   src_v.at[pl.ds(0, 4 * T_HALF)], sem.at[0])
    gate_cp = pltpu.make_async_copy(gate_hbm.at[pl.ds(pl.multiple_of(h * T_HALF, 8), T_HALF)],
                                    gate_v.at[pl.ds(0, T_HALF)], sem.at[1])
    src_cp.start()
    gate_cp.start()
    pad_v[...] = jnp.full(pad_v.shape, T_HALF, jnp.int32)
    pad_cp = pltpu.make_async_copy(pad_v, tok_ref, sem.at[2])
    pad_cp.start()

    def weight_copies(x, slot):
        return [pltpu.make_async_copy(w_hbm.at[x], w_l.at[slot], wsem.at[slot, i])
                for i, (w_hbm, w_l) in enumerate(((wg_hbm, wg_l), (wu_hbm, wu_l), (wd_hbm, wd_l)))]

    n_tiles, n_live = _prepare_half(cnt_ref, tile_e, tile_first, elist, h)
    live[0] = 0
    live[1] = n_live

    for ahead in range(W_SLOTS - 1):
        @pl.when(ahead < n_live)
        def _(ahead=ahead):
            for cp in weight_copies(elist[ahead], ahead):
                cp.start()

    zero = jnp.zeros((8, 128), F32)
    src_v[pl.ds(4 * T_HALF, 8), :] = pltpu.pack_elementwise([zero, zero], packed_dtype=BF16)
    gate_v[pl.ds(T_HALF, 8), :] = zero
    ybuf0[...] = jnp.zeros(ybuf0.shape, F32)
    ybuf1[...] = jnp.zeros(ybuf1.shape, F32)
    pad_cp.wait()
    _invert_slots(slot_ref, tok_ref, acc_v, h)
    src_cp.wait()
    gate_cp.wait()

    def switch_expert():
        q = live[0]
        slot = q % W_SLOTS
        for cp in weight_copies(elist[q], slot):
            cp.wait()
        wgu_b[:, 0:D_EXPERT] = wg_l[slot].astype(BF16)
        wgu_b[:, D_EXPERT:] = wu_l[slot].astype(BF16)
        wd_b[...] = wd_l[slot].astype(BF16)
        live[0] = q + 1
        nxt = q + W_SLOTS - 1

        @pl.when(nxt < live[1])
        def _():
            for cp in weight_copies(elist[nxt], nxt % W_SLOTS):
                cp.start()

    def gather(j, xb, gb, rows=(0, MT)):
        base = j * MT
        for m in range(*rows):
            t = tok_ref[base + m]
            xb[pl.ds(m, 4, stride=MT_STRIDE), :] = src_v[pl.ds(pl.multiple_of(t * 4, 4), 4), :]
            gb[m:m + 1, :] = gate_v[pl.ds(t, 1), :]

    def scatter(j, yb, rows=(0, MT)):
        base = j * MT
        for b in range(rows[0] // RMW_BATCH, rows[1] // RMW_BATCH):
            ms = [b * RMW_BATCH + u for u in range(RMW_BATCH)]
            targets = [pl.ds(pl.multiple_of(tok_ref[base + m] * 8, 8), 8) for m in ms]
            vals = [acc_v[r, :] + yb[pl.ds(m, 8, stride=MT_STRIDE), :] for r, m in zip(targets, ms)]
            for r, v in zip(targets, vals):
                acc_v[r, :] = v

    quarters = [(q * MT // 4, (q + 1) * MT // 4) for q in range(4)]

    def step(j, p):
        pl.when(tile_first[j] == 1)(switch_expert)
        xb, gb, yb = xbuf[p], gbuf[p], ybuf[p]
        nxt = (j + 1, xbuf[1 - p], gbuf[1 - p])
        prv = (jnp.maximum(j - 1, 0), ybuf[1 - p])
        lo, hi = [], []
        for c in range(4):
            words = xb[MT_STRIDE * c:MT_STRIDE * c + MT, :]
            unpack = functools.partial(pltpu.unpack_elementwise, words, packed_dtype=BF16, unpacked_dtype=F32)
            lo.append(unpack(index=0).astype(BF16))
            hi.append(unpack(index=1).astype(BF16))
        x = jnp.concatenate(lo + hi, axis=1)
        gather(*nxt, rows=quarters[0])
        h_gate = _dot(x, wgu_b[:, 0:D_EXPERT])
        gather(*nxt, rows=quarters[1])
        h_up = _dot(x, wgu_b[:, D_EXPERT:])
        gather(*nxt, rows=quarters[2])
        g = gb[...]
        g_hi = g.astype(BF16)
        g_lo = (g - g_hi.astype(F32)).astype(BF16)
        pick = (lax.broadcasted_iota(jnp.int32, (128, D_EXPERT), 0) == tile_e[j]).astype(BF16)
        g_col = _dot(g_hi, pick) + _dot(g_lo, pick)
        a = (_silu(h_gate) * h_up * g_col).astype(BF16)
        gather(*nxt, rows=quarters[3])
        scatter(*prv, rows=quarters[0])
        y_lo = _dot(a, wd_b[:, 0:D_MODEL // 2])
        scatter(*prv, rows=quarters[1])
        y_hi = _dot(a, wd_b[:, D_MODEL // 2:])
        scatter(*prv, rows=quarters[2])
        for c in range(4):
            yb[MT_STRIDE * c:MT_STRIDE * c + MT, :] = y_lo[:, 128 * c:128 * (c + 1)]
            yb[MT_STRIDE * (c + 4):MT_STRIDE * (c + 4) + MT, :] = y_hi[:, 128 * c:128 * (c + 1)]
        scatter(*prv, rows=quarters[3])

    gather(0, xbuf[0], gbuf[0])
    n_pairs = (n_tiles + 1) // 2

    def pair(i, carry):
        step(2 * i, 0)
        step(2 * i + 1, 1)
        return carry

    lax.fori_loop(0, n_pairs, pair, 0)
    scatter(jnp.maximum(2 * n_pairs - 1, 0), ybuf[1])

    n_fin = T_HALF // FIN

    def first_token(c):
        return h * T_HALF + c * FIN

    def base_copy(c, slot):
        return pltpu.make_async_copy(base_hbm.at[pl.ds(pl.multiple_of(first_token(c), FIN), FIN)],
                                     base_buf.at[slot], fsem.at[0, slot])

    def on_out_copy(c, action):
        t0 = first_token(c)
        slot = c % FIN_OUT_SLOTS

        @pl.when(t0 < T_CTX)
        def _():
            action(pltpu.make_async_copy(out_buf.at[slot], outc_hbm.at[pl.ds(pl.multiple_of(t0, FIN), FIN)],
                                         fsem.at[1, slot]))

        @pl.when(t0 >= T_CTX)
        def _():
            action(pltpu.make_async_copy(out_buf.at[slot],
                                         outl_hbm.at[pl.ds(pl.multiple_of(t0 - T_CTX, FIN), FIN)],
                                         fsem.at[1, slot]))

    for c in range(FIN_BASE_SLOTS - 1):
        base_copy(c, c).start()

    def combine(c, carry):
        slot = c % FIN_BASE_SLOTS
        oslot = c % FIN_OUT_SLOTS
        base_copy(c, slot).wait()
        ahead = c + FIN_BASE_SLOTS - 1

        @pl.when(ahead < n_fin)
        def _():
            base_copy(ahead, ahead % FIN_BASE_SLOTS).start()

        @pl.when(c >= FIN_OUT_SLOTS)
        def _():
            on_out_copy(c - FIN_OUT_SLOTS, lambda cp: cp.wait())

        t0 = first_token(c)
        mod_row = jnp.where(t0 < T_CTX, 0, 1 + (t0 - T_CTX) // DEC_SEQ)
        acc_row = pl.multiple_of(8 * FIN * c, 8)
        for cc in range(8):
            cs = slice(128 * cc, 128 * (cc + 1))
            routed = acc_v[pl.ds(acc_row + cc, FIN, stride=8), :]
            out_buf[oslot, :, cs] = base_buf[slot, :, cs] + mod_ref[mod_row, 5:6, cs] * routed
        on_out_copy(c, lambda cp: cp.start())
        return carry

    lax.fori_loop(0, n_fin, combine, 0)
    for c in range(n_fin - FIN_OUT_SLOTS, n_fin):
        on_out_copy(c, lambda cp: cp.wait())


def _moe(slot_words, cnt, src, gate, wg, wu, wd, base, mod3):
    any_spec = pl.BlockSpec(memory_space=pl.ANY)
    tile_buf = lambda rows, dt: pltpu.VMEM((rows * MT_STRIDE, 128), dt)
    return pl.pallas_call(
        _moe_kernel,
        grid_spec=pltpu.PrefetchScalarGridSpec(
            num_scalar_prefetch=2,
            grid=(2,),
            in_specs=[any_spec] * 6 + [pl.BlockSpec(mod3.shape, lambda h, *_: (0, 0, 0))],
            out_specs=[any_spec, any_spec],
            scratch_shapes=[pltpu.VMEM((4 * T_HALF + 8, 128), jnp.uint32),
                            pltpu.VMEM((T_HALF + 8, 128), F32),
                            pltpu.VMEM((8 * T_HALF + 8, 128), F32),
                            pltpu.VMEM((W_SLOTS, D_MODEL, D_EXPERT), F32),
                            pltpu.VMEM((W_SLOTS, D_MODEL, D_EXPERT), F32),
                            pltpu.VMEM((W_SLOTS, D_EXPERT, D_MODEL), F32),
                            pltpu.VMEM((D_MODEL, 2 * D_EXPERT), BF16),
                            pltpu.VMEM((D_EXPERT, D_MODEL), BF16),
                            tile_buf(4, jnp.uint32), tile_buf(4, jnp.uint32),
                            pltpu.VMEM((MT, 128), F32), pltpu.VMEM((MT, 128), F32),
                            tile_buf(8, F32), tile_buf(8, F32),
                            pltpu.SMEM((SLOT_CAP,), jnp.int32),
                            pltpu.SMEM((NT_MAX + 2,), jnp.int32),
                            pltpu.SMEM((NT_MAX + 2,), jnp.int32),
                            pltpu.SMEM((N_EXPERTS,), jnp.int32),
                            pltpu.SMEM((2,), jnp.int32),
                            pltpu.VMEM((FIN_BASE_SLOTS, FIN, D_MODEL), F32),
                            pltpu.VMEM((FIN_OUT_SLOTS, FIN, D_MODEL), F32),
                            pltpu.VMEM((SLOT_CAP,), jnp.int32),
                            pltpu.SemaphoreType.DMA((3,)),
                            pltpu.SemaphoreType.DMA((W_SLOTS, 3)),
                            pltpu.SemaphoreType.DMA((2, FIN_BASE_SLOTS))]),
        out_shape=[jax.ShapeDtypeStruct((T_CTX, D_MODEL), F32), jax.ShapeDtypeStruct((T_LAT, D_MODEL), F32)],
        compiler_params=_cparams(("arbitrary",)),
        name="moe",
    )(slot_words, cnt, src, gate, wg, wu, wd, base, mod3)


def _head_indicators(width):
    head = jnp.arange(width) // DH_ATT
    ind = (head[:, None] == jnp.arange(128)[None, :]).astype(BF16)
    return ind, ind.T


def kernel(x_prompt, x_sample, cache_attn_k, cache_attn_v, state_mlstm_c, state_mlstm_n, state_mlstm_m, c, c_ctx, w_mod, b_mod, norm1_w, norm2_w, w_in, q_norm_w, k_norm_w, b_gates, m_norm_w, w_out, router_w, router_bias, w_gate, w_up, w_down, ws_gate, ws_up, ws_down):
    x = (x_prompt.reshape(T_CTX, D_MODEL), x_sample.reshape(T_LAT, D_MODEL))
    mod3 = _modulation(c, c_ctx, w_mod[0], b_mod[0])

    w_main = w_in[0].T
    w_gates_t = w_in[0, :, P_MAIN:].T.astype(BF16)
    qw = jnp.tile(q_norm_w, (1, N_HEADS_ATT))
    kw = jnp.tile(k_norm_w, (1, N_KV_HEADS))
    inds = _head_indicators(512) + _head_indicators(128)
    qn, kn, va, qm, km, vm, om, gt, new_k, new_v = _inproj(*x, mod3, norm1_w, w_main, w_gates_t, qw, kw, inds,
                                                           _rope_tables())

    att_c = _attention(qn, kn, va, None, n_batch=BATCH, seq=SEQ, first_row=0)
    att_l = _attention(qn, kn, va, (cache_attn_k, cache_attn_v), n_batch=DEC_BATCH, seq=DEC_SEQ,
                       first_row=T_CTX)

    g3 = gt.reshape(16, T_ALL // CHUNK, CHUNK).transpose(1, 0, 2)
    bg = b_gates.reshape(16, 1)
    hm_c, c_new, n_new, m_new = _mlstm(qm, km, vm, g3, bg, om, m_norm_w, None,
                                       n_batch=BATCH, seq=SEQ, first_block=0)
    m0 = jnp.broadcast_to(state_mlstm_m.reshape(DEC_BATCH, 8, 1), (DEC_BATCH, 8, CHUNK))
    hm_l = _mlstm(qm, km, vm, g3, bg, om, m_norm_w, (state_mlstm_c, state_mlstm_n, m0),
                  n_batch=DEC_BATCH, seq=DEC_SEQ, first_block=N_CTX_BLOCKS)

    rw_t = router_w[0].T
    rw_hi = rw_t.astype(BF16)
    rw_lo = (rw_t - rw_hi.astype(F32)).astype(BF16)
    base, hn2, logits_t = _outproj((att_c, att_l), (hm_c, hm_l), x, mod3, norm2_w, w_out[0].astype(BF16),
                                   rw_hi, rw_lo, ws_gate[0].astype(BF16), ws_up[0].astype(BF16),
                                   ws_down[0].astype(BF16))
    gate, idx8 = _router(logits_t, router_bias.reshape(N_EXPERTS, 1))
    slot_words, cnt = _plan(idx8)
    out_c, out_l = _moe(slot_words.reshape(TOP_K * T_HALF), cnt[:, :, 0].reshape(2 * N_EXPERTS),
                        hn2, gate, w_gate[0], w_up[0], w_down[0], base, mod3)

    y_prompt = out_c.reshape(BATCH, SEQ, D_MODEL)
    y_sample = out_l.reshape(DEC_BATCH, DEC_SEQ, D_MODEL)
    new_m = m_new[:, :, 0].reshape(BATCH, 1, 2, N_HEADS_M)
    to_cache = lambda a: a.transpose(0, 1, 3, 2)[:, None]
    return (y_prompt, y_sample, to_cache(new_k), to_cache(new_v), c_new[:, None], n_new[:, None], new_m)
```
